```python
import math
import jax
import jax.numpy as jnp
from jax import lax
import numpy as np

D_MODEL = 2048
BATCH = 8
SEQ = 4096
DEPTH = 4

N_MIXERS = 3
D_INNER = D_MODEL
CHUNK = 128
SG_GROUPS = 16
SG_GROUP_DIM = D_INNER // SG_GROUPS
SG_COLS = 3 * D_INNER
HEAD_DIM = 64
SWA_HEADS = D_INNER // HEAD_DIM
SWA_KV_HEADS = SWA_HEADS // 8
SWA_REP = SWA_HEADS // SWA_KV_HEADS
WINDOW = 128
BLOCK = 128
ROPE_THETA = 10000.0
SWA_COLS = 2 * D_INNER + 2 * SWA_KV_HEADS * HEAD_DIM
RWKV_HEAD_DIM = 64
RWKV_HEADS = D_INNER // RWKV_HEAD_DIM
DECAY_LORA = 96
AAA_LORA = 96
RWKV_COLS = 4 * D_INNER + DECAY_LORA + AAA_LORA
DECAY_SCALE = math.exp(-0.5)
GN_EPS = 64e-5
RMS_EPS = 1e-6
LN_EPS = 1e-5
N_A = (DEPTH + 2) // 3
N_B = (DEPTH + 1) // 3
N_C = DEPTH // 3

kernel_name = 'hybrid_sgmlp_swa_rwkv7_adaln'


def rms_norm(x, g):
    xf = x.astype(jnp.float32)
    y = xf * lax.rsqrt(jnp.mean(xf * xf, axis=-1, keepdims=True) + RMS_EPS)
    return (y * g.astype(jnp.float32)).astype(x.dtype)


def token_shift(t):
    return jnp.concatenate([jnp.zeros_like(t[:, :1]), t[:, :-1]], axis=1)


def rope(x, positions):
    half = HEAD_DIM // 2
    inv_freq = ROPE_THETA ** (-jnp.arange(half, dtype=jnp.float32) / half)
    ang = positions.astype(jnp.float32)[..., None] * inv_freq
    cos = jnp.cos(ang)[:, :, None, :]
    sin = jnp.sin(ang)[:, :, None, :]
    xf = x.astype(jnp.float32)
    x1, x2 = xf[..., :half], xf[..., half:]
    return jnp.concatenate([x1 * cos - x2 * sin, x2 * cos + x1 * sin], axis=-1).astype(x.dtype)


def chunked_spatial_gating(p, ln_g, ln_b, w_s, b_s):
    B, T, _ = p.shape
    u, v, z = jnp.split(p, 3, axis=-1)
    u = jax.nn.gelu(u)
    vf = jax.nn.gelu(v).astype(jnp.float32)
    mean = jnp.mean(vf, axis=-1, keepdims=True)
    var = jnp.mean(jnp.square(vf - mean), axis=-1, keepdims=True)
    v = ((vf - mean) * lax.rsqrt(var + LN_EPS) * ln_g.astype(jnp.float32) + ln_b.astype(jnp.float32)).astype(p.dtype)
    nc = T // CHUNK
    v = v.reshape(B, nc, CHUNK, SG_GROUPS, SG_GROUP_DIM)
    causal = jnp.tril(jnp.ones((CHUNK, CHUNK), dtype=bool))
    w = jnp.where(causal[None], w_s, jnp.zeros_like(w_s))
    f = jnp.einsum('gts,bnsgc->bntgc', w, v) + b_s.T[:, :, None]
    f = f.reshape(B, T, D_INNER)
    return u * f * jax.nn.silu(z)


def sliding_window_attention(p, positions, sinks):
    B, T, _ = p.shape
    kvw = SWA_KV_HEADS * HEAD_DIM
    q, k, v, z = jnp.split(p, [D_INNER, D_INNER + kvw, D_INNER + 2 * kvw], axis=-1)
    q = rope(q.reshape(B, T, SWA_HEADS, HEAD_DIM), positions)
    k = rope(k.reshape(B, T, SWA_KV_HEADS, HEAD_DIM), positions)
    v = v.reshape(B, T, SWA_KV_HEADS, HEAD_DIM)
    nb = T // BLOCK
    qb = q.reshape(B, nb, BLOCK, SWA_KV_HEADS, SWA_REP, HEAD_DIM)

    def with_prev(t):
        tb = t.reshape(B, nb, BLOCK, SWA_KV_HEADS, HEAD_DIM)
        prev = jnp.concatenate([jnp.zeros_like(tb[:, :1]), tb[:, :-1]], axis=1)
        return jnp.concatenate([prev, tb], axis=2)

    kb, vb = with_prev(k), with_prev(v)
    s = jnp.einsum('bnqgrd,bnkgd->bngrqk', qb, kb,
                   preferred_element_type=jnp.float32) * (HEAD_DIM ** -0.5)
    qi = jnp.arange(BLOCK)[:, None]
    kj = jnp.arange(2 * BLOCK)[None, :]
    rel = qi + BLOCK - kj
    band = (rel >= 0) & (rel < WINDOW)
    key_pos = jnp.arange(nb)[:, None] * BLOCK + jnp.arange(2 * BLOCK)[None, :] - BLOCK
    mask = band[None] & (key_pos >= 0)[:, None, :]
    s = jnp.where(mask[None, :, None, None], s, -jnp.inf)
    sink = sinks.astype(jnp.float32).reshape(SWA_KV_HEADS, SWA_REP)[None, None, :, :, None, None]
    m = jnp.maximum(jnp.max(s, axis=-1, keepdims=True), sink)
    e = jnp.exp(s - m)
    denom = jnp.sum(e, axis=-1, keepdims=True) + jnp.exp(sink - m)
    prob = (e / denom).astype(p.dtype)
    o = jnp.einsum('bngrqk,bnkgd->bnqgrd', prob, vb).reshape(B, T, D_INNER)
    return o * jax.nn.silu(z)


def rwkv7_time_mix(p, mu, w0, w_lora, a0, a_lora, k_k, k_a, r_k, gn_g, gn_b):
    B, T, _ = p.shape
    H, N = RWKV_HEADS, RWKV_HEAD_DIM
    p = p + (token_shift(p) - p) * mu
    r, k, v, z, dw, da = jnp.split(p, [D_INNER, 2 * D_INNER, 3 * D_INNER, 4 * D_INNER,
                                       4 * D_INNER + DECAY_LORA], axis=-1)
    decay = jnp.exp(-DECAY_SCALE * jax.nn.sigmoid((w0 + jnp.tanh(dw) @ w_lora).astype(jnp.float32)))
    a = jax.nn.sigmoid((a0 + da @ a_lora).astype(jnp.float32))

    def heads(t):
        return t.astype(jnp.float32).reshape(B, T, H, N)

    r, k, v, decay, a = heads(r), heads(k), heads(v), heads(decay), heads(a)
    kk = k * k_k.astype(jnp.float32).reshape(H, N)
    kk = kk / jnp.maximum(jnp.sqrt(jnp.sum(kk * kk, axis=-1, keepdims=True)), 1e-12)
    k = k * (1.0 + (a - 1.0) * k_a.astype(jnp.float32).reshape(H, N))
    b_vec = kk * a

    def step(S, inp):
        r_t, w_t, k_t, v_t, kk_t, b_t = inp
        sa = jnp.einsum('bhvk,bhk->bhv', S, kk_t)
        S = S * w_t[:, :, None, :] - sa[..., None] * b_t[:, :, None, :] + v_t[..., None] * k_t[:, :, None, :]
        y = jnp.einsum('bhvk,bhk->bhv', S, r_t)
        return S, y

    xs = tuple(jnp.moveaxis(t, 1, 0) for t in (r, decay, k, v, kk, b_vec))
    S0 = jnp.zeros((B, H, N, N), jnp.float32)
    _, y = lax.scan(step, S0, xs)
    y = jnp.moveaxis(y, 0, 1)
    mean = jnp.mean(y, axis=-1, keepdims=True)
    var = jnp.mean(jnp.square(y - mean), axis=-1, keepdims=True)
    y = (y - mean) * lax.rsqrt(var + GN_EPS) * gn_g.astype(jnp.float32).reshape(H, N) \
        + gn_b.astype(jnp.float32).reshape(H, N)
    y = y + jnp.sum(r * k * r_k.astype(jnp.float32), axis=-1, keepdims=True) * v
    y = y.reshape(B, T, D_INNER).astype(p.dtype)
    return y * jax.nn.silu(z)


def _fwd_setup_inputs(seed: int = 0) -> dict:
    key = jax.random.key(seed)
    keys = jax.random.split(key, 32)

    def nrm(i, shape, scale):
        return jax.random.normal(keys[i], shape, jnp.float32) * scale

    positions = jax.random.randint(keys[2], (BATCH, 1), 0, 1024, dtype=jnp.int32) \
        + jnp.arange(SEQ, dtype=jnp.int32)[None, :]
    return {
        'x': nrm(0, (BATCH, SEQ, D_MODEL), 1.0),
        'c': nrm(1, (BATCH, D_MODEL), 1.0),
        'positions': positions,
        'norm_g': 1.0 + nrm(3, (DEPTH, D_MODEL), 0.1),
        'mod_w': nrm(4, (DEPTH, D_MODEL, 3 * D_MODEL), 0.5 * D_MODEL ** -0.5),
        'mod_b': nrm(5, (DEPTH, 3 * D_MODEL), 0.02),
        'final_norm_g': 1.0 + nrm(6, (D_MODEL,), 0.1),
        'sg_w_in': nrm(7, (N_A, D_MODEL, SG_COLS), D_MODEL ** -0.5),
        'sg_w_out': nrm(8, (N_A, D_INNER, D_MODEL), D_INNER ** -0.5),
        'sg_ln_g': 1.0 + nrm(9, (N_A, D_INNER), 0.1),
        'sg_ln_b': nrm(10, (N_A, D_INNER), 0.02),
        'sg_w_spatial': nrm(11, (N_A, SG_GROUPS, CHUNK, CHUNK), CHUNK ** -0.5),
        'sg_b_spatial': 1.0 + nrm(12, (N_A, SG_GROUPS, CHUNK), 0.1),
        'swa_w_in': nrm(13, (N_B, D_MODEL, SWA_COLS), D_MODEL ** -0.5),
        'swa_w_out': nrm(14, (N_B, D_INNER, D_MODEL), D_INNER ** -0.5),
        'swa_sinks': nrm(15, (N_B, SWA_HEADS), 1.0),
        'rwkv_w_in': nrm(16, (N_C, D_MODEL, RWKV_COLS), D_MODEL ** -0.5),
        'rwkv_w_out': nrm(17, (N_C, D_INNER, D_MODEL), D_INNER ** -0.5),
        'rwkv_mu': jax.random.uniform(keys[18], (N_C, RWKV_COLS), jnp.float32),
        'rwkv_w0': jax.random.uniform(keys[19], (N_C, D_INNER), jnp.float32, -4.0, 1.0),
        'rwkv_w_lora': nrm(20, (N_C, DECAY_LORA, D_INNER), DECAY_LORA ** -0.5),
        'rwkv_a0': nrm(21, (N_C, D_INNER), 0.5),
        'rwkv_a_lora': nrm(22, (N_C, AAA_LORA, D_INNER), 0.5 * AAA_LORA ** -0.5),
        'rwkv_k_k': 0.85 + nrm(23, (N_C, D_INNER), 0.1),
        'rwkv_k_a': 1.0 + nrm(24, (N_C, D_INNER), 0.1),
        'rwkv_r_k': nrm(25, (N_C, RWKV_HEADS, RWKV_HEAD_DIM), 0.1),
        'rwkv_gn_g': 1.0 + nrm(26, (N_C, D_INNER), 0.1),
        'rwkv_gn_b': nrm(27, (N_C, D_INNER), 0.02),
    }


def _fwd_reference(x, c, positions, norm_g, mod_w, mod_b, final_norm_g,
              sg_w_in, sg_w_out, sg_ln_g, sg_ln_b, sg_w_spatial, sg_b_spatial,
              swa_w_in, swa_w_out, swa_sinks,
              rwkv_w_in, rwkv_w_out, rwkv_mu, rwkv_w0, rwkv_w_lora, rwkv_a0, rwkv_a_lora,
              rwkv_k_k, rwkv_k_a, rwkv_r_k, rwkv_gn_g, rwkv_gn_b):
    cond = jax.nn.silu(c)
    for i in range(DEPTH):
        kind, j = i % N_MIXERS, i // N_MIXERS
        mod = (cond @ mod_w[i] + mod_b[i])[:, None, :]
        shift, scale, gate = jnp.split(mod, 3, axis=-1)
        h = rms_norm(x, norm_g[i]) * (1.0 + scale) + shift
        if kind == 0:
            y = chunked_spatial_gating(h @ sg_w_in[j], sg_ln_g[j], sg_ln_b[j],
                                       sg_w_spatial[j], sg_b_spatial[j]) @ sg_w_out[j]
        elif kind == 1:
            y = sliding_window_attention(h @ swa_w_in[j], positions, swa_sinks[j]) @ swa_w_out[j]
        else:
            y = rwkv7_time_mix(h @ rwkv_w_in[j], rwkv_mu[j], rwkv_w0[j], rwkv_w_lora[j],
                               rwkv_a0[j], rwkv_a_lora[j], rwkv_k_k[j], rwkv_k_a[j],
                               rwkv_r_k[j], rwkv_gn_g[j], rwkv_gn_b[j]) @ rwkv_w_out[j]
        x = x + gate * y
    return rms_norm(x, final_norm_g)


import jax as _jax
import jax.numpy as _jnp

TWIN_FORMAT = 'train_step'
FWD_PARAMS = ['x', 'c', 'positions', 'norm_g', 'mod_w', 'mod_b', 'final_norm_g', 'sg_w_in', 'sg_w_out', 'sg_ln_g', 'sg_ln_b', 'sg_w_spatial', 'sg_b_spatial', 'swa_w_in', 'swa_w_out', 'swa_sinks', 'rwkv_w_in', 'rwkv_w_out', 'rwkv_mu', 'rwkv_w0', 'rwkv_w_lora', 'rwkv_a0', 'rwkv_a_lora', 'rwkv_k_k', 'rwkv_k_a', 'rwkv_r_k', 'rwkv_gn_g', 'rwkv_gn_b']
TWIN_WEIGHTS = ['norm_g', 'mod_w', 'mod_b', 'final_norm_g', 'sg_w_in', 'sg_w_out', 'sg_ln_g', 'sg_ln_b', 'sg_w_spatial', 'sg_b_spatial', 'swa_w_in', 'swa_w_out', 'swa_sinks', 'rwkv_w_in', 'rwkv_w_out', 'rwkv_mu', 'rwkv_w0', 'rwkv_w_lora', 'rwkv_a0', 'rwkv_a_lora', 'rwkv_k_k', 'rwkv_k_a', 'rwkv_r_k', 'rwkv_gn_g', 'rwkv_gn_b']
TWIN_DIFF_INPUT = 'x'
TWIN_INPUTS = ['x', 'c', 'positions', 'norm_g', 'mod_w', 'mod_b', 'final_norm_g', 'sg_w_in', 'sg_w_out', 'sg_ln_g', 'sg_ln_b', 'sg_w_spatial', 'sg_b_spatial', 'swa_w_in', 'swa_w_out', 'swa_sinks', 'rwkv_w_in', 'rwkv_w_out', 'rwkv_mu', 'rwkv_w0', 'rwkv_w_lora', 'rwkv_a0', 'rwkv_a_lora', 'rwkv_k_k', 'rwkv_k_a', 'rwkv_r_k', 'rwkv_gn_g', 'rwkv_gn_b', 'loss_target', 'm_norm_g', 'm_mod_w', 'm_mod_b', 'm_final_norm_g', 'm_sg_w_in', 'm_sg_w_out', 'm_sg_ln_g', 'm_sg_ln_b', 'm_sg_w_spatial', 'm_sg_b_spatial', 'm_swa_w_in', 'm_swa_w_out', 'm_swa_sinks', 'm_rwkv_w_in', 'm_rwkv_w_out', 'm_rwkv_mu', 'm_rwkv_w0', 'm_rwkv_w_lora', 'm_rwkv_a0', 'm_rwkv_a_lora', 'm_rwkv_k_k', 'm_rwkv_k_a', 'm_rwkv_r_k', 'm_rwkv_gn_g', 'm_rwkv_gn_b', 'v_norm_g', 'v_mod_w', 'v_mod_b', 'v_final_norm_g', 'v_sg_w_in', 'v_sg_w_out', 'v_sg_ln_g', 'v_sg_ln_b', 'v_sg_w_spatial', 'v_sg_b_spatial', 'v_swa_w_in', 'v_swa_w_out', 'v_swa_sinks', 'v_rwkv_w_in', 'v_rwkv_w_out', 'v_rwkv_mu', 'v_rwkv_w0', 'v_rwkv_w_lora', 'v_rwkv_a0', 'v_rwkv_a_lora', 'v_rwkv_k_k', 'v_rwkv_k_a', 'v_rwkv_r_k', 'v_rwkv_gn_g', 'v_rwkv_gn_b']
TWIN_OUTPUTS = ['loss', 'grad_x', 'grad_norm_g', 'grad_mod_w', 'grad_mod_b', 'grad_final_norm_g', 'grad_sg_w_in', 'grad_sg_w_out', 'grad_sg_ln_g', 'grad_sg_ln_b', 'grad_sg_w_spatial', 'grad_sg_b_spatial', 'grad_swa_w_in', 'grad_swa_w_out', 'grad_swa_sinks', 'grad_rwkv_w_in', 'grad_rwkv_w_out', 'grad_rwkv_mu', 'grad_rwkv_w0', 'grad_rwkv_w_lora', 'grad_rwkv_a0', 'grad_rwkv_a_lora', 'grad_rwkv_k_k', 'grad_rwkv_k_a', 'grad_rwkv_r_k', 'grad_rwkv_gn_g', 'grad_rwkv_gn_b', 'delta_norm_g', 'delta_mod_w', 'delta_mod_b', 'delta_final_norm_g', 'delta_sg_w_in', 'delta_sg_w_out', 'delta_sg_ln_g', 'delta_sg_ln_b', 'delta_sg_w_spatial', 'delta_sg_b_spatial', 'delta_swa_w_in', 'delta_swa_w_out', 'delta_swa_sinks', 'delta_rwkv_w_in', 'delta_rwkv_w_out', 'delta_rwkv_mu', 'delta_rwkv_w0', 'delta_rwkv_w_lora', 'delta_rwkv_a0', 'delta_rwkv_a_lora', 'delta_rwkv_k_k', 'delta_rwkv_k_a', 'delta_rwkv_r_k', 'delta_rwkv_gn_g', 'delta_rwkv_gn_b', 'new_m_norm_g', 'new_m_mod_w', 'new_m_mod_b', 'new_m_final_norm_g', 'new_m_sg_w_in', 'new_m_sg_w_out', 'new_m_sg_ln_g', 'new_m_sg_ln_b', 'new_m_sg_w_spatial', 'new_m_sg_b_spatial', 'new_m_swa_w_in', 'new_m_swa_w_out', 'new_m_swa_sinks', 'new_m_rwkv_w_in', 'new_m_rwkv_w_out', 'new_m_rwkv_mu', 'new_m_rwkv_w0', 'new_m_rwkv_w_lora', 'new_m_rwkv_a0', 'new_m_rwkv_a_lora', 'new_m_rwkv_k_k', 'new_m_rwkv_k_a', 'new_m_rwkv_r_k', 'new_m_rwkv_gn_g', 'new_m_rwkv_gn_b', 'new_v_norm_g', 'new_v_mod_w', 'new_v_mod_b', 'new_v_final_norm_g', 'new_v_sg_w_in', 'new_v_sg_w_out', 'new_v_sg_ln_g', 'new_v_sg_ln_b', 'new_v_sg_w_spatial', 'new_v_sg_b_spatial', 'new_v_swa_w_in', 'new_v_swa_w_out', 'new_v_swa_sinks', 'new_v_rwkv_w_in', 'new_v_rwkv_w_out', 'new_v_rwkv_mu', 'new_v_rwkv_w0', 'new_v_rwkv_w_lora', 'new_v_rwkv_a0', 'new_v_rwkv_a_lora', 'new_v_rwkv_k_k', 'new_v_rwkv_k_a', 'new_v_rwkv_r_k', 'new_v_rwkv_gn_g', 'new_v_rwkv_gn_b']
TWIN_LEAF_KINDS = {'loss': 'loss', 'grad_x': 'grad_x', 'grad_norm_g': 'grad_w', 'grad_mod_w': 'grad_w', 'grad_mod_b': 'grad_w', 'grad_final_norm_g': 'grad_w', 'grad_sg_w_in': 'grad_w', 'grad_sg_w_out': 'grad_w', 'grad_sg_ln_g': 'grad_w', 'grad_sg_ln_b': 'grad_w', 'grad_sg_w_spatial': 'grad_w', 'grad_sg_b_spatial': 'grad_w', 'grad_swa_w_in': 'grad_w', 'grad_swa_w_out': 'grad_w', 'grad_swa_sinks': 'grad_w', 'grad_rwkv_w_in': 'grad_w', 'grad_rwkv_w_out': 'grad_w', 'grad_rwkv_mu': 'grad_w', 'grad_rwkv_w0': 'grad_w', 'grad_rwkv_w_lora': 'grad_w', 'grad_rwkv_a0': 'grad_w', 'grad_rwkv_a_lora': 'grad_w', 'grad_rwkv_k_k': 'grad_w', 'grad_rwkv_k_a': 'grad_w', 'grad_rwkv_r_k': 'grad_w', 'grad_rwkv_gn_g': 'grad_w', 'grad_rwkv_gn_b': 'grad_w', 'delta_norm_g': 'delta_w', 'delta_mod_w': 'delta_w', 'delta_mod_b': 'delta_w', 'delta_final_norm_g': 'delta_w', 'delta_sg_w_in': 'delta_w', 'delta_sg_w_out': 'delta_w', 'delta_sg_ln_g': 'delta_w', 'delta_sg_ln_b': 'delta_w', 'delta_sg_w_spatial': 'delta_w', 'delta_sg_b_spatial': 'delta_w', 'delta_swa_w_in': 'delta_w', 'delta_swa_w_out': 'delta_w', 'delta_swa_sinks': 'delta_w', 'delta_rwkv_w_in': 'delta_w', 'delta_rwkv_w_out': 'delta_w', 'delta_rwkv_mu': 'delta_w', 'delta_rwkv_w0': 'delta_w', 'delta_rwkv_w_lora': 'delta_w', 'delta_rwkv_a0': 'delta_w', 'delta_rwkv_a_lora': 'delta_w', 'delta_rwkv_k_k': 'delta_w', 'delta_rwkv_k_a': 'delta_w', 'delta_rwkv_r_k': 'delta_w', 'delta_rwkv_gn_g': 'delta_w', 'delta_rwkv_gn_b': 'delta_w', 'new_m_norm_g': 'new_m', 'new_m_mod_w': 'new_m', 'new_m_mod_b': 'new_m', 'new_m_final_norm_g': 'new_m', 'new_m_sg_w_in': 'new_m', 'new_m_sg_w_out': 'new_m', 'new_m_sg_ln_g': 'new_m', 'new_m_sg_ln_b': 'new_m', 'new_m_sg_w_spatial': 'new_m', 'new_m_sg_b_spatial': 'new_m', 'new_m_swa_w_in': 'new_m', 'new_m_swa_w_out': 'new_m', 'new_m_swa_sinks': 'new_m', 'new_m_rwkv_w_in': 'new_m', 'new_m_rwkv_w_out': 'new_m', 'new_m_rwkv_mu': 'new_m', 'new_m_rwkv_w0': 'new_m', 'new_m_rwkv_w_lora': 'new_m', 'new_m_rwkv_a0': 'new_m', 'new_m_rwkv_a_lora': 'new_m', 'new_m_rwkv_k_k': 'new_m', 'new_m_rwkv_k_a': 'new_m', 'new_m_rwkv_r_k': 'new_m', 'new_m_rwkv_gn_g': 'new_m', 'new_m_rwkv_gn_b': 'new_m', 'new_v_norm_g': 'new_v', 'new_v_mod_w': 'new_v', 'new_v_mod_b': 'new_v', 'new_v_final_norm_g': 'new_v', 'new_v_sg_w_in': 'new_v', 'new_v_sg_w_out': 'new_v', 'new_v_sg_ln_g': 'new_v', 'new_v_sg_ln_b': 'new_v', 'new_v_sg_w_spatial': 'new_v', 'new_v_sg_b_spatial': 'new_v', 'new_v_swa_w_in': 'new_v', 'new_v_swa_w_out': 'new_v', 'new_v_swa_sinks': 'new_v', 'new_v_rwkv_w_in': 'new_v', 'new_v_rwkv_w_out': 'new_v', 'new_v_rwkv_mu': 'new_v', 'new_v_rwkv_w0': 'new_v', 'new_v_rwkv_w_lora': 'new_v', 'new_v_rwkv_a0': 'new_v', 'new_v_rwkv_a_lora': 'new_v', 'new_v_rwkv_k_k': 'new_v', 'new_v_rwkv_k_a': 'new_v', 'new_v_rwkv_r_k': 'new_v', 'new_v_rwkv_gn_g': 'new_v', 'new_v_rwkv_gn_b': 'new_v'}


def _forward(args):
    return _fwd_reference(*[args[k] for k in FWD_PARAMS])


def _output_shape():
    def fwd():
        inp = _fwd_setup_inputs(0)
        return _fwd_reference(*[inp[k] for k in FWD_PARAMS])
    out = _jax.eval_shape(fwd)
    return out.shape, out.dtype

N_MICROBATCH = 1
ADAM_LR = 0.001
ADAM_B1 = 0.9
ADAM_B2 = 0.999
ADAM_EPS = 1e-08
ADAM_WD = 0.01
ADAM_STEP = 10
PER_EXAMPLE_BATCH_AXIS = {'x': 0, 'c': 0, 'positions': 0, 'loss_target': 0}
SHARED_INPUTS = []
_WEIGHT_DTYPES = {'norm_g': _jnp.float32, 'mod_w': _jnp.float32, 'mod_b': _jnp.float32, 'final_norm_g': _jnp.float32, 'sg_w_in': _jnp.float32, 'sg_w_out': _jnp.float32, 'sg_ln_g': _jnp.float32, 'sg_ln_b': _jnp.float32, 'sg_w_spatial': _jnp.float32, 'sg_b_spatial': _jnp.float32, 'swa_w_in': _jnp.float32, 'swa_w_out': _jnp.float32, 'swa_sinks': _jnp.float32, 'rwkv_w_in': _jnp.float32, 'rwkv_w_out': _jnp.float32, 'rwkv_mu': _jnp.float32, 'rwkv_w0': _jnp.float32, 'rwkv_w_lora': _jnp.float32, 'rwkv_a0': _jnp.float32, 'rwkv_a_lora': _jnp.float32, 'rwkv_k_k': _jnp.float32, 'rwkv_k_a': _jnp.float32, 'rwkv_r_k': _jnp.float32, 'rwkv_gn_g': _jnp.float32, 'rwkv_gn_b': _jnp.float32}
MOMENT_SCALE = {'norm_g': 2.389470e-02, 'mod_w': 3.071151e-02, 'mod_b': 6.071265e-02, 'final_norm_g': 1.608620e+01, 'sg_w_in': 1.478703e-02, 'sg_w_out': 1.610114e-02, 'sg_ln_g': 9.164785e-03, 'sg_ln_b': 8.987109e-03, 'sg_w_spatial': 9.147289e-03, 'sg_b_spatial': 1.295388e-02, 'swa_w_in': 8.037498e-03, 'swa_w_out': 7.237264e-03, 'swa_sinks': 3.143593e-03, 'rwkv_w_in': 1.610842e-02, 'rwkv_w_out': 1.527324e-02, 'rwkv_mu': 2.519574e-02, 'rwkv_w0': 7.761886e-03, 'rwkv_w_lora': 2.013256e-03, 'rwkv_a0': 6.678535e-03, 'rwkv_a_lora': 6.170344e-03, 'rwkv_k_k': 1.189553e-02, 'rwkv_k_a': 2.132560e-02, 'rwkv_r_k': 4.968083e-02, 'rwkv_gn_g': 1.536680e-02, 'rwkv_gn_b': 1.471949e-02}


def _to_microbatches(a, axis):
    t = _jnp.moveaxis(a, axis, 0)
    t = t.reshape((N_MICROBATCH, t.shape[0] // N_MICROBATCH) + t.shape[1:])
    return _jnp.moveaxis(t, 1, axis + 1)


def setup_inputs(seed: int = 0) -> dict:
    inp = _fwd_setup_inputs(seed)
    key = _jax.random.fold_in(_jax.random.key(seed), 7919)
    shape, _ = _output_shape()
    out = dict(inp)
    out["loss_target"] = _jax.random.normal(_jax.random.fold_in(key, 0), shape, _jnp.float32)
    for i, name in enumerate(TWIN_WEIGHTS):
        w = inp[name].astype(_jnp.float32)
        if MOMENT_SCALE is None:
            s = _jnp.sqrt(_jnp.mean(_jnp.square(w)) + 1e-30)
        else:
            s = MOMENT_SCALE[name]
        km, kv = _jax.random.split(_jax.random.fold_in(key, i + 1))
        out[name] = w
        out["m_" + name] = s * _jax.random.normal(km, w.shape, _jnp.float32)
        out["v_" + name] = (s * s) * _jax.random.uniform(kv, w.shape, _jnp.float32, 0.5, 1.5)
    if N_MICROBATCH > 1:
        for name, axis in PER_EXAMPLE_BATCH_AXIS.items():
            out[name] = _to_microbatches(out[name], axis)
    return {'x': out['x'], 'c': out['c'], 'positions': out['positions'], 'norm_g': out['norm_g'], 'mod_w': out['mod_w'], 'mod_b': out['mod_b'], 'final_norm_g': out['final_norm_g'], 'sg_w_in': out['sg_w_in'], 'sg_w_out': out['sg_w_out'], 'sg_ln_g': out['sg_ln_g'], 'sg_ln_b': out['sg_ln_b'], 'sg_w_spatial': out['sg_w_spatial'], 'sg_b_spatial': out['sg_b_spatial'], 'swa_w_in': out['swa_w_in'], 'swa_w_out': out['swa_w_out'], 'swa_sinks': out['swa_sinks'], 'rwkv_w_in': out['rwkv_w_in'], 'rwkv_w_out': out['rwkv_w_out'], 'rwkv_mu': out['rwkv_mu'], 'rwkv_w0': out['rwkv_w0'], 'rwkv_w_lora': out['rwkv_w_lora'], 'rwkv_a0': out['rwkv_a0'], 'rwkv_a_lora': out['rwkv_a_lora'], 'rwkv_k_k': out['rwkv_k_k'], 'rwkv_k_a': out['rwkv_k_a'], 'rwkv_r_k': out['rwkv_r_k'], 'rwkv_gn_g': out['rwkv_gn_g'], 'rwkv_gn_b': out['rwkv_gn_b'], 'loss_target': out['loss_target'], 'm_norm_g': out['m_norm_g'], 'm_mod_w': out['m_mod_w'], 'm_mod_b': out['m_mod_b'], 'm_final_norm_g': out['m_final_norm_g'], 'm_sg_w_in': out['m_sg_w_in'], 'm_sg_w_out': out['m_sg_w_out'], 'm_sg_ln_g': out['m_sg_ln_g'], 'm_sg_ln_b': out['m_sg_ln_b'], 'm_sg_w_spatial': out['m_sg_w_spatial'], 'm_sg_b_spatial': out['m_sg_b_spatial'], 'm_swa_w_in': out['m_swa_w_in'], 'm_swa_w_out': out['m_swa_w_out'], 'm_swa_sinks': out['m_swa_sinks'], 'm_rwkv_w_in': out['m_rwkv_w_in'], 'm_rwkv_w_out': out['m_rwkv_w_out'], 'm_rwkv_mu': out['m_rwkv_mu'], 'm_rwkv_w0': out['m_rwkv_w0'], 'm_rwkv_w_lora': out['m_rwkv_w_lora'], 'm_rwkv_a0': out['m_rwkv_a0'], 'm_rwkv_a_lora': out['m_rwkv_a_lora'], 'm_rwkv_k_k': out['m_rwkv_k_k'], 'm_rwkv_k_a': out['m_rwkv_k_a'], 'm_rwkv_r_k': out['m_rwkv_r_k'], 'm_rwkv_gn_g': out['m_rwkv_gn_g'], 'm_rwkv_gn_b': out['m_rwkv_gn_b'], 'v_norm_g': out['v_norm_g'], 'v_mod_w': out['v_mod_w'], 'v_mod_b': out['v_mod_b'], 'v_final_norm_g': out['v_final_norm_g'], 'v_sg_w_in': out['v_sg_w_in'], 'v_sg_w_out': out['v_sg_w_out'], 'v_sg_ln_g': out['v_sg_ln_g'], 'v_sg_ln_b': out['v_sg_ln_b'], 'v_sg_w_spatial': out['v_sg_w_spatial'], 'v_sg_b_spatial': out['v_sg_b_spatial'], 'v_swa_w_in': out['v_swa_w_in'], 'v_swa_w_out': out['v_swa_w_out'], 'v_swa_sinks': out['v_swa_sinks'], 'v_rwkv_w_in': out['v_rwkv_w_in'], 'v_rwkv_w_out': out['v_rwkv_w_out'], 'v_rwkv_mu': out['v_rwkv_mu'], 'v_rwkv_w0': out['v_rwkv_w0'], 'v_rwkv_w_lora': out['v_rwkv_w_lora'], 'v_rwkv_a0': out['v_rwkv_a0'], 'v_rwkv_a_lora': out['v_rwkv_a_lora'], 'v_rwkv_k_k': out['v_rwkv_k_k'], 'v_rwkv_k_a': out['v_rwkv_k_a'], 'v_rwkv_r_k': out['v_rwkv_r_k'], 'v_rwkv_gn_g': out['v_rwkv_gn_g'], 'v_rwkv_gn_b': out['v_rwkv_gn_b']}


def _loss(weights, diff, rest, loss_target):
    with _jax.named_scope("forward"):
        args = {**rest, TWIN_DIFF_INPUT: diff, **{k: w.astype(_WEIGHT_DTYPES[k]) for k, w in weights.items()}}
        y = _forward(args)
    with _jax.named_scope("loss_head"):
        err = _jnp.square(y.astype(_jnp.float32) - loss_target)
        return 0.5 * _jnp.sum(_jnp.mean(err, axis=-1)) if err.ndim else 0.5 * err


def _adamw(w, g, m, v):
    m = ADAM_B1 * m + (1.0 - ADAM_B1) * g
    v = ADAM_B2 * v + (1.0 - ADAM_B2) * _jnp.square(g)
    m_hat = m / (1.0 - ADAM_B1 ** ADAM_STEP)
    v_hat = v / (1.0 - ADAM_B2 ** ADAM_STEP)
    delta = -ADAM_LR * (m_hat / (_jnp.sqrt(v_hat) + ADAM_EPS) + ADAM_WD * w)
    return delta, m, v


def reference(x, c, positions, norm_g, mod_w, mod_b, final_norm_g, sg_w_in, sg_w_out, sg_ln_g, sg_ln_b, sg_w_spatial, sg_b_spatial, swa_w_in, swa_w_out, swa_sinks, rwkv_w_in, rwkv_w_out, rwkv_mu, rwkv_w0, rwkv_w_lora, rwkv_a0, rwkv_a_lora, rwkv_k_k, rwkv_k_a, rwkv_r_k, rwkv_gn_g, rwkv_gn_b, loss_target, m_norm_g, m_mod_w, m_mod_b, m_final_norm_g, m_sg_w_in, m_sg_w_out, m_sg_ln_g, m_sg_ln_b, m_sg_w_spatial, m_sg_b_spatial, m_swa_w_in, m_swa_w_out, m_swa_sinks, m_rwkv_w_in, m_rwkv_w_out, m_rwkv_mu, m_rwkv_w0, m_rwkv_w_lora, m_rwkv_a0, m_rwkv_a_lora, m_rwkv_k_k, m_rwkv_k_a, m_rwkv_r_k, m_rwkv_gn_g, m_rwkv_gn_b, v_norm_g, v_mod_w, v_mod_b, v_final_norm_g, v_sg_w_in, v_sg_w_out, v_sg_ln_g, v_sg_ln_b, v_sg_w_spatial, v_sg_b_spatial, v_swa_w_in, v_swa_w_out, v_swa_sinks, v_rwkv_w_in, v_rwkv_w_out, v_rwkv_mu, v_rwkv_w0, v_rwkv_w_lora, v_rwkv_a0, v_rwkv_a_lora, v_rwkv_k_k, v_rwkv_k_a, v_rwkv_r_k, v_rwkv_gn_g, v_rwkv_gn_b):
    given = dict(x=x, c=c, positions=positions, norm_g=norm_g, mod_w=mod_w, mod_b=mod_b, final_norm_g=final_norm_g, sg_w_in=sg_w_in, sg_w_out=sg_w_out, sg_ln_g=sg_ln_g, sg_ln_b=sg_ln_b, sg_w_spatial=sg_w_spatial, sg_b_spatial=sg_b_spatial, swa_w_in=swa_w_in, swa_w_out=swa_w_out, swa_sinks=swa_sinks, rwkv_w_in=rwkv_w_in, rwkv_w_out=rwkv_w_out, rwkv_mu=rwkv_mu, rwkv_w0=rwkv_w0, rwkv_w_lora=rwkv_w_lora, rwkv_a0=rwkv_a0, rwkv_a_lora=rwkv_a_lora, rwkv_k_k=rwkv_k_k, rwkv_k_a=rwkv_k_a, rwkv_r_k=rwkv_r_k, rwkv_gn_g=rwkv_gn_g, rwkv_gn_b=rwkv_gn_b, loss_target=loss_target, m_norm_g=m_norm_g, m_mod_w=m_mod_w, m_mod_b=m_mod_b, m_final_norm_g=m_final_norm_g, m_sg_w_in=m_sg_w_in, m_sg_w_out=m_sg_w_out, m_sg_ln_g=m_sg_ln_g, m_sg_ln_b=m_sg_ln_b, m_sg_w_spatial=m_sg_w_spatial, m_sg_b_spatial=m_sg_b_spatial, m_swa_w_in=m_swa_w_in, m_swa_w_out=m_swa_w_out, m_swa_sinks=m_swa_sinks, m_rwkv_w_in=m_rwkv_w_in, m_rwkv_w_out=m_rwkv_w_out, m_rwkv_mu=m_rwkv_mu, m_rwkv_w0=m_rwkv_w0, m_rwkv_w_lora=m_rwkv_w_lora, m_rwkv_a0=m_rwkv_a0, m_rwkv_a_lora=m_rwkv_a_lora, m_rwkv_k_k=m_rwkv_k_k, m_rwkv_k_a=m_rwkv_k_a, m_rwkv_r_k=m_rwkv_r_k, m_rwkv_gn_g=m_rwkv_gn_g, m_rwkv_gn_b=m_rwkv_gn_b, v_norm_g=v_norm_g, v_mod_w=v_mod_w, v_mod_b=v_mod_b, v_final_norm_g=v_final_norm_g, v_sg_w_in=v_sg_w_in, v_sg_w_out=v_sg_w_out, v_sg_ln_g=v_sg_ln_g, v_sg_ln_b=v_sg_ln_b, v_sg_w_spatial=v_sg_w_spatial, v_sg_b_spatial=v_sg_b_spatial, v_swa_w_in=v_swa_w_in, v_swa_w_out=v_swa_w_out, v_swa_sinks=v_swa_sinks, v_rwkv_w_in=v_rwkv_w_in, v_rwkv_w_out=v_rwkv_w_out, v_rwkv_mu=v_rwkv_mu, v_rwkv_w0=v_rwkv_w0, v_rwkv_w_lora=v_rwkv_w_lora, v_rwkv_a0=v_rwkv_a0, v_rwkv_a_lora=v_rwkv_a_lora, v_rwkv_k_k=v_rwkv_k_k, v_rwkv_k_a=v_rwkv_k_a, v_rwkv_r_k=v_rwkv_r_k, v_rwkv_gn_g=v_rwkv_gn_g, v_rwkv_gn_b=v_rwkv_gn_b)
    weights = {n: given[n] for n in TWIN_WEIGHTS}
    shared = {n: given[n] for n in SHARED_INPUTS}
    per_example = {n: given[n] for n in ['x', 'c', 'positions']}
    grad_fn = _jax.value_and_grad(_loss, argnums=(0, 1))

    def one_microbatch(ex, loss_target):
        ex = dict(ex)
        diff = ex.pop(TWIN_DIFF_INPUT)
        return grad_fn(weights, diff, {**shared, **ex}, loss_target)

    if N_MICROBATCH == 1:
        loss, (grad_w, grad_x) = one_microbatch(per_example, given["loss_target"])
    else:
        def body(carry, xs):
            loss_sum, grad_sum = carry
            l_k, (gw_k, gx_k) = one_microbatch(xs[0], xs[1])
            with _jax.named_scope("update"):
                return (loss_sum + l_k, _jax.tree.map(_jnp.add, grad_sum, gw_k)), gx_k

        init = (_jnp.zeros((), _jnp.float32), _jax.tree.map(_jnp.zeros_like, weights))
        (loss, grad_w), grad_x = _jax.lax.scan(body, init, (per_example, given["loss_target"]))
    with _jax.named_scope("update"):
        delta_w, new_m, new_v = {}, {}, {}
        for n in TWIN_WEIGHTS:
            delta_w[n], new_m[n], new_v[n] = _adamw(weights[n], grad_w[n], given["m_" + n], given["v_" + n])
    return (loss, grad_x, *[grad_w[n] for n in TWIN_WEIGHTS], *[delta_w[n] for n in TWIN_WEIGHTS],
            *[new_m[n] for n in TWIN_WEIGHTS], *[new_v[n] for n in TWIN_WEIGHTS])
```

```python
import functools
import math

import jax
import jax.numpy as jnp
from jax import lax
from jax.experimental import pallas as pl
from jax.experimental.pallas import tpu as pltpu

F32, BF16 = jnp.float32, jnp.bfloat16
HI = lax.Precision.HIGHEST
S = jax.ShapeDtypeStruct
MESH = pl.DeviceIdType.MESH

N_DEV = 8
DEPTH = 4
HEAD = 64
SG_GROUPS = 16
SG_CHUNK = 128
SWA_BLOCK = 128
SWA_KV = 4
SWA_REP = 8
ROPE_THETA = 10000.0
LORA = 96
LORA_PAD = 256
RW_CHUNK = 64
DECAY_SCALE = math.exp(-0.5)
GN_EPS = 64e-5
RMS_EPS = 1e-6
LN_EPS = 1e-5
NEG = -1e30
ADAM_LR, ADAM_B1, ADAM_B2, ADAM_EPS, ADAM_WD, ADAM_STEP = 0.001, 0.9, 0.999, 1e-08, 0.01, 10
VMEM_MB = 56


def _params(sem=None):
    kw = dict(vmem_limit_bytes=VMEM_MB << 20)
    if sem is not None:
        kw["dimension_semantics"] = sem
    return pltpu.CompilerParams(**kw)


def _pick(n, opts):
    for o in opts:
        if n % o == 0:
            return o
    raise ValueError(f"no tile for {n}")


def _rows(name, fn, rows, consts, out_rows, out_accs, tm):
    t = rows[0].shape[0]
    nr, nc, no = len(rows), len(consts), len(out_rows)

    def body(*refs):
        outs = fn(*[r[...] for r in refs[:nr + nc]])
        if not isinstance(outs, (tuple, list)):
            outs = (outs,)
        for r, o in zip(refs[nr + nc:nr + nc + no], outs[:no]):
            r[...] = o.astype(r.dtype)
        i = pl.program_id(0)
        for r, o in zip(refs[nr + nc + no:], outs[no:]):
            @pl.when(i == 0)
            def _(r=r, o=o):
                r[...] = o.astype(r.dtype)

            @pl.when(i > 0)
            def _(r=r, o=o):
                r[...] += o.astype(r.dtype)

    in_specs = [pl.BlockSpec((tm, a.shape[1]), lambda i: (i, 0)) for a in rows]
    in_specs += [pl.BlockSpec(c.shape, lambda i, nd=c.ndim: (0,) * nd) for c in consts]
    out_specs = [pl.BlockSpec((tm, n), lambda i: (i, 0)) for n, _ in out_rows]
    out_specs += [pl.BlockSpec(s, lambda i, nd=len(s): (0,) * nd) for s in out_accs]
    out_shape = [S((t, n), dt) for n, dt in out_rows] + [S(s, F32) for s in out_accs]
    res = pl.pallas_call(body, grid=(t // tm,), in_specs=in_specs, out_specs=out_specs, out_shape=out_shape,
                         name=name, compiler_params=_params(("arbitrary",)))(*rows, *consts)
    return res


_DN = {"nn": (((1,), (0,)), ((), ())), "nt": (((1,), (1,)), ((), ())), "tn": (((0,), (0,)), ((), ()))}


def _mm(name, a, b, mode, out_dtype, add=None):
    if mode == "nn":
        (m, k), (_, n) = a.shape, b.shape
    elif mode == "nt":
        (m, k), (n, _) = a.shape, b.shape
    else:
        (k, m), (_, n) = a.shape, b.shape
    tm, tn, tk = _pick(m, (512, 256, 128)), _pick(n, (512, 384, 256, 128)), _pick(k, (512, 384, 256, 128))
    nk = k // tk
    has_add = add is not None

    def body(*refs):
        a_ref, b_ref = refs[0], refs[1]
        o_ref, acc = refs[-2], refs[-1]
        kk = pl.program_id(2)

        @pl.when(kk == 0)
        def _():
            acc[...] = refs[2][...].astype(F32) if has_add else jnp.zeros_like(acc)

        acc[...] += lax.dot_general(a_ref[...].astype(BF16), b_ref[...].astype(BF16), _DN[mode],
                                    preferred_element_type=F32)

        @pl.when(kk == nk - 1)
        def _():
            o_ref[...] = acc[...].astype(o_ref.dtype)

    a_spec = pl.BlockSpec((tk, tm), lambda i, j, q: (q, i)) if mode == "tn" else pl.BlockSpec((tm, tk), lambda i, j, q: (i, q))
    b_spec = pl.BlockSpec((tn, tk), lambda i, j, q: (j, q)) if mode == "nt" else pl.BlockSpec((tk, tn), lambda i, j, q: (q, j))
    o_spec = pl.BlockSpec((tm, tn), lambda i, j, q: (i, j))
    ins, specs = [a, b], [a_spec, b_spec]
    if has_add:
        ins.append(add)
        specs.append(o_spec)
    return pl.pallas_call(body, grid=(m // tm, n // tn, nk), in_specs=specs, out_specs=o_spec,
                          out_shape=S((m, n), out_dtype), scratch_shapes=[pltpu.VMEM((tm, tn), F32)], name=name,
                          compiler_params=_params(("parallel", "parallel", "arbitrary")))(*ins)


def _exchange(name, src, scatter):
    blk = src.shape[1:] if scatter else src.shape

    def body(src_ref, dst_ref, send_sems, recv_sems, loc_sem):
        x, y, c = lax.axis_index("x"), lax.axis_index("y"), lax.axis_index("c")
        me = 4 * x + 2 * y + c

        def mine(d):
            return src_ref.at[d] if scatter else src_ref

        local = pltpu.make_async_copy(mine(me), dst_ref.at[me], loc_sem)
        local.start()
        sends, peers = [], []
        for k in range(1, N_DEV):
            px = 1 - x if k & 4 else x
            py = 1 - y if k & 2 else y
            pc = 1 - c if k & 1 else c
            pid = 4 * px + 2 * py + pc
            cp = pltpu.make_async_remote_copy(src_ref=mine(pid), dst_ref=dst_ref.at[me], send_sem=send_sems.at[k - 1],
                                              recv_sem=recv_sems.at[k - 1], device_id=(px, py, pc), device_id_type=MESH)
            cp.start()
            sends.append(cp)
            peers.append((pid, (px, py, pc)))
        for k in range(1, N_DEV):
            pid, dev = peers[k - 1]
            pltpu.make_async_remote_copy(src_ref=mine(pid), dst_ref=dst_ref.at[pid], send_sem=send_sems.at[k - 1],
                                         recv_sem=recv_sems.at[k - 1], device_id=dev, device_id_type=MESH).wait_recv()
        for cp in sends:
            cp.wait_send()
        local.wait()

    return pl.pallas_call(
        body, out_shape=S((N_DEV,) + tuple(blk), src.dtype),
        in_specs=[pl.BlockSpec(memory_space=pl.ANY)], out_specs=pl.BlockSpec(memory_space=pl.ANY),
        scratch_shapes=[pltpu.SemaphoreType.DMA((N_DEV - 1,)), pltpu.SemaphoreType.DMA((N_DEV - 1,)),
                        pltpu.SemaphoreType.DMA],
        name=name)(src)


def _sum_parts(name, parts):
    _, r, c = parts.shape
    tm = _pick(r, (512, 256, 128, 64, 32, 16, 8)) if r % 8 == 0 else r

    def body(p_ref, o_ref):
        acc = p_ref[0].astype(F32)
        for d in range(1, N_DEV):
            acc = acc + p_ref[d].astype(F32)
        o_ref[...] = acc

    return pl.pallas_call(body, grid=(r // tm,), in_specs=[pl.BlockSpec((N_DEV, tm, c), lambda i: (0, i, 0))],
                          out_specs=pl.BlockSpec((tm, c), lambda i: (i, 0)), out_shape=S((r, c), F32), name=name,
                          compiler_params=_params(("parallel",)))(parts)


def _adamw(name, w, g, m, v):
    r, c = w.shape
    parts = g.ndim == 3
    tm = _pick(r, (256, 128, 64, 32, 16, 8)) if r % 8 == 0 else r

    def body(w_ref, g_ref, m_ref, v_ref, go_ref, d_ref, mo_ref, vo_ref):
        if parts:
            gg = g_ref[0].astype(F32)
            for d in range(1, N_DEV):
                gg = gg + g_ref[d].astype(F32)
        else:
            gg = g_ref[...]
        mm = ADAM_B1 * m_ref[...] + (1.0 - ADAM_B1) * gg
        vv = ADAM_B2 * v_ref[...] + (1.0 - ADAM_B2) * jnp.square(gg)
        m_hat = mm / (1.0 - ADAM_B1 ** ADAM_STEP)
        v_hat = vv / (1.0 - ADAM_B2 ** ADAM_STEP)
        go_ref[...] = gg
        d_ref[...] = -ADAM_LR * (m_hat / (jnp.sqrt(v_hat) + ADAM_EPS) + ADAM_WD * w_ref[...])
        mo_ref[...] = mm
        vo_ref[...] = vv

    spec = pl.BlockSpec((tm, c), lambda i: (i, 0))
    g_spec = pl.BlockSpec((N_DEV, tm, c), lambda i: (0, i, 0)) if parts else spec
    return pl.pallas_call(body, grid=(r // tm,), in_specs=[spec, g_spec, spec, spec], out_specs=[spec] * 4,
                          out_shape=[S((r, c), F32)] * 4, name=name, compiler_params=_params(("parallel",)))(w, g, m, v)


def _rms(x, g):
    return x * lax.rsqrt(jnp.mean(x * x, axis=-1, keepdims=True) + RMS_EPS) * g


def _adaln(x, g, shift, scale):
    return _rms(x, g) * (1.0 + scale) + shift


def _dot(a, b, dn="nn", hi=False):
    if hi:
        return lax.dot_general(a, b, _DN[dn], precision=HI, preferred_element_type=F32)
    return lax.dot_general(a.astype(BF16), b.astype(BF16), _DN[dn], preferred_element_type=F32)


def _sg_mix(p, ln_g, ln_b, w_s, b_st):
    d = p.shape[1] // 3
    gd = d // SG_GROUPS
    u = jax.nn.gelu(p[:, :d])
    vf = jax.nn.gelu(p[:, d:2 * d])
    z = p[:, 2 * d:]
    mean = jnp.mean(vf, axis=-1, keepdims=True)
    var = jnp.mean(jnp.square(vf - mean), axis=-1, keepdims=True)
    vn = (vf - mean) * lax.rsqrt(var + LN_EPS) * ln_g + ln_b
    row = lax.broadcasted_iota(jnp.int32, (SG_CHUNK, SG_CHUNK), 0)
    col = lax.broadcasted_iota(jnp.int32, (SG_CHUNK, SG_CHUNK), 1)
    fs = []
    for g in range(SG_GROUPS):
        w = jnp.where(row >= col, w_s[g], 0.0)
        fs.append(_dot(w, vn[:, g * gd:(g + 1) * gd]))
    sel = (lax.broadcasted_iota(jnp.int32, (SG_GROUPS, d), 1) // gd
           == lax.broadcasted_iota(jnp.int32, (SG_GROUPS, d), 0)).astype(F32)
    f = jnp.concatenate(fs, axis=1) + _dot(b_st, sel, hi=True)
    return u * f * jax.nn.silu(z)


def _rot_half(x):
    n = x.shape[1]
    lane = lax.broadcasted_iota(jnp.int32, x.shape, 1)
    return jnp.where(lane % HEAD < HEAD // 2, -pltpu.roll(x, n - HEAD // 2, 1), pltpu.roll(x, HEAD // 2, 1))


def _rope(x, cos, sin, sign):
    reps = x.shape[1] // cos.shape[1]
    return x * jnp.tile(cos, (1, reps)) + sign * _rot_half(x) * jnp.tile(sin, (1, reps))


def _attn_block(q, kp, kc, vp, vc, sink, prev_bias):
    r = q.shape[0]
    q2 = q.reshape(r * SWA_BLOCK, HEAD)
    sp = (_dot(q2, kp, "nt") * (HEAD ** -0.5)).reshape(r, SWA_BLOCK, SWA_BLOCK)
    sc = (_dot(q2, kc, "nt") * (HEAD ** -0.5)).reshape(r, SWA_BLOCK, SWA_BLOCK)
    qi = lax.broadcasted_iota(jnp.int32, (r, SWA_BLOCK, SWA_BLOCK), 1)
    kj = lax.broadcasted_iota(jnp.int32, (r, SWA_BLOCK, SWA_BLOCK), 2)
    sp = jnp.where(kj > qi, sp, NEG) + prev_bias
    sc = jnp.where(kj <= qi, sc, NEG)
    m = jnp.maximum(jnp.maximum(jnp.max(sp, axis=-1, keepdims=True), jnp.max(sc, axis=-1, keepdims=True)), sink)
    ep, ec = jnp.exp(sp - m), jnp.exp(sc - m)
    denom = jnp.sum(ep, axis=-1, keepdims=True) + jnp.sum(ec, axis=-1, keepdims=True) + jnp.exp(sink - m)
    pp = (ep / denom).reshape(r * SWA_BLOCK, SWA_BLOCK)
    pc = (ec / denom).reshape(r * SWA_BLOCK, SWA_BLOCK)
    return (_dot(pp, vp) + _dot(pc, vc)).reshape(r, SWA_BLOCK, HEAD)


def _rwkv_chunk(s0, r, k, v, logw, a, k_k, k_a, r_k, gn_g, gn_b):
    c = r.shape[0]
    kk = k * k_k
    kk = kk / jnp.maximum(jnp.sqrt(jnp.sum(kk * kk, axis=-1, keepdims=True)), 1e-12)
    km = k * (1.0 + (a - 1.0) * k_a)
    b = kk * a
    row = lax.broadcasted_iota(jnp.int32, (c, c), 0)
    col = lax.broadcasted_iota(jnp.int32, (c, c), 1)
    cum = _dot((row >= col).astype(F32), logw, hi=True)
    p_in, p_prev, p_inv = jnp.exp(cum), jnp.exp(cum - logw), jnp.exp(-cum)
    alpha, beta, kap, rho = kk * p_prev, b * p_inv, km * p_inv, r * p_in
    strict = row > col
    lab = jnp.where(strict, _dot(alpha, beta, "nt", hi=True), 0.0)
    lak = jnp.where(strict, _dot(alpha, kap, "nt", hi=True), 0.0)
    xs = _dot(alpha, s0, "nt", hi=True) + _dot(lak, v, hi=True)
    xs = xs - _dot(lab, xs, hi=True)
    lp, power = lab, 2
    while power < c:
        lp = _dot(lp, lp, hi=True)
        xs = xs + _dot(lp, xs, hi=True)
        power *= 2
    u = -xs
    incl = row >= col
    mrb = jnp.where(incl, _dot(rho, beta, "nt", hi=True), 0.0)
    mrk = jnp.where(incl, _dot(rho, kap, "nt", hi=True), 0.0)
    y = _dot(rho, s0, "nt", hi=True) + _dot(mrb, u, hi=True) + _dot(mrk, v, hi=True)
    p_last = jnp.exp(jnp.sum(logw, axis=0, keepdims=True))
    s1 = (s0 + _dot(u, beta, "tn", hi=True) + _dot(v, kap, "tn", hi=True)) * p_last
    mean = jnp.mean(y, axis=-1, keepdims=True)
    var = jnp.mean(jnp.square(y - mean), axis=-1, keepdims=True)
    y = (y - mean) * lax.rsqrt(var + GN_EPS) * gn_g + gn_b
    y = y + jnp.sum(r * km * r_k, axis=-1, keepdims=True) * v
    return y, s1


def _norm_fwd(name, x, g, shift, scale):
    return _rows(name, lambda x_, g_, sh, sc: _adaln(x_, g_, sh, sc), [x], [g, shift, scale], [(x.shape[1], BF16)], [], 256)[0]


def _norm_bwd(name, x, dh, dx_res, g, shift, scale):
    d = x.shape[1]

    def fn(x_, dh_, dr_, g_, sh, sc):
        _, vjp = jax.vjp(_adaln, x_, g_, sh, sc)
        dx, dg, dsh, dsc = vjp(dh_)
        return dx + dr_, dg, dsh, dsc

    return _rows(name, fn, [x, dh, dx_res], [g, shift, scale], [(d, F32)], [(1, d)] * 3, 256)


def _resid_fwd(name, x, y, gate):
    return _rows(name, lambda x_, y_, g_: x_ + g_ * y_, [x, y], [gate], [(x.shape[1], F32)], [], 256)[0]


def _resid_bwd(name, dx, y, gate):
    d = dx.shape[1]
    return _rows(name, lambda dx_, y_, g_: (g_ * dx_, jnp.sum(dx_ * y_, axis=0, keepdims=True)), [dx, y], [gate],
                 [(d, BF16)], [(1, d)], 256)


def _sg_fwd(name, p, ln_g, ln_b, w_s, b_st):
    d = p.shape[1] // 3
    return _rows(name, _sg_mix, [p], [ln_g, ln_b, w_s, b_st], [(d, BF16)], [], SG_CHUNK)[0]


def _sg_bwd(name, p, dmix, ln_g, ln_b, w_s, b_st):
    def fn(p_, dm_, lg, lb, ws, bs):
        _, vjp = jax.vjp(_sg_mix, p_, lg, lb, ws, bs)
        return vjp(dm_)

    return _rows(name, fn, [p, dmix], [ln_g, ln_b, w_s, b_st], [(p.shape[1], BF16)],
                 [ln_g.shape, ln_b.shape, w_s.shape, b_st.shape], SG_CHUNK)


def _rope_tables(pos, inv_freq):
    ang = pos * inv_freq
    return jnp.cos(ang), jnp.sin(ang)


def _swa_pre(name, p, pos, inv_freq, d):
    kvw = SWA_KV * HEAD

    def fn(p_, pos_, fr):
        cos, sin = _rope_tables(pos_, fr)
        return (_rope(p_[:, :d], cos, sin, 1.0), _rope(p_[:, d:d + kvw], cos, sin, 1.0), p_[:, d + kvw:d + 2 * kvw])

    return _rows(name, fn, [p, pos], [inv_freq], [(d, BF16), (kvw, BF16), (kvw, BF16)], [], 256)


def _swa_attn_fwd(name, q, k, v, sinks):
    kv, r, t, _ = q.shape
    nb = t // SWA_BLOCK

    def body(q_ref, kp_ref, kc_ref, vp_ref, vc_ref, s_ref, o_ref):
        prev_bias = jnp.where(pl.program_id(1) > 0, 0.0, NEG).astype(F32)
        o_ref[0] = _attn_block(q_ref[0], kp_ref[0], kc_ref[0], vp_ref[0], vc_ref[0], s_ref[0], prev_bias)

    qs = pl.BlockSpec((1, r, SWA_BLOCK, HEAD), lambda g, n: (g, 0, n, 0))
    cur = pl.BlockSpec((1, SWA_BLOCK, HEAD), lambda g, n: (g, n, 0))
    prev = pl.BlockSpec((1, SWA_BLOCK, HEAD), lambda g, n: (g, jnp.maximum(n - 1, 0), 0))
    ss = pl.BlockSpec((1, r, 1, 1), lambda g, n: (g, 0, 0, 0))
    return pl.pallas_call(body, grid=(kv, nb), in_specs=[qs, prev, cur, prev, cur, ss], out_specs=qs,
                          out_shape=S(q.shape, F32), name=name,
                          compiler_params=_params(("parallel", "arbitrary")))(q, k, k, v, v, sinks)


def _swa_attn_bwd(name, q, k, v, sinks, do):
    kv, r, t, _ = q.shape
    nb = t // SWA_BLOCK

    def body(q_ref, kp_ref, kc_ref, vp_ref, vc_ref, s_ref, do_ref, dq_ref, dkc_ref, dkp_ref, dvc_ref, dvp_ref, ds_ref):
        n = pl.program_id(1)
        prev_bias = jnp.where(n > 0, 0.0, NEG).astype(F32)
        fn = functools.partial(_attn_block, prev_bias=prev_bias)
        args = [ref[0].astype(F32) for ref in (q_ref, kp_ref, kc_ref, vp_ref, vc_ref)] + [s_ref[0]]
        _, vjp = jax.vjp(fn, *args)
        dq, dkp, dkc, dvp, dvc, ds = vjp(do_ref[0])
        dq_ref[0], dkc_ref[0], dkp_ref[0], dvc_ref[0], dvp_ref[0] = dq, dkc, dkp, dvc, dvp

        @pl.when(n == 0)
        def _():
            ds_ref[0] = ds

        @pl.when(n > 0)
        def _():
            ds_ref[0] += ds

    qs = pl.BlockSpec((1, r, SWA_BLOCK, HEAD), lambda g, n: (g, 0, n, 0))
    cur = pl.BlockSpec((1, SWA_BLOCK, HEAD), lambda g, n: (g, n, 0))
    prev = pl.BlockSpec((1, SWA_BLOCK, HEAD), lambda g, n: (g, jnp.maximum(n - 1, 0), 0))
    ss = pl.BlockSpec((1, r, 1, 1), lambda g, n: (g, 0, 0, 0))
    return pl.pallas_call(
        body, grid=(kv, nb), in_specs=[qs, prev, cur, prev, cur, ss, qs], out_specs=[qs, cur, cur, cur, cur, ss],
        out_shape=[S(q.shape, F32)] + [S(k.shape, F32)] * 4 + [S(sinks.shape, F32)], name=name,
        compiler_params=_params(("parallel", "arbitrary")))(q, k, k, v, v, sinks, do)


def _gate_fwd(name, o, z_src, z_off, d):
    return _rows(name, lambda o_, p_: o_ * jax.nn.silu(p_[:, z_off:z_off + d]), [o, z_src], [], [(d, BF16)], [], 256)[0]


def _gate_bwd(name, o, z_src, z_off, d, dmix):
    def fn(o_, p_, dm_):
        _, vjp = jax.vjp(lambda oo, zz: oo * jax.nn.silu(zz), o_, p_[:, z_off:z_off + d])
        return vjp(dm_)

    return _rows(name, fn, [o, z_src, dmix], [], [(d, F32), (d, F32)], [], 256)


def _swa_post_bwd(name, dq, dkc, dkp_up, dvc, dvp_up, dz, pos, inv_freq):
    def fn(dq_, dkc_, dkp_, dvc_, dvp_, dz_, pos_, fr):
        cos, sin = _rope_tables(pos_, fr)
        return jnp.concatenate([_rope(dq_, cos, sin, -1.0), _rope(dkc_ + dkp_, cos, sin, -1.0), dvc_ + dvp_, dz_], axis=1)

    n = dq.shape[1] + dkc.shape[1] + dvc.shape[1] + dz.shape[1]
    return _rows(name, fn, [dq, dkc, dkp_up, dvc, dvp_up, dz, pos], [inv_freq], [(n, BF16)], [], 256)[0]


def _lerp_fwd(name, p, ps, mu, widths):
    def fn(p_, ps_, mu_):
        pm = p_ + (ps_ - p_) * mu_
        outs, o = [], 0
        for w in widths:
            outs.append(pm[:, o:o + w])
            o += w
        return tuple(outs)

    return _rows(name, fn, [p, ps], [mu], [(w, F32) for w in widths], [], 128)


def _lerp_bwd(name, dpm_parts, p, ps, mu):
    n = p.shape[1]
    k = len(dpm_parts)

    def fn(*a):
        dpm = jnp.concatenate(a[:k], axis=1) if k > 1 else a[0]
        p_, ps_, mu_ = a[k:]
        return dpm * (1.0 - mu_), dpm * mu_, jnp.sum(dpm * (ps_ - p_), axis=0, keepdims=True)

    return _rows(name, fn, list(dpm_parts) + [p, ps], [mu], [(n, F32), (n, F32)], [(1, n)], 64)


def _shift_add(name, a, b_up, dtype):
    return _rows(name, lambda x_, y_: x_ + y_, [a, b_up], [], [(a.shape[1], dtype)], [], 128)[0]


def _lora_act(pl_, w0, w_lora, a0, a_lora):
    logw = -DECAY_SCALE * jax.nn.sigmoid(w0 + _dot(jnp.tanh(pl_), w_lora))
    a = jax.nn.sigmoid(a0 + _dot(pl_, a_lora))
    return logw, a


def _lora_fwd(name, pl_, w0, w_lora, a0, a_lora):
    d = w0.shape[1]
    return _rows(name, _lora_act, [pl_], [w0, w_lora, a0, a_lora], [(d, F32), (d, F32)], [], 256)


def _lora_bwd(name, pl_, dlogw, da, w0, w_lora, a0, a_lora):
    def fn(p_, dl_, da_, w0_, wl_, a0_, al_):
        _, vjp = jax.vjp(_lora_act, p_, w0_, wl_, a0_, al_)
        return vjp((dl_, da_))

    return _rows(name, fn, [pl_, dlogw, da], [w0, w_lora, a0, a_lora], [(pl_.shape[1], F32)],
                 [w0.shape, w_lora.shape, a0.shape, a_lora.shape], 256)


def _rwkv_scan_fwd(name, r, k, v, logw, a, hp):
    h, t, _ = r.shape
    nc = t // RW_CHUNK

    def body(r_ref, k_ref, v_ref, w_ref, a_ref, kk_ref, ka_ref, rk_ref, gg_ref, gb_ref, y_ref, st_ref, s_scr):
        @pl.when(pl.program_id(1) == 0)
        def _():
            s_scr[...] = jnp.zeros_like(s_scr)

        s0 = s_scr[...]
        st_ref[0, 0] = s0
        y, s1 = _rwkv_chunk(s0, r_ref[0], k_ref[0], v_ref[0], w_ref[0], a_ref[0],
                            kk_ref[0], ka_ref[0], rk_ref[0], gg_ref[0], gb_ref[0])
        y_ref[0] = y
        s_scr[...] = s1

    seq = pl.BlockSpec((1, RW_CHUNK, HEAD), lambda i, n: (i, n, 0))
    par = pl.BlockSpec((1, 1, HEAD), lambda i, n: (i, 0, 0))
    st = pl.BlockSpec((1, 1, HEAD, HEAD), lambda i, n: (i, n, 0, 0))
    return pl.pallas_call(body, grid=(h, nc), in_specs=[seq] * 5 + [par] * 5, out_specs=[seq, st],
                          out_shape=[S((h, t, HEAD), F32), S((h, nc, HEAD, HEAD), F32)],
                          scratch_shapes=[pltpu.VMEM((HEAD, HEAD), F32)], name=name,
                          compiler_params=_params(("parallel", "arbitrary")))(r, k, v, logw, a, *hp)


def _rwkv_scan_bwd(name, r, k, v, logw, a, hp, states, dy):
    h, t, _ = r.shape
    nc = t // RW_CHUNK

    def body(r_ref, k_ref, v_ref, w_ref, a_ref, kk_ref, ka_ref, rk_ref, gg_ref, gb_ref, st_ref, dy_ref,
             dr_ref, dk_ref, dv_ref, dw_ref, da_ref, dkk_ref, dka_ref, drk_ref, dgg_ref, dgb_ref, ds_scr):
        n = pl.program_id(1)

        @pl.when(n == 0)
        def _():
            ds_scr[...] = jnp.zeros_like(ds_scr)

        _, vjp = jax.vjp(_rwkv_chunk, st_ref[0, 0], r_ref[0], k_ref[0], v_ref[0], w_ref[0], a_ref[0],
                         kk_ref[0], ka_ref[0], rk_ref[0], gg_ref[0], gb_ref[0])
        ds0, dr, dk, dv, dw, da, dkk, dka, drk, dgg, dgb = vjp((dy_ref[0], ds_scr[...]))
        ds_scr[...] = ds0
        dr_ref[0], dk_ref[0], dv_ref[0], dw_ref[0], da_ref[0] = dr, dk, dv, dw, da
        for ref, val in ((dkk_ref, dkk), (dka_ref, dka), (drk_ref, drk), (dgg_ref, dgg), (dgb_ref, dgb)):
            @pl.when(n == 0)
            def _(ref=ref, val=val):
                ref[0] = val

            @pl.when(n > 0)
            def _(ref=ref, val=val):
                ref[0] += val

    seq = pl.BlockSpec((1, RW_CHUNK, HEAD), lambda i, n: (i, nc - 1 - n, 0))
    par = pl.BlockSpec((1, 1, HEAD), lambda i, n: (i, 0, 0))
    st = pl.BlockSpec((1, 1, HEAD, HEAD), lambda i, n: (i, nc - 1 - n, 0, 0))
    return pl.pallas_call(body, grid=(h, nc), in_specs=[seq] * 5 + [par] * 5 + [st, seq], out_specs=[seq] * 5 + [par] * 5,
                          out_shape=[S((h, t, HEAD), F32)] * 5 + [S((h, 1, HEAD), F32)] * 5,
                          scratch_shapes=[pltpu.VMEM((HEAD, HEAD), F32)], name=name,
                          compiler_params=_params(("parallel", "arbitrary")))(r, k, v, logw, a, *hp, states, dy)


def _loss_head(name, x, target, g):
    d = x.shape[1]

    def fn(x_, t_, g_):
        def f(xx, gg):
            err = _rms(xx, gg) - t_
            return 0.5 * jnp.sum(jnp.mean(err * err, axis=-1, keepdims=True), axis=0, keepdims=True)

        l, vjp = jax.vjp(f, x_, g_)
        dx, dg = vjp(jnp.ones((1, 1), F32))
        return dx, dg, jnp.broadcast_to(l, (1, 128))

    return _rows(name, fn, [x, target], [g], [(d, F32)], [(1, d), (1, 128)], 256)


def _mod_fwd(name, cond_all, mod_w, mod_b_cols):
    l, d, n = mod_w.shape

    def body(c_ref, w_ref, b_ref, o_ref):
        o_ref[0] = _dot(jax.nn.silu(c_ref[...]), w_ref[0], hi=True) + b_ref[0]

    return pl.pallas_call(body, grid=(l,), in_specs=[pl.BlockSpec((N_DEV, d), lambda i: (0, 0)),
                                                      pl.BlockSpec((1, d, n), lambda i: (i, 0, 0)),
                                                      pl.BlockSpec((1, 1, n), lambda i: (i, 0, 0))],
                          out_specs=pl.BlockSpec((1, N_DEV, n), lambda i: (i, 0, 0)), out_shape=S((l, N_DEV, n), F32),
                          name=name, compiler_params=_params(("parallel",)))(cond_all, mod_w, mod_b_cols)


def _mod_bwd(name, cond_all, dmod_cols, dmod_all):
    l, _, n = dmod_cols.shape
    d = cond_all.shape[1]
    nb = dmod_all.shape[2]

    def body(c_ref, dc_ref, da_ref, gw_ref, gb_ref):
        gw_ref[0] = _dot(jax.nn.silu(c_ref[...]), dc_ref[0], "tn", hi=True)
        acc = da_ref[0, 0:1, :]
        for bi in range(1, N_DEV):
            acc = acc + da_ref[0, bi:bi + 1, :]
        gb_ref[0] = acc

    return pl.pallas_call(body, grid=(l,), in_specs=[pl.BlockSpec((N_DEV, d), lambda i: (0, 0)),
                                                      pl.BlockSpec((1, N_DEV, n), lambda i: (i, 0, 0)),
                                                      pl.BlockSpec((1, N_DEV, nb), lambda i: (i, 0, 0))],
                          out_specs=[pl.BlockSpec((1, d, n), lambda i: (i, 0, 0)), pl.BlockSpec((1, 1, nb), lambda i: (i, 0, 0))],
                          out_shape=[S((l, d, n), F32), S((l, 1, nb), F32)], name=name,
                          compiler_params=_params(("parallel",)))(cond_all, dmod_cols, dmod_all)


def _to_heads(a):
    t, n = a.shape
    return a.reshape(t, n // HEAD, HEAD).transpose(1, 0, 2)


def _from_heads(a):
    h, t, _ = a.shape
    return a.transpose(1, 0, 2).reshape(t, h * HEAD)


def _shift_down(a):
    return jnp.concatenate([jnp.zeros_like(a[:1]), a[:-1]], axis=0)


def _shift_up(a, n=1):
    return jnp.concatenate([a[n:], jnp.zeros_like(a[:n])], axis=0)


def _cols_full(g):
    return g.transpose(1, 0, 2).reshape(g.shape[1], -1)


def _cols_parts(full):
    r, n = full.shape
    return full.reshape(r, N_DEV, n // N_DEV).transpose(1, 0, 2)


def _pack(arrs, mult=1024):
    flat = jnp.concatenate([a.reshape(-1) for a in arrs])
    pad = (-flat.shape[0]) % mult
    return jnp.pad(flat, (0, pad)).reshape(-1, 128)


def _unpack(flat, shapes):
    out, o = [], 0
    for s in shapes:
        n = math.prod(s)
        out.append(flat[o:o + n].reshape(s))
        o += n
    return out


def _local_step(x, pos, target, mods, norm_g, final_norm_g, sg, swa, rw):
    t, d = x.shape
    kinds = [i % 3 for i in range(DEPTH)]
    inv_freq = (ROPE_THETA ** (-jnp.arange(HEAD // 2, dtype=F32) / (HEAD // 2)))
    inv_freq = jnp.tile(inv_freq, 128 // (HEAD // 2)).reshape(1, 128)
    saved = []
    for i, kind in enumerate(kinds):
        j = i // 3
        shift, scale, gate = (mods[i, q * d:(q + 1) * d].reshape(1, d) for q in range(3))
        g = norm_g[i].reshape(1, d)
        h = _norm_fwd(f"norm_fwd{i}", x, g, shift, scale)
        sv = dict(x=x, h=h, g=g, shift=shift, scale=scale, gate=gate)
        if kind == 0:
            p = _mm(f"sg_in{i}", h, sg["w_in"][j], "nn", F32)
            mix = _sg_fwd(f"sg_mix{i}", p, sg["ln_g"][j], sg["ln_b"][j], sg["w_s"][j], sg["b_st"][j])
            w_out = sg["w_out"][j]
            sv.update(p=p)
        elif kind == 1:
            p = _mm(f"swa_in{i}", h, swa["w_in"], "nn", F32)
            q, k, v = _swa_pre(f"swa_pre{i}", p, pos, inv_freq, d)
            qh = _to_heads(q).reshape(SWA_KV, SWA_REP, t, HEAD)
            kh, vh = _to_heads(k), _to_heads(v)
            o = _swa_attn_fwd(f"swa_attn{i}", qh, kh, vh, swa["sinks"])
            o = _from_heads(o.reshape(SWA_KV * SWA_REP, t, HEAD))
            mix = _gate_fwd(f"swa_gate{i}", o, p, d + 2 * SWA_KV * HEAD, d)
            w_out = swa["w_out"]
            sv.update(p=p, qh=qh, kh=kh, vh=vh, o=o)
        else:
            pm = _mm(f"rw_in{i}", h, rw["w_main"], "nn", F32)
            plo = _mm(f"rw_inl{i}", h, rw["w_lorain"], "nn", F32)
            pms, plos = _shift_down(pm), _shift_down(plo)
            r, k, v, z = _lerp_fwd(f"rw_lerp{i}", pm, pms, rw["mu_main"], [d] * 4)
            (pll,) = _lerp_fwd(f"rw_lerpl{i}", plo, plos, rw["mu_lora"], [LORA_PAD])
            logw, a = _lora_fwd(f"rw_lora{i}", pll, rw["w0"], rw["w_lora"], rw["a0"], rw["a_lora"])
            hs = [_to_heads(u) for u in (r, k, v, logw, a)]
            yh, states = _rwkv_scan_fwd(f"rw_scan{i}", *hs, rw["hp"])
            o = _from_heads(yh)
            mix = _gate_fwd(f"rw_gate{i}", o, z, 0, d)
            w_out = rw["w_out"]
            sv.update(pm=pm, pms=pms, plo=plo, plos=plos, pll=pll, z=z, hs=hs, states=states, o=o)
        y = _mm(f"out{i}", mix, w_out, "nn", F32)
        sv.update(mix=mix, y=y)
        saved.append(sv)
        x = _resid_fwd(f"resid{i}", x, y, gate)

    dx, d_final_g, loss = _loss_head("loss_head", x, target, final_norm_g.reshape(1, d))

    grads = dict(norm_g=[None] * DEPTH, sg_w_in=[None] * 2, sg_w_out=[None] * 2, sg_ln_g=[None] * 2, sg_ln_b=[None] * 2,
                 sg_w_s=[None] * 2, sg_b_st=[None] * 2, final_norm_g=d_final_g)
    dmods = [None] * DEPTH
    for i in reversed(range(DEPTH)):
        kind, j, sv = kinds[i], i // 3, saved[i]
        dy, dgate = _resid_bwd(f"resid_bwd{i}", dx, sv["y"], sv["gate"])
        w_out = (sg["w_out"][j], swa["w_out"], rw["w_out"])[kind]
        d_w_out = _mm(f"out_dw{i}", sv["mix"], dy, "tn", F32)
        dmix = _mm(f"out_dx{i}", dy, w_out, "nt", F32)
        if kind == 0:
            dp, dlg, dlb, dws, dbs = _sg_bwd(f"sg_mix_bwd{i}", sv["p"], dmix, sg["ln_g"][j], sg["ln_b"][j],
                                            sg["w_s"][j], sg["b_st"][j])
            grads["sg_ln_g"][j], grads["sg_ln_b"][j], grads["sg_w_s"][j], grads["sg_b_st"][j] = dlg, dlb, dws, dbs
            grads["sg_w_out"][j] = d_w_out
            grads["sg_w_in"][j] = _mm(f"sg_in_dw{i}", sv["h"], dp, "tn", F32)
            dh = _mm(f"sg_in_dx{i}", dp, sg["w_in"][j], "nt", F32)
        elif kind == 1:
            z_off = d + 2 * SWA_KV * HEAD
            do, dz = _gate_bwd(f"swa_gate_bwd{i}", sv["o"], sv["p"], z_off, d, dmix)
            doh = _to_heads(do).reshape(SWA_KV, SWA_REP, t, HEAD)
            dq, dkc, dkp, dvc, dvp, dsinks = _swa_attn_bwd(f"swa_attn_bwd{i}", sv["qh"], sv["kh"], sv["vh"], swa["sinks"], doh)
            dq = _from_heads(dq.reshape(SWA_KV * SWA_REP, t, HEAD))
            dkc, dvc = _from_heads(dkc), _from_heads(dvc)
            dkp, dvp = _shift_up(_from_heads(dkp), SWA_BLOCK), _shift_up(_from_heads(dvp), SWA_BLOCK)
            dp = _swa_post_bwd(f"swa_post_bwd{i}", dq, dkc, dkp, dvc, dvp, dz, pos, inv_freq)
            grads.update(swa_sinks=dsinks, swa_w_out=d_w_out)
            grads["swa_w_in"] = _mm(f"swa_in_dw{i}", sv["h"], dp, "tn", F32)
            dh = _mm(f"swa_in_dx{i}", dp, swa["w_in"], "nt", F32)
        else:
            do, dz = _gate_bwd(f"rw_gate_bwd{i}", sv["o"], sv["z"], 0, d, dmix)
            res = _rwkv_scan_bwd(f"rw_scan_bwd{i}", *sv["hs"], rw["hp"], sv["states"], _to_heads(do))
            dr, dk, dv, dlogw, da = (_from_heads(u) for u in res[:5])
            dpll, dw0, dwl, da0, dal = _lora_bwd(f"rw_lora_bwd{i}", sv["pll"], dlogw, da, rw["w0"], rw["w_lora"],
                                                  rw["a0"], rw["a_lora"])
            dpm, dpms, dmu_main = _lerp_bwd(f"rw_lerp_bwd{i}", [dr, dk, dv, dz], sv["pm"], sv["pms"], rw["mu_main"])
            dpl, dpls, dmu_lora = _lerp_bwd(f"rw_lerpl_bwd{i}", [dpll], sv["plo"], sv["plos"], rw["mu_lora"])
            dpm = _shift_add(f"rw_shift_add{i}", dpm, _shift_up(dpms), BF16)
            dpl = _shift_add(f"rw_shift_addl{i}", dpl, _shift_up(dpls), BF16)
            grads.update(rw_w_out=d_w_out, rw_hp=res[5:], rw_w0=dw0, rw_w_lora=dwl, rw_a0=da0, rw_a_lora=dal,
                         rw_mu_main=dmu_main, rw_mu_lora=dmu_lora)
            grads["rw_w_main"] = _mm(f"rw_in_dw{i}", sv["h"], dpm, "tn", F32)
            grads["rw_w_lorain"] = _mm(f"rw_inl_dw{i}", sv["h"], dpl, "tn", F32)
            dh = _mm(f"rw_inl_dx{i}", dpl, rw["w_lorain"], "nt", F32)
            dh = _mm(f"rw_in_dx{i}", dpm, rw["w_main"], "nt", F32, add=dh)
        dx, dg, dshift, dscale = _norm_bwd(f"norm_bwd{i}", sv["x"], dh, dx, sv["g"], sv["shift"], sv["scale"])
        grads["norm_g"][i] = dg
        dmods[i] = jnp.concatenate([dshift, dscale, dgate], axis=1)
    return loss, dx, jnp.concatenate(dmods, axis=0), grads


def kernel(x, c, positions, norm_g, mod_w, mod_b, final_norm_g, sg_w_in, sg_w_out, sg_ln_g, sg_ln_b, sg_w_spatial, sg_b_spatial, swa_w_in, swa_w_out, swa_sinks, rwkv_w_in, rwkv_w_out, rwkv_mu, rwkv_w0, rwkv_w_lora, rwkv_a0, rwkv_a_lora, rwkv_k_k, rwkv_k_a, rwkv_r_k, rwkv_gn_g, rwkv_gn_b, loss_target, m_norm_g, m_mod_w, m_mod_b, m_final_norm_g, m_sg_w_in, m_sg_w_out, m_sg_ln_g, m_sg_ln_b, m_sg_w_spatial, m_sg_b_spatial, m_swa_w_in, m_swa_w_out, m_swa_sinks, m_rwkv_w_in, m_rwkv_w_out, m_rwkv_mu, m_rwkv_w0, m_rwkv_w_lora, m_rwkv_a0, m_rwkv_a_lora, m_rwkv_k_k, m_rwkv_k_a, m_rwkv_r_k, m_rwkv_gn_g, m_rwkv_gn_b, v_norm_g, v_mod_w, v_mod_b, v_final_norm_g, v_sg_w_in, v_sg_w_out, v_sg_ln_g, v_sg_ln_b, v_sg_w_spatial, v_sg_b_spatial, v_swa_w_in, v_swa_w_out, v_swa_sinks, v_rwkv_w_in, v_rwkv_w_out, v_rwkv_mu, v_rwkv_w0, v_rwkv_w_lora, v_rwkv_a0, v_rwkv_a_lora, v_rwkv_k_k, v_rwkv_k_a, v_rwkv_r_k, v_rwkv_gn_g, v_rwkv_gn_b):
    weights = dict(norm_g=norm_g, mod_w=mod_w, mod_b=mod_b, final_norm_g=final_norm_g, sg_w_in=sg_w_in, sg_w_out=sg_w_out,
                   sg_ln_g=sg_ln_g, sg_ln_b=sg_ln_b, sg_w_spatial=sg_w_spatial, sg_b_spatial=sg_b_spatial, swa_w_in=swa_w_in,
                   swa_w_out=swa_w_out, swa_sinks=swa_sinks, rwkv_w_in=rwkv_w_in, rwkv_w_out=rwkv_w_out, rwkv_mu=rwkv_mu,
                   rwkv_w0=rwkv_w0, rwkv_w_lora=rwkv_w_lora, rwkv_a0=rwkv_a0, rwkv_a_lora=rwkv_a_lora, rwkv_k_k=rwkv_k_k,
                   rwkv_k_a=rwkv_k_a, rwkv_r_k=rwkv_r_k, rwkv_gn_g=rwkv_gn_g, rwkv_gn_b=rwkv_gn_b)
    mom_m = dict(norm_g=m_norm_g, mod_w=m_mod_w, mod_b=m_mod_b, final_norm_g=m_final_norm_g, sg_w_in=m_sg_w_in,
                 sg_w_out=m_sg_w_out, sg_ln_g=m_sg_ln_g, sg_ln_b=m_sg_ln_b, sg_w_spatial=m_sg_w_spatial,
                 sg_b_spatial=m_sg_b_spatial, swa_w_in=m_swa_w_in, swa_w_out=m_swa_w_out, swa_sinks=m_swa_sinks,
                 rwkv_w_in=m_rwkv_w_in, rwkv_w_out=m_rwkv_w_out, rwkv_mu=m_rwkv_mu, rwkv_w0=m_rwkv_w0,
                 rwkv_w_lora=m_rwkv_w_lora, rwkv_a0=m_rwkv_a0, rwkv_a_lora=m_rwkv_a_lora, rwkv_k_k=m_rwkv_k_k,
                 rwkv_k_a=m_rwkv_k_a, rwkv_r_k=m_rwkv_r_k, rwkv_gn_g=m_rwkv_gn_g, rwkv_gn_b=m_rwkv_gn_b)
    mom_v = dict(norm_g=v_norm_g, mod_w=v_mod_w, mod_b=v_mod_b, final_norm_g=v_final_norm_g, sg_w_in=v_sg_w_in,
                 sg_w_out=v_sg_w_out, sg_ln_g=v_sg_ln_g, sg_ln_b=v_sg_ln_b, sg_w_spatial=v_sg_w_spatial,
                 sg_b_spatial=v_sg_b_spatial, swa_w_in=v_swa_w_in, swa_w_out=v_swa_w_out, swa_sinks=v_swa_sinks,
                 rwkv_w_in=v_rwkv_w_in, rwkv_w_out=v_rwkv_w_out, rwkv_mu=v_rwkv_mu, rwkv_w0=v_rwkv_w0,
                 rwkv_w_lora=v_rwkv_w_lora, rwkv_a0=v_rwkv_a0, rwkv_a_lora=v_rwkv_a_lora, rwkv_k_k=v_rwkv_k_k,
                 rwkv_k_a=v_rwkv_k_a, rwkv_r_k=v_rwkv_r_k, rwkv_gn_g=v_rwkv_gn_g, rwkv_gn_b=v_rwkv_gn_b)
    names = list(weights)
    t, d = x.shape[1], x.shape[2]
    me = 4 * lax.axis_index("x") + 2 * lax.axis_index("y") + lax.axis_index("c")
    n_mod = mod_w.shape[2]
    n_rw = rwkv_w_in.shape[2]

    small_names = ["sg_ln_g", "sg_ln_b", "rwkv_mu", "rwkv_w0", "rwkv_a0", "rwkv_k_k", "rwkv_k_a", "rwkv_gn_g", "rwkv_gn_b",
                   "rwkv_w_lora", "rwkv_a_lora"]
    small_shapes = [weights[n].shape for n in small_names]
    pk = _pack([c] + [weights[n] for n in small_names])
    gathered = _exchange("gather_small", pk, False).reshape(N_DEV, -1)
    c_all = gathered[:, :d]
    per_dev = [_unpack(gathered[dv, d:], small_shapes) for dv in range(N_DEV)]
    full_small = {}
    for q, n in enumerate(small_names):
        full_small[n] = jnp.concatenate([per_dev[dv][q] for dv in range(N_DEV)], axis=-1)

    mod_b_cols = lax.dynamic_slice_in_dim(mod_b, me * n_mod, n_mod, axis=1).reshape(DEPTH, 1, n_mod)
    mod_part = _mod_fwd("mod_fwd", c_all, mod_w, mod_b_cols)
    mod_g = _exchange("gather_mod", mod_part.reshape(DEPTH * N_DEV, n_mod), False)
    mod_g = mod_g.reshape(N_DEV, DEPTH, N_DEV, n_mod)
    mods = lax.dynamic_index_in_dim(mod_g, me, axis=2, keepdims=False)
    mods = mods.transpose(1, 0, 2).reshape(DEPTH, N_DEV * n_mod)

    def gather_big(name, w2d):
        return _exchange(name, w2d.astype(BF16), False)

    g_sg_in = gather_big("gather_sg_in", sg_w_in.reshape(-1, sg_w_in.shape[2])).reshape(N_DEV, 2, d, -1)
    g_sg_out = gather_big("gather_sg_out", sg_w_out.reshape(-1, d)).reshape(N_DEV, 2, -1, d)
    g_swa_in = gather_big("gather_swa_in", swa_w_in[0])
    g_swa_out = gather_big("gather_swa_out", swa_w_out[0])
    g_rw_in = gather_big("gather_rw_in", rwkv_w_in[0])
    g_rw_out = gather_big("gather_rw_out", rwkv_w_out[0])
    rw_in_full = _cols_full(g_rw_in)
    lora_rows = lambda w, off: jnp.zeros((LORA_PAD, d), F32).at[off:off + LORA].set(w)
    mu = full_small["rwkv_mu"].reshape(1, -1)
    heads = lambda a: a.reshape(-1, 1, HEAD)
    sg = dict(w_in=[_cols_full(g_sg_in[:, j]) for j in range(2)], w_out=[g_sg_out[:, j].reshape(d, d) for j in range(2)],
              ln_g=[full_small["sg_ln_g"][j].reshape(1, d) for j in range(2)],
              ln_b=[full_small["sg_ln_b"][j].reshape(1, d) for j in range(2)],
              w_s=[sg_w_spatial[j] for j in range(2)], b_st=[sg_b_spatial[j].T for j in range(2)])
    swa = dict(w_in=_cols_full(g_swa_in), w_out=g_swa_out.reshape(d, d), sinks=swa_sinks.reshape(SWA_KV, SWA_REP, 1, 1))
    rw = dict(w_main=rw_in_full[:, :4 * d], w_lorain=jnp.pad(rw_in_full[:, 4 * d:], ((0, 0), (0, LORA_PAD - 2 * LORA))),
              w_out=g_rw_out.reshape(d, d), mu_main=mu[:, :4 * d], mu_lora=jnp.pad(mu[:, 4 * d:], ((0, 0), (0, LORA_PAD - 2 * LORA))),
              w0=full_small["rwkv_w0"], a0=full_small["rwkv_a0"],
              w_lora=lora_rows(full_small["rwkv_w_lora"][0], 0), a_lora=lora_rows(full_small["rwkv_a_lora"][0], LORA),
              hp=[heads(full_small["rwkv_k_k"]), heads(full_small["rwkv_k_a"]), heads(rwkv_r_k), heads(full_small["rwkv_gn_g"]),
                  heads(full_small["rwkv_gn_b"])])

    loss, dx, dmods, g = _local_step(x[0], positions.reshape(t, 1).astype(F32), loss_target[0], mods, norm_g, final_norm_g,
                                     sg, swa, rw)

    dmod_g = _exchange("gather_dmod", dmods, False)
    dmod_all = dmod_g.transpose(1, 0, 2)
    dmod_cols = lax.dynamic_slice_in_dim(dmod_all, me * n_mod, n_mod, axis=2)
    g_mod_w, g_mod_b = _mod_bwd("mod_bwd", c_all, dmod_cols, dmod_all)

    d_b_sp = [g["sg_b_st"][j].T for j in range(2)]
    rep = [loss[0, :1], jnp.concatenate(g["norm_g"], axis=0), g["final_norm_g"], jnp.stack(g["sg_w_s"]), jnp.stack(d_b_sp),
           g["swa_sinks"], g["rw_hp"][2]]
    rep_shapes = [(1,), norm_g.shape, final_norm_g.shape, sg_w_spatial.shape, sg_b_spatial.shape, swa_sinks.shape, rwkv_r_k.shape]
    rep_sum = _sum_parts("sum_rep", _exchange("gather_rep", _pack(rep, 128 * 256), False)).reshape(-1)
    loss_tot, g_norm_g, g_final, g_w_sp, g_b_sp, g_sinks, g_r_k = _unpack(rep_sum, rep_shapes)

    def scatter_big(name, parts):
        return _exchange(name, parts.astype(BF16), True)

    p_sg_in = scatter_big("scatter_sg_in", jnp.concatenate([_cols_parts(g["sg_w_in"][j]) for j in range(2)], axis=1))
    p_sg_out = scatter_big("scatter_sg_out", jnp.concatenate([g["sg_w_out"][j].reshape(N_DEV, -1, d) for j in range(2)], axis=1))
    p_swa_in = scatter_big("scatter_swa_in", _cols_parts(g["swa_w_in"]))
    p_swa_out = scatter_big("scatter_swa_out", g["swa_w_out"].reshape(N_DEV, -1, d))
    d_rw_in = jnp.concatenate([g["rw_w_main"], g["rw_w_lorain"][:, :2 * LORA]], axis=1)
    p_rw_in = scatter_big("scatter_rw_in", _cols_parts(d_rw_in))
    p_rw_out = scatter_big("scatter_rw_out", g["rw_w_out"].reshape(N_DEV, -1, d))
    d_mu = jnp.concatenate([g["rw_mu_main"], g["rw_mu_lora"][:, :2 * LORA]], axis=1)
    hp_flat = lambda a: a.reshape(1, -1)
    small_grads = dict(sg_ln_g=jnp.concatenate(g["sg_ln_g"], axis=0), sg_ln_b=jnp.concatenate(g["sg_ln_b"], axis=0), rwkv_mu=d_mu,
                       rwkv_w0=g["rw_w0"], rwkv_a0=g["rw_a0"], rwkv_k_k=hp_flat(g["rw_hp"][0]), rwkv_k_a=hp_flat(g["rw_hp"][1]),
                       rwkv_gn_g=hp_flat(g["rw_hp"][3]), rwkv_gn_b=hp_flat(g["rw_hp"][4]),
                       rwkv_w_lora=g["rw_w_lora"][None, :LORA], rwkv_a_lora=g["rw_a_lora"][None, LORA:2 * LORA])
    per_dest = []
    for dv in range(N_DEV):
        shards = []
        for n in small_names:
            full, w = small_grads[n], weights[n].shape[-1]
            shards.append(full[..., dv * w:(dv + 1) * w])
        per_dest.append(_pack(shards))
    small_parts = _exchange("scatter_small", jnp.stack(per_dest), True)

    out_g, out_d, out_m, out_v = {}, {}, {}, {}

    def update(name, grad, shape2d):
        w2, m2, v2 = (a[name].reshape(shape2d) for a in (weights, mom_m, mom_v))
        gg, dd, mm, vv = _adamw("adamw_" + name, w2, grad, m2, v2)
        shp = weights[name].shape
        out_g[name], out_d[name], out_m[name], out_v[name] = gg.reshape(shp), dd.reshape(shp), mm.reshape(shp), vv.reshape(shp)

    update("mod_w", g_mod_w.reshape(-1, n_mod), (-1, n_mod))
    update("sg_w_in", p_sg_in, (-1, sg_w_in.shape[2]))
    update("sg_w_out", p_sg_out, (-1, d))
    update("swa_w_in", p_swa_in, (-1, swa_w_in.shape[2]))
    update("swa_w_out", p_swa_out, (-1, d))
    update("rwkv_w_in", p_rw_in, (-1, n_rw))
    update("rwkv_w_out", p_rw_out, (-1, d))
    update("sg_w_spatial", g_w_sp.reshape(-1, 128), (-1, 128))
    w_pk, m_pk, v_pk = (_pack([a[n] for n in small_names]) for a in (weights, mom_m, mom_v))
    res = _adamw("adamw_small", w_pk, small_parts, m_pk, v_pk)
    for q, arrs in enumerate(zip(*[_unpack(r_.reshape(-1), small_shapes) for r_ in res])):
        out_g[small_names[q]], out_d[small_names[q]], out_m[small_names[q]], out_v[small_names[q]] = arrs
    rep_names = ["norm_g", "mod_b", "final_norm_g", "sg_b_spatial", "swa_sinks", "rwkv_r_k"]
    rep_grads = [g_norm_g, g_mod_b.reshape(mod_b.shape), g_final, g_b_sp, g_sinks, g_r_k]
    rep_shapes2 = [weights[n].shape for n in rep_names]
    w_pk, m_pk, v_pk = (_pack([a[n] for n in rep_names]) for a in (weights, mom_m, mom_v))
    res = _adamw("adamw_rep", w_pk, _pack(rep_grads), m_pk, v_pk)
    for q, arrs in enumerate(zip(*[_unpack(r_.reshape(-1), rep_shapes2) for r_ in res])):
        out_g[rep_names[q]], out_d[rep_names[q]], out_m[rep_names[q]], out_v[rep_names[q]] = arrs

    return (loss_tot.reshape(()), dx[None], *[out_g[n] for n in names], *[out_d[n] for n in names],
            *[out_m[n] for n in names], *[out_v[n] for n in names])
```

```python
import functools
import math

import jax
import jax.numpy as jnp
from jax import lax
from jax.experimental import pallas as pl
from jax.experimental.pallas import tpu as pltpu

F32, BF16 = jnp.float32, jnp.bfloat16
HI = lax.Precision.HIGHEST
S = jax.ShapeDtypeStruct
MESH = pl.DeviceIdType.MESH

N_DEV = 8
DEPTH = 4
HEAD = 64
SG_GROUPS = 16
SG_CHUNK = 128
SWA_BLOCK = 128
SWA_KV = 4
SWA_REP = 8
ROPE_THETA = 10000.0
LORA = 96
LORA_PAD = 256
RW_CHUNK = 64
RW_HEADS = 8
RW_PREC = lax.Precision.HIGH
DECAY_SCALE = math.exp(-0.5)
GN_EPS = 64e-5
RMS_EPS = 1e-6
LN_EPS = 1e-5
NEG = -1e30
ADAM_LR, ADAM_B1, ADAM_B2, ADAM_EPS, ADAM_WD, ADAM_STEP = 0.001, 0.9, 0.999, 1e-08, 0.01, 10
VMEM_MB = 56


def _params(sem=None):
    kw = dict(vmem_limit_bytes=VMEM_MB << 20)
    if sem is not None:
        kw["dimension_semantics"] = sem
    return pltpu.CompilerParams(**kw)


def _pick(n, opts):
    for o in opts:
        if n % o == 0:
            return o
    raise ValueError(f"no tile for {n}")


def _rows(name, fn, rows, consts, out_rows, out_accs, tm):
    t = rows[0].shape[0]
    nr, nc, no = len(rows), len(consts), len(out_rows)

    def body(*refs):
        outs = fn(*[r[...] for r in refs[:nr + nc]])
        if not isinstance(outs, (tuple, list)):
            outs = (outs,)
        for r, o in zip(refs[nr + nc:nr + nc + no], outs[:no]):
            r[...] = o.astype(r.dtype)
        i = pl.program_id(0)
        for r, o in zip(refs[nr + nc + no:], outs[no:]):
            @pl.when(i == 0)
            def _(r=r, o=o):
                r[...] = o.astype(r.dtype)

            @pl.when(i > 0)
            def _(r=r, o=o):
                r[...] += o.astype(r.dtype)

    in_specs = [pl.BlockSpec((tm, a.shape[1]), lambda i: (i, 0)) for a in rows]
    in_specs += [pl.BlockSpec(c.shape, lambda i, nd=c.ndim: (0,) * nd) for c in consts]
    out_specs = [pl.BlockSpec((tm, n), lambda i: (i, 0)) for n, _ in out_rows]
    out_specs += [pl.BlockSpec(s, lambda i, nd=len(s): (0,) * nd) for s in out_accs]
    out_shape = [S((t, n), dt) for n, dt in out_rows] + [S(s, F32) for s in out_accs]
    res = pl.pallas_call(body, grid=(t // tm,), in_specs=in_specs, out_specs=out_specs, out_shape=out_shape,
                         name=name, compiler_params=_params(("arbitrary",)))(*rows, *consts)
    return res


_DN = {"nn": (((1,), (0,)), ((), ())), "nt": (((1,), (1,)), ((), ())), "tn": (((0,), (0,)), ((), ()))}


def _mm(name, a, b, mode, out_dtype, add=None):
    if mode == "nn":
        (m, k), (_, n) = a.shape, b.shape
    elif mode == "nt":
        (m, k), (n, _) = a.shape, b.shape
    else:
        (k, m), (_, n) = a.shape, b.shape
    tm, tn, tk = _pick(m, (512, 256, 128)), _pick(n, (512, 384, 256, 128)), _pick(k, (2048, 1536, 1024, 512, 384, 256, 128))
    nk = k // tk
    has_add = add is not None

    def body(*refs):
        a_ref, b_ref = refs[0], refs[1]
        o_ref, acc = refs[-2], refs[-1]
        kk = pl.program_id(2)
        prod = lax.dot_general(a_ref[...].astype(BF16), b_ref[...].astype(BF16), _DN[mode], preferred_element_type=F32)
        if nk == 1:
            o_ref[...] = (prod + refs[2][...].astype(F32) if has_add else prod).astype(o_ref.dtype)
            return

        @pl.when(kk == 0)
        def _():
            acc[...] = prod + refs[2][...].astype(F32) if has_add else prod

        @pl.when(kk > 0)
        def _():
            acc[...] += prod

        @pl.when(kk == nk - 1)
        def _():
            o_ref[...] = acc[...].astype(o_ref.dtype)

    a_spec = pl.BlockSpec((tk, tm), lambda i, j, q: (q, i)) if mode == "tn" else pl.BlockSpec((tm, tk), lambda i, j, q: (i, q))
    b_spec = pl.BlockSpec((tn, tk), lambda i, j, q: (j, q)) if mode == "nt" else pl.BlockSpec((tk, tn), lambda i, j, q: (q, j))
    o_spec = pl.BlockSpec((tm, tn), lambda i, j, q: (i, j))
    ins, specs = [a, b], [a_spec, b_spec]
    if has_add:
        ins.append(add)
        specs.append(o_spec)
    return pl.pallas_call(body, grid=(m // tm, n // tn, nk), in_specs=specs, out_specs=o_spec,
                          out_shape=S((m, n), out_dtype), scratch_shapes=[pltpu.VMEM((tm, tn), F32)], name=name,
                          compiler_params=_params(("parallel", "parallel", "arbitrary")))(*ins)


def _exchange(name, src, scatter):
    blk = src.shape[1:] if scatter else src.shape

    def body(src_ref, dst_ref, send_sems, recv_sems, loc_sem):
        x, y, c = lax.axis_index("x"), lax.axis_index("y"), lax.axis_index("c")
        me = 4 * x + 2 * y + c

        def mine(d):
            return src_ref.at[d] if scatter else src_ref

        local = pltpu.make_async_copy(mine(me), dst_ref.at[me], loc_sem)
        local.start()
        sends, peers = [], []
        for k in range(1, N_DEV):
            px = 1 - x if k & 4 else x
            py = 1 - y if k & 2 else y
            pc = 1 - c if k & 1 else c
            pid = 4 * px + 2 * py + pc
            cp = pltpu.make_async_remote_copy(src_ref=mine(pid), dst_ref=dst_ref.at[me], send_sem=send_sems.at[k - 1],
                                              recv_sem=recv_sems.at[k - 1], device_id=(px, py, pc), device_id_type=MESH)
            cp.start()
            sends.append(cp)
            peers.append((pid, (px, py, pc)))
        for k in range(1, N_DEV):
            pid, dev = peers[k - 1]
            pltpu.make_async_remote_copy(src_ref=mine(pid), dst_ref=dst_ref.at[pid], send_sem=send_sems.at[k - 1],
                                         recv_sem=recv_sems.at[k - 1], device_id=dev, device_id_type=MESH).wait_recv()
        for cp in sends:
            cp.wait_send()
        local.wait()

    return pl.pallas_call(
        body, out_shape=S((N_DEV,) + tuple(blk), src.dtype),
        in_specs=[pl.BlockSpec(memory_space=pl.ANY)], out_specs=pl.BlockSpec(memory_space=pl.ANY),
        scratch_shapes=[pltpu.SemaphoreType.DMA((N_DEV - 1,)), pltpu.SemaphoreType.DMA((N_DEV - 1,)),
                        pltpu.SemaphoreType.DMA],
        name=name)(src)


def _sum_parts(name, parts):
    _, r, c = parts.shape
    tm = _pick(r, (512, 256, 128, 64, 32, 16, 8)) if r % 8 == 0 else r

    def body(p_ref, o_ref):
        acc = p_ref[0].astype(F32)
        for d in range(1, N_DEV):
            acc = acc + p_ref[d].astype(F32)
        o_ref[...] = acc

    return pl.pallas_call(body, grid=(r // tm,), in_specs=[pl.BlockSpec((N_DEV, tm, c), lambda i: (0, i, 0))],
                          out_specs=pl.BlockSpec((tm, c), lambda i: (i, 0)), out_shape=S((r, c), F32), name=name,
                          compiler_params=_params(("parallel",)))(parts)


def _adamw(name, w, g, m, v):
    r, c = w.shape
    parts = g.ndim == 3
    tm = _pick(r, (256, 128, 64, 32, 16, 8)) if r % 8 == 0 else r

    def body(w_ref, g_ref, m_ref, v_ref, go_ref, d_ref, mo_ref, vo_ref):
        if parts:
            gg = g_ref[0].astype(F32)
            for d in range(1, N_DEV):
                gg = gg + g_ref[d].astype(F32)
        else:
            gg = g_ref[...]
        mm = ADAM_B1 * m_ref[...] + (1.0 - ADAM_B1) * gg
        vv = ADAM_B2 * v_ref[...] + (1.0 - ADAM_B2) * jnp.square(gg)
        m_hat = mm / (1.0 - ADAM_B1 ** ADAM_STEP)
        v_hat = vv / (1.0 - ADAM_B2 ** ADAM_STEP)
        go_ref[...] = gg
        d_ref[...] = -ADAM_LR * (m_hat / (jnp.sqrt(v_hat) + ADAM_EPS) + ADAM_WD * w_ref[...])
        mo_ref[...] = mm
        vo_ref[...] = vv

    spec = pl.BlockSpec((tm, c), lambda i: (i, 0))
    g_spec = pl.BlockSpec((N_DEV, tm, c), lambda i: (0, i, 0)) if parts else spec
    return pl.pallas_call(body, grid=(r // tm,), in_specs=[spec, g_spec, spec, spec], out_specs=[spec] * 4,
                          out_shape=[S((r, c), F32)] * 4, name=name, compiler_params=_params(("parallel",)))(w, g, m, v)


def _rms(x, g):
    return x * lax.rsqrt(jnp.mean(x * x, axis=-1, keepdims=True) + RMS_EPS) * g


def _adaln(x, g, shift, scale):
    return _rms(x, g) * (1.0 + scale) + shift


def _dot(a, b, dn="nn", hi=False, prec=None):
    if hi or prec is not None:
        return lax.dot_general(a, b, _DN[dn], precision=HI if hi else prec, preferred_element_type=F32)
    return lax.dot_general(a.astype(BF16), b.astype(BF16), _DN[dn], preferred_element_type=F32)


def _sg_mix(p, ln_g, ln_b, w_s, b_st):
    d = p.shape[1] // 3
    gd = d // SG_GROUPS
    u = jax.nn.gelu(p[:, :d])
    vf = jax.nn.gelu(p[:, d:2 * d])
    z = p[:, 2 * d:]
    mean = jnp.mean(vf, axis=-1, keepdims=True)
    var = jnp.mean(jnp.square(vf - mean), axis=-1, keepdims=True)
    vn = (vf - mean) * lax.rsqrt(var + LN_EPS) * ln_g + ln_b
    row = lax.broadcasted_iota(jnp.int32, (SG_CHUNK, SG_CHUNK), 0)
    col = lax.broadcasted_iota(jnp.int32, (SG_CHUNK, SG_CHUNK), 1)
    fs = []
    for g in range(SG_GROUPS):
        w = jnp.where(row >= col, w_s[g], 0.0)
        fs.append(_dot(w, vn[:, g * gd:(g + 1) * gd]))
    sel = (lax.broadcasted_iota(jnp.int32, (SG_GROUPS, d), 1) // gd
           == lax.broadcasted_iota(jnp.int32, (SG_GROUPS, d), 0)).astype(F32)
    f = jnp.concatenate(fs, axis=1) + _dot(b_st, sel, hi=True)
    return u * f * jax.nn.silu(z)


def _rot_half(x):
    n = x.shape[1]
    lane = lax.broadcasted_iota(jnp.int32, x.shape, 1)
    return jnp.where(lane % HEAD < HEAD // 2, -pltpu.roll(x, n - HEAD // 2, 1), pltpu.roll(x, HEAD // 2, 1))


def _rope(x, cos, sin, sign):
    reps = x.shape[1] // cos.shape[1]
    return x * jnp.tile(cos, (1, reps)) + sign * _rot_half(x) * jnp.tile(sin, (1, reps))


def _attn_block(q, kp, kc, vp, vc, sink, prev_bias):
    r = q.shape[0]
    q2 = q.reshape(r * SWA_BLOCK, HEAD)
    sp = (_dot(q2, kp, "nt") * (HEAD ** -0.5)).reshape(r, SWA_BLOCK, SWA_BLOCK)
    sc = (_dot(q2, kc, "nt") * (HEAD ** -0.5)).reshape(r, SWA_BLOCK, SWA_BLOCK)
    qi = lax.broadcasted_iota(jnp.int32, (r, SWA_BLOCK, SWA_BLOCK), 1)
    kj = lax.broadcasted_iota(jnp.int32, (r, SWA_BLOCK, SWA_BLOCK), 2)
    sp = jnp.where(kj > qi, sp, NEG) + prev_bias
    sc = jnp.where(kj <= qi, sc, NEG)
    m = jnp.maximum(jnp.maximum(jnp.max(sp, axis=-1, keepdims=True), jnp.max(sc, axis=-1, keepdims=True)), sink)
    ep, ec = jnp.exp(sp - m), jnp.exp(sc - m)
    denom = jnp.sum(ep, axis=-1, keepdims=True) + jnp.sum(ec, axis=-1, keepdims=True) + jnp.exp(sink - m)
    pp = (ep / denom).reshape(r * SWA_BLOCK, SWA_BLOCK)
    pc = (ec / denom).reshape(r * SWA_BLOCK, SWA_BLOCK)
    return (_dot(pp, vp) + _dot(pc, vc)).reshape(r, SWA_BLOCK, HEAD)


def _rwkv_chunk(s0, r, k, v, logw, a, k_k, k_a, r_k, gn_g, gn_b):
    c = r[0].shape[0]
    each = lambda f, *ls: [f(*xs) for xs in zip(*ls)]
    dot = functools.partial(_dot, prec=RW_PREC)
    row = lax.broadcasted_iota(jnp.int32, (c, c), 0)
    col = lax.broadcasted_iota(jnp.int32, (c, c), 1)
    incl, strict = row >= col, row > col
    ones_l = incl.astype(F32)

    def unit(x):
        return x / jnp.maximum(jnp.sqrt(jnp.sum(x * x, axis=-1, keepdims=True)), 1e-12)

    kk = each(lambda k_, p: unit(k_ * p), k, k_k)
    km = each(lambda k_, a_, p: k_ * (1.0 + (a_ - 1.0) * p), k, a, k_a)
    b = each(lambda x, a_: x * a_, kk, a)
    cum = each(lambda w: _dot(ones_l, w, hi=True), logw)
    alpha = each(lambda x, cu, w: x * jnp.exp(cu - w), kk, cum, logw)
    beta = each(lambda x, cu: x * jnp.exp(-cu), b, cum)
    kap = each(lambda x, cu: x * jnp.exp(-cu), km, cum)
    rho = each(lambda x, cu: x * jnp.exp(cu), r, cum)
    lab = each(lambda x, y_: jnp.where(strict, dot(x, y_, "nt"), 0.0), alpha, beta)
    lak = each(lambda x, y_: jnp.where(strict, dot(x, y_, "nt"), 0.0), alpha, kap)
    xs = each(lambda al, s, l, v_: dot(al, s, "nt") + dot(l, v_), alpha, s0, lak, v)
    xs = each(lambda x, l: x - dot(l, x), xs, lab)
    lp, power = lab, 2
    while power < c:
        lp = each(lambda l: dot(l, l), lp)
        xs = each(lambda x, l: x + dot(l, x), xs, lp)
        power *= 2
    u = each(lambda x: -x, xs)
    mrb = each(lambda x, y_: jnp.where(incl, dot(x, y_, "nt"), 0.0), rho, beta)
    mrk = each(lambda x, y_: jnp.where(incl, dot(x, y_, "nt"), 0.0), rho, kap)
    y = each(lambda rh, s, mb, u_, mk, v_: dot(rh, s, "nt") + dot(mb, u_) + dot(mk, v_), rho, s0, mrb, u, mrk, v)
    s1 = each(lambda s, u_, be, v_, ka, w: (s + dot(u_, be, "tn") + dot(v_, ka, "tn")) * jnp.exp(jnp.sum(w, axis=0, keepdims=True)),
              s0, u, beta, v, kap, logw)

    def finish(y_, g, bias, r_, km_, rk, v_):
        mean = jnp.mean(y_, axis=-1, keepdims=True)
        var = jnp.mean(jnp.square(y_ - mean), axis=-1, keepdims=True)
        y_ = (y_ - mean) * lax.rsqrt(var + GN_EPS) * g + bias
        return y_ + jnp.sum(r_ * km_ * rk, axis=-1, keepdims=True) * v_

    return each(finish, y, gn_g, gn_b, r, km, r_k, v), s1


def _norm_fwd(name, x, g, shift, scale):
    return _rows(name, lambda x_, g_, sh, sc: _adaln(x_, g_, sh, sc), [x], [g, shift, scale], [(x.shape[1], BF16)], [], 256)[0]


def _norm_bwd(name, x, dh, dx_res, g, shift, scale):
    d = x.shape[1]

    def fn(x_, dh_, dr_, g_, sh, sc):
        _, vjp = jax.vjp(_adaln, x_, g_, sh, sc)
        dx, dg, dsh, dsc = vjp(dh_)
        return dx + dr_, dg, dsh, dsc

    return _rows(name, fn, [x, dh, dx_res], [g, shift, scale], [(d, F32)], [(1, d)] * 3, 256)


def _resid_fwd(name, x, y, gate):
    return _rows(name, lambda x_, y_, g_: x_ + g_ * y_, [x, y], [gate], [(x.shape[1], F32)], [], 256)[0]


def _resid_bwd(name, dx, y, gate):
    d = dx.shape[1]
    return _rows(name, lambda dx_, y_, g_: (g_ * dx_, jnp.sum(dx_ * y_, axis=0, keepdims=True)), [dx, y], [gate],
                 [(d, BF16)], [(1, d)], 256)


def _sg_fwd(name, p, ln_g, ln_b, w_s, b_st):
    d = p.shape[1] // 3
    return _rows(name, _sg_mix, [p], [ln_g, ln_b, w_s, b_st], [(d, BF16)], [], SG_CHUNK)[0]


def _sg_bwd(name, p, dmix, ln_g, ln_b, w_s, b_st):
    def fn(p_, dm_, lg, lb, ws, bs):
        _, vjp = jax.vjp(_sg_mix, p_, lg, lb, ws, bs)
        return vjp(dm_)

    return _rows(name, fn, [p, dmix], [ln_g, ln_b, w_s, b_st], [(p.shape[1], BF16)],
                 [ln_g.shape, ln_b.shape, w_s.shape, b_st.shape], SG_CHUNK)


def _rope_tables(pos, inv_freq):
    ang = pos * inv_freq
    return jnp.cos(ang), jnp.sin(ang)


def _swa_pre(name, p, pos, inv_freq, d):
    kvw = SWA_KV * HEAD

    def fn(p_, pos_, fr):
        cos, sin = _rope_tables(pos_, fr)
        return (_rope(p_[:, :d], cos, sin, 1.0), _rope(p_[:, d:d + kvw], cos, sin, 1.0), p_[:, d + kvw:d + 2 * kvw])

    return _rows(name, fn, [p, pos], [inv_freq], [(d, BF16), (kvw, BF16), (kvw, BF16)], [], 256)


def _swa_attn_fwd(name, q, k, v, sinks):
    kv, r, t, _ = q.shape
    nb = t // SWA_BLOCK

    def body(q_ref, kp_ref, kc_ref, vp_ref, vc_ref, s_ref, o_ref):
        prev_bias = jnp.where(pl.program_id(1) > 0, 0.0, NEG).astype(F32)
        o_ref[0] = _attn_block(q_ref[0], kp_ref[0], kc_ref[0], vp_ref[0], vc_ref[0], s_ref[0], prev_bias)

    qs = pl.BlockSpec((1, r, SWA_BLOCK, HEAD), lambda g, n: (g, 0, n, 0))
    cur = pl.BlockSpec((1, SWA_BLOCK, HEAD), lambda g, n: (g, n, 0))
    prev = pl.BlockSpec((1, SWA_BLOCK, HEAD), lambda g, n: (g, jnp.maximum(n - 1, 0), 0))
    ss = pl.BlockSpec((1, r, 1, 1), lambda g, n: (g, 0, 0, 0))
    return pl.pallas_call(body, grid=(kv, nb), in_specs=[qs, prev, cur, prev, cur, ss], out_specs=qs,
                          out_shape=S(q.shape, F32), name=name,
                          compiler_params=_params(("parallel", "arbitrary")))(q, k, k, v, v, sinks)


def _swa_attn_bwd(name, q, k, v, sinks, do):
    kv, r, t, _ = q.shape
    nb = t // SWA_BLOCK

    def body(q_ref, kp_ref, kc_ref, vp_ref, vc_ref, s_ref, do_ref, dq_ref, dkc_ref, dkp_ref, dvc_ref, dvp_ref, ds_ref):
        n = pl.program_id(1)
        prev_bias = jnp.where(n > 0, 0.0, NEG).astype(F32)
        fn = functools.partial(_attn_block, prev_bias=prev_bias)
        args = [ref[0].astype(F32) for ref in (q_ref, kp_ref, kc_ref, vp_ref, vc_ref)] + [s_ref[0]]
        _, vjp = jax.vjp(fn, *args)
        dq, dkp, dkc, dvp, dvc, ds = vjp(do_ref[0])
        dq_ref[0], dkc_ref[0], dkp_ref[0], dvc_ref[0], dvp_ref[0] = dq, dkc, dkp, dvc, dvp

        @pl.when(n == 0)
        def _():
            ds_ref[0] = ds

        @pl.when(n > 0)
        def _():
            ds_ref[0] += ds

    qs = pl.BlockSpec((1, r, SWA_BLOCK, HEAD), lambda g, n: (g, 0, n, 0))
    cur = pl.BlockSpec((1, SWA_BLOCK, HEAD), lambda g, n: (g, n, 0))
    prev = pl.BlockSpec((1, SWA_BLOCK, HEAD), lambda g, n: (g, jnp.maximum(n - 1, 0), 0))
    ss = pl.BlockSpec((1, r, 1, 1), lambda g, n: (g, 0, 0, 0))
    return pl.pallas_call(
        body, grid=(kv, nb), in_specs=[qs, prev, cur, prev, cur, ss, qs], out_specs=[qs, cur, cur, cur, cur, ss],
        out_shape=[S(q.shape, F32)] + [S(k.shape, F32)] * 4 + [S(sinks.shape, F32)], name=name,
        compiler_params=_params(("parallel", "arbitrary")))(q, k, k, v, v, sinks, do)


def _gate_fwd(name, o, z_src, z_off, d):
    return _rows(name, lambda o_, p_: o_ * jax.nn.silu(p_[:, z_off:z_off + d]), [o, z_src], [], [(d, BF16)], [], 256)[0]


def _gate_bwd(name, o, z_src, z_off, d, dmix):
    def fn(o_, p_, dm_):
        _, vjp = jax.vjp(lambda oo, zz: oo * jax.nn.silu(zz), o_, p_[:, z_off:z_off + d])
        return vjp(dm_)

    return _rows(name, fn, [o, z_src, dmix], [], [(d, F32), (d, F32)], [], 256)


def _swa_post_bwd(name, dq, dkc, dkp_up, dvc, dvp_up, dz, pos, inv_freq):
    def fn(dq_, dkc_, dkp_, dvc_, dvp_, dz_, pos_, fr):
        cos, sin = _rope_tables(pos_, fr)
        return jnp.concatenate([_rope(dq_, cos, sin, -1.0), _rope(dkc_ + dkp_, cos, sin, -1.0), dvc_ + dvp_, dz_], axis=1)

    n = dq.shape[1] + dkc.shape[1] + dvc.shape[1] + dz.shape[1]
    return _rows(name, fn, [dq, dkc, dkp_up, dvc, dvp_up, dz, pos], [inv_freq], [(n, BF16)], [], 256)[0]


def _lerp_fwd(name, p, ps, mu, widths):
    def fn(p_, ps_, mu_):
        pm = p_ + (ps_ - p_) * mu_
        outs, o = [], 0
        for w in widths:
            outs.append(pm[:, o:o + w])
            o += w
        return tuple(outs)

    return _rows(name, fn, [p, ps], [mu], [(w, F32) for w in widths], [], 128)


def _lerp_bwd(name, dpm_parts, p, ps, mu):
    n = p.shape[1]
    k = len(dpm_parts)

    def fn(*a):
        dpm = jnp.concatenate(a[:k], axis=1) if k > 1 else a[0]
        p_, ps_, mu_ = a[k:]
        return dpm * (1.0 - mu_), dpm * mu_, jnp.sum(dpm * (ps_ - p_), axis=0, keepdims=True)

    return _rows(name, fn, list(dpm_parts) + [p, ps], [mu], [(n, F32), (n, F32)], [(1, n)], 64)


def _shift_add(name, a, b_up, dtype):
    return _rows(name, lambda x_, y_: x_ + y_, [a, b_up], [], [(a.shape[1], dtype)], [], 128)[0]


def _lora_act(pl_, w0, w_lora, a0, a_lora):
    logw = -DECAY_SCALE * jax.nn.sigmoid(w0 + _dot(jnp.tanh(pl_), w_lora))
    a = jax.nn.sigmoid(a0 + _dot(pl_, a_lora))
    return logw, a


def _lora_fwd(name, pl_, w0, w_lora, a0, a_lora):
    d = w0.shape[1]
    return _rows(name, _lora_act, [pl_], [w0, w_lora, a0, a_lora], [(d, F32), (d, F32)], [], 256)


def _lora_bwd(name, pl_, dlogw, da, w0, w_lora, a0, a_lora):
    def fn(p_, dl_, da_, w0_, wl_, a0_, al_):
        _, vjp = jax.vjp(_lora_act, p_, w0_, wl_, a0_, al_)
        return vjp((dl_, da_))

    return _rows(name, fn, [pl_, dlogw, da], [w0, w_lora, a0, a_lora], [(pl_.shape[1], F32)],
                 [w0.shape, w_lora.shape, a0.shape, a_lora.shape], 256)


def _rwkv_scan_fwd(name, r, k, v, logw, a, hp):
    h, t, _ = r.shape
    nc = t // RW_CHUNK

    hb = RW_HEADS

    def body(r_ref, k_ref, v_ref, w_ref, a_ref, kk_ref, ka_ref, rk_ref, gg_ref, gb_ref, y_ref, st_ref, s_scr):
        @pl.when(pl.program_id(1) == 0)
        def _():
            s_scr[...] = jnp.zeros_like(s_scr)

        heads = lambda ref: [ref[i] for i in range(hb)]
        s0 = heads(s_scr)
        for i in range(hb):
            st_ref[i, 0] = s0[i]
        y, s1 = _rwkv_chunk(s0, *[heads(ref) for ref in (r_ref, k_ref, v_ref, w_ref, a_ref, kk_ref, ka_ref, rk_ref, gg_ref, gb_ref)])
        for i in range(hb):
            y_ref[i] = y[i]
            s_scr[i] = s1[i]

    seq = pl.BlockSpec((hb, RW_CHUNK, HEAD), lambda i, n: (i, n, 0))
    par = pl.BlockSpec((hb, 1, HEAD), lambda i, n: (i, 0, 0))
    st = pl.BlockSpec((hb, 1, HEAD, HEAD), lambda i, n: (i, n, 0, 0))
    return pl.pallas_call(body, grid=(h // hb, nc), in_specs=[seq] * 5 + [par] * 5, out_specs=[seq, st],
                          out_shape=[S((h, t, HEAD), F32), S((h, nc, HEAD, HEAD), F32)],
                          scratch_shapes=[pltpu.VMEM((hb, HEAD, HEAD), F32)], name=name,
                          compiler_params=_params(("parallel", "arbitrary")))(r, k, v, logw, a, *hp)


def _rwkv_scan_bwd(name, r, k, v, logw, a, hp, states, dy):
    h, t, _ = r.shape
    nc = t // RW_CHUNK

    def body(r_ref, k_ref, v_ref, w_ref, a_ref, kk_ref, ka_ref, rk_ref, gg_ref, gb_ref, st_ref, dy_ref,
             dr_ref, dk_ref, dv_ref, dw_ref, da_ref, dkk_ref, dka_ref, drk_ref, dgg_ref, dgb_ref, ds_scr):
        n = pl.program_id(1)

        @pl.when(n == 0)
        def _():
            ds_scr[...] = jnp.zeros_like(ds_scr)
            for ref in (dkk_ref, dka_ref, drk_ref, dgg_ref, dgb_ref):
                ref[...] = jnp.zeros_like(ref)

        heads = lambda ref: [ref[i] for i in range(hb)]
        ins = [[st_ref[i, 0] for i in range(hb)]] + [heads(ref) for ref in (r_ref, k_ref, v_ref, w_ref, a_ref, kk_ref, ka_ref,
                                                                           rk_ref, gg_ref, gb_ref)]
        _, vjp = jax.vjp(_rwkv_chunk, *ins)
        ds0, *dseq, dkk, dka, drk, dgg, dgb = vjp((heads(dy_ref), heads(ds_scr)))
        for i in range(hb):
            ds_scr[i] = ds0[i]
            for ref, val in zip((dr_ref, dk_ref, dv_ref, dw_ref, da_ref), dseq):
                ref[i] = val[i]
            for ref, val in ((dkk_ref, dkk), (dka_ref, dka), (drk_ref, drk), (dgg_ref, dgg), (dgb_ref, dgb)):
                ref[i] += val[i]

    hb = RW_HEADS
    seq = pl.BlockSpec((hb, RW_CHUNK, HEAD), lambda i, n: (i, nc - 1 - n, 0))
    par = pl.BlockSpec((hb, 1, HEAD), lambda i, n: (i, 0, 0))
    st = pl.BlockSpec((hb, 1, HEAD, HEAD), lambda i, n: (i, nc - 1 - n, 0, 0))
    return pl.pallas_call(body, grid=(h // hb, nc), in_specs=[seq] * 5 + [par] * 5 + [st, seq], out_specs=[seq] * 5 + [par] * 5,
                          out_shape=[S((h, t, HEAD), F32)] * 5 + [S((h, 1, HEAD), F32)] * 5,
                          scratch_shapes=[pltpu.VMEM((hb, HEAD, HEAD), F32)], name=name,
                          compiler_params=_params(("parallel", "arbitrary")))(r, k, v, logw, a, *hp, states, dy)


def _loss_head(name, x, target, g):
    d = x.shape[1]

    def fn(x_, t_, g_):
        def f(xx, gg):
            err = _rms(xx, gg) - t_
            return 0.5 * jnp.sum(jnp.mean(err * err, axis=-1, keepdims=True), axis=0, keepdims=True)

        l, vjp = jax.vjp(f, x_, g_)
        dx, dg = vjp(jnp.ones((1, 1), F32))
        return dx, dg, jnp.broadcast_to(l, (1, 128))

    return _rows(name, fn, [x, target], [g], [(d, F32)], [(1, d), (1, 128)], 256)


def _mod_fwd(name, cond_all, mod_w, mod_b_cols):
    l, d, n = mod_w.shape

    def body(c_ref, w_ref, b_ref, o_ref):
        o_ref[0] = _dot(jax.nn.silu(c_ref[...]), w_ref[0], hi=True) + b_ref[0]

    return pl.pallas_call(body, grid=(l,), in_specs=[pl.BlockSpec((N_DEV, d), lambda i: (0, 0)),
                                                      pl.BlockSpec((1, d, n), lambda i: (i, 0, 0)),
                                                      pl.BlockSpec((1, 1, n), lambda i: (i, 0, 0))],
                          out_specs=pl.BlockSpec((1, N_DEV, n), lambda i: (i, 0, 0)), out_shape=S((l, N_DEV, n), F32),
                          name=name, compiler_params=_params(("parallel",)))(cond_all, mod_w, mod_b_cols)


def _mod_bwd(name, cond_all, dmod_cols, dmod_all):
    l, _, n = dmod_cols.shape
    d = cond_all.shape[1]
    nb = dmod_all.shape[2]

    def body(c_ref, dc_ref, da_ref, gw_ref, gb_ref):
        gw_ref[0] = _dot(jax.nn.silu(c_ref[...]), dc_ref[0], "tn", hi=True)
        acc = da_ref[0, 0:1, :]
        for bi in range(1, N_DEV):
            acc = acc + da_ref[0, bi:bi + 1, :]
        gb_ref[0] = acc

    return pl.pallas_call(body, grid=(l,), in_specs=[pl.BlockSpec((N_DEV, d), lambda i: (0, 0)),
                                                      pl.BlockSpec((1, N_DEV, n), lambda i: (i, 0, 0)),
                                                      pl.BlockSpec((1, N_DEV, nb), lambda i: (i, 0, 0))],
                          out_specs=[pl.BlockSpec((1, d, n), lambda i: (i, 0, 0)), pl.BlockSpec((1, 1, nb), lambda i: (i, 0, 0))],
                          out_shape=[S((l, d, n), F32), S((l, 1, nb), F32)], name=name,
                          compiler_params=_params(("parallel",)))(cond_all, dmod_cols, dmod_all)


def _to_heads(a):
    t, n = a.shape
    return a.reshape(t, n // HEAD, HEAD).transpose(1, 0, 2)


def _from_heads(a):
    h, t, _ = a.shape
    return a.transpose(1, 0, 2).reshape(t, h * HEAD)


def _shift_down(a):
    return jnp.concatenate([jnp.zeros_like(a[:1]), a[:-1]], axis=0)


def _shift_up(a, n=1):
    return jnp.concatenate([a[n:], jnp.zeros_like(a[:n])], axis=0)


def _cols_full(g):
    return g.transpose(1, 0, 2).reshape(g.shape[1], -1)


def _cols_parts(full):
    r, n = full.shape
    return full.reshape(r, N_DEV, n // N_DEV).transpose(1, 0, 2)


def _pack(arrs, mult=1024):
    flat = jnp.concatenate([a.reshape(-1) for a in arrs])
    pad = (-flat.shape[0]) % mult
    return jnp.pad(flat, (0, pad)).reshape(-1, 128)


def _unpack(flat, shapes):
    out, o = [], 0
    for s in shapes:
        n = math.prod(s)
        out.append(flat[o:o + n].reshape(s))
        o += n
    return out


def _local_step(x, pos, target, mods, norm_g, final_norm_g, sg, swa, rw):
    t, d = x.shape
    kinds = [i % 3 for i in range(DEPTH)]
    inv_freq = (ROPE_THETA ** (-jnp.arange(HEAD // 2, dtype=F32) / (HEAD // 2)))
    inv_freq = jnp.tile(inv_freq, 128 // (HEAD // 2)).reshape(1, 128)
    saved = []
    for i, kind in enumerate(kinds):
        j = i // 3
        shift, scale, gate = (mods[i, q * d:(q + 1) * d].reshape(1, d) for q in range(3))
        g = norm_g[i].reshape(1, d)
        h = _norm_fwd(f"norm_fwd{i}", x, g, shift, scale)
        sv = dict(x=x, h=h, g=g, shift=shift, scale=scale, gate=gate)
        if kind == 0:
            p = _mm(f"sg_in{i}", h, sg["w_in"][j], "nn", F32)
            mix = _sg_fwd(f"sg_mix{i}", p, sg["ln_g"][j], sg["ln_b"][j], sg["w_s"][j], sg["b_st"][j])
            w_out = sg["w_out"][j]
            sv.update(p=p)
        elif kind == 1:
            p = _mm(f"swa_in{i}", h, swa["w_in"], "nn", F32)
            q, k, v = _swa_pre(f"swa_pre{i}", p, pos, inv_freq, d)
            qh = _to_heads(q).reshape(SWA_KV, SWA_REP, t, HEAD)
            kh, vh = _to_heads(k), _to_heads(v)
            o = _swa_attn_fwd(f"swa_attn{i}", qh, kh, vh, swa["sinks"])
            o = _from_heads(o.reshape(SWA_KV * SWA_REP, t, HEAD))
            mix = _gate_fwd(f"swa_gate{i}", o, p, d + 2 * SWA_KV * HEAD, d)
            w_out = swa["w_out"]
            sv.update(p=p, qh=qh, kh=kh, vh=vh, o=o)
        else:
            pm = _mm(f"rw_in{i}", h, rw["w_main"], "nn", F32)
            plo = _mm(f"rw_inl{i}", h, rw["w_lorain"], "nn", F32)
            pms, plos = _shift_down(pm), _shift_down(plo)
            r, k, v, z = _lerp_fwd(f"rw_lerp{i}", pm, pms, rw["mu_main"], [d] * 4)
            (pll,) = _lerp_fwd(f"rw_lerpl{i}", plo, plos, rw["mu_lora"], [LORA_PAD])
            logw, a = _lora_fwd(f"rw_lora{i}", pll, rw["w0"], rw["w_lora"], rw["a0"], rw["a_lora"])
            hs = [_to_heads(u) for u in (r, k, v, logw, a)]
            yh, states = _rwkv_scan_fwd(f"rw_scan{i}", *hs, rw["hp"])
            o = _from_heads(yh)
            mix = _gate_fwd(f"rw_gate{i}", o, z, 0, d)
            w_out = rw["w_out"]
            sv.update(pm=pm, pms=pms, plo=plo, plos=plos, pll=pll, z=z, hs=hs, states=states, o=o)
        y = _mm(f"out{i}", mix, w_out, "nn", F32)
        sv.update(mix=mix, y=y)
        saved.append(sv)
        x = _resid_fwd(f"resid{i}", x, y, gate)

    dx, d_final_g, loss = _loss_head("loss_head", x, target, final_norm_g.reshape(1, d))

    grads = dict(norm_g=[None] * DEPTH, sg_w_in=[None] * 2, sg_w_out=[None] * 2, sg_ln_g=[None] * 2, sg_ln_b=[None] * 2,
                 sg_w_s=[None] * 2, sg_b_st=[None] * 2, final_norm_g=d_final_g)
    dmods = [None] * DEPTH
    for i in reversed(range(DEPTH)):
        kind, j, sv = kinds[i], i // 3, saved[i]
        dy, dgate = _resid_bwd(f"resid_bwd{i}", dx, sv["y"], sv["gate"])
        w_out = (sg["w_out"][j], swa["w_out"], rw["w_out"])[kind]
        d_w_out = _mm(f"out_dw{i}", sv["mix"], dy, "tn", F32)
        dmix = _mm(f"out_dx{i}", dy, w_out, "nt", F32)
        if kind == 0:
            dp, dlg, dlb, dws, dbs = _sg_bwd(f"sg_mix_bwd{i}", sv["p"], dmix, sg["ln_g"][j], sg["ln_b"][j],
                                            sg["w_s"][j], sg["b_st"][j])
            grads["sg_ln_g"][j], grads["sg_ln_b"][j], grads["sg_w_s"][j], grads["sg_b_st"][j] = dlg, dlb, dws, dbs
            grads["sg_w_out"][j] = d_w_out
            grads["sg_w_in"][j] = _mm(f"sg_in_dw{i}", sv["h"], dp, "tn", F32)
            dh = _mm(f"sg_in_dx{i}", dp, sg["w_in"][j], "nt", F32)
        elif kind == 1:
            z_off = d + 2 * SWA_KV * HEAD
            do, dz = _gate_bwd(f"swa_gate_bwd{i}", sv["o"], sv["p"], z_off, d, dmix)
            doh = _to_heads(do).reshape(SWA_KV, SWA_REP, t, HEAD)
            dq, dkc, dkp, dvc, dvp, dsinks = _swa_attn_bwd(f"swa_attn_bwd{i}", sv["qh"], sv["kh"], sv["vh"], swa["sinks"], doh)
            dq = _from_heads(dq.reshape(SWA_KV * SWA_REP, t, HEAD))
            dkc, dvc = _from_heads(dkc), _from_heads(dvc)
            dkp, dvp = _shift_up(_from_heads(dkp), SWA_BLOCK), _shift_up(_from_heads(dvp), SWA_BLOCK)
            dp = _swa_post_bwd(f"swa_post_bwd{i}", dq, dkc, dkp, dvc, dvp, dz, pos, inv_freq)
            grads.update(swa_sinks=dsinks, swa_w_out=d_w_out)
            grads["swa_w_in"] = _mm(f"swa_in_dw{i}", sv["h"], dp, "tn", F32)
            dh = _mm(f"swa_in_dx{i}", dp, swa["w_in"], "nt", F32)
        else:
            do, dz = _gate_bwd(f"rw_gate_bwd{i}", sv["o"], sv["z"], 0, d, dmix)
            res = _rwkv_scan_bwd(f"rw_scan_bwd{i}", *sv["hs"], rw["hp"], sv["states"], _to_heads(do))
            dr, dk, dv, dlogw, da = (_from_heads(u) for u in res[:5])
            dpll, dw0, dwl, da0, dal = _lora_bwd(f"rw_lora_bwd{i}", sv["pll"], dlogw, da, rw["w0"], rw["w_lora"],
                                                  rw["a0"], rw["a_lora"])
            dpm, dpms, dmu_main = _lerp_bwd(f"rw_lerp_bwd{i}", [dr, dk, dv, dz], sv["pm"], sv["pms"], rw["mu_main"])
            dpl, dpls, dmu_lora = _lerp_bwd(f"rw_lerpl_bwd{i}", [dpll], sv["plo"], sv["plos"], rw["mu_lora"])
            dpm = _shift_add(f"rw_shift_add{i}", dpm, _shift_up(dpms), BF16)
            dpl = _shift_add(f"rw_shift_addl{i}", dpl, _shift_up(dpls), BF16)
            grads.update(rw_w_out=d_w_out, rw_hp=res[5:], rw_w0=dw0, rw_w_lora=dwl, rw_a0=da0, rw_a_lora=dal,
                         rw_mu_main=dmu_main, rw_mu_lora=dmu_lora)
            grads["rw_w_main"] = _mm(f"rw_in_dw{i}", sv["h"], dpm, "tn", F32)
            grads["rw_w_lorain"] = _mm(f"rw_inl_dw{i}", sv["h"], dpl, "tn", F32)
            dh = _mm(f"rw_inl_dx{i}", dpl, rw["w_lorain"], "nt", F32)
            dh = _mm(f"rw_in_dx{i}", dpm, rw["w_main"], "nt", F32, add=dh)
        dx, dg, dshift, dscale = _norm_bwd(f"norm_bwd{i}", sv["x"], dh, dx, sv["g"], sv["shift"], sv["scale"])
        grads["norm_g"][i] = dg
        dmods[i] = jnp.concatenate([dshift, dscale, dgate], axis=1)
    return loss, dx, jnp.concatenate(dmods, axis=0), grads


def kernel(x, c, positions, norm_g, mod_w, mod_b, final_norm_g, sg_w_in, sg_w_out, sg_ln_g, sg_ln_b, sg_w_spatial, sg_b_spatial, swa_w_in, swa_w_out, swa_sinks, rwkv_w_in, rwkv_w_out, rwkv_mu, rwkv_w0, rwkv_w_lora, rwkv_a0, rwkv_a_lora, rwkv_k_k, rwkv_k_a, rwkv_r_k, rwkv_gn_g, rwkv_gn_b, loss_target, m_norm_g, m_mod_w, m_mod_b, m_final_norm_g, m_sg_w_in, m_sg_w_out, m_sg_ln_g, m_sg_ln_b, m_sg_w_spatial, m_sg_b_spatial, m_swa_w_in, m_swa_w_out, m_swa_sinks, m_rwkv_w_in, m_rwkv_w_out, m_rwkv_mu, m_rwkv_w0, m_rwkv_w_lora, m_rwkv_a0, m_rwkv_a_lora, m_rwkv_k_k, m_rwkv_k_a, m_rwkv_r_k, m_rwkv_gn_g, m_rwkv_gn_b, v_norm_g, v_mod_w, v_mod_b, v_final_norm_g, v_sg_w_in, v_sg_w_out, v_sg_ln_g, v_sg_ln_b, v_sg_w_spatial, v_sg_b_spatial, v_swa_w_in, v_swa_w_out, v_swa_sinks, v_rwkv_w_in, v_rwkv_w_out, v_rwkv_mu, v_rwkv_w0, v_rwkv_w_lora, v_rwkv_a0, v_rwkv_a_lora, v_rwkv_k_k, v_rwkv_k_a, v_rwkv_r_k, v_rwkv_gn_g, v_rwkv_gn_b):
    weights = dict(norm_g=norm_g, mod_w=mod_w, mod_b=mod_b, final_norm_g=final_norm_g, sg_w_in=sg_w_in, sg_w_out=sg_w_out,
                   sg_ln_g=sg_ln_g, sg_ln_b=sg_ln_b, sg_w_spatial=sg_w_spatial, sg_b_spatial=sg_b_spatial, swa_w_in=swa_w_in,
                   swa_w_out=swa_w_out, swa_sinks=swa_sinks, rwkv_w_in=rwkv_w_in, rwkv_w_out=rwkv_w_out, rwkv_mu=rwkv_mu,
                   rwkv_w0=rwkv_w0, rwkv_w_lora=rwkv_w_lora, rwkv_a0=rwkv_a0, rwkv_a_lora=rwkv_a_lora, rwkv_k_k=rwkv_k_k,
                   rwkv_k_a=rwkv_k_a, rwkv_r_k=rwkv_r_k, rwkv_gn_g=rwkv_gn_g, rwkv_gn_b=rwkv_gn_b)
    mom_m = dict(norm_g=m_norm_g, mod_w=m_mod_w, mod_b=m_mod_b, final_norm_g=m_final_norm_g, sg_w_in=m_sg_w_in,
                 sg_w_out=m_sg_w_out, sg_ln_g=m_sg_ln_g, sg_ln_b=m_sg_ln_b, sg_w_spatial=m_sg_w_spatial,
                 sg_b_spatial=m_sg_b_spatial, swa_w_in=m_swa_w_in, swa_w_out=m_swa_w_out, swa_sinks=m_swa_sinks,
                 rwkv_w_in=m_rwkv_w_in, rwkv_w_out=m_rwkv_w_out, rwkv_mu=m_rwkv_mu, rwkv_w0=m_rwkv_w0,
                 rwkv_w_lora=m_rwkv_w_lora, rwkv_a0=m_rwkv_a0, rwkv_a_lora=m_rwkv_a_lora, rwkv_k_k=m_rwkv_k_k,
                 rwkv_k_a=m_rwkv_k_a, rwkv_r_k=m_rwkv_r_k, rwkv_gn_g=m_rwkv_gn_g, rwkv_gn_b=m_rwkv_gn_b)
    mom_v = dict(norm_g=v_norm_g, mod_w=v_mod_w, mod_b=v_mod_b, final_norm_g=v_final_norm_g, sg_w_in=v_sg_w_in,
                 sg_w_out=v_sg_w_out, sg_ln_g=v_sg_ln_g, sg_ln_b=v_sg_ln_b, sg_w_spatial=v_sg_w_spatial,
                 sg_b_spatial=v_sg_b_spatial, swa_w_in=v_swa_w_in, swa_w_out=v_swa_w_out, swa_sinks=v_swa_sinks,
                 rwkv_w_in=v_rwkv_w_in, rwkv_w_out=v_rwkv_w_out, rwkv_mu=v_rwkv_mu, rwkv_w0=v_rwkv_w0,
                 rwkv_w_lora=v_rwkv_w_lora, rwkv_a0=v_rwkv_a0, rwkv_a_lora=v_rwkv_a_lora, rwkv_k_k=v_rwkv_k_k,
                 rwkv_k_a=v_rwkv_k_a, rwkv_r_k=v_rwkv_r_k, rwkv_gn_g=v_rwkv_gn_g, rwkv_gn_b=v_rwkv_gn_b)
    names = list(weights)
    t, d = x.shape[1], x.shape[2]
    me = 4 * lax.axis_index("x") + 2 * lax.axis_index("y") + lax.axis_index("c")
    n_mod = mod_w.shape[2]
    n_rw = rwkv_w_in.shape[2]

    small_names = ["sg_ln_g", "sg_ln_b", "rwkv_mu", "rwkv_w0", "rwkv_a0", "rwkv_k_k", "rwkv_k_a", "rwkv_gn_g", "rwkv_gn_b",
                   "rwkv_w_lora", "rwkv_a_lora"]
    small_shapes = [weights[n].shape for n in small_names]
    pk = _pack([c] + [weights[n] for n in small_names])
    gathered = _exchange("gather_small", pk, False).reshape(N_DEV, -1)
    c_all = gathered[:, :d]
    per_dev = [_unpack(gathered[dv, d:], small_shapes) for dv in range(N_DEV)]
    full_small = {}
    for q, n in enumerate(small_names):
        full_small[n] = jnp.concatenate([per_dev[dv][q] for dv in range(N_DEV)], axis=-1)

    mod_b_cols = lax.dynamic_slice_in_dim(mod_b, me * n_mod, n_mod, axis=1).reshape(DEPTH, 1, n_mod)
    mod_part = _mod_fwd("mod_fwd", c_all, mod_w, mod_b_cols)
    mod_g = _exchange("gather_mod", mod_part.reshape(DEPTH * N_DEV, n_mod), False)
    mod_g = mod_g.reshape(N_DEV, DEPTH, N_DEV, n_mod)
    mods = lax.dynamic_index_in_dim(mod_g, me, axis=2, keepdims=False)
    mods = mods.transpose(1, 0, 2).reshape(DEPTH, N_DEV * n_mod)

    def gather_big(name, w2d):
        return _exchange(name, w2d.astype(BF16), False)

    g_sg_in = gather_big("gather_sg_in", sg_w_in.reshape(-1, sg_w_in.shape[2])).reshape(N_DEV, 2, d, -1)
    g_sg_out = gather_big("gather_sg_out", sg_w_out.reshape(-1, d)).reshape(N_DEV, 2, -1, d)
    g_swa_in = gather_big("gather_swa_in", swa_w_in[0])
    g_swa_out = gather_big("gather_swa_out", swa_w_out[0])
    g_rw_in = gather_big("gather_rw_in", rwkv_w_in[0])
    g_rw_out = gather_big("gather_rw_out", rwkv_w_out[0])
    rw_in_full = _cols_full(g_rw_in)
    lora_rows = lambda w, off: jnp.zeros((LORA_PAD, d), F32).at[off:off + LORA].set(w)
    mu = full_small["rwkv_mu"].reshape(1, -1)
    heads = lambda a: a.reshape(-1, 1, HEAD)
    sg = dict(w_in=[_cols_full(g_sg_in[:, j]) for j in range(2)], w_out=[g_sg_out[:, j].reshape(d, d) for j in range(2)],
              ln_g=[full_small["sg_ln_g"][j].reshape(1, d) for j in range(2)],
              ln_b=[full_small["sg_ln_b"][j].reshape(1, d) for j in range(2)],
              w_s=[sg_w_spatial[j] for j in range(2)], b_st=[sg_b_spatial[j].T for j in range(2)])
    swa = dict(w_in=_cols_full(g_swa_in), w_out=g_swa_out.reshape(d, d), sinks=swa_sinks.reshape(SWA_KV, SWA_REP, 1, 1))
    rw = dict(w_main=rw_in_full[:, :4 * d], w_lorain=jnp.pad(rw_in_full[:, 4 * d:], ((0, 0), (0, LORA_PAD - 2 * LORA))),
              w_out=g_rw_out.reshape(d, d), mu_main=mu[:, :4 * d], mu_lora=jnp.pad(mu[:, 4 * d:], ((0, 0), (0, LORA_PAD - 2 * LORA))),
              w0=full_small["rwkv_w0"], a0=full_small["rwkv_a0"],
              w_lora=lora_rows(full_small["rwkv_w_lora"][0], 0), a_lora=lora_rows(full_small["rwkv_a_lora"][0], LORA),
              hp=[heads(full_small["rwkv_k_k"]), heads(full_small["rwkv_k_a"]), heads(rwkv_r_k), heads(full_small["rwkv_gn_g"]),
                  heads(full_small["rwkv_gn_b"])])

    loss, dx, dmods, g = _local_step(x[0], positions.reshape(t, 1).astype(F32), loss_target[0], mods, norm_g, final_norm_g,
                                     sg, swa, rw)

    dmod_g = _exchange("gather_dmod", dmods, False)
    dmod_all = dmod_g.transpose(1, 0, 2)
    dmod_cols = lax.dynamic_slice_in_dim(dmod_all, me * n_mod, n_mod, axis=2)
    g_mod_w, g_mod_b = _mod_bwd("mod_bwd", c_all, dmod_cols, dmod_all)

    d_b_sp = [g["sg_b_st"][j].T for j in range(2)]
    rep = [loss[0, :1], jnp.concatenate(g["norm_g"], axis=0), g["final_norm_g"], jnp.stack(g["sg_w_s"]), jnp.stack(d_b_sp),
           g["swa_sinks"], g["rw_hp"][2]]
    rep_shapes = [(1,), norm_g.shape, final_norm_g.shape, sg_w_spatial.shape, sg_b_spatial.shape, swa_sinks.shape, rwkv_r_k.shape]
    rep_sum = _sum_parts("sum_rep", _exchange("gather_rep", _pack(rep, 128 * 256), False)).reshape(-1)
    loss_tot, g_norm_g, g_final, g_w_sp, g_b_sp, g_sinks, g_r_k = _unpack(rep_sum, rep_shapes)

    def scatter_big(name, parts):
        return _exchange(name, parts.astype(BF16), True)

    p_sg_in = scatter_big("scatter_sg_in", jnp.concatenate([_cols_parts(g["sg_w_in"][j]) for j in range(2)], axis=1))
    p_sg_out = scatter_big("scatter_sg_out", jnp.concatenate([g["sg_w_out"][j].reshape(N_DEV, -1, d) for j in range(2)], axis=1))
    p_swa_in = scatter_big("scatter_swa_in", _cols_parts(g["swa_w_in"]))
    p_swa_out = scatter_big("scatter_swa_out", g["swa_w_out"].reshape(N_DEV, -1, d))
    d_rw_in = jnp.concatenate([g["rw_w_main"], g["rw_w_lorain"][:, :2 * LORA]], axis=1)
    p_rw_in = scatter_big("scatter_rw_in", _cols_parts(d_rw_in))
    p_rw_out = scatter_big("scatter_rw_out", g["rw_w_out"].reshape(N_DEV, -1, d))
    d_mu = jnp.concatenate([g["rw_mu_main"], g["rw_mu_lora"][:, :2 * LORA]], axis=1)
    hp_flat = lambda a: a.reshape(1, -1)
    small_grads = dict(sg_ln_g=jnp.concatenate(g["sg_ln_g"], axis=0), sg_ln_b=jnp.concatenate(g["sg_ln_b"], axis=0), rwkv_mu=d_mu,
                       rwkv_w0=g["rw_w0"], rwkv_a0=g["rw_a0"], rwkv_k_k=hp_flat(g["rw_hp"][0]), rwkv_k_a=hp_flat(g["rw_hp"][1]),
                       rwkv_gn_g=hp_flat(g["rw_hp"][3]), rwkv_gn_b=hp_flat(g["rw_hp"][4]),
                       rwkv_w_lora=g["rw_w_lora"][None, :LORA], rwkv_a_lora=g["rw_a_lora"][None, LORA:2 * LORA])
    per_dest = []
    for dv in range(N_DEV):
        shards = []
        for n in small_names:
            full, w = small_grads[n], weights[n].shape[-1]
            shards.append(full[..., dv * w:(dv + 1) * w])
        per_dest.append(_pack(shards))
    small_parts = _exchange("scatter_small", jnp.stack(per_dest), True)

    out_g, out_d, out_m, out_v = {}, {}, {}, {}

    def update(name, grad, shape2d):
        w2, m2, v2 = (a[name].reshape(shape2d) for a in (weights, mom_m, mom_v))
        gg, dd, mm, vv = _adamw("adamw_" + name, w2, grad, m2, v2)
        shp = weights[name].shape
        out_g[name], out_d[name], out_m[name], out_v[name] = gg.reshape(shp), dd.reshape(shp), mm.reshape(shp), vv.reshape(shp)

    update("mod_w", g_mod_w.reshape(-1, n_mod), (-1, n_mod))
    update("sg_w_in", p_sg_in, (-1, sg_w_in.shape[2]))
    update("sg_w_out", p_sg_out, (-1, d))
    update("swa_w_in", p_swa_in, (-1, swa_w_in.shape[2]))
    update("swa_w_out", p_swa_out, (-1, d))
    update("rwkv_w_in", p_rw_in, (-1, n_rw))
    update("rwkv_w_out", p_rw_out, (-1, d))
    update("sg_w_spatial", g_w_sp.reshape(-1, 128), (-1, 128))
    w_pk, m_pk, v_pk = (_pack([a[n] for n in small_names]) for a in (weights, mom_m, mom_v))
    res = _adamw("adamw_small", w_pk, small_parts, m_pk, v_pk)
    for q, arrs in enumerate(zip(*[_unpack(r_.reshape(-1), small_shapes) for r_ in res])):
        out_g[small_names[q]], out_d[small_names[q]], out_m[small_names[q]], out_v[small_names[q]] = arrs
    rep_names = ["norm_g", "mod_b", "final_norm_g", "sg_b_spatial", "swa_sinks", "rwkv_r_k"]
    rep_grads = [g_norm_g, g_mod_b.reshape(mod_b.shape), g_final, g_b_sp, g_sinks, g_r_k]
    rep_shapes2 = [weights[n].shape for n in rep_names]
    w_pk, m_pk, v_pk = (_pack([a[n] for n in rep_names]) for a in (weights, mom_m, mom_v))
    res = _adamw("adamw_rep", w_pk, _pack(rep_grads), m_pk, v_pk)
    for q, arrs in enumerate(zip(*[_unpack(r_.reshape(-1), rep_shapes2) for r_ in res])):
        out_g[rep_names[q]], out_d[rep_names[q]], out_m[rep_names[q]], out_v[rep_names[q]] = arrs

    return (loss_tot.reshape(()), dx[None], *[out_g[n] for n in names], *[out_d[n] for n in names],
            *[out_m[n] for n in names], *[out_v[n] for n in names])
```

```python
import functools
import math

import jax
import jax.numpy as jnp
from jax import lax
from jax.experimental import pallas as pl
from jax.experimental.pallas import tpu as pltpu

F32, BF16 = jnp.float32, jnp.bfloat16
HI = lax.Precision.HIGHEST
S = jax.ShapeDtypeStruct
MESH = pl.DeviceIdType.MESH

N_DEV = 8
DEPTH = 4
HEAD = 64
SG_GROUPS = 16
SG_CHUNK = 128
SWA_BLOCK = 128
SWA_KV = 4
SWA_REP = 8
ROPE_THETA = 10000.0
LORA = 96
LORA_PAD = 256
RW_CHUNK = 64
RW_HEADS = 8
RW_PREC = lax.Precision.HIGH
DECAY_SCALE = math.exp(-0.5)
GN_EPS = 64e-5
RMS_EPS = 1e-6
LN_EPS = 1e-5
NEG = -1e30
ADAM_LR, ADAM_B1, ADAM_B2, ADAM_EPS, ADAM_WD, ADAM_STEP = 0.001, 0.9, 0.999, 1e-08, 0.01, 10
VMEM_MB = 56


def _params(sem=None):
    kw = dict(vmem_limit_bytes=VMEM_MB << 20)
    if sem is not None:
        kw["dimension_semantics"] = sem
    return pltpu.CompilerParams(**kw)


def _pick(n, opts):
    for o in opts:
        if n % o == 0:
            return o
    raise ValueError(f"no tile for {n}")


def _rows(name, fn, rows, consts, out_rows, out_accs, tm):
    t = rows[0].shape[0]
    nr, nc, no = len(rows), len(consts), len(out_rows)

    def body(*refs):
        outs = fn(*[r[...] for r in refs[:nr + nc]])
        if not isinstance(outs, (tuple, list)):
            outs = (outs,)
        for r, o in zip(refs[nr + nc:nr + nc + no], outs[:no]):
            r[...] = o.astype(r.dtype)
        i = pl.program_id(0)
        for r, o in zip(refs[nr + nc + no:], outs[no:]):
            @pl.when(i == 0)
            def _(r=r, o=o):
                r[...] = o.astype(r.dtype)

            @pl.when(i > 0)
            def _(r=r, o=o):
                r[...] += o.astype(r.dtype)

    in_specs = [pl.BlockSpec((tm, a.shape[1]), lambda i: (i, 0)) for a in rows]
    in_specs += [pl.BlockSpec(c.shape, lambda i, nd=c.ndim: (0,) * nd) for c in consts]
    out_specs = [pl.BlockSpec((tm, n), lambda i: (i, 0)) for n, _ in out_rows]
    out_specs += [pl.BlockSpec(s, lambda i, nd=len(s): (0,) * nd) for s in out_accs]
    out_shape = [S((t, n), dt) for n, dt in out_rows] + [S(s, F32) for s in out_accs]
    res = pl.pallas_call(body, grid=(t // tm,), in_specs=in_specs, out_specs=out_specs, out_shape=out_shape,
                         name=name, compiler_params=_params(("arbitrary",)))(*rows, *consts)
    return res


_DN = {"nn": (((1,), (0,)), ((), ())), "nt": (((1,), (1,)), ((), ())), "tn": (((0,), (0,)), ((), ()))}


def _mm(name, a, b, mode, out_dtype, add=None):
    if mode == "nn":
        (m, k), (_, n) = a.shape, b.shape
    elif mode == "nt":
        (m, k), (n, _) = a.shape, b.shape
    else:
        (k, m), (_, n) = a.shape, b.shape
    tm, tn, tk = _pick(m, (512, 256, 128)), _pick(n, (1024, 512, 384, 256, 128)), _pick(k, (2048, 1536, 1024, 512, 384, 256, 128))
    nk = k // tk
    has_add = add is not None

    def body(*refs):
        a_ref, b_ref = refs[0], refs[1]
        o_ref, acc = refs[-2], refs[-1]
        kk = pl.program_id(2)
        prod = lax.dot_general(a_ref[...].astype(BF16), b_ref[...].astype(BF16), _DN[mode], preferred_element_type=F32)
        if nk == 1:
            o_ref[...] = (prod + refs[2][...].astype(F32) if has_add else prod).astype(o_ref.dtype)
            return

        @pl.when(kk == 0)
        def _():
            acc[...] = prod + refs[2][...].astype(F32) if has_add else prod

        @pl.when(kk > 0)
        def _():
            acc[...] += prod

        @pl.when(kk == nk - 1)
        def _():
            o_ref[...] = acc[...].astype(o_ref.dtype)

    a_spec = pl.BlockSpec((tk, tm), lambda i, j, q: (q, i)) if mode == "tn" else pl.BlockSpec((tm, tk), lambda i, j, q: (i, q))
    b_spec = pl.BlockSpec((tn, tk), lambda i, j, q: (j, q)) if mode == "nt" else pl.BlockSpec((tk, tn), lambda i, j, q: (q, j))
    o_spec = pl.BlockSpec((tm, tn), lambda i, j, q: (i, j))
    ins, specs = [a, b], [a_spec, b_spec]
    if has_add:
        ins.append(add)
        specs.append(o_spec)
    return pl.pallas_call(body, grid=(m // tm, n // tn, nk), in_specs=specs, out_specs=o_spec,
                          out_shape=S((m, n), out_dtype), scratch_shapes=[pltpu.VMEM((tm, tn), F32)], name=name,
                          compiler_params=_params(("parallel", "parallel", "arbitrary")))(*ins)


def _exchange(name, src, scatter):
    blk = src.shape[1:] if scatter else src.shape

    def body(src_ref, dst_ref, send_sems, recv_sems, loc_sem):
        x, y, c = lax.axis_index("x"), lax.axis_index("y"), lax.axis_index("c")
        me = 4 * x + 2 * y + c

        def mine(d):
            return src_ref.at[d] if scatter else src_ref

        local = pltpu.make_async_copy(mine(me), dst_ref.at[me], loc_sem)
        local.start()
        sends, peers = [], []
        for k in range(1, N_DEV):
            px = 1 - x if k & 4 else x
            py = 1 - y if k & 2 else y
            pc = 1 - c if k & 1 else c
            pid = 4 * px + 2 * py + pc
            cp = pltpu.make_async_remote_copy(src_ref=mine(pid), dst_ref=dst_ref.at[me], send_sem=send_sems.at[k - 1],
                                              recv_sem=recv_sems.at[k - 1], device_id=(px, py, pc), device_id_type=MESH)
            cp.start()
            sends.append(cp)
            peers.append((pid, (px, py, pc)))
        for k in range(1, N_DEV):
            pid, dev = peers[k - 1]
            pltpu.make_async_remote_copy(src_ref=mine(pid), dst_ref=dst_ref.at[pid], send_sem=send_sems.at[k - 1],
                                         recv_sem=recv_sems.at[k - 1], device_id=dev, device_id_type=MESH).wait_recv()
        for cp in sends:
            cp.wait_send()
        local.wait()

    return pl.pallas_call(
        body, out_shape=S((N_DEV,) + tuple(blk), src.dtype),
        in_specs=[pl.BlockSpec(memory_space=pl.ANY)], out_specs=pl.BlockSpec(memory_space=pl.ANY),
        scratch_shapes=[pltpu.SemaphoreType.DMA((N_DEV - 1,)), pltpu.SemaphoreType.DMA((N_DEV - 1,)),
                        pltpu.SemaphoreType.DMA],
        name=name)(src)


def _gather(name, src):
    def body(src_ref, dst_ref, send_sems, recv_sems, loc_sem):
        x, y, c = lax.axis_index("x"), lax.axis_index("y"), lax.axis_index("c")
        me, sibling = (x, y, c), (x, y, 1 - c)
        chips = [(1 - x, y), (x, 1 - y), (1 - x, 1 - y)]

        def rows(px, py, pc):
            return dst_ref.at[4 * px + 2 * py + pc]

        def copy(k, block, to, own=False):
            return pltpu.make_async_remote_copy(src_ref=src_ref if own else rows(*block), dst_ref=rows(*block),
                                                send_sem=send_sems.at[k], recv_sem=recv_sems.at[k], device_id=to,
                                                device_id_type=MESH)

        local = pltpu.make_async_copy(src_ref, rows(*me), loc_sem)
        local.start()
        first = [copy(0, me, sibling, own=True)] + [copy(1 + j, me, (*chip, c), own=True) for j, chip in enumerate(chips)]
        for cp in first:
            cp.start()
        passed = [copy(4 + j, (*chip, c), sibling) for j, chip in enumerate(chips)]
        for j, chip in enumerate(chips):
            copy(1 + j, (*chip, c), me).wait_recv()
            passed[j].start()
        copy(0, sibling, me).wait_recv()
        for j, chip in enumerate(chips):
            copy(4 + j, (*chip, 1 - c), me).wait_recv()
        for cp in first + passed:
            cp.wait_send()
        local.wait()

    return pl.pallas_call(
        body, out_shape=S((N_DEV,) + tuple(src.shape), src.dtype),
        in_specs=[pl.BlockSpec(memory_space=pl.ANY)], out_specs=pl.BlockSpec(memory_space=pl.ANY),
        scratch_shapes=[pltpu.SemaphoreType.DMA((N_DEV - 1,)), pltpu.SemaphoreType.DMA((N_DEV - 1,)),
                        pltpu.SemaphoreType.DMA],
        name=name)(src)


def _scatter(name, parts):
    _, r, c_ = parts.shape
    n_chip = N_DEV // 2

    def stage1(src_ref, dst_ref, send_sems, recv_sems):
        x, y, c = lax.axis_index("x"), lax.axis_index("y"), lax.axis_index("c")
        sends = []
        for q in range(n_chip):
            cp = pltpu.make_async_remote_copy(src_ref=src_ref.at[2 * q + 1 - c], dst_ref=dst_ref.at[q],
                                              send_sem=send_sems.at[q], recv_sem=recv_sems.at[q],
                                              device_id=(x, y, 1 - c), device_id_type=MESH)
            cp.start()
            sends.append(cp)
        for q in range(n_chip):
            pltpu.make_async_remote_copy(src_ref=src_ref.at[2 * q + c], dst_ref=dst_ref.at[q], send_sem=send_sems.at[q],
                                         recv_sem=recv_sems.at[q], device_id=(x, y, 1 - c), device_id_type=MESH).wait_recv()
        for cp in sends:
            cp.wait_send()

    from_sibling = pl.pallas_call(
        stage1, out_shape=S((n_chip, r, c_), parts.dtype),
        in_specs=[pl.BlockSpec(memory_space=pl.ANY)], out_specs=pl.BlockSpec(memory_space=pl.ANY),
        scratch_shapes=[pltpu.SemaphoreType.DMA((n_chip,)), pltpu.SemaphoreType.DMA((n_chip,))], name=name + "_pair")(parts)

    tm = _pick(r, (512, 256, 128, 64, 32, 16, 8)) if r % 8 == 0 else r
    core = lax.axis_index("c").astype(jnp.int32).reshape(1)

    def pair_sum(core_ref, mine_ref, sib_ref, o_ref):
        o_ref[...] = (mine_ref[0].astype(F32) + sib_ref[...].astype(F32)).astype(o_ref.dtype)

    pair = pl.pallas_call(
        pair_sum, out_shape=S((n_chip, r, c_), parts.dtype),
        grid_spec=pltpu.PrefetchScalarGridSpec(
            num_scalar_prefetch=1, grid=(n_chip, r // tm),
            in_specs=[pl.BlockSpec((1, 1, tm, c_), lambda q, i, core_ref: (q, core_ref[0], i, 0)),
                      pl.BlockSpec((1, tm, c_), lambda q, i, core_ref: (q, i, 0))],
            out_specs=pl.BlockSpec((1, tm, c_), lambda q, i, core_ref: (q, i, 0))),
        name=name + "_sum", compiler_params=_params(("parallel", "parallel")))(
            core, parts.reshape(n_chip, 2, r, c_), from_sibling)

    def stage2(src_ref, dst_ref, send_sems, recv_sems, loc_sem):
        x, y, c = lax.axis_index("x"), lax.axis_index("y"), lax.axis_index("c")
        mine = 2 * x + y
        local = pltpu.make_async_copy(src_ref.at[mine], dst_ref.at[mine], loc_sem)
        local.start()
        sends, chips = [], [(1 - x, y), (x, 1 - y), (1 - x, 1 - y)]
        for j, (px, py) in enumerate(chips):
            cp = pltpu.make_async_remote_copy(src_ref=src_ref.at[2 * px + py], dst_ref=dst_ref.at[mine],
                                              send_sem=send_sems.at[j], recv_sem=recv_sems.at[j],
                                              device_id=(px, py, c), device_id_type=MESH)
            cp.start()
            sends.append(cp)
        for j, (px, py) in enumerate(chips):
            pltpu.make_async_remote_copy(src_ref=src_ref.at[mine], dst_ref=dst_ref.at[2 * px + py], send_sem=send_sems.at[j],
                                         recv_sem=recv_sems.at[j], device_id=(px, py, c), device_id_type=MESH).wait_recv()
        for cp in sends:
            cp.wait_send()
        local.wait()

    return pl.pallas_call(
        stage2, out_shape=S((n_chip, r, c_), parts.dtype),
        in_specs=[pl.BlockSpec(memory_space=pl.ANY)], out_specs=pl.BlockSpec(memory_space=pl.ANY),
        scratch_shapes=[pltpu.SemaphoreType.DMA((n_chip - 1,)), pltpu.SemaphoreType.DMA((n_chip - 1,)),
                        pltpu.SemaphoreType.DMA], name=name + "_chips")(pair)


def _sum_parts(name, parts):
    n_parts, r, c = parts.shape
    tm = _pick(r, (512, 256, 128, 64, 32, 16, 8)) if r % 8 == 0 else r

    def body(p_ref, o_ref):
        acc = p_ref[0].astype(F32)
        for d in range(1, n_parts):
            acc = acc + p_ref[d].astype(F32)
        o_ref[...] = acc

    return pl.pallas_call(body, grid=(r // tm,), in_specs=[pl.BlockSpec((n_parts, tm, c), lambda i: (0, i, 0))],
                          out_specs=pl.BlockSpec((tm, c), lambda i: (i, 0)), out_shape=S((r, c), F32), name=name,
                          compiler_params=_params(("parallel",)))(parts)


def _adamw(name, w, g, m, v):
    r, c = w.shape
    parts = g.ndim == 3
    n_parts = g.shape[0] if parts else 1
    tm = _pick(r, (256, 128, 64, 32, 16, 8)) if r % 8 == 0 else r

    def body(w_ref, g_ref, m_ref, v_ref, go_ref, d_ref, mo_ref, vo_ref):
        if parts:
            gg = g_ref[0].astype(F32)
            for d in range(1, n_parts):
                gg = gg + g_ref[d].astype(F32)
        else:
            gg = g_ref[...]
        mm = ADAM_B1 * m_ref[...] + (1.0 - ADAM_B1) * gg
        vv = ADAM_B2 * v_ref[...] + (1.0 - ADAM_B2) * jnp.square(gg)
        m_hat = mm / (1.0 - ADAM_B1 ** ADAM_STEP)
        v_hat = vv / (1.0 - ADAM_B2 ** ADAM_STEP)
        go_ref[...] = gg
        d_ref[...] = -ADAM_LR * (m_hat / (jnp.sqrt(v_hat) + ADAM_EPS) + ADAM_WD * w_ref[...])
        mo_ref[...] = mm
        vo_ref[...] = vv

    spec = pl.BlockSpec((tm, c), lambda i: (i, 0))
    g_spec = pl.BlockSpec((n_parts, tm, c), lambda i: (0, i, 0)) if parts else spec
    return pl.pallas_call(body, grid=(r // tm,), in_specs=[spec, g_spec, spec, spec], out_specs=[spec] * 4,
                          out_shape=[S((r, c), F32)] * 4, name=name, compiler_params=_params(("parallel",)))(w, g, m, v)


def _rms(x, g):
    return x * lax.rsqrt(jnp.mean(x * x, axis=-1, keepdims=True) + RMS_EPS) * g


def _adaln(x, g, shift, scale):
    return _rms(x, g) * (1.0 + scale) + shift


def _dot(a, b, dn="nn", hi=False, prec=None):
    if hi or prec is not None:
        return lax.dot_general(a, b, _DN[dn], precision=HI if hi else prec, preferred_element_type=F32)
    return lax.dot_general(a.astype(BF16), b.astype(BF16), _DN[dn], preferred_element_type=F32)


def _sg_mix(p, ln_g, ln_b, w_s, b_st):
    d = p.shape[1] // 3
    gd = d // SG_GROUPS
    u = jax.nn.gelu(p[:, :d])
    vf = jax.nn.gelu(p[:, d:2 * d])
    z = p[:, 2 * d:]
    mean = jnp.mean(vf, axis=-1, keepdims=True)
    var = jnp.mean(jnp.square(vf - mean), axis=-1, keepdims=True)
    vn = (vf - mean) * lax.rsqrt(var + LN_EPS) * ln_g + ln_b
    row = lax.broadcasted_iota(jnp.int32, (SG_CHUNK, SG_CHUNK), 0)
    col = lax.broadcasted_iota(jnp.int32, (SG_CHUNK, SG_CHUNK), 1)
    fs = []
    for g in range(SG_GROUPS):
        w = jnp.where(row >= col, w_s[g], 0.0)
        fs.append(_dot(w, vn[:, g * gd:(g + 1) * gd]))
    sel = (lax.broadcasted_iota(jnp.int32, (SG_GROUPS, d), 1) // gd
           == lax.broadcasted_iota(jnp.int32, (SG_GROUPS, d), 0)).astype(F32)
    f = jnp.concatenate(fs, axis=1) + _dot(b_st, sel, hi=True)
    return u * f * jax.nn.silu(z)


def _rot_half(x):
    n = x.shape[1]
    lane = lax.broadcasted_iota(jnp.int32, x.shape, 1)
    return jnp.where(lane % HEAD < HEAD // 2, -pltpu.roll(x, n - HEAD // 2, 1), pltpu.roll(x, HEAD // 2, 1))


def _rope(x, cos, sin, sign):
    reps = x.shape[1] // cos.shape[1]
    return x * jnp.tile(cos, (1, reps)) + sign * _rot_half(x) * jnp.tile(sin, (1, reps))


def _attn_block(q, kp, kc, vp, vc, sink, prev_bias):
    r = q.shape[0]
    q2 = q.reshape(r * SWA_BLOCK, HEAD)
    sp = (_dot(q2, kp, "nt") * (HEAD ** -0.5)).reshape(r, SWA_BLOCK, SWA_BLOCK)
    sc = (_dot(q2, kc, "nt") * (HEAD ** -0.5)).reshape(r, SWA_BLOCK, SWA_BLOCK)
    qi = lax.broadcasted_iota(jnp.int32, (r, SWA_BLOCK, SWA_BLOCK), 1)
    kj = lax.broadcasted_iota(jnp.int32, (r, SWA_BLOCK, SWA_BLOCK), 2)
    sp = jnp.where(kj > qi, sp, NEG) + prev_bias
    sc = jnp.where(kj <= qi, sc, NEG)
    m = jnp.maximum(jnp.maximum(jnp.max(sp, axis=-1, keepdims=True), jnp.max(sc, axis=-1, keepdims=True)), sink)
    ep, ec = jnp.exp(sp - m), jnp.exp(sc - m)
    denom = jnp.sum(ep, axis=-1, keepdims=True) + jnp.sum(ec, axis=-1, keepdims=True) + jnp.exp(sink - m)
    pp = (ep / denom).reshape(r * SWA_BLOCK, SWA_BLOCK)
    pc = (ec / denom).reshape(r * SWA_BLOCK, SWA_BLOCK)
    return (_dot(pp, vp) + _dot(pc, vc)).reshape(r, SWA_BLOCK, HEAD)


def _rwkv_chunk(s0, r, k, v, logw, a, k_k, k_a, r_k, gn_g, gn_b):
    c = r[0].shape[0]
    each = lambda f, *ls: [f(*xs) for xs in zip(*ls)]
    gram = functools.partial(_dot, prec=RW_PREC)
    row = lax.broadcasted_iota(jnp.int32, (c, c), 0)
    col = lax.broadcasted_iota(jnp.int32, (c, c), 1)
    incl, strict = row >= col, row > col
    ones_l = incl.astype(F32)

    def unit(x):
        return x / jnp.maximum(jnp.sqrt(jnp.sum(x * x, axis=-1, keepdims=True)), 1e-12)

    kk = each(lambda k_, p: unit(k_ * p), k, k_k)
    km = each(lambda k_, a_, p: k_ * (1.0 + (a_ - 1.0) * p), k, a, k_a)
    b = each(lambda x, a_: x * a_, kk, a)
    cum = each(lambda w: _dot(ones_l, w, hi=True), logw)
    alpha = each(lambda x, cu, w: x * jnp.exp(cu - w), kk, cum, logw)
    beta = each(lambda x, cu: x * jnp.exp(-cu), b, cum)
    kap = each(lambda x, cu: x * jnp.exp(-cu), km, cum)
    rho = each(lambda x, cu: x * jnp.exp(cu), r, cum)
    lab = each(lambda x, y_: jnp.where(strict, gram(x, y_, "nt"), 0.0), alpha, beta)
    lak = each(lambda x, y_: jnp.where(strict, gram(x, y_, "nt"), 0.0), alpha, kap)
    xs = each(lambda al, s, l, v_: _dot(al, s, "nt") + _dot(l, v_), alpha, s0, lak, v)
    xs = each(lambda x, l: x - _dot(l, x), xs, lab)
    lp, power = lab, 2
    while power < c:
        lp = each(lambda l: _dot(l, l), lp)
        xs = each(lambda x, l: x + _dot(l, x), xs, lp)
        power *= 2
    u = each(lambda x: -x, xs)
    mrb = each(lambda x, y_: jnp.where(incl, gram(x, y_, "nt"), 0.0), rho, beta)
    mrk = each(lambda x, y_: jnp.where(incl, gram(x, y_, "nt"), 0.0), rho, kap)
    y = each(lambda rh, s, mb, u_, mk, v_: _dot(rh, s, "nt") + _dot(mb, u_) + _dot(mk, v_), rho, s0, mrb, u, mrk, v)
    s1 = each(lambda s, u_, be, v_, ka, w: (s + _dot(u_, be, "tn") + _dot(v_, ka, "tn")) * jnp.exp(jnp.sum(w, axis=0, keepdims=True)),
              s0, u, beta, v, kap, logw)

    def finish(y_, g, bias, r_, km_, rk, v_):
        mean = jnp.mean(y_, axis=-1, keepdims=True)
        var = jnp.mean(jnp.square(y_ - mean), axis=-1, keepdims=True)
        y_ = (y_ - mean) * lax.rsqrt(var + GN_EPS) * g + bias
        return y_ + jnp.sum(r_ * km_ * rk, axis=-1, keepdims=True) * v_

    return each(finish, y, gn_g, gn_b, r, km, r_k, v), s1


def _norm_fwd(name, x, g, shift, scale):
    return _rows(name, lambda x_, g_, sh, sc: _adaln(x_, g_, sh, sc), [x], [g, shift, scale], [(x.shape[1], BF16)], [], 256)[0]


def _norm_bwd(name, x, dh, dx_res, g, shift, scale):
    d = x.shape[1]

    def fn(x_, dh_, dr_, g_, sh, sc):
        _, vjp = jax.vjp(_adaln, x_, g_, sh, sc)
        dx, dg, dsh, dsc = vjp(dh_)
        return dx + dr_, dg, dsh, dsc

    return _rows(name, fn, [x, dh, dx_res], [g, shift, scale], [(d, F32)], [(1, d)] * 3, 256)


def _resid_fwd(name, x, y, gate):
    return _rows(name, lambda x_, y_, g_: x_ + g_ * y_, [x, y], [gate], [(x.shape[1], F32)], [], 256)[0]


def _resid_bwd(name, dx, y, gate):
    d = dx.shape[1]
    return _rows(name, lambda dx_, y_, g_: (g_ * dx_, jnp.sum(dx_ * y_, axis=0, keepdims=True)), [dx, y], [gate],
                 [(d, BF16)], [(1, d)], 256)


def _sg_fwd(name, p, ln_g, ln_b, w_s, b_st):
    d = p.shape[1] // 3
    return _rows(name, _sg_mix, [p], [ln_g, ln_b, w_s, b_st], [(d, BF16)], [], SG_CHUNK)[0]


def _sg_bwd(name, p, dmix, ln_g, ln_b, w_s, b_st):
    def fn(p_, dm_, lg, lb, ws, bs):
        _, vjp = jax.vjp(_sg_mix, p_, lg, lb, ws, bs)
        return vjp(dm_)

    return _rows(name, fn, [p, dmix], [ln_g, ln_b, w_s, b_st], [(p.shape[1], BF16)],
                 [ln_g.shape, ln_b.shape, w_s.shape, b_st.shape], SG_CHUNK)


def _rope_tables(pos, inv_freq):
    ang = pos * inv_freq
    return jnp.cos(ang), jnp.sin(ang)


def _swa_pre(name, p, pos, inv_freq, d):
    kvw = SWA_KV * HEAD

    def fn(p_, pos_, fr):
        cos, sin = _rope_tables(pos_, fr)
        return (_rope(p_[:, :d], cos, sin, 1.0), _rope(p_[:, d:d + kvw], cos, sin, 1.0), p_[:, d + kvw:d + 2 * kvw])

    return _rows(name, fn, [p, pos], [inv_freq], [(d, BF16), (kvw, BF16), (kvw, BF16)], [], 256)


def _swa_attn_fwd(name, q, k, v, sinks):
    kv, r, t, _ = q.shape
    nb = t // SWA_BLOCK

    def body(q_ref, kp_ref, kc_ref, vp_ref, vc_ref, s_ref, o_ref):
        prev_bias = jnp.where(pl.program_id(1) > 0, 0.0, NEG).astype(F32)
        o_ref[0] = _attn_block(q_ref[0], kp_ref[0], kc_ref[0], vp_ref[0], vc_ref[0], s_ref[0], prev_bias)

    qs = pl.BlockSpec((1, r, SWA_BLOCK, HEAD), lambda g, n: (g, 0, n, 0))
    cur = pl.BlockSpec((1, SWA_BLOCK, HEAD), lambda g, n: (g, n, 0))
    prev = pl.BlockSpec((1, SWA_BLOCK, HEAD), lambda g, n: (g, jnp.maximum(n - 1, 0), 0))
    ss = pl.BlockSpec((1, r, 1, 1), lambda g, n: (g, 0, 0, 0))
    return pl.pallas_call(body, grid=(kv, nb), in_specs=[qs, prev, cur, prev, cur, ss], out_specs=qs,
                          out_shape=S(q.shape, F32), name=name,
                          compiler_params=_params(("parallel", "arbitrary")))(q, k, k, v, v, sinks)


def _swa_attn_bwd(name, q, k, v, sinks, do):
    kv, r, t, _ = q.shape
    nb = t // SWA_BLOCK

    def body(q_ref, kp_ref, kc_ref, vp_ref, vc_ref, s_ref, do_ref, dq_ref, dkc_ref, dkp_ref, dvc_ref, dvp_ref, ds_ref):
        n = pl.program_id(1)
        prev_bias = jnp.where(n > 0, 0.0, NEG).astype(F32)
        fn = functools.partial(_attn_block, prev_bias=prev_bias)
        args = [ref[0].astype(F32) for ref in (q_ref, kp_ref, kc_ref, vp_ref, vc_ref)] + [s_ref[0]]
        _, vjp = jax.vjp(fn, *args)
        dq, dkp, dkc, dvp, dvc, ds = vjp(do_ref[0])
        dq_ref[0], dkc_ref[0], dkp_ref[0], dvc_ref[0], dvp_ref[0] = dq, dkc, dkp, dvc, dvp

        @pl.when(n == 0)
        def _():
            ds_ref[0] = ds

        @pl.when(n > 0)
        def _():
            ds_ref[0] += ds

    qs = pl.BlockSpec((1, r, SWA_BLOCK, HEAD), lambda g, n: (g, 0, n, 0))
    cur = pl.BlockSpec((1, SWA_BLOCK, HEAD), lambda g, n: (g, n, 0))
    prev = pl.BlockSpec((1, SWA_BLOCK, HEAD), lambda g, n: (g, jnp.maximum(n - 1, 0), 0))
    ss = pl.BlockSpec((1, r, 1, 1), lambda g, n: (g, 0, 0, 0))
    return pl.pallas_call(
        body, grid=(kv, nb), in_specs=[qs, prev, cur, prev, cur, ss, qs], out_specs=[qs, cur, cur, cur, cur, ss],
        out_shape=[S(q.shape, F32)] + [S(k.shape, F32)] * 4 + [S(sinks.shape, F32)], name=name,
        compiler_params=_params(("parallel", "arbitrary")))(q, k, k, v, v, sinks, do)


def _gate_fwd(name, o, z_src, z_off, d):
    return _rows(name, lambda o_, p_: o_ * jax.nn.silu(p_[:, z_off:z_off + d]), [o, z_src], [], [(d, BF16)], [], 256)[0]


def _gate_bwd(name, o, z_src, z_off, d, dmix):
    def fn(o_, p_, dm_):
        _, vjp = jax.vjp(lambda oo, zz: oo * jax.nn.silu(zz), o_, p_[:, z_off:z_off + d])
        return vjp(dm_)

    return _rows(name, fn, [o, z_src, dmix], [], [(d, F32), (d, F32)], [], 256)


def _swa_post_bwd(name, dq, dkc, dkp_up, dvc, dvp_up, dz, pos, inv_freq):
    def fn(dq_, dkc_, dkp_, dvc_, dvp_, dz_, pos_, fr):
        cos, sin = _rope_tables(pos_, fr)
        return jnp.concatenate([_rope(dq_, cos, sin, -1.0), _rope(dkc_ + dkp_, cos, sin, -1.0), dvc_ + dvp_, dz_], axis=1)

    n = dq.shape[1] + dkc.shape[1] + dvc.shape[1] + dz.shape[1]
    return _rows(name, fn, [dq, dkc, dkp_up, dvc, dvp_up, dz, pos], [inv_freq], [(n, BF16)], [], 256)[0]


def _lerp_fwd(name, p, ps, mu, widths):
    def fn(p_, ps_, mu_):
        pm = p_ + (ps_ - p_) * mu_
        outs, o = [], 0
        for w in widths:
            outs.append(pm[:, o:o + w])
            o += w
        return tuple(outs)

    return _rows(name, fn, [p, ps], [mu], [(w, F32) for w in widths], [], 128)


def _lerp_bwd(name, dpm_parts, p, ps, mu):
    n = p.shape[1]
    k = len(dpm_parts)

    def fn(*a):
        dpm = jnp.concatenate(a[:k], axis=1) if k > 1 else a[0]
        p_, ps_, mu_ = a[k:]
        return dpm * (1.0 - mu_), dpm * mu_, jnp.sum(dpm * (ps_ - p_), axis=0, keepdims=True)

    return _rows(name, fn, list(dpm_parts) + [p, ps], [mu], [(n, F32), (n, F32)], [(1, n)], 64)


def _shift_add(name, a, b_up, dtype):
    return _rows(name, lambda x_, y_: x_ + y_, [a, b_up], [], [(a.shape[1], dtype)], [], 128)[0]


def _lora_act(pl_, w0, w_lora, a0, a_lora):
    logw = -DECAY_SCALE * jax.nn.sigmoid(w0 + _dot(jnp.tanh(pl_), w_lora))
    a = jax.nn.sigmoid(a0 + _dot(pl_, a_lora))
    return logw, a


def _lora_fwd(name, pl_, w0, w_lora, a0, a_lora):
    d = w0.shape[1]
    return _rows(name, _lora_act, [pl_], [w0, w_lora, a0, a_lora], [(d, F32), (d, F32)], [], 256)


def _lora_bwd(name, pl_, dlogw, da, w0, w_lora, a0, a_lora):
    def fn(p_, dl_, da_, w0_, wl_, a0_, al_):
        _, vjp = jax.vjp(_lora_act, p_, w0_, wl_, a0_, al_)
        return vjp((dl_, da_))

    return _rows(name, fn, [pl_, dlogw, da], [w0, w_lora, a0, a_lora], [(pl_.shape[1], F32)],
                 [w0.shape, w_lora.shape, a0.shape, a_lora.shape], 256)


def _rwkv_scan_fwd(name, r, k, v, logw, a, hp):
    h, t, _ = r.shape
    nc = t // RW_CHUNK

    hb = RW_HEADS

    def body(r_ref, k_ref, v_ref, w_ref, a_ref, kk_ref, ka_ref, rk_ref, gg_ref, gb_ref, y_ref, st_ref, s_scr):
        @pl.when(pl.program_id(1) == 0)
        def _():
            s_scr[...] = jnp.zeros_like(s_scr)

        heads = lambda ref: [ref[i] for i in range(hb)]
        s0 = heads(s_scr)
        for i in range(hb):
            st_ref[i, 0] = s0[i]
        y, s1 = _rwkv_chunk(s0, *[heads(ref) for ref in (r_ref, k_ref, v_ref, w_ref, a_ref, kk_ref, ka_ref, rk_ref, gg_ref, gb_ref)])
        for i in range(hb):
            y_ref[i] = y[i]
            s_scr[i] = s1[i]

    seq = pl.BlockSpec((hb, RW_CHUNK, HEAD), lambda i, n: (i, n, 0))
    par = pl.BlockSpec((hb, 1, HEAD), lambda i, n: (i, 0, 0))
    st = pl.BlockSpec((hb, 1, HEAD, HEAD), lambda i, n: (i, n, 0, 0))
    return pl.pallas_call(body, grid=(h // hb, nc), in_specs=[seq] * 5 + [par] * 5, out_specs=[seq, st],
                          out_shape=[S((h, t, HEAD), F32), S((h, nc, HEAD, HEAD), F32)],
                          scratch_shapes=[pltpu.VMEM((hb, HEAD, HEAD), F32)], name=name,
                          compiler_params=_params(("parallel", "arbitrary")))(r, k, v, logw, a, *hp)


def _rwkv_scan_bwd(name, r, k, v, logw, a, hp, states, dy):
    h, t, _ = r.shape
    nc = t // RW_CHUNK

    def body(r_ref, k_ref, v_ref, w_ref, a_ref, kk_ref, ka_ref, rk_ref, gg_ref, gb_ref, st_ref, dy_ref,
             dr_ref, dk_ref, dv_ref, dw_ref, da_ref, dkk_ref, dka_ref, drk_ref, dgg_ref, dgb_ref, ds_scr):
        n = pl.program_id(1)

        @pl.when(n == 0)
        def _():
            ds_scr[...] = jnp.zeros_like(ds_scr)
            for ref in (dkk_ref, dka_ref, drk_ref, dgg_ref, dgb_ref):
                ref[...] = jnp.zeros_like(ref)

        heads = lambda ref: [ref[i] for i in range(hb)]
        ins = [[st_ref[i, 0] for i in range(hb)]] + [heads(ref) for ref in (r_ref, k_ref, v_ref, w_ref, a_ref, kk_ref, ka_ref,
                                                                           rk_ref, gg_ref, gb_ref)]
        _, vjp = jax.vjp(_rwkv_chunk, *ins)
        ds0, *dseq, dkk, dka, drk, dgg, dgb = vjp((heads(dy_ref), heads(ds_scr)))
        for i in range(hb):
            ds_scr[i] = ds0[i]
            for ref, val in zip((dr_ref, dk_ref, dv_ref, dw_ref, da_ref), dseq):
                ref[i] = val[i]
            for ref, val in ((dkk_ref, dkk), (dka_ref, dka), (drk_ref, drk), (dgg_ref, dgg), (dgb_ref, dgb)):
                ref[i] += val[i]

    hb = RW_HEADS
    seq = pl.BlockSpec((hb, RW_CHUNK, HEAD), lambda i, n: (i, nc - 1 - n, 0))
    par = pl.BlockSpec((hb, 1, HEAD), lambda i, n: (i, 0, 0))
    st = pl.BlockSpec((hb, 1, HEAD, HEAD), lambda i, n: (i, nc - 1 - n, 0, 0))
    return pl.pallas_call(body, grid=(h // hb, nc), in_specs=[seq] * 5 + [par] * 5 + [st, seq], out_specs=[seq] * 5 + [par] * 5,
                          out_shape=[S((h, t, HEAD), F32)] * 5 + [S((h, 1, HEAD), F32)] * 5,
                          scratch_shapes=[pltpu.VMEM((hb, HEAD, HEAD), F32)], name=name,
                          compiler_params=_params(("parallel", "arbitrary")))(r, k, v, logw, a, *hp, states, dy)


def _loss_head(name, x, target, g):
    d = x.shape[1]

    def fn(x_, t_, g_):
        def f(xx, gg):
            err = _rms(xx, gg) - t_
            return 0.5 * jnp.sum(jnp.mean(err * err, axis=-1, keepdims=True), axis=0, keepdims=True)

        l, vjp = jax.vjp(f, x_, g_)
        dx, dg = vjp(jnp.ones((1, 1), F32))
        return dx, dg, jnp.broadcast_to(l, (1, 128))

    return _rows(name, fn, [x, target], [g], [(d, F32)], [(1, d), (1, 128)], 256)


def _mod_fwd(name, cond_all, mod_w, mod_b_cols):
    l, d, n = mod_w.shape

    def body(c_ref, w_ref, b_ref, o_ref):
        o_ref[0] = _dot(jax.nn.silu(c_ref[...]), w_ref[0], hi=True) + b_ref[0]

    return pl.pallas_call(body, grid=(l,), in_specs=[pl.BlockSpec((N_DEV, d), lambda i: (0, 0)),
                                                      pl.BlockSpec((1, d, n), lambda i: (i, 0, 0)),
                                                      pl.BlockSpec((1, 1, n), lambda i: (i, 0, 0))],
                          out_specs=pl.BlockSpec((1, N_DEV, n), lambda i: (i, 0, 0)), out_shape=S((l, N_DEV, n), F32),
                          name=name, compiler_params=_params(("parallel",)))(cond_all, mod_w, mod_b_cols)


def _mod_bwd(name, cond_all, dmod_cols, dmod_all):
    l, _, n = dmod_cols.shape
    d = cond_all.shape[1]
    nb = dmod_all.shape[2]

    def body(c_ref, dc_ref, da_ref, gw_ref, gb_ref):
        gw_ref[0] = _dot(jax.nn.silu(c_ref[...]), dc_ref[0], "tn", hi=True)
        acc = da_ref[0, 0:1, :]
        for bi in range(1, N_DEV):
            acc = acc + da_ref[0, bi:bi + 1, :]
        gb_ref[0] = acc

    return pl.pallas_call(body, grid=(l,), in_specs=[pl.BlockSpec((N_DEV, d), lambda i: (0, 0)),
                                                      pl.BlockSpec((1, N_DEV, n), lambda i: (i, 0, 0)),
                                                      pl.BlockSpec((1, N_DEV, nb), lambda i: (i, 0, 0))],
                          out_specs=[pl.BlockSpec((1, d, n), lambda i: (i, 0, 0)), pl.BlockSpec((1, 1, nb), lambda i: (i, 0, 0))],
                          out_shape=[S((l, d, n), F32), S((l, 1, nb), F32)], name=name,
                          compiler_params=_params(("parallel",)))(cond_all, dmod_cols, dmod_all)


def _to_heads(a):
    t, n = a.shape
    return a.reshape(t, n // HEAD, HEAD).transpose(1, 0, 2)


def _from_heads(a):
    h, t, _ = a.shape
    return a.transpose(1, 0, 2).reshape(t, h * HEAD)


def _shift_down(a):
    return jnp.concatenate([jnp.zeros_like(a[:1]), a[:-1]], axis=0)


def _shift_up(a, n=1):
    return jnp.concatenate([a[n:], jnp.zeros_like(a[:n])], axis=0)


def _cols_full(g):
    return g.transpose(1, 0, 2).reshape(g.shape[1], -1)


def _cols_parts(full):
    r, n = full.shape
    return full.reshape(r, N_DEV, n // N_DEV).transpose(1, 0, 2)


def _pack(arrs, mult=1024):
    flat = jnp.concatenate([a.reshape(-1) for a in arrs])
    pad = (-flat.shape[0]) % mult
    return jnp.pad(flat, (0, pad)).reshape(-1, 128)


def _unpack(flat, shapes):
    out, o = [], 0
    for s in shapes:
        n = math.prod(s)
        out.append(flat[o:o + n].reshape(s))
        o += n
    return out


def _local_step(x, pos, target, mods, norm_g, final_norm_g, sg, swa, rw):
    t, d = x.shape
    kinds = [i % 3 for i in range(DEPTH)]
    inv_freq = (ROPE_THETA ** (-jnp.arange(HEAD // 2, dtype=F32) / (HEAD // 2)))
    inv_freq = jnp.tile(inv_freq, 128 // (HEAD // 2)).reshape(1, 128)
    saved = []
    for i, kind in enumerate(kinds):
        j = i // 3
        shift, scale, gate = (mods[i, q * d:(q + 1) * d].reshape(1, d) for q in range(3))
        g = norm_g[i].reshape(1, d)
        h = _norm_fwd(f"norm_fwd{i}", x, g, shift, scale)
        sv = dict(x=x, h=h, g=g, shift=shift, scale=scale, gate=gate)
        if kind == 0:
            p = _mm(f"sg_in{i}", h, sg["w_in"][j], "nn", F32)
            mix = _sg_fwd(f"sg_mix{i}", p, sg["ln_g"][j], sg["ln_b"][j], sg["w_s"][j], sg["b_st"][j])
            w_out = sg["w_out"][j]
            sv.update(p=p)
        elif kind == 1:
            p = _mm(f"swa_in{i}", h, swa["w_in"], "nn", F32)
            q, k, v = _swa_pre(f"swa_pre{i}", p, pos, inv_freq, d)
            qh = _to_heads(q).reshape(SWA_KV, SWA_REP, t, HEAD)
            kh, vh = _to_heads(k), _to_heads(v)
            o = _swa_attn_fwd(f"swa_attn{i}", qh, kh, vh, swa["sinks"])
            o = _from_heads(o.reshape(SWA_KV * SWA_REP, t, HEAD))
            mix = _gate_fwd(f"swa_gate{i}", o, p, d + 2 * SWA_KV * HEAD, d)
            w_out = swa["w_out"]
            sv.update(p=p, qh=qh, kh=kh, vh=vh, o=o)
        else:
            pm = _mm(f"rw_in{i}", h, rw["w_main"], "nn", F32)
            plo = _mm(f"rw_inl{i}", h, rw["w_lorain"], "nn", F32)
            pms, plos = _shift_down(pm), _shift_down(plo)
            r, k, v, z = _lerp_fwd(f"rw_lerp{i}", pm, pms, rw["mu_main"], [d] * 4)
            (pll,) = _lerp_fwd(f"rw_lerpl{i}", plo, plos, rw["mu_lora"], [LORA_PAD])
            logw, a = _lora_fwd(f"rw_lora{i}", pll, rw["w0"], rw["w_lora"], rw["a0"], rw["a_lora"])
            hs = [_to_heads(u) for u in (r, k, v, logw, a)]
            yh, states = _rwkv_scan_fwd(f"rw_scan{i}", *hs, rw["hp"])
            o = _from_heads(yh)
            mix = _gate_fwd(f"rw_gate{i}", o, z, 0, d)
            w_out = rw["w_out"]
            sv.update(pm=pm, pms=pms, plo=plo, plos=plos, pll=pll, z=z, hs=hs, states=states, o=o)
        y = _mm(f"out{i}", mix, w_out, "nn", F32)
        sv.update(mix=mix, y=y)
        saved.append(sv)
        x = _resid_fwd(f"resid{i}", x, y, gate)

    dx, d_final_g, loss = _loss_head("loss_head", x, target, final_norm_g.reshape(1, d))

    grads = dict(norm_g=[None] * DEPTH, sg_w_in=[None] * 2, sg_w_out=[None] * 2, sg_ln_g=[None] * 2, sg_ln_b=[None] * 2,
                 sg_w_s=[None] * 2, sg_b_st=[None] * 2, final_norm_g=d_final_g)
    dmods = [None] * DEPTH
    for i in reversed(range(DEPTH)):
        kind, j, sv = kinds[i], i // 3, saved[i]
        dy, dgate = _resid_bwd(f"resid_bwd{i}", dx, sv["y"], sv["gate"])
        w_out = (sg["w_out"][j], swa["w_out"], rw["w_out"])[kind]
        d_w_out = _mm(f"out_dw{i}", sv["mix"], dy, "tn", F32)
        dmix = _mm(f"out_dx{i}", dy, w_out, "nt", F32)
        if kind == 0:
            dp, dlg, dlb, dws, dbs = _sg_bwd(f"sg_mix_bwd{i}", sv["p"], dmix, sg["ln_g"][j], sg["ln_b"][j],
                                            sg["w_s"][j], sg["b_st"][j])
            grads["sg_ln_g"][j], grads["sg_ln_b"][j], grads["sg_w_s"][j], grads["sg_b_st"][j] = dlg, dlb, dws, dbs
            grads["sg_w_out"][j] = d_w_out
            grads["sg_w_in"][j] = _mm(f"sg_in_dw{i}", sv["h"], dp, "tn", F32)
            dh = _mm(f"sg_in_dx{i}", dp, sg["w_in"][j], "nt", F32)
        elif kind == 1:
            z_off = d + 2 * SWA_KV * HEAD
            do, dz = _gate_bwd(f"swa_gate_bwd{i}", sv["o"], sv["p"], z_off, d, dmix)
            doh = _to_heads(do).reshape(SWA_KV, SWA_REP, t, HEAD)
            dq, dkc, dkp, dvc, dvp, dsinks = _swa_attn_bwd(f"swa_attn_bwd{i}", sv["qh"], sv["kh"], sv["vh"], swa["sinks"], doh)
            dq = _from_heads(dq.reshape(SWA_KV * SWA_REP, t, HEAD))
            dkc, dvc = _from_heads(dkc), _from_heads(dvc)
            dkp, dvp = _shift_up(_from_heads(dkp), SWA_BLOCK), _shift_up(_from_heads(dvp), SWA_BLOCK)
            dp = _swa_post_bwd(f"swa_post_bwd{i}", dq, dkc, dkp, dvc, dvp, dz, pos, inv_freq)
            grads.update(swa_sinks=dsinks, swa_w_out=d_w_out)
            grads["swa_w_in"] = _mm(f"swa_in_dw{i}", sv["h"], dp, "tn", F32)
            dh = _mm(f"swa_in_dx{i}", dp, swa["w_in"], "nt", F32)
        else:
            do, dz = _gate_bwd(f"rw_gate_bwd{i}", sv["o"], sv["z"], 0, d, dmix)
            res = _rwkv_scan_bwd(f"rw_scan_bwd{i}", *sv["hs"], rw["hp"], sv["states"], _to_heads(do))
            dr, dk, dv, dlogw, da = (_from_heads(u) for u in res[:5])
            dpll, dw0, dwl, da0, dal = _lora_bwd(f"rw_lora_bwd{i}", sv["pll"], dlogw, da, rw["w0"], rw["w_lora"],
                                                  rw["a0"], rw["a_lora"])
            dpm, dpms, dmu_main = _lerp_bwd(f"rw_lerp_bwd{i}", [dr, dk, dv, dz], sv["pm"], sv["pms"], rw["mu_main"])
            dpl, dpls, dmu_lora = _lerp_bwd(f"rw_lerpl_bwd{i}", [dpll], sv["plo"], sv["plos"], rw["mu_lora"])
            dpm = _shift_add(f"rw_shift_add{i}", dpm, _shift_up(dpms), BF16)
            dpl = _shift_add(f"rw_shift_addl{i}", dpl, _shift_up(dpls), BF16)
            grads.update(rw_w_out=d_w_out, rw_hp=res[5:], rw_w0=dw0, rw_w_lora=dwl, rw_a0=da0, rw_a_lora=dal,
                         rw_mu_main=dmu_main, rw_mu_lora=dmu_lora)
            grads["rw_w_main"] = _mm(f"rw_in_dw{i}", sv["h"], dpm, "tn", F32)
            grads["rw_w_lorain"] = _mm(f"rw_inl_dw{i}", sv["h"], dpl, "tn", F32)
            dh = _mm(f"rw_inl_dx{i}", dpl, rw["w_lorain"], "nt", F32)
            dh = _mm(f"rw_in_dx{i}", dpm, rw["w_main"], "nt", F32, add=dh)
        dx, dg, dshift, dscale = _norm_bwd(f"norm_bwd{i}", sv["x"], dh, dx, sv["g"], sv["shift"], sv["scale"])
        grads["norm_g"][i] = dg
        dmods[i] = jnp.concatenate([dshift, dscale, dgate], axis=1)
    return loss, dx, jnp.concatenate(dmods, axis=0), grads


def kernel(x, c, positions, norm_g, mod_w, mod_b, final_norm_g, sg_w_in, sg_w_out, sg_ln_g, sg_ln_b, sg_w_spatial, sg_b_spatial, swa_w_in, swa_w_out, swa_sinks, rwkv_w_in, rwkv_w_out, rwkv_mu, rwkv_w0, rwkv_w_lora, rwkv_a0, rwkv_a_lora, rwkv_k_k, rwkv_k_a, rwkv_r_k, rwkv_gn_g, rwkv_gn_b, loss_target, m_norm_g, m_mod_w, m_mod_b, m_final_norm_g, m_sg_w_in, m_sg_w_out, m_sg_ln_g, m_sg_ln_b, m_sg_w_spatial, m_sg_b_spatial, m_swa_w_in, m_swa_w_out, m_swa_sinks, m_rwkv_w_in, m_rwkv_w_out, m_rwkv_mu, m_rwkv_w0, m_rwkv_w_lora, m_rwkv_a0, m_rwkv_a_lora, m_rwkv_k_k, m_rwkv_k_a, m_rwkv_r_k, m_rwkv_gn_g, m_rwkv_gn_b, v_norm_g, v_mod_w, v_mod_b, v_final_norm_g, v_sg_w_in, v_sg_w_out, v_sg_ln_g, v_sg_ln_b, v_sg_w_spatial, v_sg_b_spatial, v_swa_w_in, v_swa_w_out, v_swa_sinks, v_rwkv_w_in, v_rwkv_w_out, v_rwkv_mu, v_rwkv_w0, v_rwkv_w_lora, v_rwkv_a0, v_rwkv_a_lora, v_rwkv_k_k, v_rwkv_k_a, v_rwkv_r_k, v_rwkv_gn_g, v_rwkv_gn_b):
    weights = dict(norm_g=norm_g, mod_w=mod_w, mod_b=mod_b, final_norm_g=final_norm_g, sg_w_in=sg_w_in, sg_w_out=sg_w_out,
                   sg_ln_g=sg_ln_g, sg_ln_b=sg_ln_b, sg_w_spatial=sg_w_spatial, sg_b_spatial=sg_b_spatial, swa_w_in=swa_w_in,
                   swa_w_out=swa_w_out, swa_sinks=swa_sinks, rwkv_w_in=rwkv_w_in, rwkv_w_out=rwkv_w_out, rwkv_mu=rwkv_mu,
                   rwkv_w0=rwkv_w0, rwkv_w_lora=rwkv_w_lora, rwkv_a0=rwkv_a0, rwkv_a_lora=rwkv_a_lora, rwkv_k_k=rwkv_k_k,
                   rwkv_k_a=rwkv_k_a, rwkv_r_k=rwkv_r_k, rwkv_gn_g=rwkv_gn_g, rwkv_gn_b=rwkv_gn_b)
    mom_m = dict(norm_g=m_norm_g, mod_w=m_mod_w, mod_b=m_mod_b, final_norm_g=m_final_norm_g, sg_w_in=m_sg_w_in,
                 sg_w_out=m_sg_w_out, sg_ln_g=m_sg_ln_g, sg_ln_b=m_sg_ln_b, sg_w_spatial=m_sg_w_spatial,
                 sg_b_spatial=m_sg_b_spatial, swa_w_in=m_swa_w_in, swa_w_out=m_swa_w_out, swa_sinks=m_swa_sinks,
                 rwkv_w_in=m_rwkv_w_in, rwkv_w_out=m_rwkv_w_out, rwkv_mu=m_rwkv_mu, rwkv_w0=m_rwkv_w0,
                 rwkv_w_lora=m_rwkv_w_lora, rwkv_a0=m_rwkv_a0, rwkv_a_lora=m_rwkv_a_lora, rwkv_k_k=m_rwkv_k_k,
                 rwkv_k_a=m_rwkv_k_a, rwkv_r_k=m_rwkv_r_k, rwkv_gn_g=m_rwkv_gn_g, rwkv_gn_b=m_rwkv_gn_b)
    mom_v = dict(norm_g=v_norm_g, mod_w=v_mod_w, mod_b=v_mod_b, final_norm_g=v_final_norm_g, sg_w_in=v_sg_w_in,
                 sg_w_out=v_sg_w_out, sg_ln_g=v_sg_ln_g, sg_ln_b=v_sg_ln_b, sg_w_spatial=v_sg_w_spatial,
                 sg_b_spatial=v_sg_b_spatial, swa_w_in=v_swa_w_in, swa_w_out=v_swa_w_out, swa_sinks=v_swa_sinks,
                 rwkv_w_in=v_rwkv_w_in, rwkv_w_out=v_rwkv_w_out, rwkv_mu=v_rwkv_mu, rwkv_w0=v_rwkv_w0,
                 rwkv_w_lora=v_rwkv_w_lora, rwkv_a0=v_rwkv_a0, rwkv_a_lora=v_rwkv_a_lora, rwkv_k_k=v_rwkv_k_k,
                 rwkv_k_a=v_rwkv_k_a, rwkv_r_k=v_rwkv_r_k, rwkv_gn_g=v_rwkv_gn_g, rwkv_gn_b=v_rwkv_gn_b)
    names = list(weights)
    t, d = x.shape[1], x.shape[2]
    me = 4 * lax.axis_index("x") + 2 * lax.axis_index("y") + lax.axis_index("c")
    n_mod = mod_w.shape[2]
    n_rw = rwkv_w_in.shape[2]

    small_names = ["sg_ln_g", "sg_ln_b", "rwkv_mu", "rwkv_w0", "rwkv_a0", "rwkv_k_k", "rwkv_k_a", "rwkv_gn_g", "rwkv_gn_b",
                   "rwkv_w_lora", "rwkv_a_lora"]
    small_shapes = [weights[n].shape for n in small_names]
    pk = _pack([c] + [weights[n] for n in small_names])
    gathered = _gather("gather_small", pk).reshape(N_DEV, -1)
    c_all = gathered[:, :d]
    per_dev = [_unpack(gathered[dv, d:], small_shapes) for dv in range(N_DEV)]
    full_small = {}
    for q, n in enumerate(small_names):
        full_small[n] = jnp.concatenate([per_dev[dv][q] for dv in range(N_DEV)], axis=-1)

    mod_b_cols = lax.dynamic_slice_in_dim(mod_b, me * n_mod, n_mod, axis=1).reshape(DEPTH, 1, n_mod)
    mod_part = _mod_fwd("mod_fwd", c_all, mod_w, mod_b_cols)
    mod_g = _gather("gather_mod", mod_part.reshape(DEPTH * N_DEV, n_mod))
    mod_g = mod_g.reshape(N_DEV, DEPTH, N_DEV, n_mod)
    mods = lax.dynamic_index_in_dim(mod_g, me, axis=2, keepdims=False)
    mods = mods.transpose(1, 0, 2).reshape(DEPTH, N_DEV * n_mod)

    def gather_big(name, w2d):
        return _gather(name, w2d.astype(BF16))

    g_sg_in = gather_big("gather_sg_in", sg_w_in.reshape(-1, sg_w_in.shape[2])).reshape(N_DEV, 2, d, -1)
    g_sg_out = gather_big("gather_sg_out", sg_w_out.reshape(-1, d)).reshape(N_DEV, 2, -1, d)
    g_swa_in = gather_big("gather_swa_in", swa_w_in[0])
    g_swa_out = gather_big("gather_swa_out", swa_w_out[0])
    g_rw_in = gather_big("gather_rw_in", rwkv_w_in[0])
    g_rw_out = gather_big("gather_rw_out", rwkv_w_out[0])
    rw_in_full = _cols_full(g_rw_in)
    lora_rows = lambda w, off: jnp.zeros((LORA_PAD, d), F32).at[off:off + LORA].set(w)
    mu = full_small["rwkv_mu"].reshape(1, -1)
    heads = lambda a: a.reshape(-1, 1, HEAD)
    sg = dict(w_in=[_cols_full(g_sg_in[:, j]) for j in range(2)], w_out=[g_sg_out[:, j].reshape(d, d) for j in range(2)],
              ln_g=[full_small["sg_ln_g"][j].reshape(1, d) for j in range(2)],
              ln_b=[full_small["sg_ln_b"][j].reshape(1, d) for j in range(2)],
              w_s=[sg_w_spatial[j] for j in range(2)], b_st=[sg_b_spatial[j].T for j in range(2)])
    swa = dict(w_in=_cols_full(g_swa_in), w_out=g_swa_out.reshape(d, d), sinks=swa_sinks.reshape(SWA_KV, SWA_REP, 1, 1))
    rw = dict(w_main=rw_in_full[:, :4 * d], w_lorain=jnp.pad(rw_in_full[:, 4 * d:], ((0, 0), (0, LORA_PAD - 2 * LORA))),
              w_out=g_rw_out.reshape(d, d), mu_main=mu[:, :4 * d], mu_lora=jnp.pad(mu[:, 4 * d:], ((0, 0), (0, LORA_PAD - 2 * LORA))),
              w0=full_small["rwkv_w0"], a0=full_small["rwkv_a0"],
              w_lora=lora_rows(full_small["rwkv_w_lora"][0], 0), a_lora=lora_rows(full_small["rwkv_a_lora"][0], LORA),
              hp=[heads(full_small["rwkv_k_k"]), heads(full_small["rwkv_k_a"]), heads(rwkv_r_k), heads(full_small["rwkv_gn_g"]),
                  heads(full_small["rwkv_gn_b"])])

    loss, dx, dmods, g = _local_step(x[0], positions.reshape(t, 1).astype(F32), loss_target[0], mods, norm_g, final_norm_g,
                                     sg, swa, rw)

    dmod_g = _gather("gather_dmod", dmods)
    dmod_all = dmod_g.transpose(1, 0, 2)
    dmod_cols = lax.dynamic_slice_in_dim(dmod_all, me * n_mod, n_mod, axis=2)
    g_mod_w, g_mod_b = _mod_bwd("mod_bwd", c_all, dmod_cols, dmod_all)

    d_b_sp = [g["sg_b_st"][j].T for j in range(2)]
    rep = [loss[0, :1], jnp.concatenate(g["norm_g"], axis=0), g["final_norm_g"], jnp.stack(g["sg_w_s"]), jnp.stack(d_b_sp),
           g["swa_sinks"], g["rw_hp"][2]]
    rep_shapes = [(1,), norm_g.shape, final_norm_g.shape, sg_w_spatial.shape, sg_b_spatial.shape, swa_sinks.shape, rwkv_r_k.shape]
    rep_sum = _sum_parts("sum_rep", _gather("gather_rep", _pack(rep, 128 * 256))).reshape(-1)
    loss_tot, g_norm_g, g_final, g_w_sp, g_b_sp, g_sinks, g_r_k = _unpack(rep_sum, rep_shapes)

    def scatter_big(name, parts):
        return _scatter(name, parts.astype(BF16))

    p_sg_in = scatter_big("scatter_sg_in", jnp.concatenate([_cols_parts(g["sg_w_in"][j]) for j in range(2)], axis=1))
    p_sg_out = scatter_big("scatter_sg_out", jnp.concatenate([g["sg_w_out"][j].reshape(N_DEV, -1, d) for j in range(2)], axis=1))
    p_swa_in = scatter_big("scatter_swa_in", _cols_parts(g["swa_w_in"]))
    p_swa_out = scatter_big("scatter_swa_out", g["swa_w_out"].reshape(N_DEV, -1, d))
    d_rw_in = jnp.concatenate([g["rw_w_main"], g["rw_w_lorain"][:, :2 * LORA]], axis=1)
    p_rw_in = scatter_big("scatter_rw_in", _cols_parts(d_rw_in))
    p_rw_out = scatter_big("scatter_rw_out", g["rw_w_out"].reshape(N_DEV, -1, d))
    d_mu = jnp.concatenate([g["rw_mu_main"], g["rw_mu_lora"][:, :2 * LORA]], axis=1)
    hp_flat = lambda a: a.reshape(1, -1)
    small_grads = dict(sg_ln_g=jnp.concatenate(g["sg_ln_g"], axis=0), sg_ln_b=jnp.concatenate(g["sg_ln_b"], axis=0), rwkv_mu=d_mu,
                       rwkv_w0=g["rw_w0"], rwkv_a0=g["rw_a0"], rwkv_k_k=hp_flat(g["rw_hp"][0]), rwkv_k_a=hp_flat(g["rw_hp"][1]),
                       rwkv_gn_g=hp_flat(g["rw_hp"][3]), rwkv_gn_b=hp_flat(g["rw_hp"][4]),
                       rwkv_w_lora=g["rw_w_lora"][None, :LORA], rwkv_a_lora=g["rw_a_lora"][None, LORA:2 * LORA])
    per_dest = []
    for dv in range(N_DEV):
        shards = []
        for n in small_names:
            full, w = small_grads[n], weights[n].shape[-1]
            shards.append(full[..., dv * w:(dv + 1) * w])
        per_dest.append(_pack(shards))
    small_parts = _exchange("scatter_small", jnp.stack(per_dest), True)

    out_g, out_d, out_m, out_v = {}, {}, {}, {}

    def update(name, grad, shape2d):
        w2, m2, v2 = (a[name].reshape(shape2d) for a in (weights, mom_m, mom_v))
        gg, dd, mm, vv = _adamw("adamw_" + name, w2, grad, m2, v2)
        shp = weights[name].shape
        out_g[name], out_d[name], out_m[name], out_v[name] = gg.reshape(shp), dd.reshape(shp), mm.reshape(shp), vv.reshape(shp)

    update("mod_w", g_mod_w.reshape(-1, n_mod), (-1, n_mod))
    update("sg_w_in", p_sg_in, (-1, sg_w_in.shape[2]))
    update("sg_w_out", p_sg_out, (-1, d))
    update("swa_w_in", p_swa_in, (-1, swa_w_in.shape[2]))
    update("swa_w_out", p_swa_out, (-1, d))
    update("rwkv_w_in", p_rw_in, (-1, n_rw))
    update("rwkv_w_out", p_rw_out, (-1, d))
    update("sg_w_spatial", g_w_sp.reshape(-1, 128), (-1, 128))
    w_pk, m_pk, v_pk = (_pack([a[n] for n in small_names]) for a in (weights, mom_m, mom_v))
    res = _adamw("adamw_small", w_pk, small_parts, m_pk, v_pk)
    for q, arrs in enumerate(zip(*[_unpack(r_.reshape(-1), small_shapes) for r_ in res])):
        out_g[small_names[q]], out_d[small_names[q]], out_m[small_names[q]], out_v[small_names[q]] = arrs
    rep_names = ["norm_g", "mod_b", "final_norm_g", "sg_b_spatial", "swa_sinks", "rwkv_r_k"]
    rep_grads = [g_norm_g, g_mod_b.reshape(mod_b.shape), g_final, g_b_sp, g_sinks, g_r_k]
    rep_shapes2 = [weights[n].shape for n in rep_names]
    w_pk, m_pk, v_pk = (_pack([a[n] for n in rep_names]) for a in (weights, mom_m, mom_v))
    res = _adamw("adamw_rep", w_pk, _pack(rep_grads), m_pk, v_pk)
    for q, arrs in enumerate(zip(*[_unpack(r_.reshape(-1), rep_shapes2) for r_ in res])):
        out_g[rep_names[q]], out_d[rep_names[q]], out_m[rep_names[q]], out_v[rep_names[q]] = arrs

    return (loss_tot.reshape(()), dx[None], *[out_g[n] for n in names], *[out_d[n] for n in names],
            *[out_m[n] for n in names], *[out_v[n] for n in names])
```

```python
import functools
import math

import jax
import jax.numpy as jnp
from jax import lax
from jax.experimental import pallas as pl
from jax.experimental.pallas import tpu as pltpu

F32, BF16 = jnp.float32, jnp.bfloat16
HI = lax.Precision.HIGHEST
S = jax.ShapeDtypeStruct
MESH = pl.DeviceIdType.MESH

N_DEV = 8
DEPTH = 4
HEAD = 64
SG_GROUPS = 16
SG_CHUNK = 128
SWA_BLOCK = 128
SWA_KV = 4
SWA_REP = 8
ROPE_THETA = 10000.0
LORA = 96
LORA_PAD = 256
RW_CHUNK = 64
RW_HEADS = 16
RW_PREC = lax.Precision.HIGH
DECAY_SCALE = math.exp(-0.5)
GN_EPS = 64e-5
RMS_EPS = 1e-6
LN_EPS = 1e-5
NEG = -1e30
ADAM_LR, ADAM_B1, ADAM_B2, ADAM_EPS, ADAM_WD, ADAM_STEP = 0.001, 0.9, 0.999, 1e-08, 0.01, 10
VMEM_MB = 56


def _params(sem=None):
    kw = dict(vmem_limit_bytes=VMEM_MB << 20)
    if sem is not None:
        kw["dimension_semantics"] = sem
    return pltpu.CompilerParams(**kw)


def _pick(n, opts):
    for o in opts:
        if n % o == 0:
            return o
    raise ValueError(f"no tile for {n}")


def _rows(name, fn, rows, consts, out_rows, out_accs, tm):
    t = rows[0].shape[0]
    nr, nc, no = len(rows), len(consts), len(out_rows)

    def body(*refs):
        outs = fn(*[r[...] for r in refs[:nr + nc]])
        if not isinstance(outs, (tuple, list)):
            outs = (outs,)
        for r, o in zip(refs[nr + nc:nr + nc + no], outs[:no]):
            r[...] = o.astype(r.dtype)
        i = pl.program_id(0)
        for r, o in zip(refs[nr + nc + no:], outs[no:]):
            @pl.when(i == 0)
            def _(r=r, o=o):
                r[...] = o.astype(r.dtype)

            @pl.when(i > 0)
            def _(r=r, o=o):
                r[...] += o.astype(r.dtype)

    in_specs = [pl.BlockSpec((tm, a.shape[1]), lambda i: (i, 0)) for a in rows]
    in_specs += [pl.BlockSpec(c.shape, lambda i, nd=c.ndim: (0,) * nd) for c in consts]
    out_specs = [pl.BlockSpec((tm, n), lambda i: (i, 0)) for n, _ in out_rows]
    out_specs += [pl.BlockSpec(s, lambda i, nd=len(s): (0,) * nd) for s in out_accs]
    out_shape = [S((t, n), dt) for n, dt in out_rows] + [S(s, F32) for s in out_accs]
    res = pl.pallas_call(body, grid=(t // tm,), in_specs=in_specs, out_specs=out_specs, out_shape=out_shape,
                         name=name, compiler_params=_params(("arbitrary",)))(*rows, *consts)
    return res


_DN = {"nn": (((1,), (0,)), ((), ())), "nt": (((1,), (1,)), ((), ())), "tn": (((0,), (0,)), ((), ()))}


def _mm(name, a, b, mode, out_dtype, add=None):
    if mode == "nn":
        (m, k), (_, n) = a.shape, b.shape
    elif mode == "nt":
        (m, k), (n, _) = a.shape, b.shape
    else:
        (k, m), (_, n) = a.shape, b.shape
    tm, tn, tk = _pick(m, (512, 256, 128)), _pick(n, (1024, 512, 384, 256, 128)), _pick(k, (2048, 1536, 1024, 512, 384, 256, 128))
    nk = k // tk
    has_add = add is not None

    def body(*refs):
        a_ref, b_ref = refs[0], refs[1]
        o_ref, acc = refs[-2], refs[-1]
        kk = pl.program_id(2)
        prod = lax.dot_general(a_ref[...].astype(BF16), b_ref[...].astype(BF16), _DN[mode], preferred_element_type=F32)
        if nk == 1:
            o_ref[...] = (prod + refs[2][...].astype(F32) if has_add else prod).astype(o_ref.dtype)
            return

        @pl.when(kk == 0)
        def _():
            acc[...] = prod + refs[2][...].astype(F32) if has_add else prod

        @pl.when(kk > 0)
        def _():
            acc[...] += prod

        @pl.when(kk == nk - 1)
        def _():
            o_ref[...] = acc[...].astype(o_ref.dtype)

    a_spec = pl.BlockSpec((tk, tm), lambda i, j, q: (q, i)) if mode == "tn" else pl.BlockSpec((tm, tk), lambda i, j, q: (i, q))
    b_spec = pl.BlockSpec((tn, tk), lambda i, j, q: (j, q)) if mode == "nt" else pl.BlockSpec((tk, tn), lambda i, j, q: (q, j))
    o_spec = pl.BlockSpec((tm, tn), lambda i, j, q: (i, j))
    ins, specs = [a, b], [a_spec, b_spec]
    if has_add:
        ins.append(add)
        specs.append(o_spec)
    return pl.pallas_call(body, grid=(m // tm, n // tn, nk), in_specs=specs, out_specs=o_spec,
                          out_shape=S((m, n), out_dtype), scratch_shapes=[pltpu.VMEM((tm, tn), F32)], name=name,
                          compiler_params=_params(("parallel", "parallel", "arbitrary")))(*ins)


def _exchange(name, src, scatter):
    blk = src.shape[1:] if scatter else src.shape

    def body(src_ref, dst_ref, send_sems, recv_sems, loc_sem):
        x, y, c = lax.axis_index("x"), lax.axis_index("y"), lax.axis_index("c")
        me = 4 * x + 2 * y + c

        def mine(d):
            return src_ref.at[d] if scatter else src_ref

        local = pltpu.make_async_copy(mine(me), dst_ref.at[me], loc_sem)
        local.start()
        sends, peers = [], []
        for k in range(1, N_DEV):
            px = 1 - x if k & 4 else x
            py = 1 - y if k & 2 else y
            pc = 1 - c if k & 1 else c
            pid = 4 * px + 2 * py + pc
            cp = pltpu.make_async_remote_copy(src_ref=mine(pid), dst_ref=dst_ref.at[me], send_sem=send_sems.at[k - 1],
                                              recv_sem=recv_sems.at[k - 1], device_id=(px, py, pc), device_id_type=MESH)
            cp.start()
            sends.append(cp)
            peers.append((pid, (px, py, pc)))
        for k in range(1, N_DEV):
            pid, dev = peers[k - 1]
            pltpu.make_async_remote_copy(src_ref=mine(pid), dst_ref=dst_ref.at[pid], send_sem=send_sems.at[k - 1],
                                         recv_sem=recv_sems.at[k - 1], device_id=dev, device_id_type=MESH).wait_recv()
        for cp in sends:
            cp.wait_send()
        local.wait()

    return pl.pallas_call(
        body, out_shape=S((N_DEV,) + tuple(blk), src.dtype),
        in_specs=[pl.BlockSpec(memory_space=pl.ANY)], out_specs=pl.BlockSpec(memory_space=pl.ANY),
        scratch_shapes=[pltpu.SemaphoreType.DMA((N_DEV - 1,)), pltpu.SemaphoreType.DMA((N_DEV - 1,)),
                        pltpu.SemaphoreType.DMA],
        name=name)(src)


def _gather(name, src):
    def body(src_ref, dst_ref, send_sems, recv_sems, loc_sem):
        x, y, c = lax.axis_index("x"), lax.axis_index("y"), lax.axis_index("c")
        me, sibling = (x, y, c), (x, y, 1 - c)
        chips = [(1 - x, y), (x, 1 - y), (1 - x, 1 - y)]

        def rows(px, py, pc):
            return dst_ref.at[4 * px + 2 * py + pc]

        def copy(k, block, to, own=False):
            return pltpu.make_async_remote_copy(src_ref=src_ref if own else rows(*block), dst_ref=rows(*block),
                                                send_sem=send_sems.at[k], recv_sem=recv_sems.at[k], device_id=to,
                                                device_id_type=MESH)

        local = pltpu.make_async_copy(src_ref, rows(*me), loc_sem)
        local.start()
        first = [copy(0, me, sibling, own=True)] + [copy(1 + j, me, (*chip, c), own=True) for j, chip in enumerate(chips)]
        for cp in first:
            cp.start()
        passed = [copy(4 + j, (*chip, c), sibling) for j, chip in enumerate(chips)]
        for j, chip in enumerate(chips):
            copy(1 + j, (*chip, c), me).wait_recv()
            passed[j].start()
        copy(0, sibling, me).wait_recv()
        for j, chip in enumerate(chips):
            copy(4 + j, (*chip, 1 - c), me).wait_recv()
        for cp in first + passed:
            cp.wait_send()
        local.wait()

    return pl.pallas_call(
        body, out_shape=S((N_DEV,) + tuple(src.shape), src.dtype),
        in_specs=[pl.BlockSpec(memory_space=pl.ANY)], out_specs=pl.BlockSpec(memory_space=pl.ANY),
        scratch_shapes=[pltpu.SemaphoreType.DMA((N_DEV - 1,)), pltpu.SemaphoreType.DMA((N_DEV - 1,)),
                        pltpu.SemaphoreType.DMA],
        name=name)(src)


def _scatter(name, parts):
    _, r, c_ = parts.shape
    n_chip = N_DEV // 2

    def stage1(src_ref, dst_ref, send_sems, recv_sems):
        x, y, c = lax.axis_index("x"), lax.axis_index("y"), lax.axis_index("c")
        sends = []
        for q in range(n_chip):
            cp = pltpu.make_async_remote_copy(src_ref=src_ref.at[2 * q + 1 - c], dst_ref=dst_ref.at[q],
                                              send_sem=send_sems.at[q], recv_sem=recv_sems.at[q],
                                              device_id=(x, y, 1 - c), device_id_type=MESH)
            cp.start()
            sends.append(cp)
        for q in range(n_chip):
            pltpu.make_async_remote_copy(src_ref=src_ref.at[2 * q + c], dst_ref=dst_ref.at[q], send_sem=send_sems.at[q],
                                         recv_sem=recv_sems.at[q], device_id=(x, y, 1 - c), device_id_type=MESH).wait_recv()
        for cp in sends:
            cp.wait_send()

    from_sibling = pl.pallas_call(
        stage1, out_shape=S((n_chip, r, c_), parts.dtype),
        in_specs=[pl.BlockSpec(memory_space=pl.ANY)], out_specs=pl.BlockSpec(memory_space=pl.ANY),
        scratch_shapes=[pltpu.SemaphoreType.DMA((n_chip,)), pltpu.SemaphoreType.DMA((n_chip,))], name=name + "_pair")(parts)

    tm = _pick(r, (512, 256, 128, 64, 32, 16, 8)) if r % 8 == 0 else r
    core = lax.axis_index("c").astype(jnp.int32).reshape(1)

    def pair_sum(core_ref, mine_ref, sib_ref, o_ref):
        o_ref[...] = (mine_ref[0].astype(F32) + sib_ref[...].astype(F32)).astype(o_ref.dtype)

    pair = pl.pallas_call(
        pair_sum, out_shape=S((n_chip, r, c_), parts.dtype),
        grid_spec=pltpu.PrefetchScalarGridSpec(
            num_scalar_prefetch=1, grid=(n_chip, r // tm),
            in_specs=[pl.BlockSpec((1, 1, tm, c_), lambda q, i, core_ref: (q, core_ref[0], i, 0)),
                      pl.BlockSpec((1, tm, c_), lambda q, i, core_ref: (q, i, 0))],
            out_specs=pl.BlockSpec((1, tm, c_), lambda q, i, core_ref: (q, i, 0))),
        name=name + "_sum", compiler_params=_params(("parallel", "parallel")))(
            core, parts.reshape(n_chip, 2, r, c_), from_sibling)

    def stage2(src_ref, dst_ref, send_sems, recv_sems, loc_sem):
        x, y, c = lax.axis_index("x"), lax.axis_index("y"), lax.axis_index("c")
        mine = 2 * x + y
        local = pltpu.make_async_copy(src_ref.at[mine], dst_ref.at[mine], loc_sem)
        local.start()
        sends, chips = [], [(1 - x, y), (x, 1 - y), (1 - x, 1 - y)]
        for j, (px, py) in enumerate(chips):
            cp = pltpu.make_async_remote_copy(src_ref=src_ref.at[2 * px + py], dst_ref=dst_ref.at[mine],
                                              send_sem=send_sems.at[j], recv_sem=recv_sems.at[j],
                                              device_id=(px, py, c), device_id_type=MESH)
            cp.start()
            sends.append(cp)
        for j, (px, py) in enumerate(chips):
            pltpu.make_async_remote_copy(src_ref=src_ref.at[mine], dst_ref=dst_ref.at[2 * px + py], send_sem=send_sems.at[j],
                                         recv_sem=recv_sems.at[j], device_id=(px, py, c), device_id_type=MESH).wait_recv()
        for cp in sends:
            cp.wait_send()
        local.wait()

    return pl.pallas_call(
        stage2, out_shape=S((n_chip, r, c_), parts.dtype),
        in_specs=[pl.BlockSpec(memory_space=pl.ANY)], out_specs=pl.BlockSpec(memory_space=pl.ANY),
        scratch_shapes=[pltpu.SemaphoreType.DMA((n_chip - 1,)), pltpu.SemaphoreType.DMA((n_chip - 1,)),
                        pltpu.SemaphoreType.DMA], name=name + "_chips")(pair)


def _sum_parts(name, parts):
    n_parts, r, c = parts.shape
    tm = _pick(r, (512, 256, 128, 64, 32, 16, 8)) if r % 8 == 0 else r

    def body(p_ref, o_ref):
        acc = p_ref[0].astype(F32)
        for d in range(1, n_parts):
            acc = acc + p_ref[d].astype(F32)
        o_ref[...] = acc

    return pl.pallas_call(body, grid=(r // tm,), in_specs=[pl.BlockSpec((n_parts, tm, c), lambda i: (0, i, 0))],
                          out_specs=pl.BlockSpec((tm, c), lambda i: (i, 0)), out_shape=S((r, c), F32), name=name,
                          compiler_params=_params(("parallel",)))(parts)


def _adamw(name, w, g, m, v):
    r, c = w.shape
    parts = g.ndim == 3
    n_parts = g.shape[0] if parts else 1
    tm = _pick(r, (256, 128, 64, 32, 16, 8)) if r % 8 == 0 else r

    def body(w_ref, g_ref, m_ref, v_ref, go_ref, d_ref, mo_ref, vo_ref):
        if parts:
            gg = g_ref[0].astype(F32)
            for d in range(1, n_parts):
                gg = gg + g_ref[d].astype(F32)
        else:
            gg = g_ref[...]
        mm = ADAM_B1 * m_ref[...] + (1.0 - ADAM_B1) * gg
        vv = ADAM_B2 * v_ref[...] + (1.0 - ADAM_B2) * jnp.square(gg)
        m_hat = mm / (1.0 - ADAM_B1 ** ADAM_STEP)
        v_hat = vv / (1.0 - ADAM_B2 ** ADAM_STEP)
        go_ref[...] = gg
        d_ref[...] = -ADAM_LR * (m_hat / (jnp.sqrt(v_hat) + ADAM_EPS) + ADAM_WD * w_ref[...])
        mo_ref[...] = mm
        vo_ref[...] = vv

    spec = pl.BlockSpec((tm, c), lambda i: (i, 0))
    g_spec = pl.BlockSpec((n_parts, tm, c), lambda i: (0, i, 0)) if parts else spec
    return pl.pallas_call(body, grid=(r // tm,), in_specs=[spec, g_spec, spec, spec], out_specs=[spec] * 4,
                          out_shape=[S((r, c), F32)] * 4, name=name, compiler_params=_params(("parallel",)))(w, g, m, v)


def _rms(x, g):
    return x * lax.rsqrt(jnp.mean(x * x, axis=-1, keepdims=True) + RMS_EPS) * g


def _adaln(x, g, shift, scale):
    return _rms(x, g) * (1.0 + scale) + shift


def _dot(a, b, dn="nn", hi=False, prec=None):
    if hi or prec is not None:
        return lax.dot_general(a, b, _DN[dn], precision=HI if hi else prec, preferred_element_type=F32)
    return lax.dot_general(a.astype(BF16), b.astype(BF16), _DN[dn], preferred_element_type=F32)


def _sg_mix(p, ln_g, ln_b, w_s, b_st):
    d = p.shape[1] // 3
    gd = d // SG_GROUPS
    u = jax.nn.gelu(p[:, :d])
    vf = jax.nn.gelu(p[:, d:2 * d])
    z = p[:, 2 * d:]
    mean = jnp.mean(vf, axis=-1, keepdims=True)
    var = jnp.mean(jnp.square(vf - mean), axis=-1, keepdims=True)
    vn = (vf - mean) * lax.rsqrt(var + LN_EPS) * ln_g + ln_b
    row = lax.broadcasted_iota(jnp.int32, (SG_CHUNK, SG_CHUNK), 0)
    col = lax.broadcasted_iota(jnp.int32, (SG_CHUNK, SG_CHUNK), 1)
    fs = []
    for g in range(SG_GROUPS):
        w = jnp.where(row >= col, w_s[g], 0.0)
        fs.append(_dot(w, vn[:, g * gd:(g + 1) * gd]))
    sel = (lax.broadcasted_iota(jnp.int32, (SG_GROUPS, d), 1) // gd
           == lax.broadcasted_iota(jnp.int32, (SG_GROUPS, d), 0)).astype(F32)
    f = jnp.concatenate(fs, axis=1) + _dot(b_st, sel, hi=True)
    return u * f * jax.nn.silu(z)


def _rot_half(x):
    n = x.shape[1]
    lane = lax.broadcasted_iota(jnp.int32, x.shape, 1)
    return jnp.where(lane % HEAD < HEAD // 2, -pltpu.roll(x, n - HEAD // 2, 1), pltpu.roll(x, HEAD // 2, 1))


def _rope(x, cos, sin, sign):
    reps = x.shape[1] // cos.shape[1]
    return x * jnp.tile(cos, (1, reps)) + sign * _rot_half(x) * jnp.tile(sin, (1, reps))


def _attn_block(q, kp, kc, vp, vc, sink, prev_bias):
    r = q.shape[0]
    q2 = q.reshape(r * SWA_BLOCK, HEAD)
    sp = (_dot(q2, kp, "nt") * (HEAD ** -0.5)).reshape(r, SWA_BLOCK, SWA_BLOCK)
    sc = (_dot(q2, kc, "nt") * (HEAD ** -0.5)).reshape(r, SWA_BLOCK, SWA_BLOCK)
    qi = lax.broadcasted_iota(jnp.int32, (r, SWA_BLOCK, SWA_BLOCK), 1)
    kj = lax.broadcasted_iota(jnp.int32, (r, SWA_BLOCK, SWA_BLOCK), 2)
    sp = jnp.where(kj > qi, sp, NEG) + prev_bias
    sc = jnp.where(kj <= qi, sc, NEG)
    m = jnp.maximum(jnp.maximum(jnp.max(sp, axis=-1, keepdims=True), jnp.max(sc, axis=-1, keepdims=True)), sink)
    ep, ec = jnp.exp(sp - m), jnp.exp(sc - m)
    denom = jnp.sum(ep, axis=-1, keepdims=True) + jnp.sum(ec, axis=-1, keepdims=True) + jnp.exp(sink - m)
    pp = (ep / denom).reshape(r * SWA_BLOCK, SWA_BLOCK)
    pc = (ec / denom).reshape(r * SWA_BLOCK, SWA_BLOCK)
    return (_dot(pp, vp) + _dot(pc, vc)).reshape(r, SWA_BLOCK, HEAD)


def _rwkv_chunk(s0, r, k, v, logw, a, k_k, k_a, r_k, gn_g, gn_b):
    c = r[0].shape[0]
    each = lambda f, *ls: [f(*xs) for xs in zip(*ls)]
    gram = functools.partial(_dot, prec=RW_PREC)
    row = lax.broadcasted_iota(jnp.int32, (c, c), 0)
    col = lax.broadcasted_iota(jnp.int32, (c, c), 1)
    incl, strict = row >= col, row > col
    ones_l = incl.astype(F32)

    def unit(x):
        return x / jnp.maximum(jnp.sqrt(jnp.sum(x * x, axis=-1, keepdims=True)), 1e-12)

    kk = each(lambda k_, p: unit(k_ * p), k, k_k)
    km = each(lambda k_, a_, p: k_ * (1.0 + (a_ - 1.0) * p), k, a, k_a)
    b = each(lambda x, a_: x * a_, kk, a)
    cum = each(lambda w: _dot(ones_l, w, hi=True), logw)
    alpha = each(lambda x, cu, w: x * jnp.exp(cu - w), kk, cum, logw)
    beta = each(lambda x, cu: x * jnp.exp(-cu), b, cum)
    kap = each(lambda x, cu: x * jnp.exp(-cu), km, cum)
    rho = each(lambda x, cu: x * jnp.exp(cu), r, cum)
    lab = each(lambda x, y_: jnp.where(strict, gram(x, y_, "nt"), 0.0), alpha, beta)
    lak = each(lambda x, y_: jnp.where(strict, gram(x, y_, "nt"), 0.0), alpha, kap)
    xs = each(lambda al, s, l, v_: _dot(al, s, "nt") + _dot(l, v_), alpha, s0, lak, v)
    xs = each(lambda x, l: x - _dot(l, x), xs, lab)
    lp, power = lab, 2
    while power < c:
        lp = each(lambda l: _dot(l, l), lp)
        xs = each(lambda x, l: x + _dot(l, x), xs, lp)
        power *= 2
    u = each(lambda x: -x, xs)
    mrb = each(lambda x, y_: jnp.where(incl, gram(x, y_, "nt"), 0.0), rho, beta)
    mrk = each(lambda x, y_: jnp.where(incl, gram(x, y_, "nt"), 0.0), rho, kap)
    y = each(lambda rh, s, mb, u_, mk, v_: _dot(rh, s, "nt") + _dot(mb, u_) + _dot(mk, v_), rho, s0, mrb, u, mrk, v)
    s1 = each(lambda s, u_, be, v_, ka, w: (s + _dot(u_, be, "tn") + _dot(v_, ka, "tn")) * jnp.exp(jnp.sum(w, axis=0, keepdims=True)),
              s0, u, beta, v, kap, logw)

    def finish(y_, g, bias, r_, km_, rk, v_):
        mean = jnp.mean(y_, axis=-1, keepdims=True)
        var = jnp.mean(jnp.square(y_ - mean), axis=-1, keepdims=True)
        y_ = (y_ - mean) * lax.rsqrt(var + GN_EPS) * g + bias
        return y_ + jnp.sum(r_ * km_ * rk, axis=-1, keepdims=True) * v_

    return each(finish, y, gn_g, gn_b, r, km, r_k, v), s1


def _norm_fwd(name, x, g, shift, scale):
    return _rows(name, lambda x_, g_, sh, sc: _adaln(x_, g_, sh, sc), [x], [g, shift, scale], [(x.shape[1], BF16)], [], 256)[0]


def _norm_bwd(name, x, dh, dx_res, g, shift, scale):
    d = x.shape[1]

    def fn(x_, dh_, dr_, g_, sh, sc):
        _, vjp = jax.vjp(_adaln, x_, g_, sh, sc)
        dx, dg, dsh, dsc = vjp(dh_)
        return dx + dr_, dg, dsh, dsc

    return _rows(name, fn, [x, dh, dx_res], [g, shift, scale], [(d, F32)], [(1, d)] * 3, 256)


def _resid_fwd(name, x, y, gate):
    return _rows(name, lambda x_, y_, g_: x_ + g_ * y_, [x, y], [gate], [(x.shape[1], F32)], [], 256)[0]


def _resid_bwd(name, dx, y, gate):
    d = dx.shape[1]
    return _rows(name, lambda dx_, y_, g_: (g_ * dx_, jnp.sum(dx_ * y_, axis=0, keepdims=True)), [dx, y], [gate],
                 [(d, BF16)], [(1, d)], 256)


def _sg_fwd(name, p, ln_g, ln_b, w_s, b_st):
    d = p.shape[1] // 3
    return _rows(name, _sg_mix, [p], [ln_g, ln_b, w_s, b_st], [(d, BF16)], [], SG_CHUNK)[0]


def _sg_bwd(name, p, dmix, ln_g, ln_b, w_s, b_st):
    def fn(p_, dm_, lg, lb, ws, bs):
        _, vjp = jax.vjp(_sg_mix, p_, lg, lb, ws, bs)
        return vjp(dm_)

    return _rows(name, fn, [p, dmix], [ln_g, ln_b, w_s, b_st], [(p.shape[1], BF16)],
                 [ln_g.shape, ln_b.shape, w_s.shape, b_st.shape], SG_CHUNK)


def _rope_tables(pos, inv_freq):
    ang = pos * inv_freq
    return jnp.cos(ang), jnp.sin(ang)


def _swa_pre(name, p, pos, inv_freq, d):
    kvw = SWA_KV * HEAD

    def fn(p_, pos_, fr):
        cos, sin = _rope_tables(pos_, fr)
        return (_rope(p_[:, :d], cos, sin, 1.0), _rope(p_[:, d:d + kvw], cos, sin, 1.0), p_[:, d + kvw:d + 2 * kvw])

    return _rows(name, fn, [p, pos], [inv_freq], [(d, BF16), (kvw, BF16), (kvw, BF16)], [], 256)


def _swa_attn_fwd(name, q, k, v, sinks):
    kv, r, t, _ = q.shape
    nb = t // SWA_BLOCK

    def body(q_ref, kp_ref, kc_ref, vp_ref, vc_ref, s_ref, o_ref):
        prev_bias = jnp.where(pl.program_id(1) > 0, 0.0, NEG).astype(F32)
        o_ref[0] = _attn_block(q_ref[0], kp_ref[0], kc_ref[0], vp_ref[0], vc_ref[0], s_ref[0], prev_bias)

    qs = pl.BlockSpec((1, r, SWA_BLOCK, HEAD), lambda g, n: (g, 0, n, 0))
    cur = pl.BlockSpec((1, SWA_BLOCK, HEAD), lambda g, n: (g, n, 0))
    prev = pl.BlockSpec((1, SWA_BLOCK, HEAD), lambda g, n: (g, jnp.maximum(n - 1, 0), 0))
    ss = pl.BlockSpec((1, r, 1, 1), lambda g, n: (g, 0, 0, 0))
    return pl.pallas_call(body, grid=(kv, nb), in_specs=[qs, prev, cur, prev, cur, ss], out_specs=qs,
                          out_shape=S(q.shape, F32), name=name,
                          compiler_params=_params(("parallel", "arbitrary")))(q, k, k, v, v, sinks)


def _swa_attn_bwd(name, q, k, v, sinks, do):
    kv, r, t, _ = q.shape
    nb = t // SWA_BLOCK

    def body(q_ref, kp_ref, kc_ref, vp_ref, vc_ref, s_ref, do_ref, dq_ref, dkc_ref, dkp_ref, dvc_ref, dvp_ref, ds_ref):
        n = pl.program_id(1)
        prev_bias = jnp.where(n > 0, 0.0, NEG).astype(F32)
        fn = functools.partial(_attn_block, prev_bias=prev_bias)
        args = [ref[0].astype(F32) for ref in (q_ref, kp_ref, kc_ref, vp_ref, vc_ref)] + [s_ref[0]]
        _, vjp = jax.vjp(fn, *args)
        dq, dkp, dkc, dvp, dvc, ds = vjp(do_ref[0])
        dq_ref[0], dkc_ref[0], dkp_ref[0], dvc_ref[0], dvp_ref[0] = dq, dkc, dkp, dvc, dvp

        @pl.when(n == 0)
        def _():
            ds_ref[0] = ds

        @pl.when(n > 0)
        def _():
            ds_ref[0] += ds

    qs = pl.BlockSpec((1, r, SWA_BLOCK, HEAD), lambda g, n: (g, 0, n, 0))
    cur = pl.BlockSpec((1, SWA_BLOCK, HEAD), lambda g, n: (g, n, 0))
    prev = pl.BlockSpec((1, SWA_BLOCK, HEAD), lambda g, n: (g, jnp.maximum(n - 1, 0), 0))
    ss = pl.BlockSpec((1, r, 1, 1), lambda g, n: (g, 0, 0, 0))
    return pl.pallas_call(
        body, grid=(kv, nb), in_specs=[qs, prev, cur, prev, cur, ss, qs], out_specs=[qs, cur, cur, cur, cur, ss],
        out_shape=[S(q.shape, F32)] + [S(k.shape, F32)] * 4 + [S(sinks.shape, F32)], name=name,
        compiler_params=_params(("parallel", "arbitrary")))(q, k, k, v, v, sinks, do)


def _gate_fwd(name, o, z_src, z_off, d):
    return _rows(name, lambda o_, p_: o_ * jax.nn.silu(p_[:, z_off:z_off + d]), [o, z_src], [], [(d, BF16)], [], 256)[0]


def _gate_bwd(name, o, z_src, z_off, d, dmix):
    def fn(o_, p_, dm_):
        _, vjp = jax.vjp(lambda oo, zz: oo * jax.nn.silu(zz), o_, p_[:, z_off:z_off + d])
        return vjp(dm_)

    return _rows(name, fn, [o, z_src, dmix], [], [(d, F32), (d, F32)], [], 256)


def _swa_post_bwd(name, dq, dkc, dkp_up, dvc, dvp_up, dz, pos, inv_freq):
    def fn(dq_, dkc_, dkp_, dvc_, dvp_, dz_, pos_, fr):
        cos, sin = _rope_tables(pos_, fr)
        return jnp.concatenate([_rope(dq_, cos, sin, -1.0), _rope(dkc_ + dkp_, cos, sin, -1.0), dvc_ + dvp_, dz_], axis=1)

    n = dq.shape[1] + dkc.shape[1] + dvc.shape[1] + dz.shape[1]
    return _rows(name, fn, [dq, dkc, dkp_up, dvc, dvp_up, dz, pos], [inv_freq], [(n, BF16)], [], 256)[0]


HALO = 8


def _row_before(x, halo_ref, i):
    first = jnp.where(i > 0, halo_ref[pl.ds(HALO - 1, 1), :], 0.0)
    row = lax.broadcasted_iota(jnp.int32, x.shape, 0)
    return jnp.where(row == 0, first, pltpu.roll(x, 1, 0))


def _row_after(x, halo, i, n_tiles):
    last = jnp.where(i < n_tiles - 1, halo, 0.0)
    row = lax.broadcasted_iota(jnp.int32, x.shape, 0)
    return jnp.where(row == x.shape[0] - 1, last, pltpu.roll(x, x.shape[0] - 1, 0))


def _lerp_fwd(name, p, mu, widths):
    t, n = p.shape
    tm = 128

    def body(p_ref, halo_ref, mu_ref, *o_refs):
        x = p_ref[...]
        pm = x + (_row_before(x, halo_ref, pl.program_id(0)) - x) * mu_ref[...]
        o = 0
        for ref, w in zip(o_refs, widths):
            ref[...] = pm[:, o:o + w]
            o += w

    return pl.pallas_call(
        body, grid=(t // tm,),
        in_specs=[pl.BlockSpec((tm, n), lambda i: (i, 0)),
                  pl.BlockSpec((HALO, n), lambda i: (jnp.maximum(i * (tm // HALO) - 1, 0), 0)),
                  pl.BlockSpec((1, n), lambda i: (0, 0))],
        out_specs=[pl.BlockSpec((tm, w), lambda i: (i, 0)) for w in widths],
        out_shape=[S((t, w), F32) for w in widths], name=name, compiler_params=_params(("parallel",)))(p, p, mu)


def _lerp_bwd(name, dpm_parts, p, mu):
    t, n = p.shape
    k = len(dpm_parts)
    tm = 64
    n_tiles = t // tm

    def body(*refs):
        d_refs, dh_refs = refs[:k], refs[k:2 * k]
        p_ref, ph_ref, mu_ref, dp_ref, dmu_ref = refs[2 * k:]
        i = pl.program_id(0)
        cat = lambda vals: jnp.concatenate(vals, axis=1) if k > 1 else vals[0]
        dpm = cat([r[...] for r in d_refs])
        dnext = cat([r[pl.ds(0, 1), :] for r in dh_refs])
        x, mu_ = p_ref[...], mu_ref[...]
        dp_ref[...] = (dpm * (1.0 - mu_) + _row_after(dpm, dnext, i, n_tiles) * mu_).astype(dp_ref.dtype)
        dmu = jnp.sum(dpm * (_row_before(x, ph_ref, i) - x), axis=0, keepdims=True)

        @pl.when(i == 0)
        def _():
            dmu_ref[...] = dmu

        @pl.when(i > 0)
        def _():
            dmu_ref[...] += dmu

    per = tm // HALO
    d_specs = [pl.BlockSpec((tm, a.shape[1]), lambda i: (i, 0)) for a in dpm_parts]
    dh_specs = [pl.BlockSpec((HALO, a.shape[1]), lambda i: (jnp.minimum((i + 1) * per, t // HALO - 1), 0)) for a in dpm_parts]
    return pl.pallas_call(
        body, grid=(n_tiles,),
        in_specs=d_specs + dh_specs + [pl.BlockSpec((tm, n), lambda i: (i, 0)),
                                       pl.BlockSpec((HALO, n), lambda i: (jnp.maximum(i * per - 1, 0), 0)),
                                       pl.BlockSpec((1, n), lambda i: (0, 0))],
        out_specs=[pl.BlockSpec((tm, n), lambda i: (i, 0)), pl.BlockSpec((1, n), lambda i: (0, 0))],
        out_shape=[S((t, n), BF16), S((1, n), F32)], name=name,
        compiler_params=_params(("arbitrary",)))(*dpm_parts, *dpm_parts, p, p, mu)


def _lora_act(pl_, w0, w_lora, a0, a_lora):
    logw = -DECAY_SCALE * jax.nn.sigmoid(w0 + _dot(jnp.tanh(pl_), w_lora))
    a = jax.nn.sigmoid(a0 + _dot(pl_, a_lora))
    return logw, a


def _lora_fwd(name, pl_, w0, w_lora, a0, a_lora):
    d = w0.shape[1]
    return _rows(name, _lora_act, [pl_], [w0, w_lora, a0, a_lora], [(d, F32), (d, F32)], [], 256)


def _lora_bwd(name, pl_, dlogw, da, w0, w_lora, a0, a_lora):
    def fn(p_, dl_, da_, w0_, wl_, a0_, al_):
        _, vjp = jax.vjp(_lora_act, p_, w0_, wl_, a0_, al_)
        return vjp((dl_, da_))

    return _rows(name, fn, [pl_, dlogw, da], [w0, w_lora, a0, a_lora], [(pl_.shape[1], F32)],
                 [w0.shape, w_lora.shape, a0.shape, a_lora.shape], 256)


def _head_cols(ref, hb):
    x = ref[...]
    xo = pltpu.roll(x, x.shape[1] - HEAD, 1)
    return [(x if j % 2 == 0 else xo)[:, 2 * HEAD * (j // 2):2 * HEAD * (j // 2) + HEAD] for j in range(hb)]


def _rwkv_scan_fwd(name, r, k, v, logw, a, hp):
    t, d = r.shape
    h, nc, hb = d // HEAD, t // RW_CHUNK, RW_HEADS

    def body(r_ref, k_ref, v_ref, w_ref, a_ref, kk_ref, ka_ref, rk_ref, gg_ref, gb_ref, y_ref, st_ref, s_scr):
        @pl.when(pl.program_id(1) == 0)
        def _():
            s_scr[...] = jnp.zeros_like(s_scr)

        s0 = [s_scr[j] for j in range(hb)]
        for j in range(hb):
            st_ref[j, 0] = s0[j]
        y, s1 = _rwkv_chunk(s0, *[_head_cols(ref, hb) for ref in (r_ref, k_ref, v_ref, w_ref, a_ref, kk_ref, ka_ref, rk_ref,
                                                                 gg_ref, gb_ref)])
        y_ref[...] = jnp.concatenate(y, axis=1)
        for j in range(hb):
            s_scr[j] = s1[j]

    seq = pl.BlockSpec((RW_CHUNK, hb * HEAD), lambda i, n: (n, i))
    par = pl.BlockSpec((1, hb * HEAD), lambda i, n: (0, i))
    st = pl.BlockSpec((hb, 1, HEAD, HEAD), lambda i, n: (i, n, 0, 0))
    return pl.pallas_call(body, grid=(h // hb, nc), in_specs=[seq] * 5 + [par] * 5, out_specs=[seq, st],
                          out_shape=[S((t, d), F32), S((h, nc, HEAD, HEAD), F32)],
                          scratch_shapes=[pltpu.VMEM((hb, HEAD, HEAD), F32)], name=name,
                          compiler_params=_params(("parallel", "arbitrary")))(r, k, v, logw, a, *hp)


def _rwkv_scan_bwd(name, r, k, v, logw, a, hp, states, dy):
    t, d = r.shape
    h, nc, hb = d // HEAD, t // RW_CHUNK, RW_HEADS

    def body(r_ref, k_ref, v_ref, w_ref, a_ref, kk_ref, ka_ref, rk_ref, gg_ref, gb_ref, st_ref, dy_ref,
             dr_ref, dk_ref, dv_ref, dw_ref, da_ref, dkk_ref, dka_ref, drk_ref, dgg_ref, dgb_ref, ds_scr):
        n = pl.program_id(1)

        @pl.when(n == 0)
        def _():
            ds_scr[...] = jnp.zeros_like(ds_scr)
            for ref in (dkk_ref, dka_ref, drk_ref, dgg_ref, dgb_ref):
                ref[...] = jnp.zeros_like(ref)

        ins = [[st_ref[j, 0] for j in range(hb)]] + [_head_cols(ref, hb) for ref in (r_ref, k_ref, v_ref, w_ref, a_ref, kk_ref,
                                                                                  ka_ref, rk_ref, gg_ref, gb_ref)]
        _, vjp = jax.vjp(_rwkv_chunk, *ins)
        ds0, *dseq, dkk, dka, drk, dgg, dgb = vjp((_head_cols(dy_ref, hb), [ds_scr[j] for j in range(hb)]))
        for j in range(hb):
            ds_scr[j] = ds0[j]
        for ref, val in zip((dr_ref, dk_ref, dv_ref, dw_ref, da_ref), dseq):
            ref[...] = jnp.concatenate(val, axis=1)
        for ref, val in ((dkk_ref, dkk), (dka_ref, dka), (drk_ref, drk), (dgg_ref, dgg), (dgb_ref, dgb)):
            ref[...] += jnp.concatenate(val, axis=1)

    seq = pl.BlockSpec((RW_CHUNK, hb * HEAD), lambda i, n: (nc - 1 - n, i))
    par = pl.BlockSpec((1, hb * HEAD), lambda i, n: (0, i))
    st = pl.BlockSpec((hb, 1, HEAD, HEAD), lambda i, n: (i, nc - 1 - n, 0, 0))
    return pl.pallas_call(body, grid=(h // hb, nc), in_specs=[seq] * 5 + [par] * 5 + [st, seq], out_specs=[seq] * 5 + [par] * 5,
                          out_shape=[S((t, d), F32)] * 5 + [S((1, d), F32)] * 5,
                          scratch_shapes=[pltpu.VMEM((hb, HEAD, HEAD), F32)], name=name,
                          compiler_params=_params(("parallel", "arbitrary")))(r, k, v, logw, a, *hp, states, dy)


def _loss_head(name, x, target, g):
    d = x.shape[1]

    def fn(x_, t_, g_):
        def f(xx, gg):
            err = _rms(xx, gg) - t_
            return 0.5 * jnp.sum(jnp.mean(err * err, axis=-1, keepdims=True), axis=0, keepdims=True)

        l, vjp = jax.vjp(f, x_, g_)
        dx, dg = vjp(jnp.ones((1, 1), F32))
        return dx, dg, jnp.broadcast_to(l, (1, 128))

    return _rows(name, fn, [x, target], [g], [(d, F32)], [(1, d), (1, 128)], 256)


def _mod_fwd(name, cond_all, mod_w, mod_b_cols):
    l, d, n = mod_w.shape

    def body(c_ref, w_ref, b_ref, o_ref):
        o_ref[0] = _dot(jax.nn.silu(c_ref[...]), w_ref[0], hi=True) + b_ref[0]

    return pl.pallas_call(body, grid=(l,), in_specs=[pl.BlockSpec((N_DEV, d), lambda i: (0, 0)),
                                                      pl.BlockSpec((1, d, n), lambda i: (i, 0, 0)),
                                                      pl.BlockSpec((1, 1, n), lambda i: (i, 0, 0))],
                          out_specs=pl.BlockSpec((1, N_DEV, n), lambda i: (i, 0, 0)), out_shape=S((l, N_DEV, n), F32),
                          name=name, compiler_params=_params(("parallel",)))(cond_all, mod_w, mod_b_cols)


def _mod_bwd(name, cond_all, dmod_cols, dmod_all):
    l, _, n = dmod_cols.shape
    d = cond_all.shape[1]
    nb = dmod_all.shape[2]

    def body(c_ref, dc_ref, da_ref, gw_ref, gb_ref):
        gw_ref[0] = _dot(jax.nn.silu(c_ref[...]), dc_ref[0], "tn", hi=True)
        acc = da_ref[0, 0:1, :]
        for bi in range(1, N_DEV):
            acc = acc + da_ref[0, bi:bi + 1, :]
        gb_ref[0] = acc

    return pl.pallas_call(body, grid=(l,), in_specs=[pl.BlockSpec((N_DEV, d), lambda i: (0, 0)),
                                                      pl.BlockSpec((1, N_DEV, n), lambda i: (i, 0, 0)),
                                                      pl.BlockSpec((1, N_DEV, nb), lambda i: (i, 0, 0))],
                          out_specs=[pl.BlockSpec((1, d, n), lambda i: (i, 0, 0)), pl.BlockSpec((1, 1, nb), lambda i: (i, 0, 0))],
                          out_shape=[S((l, d, n), F32), S((l, 1, nb), F32)], name=name,
                          compiler_params=_params(("parallel",)))(cond_all, dmod_cols, dmod_all)


def _to_heads(a):
    t, n = a.shape
    return a.reshape(t, n // HEAD, HEAD).transpose(1, 0, 2)


def _from_heads(a):
    h, t, _ = a.shape
    return a.transpose(1, 0, 2).reshape(t, h * HEAD)


def _shift_up(a, n=1):
    return jnp.concatenate([a[n:], jnp.zeros_like(a[:n])], axis=0)


def _cols_full(g):
    return g.transpose(1, 0, 2).reshape(g.shape[1], -1)


def _cols_parts(full):
    r, n = full.shape
    return full.reshape(r, N_DEV, n // N_DEV).transpose(1, 0, 2)


def _pack(arrs, mult=1024):
    flat = jnp.concatenate([a.reshape(-1) for a in arrs])
    pad = (-flat.shape[0]) % mult
    return jnp.pad(flat, (0, pad)).reshape(-1, 128)


def _unpack(flat, shapes):
    out, o = [], 0
    for s in shapes:
        n = math.prod(s)
        out.append(flat[o:o + n].reshape(s))
        o += n
    return out


def _local_step(x, pos, target, mods, norm_g, final_norm_g, sg, swa, rw):
    t, d = x.shape
    kinds = [i % 3 for i in range(DEPTH)]
    inv_freq = (ROPE_THETA ** (-jnp.arange(HEAD // 2, dtype=F32) / (HEAD // 2)))
    inv_freq = jnp.tile(inv_freq, 128 // (HEAD // 2)).reshape(1, 128)
    saved = []
    for i, kind in enumerate(kinds):
        j = i // 3
        shift, scale, gate = (mods[i, q * d:(q + 1) * d].reshape(1, d) for q in range(3))
        g = norm_g[i].reshape(1, d)
        h = _norm_fwd(f"norm_fwd{i}", x, g, shift, scale)
        sv = dict(x=x, h=h, g=g, shift=shift, scale=scale, gate=gate)
        if kind == 0:
            p = _mm(f"sg_in{i}", h, sg["w_in"][j], "nn", F32)
            mix = _sg_fwd(f"sg_mix{i}", p, sg["ln_g"][j], sg["ln_b"][j], sg["w_s"][j], sg["b_st"][j])
            w_out = sg["w_out"][j]
            sv.update(p=p)
        elif kind == 1:
            p = _mm(f"swa_in{i}", h, swa["w_in"], "nn", F32)
            q, k, v = _swa_pre(f"swa_pre{i}", p, pos, inv_freq, d)
            qh = _to_heads(q).reshape(SWA_KV, SWA_REP, t, HEAD)
            kh, vh = _to_heads(k), _to_heads(v)
            o = _swa_attn_fwd(f"swa_attn{i}", qh, kh, vh, swa["sinks"])
            o = _from_heads(o.reshape(SWA_KV * SWA_REP, t, HEAD))
            mix = _gate_fwd(f"swa_gate{i}", o, p, d + 2 * SWA_KV * HEAD, d)
            w_out = swa["w_out"]
            sv.update(p=p, qh=qh, kh=kh, vh=vh, o=o)
        else:
            pm = _mm(f"rw_in{i}", h, rw["w_main"], "nn", F32)
            plo = _mm(f"rw_inl{i}", h, rw["w_lorain"], "nn", F32)
            r, k, v, z = _lerp_fwd(f"rw_lerp{i}", pm, rw["mu_main"], [d] * 4)
            (pll,) = _lerp_fwd(f"rw_lerpl{i}", plo, rw["mu_lora"], [LORA_PAD])
            logw, a = _lora_fwd(f"rw_lora{i}", pll, rw["w0"], rw["w_lora"], rw["a0"], rw["a_lora"])
            seqs = (r, k, v, logw, a)
            o, states = _rwkv_scan_fwd(f"rw_scan{i}", *seqs, rw["hp"])
            mix = _gate_fwd(f"rw_gate{i}", o, z, 0, d)
            w_out = rw["w_out"]
            sv.update(pm=pm, plo=plo, pll=pll, z=z, seqs=seqs, states=states, o=o)
        y = _mm(f"out{i}", mix, w_out, "nn", F32)
        sv.update(mix=mix, y=y)
        saved.append(sv)
        x = _resid_fwd(f"resid{i}", x, y, gate)

    dx, d_final_g, loss = _loss_head("loss_head", x, target, final_norm_g.reshape(1, d))

    grads = dict(norm_g=[None] * DEPTH, sg_w_in=[None] * 2, sg_w_out=[None] * 2, sg_ln_g=[None] * 2, sg_ln_b=[None] * 2,
                 sg_w_s=[None] * 2, sg_b_st=[None] * 2, final_norm_g=d_final_g)
    dmods = [None] * DEPTH
    for i in reversed(range(DEPTH)):
        kind, j, sv = kinds[i], i // 3, saved[i]
        dy, dgate = _resid_bwd(f"resid_bwd{i}", dx, sv["y"], sv["gate"])
        w_out = (sg["w_out"][j], swa["w_out"], rw["w_out"])[kind]
        d_w_out = _mm(f"out_dw{i}", sv["mix"], dy, "tn", F32)
        dmix = _mm(f"out_dx{i}", dy, w_out, "nt", F32)
        if kind == 0:
            dp, dlg, dlb, dws, dbs = _sg_bwd(f"sg_mix_bwd{i}", sv["p"], dmix, sg["ln_g"][j], sg["ln_b"][j],
                                            sg["w_s"][j], sg["b_st"][j])
            grads["sg_ln_g"][j], grads["sg_ln_b"][j], grads["sg_w_s"][j], grads["sg_b_st"][j] = dlg, dlb, dws, dbs
            grads["sg_w_out"][j] = d_w_out
            grads["sg_w_in"][j] = _mm(f"sg_in_dw{i}", sv["h"], dp, "tn", F32)
            dh = _mm(f"sg_in_dx{i}", dp, sg["w_in"][j], "nt", F32)
        elif kind == 1:
            z_off = d + 2 * SWA_KV * HEAD
            do, dz = _gate_bwd(f"swa_gate_bwd{i}", sv["o"], sv["p"], z_off, d, dmix)
            doh = _to_heads(do).reshape(SWA_KV, SWA_REP, t, HEAD)
            dq, dkc, dkp, dvc, dvp, dsinks = _swa_attn_bwd(f"swa_attn_bwd{i}", sv["qh"], sv["kh"], sv["vh"], swa["sinks"], doh)
            dq = _from_heads(dq.reshape(SWA_KV * SWA_REP, t, HEAD))
            dkc, dvc = _from_heads(dkc), _from_heads(dvc)
            dkp, dvp = _shift_up(_from_heads(dkp), SWA_BLOCK), _shift_up(_from_heads(dvp), SWA_BLOCK)
            dp = _swa_post_bwd(f"swa_post_bwd{i}", dq, dkc, dkp, dvc, dvp, dz, pos, inv_freq)
            grads.update(swa_sinks=dsinks, swa_w_out=d_w_out)
            grads["swa_w_in"] = _mm(f"swa_in_dw{i}", sv["h"], dp, "tn", F32)
            dh = _mm(f"swa_in_dx{i}", dp, swa["w_in"], "nt", F32)
        else:
            do, dz = _gate_bwd(f"rw_gate_bwd{i}", sv["o"], sv["z"], 0, d, dmix)
            res = _rwkv_scan_bwd(f"rw_scan_bwd{i}", *sv["seqs"], rw["hp"], sv["states"], do)
            dr, dk, dv, dlogw, da = res[:5]
            dpll, dw0, dwl, da0, dal = _lora_bwd(f"rw_lora_bwd{i}", sv["pll"], dlogw, da, rw["w0"], rw["w_lora"],
                                                  rw["a0"], rw["a_lora"])
            dpm, dmu_main = _lerp_bwd(f"rw_lerp_bwd{i}", [dr, dk, dv, dz], sv["pm"], rw["mu_main"])
            dpl, dmu_lora = _lerp_bwd(f"rw_lerpl_bwd{i}", [dpll], sv["plo"], rw["mu_lora"])
            grads.update(rw_w_out=d_w_out, rw_hp=res[5:], rw_w0=dw0, rw_w_lora=dwl, rw_a0=da0, rw_a_lora=dal,
                         rw_mu_main=dmu_main, rw_mu_lora=dmu_lora)
            grads["rw_w_main"] = _mm(f"rw_in_dw{i}", sv["h"], dpm, "tn", F32)
            grads["rw_w_lorain"] = _mm(f"rw_inl_dw{i}", sv["h"], dpl, "tn", F32)
            dh = _mm(f"rw_inl_dx{i}", dpl, rw["w_lorain"], "nt", F32)
            dh = _mm(f"rw_in_dx{i}", dpm, rw["w_main"], "nt", F32, add=dh)
        dx, dg, dshift, dscale = _norm_bwd(f"norm_bwd{i}", sv["x"], dh, dx, sv["g"], sv["shift"], sv["scale"])
        grads["norm_g"][i] = dg
        dmods[i] = jnp.concatenate([dshift, dscale, dgate], axis=1)
    return loss, dx, jnp.concatenate(dmods, axis=0), grads


def kernel(x, c, positions, norm_g, mod_w, mod_b, final_norm_g, sg_w_in, sg_w_out, sg_ln_g, sg_ln_b, sg_w_spatial, sg_b_spatial, swa_w_in, swa_w_out, swa_sinks, rwkv_w_in, rwkv_w_out, rwkv_mu, rwkv_w0, rwkv_w_lora, rwkv_a0, rwkv_a_lora, rwkv_k_k, rwkv_k_a, rwkv_r_k, rwkv_gn_g, rwkv_gn_b, loss_target, m_norm_g, m_mod_w, m_mod_b, m_final_norm_g, m_sg_w_in, m_sg_w_out, m_sg_ln_g, m_sg_ln_b, m_sg_w_spatial, m_sg_b_spatial, m_swa_w_in, m_swa_w_out, m_swa_sinks, m_rwkv_w_in, m_rwkv_w_out, m_rwkv_mu, m_rwkv_w0, m_rwkv_w_lora, m_rwkv_a0, m_rwkv_a_lora, m_rwkv_k_k, m_rwkv_k_a, m_rwkv_r_k, m_rwkv_gn_g, m_rwkv_gn_b, v_norm_g, v_mod_w, v_mod_b, v_final_norm_g, v_sg_w_in, v_sg_w_out, v_sg_ln_g, v_sg_ln_b, v_sg_w_spatial, v_sg_b_spatial, v_swa_w_in, v_swa_w_out, v_swa_sinks, v_rwkv_w_in, v_rwkv_w_out, v_rwkv_mu, v_rwkv_w0, v_rwkv_w_lora, v_rwkv_a0, v_rwkv_a_lora, v_rwkv_k_k, v_rwkv_k_a, v_rwkv_r_k, v_rwkv_gn_g, v_rwkv_gn_b):
    weights = dict(norm_g=norm_g, mod_w=mod_w, mod_b=mod_b, final_norm_g=final_norm_g, sg_w_in=sg_w_in, sg_w_out=sg_w_out,
                   sg_ln_g=sg_ln_g, sg_ln_b=sg_ln_b, sg_w_spatial=sg_w_spatial, sg_b_spatial=sg_b_spatial, swa_w_in=swa_w_in,
                   swa_w_out=swa_w_out, swa_sinks=swa_sinks, rwkv_w_in=rwkv_w_in, rwkv_w_out=rwkv_w_out, rwkv_mu=rwkv_mu,
                   rwkv_w0=rwkv_w0, rwkv_w_lora=rwkv_w_lora, rwkv_a0=rwkv_a0, rwkv_a_lora=rwkv_a_lora, rwkv_k_k=rwkv_k_k,
                   rwkv_k_a=rwkv_k_a, rwkv_r_k=rwkv_r_k, rwkv_gn_g=rwkv_gn_g, rwkv_gn_b=rwkv_gn_b)
    mom_m = dict(norm_g=m_norm_g, mod_w=m_mod_w, mod_b=m_mod_b, final_norm_g=m_final_norm_g, sg_w_in=m_sg_w_in,
                 sg_w_out=m_sg_w_out, sg_ln_g=m_sg_ln_g, sg_ln_b=m_sg_ln_b, sg_w_spatial=m_sg_w_spatial,
                 sg_b_spatial=m_sg_b_spatial, swa_w_in=m_swa_w_in, swa_w_out=m_swa_w_out, swa_sinks=m_swa_sinks,
                 rwkv_w_in=m_rwkv_w_in, rwkv_w_out=m_rwkv_w_out, rwkv_mu=m_rwkv_mu, rwkv_w0=m_rwkv_w0,
                 rwkv_w_lora=m_rwkv_w_lora, rwkv_a0=m_rwkv_a0, rwkv_a_lora=m_rwkv_a_lora, rwkv_k_k=m_rwkv_k_k,
                 rwkv_k_a=m_rwkv_k_a, rwkv_r_k=m_rwkv_r_k, rwkv_gn_g=m_rwkv_gn_g, rwkv_gn_b=m_rwkv_gn_b)
    mom_v = dict(norm_g=v_norm_g, mod_w=v_mod_w, mod_b=v_mod_b, final_norm_g=v_final_norm_g, sg_w_in=v_sg_w_in,
                 sg_w_out=v_sg_w_out, sg_ln_g=v_sg_ln_g, sg_ln_b=v_sg_ln_b, sg_w_spatial=v_sg_w_spatial,
                 sg_b_spatial=v_sg_b_spatial, swa_w_in=v_swa_w_in, swa_w_out=v_swa_w_out, swa_sinks=v_swa_sinks,
                 rwkv_w_in=v_rwkv_w_in, rwkv_w_out=v_rwkv_w_out, rwkv_mu=v_rwkv_mu, rwkv_w0=v_rwkv_w0,
                 rwkv_w_lora=v_rwkv_w_lora, rwkv_a0=v_rwkv_a0, rwkv_a_lora=v_rwkv_a_lora, rwkv_k_k=v_rwkv_k_k,
                 rwkv_k_a=v_rwkv_k_a, rwkv_r_k=v_rwkv_r_k, rwkv_gn_g=v_rwkv_gn_g, rwkv_gn_b=v_rwkv_gn_b)
    names = list(weights)
    t, d = x.shape[1], x.shape[2]
    me = 4 * lax.axis_index("x") + 2 * lax.axis_index("y") + lax.axis_index("c")
    n_mod = mod_w.shape[2]
    n_rw = rwkv_w_in.shape[2]

    small_names = ["sg_ln_g", "sg_ln_b", "rwkv_mu", "rwkv_w0", "rwkv_a0", "rwkv_k_k", "rwkv_k_a", "rwkv_gn_g", "rwkv_gn_b",
                   "rwkv_w_lora", "rwkv_a_lora"]
    small_shapes = [weights[n].shape for n in small_names]
    pk = _pack([c] + [weights[n] for n in small_names])
    gathered = _gather("gather_small", pk).reshape(N_DEV, -1)
    c_all = gathered[:, :d]
    per_dev = [_unpack(gathered[dv, d:], small_shapes) for dv in range(N_DEV)]
    full_small = {}
    for q, n in enumerate(small_names):
        full_small[n] = jnp.concatenate([per_dev[dv][q] for dv in range(N_DEV)], axis=-1)

    mod_b_cols = lax.dynamic_slice_in_dim(mod_b, me * n_mod, n_mod, axis=1).reshape(DEPTH, 1, n_mod)
    mod_part = _mod_fwd("mod_fwd", c_all, mod_w, mod_b_cols)
    mod_g = _gather("gather_mod", mod_part.reshape(DEPTH * N_DEV, n_mod))
    mod_g = mod_g.reshape(N_DEV, DEPTH, N_DEV, n_mod)
    mods = lax.dynamic_index_in_dim(mod_g, me, axis=2, keepdims=False)
    mods = mods.transpose(1, 0, 2).reshape(DEPTH, N_DEV * n_mod)

    def gather_big(name, w2d):
        return _gather(name, w2d.astype(BF16))

    g_sg_in = gather_big("gather_sg_in", sg_w_in.reshape(-1, sg_w_in.shape[2])).reshape(N_DEV, 2, d, -1)
    g_sg_out = gather_big("gather_sg_out", sg_w_out.reshape(-1, d)).reshape(N_DEV, 2, -1, d)
    g_swa_in = gather_big("gather_swa_in", swa_w_in[0])
    g_swa_out = gather_big("gather_swa_out", swa_w_out[0])
    g_rw_in = gather_big("gather_rw_in", rwkv_w_in[0])
    g_rw_out = gather_big("gather_rw_out", rwkv_w_out[0])
    rw_in_full = _cols_full(g_rw_in)
    lora_rows = lambda w, off: jnp.zeros((LORA_PAD, d), F32).at[off:off + LORA].set(w)
    mu = full_small["rwkv_mu"].reshape(1, -1)
    heads = lambda a: a.reshape(1, -1)
    sg = dict(w_in=[_cols_full(g_sg_in[:, j]) for j in range(2)], w_out=[g_sg_out[:, j].reshape(d, d) for j in range(2)],
              ln_g=[full_small["sg_ln_g"][j].reshape(1, d) for j in range(2)],
              ln_b=[full_small["sg_ln_b"][j].reshape(1, d) for j in range(2)],
              w_s=[sg_w_spatial[j] for j in range(2)], b_st=[sg_b_spatial[j].T for j in range(2)])
    swa = dict(w_in=_cols_full(g_swa_in), w_out=g_swa_out.reshape(d, d), sinks=swa_sinks.reshape(SWA_KV, SWA_REP, 1, 1))
    rw = dict(w_main=rw_in_full[:, :4 * d], w_lorain=jnp.pad(rw_in_full[:, 4 * d:], ((0, 0), (0, LORA_PAD - 2 * LORA))),
              w_out=g_rw_out.reshape(d, d), mu_main=mu[:, :4 * d], mu_lora=jnp.pad(mu[:, 4 * d:], ((0, 0), (0, LORA_PAD - 2 * LORA))),
              w0=full_small["rwkv_w0"], a0=full_small["rwkv_a0"],
              w_lora=lora_rows(full_small["rwkv_w_lora"][0], 0), a_lora=lora_rows(full_small["rwkv_a_lora"][0], LORA),
              hp=[heads(full_small["rwkv_k_k"]), heads(full_small["rwkv_k_a"]), heads(rwkv_r_k), heads(full_small["rwkv_gn_g"]),
                  heads(full_small["rwkv_gn_b"])])

    loss, dx, dmods, g = _local_step(x[0], positions.reshape(t, 1).astype(F32), loss_target[0], mods, norm_g, final_norm_g,
                                     sg, swa, rw)

    dmod_g = _gather("gather_dmod", dmods)
    dmod_all = dmod_g.transpose(1, 0, 2)
    dmod_cols = lax.dynamic_slice_in_dim(dmod_all, me * n_mod, n_mod, axis=2)
    g_mod_w, g_mod_b = _mod_bwd("mod_bwd", c_all, dmod_cols, dmod_all)

    d_b_sp = [g["sg_b_st"][j].T for j in range(2)]
    rep = [loss[0, :1], jnp.concatenate(g["norm_g"], axis=0), g["final_norm_g"], jnp.stack(g["sg_w_s"]), jnp.stack(d_b_sp),
           g["swa_sinks"], g["rw_hp"][2]]
    rep_shapes = [(1,), norm_g.shape, final_norm_g.shape, sg_w_spatial.shape, sg_b_spatial.shape, swa_sinks.shape, rwkv_r_k.shape]
    rep_sum = _sum_parts("sum_rep", _gather("gather_rep", _pack(rep, 128 * 256))).reshape(-1)
    loss_tot, g_norm_g, g_final, g_w_sp, g_b_sp, g_sinks, g_r_k = _unpack(rep_sum, rep_shapes)

    def scatter_big(name, parts):
        return _scatter(name, parts.astype(BF16))

    p_sg_in = scatter_big("scatter_sg_in", jnp.concatenate([_cols_parts(g["sg_w_in"][j]) for j in range(2)], axis=1))
    p_sg_out = scatter_big("scatter_sg_out", jnp.concatenate([g["sg_w_out"][j].reshape(N_DEV, -1, d) for j in range(2)], axis=1))
    p_swa_in = scatter_big("scatter_swa_in", _cols_parts(g["swa_w_in"]))
    p_swa_out = scatter_big("scatter_swa_out", g["swa_w_out"].reshape(N_DEV, -1, d))
    d_rw_in = jnp.concatenate([g["rw_w_main"], g["rw_w_lorain"][:, :2 * LORA]], axis=1)
    p_rw_in = scatter_big("scatter_rw_in", _cols_parts(d_rw_in))
    p_rw_out = scatter_big("scatter_rw_out", g["rw_w_out"].reshape(N_DEV, -1, d))
    d_mu = jnp.concatenate([g["rw_mu_main"], g["rw_mu_lora"][:, :2 * LORA]], axis=1)
    hp_flat = lambda a: a.reshape(1, -1)
    small_grads = dict(sg_ln_g=jnp.concatenate(g["sg_ln_g"], axis=0), sg_ln_b=jnp.concatenate(g["sg_ln_b"], axis=0), rwkv_mu=d_mu,
                       rwkv_w0=g["rw_w0"], rwkv_a0=g["rw_a0"], rwkv_k_k=hp_flat(g["rw_hp"][0]), rwkv_k_a=hp_flat(g["rw_hp"][1]),
                       rwkv_gn_g=hp_flat(g["rw_hp"][3]), rwkv_gn_b=hp_flat(g["rw_hp"][4]),
                       rwkv_w_lora=g["rw_w_lora"][None, :LORA], rwkv_a_lora=g["rw_a_lora"][None, LORA:2 * LORA])
    per_dest = []
    for dv in range(N_DEV):
        shards = []
        for n in small_names:
            full, w = small_grads[n], weights[n].shape[-1]
            shards.append(full[..., dv * w:(dv + 1) * w])
        per_dest.append(_pack(shards))
    small_parts = _exchange("scatter_small", jnp.stack(per_dest), True)

    out_g, out_d, out_m, out_v = {}, {}, {}, {}

    def update(name, grad, shape2d):
        w2, m2, v2 = (a[name].reshape(shape2d) for a in (weights, mom_m, mom_v))
        gg, dd, mm, vv = _adamw("adamw_" + name, w2, grad, m2, v2)
        shp = weights[name].shape
        out_g[name], out_d[name], out_m[name], out_v[name] = gg.reshape(shp), dd.reshape(shp), mm.reshape(shp), vv.reshape(shp)

    update("mod_w", g_mod_w.reshape(-1, n_mod), (-1, n_mod))
    update("sg_w_in", p_sg_in, (-1, sg_w_in.shape[2]))
    update("sg_w_out", p_sg_out, (-1, d))
    update("swa_w_in", p_swa_in, (-1, swa_w_in.shape[2]))
    update("swa_w_out", p_swa_out, (-1, d))
    update("rwkv_w_in", p_rw_in, (-1, n_rw))
    update("rwkv_w_out", p_rw_out, (-1, d))
    update("sg_w_spatial", g_w_sp.reshape(-1, 128), (-1, 128))
    w_pk, m_pk, v_pk = (_pack([a[n] for n in small_names]) for a in (weights, mom_m, mom_v))
    res = _adamw("adamw_small", w_pk, small_parts, m_pk, v_pk)
    for q, arrs in enumerate(zip(*[_unpack(r_.reshape(-1), small_shapes) for r_ in res])):
        out_g[small_names[q]], out_d[small_names[q]], out_m[small_names[q]], out_v[small_names[q]] = arrs
    rep_names = ["norm_g", "mod_b", "final_norm_g", "sg_b_spatial", "swa_sinks", "rwkv_r_k"]
    rep_grads = [g_norm_g, g_mod_b.reshape(mod_b.shape), g_final, g_b_sp, g_sinks, g_r_k]
    rep_shapes2 = [weights[n].shape for n in rep_names]
    w_pk, m_pk, v_pk = (_pack([a[n] for n in rep_names]) for a in (weights, mom_m, mom_v))
    res = _adamw("adamw_rep", w_pk, _pack(rep_grads), m_pk, v_pk)
    for q, arrs in enumerate(zip(*[_unpack(r_.reshape(-1), rep_shapes2) for r_ in res])):
        out_g[rep_names[q]], out_d[rep_names[q]], out_m[rep_names[q]], out_v[rep_names[q]] = arrs

    return (loss_tot.reshape(()), dx[None], *[out_g[n] for n in names], *[out_d[n] for n in names],
            *[out_m[n] for n in names], *[out_v[n] for n in names])
```

```python
import functools
import math

import jax
import jax.numpy as jnp
from jax import lax
from jax.experimental import pallas as pl
from jax.experimental.pallas import tpu as pltpu

F32, BF16 = jnp.float32, jnp.bfloat16
HI = lax.Precision.HIGHEST
S = jax.ShapeDtypeStruct
MESH = pl.DeviceIdType.MESH

N_DEV = 8
DEPTH = 4
HEAD = 64
SG_GROUPS = 16
SG_CHUNK = 128
SWA_BLOCK = 128
SWA_KV = 4
SWA_REP = 8
ROPE_THETA = 10000.0
LORA = 96
LORA_PAD = 256
RW_CHUNK = 64
RW_HEADS = 16
RW_PREC = lax.Precision.HIGH
DECAY_SCALE = math.exp(-0.5)
GN_EPS = 64e-5
RMS_EPS = 1e-6
LN_EPS = 1e-5
NEG = -1e30
ADAM_LR, ADAM_B1, ADAM_B2, ADAM_EPS, ADAM_WD, ADAM_STEP = 0.001, 0.9, 0.999, 1e-08, 0.01, 10
VMEM_MB = 56


def _params(sem=None):
    kw = dict(vmem_limit_bytes=VMEM_MB << 20)
    if sem is not None:
        kw["dimension_semantics"] = sem
    return pltpu.CompilerParams(**kw)


def _pick(n, opts):
    for o in opts:
        if n % o == 0:
            return o
    raise ValueError(f"no tile for {n}")


def _rows(name, fn, rows, consts, out_rows, out_accs, tm):
    t = rows[0].shape[0]
    nr, nc, no = len(rows), len(consts), len(out_rows)

    def body(*refs):
        outs = fn(*[r[...] for r in refs[:nr + nc]])
        if not isinstance(outs, (tuple, list)):
            outs = (outs,)
        for r, o in zip(refs[nr + nc:nr + nc + no], outs[:no]):
            r[...] = o.astype(r.dtype)
        i = pl.program_id(0)
        for r, o in zip(refs[nr + nc + no:], outs[no:]):
            @pl.when(i == 0)
            def _(r=r, o=o):
                r[...] = o.astype(r.dtype)

            @pl.when(i > 0)
            def _(r=r, o=o):
                r[...] += o.astype(r.dtype)

    in_specs = [pl.BlockSpec((tm, a.shape[1]), lambda i: (i, 0)) for a in rows]
    in_specs += [pl.BlockSpec(c.shape, lambda i, nd=c.ndim: (0,) * nd) for c in consts]
    out_specs = [pl.BlockSpec((tm, n), lambda i: (i, 0)) for n, _ in out_rows]
    out_specs += [pl.BlockSpec(s, lambda i, nd=len(s): (0,) * nd) for s in out_accs]
    out_shape = [S((t, n), dt) for n, dt in out_rows] + [S(s, F32) for s in out_accs]
    res = pl.pallas_call(body, grid=(t // tm,), in_specs=in_specs, out_specs=out_specs, out_shape=out_shape,
                         name=name, compiler_params=_params(("arbitrary",)))(*rows, *consts)
    return res


_DN = {"nn": (((1,), (0,)), ((), ())), "nt": (((1,), (1,)), ((), ())), "tn": (((0,), (0,)), ((), ()))}


def _mm(name, a, b, mode, out_dtype, add=None, jobs=()):
    if mode == "nn":
        (m, k), (_, n) = a.shape, b.shape
    elif mode == "nt":
        (m, k), (n, _) = a.shape, b.shape
    else:
        (k, m), (_, n) = a.shape, b.shape
    tm, tn, tk = _pick(m, (512, 256, 128)), _pick(n, (1024, 512, 384, 256, 128)), _pick(k, (2048, 1536, 1024, 512, 384, 256, 128))
    nk = k // tk
    has_add = add is not None

    def body(*refs):
        a_ref, b_ref = refs[0], refs[1]
        o_ref, acc = refs[-2], refs[-1]
        kk = pl.program_id(2)
        prod = lax.dot_general(a_ref[...].astype(BF16), b_ref[...].astype(BF16), _DN[mode], preferred_element_type=F32)
        if nk == 1:
            o_ref[...] = (prod + refs[2][...].astype(F32) if has_add else prod).astype(o_ref.dtype)
            return

        @pl.when(kk == 0)
        def _():
            acc[...] = prod + refs[2][...].astype(F32) if has_add else prod

        @pl.when(kk > 0)
        def _():
            acc[...] += prod

        @pl.when(kk == nk - 1)
        def _():
            o_ref[...] = acc[...].astype(o_ref.dtype)

    a_spec = pl.BlockSpec((tk, tm), lambda i, j, q: (q, i)) if mode == "tn" else pl.BlockSpec((tm, tk), lambda i, j, q: (i, q))
    b_spec = pl.BlockSpec((tn, tk), lambda i, j, q: (j, q)) if mode == "nt" else pl.BlockSpec((tk, tn), lambda i, j, q: (q, j))
    o_spec = pl.BlockSpec((tm, tn), lambda i, j, q: (i, j))
    ins, specs = [a, b], [a_spec, b_spec]
    if has_add:
        ins.append(add)
        specs.append(o_spec)
    return _pcall(body, grid=(m // tm, n // tn, nk), in_specs=specs, out_specs=[o_spec], out_shape=[S((m, n), out_dtype)],
                  scratch_shapes=[pltpu.VMEM((tm, tn), F32)], name=name, semantics=("parallel", "parallel", "arbitrary"),
                  inputs=ins, jobs=jobs)[0]


def _exchange(name, src, scatter):
    blk = src.shape[1:] if scatter else src.shape

    def body(src_ref, dst_ref, send_sems, recv_sems, loc_sem):
        x, y, c = lax.axis_index("x"), lax.axis_index("y"), lax.axis_index("c")
        me = 4 * x + 2 * y + c

        def mine(d):
            return src_ref.at[d] if scatter else src_ref

        local = pltpu.make_async_copy(mine(me), dst_ref.at[me], loc_sem)
        local.start()
        sends, peers = [], []
        for k in range(1, N_DEV):
            px = 1 - x if k & 4 else x
            py = 1 - y if k & 2 else y
            pc = 1 - c if k & 1 else c
            pid = 4 * px + 2 * py + pc
            cp = pltpu.make_async_remote_copy(src_ref=mine(pid), dst_ref=dst_ref.at[me], send_sem=send_sems.at[k - 1],
                                              recv_sem=recv_sems.at[k - 1], device_id=(px, py, pc), device_id_type=MESH)
            cp.start()
            sends.append(cp)
            peers.append((pid, (px, py, pc)))
        for k in range(1, N_DEV):
            pid, dev = peers[k - 1]
            pltpu.make_async_remote_copy(src_ref=mine(pid), dst_ref=dst_ref.at[pid], send_sem=send_sems.at[k - 1],
                                         recv_sem=recv_sems.at[k - 1], device_id=dev, device_id_type=MESH).wait_recv()
        for cp in sends:
            cp.wait_send()
        local.wait()

    return pl.pallas_call(
        body, out_shape=S((N_DEV,) + tuple(blk), src.dtype),
        in_specs=[pl.BlockSpec(memory_space=pl.ANY)], out_specs=pl.BlockSpec(memory_space=pl.ANY),
        scratch_shapes=[pltpu.SemaphoreType.DMA((N_DEV - 1,)), pltpu.SemaphoreType.DMA((N_DEV - 1,)),
                        pltpu.SemaphoreType.DMA],
        name=name)(src)


class _Gather:
    @staticmethod
    def out_shape(src):
        return S((N_DEV,) + tuple(src.shape), src.dtype)

    scratch = (pltpu.SemaphoreType.DMA((N_DEV - 1,)), pltpu.SemaphoreType.DMA((N_DEV - 1,)), pltpu.SemaphoreType.DMA)

    def __init__(self, src_ref, dst_ref, send_sems, recv_sems, loc_sem):
        self.refs = (src_ref, dst_ref, send_sems, recv_sems, loc_sem)
        x, y, c = lax.axis_index("x"), lax.axis_index("y"), lax.axis_index("c")
        self.c, self.me, self.sibling = c, (x, y, c), (x, y, 1 - c)
        self.chips = [(1 - x, y), (x, 1 - y), (1 - x, 1 - y)]

    def rows(self, px, py, pc):
        return self.refs[1].at[4 * px + 2 * py + pc]

    def copy(self, k, block, to, own=False):
        src_ref, _, send_sems, recv_sems, _ = self.refs
        return pltpu.make_async_remote_copy(src_ref=src_ref if own else self.rows(*block), dst_ref=self.rows(*block),
                                            send_sem=send_sems.at[k], recv_sem=recv_sems.at[k], device_id=to,
                                            device_id_type=MESH)

    def local(self):
        return pltpu.make_async_copy(self.refs[0], self.rows(*self.me), self.refs[4])

    def first(self):
        return [self.copy(0, self.me, self.sibling, own=True)] + [self.copy(1 + j, self.me, (*chip, self.c), own=True)
                                                                  for j, chip in enumerate(self.chips)]

    def start(self):
        self.local().start()
        for cp in self.first():
            cp.start()

    def finish(self):
        c = self.c
        passed = [self.copy(4 + j, (*chip, c), self.sibling) for j, chip in enumerate(self.chips)]
        for j, chip in enumerate(self.chips):
            self.copy(1 + j, (*chip, c), self.me).wait_recv()
            passed[j].start()
        self.copy(0, self.sibling, self.me).wait_recv()
        for j, chip in enumerate(self.chips):
            self.copy(4 + j, (*chip, 1 - c), self.me).wait_recv()
        for cp in self.first() + passed:
            cp.wait_send()
        self.local().wait()


class _Chips:
    @staticmethod
    def out_shape(src):
        return S(src.shape, src.dtype)

    scratch = (pltpu.SemaphoreType.DMA((N_DEV // 2 - 1,)), pltpu.SemaphoreType.DMA((N_DEV // 2 - 1,)), pltpu.SemaphoreType.DMA)

    def __init__(self, src_ref, dst_ref, send_sems, recv_sems, loc_sem):
        self.refs = (src_ref, dst_ref, send_sems, recv_sems, loc_sem)
        x, y, c = lax.axis_index("x"), lax.axis_index("y"), lax.axis_index("c")
        self.c, self.mine = c, 2 * x + y
        self.chips = [(1 - x, y), (x, 1 - y), (1 - x, 1 - y)]

    def local(self):
        src_ref, dst_ref, _, _, loc_sem = self.refs
        return pltpu.make_async_copy(src_ref.at[self.mine], dst_ref.at[self.mine], loc_sem)

    def send(self, j):
        src_ref, dst_ref, send_sems, recv_sems, _ = self.refs
        px, py = self.chips[j]
        return pltpu.make_async_remote_copy(src_ref=src_ref.at[2 * px + py], dst_ref=dst_ref.at[self.mine],
                                            send_sem=send_sems.at[j], recv_sem=recv_sems.at[j],
                                            device_id=(px, py, self.c), device_id_type=MESH)

    def arrival(self, j):
        src_ref, dst_ref, send_sems, recv_sems, _ = self.refs
        px, py = self.chips[j]
        return pltpu.make_async_remote_copy(src_ref=src_ref.at[self.mine], dst_ref=dst_ref.at[2 * px + py],
                                            send_sem=send_sems.at[j], recv_sem=recv_sems.at[j],
                                            device_id=(px, py, self.c), device_id_type=MESH)

    def start(self):
        self.local().start()
        for j in range(len(self.chips)):
            self.send(j).start()

    def finish(self):
        for j in range(len(self.chips)):
            self.arrival(j).wait_recv()
        for j in range(len(self.chips)):
            self.send(j).wait_send()
        self.local().wait()


def _exchange_call(name, cls, src):
    def body(*refs):
        ex = cls(*refs)
        ex.start()
        ex.finish()

    return pl.pallas_call(body, out_shape=cls.out_shape(src), in_specs=[pl.BlockSpec(memory_space=pl.ANY)],
                          out_specs=pl.BlockSpec(memory_space=pl.ANY), scratch_shapes=list(cls.scratch), name=name)(src)


def _gather(name, src):
    return _exchange_call(name, _Gather, src)


def _pcall(body, *, grid, in_specs, out_specs, out_shape, scratch_shapes=(), name, semantics, inputs, jobs=()):
    if not jobs:
        return pl.pallas_call(body, grid=grid, in_specs=in_specs, out_specs=out_specs, out_shape=out_shape,
                              scratch_shapes=list(scratch_shapes), name=name, compiler_params=_params(semantics))(*inputs)
    n_in, n_out, n_scr, nj = len(in_specs), len(out_specs), len(scratch_shapes), len(jobs)

    def hosted(*refs):
        ins, srcs = refs[:n_in], refs[n_in:n_in + nj]
        outs, dsts = refs[n_in + nj:n_in + nj + n_out], refs[n_in + nj + n_out:n_in + 2 * nj + n_out]
        scr, sems = refs[n_in + 2 * nj + n_out:n_in + 2 * nj + n_out + n_scr], refs[n_in + 2 * nj + n_out + n_scr:]
        ids = [pl.program_id(q) for q in range(len(grid))]
        first = functools.reduce(jnp.logical_and, [i == 0 for i in ids])
        last = functools.reduce(jnp.logical_and, [i == g - 1 for i, g in zip(ids, grid)])
        make = lambda q: jobs[q]["cls"](srcs[q], dsts[q], *sems[3 * q:3 * q + 3])

        @pl.when(first)
        def _():
            for q in range(nj):
                make(q).start()

        body(*ins, *outs, *scr)

        @pl.when(last)
        def _():
            for q in range(nj):
                make(q).finish()

    anyspec = pl.BlockSpec(memory_space=pl.ANY)
    res = pl.pallas_call(
        hosted, grid=grid, in_specs=list(in_specs) + [anyspec] * nj, out_specs=list(out_specs) + [anyspec] * nj,
        out_shape=list(out_shape) + [j["cls"].out_shape(j["src"]) for j in jobs],
        scratch_shapes=list(scratch_shapes) + [s for j in jobs for s in j["cls"].scratch], name=name,
        compiler_params=_params(("arbitrary",) * len(grid)))(*inputs, *[j["src"] for j in jobs])
    for j, out in zip(jobs, res[n_out:]):
        j["out"] = out
    return res[:n_out]


def _scatter_pairs(name, parts):
    _, r, c_ = parts.shape
    n_chip = N_DEV // 2

    def stage1(src_ref, dst_ref, send_sems, recv_sems):
        x, y, c = lax.axis_index("x"), lax.axis_index("y"), lax.axis_index("c")
        sends = []
        for q in range(n_chip):
            cp = pltpu.make_async_remote_copy(src_ref=src_ref.at[2 * q + 1 - c], dst_ref=dst_ref.at[q],
                                              send_sem=send_sems.at[q], recv_sem=recv_sems.at[q],
                                              device_id=(x, y, 1 - c), device_id_type=MESH)
            cp.start()
            sends.append(cp)
        for q in range(n_chip):
            pltpu.make_async_remote_copy(src_ref=src_ref.at[2 * q + c], dst_ref=dst_ref.at[q], send_sem=send_sems.at[q],
                                         recv_sem=recv_sems.at[q], device_id=(x, y, 1 - c), device_id_type=MESH).wait_recv()
        for cp in sends:
            cp.wait_send()

    from_sibling = pl.pallas_call(
        stage1, out_shape=S((n_chip, r, c_), parts.dtype),
        in_specs=[pl.BlockSpec(memory_space=pl.ANY)], out_specs=pl.BlockSpec(memory_space=pl.ANY),
        scratch_shapes=[pltpu.SemaphoreType.DMA((n_chip,)), pltpu.SemaphoreType.DMA((n_chip,))], name=name + "_pair")(parts)

    tm = _pick(r, (512, 256, 128, 64, 32, 16, 8)) if r % 8 == 0 else r
    core = lax.axis_index("c").astype(jnp.int32).reshape(1)

    def pair_sum(core_ref, mine_ref, sib_ref, o_ref):
        o_ref[...] = (mine_ref[0].astype(F32) + sib_ref[...].astype(F32)).astype(o_ref.dtype)

    pair = pl.pallas_call(
        pair_sum, out_shape=S((n_chip, r, c_), parts.dtype),
        grid_spec=pltpu.PrefetchScalarGridSpec(
            num_scalar_prefetch=1, grid=(n_chip, r // tm),
            in_specs=[pl.BlockSpec((1, 1, tm, c_), lambda q, i, core_ref: (q, core_ref[0], i, 0)),
                      pl.BlockSpec((1, tm, c_), lambda q, i, core_ref: (q, i, 0))],
            out_specs=pl.BlockSpec((1, tm, c_), lambda q, i, core_ref: (q, i, 0))),
        name=name + "_sum", compiler_params=_params(("parallel", "parallel")))(
            core, parts.reshape(n_chip, 2, r, c_), from_sibling)

    return pair


def _sum_parts(name, parts):
    n_parts, r, c = parts.shape
    tm = _pick(r, (512, 256, 128, 64, 32, 16, 8)) if r % 8 == 0 else r

    def body(p_ref, o_ref):
        acc = p_ref[0].astype(F32)
        for d in range(1, n_parts):
            acc = acc + p_ref[d].astype(F32)
        o_ref[...] = acc

    return pl.pallas_call(body, grid=(r // tm,), in_specs=[pl.BlockSpec((n_parts, tm, c), lambda i: (0, i, 0))],
                          out_specs=pl.BlockSpec((tm, c), lambda i: (i, 0)), out_shape=S((r, c), F32), name=name,
                          compiler_params=_params(("parallel",)))(parts)


def _adamw(name, w, g, m, v):
    r, c = w.shape
    parts = g.ndim == 3
    n_parts = g.shape[0] if parts else 1
    tm = _pick(r, (256, 128, 64, 32, 16, 8)) if r % 8 == 0 else r

    def body(w_ref, g_ref, m_ref, v_ref, go_ref, d_ref, mo_ref, vo_ref):
        if parts:
            gg = g_ref[0].astype(F32)
            for d in range(1, n_parts):
                gg = gg + g_ref[d].astype(F32)
        else:
            gg = g_ref[...]
        mm = ADAM_B1 * m_ref[...] + (1.0 - ADAM_B1) * gg
        vv = ADAM_B2 * v_ref[...] + (1.0 - ADAM_B2) * jnp.square(gg)
        m_hat = mm / (1.0 - ADAM_B1 ** ADAM_STEP)
        v_hat = vv / (1.0 - ADAM_B2 ** ADAM_STEP)
        go_ref[...] = gg
        d_ref[...] = -ADAM_LR * (m_hat / (jnp.sqrt(v_hat) + ADAM_EPS) + ADAM_WD * w_ref[...])
        mo_ref[...] = mm
        vo_ref[...] = vv

    spec = pl.BlockSpec((tm, c), lambda i: (i, 0))
    g_spec = pl.BlockSpec((n_parts, tm, c), lambda i: (0, i, 0)) if parts else spec
    return pl.pallas_call(body, grid=(r // tm,), in_specs=[spec, g_spec, spec, spec], out_specs=[spec] * 4,
                          out_shape=[S((r, c), F32)] * 4, name=name, compiler_params=_params(("parallel",)))(w, g, m, v)


def _rms(x, g):
    return x * lax.rsqrt(jnp.mean(x * x, axis=-1, keepdims=True) + RMS_EPS) * g


def _adaln(x, g, shift, scale):
    return _rms(x, g) * (1.0 + scale) + shift


def _dot(a, b, dn="nn", hi=False, prec=None):
    if hi or prec is not None:
        return lax.dot_general(a, b, _DN[dn], precision=HI if hi else prec, preferred_element_type=F32)
    return lax.dot_general(a.astype(BF16), b.astype(BF16), _DN[dn], preferred_element_type=F32)


def _sg_mix(p, ln_g, ln_b, w_s, b_st):
    d = p.shape[1] // 3
    gd = d // SG_GROUPS
    u = jax.nn.gelu(p[:, :d])
    vf = jax.nn.gelu(p[:, d:2 * d])
    z = p[:, 2 * d:]
    mean = jnp.mean(vf, axis=-1, keepdims=True)
    var = jnp.mean(jnp.square(vf - mean), axis=-1, keepdims=True)
    vn = (vf - mean) * lax.rsqrt(var + LN_EPS) * ln_g + ln_b
    row = lax.broadcasted_iota(jnp.int32, (SG_CHUNK, SG_CHUNK), 0)
    col = lax.broadcasted_iota(jnp.int32, (SG_CHUNK, SG_CHUNK), 1)
    fs = []
    for g in range(SG_GROUPS):
        w = jnp.where(row >= col, w_s[g], 0.0)
        fs.append(_dot(w, vn[:, g * gd:(g + 1) * gd]))
    sel = (lax.broadcasted_iota(jnp.int32, (SG_GROUPS, d), 1) // gd
           == lax.broadcasted_iota(jnp.int32, (SG_GROUPS, d), 0)).astype(F32)
    f = jnp.concatenate(fs, axis=1) + _dot(b_st, sel, hi=True)
    return u * f * jax.nn.silu(z)


def _rot_half(x):
    n = x.shape[1]
    lane = lax.broadcasted_iota(jnp.int32, x.shape, 1)
    return jnp.where(lane % HEAD < HEAD // 2, -pltpu.roll(x, n - HEAD // 2, 1), pltpu.roll(x, HEAD // 2, 1))


def _rope(x, cos, sin, sign):
    reps = x.shape[1] // cos.shape[1]
    return x * jnp.tile(cos, (1, reps)) + sign * _rot_half(x) * jnp.tile(sin, (1, reps))


def _attn_block(q, kp, kc, vp, vc, sink, prev_bias):
    r = q.shape[0]
    q2 = q.reshape(r * SWA_BLOCK, HEAD)
    sp = (_dot(q2, kp, "nt") * (HEAD ** -0.5)).reshape(r, SWA_BLOCK, SWA_BLOCK)
    sc = (_dot(q2, kc, "nt") * (HEAD ** -0.5)).reshape(r, SWA_BLOCK, SWA_BLOCK)
    qi = lax.broadcasted_iota(jnp.int32, (r, SWA_BLOCK, SWA_BLOCK), 1)
    kj = lax.broadcasted_iota(jnp.int32, (r, SWA_BLOCK, SWA_BLOCK), 2)
    sp = jnp.where(kj > qi, sp, NEG) + prev_bias
    sc = jnp.where(kj <= qi, sc, NEG)
    m = jnp.maximum(jnp.maximum(jnp.max(sp, axis=-1, keepdims=True), jnp.max(sc, axis=-1, keepdims=True)), sink)
    ep, ec = jnp.exp(sp - m), jnp.exp(sc - m)
    denom = jnp.sum(ep, axis=-1, keepdims=True) + jnp.sum(ec, axis=-1, keepdims=True) + jnp.exp(sink - m)
    pp = (ep / denom).reshape(r * SWA_BLOCK, SWA_BLOCK)
    pc = (ec / denom).reshape(r * SWA_BLOCK, SWA_BLOCK)
    return (_dot(pp, vp) + _dot(pc, vc)).reshape(r, SWA_BLOCK, HEAD)


def _rwkv_chunk(s0, r, k, v, logw, a, k_k, k_a, r_k, gn_g, gn_b):
    c = r[0].shape[0]
    each = lambda f, *ls: [f(*xs) for xs in zip(*ls)]
    gram = functools.partial(_dot, prec=RW_PREC)
    row = lax.broadcasted_iota(jnp.int32, (c, c), 0)
    col = lax.broadcasted_iota(jnp.int32, (c, c), 1)
    incl, strict = row >= col, row > col
    ones_l = incl.astype(F32)

    def unit(x):
        return x / jnp.maximum(jnp.sqrt(jnp.sum(x * x, axis=-1, keepdims=True)), 1e-12)

    kk = each(lambda k_, p: unit(k_ * p), k, k_k)
    km = each(lambda k_, a_, p: k_ * (1.0 + (a_ - 1.0) * p), k, a, k_a)
    b = each(lambda x, a_: x * a_, kk, a)
    cum = each(lambda w: _dot(ones_l, w, hi=True), logw)
    alpha = each(lambda x, cu, w: x * jnp.exp(cu - w), kk, cum, logw)
    beta = each(lambda x, cu: x * jnp.exp(-cu), b, cum)
    kap = each(lambda x, cu: x * jnp.exp(-cu), km, cum)
    rho = each(lambda x, cu: x * jnp.exp(cu), r, cum)
    lab = each(lambda x, y_: jnp.where(strict, gram(x, y_, "nt"), 0.0), alpha, beta)
    lak = each(lambda x, y_: jnp.where(strict, gram(x, y_, "nt"), 0.0), alpha, kap)
    xs = each(lambda al, s, l, v_: _dot(al, s, "nt") + _dot(l, v_), alpha, s0, lak, v)
    xs = each(lambda x, l: x - _dot(l, x), xs, lab)
    lp, power = lab, 2
    while power < c:
        lp = each(lambda l: _dot(l, l), lp)
        xs = each(lambda x, l: x + _dot(l, x), xs, lp)
        power *= 2
    u = each(lambda x: -x, xs)
    mrb = each(lambda x, y_: jnp.where(incl, gram(x, y_, "nt"), 0.0), rho, beta)
    mrk = each(lambda x, y_: jnp.where(incl, gram(x, y_, "nt"), 0.0), rho, kap)
    y = each(lambda rh, s, mb, u_, mk, v_: _dot(rh, s, "nt") + _dot(mb, u_) + _dot(mk, v_), rho, s0, mrb, u, mrk, v)
    s1 = each(lambda s, u_, be, v_, ka, w: (s + _dot(u_, be, "tn") + _dot(v_, ka, "tn")) * jnp.exp(jnp.sum(w, axis=0, keepdims=True)),
              s0, u, beta, v, kap, logw)

    def finish(y_, g, bias, r_, km_, rk, v_):
        mean = jnp.mean(y_, axis=-1, keepdims=True)
        var = jnp.mean(jnp.square(y_ - mean), axis=-1, keepdims=True)
        y_ = (y_ - mean) * lax.rsqrt(var + GN_EPS) * g + bias
        return y_ + jnp.sum(r_ * km_ * rk, axis=-1, keepdims=True) * v_

    return each(finish, y, gn_g, gn_b, r, km, r_k, v), s1


def _norm_fwd(name, x, g, shift, scale):
    return _rows(name, lambda x_, g_, sh, sc: _adaln(x_, g_, sh, sc), [x], [g, shift, scale], [(x.shape[1], BF16)], [], 256)[0]


def _norm_bwd(name, x, dh, dx_res, g, shift, scale):
    d = x.shape[1]

    def fn(x_, dh_, dr_, g_, sh, sc):
        _, vjp = jax.vjp(_adaln, x_, g_, sh, sc)
        dx, dg, dsh, dsc = vjp(dh_)
        return dx + dr_, dg, dsh, dsc

    return _rows(name, fn, [x, dh, dx_res], [g, shift, scale], [(d, F32)], [(1, d)] * 3, 256)


def _resid_fwd(name, x, y, gate):
    return _rows(name, lambda x_, y_, g_: x_ + g_ * y_, [x, y], [gate], [(x.shape[1], F32)], [], 256)[0]


def _resid_bwd(name, dx, y, gate):
    d = dx.shape[1]
    return _rows(name, lambda dx_, y_, g_: (g_ * dx_, jnp.sum(dx_ * y_, axis=0, keepdims=True)), [dx, y], [gate],
                 [(d, BF16)], [(1, d)], 256)


def _sg_fwd(name, p, ln_g, ln_b, w_s, b_st):
    d = p.shape[1] // 3
    return _rows(name, _sg_mix, [p], [ln_g, ln_b, w_s, b_st], [(d, BF16)], [], SG_CHUNK)[0]


def _sg_bwd(name, p, dmix, ln_g, ln_b, w_s, b_st):
    def fn(p_, dm_, lg, lb, ws, bs):
        _, vjp = jax.vjp(_sg_mix, p_, lg, lb, ws, bs)
        return vjp(dm_)

    return _rows(name, fn, [p, dmix], [ln_g, ln_b, w_s, b_st], [(p.shape[1], BF16)],
                 [ln_g.shape, ln_b.shape, w_s.shape, b_st.shape], SG_CHUNK)


def _rope_tables(pos, inv_freq):
    ang = pos * inv_freq
    return jnp.cos(ang), jnp.sin(ang)


def _swa_pre(name, p, pos, inv_freq, d):
    kvw = SWA_KV * HEAD

    def fn(p_, pos_, fr):
        cos, sin = _rope_tables(pos_, fr)
        return (_rope(p_[:, :d], cos, sin, 1.0), _rope(p_[:, d:d + kvw], cos, sin, 1.0), p_[:, d + kvw:d + 2 * kvw])

    return _rows(name, fn, [p, pos], [inv_freq], [(d, BF16), (kvw, BF16), (kvw, BF16)], [], 256)


def _swa_attn_fwd(name, q, k, v, sinks, jobs=()):
    kv, r, t, _ = q.shape
    nb = t // SWA_BLOCK

    def body(q_ref, kp_ref, kc_ref, vp_ref, vc_ref, s_ref, o_ref):
        prev_bias = jnp.where(pl.program_id(1) > 0, 0.0, NEG).astype(F32)
        o_ref[0] = _attn_block(q_ref[0], kp_ref[0], kc_ref[0], vp_ref[0], vc_ref[0], s_ref[0], prev_bias)

    qs = pl.BlockSpec((1, r, SWA_BLOCK, HEAD), lambda g, n: (g, 0, n, 0))
    cur = pl.BlockSpec((1, SWA_BLOCK, HEAD), lambda g, n: (g, n, 0))
    prev = pl.BlockSpec((1, SWA_BLOCK, HEAD), lambda g, n: (g, jnp.maximum(n - 1, 0), 0))
    ss = pl.BlockSpec((1, r, 1, 1), lambda g, n: (g, 0, 0, 0))
    return _pcall(body, grid=(kv, nb), in_specs=[qs, prev, cur, prev, cur, ss], out_specs=[qs], out_shape=[S(q.shape, F32)],
                  name=name, semantics=("parallel", "arbitrary"), inputs=(q, k, k, v, v, sinks), jobs=jobs)[0]


def _swa_attn_bwd(name, q, k, v, sinks, do, jobs=()):
    kv, r, t, _ = q.shape
    nb = t // SWA_BLOCK

    def body(q_ref, kp_ref, kc_ref, vp_ref, vc_ref, s_ref, do_ref, dq_ref, dkc_ref, dkp_ref, dvc_ref, dvp_ref, ds_ref):
        n = pl.program_id(1)
        prev_bias = jnp.where(n > 0, 0.0, NEG).astype(F32)
        fn = functools.partial(_attn_block, prev_bias=prev_bias)
        args = [ref[0].astype(F32) for ref in (q_ref, kp_ref, kc_ref, vp_ref, vc_ref)] + [s_ref[0]]
        _, vjp = jax.vjp(fn, *args)
        dq, dkp, dkc, dvp, dvc, ds = vjp(do_ref[0])
        dq_ref[0], dkc_ref[0], dkp_ref[0], dvc_ref[0], dvp_ref[0] = dq, dkc, dkp, dvc, dvp

        @pl.when(n == 0)
        def _():
            ds_ref[0] = ds

        @pl.when(n > 0)
        def _():
            ds_ref[0] += ds

    qs = pl.BlockSpec((1, r, SWA_BLOCK, HEAD), lambda g, n: (g, 0, n, 0))
    cur = pl.BlockSpec((1, SWA_BLOCK, HEAD), lambda g, n: (g, n, 0))
    prev = pl.BlockSpec((1, SWA_BLOCK, HEAD), lambda g, n: (g, jnp.maximum(n - 1, 0), 0))
    ss = pl.BlockSpec((1, r, 1, 1), lambda g, n: (g, 0, 0, 0))
    return _pcall(body, grid=(kv, nb), in_specs=[qs, prev, cur, prev, cur, ss, qs], out_specs=[qs, cur, cur, cur, cur, ss],
                  out_shape=[S(q.shape, F32)] + [S(k.shape, F32)] * 4 + [S(sinks.shape, F32)], name=name,
                  semantics=("parallel", "arbitrary"), inputs=(q, k, k, v, v, sinks, do), jobs=jobs)


def _gate_fwd(name, o, z_src, z_off, d):
    return _rows(name, lambda o_, p_: o_ * jax.nn.silu(p_[:, z_off:z_off + d]), [o, z_src], [], [(d, BF16)], [], 256)[0]


def _gate_bwd(name, o, z_src, z_off, d, dmix):
    def fn(o_, p_, dm_):
        _, vjp = jax.vjp(lambda oo, zz: oo * jax.nn.silu(zz), o_, p_[:, z_off:z_off + d])
        return vjp(dm_)

    return _rows(name, fn, [o, z_src, dmix], [], [(d, F32), (d, F32)], [], 256)


def _swa_post_bwd(name, dq, dkc, dkp_up, dvc, dvp_up, dz, pos, inv_freq):
    def fn(dq_, dkc_, dkp_, dvc_, dvp_, dz_, pos_, fr):
        cos, sin = _rope_tables(pos_, fr)
        return jnp.concatenate([_rope(dq_, cos, sin, -1.0), _rope(dkc_ + dkp_, cos, sin, -1.0), dvc_ + dvp_, dz_], axis=1)

    n = dq.shape[1] + dkc.shape[1] + dvc.shape[1] + dz.shape[1]
    return _rows(name, fn, [dq, dkc, dkp_up, dvc, dvp_up, dz, pos], [inv_freq], [(n, BF16)], [], 256)[0]


HALO = 8


def _row_before(x, halo_ref, i):
    first = jnp.where(i > 0, halo_ref[pl.ds(HALO - 1, 1), :], 0.0)
    row = lax.broadcasted_iota(jnp.int32, x.shape, 0)
    return jnp.where(row == 0, first, pltpu.roll(x, 1, 0))


def _row_after(x, halo, i, n_tiles):
    last = jnp.where(i < n_tiles - 1, halo, 0.0)
    row = lax.broadcasted_iota(jnp.int32, x.shape, 0)
    return jnp.where(row == x.shape[0] - 1, last, pltpu.roll(x, x.shape[0] - 1, 0))


def _lerp_fwd(name, p, mu, widths):
    t, n = p.shape
    tm = 128

    def body(p_ref, halo_ref, mu_ref, *o_refs):
        x = p_ref[...]
        pm = x + (_row_before(x, halo_ref, pl.program_id(0)) - x) * mu_ref[...]
        o = 0
        for ref, w in zip(o_refs, widths):
            ref[...] = pm[:, o:o + w]
            o += w

    return pl.pallas_call(
        body, grid=(t // tm,),
        in_specs=[pl.BlockSpec((tm, n), lambda i: (i, 0)),
                  pl.BlockSpec((HALO, n), lambda i: (jnp.maximum(i * (tm // HALO) - 1, 0), 0)),
                  pl.BlockSpec((1, n), lambda i: (0, 0))],
        out_specs=[pl.BlockSpec((tm, w), lambda i: (i, 0)) for w in widths],
        out_shape=[S((t, w), F32) for w in widths], name=name, compiler_params=_params(("parallel",)))(p, p, mu)


def _lerp_bwd(name, dpm_parts, p, mu):
    t, n = p.shape
    k = len(dpm_parts)
    tm = 64
    n_tiles = t // tm

    def body(*refs):
        d_refs, dh_refs = refs[:k], refs[k:2 * k]
        p_ref, ph_ref, mu_ref, dp_ref, dmu_ref = refs[2 * k:]
        i = pl.program_id(0)
        cat = lambda vals: jnp.concatenate(vals, axis=1) if k > 1 else vals[0]
        dpm = cat([r[...] for r in d_refs])
        dnext = cat([r[pl.ds(0, 1), :] for r in dh_refs])
        x, mu_ = p_ref[...], mu_ref[...]
        dp_ref[...] = (dpm * (1.0 - mu_) + _row_after(dpm, dnext, i, n_tiles) * mu_).astype(dp_ref.dtype)
        dmu = jnp.sum(dpm * (_row_before(x, ph_ref, i) - x), axis=0, keepdims=True)

        @pl.when(i == 0)
        def _():
            dmu_ref[...] = dmu

        @pl.when(i > 0)
        def _():
            dmu_ref[...] += dmu

    per = tm // HALO
    d_specs = [pl.BlockSpec((tm, a.shape[1]), lambda i: (i, 0)) for a in dpm_parts]
    dh_specs = [pl.BlockSpec((HALO, a.shape[1]), lambda i: (jnp.minimum((i + 1) * per, t // HALO - 1), 0)) for a in dpm_parts]
    return pl.pallas_call(
        body, grid=(n_tiles,),
        in_specs=d_specs + dh_specs + [pl.BlockSpec((tm, n), lambda i: (i, 0)),
                                       pl.BlockSpec((HALO, n), lambda i: (jnp.maximum(i * per - 1, 0), 0)),
                                       pl.BlockSpec((1, n), lambda i: (0, 0))],
        out_specs=[pl.BlockSpec((tm, n), lambda i: (i, 0)), pl.BlockSpec((1, n), lambda i: (0, 0))],
        out_shape=[S((t, n), BF16), S((1, n), F32)], name=name,
        compiler_params=_params(("arbitrary",)))(*dpm_parts, *dpm_parts, p, p, mu)


def _lora_act(pl_, w0, w_lora, a0, a_lora):
    logw = -DECAY_SCALE * jax.nn.sigmoid(w0 + _dot(jnp.tanh(pl_), w_lora))
    a = jax.nn.sigmoid(a0 + _dot(pl_, a_lora))
    return logw, a


def _lora_fwd(name, pl_, w0, w_lora, a0, a_lora):
    d = w0.shape[1]
    return _rows(name, _lora_act, [pl_], [w0, w_lora, a0, a_lora], [(d, F32), (d, F32)], [], 256)


def _lora_bwd(name, pl_, dlogw, da, w0, w_lora, a0, a_lora):
    def fn(p_, dl_, da_, w0_, wl_, a0_, al_):
        _, vjp = jax.vjp(_lora_act, p_, w0_, wl_, a0_, al_)
        return vjp((dl_, da_))

    return _rows(name, fn, [pl_, dlogw, da], [w0, w_lora, a0, a_lora], [(pl_.shape[1], F32)],
                 [w0.shape, w_lora.shape, a0.shape, a_lora.shape], 256)


def _head_cols(ref, hb):
    x = ref[...]
    xo = pltpu.roll(x, x.shape[1] - HEAD, 1)
    return [(x if j % 2 == 0 else xo)[:, 2 * HEAD * (j // 2):2 * HEAD * (j // 2) + HEAD] for j in range(hb)]


def _rwkv_scan_fwd(name, r, k, v, logw, a, hp, jobs=()):
    t, d = r.shape
    h, nc, hb = d // HEAD, t // RW_CHUNK, RW_HEADS

    def body(r_ref, k_ref, v_ref, w_ref, a_ref, kk_ref, ka_ref, rk_ref, gg_ref, gb_ref, y_ref, st_ref, s_scr):
        @pl.when(pl.program_id(1) == 0)
        def _():
            s_scr[...] = jnp.zeros_like(s_scr)

        s0 = [s_scr[j] for j in range(hb)]
        for j in range(hb):
            st_ref[j, 0] = s0[j]
        y, s1 = _rwkv_chunk(s0, *[_head_cols(ref, hb) for ref in (r_ref, k_ref, v_ref, w_ref, a_ref, kk_ref, ka_ref, rk_ref,
                                                                 gg_ref, gb_ref)])
        y_ref[...] = jnp.concatenate(y, axis=1)
        for j in range(hb):
            s_scr[j] = s1[j]

    seq = pl.BlockSpec((RW_CHUNK, hb * HEAD), lambda i, n: (n, i))
    par = pl.BlockSpec((1, hb * HEAD), lambda i, n: (0, i))
    st = pl.BlockSpec((hb, 1, HEAD, HEAD), lambda i, n: (i, n, 0, 0))
    return _pcall(body, grid=(h // hb, nc), in_specs=[seq] * 5 + [par] * 5, out_specs=[seq, st],
                  out_shape=[S((t, d), F32), S((h, nc, HEAD, HEAD), F32)], scratch_shapes=[pltpu.VMEM((hb, HEAD, HEAD), F32)],
                  name=name, semantics=("parallel", "arbitrary"), inputs=(r, k, v, logw, a, *hp), jobs=jobs)


def _rwkv_scan_bwd(name, r, k, v, logw, a, hp, states, dy, jobs=()):
    t, d = r.shape
    h, nc, hb = d // HEAD, t // RW_CHUNK, RW_HEADS

    def body(r_ref, k_ref, v_ref, w_ref, a_ref, kk_ref, ka_ref, rk_ref, gg_ref, gb_ref, st_ref, dy_ref,
             dr_ref, dk_ref, dv_ref, dw_ref, da_ref, dkk_ref, dka_ref, drk_ref, dgg_ref, dgb_ref, ds_scr):
        n = pl.program_id(1)

        @pl.when(n == 0)
        def _():
            ds_scr[...] = jnp.zeros_like(ds_scr)
            for ref in (dkk_ref, dka_ref, drk_ref, dgg_ref, dgb_ref):
                ref[...] = jnp.zeros_like(ref)

        ins = [[st_ref[j, 0] for j in range(hb)]] + [_head_cols(ref, hb) for ref in (r_ref, k_ref, v_ref, w_ref, a_ref, kk_ref,
                                                                                  ka_ref, rk_ref, gg_ref, gb_ref)]
        _, vjp = jax.vjp(_rwkv_chunk, *ins)
        ds0, *dseq, dkk, dka, drk, dgg, dgb = vjp((_head_cols(dy_ref, hb), [ds_scr[j] for j in range(hb)]))
        for j in range(hb):
            ds_scr[j] = ds0[j]
        for ref, val in zip((dr_ref, dk_ref, dv_ref, dw_ref, da_ref), dseq):
            ref[...] = jnp.concatenate(val, axis=1)
        for ref, val in ((dkk_ref, dkk), (dka_ref, dka), (drk_ref, drk), (dgg_ref, dgg), (dgb_ref, dgb)):
            ref[...] += jnp.concatenate(val, axis=1)

    seq = pl.BlockSpec((RW_CHUNK, hb * HEAD), lambda i, n: (nc - 1 - n, i))
    par = pl.BlockSpec((1, hb * HEAD), lambda i, n: (0, i))
    st = pl.BlockSpec((hb, 1, HEAD, HEAD), lambda i, n: (i, nc - 1 - n, 0, 0))
    return _pcall(body, grid=(h // hb, nc), in_specs=[seq] * 5 + [par] * 5 + [st, seq], out_specs=[seq] * 5 + [par] * 5,
                  out_shape=[S((t, d), F32)] * 5 + [S((1, d), F32)] * 5, scratch_shapes=[pltpu.VMEM((hb, HEAD, HEAD), F32)],
                  name=name, semantics=("parallel", "arbitrary"), inputs=(r, k, v, logw, a, *hp, states, dy), jobs=jobs)


def _loss_head(name, x, target, g):
    d = x.shape[1]

    def fn(x_, t_, g_):
        def f(xx, gg):
            err = _rms(xx, gg) - t_
            return 0.5 * jnp.sum(jnp.mean(err * err, axis=-1, keepdims=True), axis=0, keepdims=True)

        l, vjp = jax.vjp(f, x_, g_)
        dx, dg = vjp(jnp.ones((1, 1), F32))
        return dx, dg, jnp.broadcast_to(l, (1, 128))

    return _rows(name, fn, [x, target], [g], [(d, F32)], [(1, d), (1, 128)], 256)


def _mod_fwd(name, cond_all, mod_w, mod_b_cols):
    l, d, n = mod_w.shape

    def body(c_ref, w_ref, b_ref, o_ref):
        o_ref[0] = _dot(jax.nn.silu(c_ref[...]), w_ref[0], hi=True) + b_ref[0]

    return pl.pallas_call(body, grid=(l,), in_specs=[pl.BlockSpec((N_DEV, d), lambda i: (0, 0)),
                                                      pl.BlockSpec((1, d, n), lambda i: (i, 0, 0)),
                                                      pl.BlockSpec((1, 1, n), lambda i: (i, 0, 0))],
                          out_specs=pl.BlockSpec((1, N_DEV, n), lambda i: (i, 0, 0)), out_shape=S((l, N_DEV, n), F32),
                          name=name, compiler_params=_params(("parallel",)))(cond_all, mod_w, mod_b_cols)


def _mod_bwd(name, cond_all, dmod_cols, dmod_all):
    l, _, n = dmod_cols.shape
    d = cond_all.shape[1]
    nb = dmod_all.shape[2]

    def body(c_ref, dc_ref, da_ref, gw_ref, gb_ref):
        gw_ref[0] = _dot(jax.nn.silu(c_ref[...]), dc_ref[0], "tn", hi=True)
        acc = da_ref[0, 0:1, :]
        for bi in range(1, N_DEV):
            acc = acc + da_ref[0, bi:bi + 1, :]
        gb_ref[0] = acc

    return pl.pallas_call(body, grid=(l,), in_specs=[pl.BlockSpec((N_DEV, d), lambda i: (0, 0)),
                                                      pl.BlockSpec((1, N_DEV, n), lambda i: (i, 0, 0)),
                                                      pl.BlockSpec((1, N_DEV, nb), lambda i: (i, 0, 0))],
                          out_specs=[pl.BlockSpec((1, d, n), lambda i: (i, 0, 0)), pl.BlockSpec((1, 1, nb), lambda i: (i, 0, 0))],
                          out_shape=[S((l, d, n), F32), S((l, 1, nb), F32)], name=name,
                          compiler_params=_params(("parallel",)))(cond_all, dmod_cols, dmod_all)


def _to_heads(a):
    t, n = a.shape
    return a.reshape(t, n // HEAD, HEAD).transpose(1, 0, 2)


def _from_heads(a):
    h, t, _ = a.shape
    return a.transpose(1, 0, 2).reshape(t, h * HEAD)


def _shift_up(a, n=1):
    return jnp.concatenate([a[n:], jnp.zeros_like(a[:n])], axis=0)


def _cols_full(g):
    return g.transpose(1, 0, 2).reshape(g.shape[1], -1)


def _cols_parts(full):
    r, n = full.shape
    return full.reshape(r, N_DEV, n // N_DEV).transpose(1, 0, 2)


def _pack(arrs, mult=1024):
    flat = jnp.concatenate([a.reshape(-1) for a in arrs])
    pad = (-flat.shape[0]) % mult
    return jnp.pad(flat, (0, pad)).reshape(-1, 128)


def _unpack(flat, shapes):
    out, o = [], 0
    for s in shapes:
        n = math.prod(s)
        out.append(flat[o:o + n].reshape(s))
        o += n
    return out


def _local_step(x, pos, target, mods, norm_g, final_norm_g, layer_weights, hooks=None, on_grads=None):
    t, d = x.shape
    hooks = hooks or {}
    jobs = lambda nm: hooks.get(nm, ())
    kinds = [i % 3 for i in range(DEPTH)]
    inv_freq = (ROPE_THETA ** (-jnp.arange(HEAD // 2, dtype=F32) / (HEAD // 2)))
    inv_freq = jnp.tile(inv_freq, 128 // (HEAD // 2)).reshape(1, 128)
    saved = []
    for i, kind in enumerate(kinds):
        lw = layer_weights(i)
        shift, scale, gate = (mods[i, q * d:(q + 1) * d].reshape(1, d) for q in range(3))
        g = norm_g[i].reshape(1, d)
        h = _norm_fwd(f"norm_fwd{i}", x, g, shift, scale)
        sv = dict(x=x, h=h, g=g, shift=shift, scale=scale, gate=gate, lw=lw)
        if kind == 0:
            p = _mm(f"sg_in{i}", h, lw["w_in"], "nn", F32, jobs=jobs(f"sg_in{i}"))
            mix = _sg_fwd(f"sg_mix{i}", p, lw["ln_g"], lw["ln_b"], lw["w_s"], lw["b_st"])
            sv.update(p=p)
        elif kind == 1:
            p = _mm(f"swa_in{i}", h, lw["w_in"], "nn", F32, jobs=jobs(f"swa_in{i}"))
            q, k, v = _swa_pre(f"swa_pre{i}", p, pos, inv_freq, d)
            qh = _to_heads(q).reshape(SWA_KV, SWA_REP, t, HEAD)
            kh, vh = _to_heads(k), _to_heads(v)
            o = _swa_attn_fwd(f"swa_attn{i}", qh, kh, vh, lw["sinks"], jobs=jobs(f"swa_attn{i}"))
            o = _from_heads(o.reshape(SWA_KV * SWA_REP, t, HEAD))
            mix = _gate_fwd(f"swa_gate{i}", o, p, d + 2 * SWA_KV * HEAD, d)
            sv.update(p=p, qh=qh, kh=kh, vh=vh, o=o)
        else:
            pm = _mm(f"rw_in{i}", h, lw["w_main"], "nn", F32)
            plo = _mm(f"rw_inl{i}", h, lw["w_lorain"], "nn", F32)
            r, k, v, z = _lerp_fwd(f"rw_lerp{i}", pm, lw["mu_main"], [d] * 4)
            (pll,) = _lerp_fwd(f"rw_lerpl{i}", plo, lw["mu_lora"], [LORA_PAD])
            logw, a = _lora_fwd(f"rw_lora{i}", pll, lw["w0"], lw["w_lora"], lw["a0"], lw["a_lora"])
            seqs = (r, k, v, logw, a)
            o, states = _rwkv_scan_fwd(f"rw_scan{i}", *seqs, lw["hp"], jobs=jobs(f"rw_scan{i}"))
            mix = _gate_fwd(f"rw_gate{i}", o, z, 0, d)
            sv.update(pm=pm, plo=plo, pll=pll, z=z, seqs=seqs, states=states, o=o)
        y = _mm(f"out{i}", mix, lw["w_out"], "nn", F32, jobs=jobs(f"out{i}"))
        sv.update(mix=mix, y=y)
        saved.append(sv)
        x = _resid_fwd(f"resid{i}", x, y, gate)

    dx, d_final_g, loss = _loss_head("loss_head", x, target, final_norm_g.reshape(1, d))

    grads = dict(norm_g=[None] * DEPTH, sg_w_in=[None] * 2, sg_w_out=[None] * 2, sg_ln_g=[None] * 2, sg_ln_b=[None] * 2,
                 sg_w_s=[None] * 2, sg_b_st=[None] * 2, final_norm_g=d_final_g)
    dmods = [None] * DEPTH
    for i in reversed(range(DEPTH)):
        kind, j, sv = kinds[i], i // 3, saved[i]
        lw = sv["lw"]
        dy, dgate = _resid_bwd(f"resid_bwd{i}", dx, sv["y"], sv["gate"])
        d_w_out = _mm(f"out_dw{i}", sv["mix"], dy, "tn", F32)
        dmix = _mm(f"out_dx{i}", dy, lw["w_out"], "nt", F32)
        if kind == 0:
            dp, dlg, dlb, dws, dbs = _sg_bwd(f"sg_mix_bwd{i}", sv["p"], dmix, lw["ln_g"], lw["ln_b"], lw["w_s"], lw["b_st"])
            grads["sg_ln_g"][j], grads["sg_ln_b"][j], grads["sg_w_s"][j], grads["sg_b_st"][j] = dlg, dlb, dws, dbs
            grads["sg_w_out"][j] = d_w_out
            grads["sg_w_in"][j] = _mm(f"sg_in_dw{i}", sv["h"], dp, "tn", F32, jobs=jobs(f"sg_in_dw{i}"))
            dh = _mm(f"sg_in_dx{i}", dp, lw["w_in"], "nt", F32, jobs=jobs(f"sg_in_dx{i}"))
        elif kind == 1:
            z_off = d + 2 * SWA_KV * HEAD
            do, dz = _gate_bwd(f"swa_gate_bwd{i}", sv["o"], sv["p"], z_off, d, dmix)
            doh = _to_heads(do).reshape(SWA_KV, SWA_REP, t, HEAD)
            dq, dkc, dkp, dvc, dvp, dsinks = _swa_attn_bwd(f"swa_attn_bwd{i}", sv["qh"], sv["kh"], sv["vh"], lw["sinks"], doh,
                                                           jobs=jobs(f"swa_attn_bwd{i}"))
            dq = _from_heads(dq.reshape(SWA_KV * SWA_REP, t, HEAD))
            dkc, dvc = _from_heads(dkc), _from_heads(dvc)
            dkp, dvp = _shift_up(_from_heads(dkp), SWA_BLOCK), _shift_up(_from_heads(dvp), SWA_BLOCK)
            dp = _swa_post_bwd(f"swa_post_bwd{i}", dq, dkc, dkp, dvc, dvp, dz, pos, inv_freq)
            grads.update(swa_sinks=dsinks, swa_w_out=d_w_out)
            grads["swa_w_in"] = _mm(f"swa_in_dw{i}", sv["h"], dp, "tn", F32)
            dh = _mm(f"swa_in_dx{i}", dp, lw["w_in"], "nt", F32)
        else:
            do, dz = _gate_bwd(f"rw_gate_bwd{i}", sv["o"], sv["z"], 0, d, dmix)
            res = _rwkv_scan_bwd(f"rw_scan_bwd{i}", *sv["seqs"], lw["hp"], sv["states"], do, jobs=jobs(f"rw_scan_bwd{i}"))
            dr, dk, dv, dlogw, da = res[:5]
            dpll, dw0, dwl, da0, dal = _lora_bwd(f"rw_lora_bwd{i}", sv["pll"], dlogw, da, lw["w0"], lw["w_lora"],
                                                  lw["a0"], lw["a_lora"])
            dpm, dmu_main = _lerp_bwd(f"rw_lerp_bwd{i}", [dr, dk, dv, dz], sv["pm"], lw["mu_main"])
            dpl, dmu_lora = _lerp_bwd(f"rw_lerpl_bwd{i}", [dpll], sv["plo"], lw["mu_lora"])
            grads.update(rw_w_out=d_w_out, rw_hp=res[5:], rw_w0=dw0, rw_w_lora=dwl, rw_a0=da0, rw_a_lora=dal,
                         rw_mu_main=dmu_main, rw_mu_lora=dmu_lora)
            grads["rw_w_main"] = _mm(f"rw_in_dw{i}", sv["h"], dpm, "tn", F32)
            grads["rw_w_lorain"] = _mm(f"rw_inl_dw{i}", sv["h"], dpl, "tn", F32)
            dh = _mm(f"rw_inl_dx{i}", dpl, lw["w_lorain"], "nt", F32)
            dh = _mm(f"rw_in_dx{i}", dpm, lw["w_main"], "nt", F32, add=dh)
        if on_grads is not None:
            on_grads(i, grads)
        dx, dg, dshift, dscale = _norm_bwd(f"norm_bwd{i}", sv["x"], dh, dx, sv["g"], sv["shift"], sv["scale"])
        grads["norm_g"][i] = dg
        dmods[i] = jnp.concatenate([dshift, dscale, dgate], axis=1)
    return loss, dx, jnp.concatenate(dmods, axis=0), grads


def kernel(x, c, positions, norm_g, mod_w, mod_b, final_norm_g, sg_w_in, sg_w_out, sg_ln_g, sg_ln_b, sg_w_spatial, sg_b_spatial, swa_w_in, swa_w_out, swa_sinks, rwkv_w_in, rwkv_w_out, rwkv_mu, rwkv_w0, rwkv_w_lora, rwkv_a0, rwkv_a_lora, rwkv_k_k, rwkv_k_a, rwkv_r_k, rwkv_gn_g, rwkv_gn_b, loss_target, m_norm_g, m_mod_w, m_mod_b, m_final_norm_g, m_sg_w_in, m_sg_w_out, m_sg_ln_g, m_sg_ln_b, m_sg_w_spatial, m_sg_b_spatial, m_swa_w_in, m_swa_w_out, m_swa_sinks, m_rwkv_w_in, m_rwkv_w_out, m_rwkv_mu, m_rwkv_w0, m_rwkv_w_lora, m_rwkv_a0, m_rwkv_a_lora, m_rwkv_k_k, m_rwkv_k_a, m_rwkv_r_k, m_rwkv_gn_g, m_rwkv_gn_b, v_norm_g, v_mod_w, v_mod_b, v_final_norm_g, v_sg_w_in, v_sg_w_out, v_sg_ln_g, v_sg_ln_b, v_sg_w_spatial, v_sg_b_spatial, v_swa_w_in, v_swa_w_out, v_swa_sinks, v_rwkv_w_in, v_rwkv_w_out, v_rwkv_mu, v_rwkv_w0, v_rwkv_w_lora, v_rwkv_a0, v_rwkv_a_lora, v_rwkv_k_k, v_rwkv_k_a, v_rwkv_r_k, v_rwkv_gn_g, v_rwkv_gn_b):
    weights = dict(norm_g=norm_g, mod_w=mod_w, mod_b=mod_b, final_norm_g=final_norm_g, sg_w_in=sg_w_in, sg_w_out=sg_w_out,
                   sg_ln_g=sg_ln_g, sg_ln_b=sg_ln_b, sg_w_spatial=sg_w_spatial, sg_b_spatial=sg_b_spatial, swa_w_in=swa_w_in,
                   swa_w_out=swa_w_out, swa_sinks=swa_sinks, rwkv_w_in=rwkv_w_in, rwkv_w_out=rwkv_w_out, rwkv_mu=rwkv_mu,
                   rwkv_w0=rwkv_w0, rwkv_w_lora=rwkv_w_lora, rwkv_a0=rwkv_a0, rwkv_a_lora=rwkv_a_lora, rwkv_k_k=rwkv_k_k,
                   rwkv_k_a=rwkv_k_a, rwkv_r_k=rwkv_r_k, rwkv_gn_g=rwkv_gn_g, rwkv_gn_b=rwkv_gn_b)
    mom_m = dict(norm_g=m_norm_g, mod_w=m_mod_w, mod_b=m_mod_b, final_norm_g=m_final_norm_g, sg_w_in=m_sg_w_in,
                 sg_w_out=m_sg_w_out, sg_ln_g=m_sg_ln_g, sg_ln_b=m_sg_ln_b, sg_w_spatial=m_sg_w_spatial,
                 sg_b_spatial=m_sg_b_spatial, swa_w_in=m_swa_w_in, swa_w_out=m_swa_w_out, swa_sinks=m_swa_sinks,
                 rwkv_w_in=m_rwkv_w_in, rwkv_w_out=m_rwkv_w_out, rwkv_mu=m_rwkv_mu, rwkv_w0=m_rwkv_w0,
                 rwkv_w_lora=m_rwkv_w_lora, rwkv_a0=m_rwkv_a0, rwkv_a_lora=m_rwkv_a_lora, rwkv_k_k=m_rwkv_k_k,
                 rwkv_k_a=m_rwkv_k_a, rwkv_r_k=m_rwkv_r_k, rwkv_gn_g=m_rwkv_gn_g, rwkv_gn_b=m_rwkv_gn_b)
    mom_v = dict(norm_g=v_norm_g, mod_w=v_mod_w, mod_b=v_mod_b, final_norm_g=v_final_norm_g, sg_w_in=v_sg_w_in,
                 sg_w_out=v_sg_w_out, sg_ln_g=v_sg_ln_g, sg_ln_b=v_sg_ln_b, sg_w_spatial=v_sg_w_spatial,
                 sg_b_spatial=v_sg_b_spatial, swa_w_in=v_swa_w_in, swa_w_out=v_swa_w_out, swa_sinks=v_swa_sinks,
                 rwkv_w_in=v_rwkv_w_in, rwkv_w_out=v_rwkv_w_out, rwkv_mu=v_rwkv_mu, rwkv_w0=v_rwkv_w0,
                 rwkv_w_lora=v_rwkv_w_lora, rwkv_a0=v_rwkv_a0, rwkv_a_lora=v_rwkv_a_lora, rwkv_k_k=v_rwkv_k_k,
                 rwkv_k_a=v_rwkv_k_a, rwkv_r_k=v_rwkv_r_k, rwkv_gn_g=v_rwkv_gn_g, rwkv_gn_b=v_rwkv_gn_b)
    names = list(weights)
    t, d = x.shape[1], x.shape[2]
    me = 4 * lax.axis_index("x") + 2 * lax.axis_index("y") + lax.axis_index("c")
    n_mod = mod_w.shape[2]
    n_rw = rwkv_w_in.shape[2]

    small_names = ["sg_ln_g", "sg_ln_b", "rwkv_mu", "rwkv_w0", "rwkv_a0", "rwkv_k_k", "rwkv_k_a", "rwkv_gn_g", "rwkv_gn_b",
                   "rwkv_w_lora", "rwkv_a_lora"]
    small_shapes = [weights[n].shape for n in small_names]
    pk = _pack([c] + [weights[n] for n in small_names])
    gathered = _gather("gather_small", pk).reshape(N_DEV, -1)
    c_all = gathered[:, :d]
    per_dev = [_unpack(gathered[dv, d:], small_shapes) for dv in range(N_DEV)]
    full_small = {}
    for q, n in enumerate(small_names):
        full_small[n] = jnp.concatenate([per_dev[dv][q] for dv in range(N_DEV)], axis=-1)

    mod_b_cols = lax.dynamic_slice_in_dim(mod_b, me * n_mod, n_mod, axis=1).reshape(DEPTH, 1, n_mod)
    mod_part = _mod_fwd("mod_fwd", c_all, mod_w, mod_b_cols)
    mod_g = _gather("gather_mod", mod_part.reshape(DEPTH * N_DEV, n_mod))
    mod_g = mod_g.reshape(N_DEV, DEPTH, N_DEV, n_mod)
    mods = lax.dynamic_index_in_dim(mod_g, me, axis=2, keepdims=False)
    mods = mods.transpose(1, 0, 2).reshape(DEPTH, N_DEV * n_mod)

    job = lambda cls, src: dict(cls=cls, src=src)
    gj = dict(swa_in=job(_Gather, swa_w_in[0].astype(BF16)), swa_out=job(_Gather, swa_w_out[0].astype(BF16)),
              rw_in=job(_Gather, rwkv_w_in[0].astype(BF16)), rw_out=job(_Gather, rwkv_w_out[0].astype(BF16)),
              sg_in1=job(_Gather, sg_w_in[1].astype(BF16)), sg_out1=job(_Gather, sg_w_out[1].astype(BF16)))
    hooks = {"sg_in0": [gj["swa_in"]], "out0": [gj["swa_out"]], "swa_in1": [gj["rw_out"]], "swa_attn1": [gj["rw_in"]],
             "rw_scan2": [gj["sg_in1"], gj["sg_out1"]]}
    g_sg_in0 = _gather("gather_sg_in0", sg_w_in[0].astype(BF16))
    g_sg_out0 = _gather("gather_sg_out0", sg_w_out[0].astype(BF16))
    lora_rows = lambda w, off: jnp.zeros((LORA_PAD, d), F32).at[off:off + LORA].set(w)
    mu = full_small["rwkv_mu"].reshape(1, -1)
    heads = lambda a: a.reshape(1, -1)

    def layer_weights(i):
        if i % 3 == 0:
            j = i // 3
            g_in, g_out = (g_sg_in0, g_sg_out0) if j == 0 else (gj["sg_in1"]["out"], gj["sg_out1"]["out"])
            return dict(w_in=_cols_full(g_in), w_out=g_out.reshape(d, d), ln_g=full_small["sg_ln_g"][j].reshape(1, d),
                        ln_b=full_small["sg_ln_b"][j].reshape(1, d), w_s=sg_w_spatial[j], b_st=sg_b_spatial[j].T)
        if i % 3 == 1:
            return dict(w_in=_cols_full(gj["swa_in"]["out"]), w_out=gj["swa_out"]["out"].reshape(d, d),
                        sinks=swa_sinks.reshape(SWA_KV, SWA_REP, 1, 1))
        rw_in_full = _cols_full(gj["rw_in"]["out"])
        return dict(w_main=rw_in_full[:, :4 * d], w_lorain=jnp.pad(rw_in_full[:, 4 * d:], ((0, 0), (0, LORA_PAD - 2 * LORA))),
                    w_out=gj["rw_out"]["out"].reshape(d, d), mu_main=mu[:, :4 * d],
                    mu_lora=jnp.pad(mu[:, 4 * d:], ((0, 0), (0, LORA_PAD - 2 * LORA))),
                    w0=full_small["rwkv_w0"], a0=full_small["rwkv_a0"],
                    w_lora=lora_rows(full_small["rwkv_w_lora"][0], 0), a_lora=lora_rows(full_small["rwkv_a_lora"][0], LORA),
                    hp=[heads(full_small["rwkv_k_k"]), heads(full_small["rwkv_k_a"]), heads(rwkv_r_k),
                        heads(full_small["rwkv_gn_g"]), heads(full_small["rwkv_gn_b"])])

    sj = {}

    def on_grads(i, g):
        pairs = lambda nm, parts: job(_Chips, _scatter_pairs(nm, parts.astype(BF16)))
        if i == 3:
            sj["sg_in1"] = pairs("scatter_sg_in1", _cols_parts(g["sg_w_in"][1]))
            sj["sg_out1"] = pairs("scatter_sg_out1", g["sg_w_out"][1].reshape(N_DEV, -1, d))
            hooks["rw_scan_bwd2"] = [sj["sg_in1"], sj["sg_out1"]]
        elif i == 2:
            d_rw_in = jnp.concatenate([g["rw_w_main"], g["rw_w_lorain"][:, :2 * LORA]], axis=1)
            sj["rw_in"] = pairs("scatter_rw_in", _cols_parts(d_rw_in))
            sj["rw_out"] = pairs("scatter_rw_out", g["rw_w_out"].reshape(N_DEV, -1, d))
            hooks["swa_attn_bwd1"] = [sj["rw_in"], sj["rw_out"]]
        elif i == 1:
            sj["swa_in"] = pairs("scatter_swa_in", _cols_parts(g["swa_w_in"]))
            sj["swa_out"] = pairs("scatter_swa_out", g["swa_w_out"].reshape(N_DEV, -1, d))
            hooks["sg_in_dx0"], hooks["sg_in_dw0"] = [sj["swa_in"]], [sj["swa_out"]]
        else:
            for nm, parts in (("sg_in0", _cols_parts(g["sg_w_in"][0])), ("sg_out0", g["sg_w_out"][0].reshape(N_DEV, -1, d))):
                sj[nm] = pairs("scatter_" + nm, parts)
                sj[nm]["out"] = _exchange_call("scatter_" + nm + "_chips", _Chips, sj[nm]["src"])

    loss, dx, dmods, g = _local_step(x[0], positions.reshape(t, 1).astype(F32), loss_target[0], mods, norm_g, final_norm_g,
                                     layer_weights, hooks, on_grads)

    dmod_g = _gather("gather_dmod", dmods)
    dmod_all = dmod_g.transpose(1, 0, 2)
    dmod_cols = lax.dynamic_slice_in_dim(dmod_all, me * n_mod, n_mod, axis=2)
    g_mod_w, g_mod_b = _mod_bwd("mod_bwd", c_all, dmod_cols, dmod_all)

    d_b_sp = [g["sg_b_st"][j].T for j in range(2)]
    rep = [loss[0, :1], jnp.concatenate(g["norm_g"], axis=0), g["final_norm_g"], jnp.stack(g["sg_w_s"]), jnp.stack(d_b_sp),
           g["swa_sinks"], g["rw_hp"][2]]
    rep_shapes = [(1,), norm_g.shape, final_norm_g.shape, sg_w_spatial.shape, sg_b_spatial.shape, swa_sinks.shape, rwkv_r_k.shape]
    rep_sum = _sum_parts("sum_rep", _gather("gather_rep", _pack(rep, 128 * 256))).reshape(-1)
    loss_tot, g_norm_g, g_final, g_w_sp, g_b_sp, g_sinks, g_r_k = _unpack(rep_sum, rep_shapes)

    p_sg_in = jnp.concatenate([sj["sg_in0"]["out"], sj["sg_in1"]["out"]], axis=1)
    p_sg_out = jnp.concatenate([sj["sg_out0"]["out"], sj["sg_out1"]["out"]], axis=1)
    p_swa_in, p_swa_out, p_rw_in, p_rw_out = (sj[nm]["out"] for nm in ("swa_in", "swa_out", "rw_in", "rw_out"))
    d_mu = jnp.concatenate([g["rw_mu_main"], g["rw_mu_lora"][:, :2 * LORA]], axis=1)
    hp_flat = lambda a: a.reshape(1, -1)
    small_grads = dict(sg_ln_g=jnp.concatenate(g["sg_ln_g"], axis=0), sg_ln_b=jnp.concatenate(g["sg_ln_b"], axis=0), rwkv_mu=d_mu,
                       rwkv_w0=g["rw_w0"], rwkv_a0=g["rw_a0"], rwkv_k_k=hp_flat(g["rw_hp"][0]), rwkv_k_a=hp_flat(g["rw_hp"][1]),
                       rwkv_gn_g=hp_flat(g["rw_hp"][3]), rwkv_gn_b=hp_flat(g["rw_hp"][4]),
                       rwkv_w_lora=g["rw_w_lora"][None, :LORA], rwkv_a_lora=g["rw_a_lora"][None, LORA:2 * LORA])
    per_dest = []
    for dv in range(N_DEV):
        shards = []
        for n in small_names:
            full, w = small_grads[n], weights[n].shape[-1]
            shards.append(full[..., dv * w:(dv + 1) * w])
        per_dest.append(_pack(shards))
    small_parts = _exchange("scatter_small", jnp.stack(per_dest), True)

    out_g, out_d, out_m, out_v = {}, {}, {}, {}

    def update(name, grad, shape2d):
        w2, m2, v2 = (a[name].reshape(shape2d) for a in (weights, mom_m, mom_v))
        gg, dd, mm, vv = _adamw("adamw_" + name, w2, grad, m2, v2)
        shp = weights[name].shape
        out_g[name], out_d[name], out_m[name], out_v[name] = gg.reshape(shp), dd.reshape(shp), mm.reshape(shp), vv.reshape(shp)

    update("mod_w", g_mod_w.reshape(-1, n_mod), (-1, n_mod))
    update("sg_w_in", p_sg_in, (-1, sg_w_in.shape[2]))
    update("sg_w_out", p_sg_out, (-1, d))
    update("swa_w_in", p_swa_in, (-1, swa_w_in.shape[2]))
    update("swa_w_out", p_swa_out, (-1, d))
    update("rwkv_w_in", p_rw_in, (-1, n_rw))
    update("rwkv_w_out", p_rw_out, (-1, d))
    update("sg_w_spatial", g_w_sp.reshape(-1, 128), (-1, 128))
    w_pk, m_pk, v_pk = (_pack([a[n] for n in small_names]) for a in (weights, mom_m, mom_v))
    res = _adamw("adamw_small", w_pk, small_parts, m_pk, v_pk)
    for q, arrs in enumerate(zip(*[_unpack(r_.reshape(-1), small_shapes) for r_ in res])):
        out_g[small_names[q]], out_d[small_names[q]], out_m[small_names[q]], out_v[small_names[q]] = arrs
    rep_names = ["norm_g", "mod_b", "final_norm_g", "sg_b_spatial", "swa_sinks", "rwkv_r_k"]
    rep_grads = [g_norm_g, g_mod_b.reshape(mod_b.shape), g_final, g_b_sp, g_sinks, g_r_k]
    rep_shapes2 = [weights[n].shape for n in rep_names]
    w_pk, m_pk, v_pk = (_pack([a[n] for n in rep_names]) for a in (weights, mom_m, mom_v))
    res = _adamw("adamw_rep", w_pk, _pack(rep_grads), m_pk, v_pk)
    for q, arrs in enumerate(zip(*[_unpack(r_.reshape(-1), rep_shapes2) for r_ in res])):
        out_g[rep_names[q]], out_d[rep_names[q]], out_m[rep_names[q]], out_v[rep_names[q]] = arrs

    return (loss_tot.reshape(()), dx[None], *[out_g[n] for n in names], *[out_d[n] for n in names],
            *[out_m[n] for n in names], *[out_v[n] for n in names])
```

```python
import functools
import math

import jax
import jax.numpy as jnp
from jax import lax
from jax.experimental import pallas as pl
from jax.experimental.pallas import tpu as pltpu

F32, BF16 = jnp.float32, jnp.bfloat16
HI = lax.Precision.HIGHEST
S = jax.ShapeDtypeStruct
MESH = pl.DeviceIdType.MESH

N_DEV = 8
DEPTH = 4
HEAD = 64
SG_GROUPS = 16
SG_CHUNK = 128
SWA_BLOCK = 128
SWA_KV = 4
SWA_REP = 8
ROPE_THETA = 10000.0
LORA = 96
LORA_PAD = 256
RW_CHUNK = 64
RW_HEADS = 16
RW_PREC = lax.Precision.HIGH
DECAY_SCALE = math.exp(-0.5)
GN_EPS = 64e-5
RMS_EPS = 1e-6
LN_EPS = 1e-5
NEG = -1e30
ADAM_LR, ADAM_B1, ADAM_B2, ADAM_EPS, ADAM_WD, ADAM_STEP = 0.001, 0.9, 0.999, 1e-08, 0.01, 10
VMEM_MB = 56


def _params(sem=None):
    kw = dict(vmem_limit_bytes=VMEM_MB << 20)
    if sem is not None:
        kw["dimension_semantics"] = sem
    return pltpu.CompilerParams(**kw)


def _pick(n, opts):
    for o in opts:
        if n % o == 0:
            return o
    raise ValueError(f"no tile for {n}")


def _rows(name, fn, rows, consts, out_rows, out_accs, tm, jobs=()):
    t = rows[0].shape[0]
    nr, nc, no = len(rows), len(consts), len(out_rows)

    def body(*refs):
        outs = fn(*[r[...] for r in refs[:nr + nc]])
        if not isinstance(outs, (tuple, list)):
            outs = (outs,)
        for r, o in zip(refs[nr + nc:nr + nc + no], outs[:no]):
            r[...] = o.astype(r.dtype)
        i = pl.program_id(0)
        for r, o in zip(refs[nr + nc + no:], outs[no:]):
            @pl.when(i == 0)
            def _(r=r, o=o):
                r[...] = o.astype(r.dtype)

            @pl.when(i > 0)
            def _(r=r, o=o):
                r[...] += o.astype(r.dtype)

    in_specs = [pl.BlockSpec((tm, a.shape[1]), lambda i: (i, 0)) for a in rows]
    in_specs += [pl.BlockSpec(c.shape, lambda i, nd=c.ndim: (0,) * nd) for c in consts]
    out_specs = [pl.BlockSpec((tm, n), lambda i: (i, 0)) for n, _ in out_rows]
    out_specs += [pl.BlockSpec(s, lambda i, nd=len(s): (0,) * nd) for s in out_accs]
    out_shape = [S((t, n), dt) for n, dt in out_rows] + [S(s, F32) for s in out_accs]
    return _pcall(body, grid=(t // tm,), in_specs=in_specs, out_specs=out_specs, out_shape=out_shape, name=name,
                  semantics=("arbitrary",), inputs=(*rows, *consts), jobs=jobs)


_DN = {"nn": (((1,), (0,)), ((), ())), "nt": (((1,), (1,)), ((), ())), "tn": (((0,), (0,)), ((), ()))}


def _mm(name, a, b, mode, out_dtype, add=None, jobs=()):
    if mode == "nn":
        (m, k), (_, n) = a.shape, b.shape
    elif mode == "nt":
        (m, k), (n, _) = a.shape, b.shape
    else:
        (k, m), (_, n) = a.shape, b.shape
    tm, tn, tk = _pick(m, (512, 256, 128)), _pick(n, (1024, 512, 384, 256, 128)), _pick(k, (2048, 1536, 1024, 512, 384, 256, 128))
    nk = k // tk
    has_add = add is not None

    def body(*refs):
        a_ref, b_ref = refs[0], refs[1]
        o_ref, acc = refs[-2], refs[-1]
        kk = pl.program_id(2)
        prod = lax.dot_general(a_ref[...].astype(BF16), b_ref[...].astype(BF16), _DN[mode], preferred_element_type=F32)
        if nk == 1:
            o_ref[...] = (prod + refs[2][...].astype(F32) if has_add else prod).astype(o_ref.dtype)
            return

        @pl.when(kk == 0)
        def _():
            acc[...] = prod + refs[2][...].astype(F32) if has_add else prod

        @pl.when(kk > 0)
        def _():
            acc[...] += prod

        @pl.when(kk == nk - 1)
        def _():
            o_ref[...] = acc[...].astype(o_ref.dtype)

    a_spec = pl.BlockSpec((tk, tm), lambda i, j, q: (q, i)) if mode == "tn" else pl.BlockSpec((tm, tk), lambda i, j, q: (i, q))
    b_spec = pl.BlockSpec((tn, tk), lambda i, j, q: (j, q)) if mode == "nt" else pl.BlockSpec((tk, tn), lambda i, j, q: (q, j))
    o_spec = pl.BlockSpec((tm, tn), lambda i, j, q: (i, j))
    ins, specs = [a, b], [a_spec, b_spec]
    if has_add:
        ins.append(add)
        specs.append(o_spec)
    return _pcall(body, grid=(m // tm, n // tn, nk), in_specs=specs, out_specs=[o_spec], out_shape=[S((m, n), out_dtype)],
                  scratch_shapes=[pltpu.VMEM((tm, tn), F32)], name=name, semantics=("parallel", "parallel", "arbitrary"),
                  inputs=ins, jobs=jobs)[0]


def _exchange(name, src, scatter):
    blk = src.shape[1:] if scatter else src.shape

    def body(src_ref, dst_ref, send_sems, recv_sems, loc_sem):
        x, y, c = lax.axis_index("x"), lax.axis_index("y"), lax.axis_index("c")
        me = 4 * x + 2 * y + c

        def mine(d):
            return src_ref.at[d] if scatter else src_ref

        local = pltpu.make_async_copy(mine(me), dst_ref.at[me], loc_sem)
        local.start()
        sends, peers = [], []
        for k in range(1, N_DEV):
            px = 1 - x if k & 4 else x
            py = 1 - y if k & 2 else y
            pc = 1 - c if k & 1 else c
            pid = 4 * px + 2 * py + pc
            cp = pltpu.make_async_remote_copy(src_ref=mine(pid), dst_ref=dst_ref.at[me], send_sem=send_sems.at[k - 1],
                                              recv_sem=recv_sems.at[k - 1], device_id=(px, py, pc), device_id_type=MESH)
            cp.start()
            sends.append(cp)
            peers.append((pid, (px, py, pc)))
        for k in range(1, N_DEV):
            pid, dev = peers[k - 1]
            pltpu.make_async_remote_copy(src_ref=mine(pid), dst_ref=dst_ref.at[pid], send_sem=send_sems.at[k - 1],
                                         recv_sem=recv_sems.at[k - 1], device_id=dev, device_id_type=MESH).wait_recv()
        for cp in sends:
            cp.wait_send()
        local.wait()

    return pl.pallas_call(
        body, out_shape=S((N_DEV,) + tuple(blk), src.dtype),
        in_specs=[pl.BlockSpec(memory_space=pl.ANY)], out_specs=pl.BlockSpec(memory_space=pl.ANY),
        scratch_shapes=[pltpu.SemaphoreType.DMA((N_DEV - 1,)), pltpu.SemaphoreType.DMA((N_DEV - 1,)),
                        pltpu.SemaphoreType.DMA],
        name=name)(src)


class _Gather:
    @staticmethod
    def out_shape(src):
        return S((N_DEV,) + tuple(src.shape), src.dtype)

    scratch = (pltpu.SemaphoreType.DMA((N_DEV - 1,)), pltpu.SemaphoreType.DMA((N_DEV - 1,)), pltpu.SemaphoreType.DMA)

    def __init__(self, src_ref, dst_ref, send_sems, recv_sems, loc_sem):
        self.refs = (src_ref, dst_ref, send_sems, recv_sems, loc_sem)
        x, y, c = lax.axis_index("x"), lax.axis_index("y"), lax.axis_index("c")
        self.c, self.me, self.sibling = c, (x, y, c), (x, y, 1 - c)
        self.chips = [(1 - x, y), (x, 1 - y), (1 - x, 1 - y)]

    def rows(self, px, py, pc):
        return self.refs[1].at[4 * px + 2 * py + pc]

    def copy(self, k, block, to, own=False):
        src_ref, _, send_sems, recv_sems, _ = self.refs
        return pltpu.make_async_remote_copy(src_ref=src_ref if own else self.rows(*block), dst_ref=self.rows(*block),
                                            send_sem=send_sems.at[k], recv_sem=recv_sems.at[k], device_id=to,
                                            device_id_type=MESH)

    def local(self):
        return pltpu.make_async_copy(self.refs[0], self.rows(*self.me), self.refs[4])

    def first(self):
        return [self.copy(0, self.me, self.sibling, own=True)] + [self.copy(1 + j, self.me, (*chip, self.c), own=True)
                                                                  for j, chip in enumerate(self.chips)]

    def start(self):
        self.local().start()
        for cp in self.first():
            cp.start()

    def finish(self):
        c = self.c
        passed = [self.copy(4 + j, (*chip, c), self.sibling) for j, chip in enumerate(self.chips)]
        for j, chip in enumerate(self.chips):
            self.copy(1 + j, (*chip, c), self.me).wait_recv()
            passed[j].start()
        self.copy(0, self.sibling, self.me).wait_recv()
        for j, chip in enumerate(self.chips):
            self.copy(4 + j, (*chip, 1 - c), self.me).wait_recv()
        for cp in self.first() + passed:
            cp.wait_send()
        self.local().wait()


class _Chips:
    @staticmethod
    def out_shape(src):
        return S(src.shape, src.dtype)

    scratch = (pltpu.SemaphoreType.DMA((N_DEV // 2 - 1,)), pltpu.SemaphoreType.DMA((N_DEV // 2 - 1,)), pltpu.SemaphoreType.DMA)

    def __init__(self, src_ref, dst_ref, send_sems, recv_sems, loc_sem):
        self.refs = (src_ref, dst_ref, send_sems, recv_sems, loc_sem)
        x, y, c = lax.axis_index("x"), lax.axis_index("y"), lax.axis_index("c")
        self.c, self.mine = c, 2 * x + y
        self.chips = [(1 - x, y), (x, 1 - y), (1 - x, 1 - y)]

    def local(self):
        src_ref, dst_ref, _, _, loc_sem = self.refs
        return pltpu.make_async_copy(src_ref.at[self.mine], dst_ref.at[self.mine], loc_sem)

    def send(self, j):
        src_ref, dst_ref, send_sems, recv_sems, _ = self.refs
        px, py = self.chips[j]
        return pltpu.make_async_remote_copy(src_ref=src_ref.at[2 * px + py], dst_ref=dst_ref.at[self.mine],
                                            send_sem=send_sems.at[j], recv_sem=recv_sems.at[j],
                                            device_id=(px, py, self.c), device_id_type=MESH)

    def arrival(self, j):
        src_ref, dst_ref, send_sems, recv_sems, _ = self.refs
        px, py = self.chips[j]
        return pltpu.make_async_remote_copy(src_ref=src_ref.at[self.mine], dst_ref=dst_ref.at[2 * px + py],
                                            send_sem=send_sems.at[j], recv_sem=recv_sems.at[j],
                                            device_id=(px, py, self.c), device_id_type=MESH)

    def start(self):
        self.local().start()
        for j in range(len(self.chips)):
            self.send(j).start()

    def finish(self):
        for j in range(len(self.chips)):
            self.arrival(j).wait_recv()
        for j in range(len(self.chips)):
            self.send(j).wait_send()
        self.local().wait()


def _exchange_call(name, cls, src):
    def body(*refs):
        ex = cls(*refs)
        ex.start()
        ex.finish()

    return pl.pallas_call(body, out_shape=cls.out_shape(src), in_specs=[pl.BlockSpec(memory_space=pl.ANY)],
                          out_specs=pl.BlockSpec(memory_space=pl.ANY), scratch_shapes=list(cls.scratch), name=name)(src)


def _gather(name, src):
    return _exchange_call(name, _Gather, src)


def _pcall(body, *, grid, in_specs, out_specs, out_shape, scratch_shapes=(), name, semantics, inputs, jobs=()):
    if not jobs:
        return pl.pallas_call(body, grid=grid, in_specs=in_specs, out_specs=out_specs, out_shape=out_shape,
                              scratch_shapes=list(scratch_shapes), name=name, compiler_params=_params(semantics))(*inputs)
    n_in, n_out, n_scr, nj = len(in_specs), len(out_specs), len(scratch_shapes), len(jobs)

    def hosted(*refs):
        ins, srcs = refs[:n_in], refs[n_in:n_in + nj]
        outs, dsts = refs[n_in + nj:n_in + nj + n_out], refs[n_in + nj + n_out:n_in + 2 * nj + n_out]
        scr, sems = refs[n_in + 2 * nj + n_out:n_in + 2 * nj + n_out + n_scr], refs[n_in + 2 * nj + n_out + n_scr:]
        ids = [pl.program_id(q) for q in range(len(grid))]
        first = functools.reduce(jnp.logical_and, [i == 0 for i in ids])
        last = functools.reduce(jnp.logical_and, [i == g - 1 for i, g in zip(ids, grid)])
        make = lambda q: jobs[q]["cls"](srcs[q], dsts[q], *sems[3 * q:3 * q + 3])

        @pl.when(first)
        def _():
            for q in range(nj):
                make(q).start()

        body(*ins, *outs, *scr)

        @pl.when(last)
        def _():
            for q in range(nj):
                make(q).finish()

    anyspec = pl.BlockSpec(memory_space=pl.ANY)
    res = pl.pallas_call(
        hosted, grid=grid, in_specs=list(in_specs) + [anyspec] * nj, out_specs=list(out_specs) + [anyspec] * nj,
        out_shape=list(out_shape) + [j["cls"].out_shape(j["src"]) for j in jobs],
        scratch_shapes=list(scratch_shapes) + [s for j in jobs for s in j["cls"].scratch], name=name,
        compiler_params=_params(("arbitrary",) * len(grid)))(*inputs, *[j["src"] for j in jobs])
    for j, out in zip(jobs, res[n_out:]):
        j["out"] = out
    return res[:n_out]


def _scatter_pairs(name, parts):
    _, r, c_ = parts.shape
    n_chip = N_DEV // 2

    def stage1(src_ref, dst_ref, send_sems, recv_sems):
        x, y, c = lax.axis_index("x"), lax.axis_index("y"), lax.axis_index("c")
        sends = []
        for q in range(n_chip):
            cp = pltpu.make_async_remote_copy(src_ref=src_ref.at[2 * q + 1 - c], dst_ref=dst_ref.at[q],
                                              send_sem=send_sems.at[q], recv_sem=recv_sems.at[q],
                                              device_id=(x, y, 1 - c), device_id_type=MESH)
            cp.start()
            sends.append(cp)
        for q in range(n_chip):
            pltpu.make_async_remote_copy(src_ref=src_ref.at[2 * q + c], dst_ref=dst_ref.at[q], send_sem=send_sems.at[q],
                                         recv_sem=recv_sems.at[q], device_id=(x, y, 1 - c), device_id_type=MESH).wait_recv()
        for cp in sends:
            cp.wait_send()

    from_sibling = pl.pallas_call(
        stage1, out_shape=S((n_chip, r, c_), parts.dtype),
        in_specs=[pl.BlockSpec(memory_space=pl.ANY)], out_specs=pl.BlockSpec(memory_space=pl.ANY),
        scratch_shapes=[pltpu.SemaphoreType.DMA((n_chip,)), pltpu.SemaphoreType.DMA((n_chip,))], name=name + "_pair")(parts)

    tm = _pick(r, (512, 256, 128, 64, 32, 16, 8)) if r % 8 == 0 else r
    core = lax.axis_index("c").astype(jnp.int32).reshape(1)

    def pair_sum(core_ref, mine_ref, sib_ref, o_ref):
        o_ref[...] = (mine_ref[0].astype(F32) + sib_ref[...].astype(F32)).astype(o_ref.dtype)

    pair = pl.pallas_call(
        pair_sum, out_shape=S((n_chip, r, c_), parts.dtype),
        grid_spec=pltpu.PrefetchScalarGridSpec(
            num_scalar_prefetch=1, grid=(n_chip, r // tm),
            in_specs=[pl.BlockSpec((1, 1, tm, c_), lambda q, i, core_ref: (q, core_ref[0], i, 0)),
                      pl.BlockSpec((1, tm, c_), lambda q, i, core_ref: (q, i, 0))],
            out_specs=pl.BlockSpec((1, tm, c_), lambda q, i, core_ref: (q, i, 0))),
        name=name + "_sum", compiler_params=_params(("parallel", "parallel")))(
            core, parts.reshape(n_chip, 2, r, c_), from_sibling)

    return pair


def _sum_parts(name, parts):
    n_parts, r, c = parts.shape
    tm = _pick(r, (512, 256, 128, 64, 32, 16, 8)) if r % 8 == 0 else r

    def body(p_ref, o_ref):
        acc = p_ref[0].astype(F32)
        for d in range(1, n_parts):
            acc = acc + p_ref[d].astype(F32)
        o_ref[...] = acc

    return pl.pallas_call(body, grid=(r // tm,), in_specs=[pl.BlockSpec((n_parts, tm, c), lambda i: (0, i, 0))],
                          out_specs=pl.BlockSpec((tm, c), lambda i: (i, 0)), out_shape=S((r, c), F32), name=name,
                          compiler_params=_params(("parallel",)))(parts)


def _adamw(name, w, g, m, v):
    r, c = w.shape
    parts = g.ndim == 3
    n_parts = g.shape[0] if parts else 1
    tm = _pick(r, (256, 128, 64, 32, 16, 8)) if r % 8 == 0 else r

    def body(w_ref, g_ref, m_ref, v_ref, go_ref, d_ref, mo_ref, vo_ref):
        if parts:
            gg = g_ref[0].astype(F32)
            for d in range(1, n_parts):
                gg = gg + g_ref[d].astype(F32)
        else:
            gg = g_ref[...]
        mm = ADAM_B1 * m_ref[...] + (1.0 - ADAM_B1) * gg
        vv = ADAM_B2 * v_ref[...] + (1.0 - ADAM_B2) * jnp.square(gg)
        m_hat = mm / (1.0 - ADAM_B1 ** ADAM_STEP)
        v_hat = vv / (1.0 - ADAM_B2 ** ADAM_STEP)
        go_ref[...] = gg
        d_ref[...] = -ADAM_LR * (m_hat / (jnp.sqrt(v_hat) + ADAM_EPS) + ADAM_WD * w_ref[...])
        mo_ref[...] = mm
        vo_ref[...] = vv

    spec = pl.BlockSpec((tm, c), lambda i: (i, 0))
    g_spec = pl.BlockSpec((n_parts, tm, c), lambda i: (0, i, 0)) if parts else spec
    return pl.pallas_call(body, grid=(r // tm,), in_specs=[spec, g_spec, spec, spec], out_specs=[spec] * 4,
                          out_shape=[S((r, c), F32)] * 4, name=name, compiler_params=_params(("parallel",)))(w, g, m, v)


def _rms(x, g):
    return x * lax.rsqrt(jnp.mean(x * x, axis=-1, keepdims=True) + RMS_EPS) * g


def _adaln(x, g, shift, scale):
    return _rms(x, g) * (1.0 + scale) + shift


def _dot(a, b, dn="nn", hi=False, prec=None):
    if hi or prec is not None:
        return lax.dot_general(a, b, _DN[dn], precision=HI if hi else prec, preferred_element_type=F32)
    return lax.dot_general(a.astype(BF16), b.astype(BF16), _DN[dn], preferred_element_type=F32)


def _sg_mix(p, ln_g, ln_b, w_s, b_st):
    d = p.shape[1] // 3
    gd = d // SG_GROUPS
    u = jax.nn.gelu(p[:, :d])
    vf = jax.nn.gelu(p[:, d:2 * d])
    z = p[:, 2 * d:]
    mean = jnp.mean(vf, axis=-1, keepdims=True)
    var = jnp.mean(jnp.square(vf - mean), axis=-1, keepdims=True)
    vn = (vf - mean) * lax.rsqrt(var + LN_EPS) * ln_g + ln_b
    row = lax.broadcasted_iota(jnp.int32, (SG_CHUNK, SG_CHUNK), 0)
    col = lax.broadcasted_iota(jnp.int32, (SG_CHUNK, SG_CHUNK), 1)
    fs = []
    for g in range(SG_GROUPS):
        w = jnp.where(row >= col, w_s[g], 0.0)
        fs.append(_dot(w, vn[:, g * gd:(g + 1) * gd]))
    sel = (lax.broadcasted_iota(jnp.int32, (SG_GROUPS, d), 1) // gd
           == lax.broadcasted_iota(jnp.int32, (SG_GROUPS, d), 0)).astype(F32)
    f = jnp.concatenate(fs, axis=1) + _dot(b_st, sel, hi=True)
    return u * f * jax.nn.silu(z)


def _rot_half(x):
    n = x.shape[1]
    lane = lax.broadcasted_iota(jnp.int32, x.shape, 1)
    return jnp.where(lane % HEAD < HEAD // 2, -pltpu.roll(x, n - HEAD // 2, 1), pltpu.roll(x, HEAD // 2, 1))


def _rope(x, cos, sin, sign):
    reps = x.shape[1] // cos.shape[1]
    return x * jnp.tile(cos, (1, reps)) + sign * _rot_half(x) * jnp.tile(sin, (1, reps))


def _attn_block(q, kp, kc, vp, vc, sink, prev_bias):
    each = lambda f, *ls: [f(*xs) for xs in zip(*ls)]
    r = sink[0].shape[0]
    scores = lambda a, b: (_dot(a, b, "nt") * (HEAD ** -0.5)).reshape(r, SWA_BLOCK, SWA_BLOCK)
    sp, sc = each(scores, q, kp), each(scores, q, kc)
    qi = lax.broadcasted_iota(jnp.int32, (r, SWA_BLOCK, SWA_BLOCK), 1)
    kj = lax.broadcasted_iota(jnp.int32, (r, SWA_BLOCK, SWA_BLOCK), 2)
    sp = each(lambda s: jnp.where(kj > qi, s, NEG) + prev_bias, sp)
    sc = each(lambda s: jnp.where(kj <= qi, s, NEG), sc)
    m = each(lambda a, b, s: jnp.maximum(jnp.maximum(jnp.max(a, axis=-1, keepdims=True), jnp.max(b, axis=-1, keepdims=True)), s),
             sp, sc, sink)
    ep, ec = each(lambda s, m_: jnp.exp(s - m_), sp, m), each(lambda s, m_: jnp.exp(s - m_), sc, m)
    denom = each(lambda a, b, s, m_: jnp.sum(a, axis=-1, keepdims=True) + jnp.sum(b, axis=-1, keepdims=True) + jnp.exp(s - m_),
                 ep, ec, sink, m)
    flat = lambda e, dn: (e / dn).reshape(r * SWA_BLOCK, SWA_BLOCK)
    pp, pc = each(flat, ep, denom), each(flat, ec, denom)
    return each(lambda a, va, b, vb: _dot(a, va) + _dot(b, vb), pp, vp, pc, vc)


def _rwkv_chunk(s0, r, k, v, logw, a, k_k, k_a, r_k, gn_g, gn_b):
    c = r[0].shape[0]
    each = lambda f, *ls: [f(*xs) for xs in zip(*ls)]
    gram = functools.partial(_dot, prec=RW_PREC)
    row = lax.broadcasted_iota(jnp.int32, (c, c), 0)
    col = lax.broadcasted_iota(jnp.int32, (c, c), 1)
    incl, strict = row >= col, row > col
    ones_l = incl.astype(F32)

    def unit(x):
        return x / jnp.maximum(jnp.sqrt(jnp.sum(x * x, axis=-1, keepdims=True)), 1e-12)

    kk = each(lambda k_, p: unit(k_ * p), k, k_k)
    km = each(lambda k_, a_, p: k_ * (1.0 + (a_ - 1.0) * p), k, a, k_a)
    b = each(lambda x, a_: x * a_, kk, a)
    first_half = lax.broadcasted_iota(jnp.int32, (c, HEAD), 0) < c // 2
    mid = each(lambda w: jnp.sum(jnp.where(first_half, w, 0.0), axis=0, keepdims=True), logw)
    cum = each(lambda w, m: _dot(ones_l, w, hi=True) - m, logw, mid)
    alpha = each(lambda x, cu, w: x * jnp.exp(cu - w), kk, cum, logw)
    beta = each(lambda x, cu: x * jnp.exp(-cu), b, cum)
    kap = each(lambda x, cu: x * jnp.exp(-cu), km, cum)
    rho = each(lambda x, cu: x * jnp.exp(cu), r, cum)
    s0 = each(lambda s, m: s * jnp.exp(m), s0, mid)
    lab = each(lambda x, y_: jnp.where(strict, gram(x, y_, "nt"), 0.0), alpha, beta)
    lak = each(lambda x, y_: jnp.where(strict, gram(x, y_, "nt"), 0.0), alpha, kap)
    xs = each(lambda al, s, l, v_: _dot(al, s, "nt") + _dot(l, v_), alpha, s0, lak, v)
    xs = each(lambda x, l: x - _dot(l, x), xs, lab)
    lp, power = lab, 2
    while power < c:
        lp = each(lambda l: _dot(l, l), lp)
        xs = each(lambda x, l: x + _dot(l, x), xs, lp)
        power *= 2
    u = each(lambda x: -x, xs)
    mrb = each(lambda x, y_: jnp.where(incl, gram(x, y_, "nt"), 0.0), rho, beta)
    mrk = each(lambda x, y_: jnp.where(incl, gram(x, y_, "nt"), 0.0), rho, kap)
    y = each(lambda rh, s, mb, u_, mk, v_: _dot(rh, s, "nt") + _dot(mb, u_) + _dot(mk, v_), rho, s0, mrb, u, mrk, v)
    s1 = each(lambda s, u_, be, v_, ka, w, m: (s + _dot(u_, be, "tn") + _dot(v_, ka, "tn"))
              * jnp.exp(jnp.sum(w, axis=0, keepdims=True) - m), s0, u, beta, v, kap, logw, mid)

    def finish(y_, g, bias, r_, km_, rk, v_):
        mean = jnp.mean(y_, axis=-1, keepdims=True)
        var = jnp.mean(jnp.square(y_ - mean), axis=-1, keepdims=True)
        y_ = (y_ - mean) * lax.rsqrt(var + GN_EPS) * g + bias
        return y_ + jnp.sum(r_ * km_ * rk, axis=-1, keepdims=True) * v_

    return each(finish, y, gn_g, gn_b, r, km, r_k, v), s1


def _norm_fwd(name, x, g, shift, scale):
    return _rows(name, lambda x_, g_, sh, sc: _adaln(x_, g_, sh, sc), [x], [g, shift, scale], [(x.shape[1], BF16)], [], 256)[0]


def _norm_bwd(name, x, dh, dx_res, g, shift, scale):
    d = x.shape[1]

    def fn(x_, dh_, dr_, g_, sh, sc):
        _, vjp = jax.vjp(_adaln, x_, g_, sh, sc)
        dx, dg, dsh, dsc = vjp(dh_)
        return dx + dr_, dg, dsh, dsc

    return _rows(name, fn, [x, dh, dx_res], [g, shift, scale], [(d, F32)], [(1, d)] * 3, 256)


def _resid_fwd(name, x, y, gate):
    return _rows(name, lambda x_, y_, g_: x_ + g_ * y_, [x, y], [gate], [(x.shape[1], F32)], [], 256)[0]


def _resid_bwd(name, dx, y, gate):
    d = dx.shape[1]
    return _rows(name, lambda dx_, y_, g_: (g_ * dx_, jnp.sum(dx_ * y_, axis=0, keepdims=True)), [dx, y], [gate],
                 [(d, BF16)], [(1, d)], 256)


def _sg_fwd(name, p, ln_g, ln_b, w_s, b_st):
    d = p.shape[1] // 3
    return _rows(name, _sg_mix, [p], [ln_g, ln_b, w_s, b_st], [(d, BF16)], [], SG_CHUNK)[0]


def _sg_bwd(name, p, dmix, ln_g, ln_b, w_s, b_st, jobs=()):
    def fn(p_, dm_, lg, lb, ws, bs):
        _, vjp = jax.vjp(_sg_mix, p_, lg, lb, ws, bs)
        return vjp(dm_)

    return _rows(name, fn, [p, dmix], [ln_g, ln_b, w_s, b_st], [(p.shape[1], BF16)],
                 [ln_g.shape, ln_b.shape, w_s.shape, b_st.shape], SG_CHUNK, jobs=jobs)


def _rope_tables(pos, inv_freq):
    ang = pos * inv_freq
    return jnp.cos(ang), jnp.sin(ang)


def _swa_pre(name, p, pos, inv_freq, d):
    kvw = SWA_KV * HEAD

    def fn(p_, pos_, fr):
        cos, sin = _rope_tables(pos_, fr)
        return (_rope(p_[:, :d], cos, sin, 1.0), _rope(p_[:, d:d + kvw], cos, sin, 1.0), p_[:, d + kvw:d + 2 * kvw])

    return _rows(name, fn, [p, pos], [inv_freq], [(d, BF16), (kvw, BF16), (kvw, BF16)], [], 256)


def _q_groups(ref, kv, rep):
    heads = _head_cols(ref, kv * rep)
    return [jnp.concatenate(heads[g * rep:(g + 1) * rep], axis=0) for g in range(kv)]


def _q_ungroup(groups, rep):
    return jnp.concatenate([g[h * SWA_BLOCK:(h + 1) * SWA_BLOCK] for g in groups for h in range(rep)], axis=1)


def _swa_attn_fwd(name, q, k, v, sinks, jobs=()):
    t, d = q.shape
    kv, rep = sinks.shape[0], sinks.shape[1]
    nb = t // SWA_BLOCK

    def body(q_ref, kp_ref, kc_ref, vp_ref, vc_ref, s_ref, o_ref):
        prev_bias = jnp.where(pl.program_id(0) > 0, 0.0, NEG).astype(F32)
        o = _attn_block(_q_groups(q_ref, kv, rep), *[_head_cols(ref, kv) for ref in (kp_ref, kc_ref, vp_ref, vc_ref)],
                        [s_ref[g] for g in range(kv)], prev_bias)
        o_ref[...] = _q_ungroup(o, rep)

    qs = pl.BlockSpec((SWA_BLOCK, d), lambda n: (n, 0))
    cur = pl.BlockSpec((SWA_BLOCK, kv * HEAD), lambda n: (n, 0))
    prev = pl.BlockSpec((SWA_BLOCK, kv * HEAD), lambda n: (jnp.maximum(n - 1, 0), 0))
    ss = pl.BlockSpec(sinks.shape, lambda n: (0, 0, 0, 0))
    return _pcall(body, grid=(nb,), in_specs=[qs, prev, cur, prev, cur, ss], out_specs=[qs], out_shape=[S((t, d), F32)],
                  name=name, semantics=("parallel",), inputs=(q, k, k, v, v, sinks), jobs=jobs)[0]


def _swa_attn_bwd(name, q, k, v, sinks, do, jobs=()):
    t, d = q.shape
    kv, rep = sinks.shape[0], sinks.shape[1]
    nb = t // SWA_BLOCK

    def body(q_ref, kp_ref, kc_ref, vp_ref, vc_ref, s_ref, do_ref, dq_ref, dkc_ref, dkp_ref, dvc_ref, dvp_ref, ds_ref):
        n = pl.program_id(0)
        prev_bias = jnp.where(n > 0, 0.0, NEG).astype(F32)
        fn = functools.partial(_attn_block, prev_bias=prev_bias)
        args = [_q_groups(q_ref, kv, rep)] + [_head_cols(ref, kv) for ref in (kp_ref, kc_ref, vp_ref, vc_ref)]
        _, vjp = jax.vjp(fn, *args, [s_ref[g] for g in range(kv)])
        dq, dkp, dkc, dvp, dvc, ds = vjp(_q_groups(do_ref, kv, rep))
        dq_ref[...] = _q_ungroup(dq, rep)
        for ref, val in ((dkc_ref, dkc), (dkp_ref, dkp), (dvc_ref, dvc), (dvp_ref, dvp)):
            ref[...] = jnp.concatenate(val, axis=1)

        @pl.when(n == 0)
        def _():
            ds_ref[...] = jnp.zeros_like(ds_ref)

        for g in range(kv):
            ds_ref[g] += ds[g]

    qs = pl.BlockSpec((SWA_BLOCK, d), lambda n: (n, 0))
    cur = pl.BlockSpec((SWA_BLOCK, kv * HEAD), lambda n: (n, 0))
    prev = pl.BlockSpec((SWA_BLOCK, kv * HEAD), lambda n: (jnp.maximum(n - 1, 0), 0))
    ss = pl.BlockSpec(sinks.shape, lambda n: (0, 0, 0, 0))
    return _pcall(body, grid=(nb,), in_specs=[qs, prev, cur, prev, cur, ss, qs], out_specs=[qs, cur, cur, cur, cur, ss],
                  out_shape=[S((t, d), F32)] + [S((t, kv * HEAD), F32)] * 4 + [S(sinks.shape, F32)], name=name,
                  semantics=("arbitrary",), inputs=(q, k, k, v, v, sinks, do), jobs=jobs)


def _gate_fwd(name, o, z_src, z_off, d):
    return _rows(name, lambda o_, p_: o_ * jax.nn.silu(p_[:, z_off:z_off + d]), [o, z_src], [], [(d, BF16)], [], 256)[0]


def _gate_bwd(name, o, z_src, z_off, d, dmix):
    def fn(o_, p_, dm_):
        _, vjp = jax.vjp(lambda oo, zz: oo * jax.nn.silu(zz), o_, p_[:, z_off:z_off + d])
        return vjp(dm_)

    return _rows(name, fn, [o, z_src, dmix], [], [(d, F32), (d, F32)], [], 256)


def _swa_post_bwd(name, dq, dkc, dkp_up, dvc, dvp_up, dz, pos, inv_freq):
    def fn(dq_, dkc_, dkp_, dvc_, dvp_, dz_, pos_, fr):
        cos, sin = _rope_tables(pos_, fr)
        return jnp.concatenate([_rope(dq_, cos, sin, -1.0), _rope(dkc_ + dkp_, cos, sin, -1.0), dvc_ + dvp_, dz_], axis=1)

    n = dq.shape[1] + dkc.shape[1] + dvc.shape[1] + dz.shape[1]
    return _rows(name, fn, [dq, dkc, dkp_up, dvc, dvp_up, dz, pos], [inv_freq], [(n, BF16)], [], 256)[0]


HALO = 8


def _row_before(x, halo_ref, i):
    first = jnp.where(i > 0, halo_ref[pl.ds(HALO - 1, 1), :], 0.0)
    row = lax.broadcasted_iota(jnp.int32, x.shape, 0)
    return jnp.where(row == 0, first, pltpu.roll(x, 1, 0))


def _row_after(x, halo, i, n_tiles):
    last = jnp.where(i < n_tiles - 1, halo, 0.0)
    row = lax.broadcasted_iota(jnp.int32, x.shape, 0)
    return jnp.where(row == x.shape[0] - 1, last, pltpu.roll(x, x.shape[0] - 1, 0))


def _lerp_fwd(name, p, mu, widths):
    t, n = p.shape
    tm = 128

    def body(p_ref, halo_ref, mu_ref, *o_refs):
        x = p_ref[...]
        pm = x + (_row_before(x, halo_ref, pl.program_id(0)) - x) * mu_ref[...]
        o = 0
        for ref, w in zip(o_refs, widths):
            ref[...] = pm[:, o:o + w]
            o += w

    return pl.pallas_call(
        body, grid=(t // tm,),
        in_specs=[pl.BlockSpec((tm, n), lambda i: (i, 0)),
                  pl.BlockSpec((HALO, n), lambda i: (jnp.maximum(i * (tm // HALO) - 1, 0), 0)),
                  pl.BlockSpec((1, n), lambda i: (0, 0))],
        out_specs=[pl.BlockSpec((tm, w), lambda i: (i, 0)) for w in widths],
        out_shape=[S((t, w), F32) for w in widths], name=name, compiler_params=_params(("parallel",)))(p, p, mu)


def _lerp_bwd(name, dpm_parts, p, mu):
    t, n = p.shape
    k = len(dpm_parts)
    tm = 64
    n_tiles = t // tm

    def body(*refs):
        d_refs, dh_refs = refs[:k], refs[k:2 * k]
        p_ref, ph_ref, mu_ref, dp_ref, dmu_ref = refs[2 * k:]
        i = pl.program_id(0)
        cat = lambda vals: jnp.concatenate(vals, axis=1) if k > 1 else vals[0]
        dpm = cat([r[...] for r in d_refs])
        dnext = cat([r[pl.ds(0, 1), :] for r in dh_refs])
        x, mu_ = p_ref[...], mu_ref[...]
        dp_ref[...] = (dpm * (1.0 - mu_) + _row_after(dpm, dnext, i, n_tiles) * mu_).astype(dp_ref.dtype)
        dmu = jnp.sum(dpm * (_row_before(x, ph_ref, i) - x), axis=0, keepdims=True)

        @pl.when(i == 0)
        def _():
            dmu_ref[...] = dmu

        @pl.when(i > 0)
        def _():
            dmu_ref[...] += dmu

    per = tm // HALO
    d_specs = [pl.BlockSpec((tm, a.shape[1]), lambda i: (i, 0)) for a in dpm_parts]
    dh_specs = [pl.BlockSpec((HALO, a.shape[1]), lambda i: (jnp.minimum((i + 1) * per, t // HALO - 1), 0)) for a in dpm_parts]
    return pl.pallas_call(
        body, grid=(n_tiles,),
        in_specs=d_specs + dh_specs + [pl.BlockSpec((tm, n), lambda i: (i, 0)),
                                       pl.BlockSpec((HALO, n), lambda i: (jnp.maximum(i * per - 1, 0), 0)),
                                       pl.BlockSpec((1, n), lambda i: (0, 0))],
        out_specs=[pl.BlockSpec((tm, n), lambda i: (i, 0)), pl.BlockSpec((1, n), lambda i: (0, 0))],
        out_shape=[S((t, n), BF16), S((1, n), F32)], name=name,
        compiler_params=_params(("arbitrary",)))(*dpm_parts, *dpm_parts, p, p, mu)


def _lora_act(pl_, w0, w_lora, a0, a_lora):
    logw = -DECAY_SCALE * jax.nn.sigmoid(w0 + _dot(jnp.tanh(pl_), w_lora))
    a = jax.nn.sigmoid(a0 + _dot(pl_, a_lora))
    return logw, a


def _lora_fwd(name, pl_, w0, w_lora, a0, a_lora):
    d = w0.shape[1]
    return _rows(name, _lora_act, [pl_], [w0, w_lora, a0, a_lora], [(d, F32), (d, F32)], [], 256)


def _lora_bwd(name, pl_, dlogw, da, w0, w_lora, a0, a_lora):
    def fn(p_, dl_, da_, w0_, wl_, a0_, al_):
        _, vjp = jax.vjp(_lora_act, p_, w0_, wl_, a0_, al_)
        return vjp((dl_, da_))

    return _rows(name, fn, [pl_, dlogw, da], [w0, w_lora, a0, a_lora], [(pl_.shape[1], F32)],
                 [w0.shape, w_lora.shape, a0.shape, a_lora.shape], 256)


def _head_cols(ref, hb):
    x = ref[...].astype(F32)
    xo = pltpu.roll(x, x.shape[1] - HEAD, 1)
    return [(x if j % 2 == 0 else xo)[:, 2 * HEAD * (j // 2):2 * HEAD * (j // 2) + HEAD] for j in range(hb)]


def _rwkv_scan_fwd(name, r, k, v, logw, a, hp, jobs=()):
    t, d = r.shape
    h, nc, hb = d // HEAD, t // RW_CHUNK, RW_HEADS

    def body(r_ref, k_ref, v_ref, w_ref, a_ref, kk_ref, ka_ref, rk_ref, gg_ref, gb_ref, y_ref, st_ref, s_scr):
        @pl.when(pl.program_id(1) == 0)
        def _():
            s_scr[...] = jnp.zeros_like(s_scr)

        s0 = [s_scr[j] for j in range(hb)]
        for j in range(hb):
            st_ref[j, 0] = s0[j]
        y, s1 = _rwkv_chunk(s0, *[_head_cols(ref, hb) for ref in (r_ref, k_ref, v_ref, w_ref, a_ref, kk_ref, ka_ref, rk_ref,
                                                                 gg_ref, gb_ref)])
        y_ref[...] = jnp.concatenate(y, axis=1)
        for j in range(hb):
            s_scr[j] = s1[j]

    seq = pl.BlockSpec((RW_CHUNK, hb * HEAD), lambda i, n: (n, i))
    par = pl.BlockSpec((1, hb * HEAD), lambda i, n: (0, i))
    st = pl.BlockSpec((hb, 1, HEAD, HEAD), lambda i, n: (i, n, 0, 0))
    return _pcall(body, grid=(h // hb, nc), in_specs=[seq] * 5 + [par] * 5, out_specs=[seq, st],
                  out_shape=[S((t, d), F32), S((h, nc, HEAD, HEAD), F32)], scratch_shapes=[pltpu.VMEM((hb, HEAD, HEAD), F32)],
                  name=name, semantics=("parallel", "arbitrary"), inputs=(r, k, v, logw, a, *hp), jobs=jobs)


def _rwkv_scan_bwd(name, r, k, v, logw, a, hp, states, dy, jobs=()):
    t, d = r.shape
    h, nc, hb = d // HEAD, t // RW_CHUNK, RW_HEADS

    def body(r_ref, k_ref, v_ref, w_ref, a_ref, kk_ref, ka_ref, rk_ref, gg_ref, gb_ref, st_ref, dy_ref,
             dr_ref, dk_ref, dv_ref, dw_ref, da_ref, dkk_ref, dka_ref, drk_ref, dgg_ref, dgb_ref, ds_scr):
        n = pl.program_id(1)

        @pl.when(n == 0)
        def _():
            ds_scr[...] = jnp.zeros_like(ds_scr)
            for ref in (dkk_ref, dka_ref, drk_ref, dgg_ref, dgb_ref):
                ref[...] = jnp.zeros_like(ref)

        ins = [[st_ref[j, 0] for j in range(hb)]] + [_head_cols(ref, hb) for ref in (r_ref, k_ref, v_ref, w_ref, a_ref, kk_ref,
                                                                                  ka_ref, rk_ref, gg_ref, gb_ref)]
        _, vjp = jax.vjp(_rwkv_chunk, *ins)
        ds0, *dseq, dkk, dka, drk, dgg, dgb = vjp((_head_cols(dy_ref, hb), [ds_scr[j] for j in range(hb)]))
        for j in range(hb):
            ds_scr[j] = ds0[j]
        for ref, val in zip((dr_ref, dk_ref, dv_ref, dw_ref, da_ref), dseq):
            ref[...] = jnp.concatenate(val, axis=1)
        for ref, val in ((dkk_ref, dkk), (dka_ref, dka), (drk_ref, drk), (dgg_ref, dgg), (dgb_ref, dgb)):
            ref[...] += jnp.concatenate(val, axis=1)

    seq = pl.BlockSpec((RW_CHUNK, hb * HEAD), lambda i, n: (nc - 1 - n, i))
    par = pl.BlockSpec((1, hb * HEAD), lambda i, n: (0, i))
    st = pl.BlockSpec((hb, 1, HEAD, HEAD), lambda i, n: (i, nc - 1 - n, 0, 0))
    return _pcall(body, grid=(h // hb, nc), in_specs=[seq] * 5 + [par] * 5 + [st, seq], out_specs=[seq] * 5 + [par] * 5,
                  out_shape=[S((t, d), F32)] * 5 + [S((1, d), F32)] * 5, scratch_shapes=[pltpu.VMEM((hb, HEAD, HEAD), F32)],
                  name=name, semantics=("parallel", "arbitrary"), inputs=(r, k, v, logw, a, *hp, states, dy), jobs=jobs)


def _loss_head(name, x, target, g):
    d = x.shape[1]

    def fn(x_, t_, g_):
        def f(xx, gg):
            err = _rms(xx, gg) - t_
            return 0.5 * jnp.sum(jnp.mean(err * err, axis=-1, keepdims=True), axis=0, keepdims=True)

        l, vjp = jax.vjp(f, x_, g_)
        dx, dg = vjp(jnp.ones((1, 1), F32))
        return dx, dg, jnp.broadcast_to(l, (1, 128))

    return _rows(name, fn, [x, target], [g], [(d, F32)], [(1, d), (1, 128)], 256)


def _mod_fwd(name, cond_all, mod_w, mod_b_cols):
    l, d, n = mod_w.shape

    def body(c_ref, w_ref, b_ref, o_ref):
        o_ref[0] = _dot(jax.nn.silu(c_ref[...]), w_ref[0], hi=True) + b_ref[0]

    return pl.pallas_call(body, grid=(l,), in_specs=[pl.BlockSpec((N_DEV, d), lambda i: (0, 0)),
                                                      pl.BlockSpec((1, d, n), lambda i: (i, 0, 0)),
                                                      pl.BlockSpec((1, 1, n), lambda i: (i, 0, 0))],
                          out_specs=pl.BlockSpec((1, N_DEV, n), lambda i: (i, 0, 0)), out_shape=S((l, N_DEV, n), F32),
                          name=name, compiler_params=_params(("parallel",)))(cond_all, mod_w, mod_b_cols)


def _mod_bwd(name, cond_all, dmod_cols, dmod_all):
    l, _, n = dmod_cols.shape
    d = cond_all.shape[1]
    nb = dmod_all.shape[2]

    def body(c_ref, dc_ref, da_ref, gw_ref, gb_ref):
        gw_ref[0] = _dot(jax.nn.silu(c_ref[...]), dc_ref[0], "tn", hi=True)
        acc = da_ref[0, 0:1, :]
        for bi in range(1, N_DEV):
            acc = acc + da_ref[0, bi:bi + 1, :]
        gb_ref[0] = acc

    return pl.pallas_call(body, grid=(l,), in_specs=[pl.BlockSpec((N_DEV, d), lambda i: (0, 0)),
                                                      pl.BlockSpec((1, N_DEV, n), lambda i: (i, 0, 0)),
                                                      pl.BlockSpec((1, N_DEV, nb), lambda i: (i, 0, 0))],
                          out_specs=[pl.BlockSpec((1, d, n), lambda i: (i, 0, 0)), pl.BlockSpec((1, 1, nb), lambda i: (i, 0, 0))],
                          out_shape=[S((l, d, n), F32), S((l, 1, nb), F32)], name=name,
                          compiler_params=_params(("parallel",)))(cond_all, dmod_cols, dmod_all)


def _shift_up(a, n=1):
    return jnp.concatenate([a[n:], jnp.zeros_like(a[:n])], axis=0)


def _cols_full(g):
    return g.transpose(1, 0, 2).reshape(g.shape[1], -1)


def _cols_parts(full):
    r, n = full.shape
    return full.reshape(r, N_DEV, n // N_DEV).transpose(1, 0, 2)


def _pack(arrs, mult=1024):
    flat = jnp.concatenate([a.reshape(-1) for a in arrs])
    pad = (-flat.shape[0]) % mult
    return jnp.pad(flat, (0, pad)).reshape(-1, 128)


def _unpack(flat, shapes):
    out, o = [], 0
    for s in shapes:
        n = math.prod(s)
        out.append(flat[o:o + n].reshape(s))
        o += n
    return out


def _local_step(x, pos, target, mods, norm_g, final_norm_g, layer_weights, hooks=None, on_grads=None):
    t, d = x.shape
    hooks = hooks or {}
    jobs = lambda nm: hooks.get(nm, ())
    notify = on_grads or (lambda *a: None)
    kinds = [i % 3 for i in range(DEPTH)]
    inv_freq = (ROPE_THETA ** (-jnp.arange(HEAD // 2, dtype=F32) / (HEAD // 2)))
    inv_freq = jnp.tile(inv_freq, 128 // (HEAD // 2)).reshape(1, 128)
    saved = []
    for i, kind in enumerate(kinds):
        lw = layer_weights(i)
        shift, scale, gate = (mods[i, q * d:(q + 1) * d].reshape(1, d) for q in range(3))
        g = norm_g[i].reshape(1, d)
        h = _norm_fwd(f"norm_fwd{i}", x, g, shift, scale)
        sv = dict(x=x, h=h, g=g, shift=shift, scale=scale, gate=gate, lw=lw)
        if kind == 0:
            p = _mm(f"sg_in{i}", h, lw["w_in"], "nn", F32, jobs=jobs(f"sg_in{i}"))
            mix = _sg_fwd(f"sg_mix{i}", p, lw["ln_g"], lw["ln_b"], lw["w_s"], lw["b_st"])
            sv.update(p=p)
        elif kind == 1:
            p = _mm(f"swa_in{i}", h, lw["w_in"], "nn", F32, jobs=jobs(f"swa_in{i}"))
            q, k, v = _swa_pre(f"swa_pre{i}", p, pos, inv_freq, d)
            o = _swa_attn_fwd(f"swa_attn{i}", q, k, v, lw["sinks"], jobs=jobs(f"swa_attn{i}"))
            mix = _gate_fwd(f"swa_gate{i}", o, p, d + 2 * SWA_KV * HEAD, d)
            sv.update(p=p, qkv=(q, k, v), o=o)
        else:
            pm = _mm(f"rw_in{i}", h, lw["w_main"], "nn", F32)
            plo = _mm(f"rw_inl{i}", h, lw["w_lorain"], "nn", F32)
            r, k, v, z = _lerp_fwd(f"rw_lerp{i}", pm, lw["mu_main"], [d] * 4)
            (pll,) = _lerp_fwd(f"rw_lerpl{i}", plo, lw["mu_lora"], [LORA_PAD])
            logw, a = _lora_fwd(f"rw_lora{i}", pll, lw["w0"], lw["w_lora"], lw["a0"], lw["a_lora"])
            seqs = (r, k, v, logw, a)
            o, states = _rwkv_scan_fwd(f"rw_scan{i}", *seqs, lw["hp"], jobs=jobs(f"rw_scan{i}"))
            mix = _gate_fwd(f"rw_gate{i}", o, z, 0, d)
            sv.update(pm=pm, plo=plo, pll=pll, z=z, seqs=seqs, states=states, o=o)
        y = _mm(f"out{i}", mix, lw["w_out"], "nn", F32, jobs=jobs(f"out{i}"))
        sv.update(mix=mix, y=y)
        saved.append(sv)
        x = _resid_fwd(f"resid{i}", x, y, gate)

    dx, d_final_g, loss = _loss_head("loss_head", x, target, final_norm_g.reshape(1, d))

    grads = dict(norm_g=[None] * DEPTH, sg_w_in=[None] * 2, sg_w_out=[None] * 2, sg_ln_g=[None] * 2, sg_ln_b=[None] * 2,
                 sg_w_s=[None] * 2, sg_b_st=[None] * 2, final_norm_g=d_final_g)
    dmods = [None] * DEPTH
    for i in reversed(range(DEPTH)):
        kind, j, sv = kinds[i], i // 3, saved[i]
        lw = sv["lw"]
        dy, dgate = _resid_bwd(f"resid_bwd{i}", dx, sv["y"], sv["gate"])
        d_w_out = _mm(f"out_dw{i}", sv["mix"], dy, "tn", F32)
        if kind == 0:
            grads["sg_w_out"][j] = d_w_out
        else:
            grads[("swa_w_out", "rw_w_out")[kind - 1]] = d_w_out
        notify(i, "out", grads)
        dmix = _mm(f"out_dx{i}", dy, lw["w_out"], "nt", F32, jobs=jobs(f"out_dx{i}"))
        if kind == 0:
            dp, dlg, dlb, dws, dbs = _sg_bwd(f"sg_mix_bwd{i}", sv["p"], dmix, lw["ln_g"], lw["ln_b"], lw["w_s"], lw["b_st"],
                                            jobs=jobs(f"sg_mix_bwd{i}"))
            grads["sg_ln_g"][j], grads["sg_ln_b"][j], grads["sg_w_s"][j], grads["sg_b_st"][j] = dlg, dlb, dws, dbs
            grads["sg_w_in"][j] = _mm(f"sg_in_dw{i}", sv["h"], dp, "tn", F32, jobs=jobs(f"sg_in_dw{i}"))
            notify(i, "in", grads)
            dh = _mm(f"sg_in_dx{i}", dp, lw["w_in"], "nt", F32, jobs=jobs(f"sg_in_dx{i}"))
        elif kind == 1:
            z_off = d + 2 * SWA_KV * HEAD
            do, dz = _gate_bwd(f"swa_gate_bwd{i}", sv["o"], sv["p"], z_off, d, dmix)
            dq, dkc, dkp, dvc, dvp, dsinks = _swa_attn_bwd(f"swa_attn_bwd{i}", *sv["qkv"], lw["sinks"], do,
                                                           jobs=jobs(f"swa_attn_bwd{i}"))
            dkp, dvp = _shift_up(dkp, SWA_BLOCK), _shift_up(dvp, SWA_BLOCK)
            dp = _swa_post_bwd(f"swa_post_bwd{i}", dq, dkc, dkp, dvc, dvp, dz, pos, inv_freq)
            grads.update(swa_sinks=dsinks, swa_w_out=d_w_out)
            grads["swa_w_in"] = _mm(f"swa_in_dw{i}", sv["h"], dp, "tn", F32)
            dh = _mm(f"swa_in_dx{i}", dp, lw["w_in"], "nt", F32)
        else:
            do, dz = _gate_bwd(f"rw_gate_bwd{i}", sv["o"], sv["z"], 0, d, dmix)
            res = _rwkv_scan_bwd(f"rw_scan_bwd{i}", *sv["seqs"], lw["hp"], sv["states"], do, jobs=jobs(f"rw_scan_bwd{i}"))
            dr, dk, dv, dlogw, da = res[:5]
            dpll, dw0, dwl, da0, dal = _lora_bwd(f"rw_lora_bwd{i}", sv["pll"], dlogw, da, lw["w0"], lw["w_lora"],
                                                  lw["a0"], lw["a_lora"])
            dpm, dmu_main = _lerp_bwd(f"rw_lerp_bwd{i}", [dr, dk, dv, dz], sv["pm"], lw["mu_main"])
            dpl, dmu_lora = _lerp_bwd(f"rw_lerpl_bwd{i}", [dpll], sv["plo"], lw["mu_lora"])
            grads.update(rw_w_out=d_w_out, rw_hp=res[5:], rw_w0=dw0, rw_w_lora=dwl, rw_a0=da0, rw_a_lora=dal,
                         rw_mu_main=dmu_main, rw_mu_lora=dmu_lora)
            grads["rw_w_main"] = _mm(f"rw_in_dw{i}", sv["h"], dpm, "tn", F32)
            grads["rw_w_lorain"] = _mm(f"rw_inl_dw{i}", sv["h"], dpl, "tn", F32)
            dh = _mm(f"rw_inl_dx{i}", dpl, lw["w_lorain"], "nt", F32)
            dh = _mm(f"rw_in_dx{i}", dpm, lw["w_main"], "nt", F32, add=dh)
        if kind != 0:
            notify(i, "in", grads)
        dx, dg, dshift, dscale = _norm_bwd(f"norm_bwd{i}", sv["x"], dh, dx, sv["g"], sv["shift"], sv["scale"])
        grads["norm_g"][i] = dg
        dmods[i] = jnp.concatenate([dshift, dscale, dgate], axis=1)
    return loss, dx, jnp.concatenate(dmods, axis=0), grads


def kernel(x, c, positions, norm_g, mod_w, mod_b, final_norm_g, sg_w_in, sg_w_out, sg_ln_g, sg_ln_b, sg_w_spatial, sg_b_spatial, swa_w_in, swa_w_out, swa_sinks, rwkv_w_in, rwkv_w_out, rwkv_mu, rwkv_w0, rwkv_w_lora, rwkv_a0, rwkv_a_lora, rwkv_k_k, rwkv_k_a, rwkv_r_k, rwkv_gn_g, rwkv_gn_b, loss_target, m_norm_g, m_mod_w, m_mod_b, m_final_norm_g, m_sg_w_in, m_sg_w_out, m_sg_ln_g, m_sg_ln_b, m_sg_w_spatial, m_sg_b_spatial, m_swa_w_in, m_swa_w_out, m_swa_sinks, m_rwkv_w_in, m_rwkv_w_out, m_rwkv_mu, m_rwkv_w0, m_rwkv_w_lora, m_rwkv_a0, m_rwkv_a_lora, m_rwkv_k_k, m_rwkv_k_a, m_rwkv_r_k, m_rwkv_gn_g, m_rwkv_gn_b, v_norm_g, v_mod_w, v_mod_b, v_final_norm_g, v_sg_w_in, v_sg_w_out, v_sg_ln_g, v_sg_ln_b, v_sg_w_spatial, v_sg_b_spatial, v_swa_w_in, v_swa_w_out, v_swa_sinks, v_rwkv_w_in, v_rwkv_w_out, v_rwkv_mu, v_rwkv_w0, v_rwkv_w_lora, v_rwkv_a0, v_rwkv_a_lora, v_rwkv_k_k, v_rwkv_k_a, v_rwkv_r_k, v_rwkv_gn_g, v_rwkv_gn_b):
    weights = dict(norm_g=norm_g, mod_w=mod_w, mod_b=mod_b, final_norm_g=final_norm_g, sg_w_in=sg_w_in, sg_w_out=sg_w_out,
                   sg_ln_g=sg_ln_g, sg_ln_b=sg_ln_b, sg_w_spatial=sg_w_spatial, sg_b_spatial=sg_b_spatial, swa_w_in=swa_w_in,
                   swa_w_out=swa_w_out, swa_sinks=swa_sinks, rwkv_w_in=rwkv_w_in, rwkv_w_out=rwkv_w_out, rwkv_mu=rwkv_mu,
                   rwkv_w0=rwkv_w0, rwkv_w_lora=rwkv_w_lora, rwkv_a0=rwkv_a0, rwkv_a_lora=rwkv_a_lora, rwkv_k_k=rwkv_k_k,
                   rwkv_k_a=rwkv_k_a, rwkv_r_k=rwkv_r_k, rwkv_gn_g=rwkv_gn_g, rwkv_gn_b=rwkv_gn_b)
    mom_m = dict(norm_g=m_norm_g, mod_w=m_mod_w, mod_b=m_mod_b, final_norm_g=m_final_norm_g, sg_w_in=m_sg_w_in,
                 sg_w_out=m_sg_w_out, sg_ln_g=m_sg_ln_g, sg_ln_b=m_sg_ln_b, sg_w_spatial=m_sg_w_spatial,
                 sg_b_spatial=m_sg_b_spatial, swa_w_in=m_swa_w_in, swa_w_out=m_swa_w_out, swa_sinks=m_swa_sinks,
                 rwkv_w_in=m_rwkv_w_in, rwkv_w_out=m_rwkv_w_out, rwkv_mu=m_rwkv_mu, rwkv_w0=m_rwkv_w0,
                 rwkv_w_lora=m_rwkv_w_lora, rwkv_a0=m_rwkv_a0, rwkv_a_lora=m_rwkv_a_lora, rwkv_k_k=m_rwkv_k_k,
                 rwkv_k_a=m_rwkv_k_a, rwkv_r_k=m_rwkv_r_k, rwkv_gn_g=m_rwkv_gn_g, rwkv_gn_b=m_rwkv_gn_b)
    mom_v = dict(norm_g=v_norm_g, mod_w=v_mod_w, mod_b=v_mod_b, final_norm_g=v_final_norm_g, sg_w_in=v_sg_w_in,
                 sg_w_out=v_sg_w_out, sg_ln_g=v_sg_ln_g, sg_ln_b=v_sg_ln_b, sg_w_spatial=v_sg_w_spatial,
                 sg_b_spatial=v_sg_b_spatial, swa_w_in=v_swa_w_in, swa_w_out=v_swa_w_out, swa_sinks=v_swa_sinks,
                 rwkv_w_in=v_rwkv_w_in, rwkv_w_out=v_rwkv_w_out, rwkv_mu=v_rwkv_mu, rwkv_w0=v_rwkv_w0,
                 rwkv_w_lora=v_rwkv_w_lora, rwkv_a0=v_rwkv_a0, rwkv_a_lora=v_rwkv_a_lora, rwkv_k_k=v_rwkv_k_k,
                 rwkv_k_a=v_rwkv_k_a, rwkv_r_k=v_rwkv_r_k, rwkv_gn_g=v_rwkv_gn_g, rwkv_gn_b=v_rwkv_gn_b)
    names = list(weights)
    t, d = x.shape[1], x.shape[2]
    me = 4 * lax.axis_index("x") + 2 * lax.axis_index("y") + lax.axis_index("c")
    n_mod = mod_w.shape[2]
    n_rw = rwkv_w_in.shape[2]

    small_names = ["sg_ln_g", "sg_ln_b", "rwkv_mu", "rwkv_w0", "rwkv_a0", "rwkv_k_k", "rwkv_k_a", "rwkv_gn_g", "rwkv_gn_b",
                   "rwkv_w_lora", "rwkv_a_lora"]
    small_shapes = [weights[n].shape for n in small_names]
    pk = _pack([c] + [weights[n] for n in small_names])
    gathered = _gather("gather_small", pk).reshape(N_DEV, -1)
    c_all = gathered[:, :d]
    per_dev = [_unpack(gathered[dv, d:], small_shapes) for dv in range(N_DEV)]
    full_small = {}
    for q, n in enumerate(small_names):
        full_small[n] = jnp.concatenate([per_dev[dv][q] for dv in range(N_DEV)], axis=-1)

    mod_b_cols = lax.dynamic_slice_in_dim(mod_b, me * n_mod, n_mod, axis=1).reshape(DEPTH, 1, n_mod)
    mod_part = _mod_fwd("mod_fwd", c_all, mod_w, mod_b_cols)
    mod_g = _gather("gather_mod", mod_part.reshape(DEPTH * N_DEV, n_mod))
    mod_g = mod_g.reshape(N_DEV, DEPTH, N_DEV, n_mod)
    mods = lax.dynamic_index_in_dim(mod_g, me, axis=2, keepdims=False)
    mods = mods.transpose(1, 0, 2).reshape(DEPTH, N_DEV * n_mod)

    job = lambda cls, src: dict(cls=cls, src=src)
    gj = dict(swa_in=job(_Gather, swa_w_in[0].astype(BF16)), swa_out=job(_Gather, swa_w_out[0].astype(BF16)),
              rw_in=job(_Gather, rwkv_w_in[0].astype(BF16)), rw_out=job(_Gather, rwkv_w_out[0].astype(BF16)),
              sg_in1=job(_Gather, sg_w_in[1].astype(BF16)), sg_out1=job(_Gather, sg_w_out[1].astype(BF16)))
    hooks = {"sg_in0": [gj["swa_in"]], "out0": [gj["swa_out"]], "swa_in1": [gj["rw_out"]], "swa_attn1": [gj["rw_in"]],
             "rw_scan2": [gj["sg_in1"], gj["sg_out1"]]}
    g_sg_in0 = _gather("gather_sg_in0", sg_w_in[0].astype(BF16))
    g_sg_out0 = _gather("gather_sg_out0", sg_w_out[0].astype(BF16))
    lora_rows = lambda w, off: jnp.zeros((LORA_PAD, d), F32).at[off:off + LORA].set(w)
    mu = full_small["rwkv_mu"].reshape(1, -1)
    heads = lambda a: a.reshape(1, -1)

    def layer_weights(i):
        if i % 3 == 0:
            j = i // 3
            g_in, g_out = (g_sg_in0, g_sg_out0) if j == 0 else (gj["sg_in1"]["out"], gj["sg_out1"]["out"])
            return dict(w_in=_cols_full(g_in), w_out=g_out.reshape(d, d), ln_g=full_small["sg_ln_g"][j].reshape(1, d),
                        ln_b=full_small["sg_ln_b"][j].reshape(1, d), w_s=sg_w_spatial[j], b_st=sg_b_spatial[j].T)
        if i % 3 == 1:
            return dict(w_in=_cols_full(gj["swa_in"]["out"]), w_out=gj["swa_out"]["out"].reshape(d, d),
                        sinks=swa_sinks.reshape(SWA_KV, SWA_REP, 1, 1))
        rw_in_full = _cols_full(gj["rw_in"]["out"])
        return dict(w_main=rw_in_full[:, :4 * d], w_lorain=jnp.pad(rw_in_full[:, 4 * d:], ((0, 0), (0, LORA_PAD - 2 * LORA))),
                    w_out=gj["rw_out"]["out"].reshape(d, d), mu_main=mu[:, :4 * d],
                    mu_lora=jnp.pad(mu[:, 4 * d:], ((0, 0), (0, LORA_PAD - 2 * LORA))),
                    w0=full_small["rwkv_w0"], a0=full_small["rwkv_a0"],
                    w_lora=lora_rows(full_small["rwkv_w_lora"][0], 0), a_lora=lora_rows(full_small["rwkv_a_lora"][0], LORA),
                    hp=[heads(full_small["rwkv_k_k"]), heads(full_small["rwkv_k_a"]), heads(rwkv_r_k),
                        heads(full_small["rwkv_gn_g"]), heads(full_small["rwkv_gn_b"])])

    sj = {}

    def on_grads(i, which, g):
        def stage(nm, parts, host):
            sj[nm] = job(_Chips, _scatter_pairs("scatter_" + nm, parts.astype(BF16)))
            hooks.setdefault(host, []).append(sj[nm])

        rows_of = lambda a: a.reshape(N_DEV, -1, d)
        if (i, which) == (3, "out"):
            stage("sg_out1", rows_of(g["sg_w_out"][1]), "rw_scan_bwd2")
        elif (i, which) == (3, "in"):
            stage("sg_in1", _cols_parts(g["sg_w_in"][1]), "rw_scan_bwd2")
        elif (i, which) == (2, "out"):
            stage("rw_out", rows_of(g["rw_w_out"]), "swa_attn_bwd1")
        elif (i, which) == (2, "in"):
            d_rw_in = jnp.concatenate([g["rw_w_main"], g["rw_w_lorain"][:, :2 * LORA]], axis=1)
            stage("rw_in", _cols_parts(d_rw_in), "swa_attn_bwd1")
        elif (i, which) == (1, "out"):
            stage("swa_out", rows_of(g["swa_w_out"]), "out_dx0")
        elif (i, which) == (1, "in"):
            stage("swa_in", _cols_parts(g["swa_w_in"]), "sg_mix_bwd0")
        elif (i, which) == (0, "out"):
            stage("sg_out0", rows_of(g["sg_w_out"][0]), "sg_in_dw0")
        else:
            stage("sg_in0", _cols_parts(g["sg_w_in"][0]), "sg_in_dx0")

    loss, dx, dmods, g = _local_step(x[0], positions.reshape(t, 1).astype(F32), loss_target[0], mods, norm_g, final_norm_g,
                                     layer_weights, hooks, on_grads)

    dmod_g = _gather("gather_dmod", dmods)
    dmod_all = dmod_g.transpose(1, 0, 2)
    dmod_cols = lax.dynamic_slice_in_dim(dmod_all, me * n_mod, n_mod, axis=2)
    g_mod_w, g_mod_b = _mod_bwd("mod_bwd", c_all, dmod_cols, dmod_all)

    d_b_sp = [g["sg_b_st"][j].T for j in range(2)]
    rep = [loss[0, :1], jnp.concatenate(g["norm_g"], axis=0), g["final_norm_g"], jnp.stack(g["sg_w_s"]), jnp.stack(d_b_sp),
           g["swa_sinks"], g["rw_hp"][2]]
    rep_shapes = [(1,), norm_g.shape, final_norm_g.shape, sg_w_spatial.shape, sg_b_spatial.shape, swa_sinks.shape, rwkv_r_k.shape]
    rep_sum = _sum_parts("sum_rep", _gather("gather_rep", _pack(rep, 128 * 256))).reshape(-1)
    loss_tot, g_norm_g, g_final, g_w_sp, g_b_sp, g_sinks, g_r_k = _unpack(rep_sum, rep_shapes)

    p_sg_in = jnp.concatenate([sj["sg_in0"]["out"], sj["sg_in1"]["out"]], axis=1)
    p_sg_out = jnp.concatenate([sj["sg_out0"]["out"], sj["sg_out1"]["out"]], axis=1)
    p_swa_in, p_swa_out, p_rw_in, p_rw_out = (sj[nm]["out"] for nm in ("swa_in", "swa_out", "rw_in", "rw_out"))
    d_mu = jnp.concatenate([g["rw_mu_main"], g["rw_mu_lora"][:, :2 * LORA]], axis=1)
    hp_flat = lambda a: a.reshape(1, -1)
    small_grads = dict(sg_ln_g=jnp.concatenate(g["sg_ln_g"], axis=0), sg_ln_b=jnp.concatenate(g["sg_ln_b"], axis=0), rwkv_mu=d_mu,
                       rwkv_w0=g["rw_w0"], rwkv_a0=g["rw_a0"], rwkv_k_k=hp_flat(g["rw_hp"][0]), rwkv_k_a=hp_flat(g["rw_hp"][1]),
                       rwkv_gn_g=hp_flat(g["rw_hp"][3]), rwkv_gn_b=hp_flat(g["rw_hp"][4]),
                       rwkv_w_lora=g["rw_w_lora"][None, :LORA], rwkv_a_lora=g["rw_a_lora"][None, LORA:2 * LORA])
    per_dest = []
    for dv in range(N_DEV):
        shards = []
        for n in small_names:
            full, w = small_grads[n], weights[n].shape[-1]
            shards.append(full[..., dv * w:(dv + 1) * w])
        per_dest.append(_pack(shards))
    small_parts = _exchange("scatter_small", jnp.stack(per_dest), True)

    out_g, out_d, out_m, out_v = {}, {}, {}, {}

    def update(name, grad, shape2d):
        w2, m2, v2 = (a[name].reshape(shape2d) for a in (weights, mom_m, mom_v))
        gg, dd, mm, vv = _adamw("adamw_" + name, w2, grad, m2, v2)
        shp = weights[name].shape
        out_g[name], out_d[name], out_m[name], out_v[name] = gg.reshape(shp), dd.reshape(shp), mm.reshape(shp), vv.reshape(shp)

    update("mod_w", g_mod_w.reshape(-1, n_mod), (-1, n_mod))
    update("sg_w_in", p_sg_in, (-1, sg_w_in.shape[2]))
    update("sg_w_out", p_sg_out, (-1, d))
    update("swa_w_in", p_swa_in, (-1, swa_w_in.shape[2]))
    update("swa_w_out", p_swa_out, (-1, d))
    update("rwkv_w_in", p_rw_in, (-1, n_rw))
    update("rwkv_w_out", p_rw_out, (-1, d))
    update("sg_w_spatial", g_w_sp.reshape(-1, 128), (-1, 128))
    w_pk, m_pk, v_pk = (_pack([a[n] for n in small_names]) for a in (weights, mom_m, mom_v))
    res = _adamw("adamw_small", w_pk, small_parts, m_pk, v_pk)
    for q, arrs in enumerate(zip(*[_unpack(r_.reshape(-1), small_shapes) for r_ in res])):
        out_g[small_names[q]], out_d[small_names[q]], out_m[small_names[q]], out_v[small_names[q]] = arrs
    rep_names = ["norm_g", "mod_b", "final_norm_g", "sg_b_spatial", "swa_sinks", "rwkv_r_k"]
    rep_grads = [g_norm_g, g_mod_b.reshape(mod_b.shape), g_final, g_b_sp, g_sinks, g_r_k]
    rep_shapes2 = [weights[n].shape for n in rep_names]
    w_pk, m_pk, v_pk = (_pack([a[n] for n in rep_names]) for a in (weights, mom_m, mom_v))
    res = _adamw("adamw_rep", w_pk, _pack(rep_grads), m_pk, v_pk)
    for q, arrs in enumerate(zip(*[_unpack(r_.reshape(-1), rep_shapes2) for r_ in res])):
        out_g[rep_names[q]], out_d[rep_names[q]], out_m[rep_names[q]], out_v[rep_names[q]] = arrs

    return (loss_tot.reshape(()), dx[None], *[out_g[n] for n in names], *[out_d[n] for n in names],
            *[out_m[n] for n in names], *[out_v[n] for n in names])
```

```python
import functools
import math

import jax
import jax.numpy as jnp
from jax import lax
from jax.experimental import pallas as pl
from jax.experimental.pallas import tpu as pltpu

F32, BF16 = jnp.float32, jnp.bfloat16
HI = lax.Precision.HIGHEST
S = jax.ShapeDtypeStruct
MESH = pl.DeviceIdType.MESH

N_DEV = 8
DEPTH = 4
HEAD = 64
SG_GROUPS = 16
SG_CHUNK = 128
SWA_BLOCK = 128
SWA_KV = 4
SWA_REP = 8
ROPE_THETA = 10000.0
LORA = 96
LORA_PAD = 256
RW_CHUNK = 64
RW_HEADS = 16
RW_PREC = lax.Precision.HIGH
DECAY_SCALE = math.exp(-0.5)
GN_EPS = 64e-5
RMS_EPS = 1e-6
LN_EPS = 1e-5
NEG = -1e30
ADAM_LR, ADAM_B1, ADAM_B2, ADAM_EPS, ADAM_WD, ADAM_STEP = 0.001, 0.9, 0.999, 1e-08, 0.01, 10
VMEM_MB = 56


def _params(sem=None):
    kw = dict(vmem_limit_bytes=VMEM_MB << 20)
    if sem is not None:
        kw["dimension_semantics"] = sem
    return pltpu.CompilerParams(**kw)


def _pick(n, opts):
    for o in opts:
        if n % o == 0:
            return o
    raise ValueError(f"no tile for {n}")


def _rows(name, fn, rows, consts, out_rows, out_accs, tm, jobs=()):
    t = rows[0].shape[0]
    nr, nc, no = len(rows), len(consts), len(out_rows)

    def body(*refs):
        outs = fn(*[r[...] for r in refs[:nr + nc]])
        if not isinstance(outs, (tuple, list)):
            outs = (outs,)
        for r, o in zip(refs[nr + nc:nr + nc + no], outs[:no]):
            r[...] = o.astype(r.dtype)
        i = pl.program_id(0)
        for r, o in zip(refs[nr + nc + no:], outs[no:]):
            @pl.when(i == 0)
            def _(r=r, o=o):
                r[...] = o.astype(r.dtype)

            @pl.when(i > 0)
            def _(r=r, o=o):
                r[...] += o.astype(r.dtype)

    in_specs = [pl.BlockSpec((tm, a.shape[1]), lambda i: (i, 0)) for a in rows]
    in_specs += [pl.BlockSpec(c.shape, lambda i, nd=c.ndim: (0,) * nd) for c in consts]
    out_specs = [pl.BlockSpec((tm, n), lambda i: (i, 0)) for n, _ in out_rows]
    out_specs += [pl.BlockSpec(s, lambda i, nd=len(s): (0,) * nd) for s in out_accs]
    out_shape = [S((t, n), dt) for n, dt in out_rows] + [S(s, F32) for s in out_accs]
    return _pcall(body, grid=(t // tm,), in_specs=in_specs, out_specs=out_specs, out_shape=out_shape, name=name,
                  semantics=("arbitrary",), inputs=(*rows, *consts), jobs=jobs)


_DN = {"nn": (((1,), (0,)), ((), ())), "nt": (((1,), (1,)), ((), ())), "tn": (((0,), (0,)), ((), ()))}


def _mm(name, a, b, mode, out_dtype, add=None, jobs=()):
    if mode == "nn":
        (m, k), (_, n) = a.shape, b.shape
    elif mode == "nt":
        (m, k), (n, _) = a.shape, b.shape
    else:
        (k, m), (_, n) = a.shape, b.shape
    tm, tn, tk = _pick(m, (1024, 512, 256, 128)), _pick(n, (1024, 512, 384, 256, 128)), _pick(k, (2048, 1536, 1024, 512, 384, 256, 128))
    nk = k // tk
    has_add = add is not None

    def body(*refs):
        a_ref, b_ref = refs[0], refs[1]
        o_ref, acc = refs[-2], refs[-1]
        kk = pl.program_id(2)
        prod = lax.dot_general(a_ref[...].astype(BF16), b_ref[...].astype(BF16), _DN[mode], preferred_element_type=F32)
        if nk == 1:
            o_ref[...] = (prod + refs[2][...].astype(F32) if has_add else prod).astype(o_ref.dtype)
            return

        @pl.when(kk == 0)
        def _():
            acc[...] = prod + refs[2][...].astype(F32) if has_add else prod

        @pl.when(kk > 0)
        def _():
            acc[...] += prod

        @pl.when(kk == nk - 1)
        def _():
            o_ref[...] = acc[...].astype(o_ref.dtype)

    a_spec = pl.BlockSpec((tk, tm), lambda i, j, q: (q, i)) if mode == "tn" else pl.BlockSpec((tm, tk), lambda i, j, q: (i, q))
    b_spec = pl.BlockSpec((tn, tk), lambda i, j, q: (j, q)) if mode == "nt" else pl.BlockSpec((tk, tn), lambda i, j, q: (q, j))
    o_spec = pl.BlockSpec((tm, tn), lambda i, j, q: (i, j))
    ins, specs = [a, b], [a_spec, b_spec]
    if has_add:
        ins.append(add)
        specs.append(o_spec)
    return _pcall(body, grid=(m // tm, n // tn, nk), in_specs=specs, out_specs=[o_spec], out_shape=[S((m, n), out_dtype)],
                  scratch_shapes=[pltpu.VMEM((tm, tn), F32)], name=name, semantics=("parallel", "parallel", "arbitrary"),
                  inputs=ins, jobs=jobs)[0]


def _exchange(name, src, scatter):
    blk = src.shape[1:] if scatter else src.shape

    def body(src_ref, dst_ref, send_sems, recv_sems, loc_sem):
        x, y, c = lax.axis_index("x"), lax.axis_index("y"), lax.axis_index("c")
        me = 4 * x + 2 * y + c

        def mine(d):
            return src_ref.at[d] if scatter else src_ref

        local = pltpu.make_async_copy(mine(me), dst_ref.at[me], loc_sem)
        local.start()
        sends, peers = [], []
        for k in range(1, N_DEV):
            px = 1 - x if k & 4 else x
            py = 1 - y if k & 2 else y
            pc = 1 - c if k & 1 else c
            pid = 4 * px + 2 * py + pc
            cp = pltpu.make_async_remote_copy(src_ref=mine(pid), dst_ref=dst_ref.at[me], send_sem=send_sems.at[k - 1],
                                              recv_sem=recv_sems.at[k - 1], device_id=(px, py, pc), device_id_type=MESH)
            cp.start()
            sends.append(cp)
            peers.append((pid, (px, py, pc)))
        for k in range(1, N_DEV):
            pid, dev = peers[k - 1]
            pltpu.make_async_remote_copy(src_ref=mine(pid), dst_ref=dst_ref.at[pid], send_sem=send_sems.at[k - 1],
                                         recv_sem=recv_sems.at[k - 1], device_id=dev, device_id_type=MESH).wait_recv()
        for cp in sends:
            cp.wait_send()
        local.wait()

    return pl.pallas_call(
        body, out_shape=S((N_DEV,) + tuple(blk), src.dtype),
        in_specs=[pl.BlockSpec(memory_space=pl.ANY)], out_specs=pl.BlockSpec(memory_space=pl.ANY),
        scratch_shapes=[pltpu.SemaphoreType.DMA((N_DEV - 1,)), pltpu.SemaphoreType.DMA((N_DEV - 1,)),
                        pltpu.SemaphoreType.DMA],
        name=name)(src)


class _Gather:
    @staticmethod
    def out_shape(src):
        return S((N_DEV,) + tuple(src.shape), src.dtype)

    scratch = (pltpu.SemaphoreType.DMA((N_DEV - 1,)), pltpu.SemaphoreType.DMA((N_DEV - 1,)), pltpu.SemaphoreType.DMA)

    def __init__(self, src_ref, dst_ref, send_sems, recv_sems, loc_sem):
        self.refs = (src_ref, dst_ref, send_sems, recv_sems, loc_sem)
        x, y, c = lax.axis_index("x"), lax.axis_index("y"), lax.axis_index("c")
        self.c, self.me, self.sibling = c, (x, y, c), (x, y, 1 - c)
        self.chips = [(1 - x, y), (x, 1 - y), (1 - x, 1 - y)]

    def rows(self, px, py, pc):
        return self.refs[1].at[4 * px + 2 * py + pc]

    def copy(self, k, block, to, own=False):
        src_ref, _, send_sems, recv_sems, _ = self.refs
        return pltpu.make_async_remote_copy(src_ref=src_ref if own else self.rows(*block), dst_ref=self.rows(*block),
                                            send_sem=send_sems.at[k], recv_sem=recv_sems.at[k], device_id=to,
                                            device_id_type=MESH)

    def local(self):
        return pltpu.make_async_copy(self.refs[0], self.rows(*self.me), self.refs[4])

    def first(self):
        return [self.copy(0, self.me, self.sibling, own=True)] + [self.copy(1 + j, self.me, (*chip, self.c), own=True)
                                                                  for j, chip in enumerate(self.chips)]

    def start(self):
        self.local().start()
        for cp in self.first():
            cp.start()

    def finish(self):
        c = self.c
        passed = [self.copy(4 + j, (*chip, c), self.sibling) for j, chip in enumerate(self.chips)]
        for j, chip in enumerate(self.chips):
            self.copy(1 + j, (*chip, c), self.me).wait_recv()
            passed[j].start()
        self.copy(0, self.sibling, self.me).wait_recv()
        for j, chip in enumerate(self.chips):
            self.copy(4 + j, (*chip, 1 - c), self.me).wait_recv()
        for cp in self.first() + passed:
            cp.wait_send()
        self.local().wait()


class _Chips:
    @staticmethod
    def out_shape(src):
        return S(src.shape, src.dtype)

    scratch = (pltpu.SemaphoreType.DMA((N_DEV // 2 - 1,)), pltpu.SemaphoreType.DMA((N_DEV // 2 - 1,)), pltpu.SemaphoreType.DMA)

    def __init__(self, src_ref, dst_ref, send_sems, recv_sems, loc_sem):
        self.refs = (src_ref, dst_ref, send_sems, recv_sems, loc_sem)
        x, y, c = lax.axis_index("x"), lax.axis_index("y"), lax.axis_index("c")
        self.c, self.mine = c, 2 * x + y
        self.chips = [(1 - x, y), (x, 1 - y), (1 - x, 1 - y)]

    def local(self):
        src_ref, dst_ref, _, _, loc_sem = self.refs
        return pltpu.make_async_copy(src_ref.at[self.mine], dst_ref.at[self.mine], loc_sem)

    def send(self, j):
        src_ref, dst_ref, send_sems, recv_sems, _ = self.refs
        px, py = self.chips[j]
        return pltpu.make_async_remote_copy(src_ref=src_ref.at[2 * px + py], dst_ref=dst_ref.at[self.mine],
                                            send_sem=send_sems.at[j], recv_sem=recv_sems.at[j],
                                            device_id=(px, py, self.c), device_id_type=MESH)

    def arrival(self, j):
        src_ref, dst_ref, send_sems, recv_sems, _ = self.refs
        px, py = self.chips[j]
        return pltpu.make_async_remote_copy(src_ref=src_ref.at[self.mine], dst_ref=dst_ref.at[2 * px + py],
                                            send_sem=send_sems.at[j], recv_sem=recv_sems.at[j],
                                            device_id=(px, py, self.c), device_id_type=MESH)

    def start(self):
        self.local().start()
        for j in range(len(self.chips)):
            self.send(j).start()

    def finish(self):
        for j in range(len(self.chips)):
            self.arrival(j).wait_recv()
        for j in range(len(self.chips)):
            self.send(j).wait_send()
        self.local().wait()


def _exchange_call(name, cls, src):
    def body(*refs):
        ex = cls(*refs)
        ex.start()
        ex.finish()

    return pl.pallas_call(body, out_shape=cls.out_shape(src), in_specs=[pl.BlockSpec(memory_space=pl.ANY)],
                          out_specs=pl.BlockSpec(memory_space=pl.ANY), scratch_shapes=list(cls.scratch), name=name)(src)


def _gather(name, src):
    return _exchange_call(name, _Gather, src)


def _pcall(body, *, grid, in_specs, out_specs, out_shape, scratch_shapes=(), name, semantics, inputs, jobs=()):
    if not jobs:
        return pl.pallas_call(body, grid=grid, in_specs=in_specs, out_specs=out_specs, out_shape=out_shape,
                              scratch_shapes=list(scratch_shapes), name=name, compiler_params=_params(semantics))(*inputs)
    n_in, n_out, n_scr, nj = len(in_specs), len(out_specs), len(scratch_shapes), len(jobs)

    def hosted(*refs):
        ins, srcs = refs[:n_in], refs[n_in:n_in + nj]
        outs, dsts = refs[n_in + nj:n_in + nj + n_out], refs[n_in + nj + n_out:n_in + 2 * nj + n_out]
        scr, sems = refs[n_in + 2 * nj + n_out:n_in + 2 * nj + n_out + n_scr], refs[n_in + 2 * nj + n_out + n_scr:]
        ids = [pl.program_id(q) for q in range(len(grid))]
        first = functools.reduce(jnp.logical_and, [i == 0 for i in ids])
        last = functools.reduce(jnp.logical_and, [i == g - 1 for i, g in zip(ids, grid)])
        make = lambda q: jobs[q]["cls"](srcs[q], dsts[q], *sems[3 * q:3 * q + 3])

        @pl.when(first)
        def _():
            for q in range(nj):
                make(q).start()

        body(*ins, *outs, *scr)

        @pl.when(last)
        def _():
            for q in range(nj):
                make(q).finish()

    anyspec = pl.BlockSpec(memory_space=pl.ANY)
    res = pl.pallas_call(
        hosted, grid=grid, in_specs=list(in_specs) + [anyspec] * nj, out_specs=list(out_specs) + [anyspec] * nj,
        out_shape=list(out_shape) + [j["cls"].out_shape(j["src"]) for j in jobs],
        scratch_shapes=list(scratch_shapes) + [s for j in jobs for s in j["cls"].scratch], name=name,
        compiler_params=_params(("arbitrary",) * len(grid)))(*inputs, *[j["src"] for j in jobs])
    for j, out in zip(jobs, res[n_out:]):
        j["out"] = out
    return res[:n_out]


def _scatter_pairs(name, parts):
    _, r, c_ = parts.shape
    n_chip = N_DEV // 2

    def stage1(src_ref, dst_ref, send_sems, recv_sems):
        x, y, c = lax.axis_index("x"), lax.axis_index("y"), lax.axis_index("c")
        sends = []
        for q in range(n_chip):
            cp = pltpu.make_async_remote_copy(src_ref=src_ref.at[2 * q + 1 - c], dst_ref=dst_ref.at[q],
                                              send_sem=send_sems.at[q], recv_sem=recv_sems.at[q],
                                              device_id=(x, y, 1 - c), device_id_type=MESH)
            cp.start()
            sends.append(cp)
        for q in range(n_chip):
            pltpu.make_async_remote_copy(src_ref=src_ref.at[2 * q + c], dst_ref=dst_ref.at[q], send_sem=send_sems.at[q],
                                         recv_sem=recv_sems.at[q], device_id=(x, y, 1 - c), device_id_type=MESH).wait_recv()
        for cp in sends:
            cp.wait_send()

    from_sibling = pl.pallas_call(
        stage1, out_shape=S((n_chip, r, c_), parts.dtype),
        in_specs=[pl.BlockSpec(memory_space=pl.ANY)], out_specs=pl.BlockSpec(memory_space=pl.ANY),
        scratch_shapes=[pltpu.SemaphoreType.DMA((n_chip,)), pltpu.SemaphoreType.DMA((n_chip,))], name=name + "_pair")(parts)

    tm = _pick(r, (512, 256, 128, 64, 32, 16, 8)) if r % 8 == 0 else r
    core = lax.axis_index("c").astype(jnp.int32).reshape(1)

    def pair_sum(core_ref, mine_ref, sib_ref, o_ref):
        o_ref[...] = (mine_ref[0].astype(F32) + sib_ref[...].astype(F32)).astype(o_ref.dtype)

    pair = pl.pallas_call(
        pair_sum, out_shape=S((n_chip, r, c_), parts.dtype),
        grid_spec=pltpu.PrefetchScalarGridSpec(
            num_scalar_prefetch=1, grid=(n_chip, r // tm),
            in_specs=[pl.BlockSpec((1, 1, tm, c_), lambda q, i, core_ref: (q, core_ref[0], i, 0)),
                      pl.BlockSpec((1, tm, c_), lambda q, i, core_ref: (q, i, 0))],
            out_specs=pl.BlockSpec((1, tm, c_), lambda q, i, core_ref: (q, i, 0))),
        name=name + "_sum", compiler_params=_params(("parallel", "parallel")))(
            core, parts.reshape(n_chip, 2, r, c_), from_sibling)

    return pair


def _sum_parts(name, parts):
    n_parts, r, c = parts.shape
    tm = _pick(r, (512, 256, 128, 64, 32, 16, 8)) if r % 8 == 0 else r

    def body(p_ref, o_ref):
        acc = p_ref[0].astype(F32)
        for d in range(1, n_parts):
            acc = acc + p_ref[d].astype(F32)
        o_ref[...] = acc

    return pl.pallas_call(body, grid=(r // tm,), in_specs=[pl.BlockSpec((n_parts, tm, c), lambda i: (0, i, 0))],
                          out_specs=pl.BlockSpec((tm, c), lambda i: (i, 0)), out_shape=S((r, c), F32), name=name,
                          compiler_params=_params(("parallel",)))(parts)


def _adamw(name, w, g, m, v, jobs=()):
    r, c = w.shape
    parts = g.ndim == 3
    n_parts = g.shape[0] if parts else 1
    tm = _pick(r, (256, 128, 64, 32, 16, 8)) if r % 8 == 0 else r

    def body(w_ref, g_ref, m_ref, v_ref, go_ref, d_ref, mo_ref, vo_ref):
        if parts:
            gg = g_ref[0].astype(F32)
            for d in range(1, n_parts):
                gg = gg + g_ref[d].astype(F32)
        else:
            gg = g_ref[...]
        mm = ADAM_B1 * m_ref[...] + (1.0 - ADAM_B1) * gg
        vv = ADAM_B2 * v_ref[...] + (1.0 - ADAM_B2) * jnp.square(gg)
        m_hat = mm / (1.0 - ADAM_B1 ** ADAM_STEP)
        v_hat = vv / (1.0 - ADAM_B2 ** ADAM_STEP)
        go_ref[...] = gg
        d_ref[...] = -ADAM_LR * (m_hat / (jnp.sqrt(v_hat) + ADAM_EPS) + ADAM_WD * w_ref[...])
        mo_ref[...] = mm
        vo_ref[...] = vv

    spec = pl.BlockSpec((tm, c), lambda i: (i, 0))
    g_spec = pl.BlockSpec((n_parts, tm, c), lambda i: (0, i, 0)) if parts else spec
    return _pcall(body, grid=(r // tm,), in_specs=[spec, g_spec, spec, spec], out_specs=[spec] * 4,
                  out_shape=[S((r, c), F32)] * 4, name=name, semantics=("parallel",), inputs=(w, g, m, v), jobs=jobs)


def _rms(x, g):
    return x * lax.rsqrt(jnp.mean(x * x, axis=-1, keepdims=True) + RMS_EPS) * g


def _adaln(x, g, shift, scale):
    return _rms(x, g) * (1.0 + scale) + shift


def _dot(a, b, dn="nn", hi=False, prec=None):
    if hi or prec is not None:
        return lax.dot_general(a, b, _DN[dn], precision=HI if hi else prec, preferred_element_type=F32)
    return lax.dot_general(a.astype(BF16), b.astype(BF16), _DN[dn], preferred_element_type=F32)


def _sg_mix(p, ln_g, ln_b, w_s, b_st):
    d = p.shape[1] // 3
    gd = d // SG_GROUPS
    u = jax.nn.gelu(p[:, :d])
    vf = jax.nn.gelu(p[:, d:2 * d])
    z = p[:, 2 * d:]
    mean = jnp.mean(vf, axis=-1, keepdims=True)
    var = jnp.mean(jnp.square(vf - mean), axis=-1, keepdims=True)
    vn = (vf - mean) * lax.rsqrt(var + LN_EPS) * ln_g + ln_b
    row = lax.broadcasted_iota(jnp.int32, (SG_CHUNK, SG_CHUNK), 0)
    col = lax.broadcasted_iota(jnp.int32, (SG_CHUNK, SG_CHUNK), 1)
    fs = []
    for g in range(SG_GROUPS):
        w = jnp.where(row >= col, w_s[g], 0.0)
        fs.append(_dot(w, vn[:, g * gd:(g + 1) * gd]))
    sel = (lax.broadcasted_iota(jnp.int32, (SG_GROUPS, d), 1) // gd
           == lax.broadcasted_iota(jnp.int32, (SG_GROUPS, d), 0)).astype(F32)
    f = jnp.concatenate(fs, axis=1) + _dot(b_st, sel, hi=True)
    return u * f * jax.nn.silu(z)


def _rot_half(x):
    n = x.shape[1]
    lane = lax.broadcasted_iota(jnp.int32, x.shape, 1)
    return jnp.where(lane % HEAD < HEAD // 2, -pltpu.roll(x, n - HEAD // 2, 1), pltpu.roll(x, HEAD // 2, 1))


def _rope(x, cos, sin, sign):
    reps = x.shape[1] // cos.shape[1]
    return x * jnp.tile(cos, (1, reps)) + sign * _rot_half(x) * jnp.tile(sin, (1, reps))


def _attn_block(q, kp, kc, vp, vc, sink, prev_bias):
    each = lambda f, *ls: [f(*xs) for xs in zip(*ls)]
    r = sink[0].shape[0]
    scores = lambda a, b: (_dot(a, b, "nt") * (HEAD ** -0.5)).reshape(r, SWA_BLOCK, SWA_BLOCK)
    sp, sc = each(scores, q, kp), each(scores, q, kc)
    qi = lax.broadcasted_iota(jnp.int32, (r, SWA_BLOCK, SWA_BLOCK), 1)
    kj = lax.broadcasted_iota(jnp.int32, (r, SWA_BLOCK, SWA_BLOCK), 2)
    sp = each(lambda s: jnp.where(kj > qi, s, NEG) + prev_bias, sp)
    sc = each(lambda s: jnp.where(kj <= qi, s, NEG), sc)
    m = each(lambda a, b, s: jnp.maximum(jnp.maximum(jnp.max(a, axis=-1, keepdims=True), jnp.max(b, axis=-1, keepdims=True)), s),
             sp, sc, sink)
    ep, ec = each(lambda s, m_: jnp.exp(s - m_), sp, m), each(lambda s, m_: jnp.exp(s - m_), sc, m)
    denom = each(lambda a, b, s, m_: jnp.sum(a, axis=-1, keepdims=True) + jnp.sum(b, axis=-1, keepdims=True) + jnp.exp(s - m_),
                 ep, ec, sink, m)
    flat = lambda e, dn: (e / dn).reshape(r * SWA_BLOCK, SWA_BLOCK)
    pp, pc = each(flat, ep, denom), each(flat, ec, denom)
    return each(lambda a, va, b, vb: _dot(a, va) + _dot(b, vb), pp, vp, pc, vc)


def _attn_block_bwd(q, kp, kc, vp, vc, sink, do, prev_bias):
    each = lambda f, *ls: [f(*xs) for xs in zip(*ls)]
    r = sink[0].shape[0]
    scale = HEAD ** -0.5
    cube = lambda x: x.reshape(r, SWA_BLOCK, SWA_BLOCK)
    flat = lambda x: x.reshape(r * SWA_BLOCK, SWA_BLOCK)
    scores = lambda a, b: cube(_dot(a, b, "nt") * scale)
    sp, sc = each(scores, q, kp), each(scores, q, kc)
    qi = lax.broadcasted_iota(jnp.int32, (r, SWA_BLOCK, SWA_BLOCK), 1)
    kj = lax.broadcasted_iota(jnp.int32, (r, SWA_BLOCK, SWA_BLOCK), 2)
    sp = each(lambda s: jnp.where(kj > qi, s, NEG) + prev_bias, sp)
    sc = each(lambda s: jnp.where(kj <= qi, s, NEG), sc)
    m = each(lambda a, b, s: jnp.maximum(jnp.maximum(jnp.max(a, axis=-1, keepdims=True), jnp.max(b, axis=-1, keepdims=True)), s),
             sp, sc, sink)
    ep, ec = each(lambda s, m_: jnp.exp(s - m_), sp, m), each(lambda s, m_: jnp.exp(s - m_), sc, m)
    es = each(lambda s, m_: jnp.exp(s - m_), sink, m)
    denom = each(lambda a, b, e: jnp.sum(a, axis=-1, keepdims=True) + jnp.sum(b, axis=-1, keepdims=True) + e, ep, ec, es)
    pp, pc = each(lambda e, dn: e / dn, ep, denom), each(lambda e, dn: e / dn, ec, denom)
    dpp, dpc = each(lambda g, v_: cube(_dot(g, v_, "nt")), do, vp), each(lambda g, v_: cube(_dot(g, v_, "nt")), do, vc)
    dvp, dvc = each(lambda p, g: _dot(flat(p), g, "tn"), pp, do), each(lambda p, g: _dot(flat(p), g, "tn"), pc, do)
    delta = each(lambda a, da, b, db: jnp.sum(a * da, axis=-1, keepdims=True) + jnp.sum(b * db, axis=-1, keepdims=True),
                 pp, dpp, pc, dpc)
    dsp, dsc = each(lambda p, dp, dl: flat(p * (dp - dl)), pp, dpp, delta), each(lambda p, dp, dl: flat(p * (dp - dl)), pc, dpc, delta)
    dsink = each(lambda e, dn, dl: -jnp.sum(e / dn * dl, axis=1, keepdims=True), es, denom, delta)
    dq = each(lambda a, ka, b, kb: (_dot(a, ka) + _dot(b, kb)) * scale, dsp, kp, dsc, kc)
    dkp, dkc = each(lambda a, q_: _dot(a, q_, "tn") * scale, dsp, q), each(lambda a, q_: _dot(a, q_, "tn") * scale, dsc, q)
    return dq, dkp, dkc, dvp, dvc, dsink


def _rwkv_chunk(s0, r, k, v, logw, a, k_k, k_a, r_k, gn_g, gn_b):
    c = r[0].shape[0]
    each = lambda f, *ls: [f(*xs) for xs in zip(*ls)]
    gram = functools.partial(_dot, prec=RW_PREC)
    row = lax.broadcasted_iota(jnp.int32, (c, c), 0)
    col = lax.broadcasted_iota(jnp.int32, (c, c), 1)
    incl, strict = row >= col, row > col
    ones_l = incl.astype(F32)

    def unit(x):
        return x / jnp.maximum(jnp.sqrt(jnp.sum(x * x, axis=-1, keepdims=True)), 1e-12)

    kk = each(lambda k_, p: unit(k_ * p), k, k_k)
    km = each(lambda k_, a_, p: k_ * (1.0 + (a_ - 1.0) * p), k, a, k_a)
    b = each(lambda x, a_: x * a_, kk, a)
    first_half = lax.broadcasted_iota(jnp.int32, (c, HEAD), 0) < c // 2
    mid = each(lambda w: jnp.sum(jnp.where(first_half, w, 0.0), axis=0, keepdims=True), logw)
    cum = each(lambda w, m: _dot(ones_l, w, hi=True) - m, logw, mid)
    alpha = each(lambda x, cu, w: x * jnp.exp(cu - w), kk, cum, logw)
    beta = each(lambda x, cu: x * jnp.exp(-cu), b, cum)
    kap = each(lambda x, cu: x * jnp.exp(-cu), km, cum)
    rho = each(lambda x, cu: x * jnp.exp(cu), r, cum)
    s0 = each(lambda s, m: s * jnp.exp(m), s0, mid)
    lab = each(lambda x, y_: jnp.where(strict, gram(x, y_, "nt"), 0.0), alpha, beta)
    lak = each(lambda x, y_: jnp.where(strict, gram(x, y_, "nt"), 0.0), alpha, kap)
    xs = each(lambda al, s, l, v_: _dot(al, s, "nt") + _dot(l, v_), alpha, s0, lak, v)
    xs = each(lambda x, l: x - _dot(l, x), xs, lab)
    lp, power = lab, 2
    while power < c:
        lp = each(lambda l: _dot(l, l), lp)
        xs = each(lambda x, l: x + _dot(l, x), xs, lp)
        power *= 2
    u = each(lambda x: -x, xs)
    mrb = each(lambda x, y_: jnp.where(incl, gram(x, y_, "nt"), 0.0), rho, beta)
    mrk = each(lambda x, y_: jnp.where(incl, gram(x, y_, "nt"), 0.0), rho, kap)
    y = each(lambda rh, s, mb, u_, mk, v_: _dot(rh, s, "nt") + _dot(mb, u_) + _dot(mk, v_), rho, s0, mrb, u, mrk, v)
    s1 = each(lambda s, u_, be, v_, ka, w, m: (s + _dot(u_, be, "tn") + _dot(v_, ka, "tn"))
              * jnp.exp(jnp.sum(w, axis=0, keepdims=True) - m), s0, u, beta, v, kap, logw, mid)

    def finish(y_, g, bias, r_, km_, rk, v_):
        mean = jnp.mean(y_, axis=-1, keepdims=True)
        var = jnp.mean(jnp.square(y_ - mean), axis=-1, keepdims=True)
        y_ = (y_ - mean) * lax.rsqrt(var + GN_EPS) * g + bias
        return y_ + jnp.sum(r_ * km_ * rk, axis=-1, keepdims=True) * v_

    return each(finish, y, gn_g, gn_b, r, km, r_k, v), s1


def _norm_fwd(name, x, g, shift, scale):
    return _rows(name, lambda x_, g_, sh, sc: _adaln(x_, g_, sh, sc), [x], [g, shift, scale], [(x.shape[1], BF16)], [], 256)[0]


def _norm_bwd(name, x, dh, dx_res, g, shift, scale):
    d = x.shape[1]

    def fn(x_, dh_, dr_, g_, sh, sc):
        _, vjp = jax.vjp(_adaln, x_, g_, sh, sc)
        dx, dg, dsh, dsc = vjp(dh_)
        return dx + dr_, dg, dsh, dsc

    return _rows(name, fn, [x, dh, dx_res], [g, shift, scale], [(d, F32)], [(1, d)] * 3, 256)


def _resid_fwd(name, x, y, gate):
    return _rows(name, lambda x_, y_, g_: x_ + g_ * y_, [x, y], [gate], [(x.shape[1], F32)], [], 256)[0]


def _resid_bwd(name, dx, y, gate):
    d = dx.shape[1]
    return _rows(name, lambda dx_, y_, g_: (g_ * dx_, jnp.sum(dx_ * y_, axis=0, keepdims=True)), [dx, y], [gate],
                 [(d, BF16)], [(1, d)], 256)


def _sg_fwd(name, p, ln_g, ln_b, w_s, b_st):
    d = p.shape[1] // 3
    return _rows(name, _sg_mix, [p], [ln_g, ln_b, w_s, b_st], [(d, BF16)], [], SG_CHUNK)[0]


def _sg_bwd(name, p, dmix, ln_g, ln_b, w_s, b_st, jobs=()):
    def fn(p_, dm_, lg, lb, ws, bs):
        _, vjp = jax.vjp(_sg_mix, p_, lg, lb, ws, bs)
        return vjp(dm_)

    return _rows(name, fn, [p, dmix], [ln_g, ln_b, w_s, b_st], [(p.shape[1], BF16)],
                 [ln_g.shape, ln_b.shape, w_s.shape, b_st.shape], SG_CHUNK, jobs=jobs)


def _rope_tables(pos, inv_freq):
    ang = pos * inv_freq
    return jnp.cos(ang), jnp.sin(ang)


def _swa_pre(name, p, pos, inv_freq, d):
    kvw = SWA_KV * HEAD

    def fn(p_, pos_, fr):
        cos, sin = _rope_tables(pos_, fr)
        return (_rope(p_[:, :d], cos, sin, 1.0), _rope(p_[:, d:d + kvw], cos, sin, 1.0), p_[:, d + kvw:d + 2 * kvw])

    return _rows(name, fn, [p, pos], [inv_freq], [(d, BF16), (kvw, BF16), (kvw, BF16)], [], 256)


def _q_groups(ref, kv, rep):
    heads = _head_cols(ref, kv * rep)
    return [jnp.concatenate(heads[g * rep:(g + 1) * rep], axis=0) for g in range(kv)]


def _q_ungroup(groups, rep):
    return jnp.concatenate([g[h * SWA_BLOCK:(h + 1) * SWA_BLOCK] for g in groups for h in range(rep)], axis=1)


def _swa_attn_fwd(name, q, k, v, sinks, jobs=()):
    t, d = q.shape
    kv, rep = sinks.shape[0], sinks.shape[1]
    nb = t // SWA_BLOCK

    def body(q_ref, kp_ref, kc_ref, vp_ref, vc_ref, s_ref, o_ref):
        prev_bias = jnp.where(pl.program_id(0) > 0, 0.0, NEG).astype(F32)
        o = _attn_block(_q_groups(q_ref, kv, rep), *[_head_cols(ref, kv) for ref in (kp_ref, kc_ref, vp_ref, vc_ref)],
                        [s_ref[g] for g in range(kv)], prev_bias)
        o_ref[...] = _q_ungroup(o, rep)

    qs = pl.BlockSpec((SWA_BLOCK, d), lambda n: (n, 0))
    cur = pl.BlockSpec((SWA_BLOCK, kv * HEAD), lambda n: (n, 0))
    prev = pl.BlockSpec((SWA_BLOCK, kv * HEAD), lambda n: (jnp.maximum(n - 1, 0), 0))
    ss = pl.BlockSpec(sinks.shape, lambda n: (0, 0, 0, 0))
    return _pcall(body, grid=(nb,), in_specs=[qs, prev, cur, prev, cur, ss], out_specs=[qs], out_shape=[S((t, d), F32)],
                  name=name, semantics=("parallel",), inputs=(q, k, k, v, v, sinks), jobs=jobs)[0]


def _swa_attn_bwd(name, q, k, v, sinks, do, jobs=()):
    t, d = q.shape
    kv, rep = sinks.shape[0], sinks.shape[1]
    nb = t // SWA_BLOCK

    def body(q_ref, kp_ref, kc_ref, vp_ref, vc_ref, s_ref, do_ref, dq_ref, dkc_ref, dkp_ref, dvc_ref, dvp_ref, ds_ref):
        n = pl.program_id(0)
        prev_bias = jnp.where(n > 0, 0.0, NEG).astype(F32)
        args = [_q_groups(q_ref, kv, rep)] + [_head_cols(ref, kv) for ref in (kp_ref, kc_ref, vp_ref, vc_ref)]
        dq, dkp, dkc, dvp, dvc, ds = _attn_block_bwd(*args, [s_ref[g] for g in range(kv)], _q_groups(do_ref, kv, rep), prev_bias)
        dq_ref[...] = _q_ungroup(dq, rep)
        for ref, val in ((dkc_ref, dkc), (dkp_ref, dkp), (dvc_ref, dvc), (dvp_ref, dvp)):
            ref[...] = jnp.concatenate(val, axis=1)

        @pl.when(n == 0)
        def _():
            ds_ref[...] = jnp.zeros_like(ds_ref)

        for g in range(kv):
            ds_ref[g] += ds[g]

    qs = pl.BlockSpec((SWA_BLOCK, d), lambda n: (n, 0))
    cur = pl.BlockSpec((SWA_BLOCK, kv * HEAD), lambda n: (n, 0))
    prev = pl.BlockSpec((SWA_BLOCK, kv * HEAD), lambda n: (jnp.maximum(n - 1, 0), 0))
    ss = pl.BlockSpec(sinks.shape, lambda n: (0, 0, 0, 0))
    return _pcall(body, grid=(nb,), in_specs=[qs, prev, cur, prev, cur, ss, qs], out_specs=[qs, cur, cur, cur, cur, ss],
                  out_shape=[S((t, d), F32)] + [S((t, kv * HEAD), F32)] * 4 + [S(sinks.shape, F32)], name=name,
                  semantics=("arbitrary",), inputs=(q, k, k, v, v, sinks, do), jobs=jobs)


def _gate_fwd(name, o, z_src, z_off, d):
    return _rows(name, lambda o_, p_: o_ * jax.nn.silu(p_[:, z_off:z_off + d]), [o, z_src], [], [(d, BF16)], [], 256)[0]


def _gate_bwd(name, o, z_src, z_off, d, dmix):
    def fn(o_, p_, dm_):
        _, vjp = jax.vjp(lambda oo, zz: oo * jax.nn.silu(zz), o_, p_[:, z_off:z_off + d])
        return vjp(dm_)

    return _rows(name, fn, [o, z_src, dmix], [], [(d, F32), (d, F32)], [], 256)


def _swa_post_bwd(name, dq, dkc, dkp_up, dvc, dvp_up, dz, pos, inv_freq):
    def fn(dq_, dkc_, dkp_, dvc_, dvp_, dz_, pos_, fr):
        cos, sin = _rope_tables(pos_, fr)
        return jnp.concatenate([_rope(dq_, cos, sin, -1.0), _rope(dkc_ + dkp_, cos, sin, -1.0), dvc_ + dvp_, dz_], axis=1)

    n = dq.shape[1] + dkc.shape[1] + dvc.shape[1] + dz.shape[1]
    return _rows(name, fn, [dq, dkc, dkp_up, dvc, dvp_up, dz, pos], [inv_freq], [(n, BF16)], [], 256)[0]


HALO = 8


def _row_before(x, halo_ref, i):
    first = jnp.where(i > 0, halo_ref[pl.ds(HALO - 1, 1), :], 0.0)
    row = lax.broadcasted_iota(jnp.int32, x.shape, 0)
    return jnp.where(row == 0, first, pltpu.roll(x, 1, 0))


def _row_after(x, halo, i, n_tiles):
    last = jnp.where(i < n_tiles - 1, halo, 0.0)
    row = lax.broadcasted_iota(jnp.int32, x.shape, 0)
    return jnp.where(row == x.shape[0] - 1, last, pltpu.roll(x, x.shape[0] - 1, 0))


def _lerp_fwd(name, p, mu, widths):
    t, n = p.shape
    tm = 128

    def body(p_ref, halo_ref, mu_ref, *o_refs):
        x = p_ref[...]
        pm = x + (_row_before(x, halo_ref, pl.program_id(0)) - x) * mu_ref[...]
        o = 0
        for ref, w in zip(o_refs, widths):
            ref[...] = pm[:, o:o + w]
            o += w

    return pl.pallas_call(
        body, grid=(t // tm,),
        in_specs=[pl.BlockSpec((tm, n), lambda i: (i, 0)),
                  pl.BlockSpec((HALO, n), lambda i: (jnp.maximum(i * (tm // HALO) - 1, 0), 0)),
                  pl.BlockSpec((1, n), lambda i: (0, 0))],
        out_specs=[pl.BlockSpec((tm, w), lambda i: (i, 0)) for w in widths],
        out_shape=[S((t, w), F32) for w in widths], name=name, compiler_params=_params(("parallel",)))(p, p, mu)


def _lerp_bwd(name, dpm_parts, p, mu):
    t, n = p.shape
    k = len(dpm_parts)
    tm = 64
    n_tiles = t // tm

    def body(*refs):
        d_refs, dh_refs = refs[:k], refs[k:2 * k]
        p_ref, ph_ref, mu_ref, dp_ref, dmu_ref = refs[2 * k:]
        i = pl.program_id(0)
        cat = lambda vals: jnp.concatenate(vals, axis=1) if k > 1 else vals[0]
        dpm = cat([r[...] for r in d_refs])
        dnext = cat([r[pl.ds(0, 1), :] for r in dh_refs])
        x, mu_ = p_ref[...], mu_ref[...]
        dp_ref[...] = (dpm * (1.0 - mu_) + _row_after(dpm, dnext, i, n_tiles) * mu_).astype(dp_ref.dtype)
        dmu = jnp.sum(dpm * (_row_before(x, ph_ref, i) - x), axis=0, keepdims=True)

        @pl.when(i == 0)
        def _():
            dmu_ref[...] = dmu

        @pl.when(i > 0)
        def _():
            dmu_ref[...] += dmu

    per = tm // HALO
    d_specs = [pl.BlockSpec((tm, a.shape[1]), lambda i: (i, 0)) for a in dpm_parts]
    dh_specs = [pl.BlockSpec((HALO, a.shape[1]), lambda i: (jnp.minimum((i + 1) * per, t // HALO - 1), 0)) for a in dpm_parts]
    return pl.pallas_call(
        body, grid=(n_tiles,),
        in_specs=d_specs + dh_specs + [pl.BlockSpec((tm, n), lambda i: (i, 0)),
                                       pl.BlockSpec((HALO, n), lambda i: (jnp.maximum(i * per - 1, 0), 0)),
                                       pl.BlockSpec((1, n), lambda i: (0, 0))],
        out_specs=[pl.BlockSpec((tm, n), lambda i: (i, 0)), pl.BlockSpec((1, n), lambda i: (0, 0))],
        out_shape=[S((t, n), BF16), S((1, n), F32)], name=name,
        compiler_params=_params(("arbitrary",)))(*dpm_parts, *dpm_parts, p, p, mu)


def _lora_act(pl_, w0, w_lora, a0, a_lora):
    logw = -DECAY_SCALE * jax.nn.sigmoid(w0 + _dot(jnp.tanh(pl_), w_lora))
    a = jax.nn.sigmoid(a0 + _dot(pl_, a_lora))
    return logw, a


def _lora_fwd(name, pl_, w0, w_lora, a0, a_lora):
    d = w0.shape[1]
    return _rows(name, _lora_act, [pl_], [w0, w_lora, a0, a_lora], [(d, F32), (d, F32)], [], 256)


def _lora_bwd(name, pl_, dlogw, da, w0, w_lora, a0, a_lora):
    def fn(p_, dl_, da_, w0_, wl_, a0_, al_):
        _, vjp = jax.vjp(_lora_act, p_, w0_, wl_, a0_, al_)
        return vjp((dl_, da_))

    return _rows(name, fn, [pl_, dlogw, da], [w0, w_lora, a0, a_lora], [(pl_.shape[1], F32)],
                 [w0.shape, w_lora.shape, a0.shape, a_lora.shape], 256)


def _head_cols(ref, hb):
    x = ref[...].astype(F32)
    xo = pltpu.roll(x, x.shape[1] - HEAD, 1)
    return [(x if j % 2 == 0 else xo)[:, 2 * HEAD * (j // 2):2 * HEAD * (j // 2) + HEAD] for j in range(hb)]


def _rwkv_scan_fwd(name, r, k, v, logw, a, hp, jobs=()):
    t, d = r.shape
    h, nc, hb = d // HEAD, t // RW_CHUNK, RW_HEADS

    def body(r_ref, k_ref, v_ref, w_ref, a_ref, kk_ref, ka_ref, rk_ref, gg_ref, gb_ref, y_ref, st_ref, s_scr):
        @pl.when(pl.program_id(1) == 0)
        def _():
            s_scr[...] = jnp.zeros_like(s_scr)

        s0 = [s_scr[j] for j in range(hb)]
        for j in range(hb):
            st_ref[j, 0] = s0[j]
        y, s1 = _rwkv_chunk(s0, *[_head_cols(ref, hb) for ref in (r_ref, k_ref, v_ref, w_ref, a_ref, kk_ref, ka_ref, rk_ref,
                                                                 gg_ref, gb_ref)])
        y_ref[...] = jnp.concatenate(y, axis=1)
        for j in range(hb):
            s_scr[j] = s1[j]

    seq = pl.BlockSpec((RW_CHUNK, hb * HEAD), lambda i, n: (n, i))
    par = pl.BlockSpec((1, hb * HEAD), lambda i, n: (0, i))
    st = pl.BlockSpec((hb, 1, HEAD, HEAD), lambda i, n: (i, n, 0, 0))
    return _pcall(body, grid=(h // hb, nc), in_specs=[seq] * 5 + [par] * 5, out_specs=[seq, st],
                  out_shape=[S((t, d), F32), S((h, nc, HEAD, HEAD), F32)], scratch_shapes=[pltpu.VMEM((hb, HEAD, HEAD), F32)],
                  name=name, semantics=("parallel", "arbitrary"), inputs=(r, k, v, logw, a, *hp), jobs=jobs)


def _rwkv_scan_bwd(name, r, k, v, logw, a, hp, states, dy, jobs=()):
    t, d = r.shape
    h, nc, hb = d // HEAD, t // RW_CHUNK, RW_HEADS

    def body(r_ref, k_ref, v_ref, w_ref, a_ref, kk_ref, ka_ref, rk_ref, gg_ref, gb_ref, st_ref, dy_ref,
             dr_ref, dk_ref, dv_ref, dw_ref, da_ref, dkk_ref, dka_ref, drk_ref, dgg_ref, dgb_ref, ds_scr):
        n = pl.program_id(1)

        @pl.when(n == 0)
        def _():
            ds_scr[...] = jnp.zeros_like(ds_scr)
            for ref in (dkk_ref, dka_ref, drk_ref, dgg_ref, dgb_ref):
                ref[...] = jnp.zeros_like(ref)

        ins = [[st_ref[j, 0] for j in range(hb)]] + [_head_cols(ref, hb) for ref in (r_ref, k_ref, v_ref, w_ref, a_ref, kk_ref,
                                                                                  ka_ref, rk_ref, gg_ref, gb_ref)]
        _, vjp = jax.vjp(_rwkv_chunk, *ins)
        ds0, *dseq, dkk, dka, drk, dgg, dgb = vjp((_head_cols(dy_ref, hb), [ds_scr[j] for j in range(hb)]))
        for j in range(hb):
            ds_scr[j] = ds0[j]
        for ref, val in zip((dr_ref, dk_ref, dv_ref, dw_ref, da_ref), dseq):
            ref[...] = jnp.concatenate(val, axis=1)
        for ref, val in ((dkk_ref, dkk), (dka_ref, dka), (drk_ref, drk), (dgg_ref, dgg), (dgb_ref, dgb)):
            ref[...] += jnp.concatenate(val, axis=1)

    seq = pl.BlockSpec((RW_CHUNK, hb * HEAD), lambda i, n: (nc - 1 - n, i))
    par = pl.BlockSpec((1, hb * HEAD), lambda i, n: (0, i))
    st = pl.BlockSpec((hb, 1, HEAD, HEAD), lambda i, n: (i, nc - 1 - n, 0, 0))
    return _pcall(body, grid=(h // hb, nc), in_specs=[seq] * 5 + [par] * 5 + [st, seq], out_specs=[seq] * 5 + [par] * 5,
                  out_shape=[S((t, d), F32)] * 5 + [S((1, d), F32)] * 5, scratch_shapes=[pltpu.VMEM((hb, HEAD, HEAD), F32)],
                  name=name, semantics=("parallel", "arbitrary"), inputs=(r, k, v, logw, a, *hp, states, dy), jobs=jobs)


def _loss_head(name, x, target, g):
    d = x.shape[1]

    def fn(x_, t_, g_):
        def f(xx, gg):
            err = _rms(xx, gg) - t_
            return 0.5 * jnp.sum(jnp.mean(err * err, axis=-1, keepdims=True), axis=0, keepdims=True)

        l, vjp = jax.vjp(f, x_, g_)
        dx, dg = vjp(jnp.ones((1, 1), F32))
        return dx, dg, jnp.broadcast_to(l, (1, 128))

    return _rows(name, fn, [x, target], [g], [(d, F32)], [(1, d), (1, 128)], 256)


def _mod_fwd(name, cond_all, mod_w, mod_b_cols):
    l, d, n = mod_w.shape

    def body(c_ref, w_ref, b_ref, o_ref):
        o_ref[0] = _dot(jax.nn.silu(c_ref[...]), w_ref[0], hi=True) + b_ref[0]

    return pl.pallas_call(body, grid=(l,), in_specs=[pl.BlockSpec((N_DEV, d), lambda i: (0, 0)),
                                                      pl.BlockSpec((1, d, n), lambda i: (i, 0, 0)),
                                                      pl.BlockSpec((1, 1, n), lambda i: (i, 0, 0))],
                          out_specs=pl.BlockSpec((1, N_DEV, n), lambda i: (i, 0, 0)), out_shape=S((l, N_DEV, n), F32),
                          name=name, compiler_params=_params(("parallel",)))(cond_all, mod_w, mod_b_cols)


def _mod_bwd(name, cond_all, dmod_cols, dmod_all):
    l, _, n = dmod_cols.shape
    d = cond_all.shape[1]
    nb = dmod_all.shape[2]

    def body(c_ref, dc_ref, da_ref, gw_ref, gb_ref):
        gw_ref[0] = _dot(jax.nn.silu(c_ref[...]), dc_ref[0], "tn", hi=True)
        acc = da_ref[0, 0:1, :]
        for bi in range(1, N_DEV):
            acc = acc + da_ref[0, bi:bi + 1, :]
        gb_ref[0] = acc

    return pl.pallas_call(body, grid=(l,), in_specs=[pl.BlockSpec((N_DEV, d), lambda i: (0, 0)),
                                                      pl.BlockSpec((1, N_DEV, n), lambda i: (i, 0, 0)),
                                                      pl.BlockSpec((1, N_DEV, nb), lambda i: (i, 0, 0))],
                          out_specs=[pl.BlockSpec((1, d, n), lambda i: (i, 0, 0)), pl.BlockSpec((1, 1, nb), lambda i: (i, 0, 0))],
                          out_shape=[S((l, d, n), F32), S((l, 1, nb), F32)], name=name,
                          compiler_params=_params(("parallel",)))(cond_all, dmod_cols, dmod_all)


def _shift_up(a, n=1):
    return jnp.concatenate([a[n:], jnp.zeros_like(a[:n])], axis=0)


def _cols_full(g):
    return g.transpose(1, 0, 2).reshape(g.shape[1], -1)


def _cols_parts(full):
    r, n = full.shape
    return full.reshape(r, N_DEV, n // N_DEV).transpose(1, 0, 2)


def _pack(arrs, mult=1024):
    flat = jnp.concatenate([a.reshape(-1) for a in arrs])
    pad = (-flat.shape[0]) % mult
    return jnp.pad(flat, (0, pad)).reshape(-1, 128)


def _unpack(flat, shapes):
    out, o = [], 0
    for s in shapes:
        n = math.prod(s)
        out.append(flat[o:o + n].reshape(s))
        o += n
    return out


def _local_step(x, pos, target, mods, norm_g, final_norm_g, layer_weights, hooks=None, on_grads=None):
    t, d = x.shape
    hooks = hooks or {}
    jobs = lambda nm: hooks.get(nm, ())
    notify = on_grads or (lambda *a: None)
    kinds = [i % 3 for i in range(DEPTH)]
    inv_freq = (ROPE_THETA ** (-jnp.arange(HEAD // 2, dtype=F32) / (HEAD // 2)))
    inv_freq = jnp.tile(inv_freq, 128 // (HEAD // 2)).reshape(1, 128)
    saved = []
    for i, kind in enumerate(kinds):
        lw = layer_weights(i)
        shift, scale, gate = (mods[i, q * d:(q + 1) * d].reshape(1, d) for q in range(3))
        g = norm_g[i].reshape(1, d)
        h = _norm_fwd(f"norm_fwd{i}", x, g, shift, scale)
        sv = dict(x=x, h=h, g=g, shift=shift, scale=scale, gate=gate, lw=lw)
        if kind == 0:
            p = _mm(f"sg_in{i}", h, lw["w_in"], "nn", F32, jobs=jobs(f"sg_in{i}"))
            mix = _sg_fwd(f"sg_mix{i}", p, lw["ln_g"], lw["ln_b"], lw["w_s"], lw["b_st"])
            sv.update(p=p)
        elif kind == 1:
            p = _mm(f"swa_in{i}", h, lw["w_in"], "nn", F32, jobs=jobs(f"swa_in{i}"))
            q, k, v = _swa_pre(f"swa_pre{i}", p, pos, inv_freq, d)
            o = _swa_attn_fwd(f"swa_attn{i}", q, k, v, lw["sinks"], jobs=jobs(f"swa_attn{i}"))
            mix = _gate_fwd(f"swa_gate{i}", o, p, d + 2 * SWA_KV * HEAD, d)
            sv.update(p=p, qkv=(q, k, v), o=o)
        else:
            pm = _mm(f"rw_in{i}", h, lw["w_main"], "nn", F32)
            plo = _mm(f"rw_inl{i}", h, lw["w_lorain"], "nn", F32)
            r, k, v, z = _lerp_fwd(f"rw_lerp{i}", pm, lw["mu_main"], [d] * 4)
            (pll,) = _lerp_fwd(f"rw_lerpl{i}", plo, lw["mu_lora"], [LORA_PAD])
            logw, a = _lora_fwd(f"rw_lora{i}", pll, lw["w0"], lw["w_lora"], lw["a0"], lw["a_lora"])
            seqs = (r, k, v, logw, a)
            o, states = _rwkv_scan_fwd(f"rw_scan{i}", *seqs, lw["hp"], jobs=jobs(f"rw_scan{i}"))
            mix = _gate_fwd(f"rw_gate{i}", o, z, 0, d)
            sv.update(pm=pm, plo=plo, pll=pll, z=z, seqs=seqs, states=states, o=o)
        y = _mm(f"out{i}", mix, lw["w_out"], "nn", F32, jobs=jobs(f"out{i}"))
        sv.update(mix=mix, y=y)
        saved.append(sv)
        x = _resid_fwd(f"resid{i}", x, y, gate)

    dx, d_final_g, loss = _loss_head("loss_head", x, target, final_norm_g.reshape(1, d))

    grads = dict(norm_g=[None] * DEPTH, sg_w_in=[None] * 2, sg_w_out=[None] * 2, sg_ln_g=[None] * 2, sg_ln_b=[None] * 2,
                 sg_w_s=[None] * 2, sg_b_st=[None] * 2, final_norm_g=d_final_g)
    dmods = [None] * DEPTH
    for i in reversed(range(DEPTH)):
        kind, j, sv = kinds[i], i // 3, saved[i]
        lw = sv["lw"]
        dy, dgate = _resid_bwd(f"resid_bwd{i}", dx, sv["y"], sv["gate"])
        d_w_out = _mm(f"out_dw{i}", sv["mix"], dy, "tn", F32)
        if kind == 0:
            grads["sg_w_out"][j] = d_w_out
        else:
            grads[("swa_w_out", "rw_w_out")[kind - 1]] = d_w_out
        notify(i, "out", grads)
        dmix = _mm(f"out_dx{i}", dy, lw["w_out"], "nt", F32, jobs=jobs(f"out_dx{i}"))
        if kind == 0:
            dp, dlg, dlb, dws, dbs = _sg_bwd(f"sg_mix_bwd{i}", sv["p"], dmix, lw["ln_g"], lw["ln_b"], lw["w_s"], lw["b_st"],
                                            jobs=jobs(f"sg_mix_bwd{i}"))
            grads["sg_ln_g"][j], grads["sg_ln_b"][j], grads["sg_w_s"][j], grads["sg_b_st"][j] = dlg, dlb, dws, dbs
            grads["sg_w_in"][j] = _mm(f"sg_in_dw{i}", sv["h"], dp, "tn", F32, jobs=jobs(f"sg_in_dw{i}"))
            notify(i, "in", grads)
            dh = _mm(f"sg_in_dx{i}", dp, lw["w_in"], "nt", F32, jobs=jobs(f"sg_in_dx{i}"))
        elif kind == 1:
            z_off = d + 2 * SWA_KV * HEAD
            do, dz = _gate_bwd(f"swa_gate_bwd{i}", sv["o"], sv["p"], z_off, d, dmix)
            dq, dkc, dkp, dvc, dvp, dsinks = _swa_attn_bwd(f"swa_attn_bwd{i}", *sv["qkv"], lw["sinks"], do,
                                                           jobs=jobs(f"swa_attn_bwd{i}"))
            dkp, dvp = _shift_up(dkp, SWA_BLOCK), _shift_up(dvp, SWA_BLOCK)
            dp = _swa_post_bwd(f"swa_post_bwd{i}", dq, dkc, dkp, dvc, dvp, dz, pos, inv_freq)
            grads.update(swa_sinks=dsinks, swa_w_out=d_w_out)
            grads["swa_w_in"] = _mm(f"swa_in_dw{i}", sv["h"], dp, "tn", F32)
            dh = _mm(f"swa_in_dx{i}", dp, lw["w_in"], "nt", F32)
        else:
            do, dz = _gate_bwd(f"rw_gate_bwd{i}", sv["o"], sv["z"], 0, d, dmix)
            res = _rwkv_scan_bwd(f"rw_scan_bwd{i}", *sv["seqs"], lw["hp"], sv["states"], do, jobs=jobs(f"rw_scan_bwd{i}"))
            dr, dk, dv, dlogw, da = res[:5]
            dpll, dw0, dwl, da0, dal = _lora_bwd(f"rw_lora_bwd{i}", sv["pll"], dlogw, da, lw["w0"], lw["w_lora"],
                                                  lw["a0"], lw["a_lora"])
            dpm, dmu_main = _lerp_bwd(f"rw_lerp_bwd{i}", [dr, dk, dv, dz], sv["pm"], lw["mu_main"])
            dpl, dmu_lora = _lerp_bwd(f"rw_lerpl_bwd{i}", [dpll], sv["plo"], lw["mu_lora"])
            grads.update(rw_w_out=d_w_out, rw_hp=res[5:], rw_w0=dw0, rw_w_lora=dwl, rw_a0=da0, rw_a_lora=dal,
                         rw_mu_main=dmu_main, rw_mu_lora=dmu_lora)
            grads["rw_w_main"] = _mm(f"rw_in_dw{i}", sv["h"], dpm, "tn", F32)
            grads["rw_w_lorain"] = _mm(f"rw_inl_dw{i}", sv["h"], dpl, "tn", F32)
            dh = _mm(f"rw_inl_dx{i}", dpl, lw["w_lorain"], "nt", F32)
            dh = _mm(f"rw_in_dx{i}", dpm, lw["w_main"], "nt", F32, add=dh)
        if kind != 0:
            notify(i, "in", grads)
        dx, dg, dshift, dscale = _norm_bwd(f"norm_bwd{i}", sv["x"], dh, dx, sv["g"], sv["shift"], sv["scale"])
        grads["norm_g"][i] = dg
        dmods[i] = jnp.concatenate([dshift, dscale, dgate], axis=1)
    return loss, dx, jnp.concatenate(dmods, axis=0), grads


def kernel(x, c, positions, norm_g, mod_w, mod_b, final_norm_g, sg_w_in, sg_w_out, sg_ln_g, sg_ln_b, sg_w_spatial, sg_b_spatial, swa_w_in, swa_w_out, swa_sinks, rwkv_w_in, rwkv_w_out, rwkv_mu, rwkv_w0, rwkv_w_lora, rwkv_a0, rwkv_a_lora, rwkv_k_k, rwkv_k_a, rwkv_r_k, rwkv_gn_g, rwkv_gn_b, loss_target, m_norm_g, m_mod_w, m_mod_b, m_final_norm_g, m_sg_w_in, m_sg_w_out, m_sg_ln_g, m_sg_ln_b, m_sg_w_spatial, m_sg_b_spatial, m_swa_w_in, m_swa_w_out, m_swa_sinks, m_rwkv_w_in, m_rwkv_w_out, m_rwkv_mu, m_rwkv_w0, m_rwkv_w_lora, m_rwkv_a0, m_rwkv_a_lora, m_rwkv_k_k, m_rwkv_k_a, m_rwkv_r_k, m_rwkv_gn_g, m_rwkv_gn_b, v_norm_g, v_mod_w, v_mod_b, v_final_norm_g, v_sg_w_in, v_sg_w_out, v_sg_ln_g, v_sg_ln_b, v_sg_w_spatial, v_sg_b_spatial, v_swa_w_in, v_swa_w_out, v_swa_sinks, v_rwkv_w_in, v_rwkv_w_out, v_rwkv_mu, v_rwkv_w0, v_rwkv_w_lora, v_rwkv_a0, v_rwkv_a_lora, v_rwkv_k_k, v_rwkv_k_a, v_rwkv_r_k, v_rwkv_gn_g, v_rwkv_gn_b):
    weights = dict(norm_g=norm_g, mod_w=mod_w, mod_b=mod_b, final_norm_g=final_norm_g, sg_w_in=sg_w_in, sg_w_out=sg_w_out,
                   sg_ln_g=sg_ln_g, sg_ln_b=sg_ln_b, sg_w_spatial=sg_w_spatial, sg_b_spatial=sg_b_spatial, swa_w_in=swa_w_in,
                   swa_w_out=swa_w_out, swa_sinks=swa_sinks, rwkv_w_in=rwkv_w_in, rwkv_w_out=rwkv_w_out, rwkv_mu=rwkv_mu,
                   rwkv_w0=rwkv_w0, rwkv_w_lora=rwkv_w_lora, rwkv_a0=rwkv_a0, rwkv_a_lora=rwkv_a_lora, rwkv_k_k=rwkv_k_k,
                   rwkv_k_a=rwkv_k_a, rwkv_r_k=rwkv_r_k, rwkv_gn_g=rwkv_gn_g, rwkv_gn_b=rwkv_gn_b)
    mom_m = dict(norm_g=m_norm_g, mod_w=m_mod_w, mod_b=m_mod_b, final_norm_g=m_final_norm_g, sg_w_in=m_sg_w_in,
                 sg_w_out=m_sg_w_out, sg_ln_g=m_sg_ln_g, sg_ln_b=m_sg_ln_b, sg_w_spatial=m_sg_w_spatial,
                 sg_b_spatial=m_sg_b_spatial, swa_w_in=m_swa_w_in, swa_w_out=m_swa_w_out, swa_sinks=m_swa_sinks,
                 rwkv_w_in=m_rwkv_w_in, rwkv_w_out=m_rwkv_w_out, rwkv_mu=m_rwkv_mu, rwkv_w0=m_rwkv_w0,
                 rwkv_w_lora=m_rwkv_w_lora, rwkv_a0=m_rwkv_a0, rwkv_a_lora=m_rwkv_a_lora, rwkv_k_k=m_rwkv_k_k,
                 rwkv_k_a=m_rwkv_k_a, rwkv_r_k=m_rwkv_r_k, rwkv_gn_g=m_rwkv_gn_g, rwkv_gn_b=m_rwkv_gn_b)
    mom_v = dict(norm_g=v_norm_g, mod_w=v_mod_w, mod_b=v_mod_b, final_norm_g=v_final_norm_g, sg_w_in=v_sg_w_in,
                 sg_w_out=v_sg_w_out, sg_ln_g=v_sg_ln_g, sg_ln_b=v_sg_ln_b, sg_w_spatial=v_sg_w_spatial,
                 sg_b_spatial=v_sg_b_spatial, swa_w_in=v_swa_w_in, swa_w_out=v_swa_w_out, swa_sinks=v_swa_sinks,
                 rwkv_w_in=v_rwkv_w_in, rwkv_w_out=v_rwkv_w_out, rwkv_mu=v_rwkv_mu, rwkv_w0=v_rwkv_w0,
                 rwkv_w_lora=v_rwkv_w_lora, rwkv_a0=v_rwkv_a0, rwkv_a_lora=v_rwkv_a_lora, rwkv_k_k=v_rwkv_k_k,
                 rwkv_k_a=v_rwkv_k_a, rwkv_r_k=v_rwkv_r_k, rwkv_gn_g=v_rwkv_gn_g, rwkv_gn_b=v_rwkv_gn_b)
    names = list(weights)
    t, d = x.shape[1], x.shape[2]
    me = 4 * lax.axis_index("x") + 2 * lax.axis_index("y") + lax.axis_index("c")
    n_mod = mod_w.shape[2]
    n_rw = rwkv_w_in.shape[2]

    small_names = ["sg_ln_g", "sg_ln_b", "rwkv_mu", "rwkv_w0", "rwkv_a0", "rwkv_k_k", "rwkv_k_a", "rwkv_gn_g", "rwkv_gn_b",
                   "rwkv_w_lora", "rwkv_a_lora"]
    small_shapes = [weights[n].shape for n in small_names]
    pk = _pack([c] + [weights[n] for n in small_names])
    gathered = _gather("gather_small", pk).reshape(N_DEV, -1)
    c_all = gathered[:, :d]
    per_dev = [_unpack(gathered[dv, d:], small_shapes) for dv in range(N_DEV)]
    full_small = {}
    for q, n in enumerate(small_names):
        full_small[n] = jnp.concatenate([per_dev[dv][q] for dv in range(N_DEV)], axis=-1)

    mod_b_cols = lax.dynamic_slice_in_dim(mod_b, me * n_mod, n_mod, axis=1).reshape(DEPTH, 1, n_mod)
    mod_part = _mod_fwd("mod_fwd", c_all, mod_w, mod_b_cols)
    mod_g = _gather("gather_mod", mod_part.reshape(DEPTH * N_DEV, n_mod))
    mod_g = mod_g.reshape(N_DEV, DEPTH, N_DEV, n_mod)
    mods = lax.dynamic_index_in_dim(mod_g, me, axis=2, keepdims=False)
    mods = mods.transpose(1, 0, 2).reshape(DEPTH, N_DEV * n_mod)

    job = lambda cls, src: dict(cls=cls, src=src)
    gj = dict(swa_in=job(_Gather, swa_w_in[0].astype(BF16)), swa_out=job(_Gather, swa_w_out[0].astype(BF16)),
              rw_in=job(_Gather, rwkv_w_in[0].astype(BF16)), rw_out=job(_Gather, rwkv_w_out[0].astype(BF16)),
              sg_in1=job(_Gather, sg_w_in[1].astype(BF16)), sg_out1=job(_Gather, sg_w_out[1].astype(BF16)))
    hooks = {"sg_in0": [gj["swa_in"]], "out0": [gj["swa_out"]], "swa_in1": [gj["rw_out"]], "swa_attn1": [gj["rw_in"]],
             "rw_scan2": [gj["sg_in1"], gj["sg_out1"]]}
    g_sg_in0 = _gather("gather_sg_in0", sg_w_in[0].astype(BF16))
    g_sg_out0 = _gather("gather_sg_out0", sg_w_out[0].astype(BF16))
    lora_rows = lambda w, off: jnp.zeros((LORA_PAD, d), F32).at[off:off + LORA].set(w)
    mu = full_small["rwkv_mu"].reshape(1, -1)
    heads = lambda a: a.reshape(1, -1)

    def layer_weights(i):
        if i % 3 == 0:
            j = i // 3
            g_in, g_out = (g_sg_in0, g_sg_out0) if j == 0 else (gj["sg_in1"]["out"], gj["sg_out1"]["out"])
            return dict(w_in=_cols_full(g_in), w_out=g_out.reshape(d, d), ln_g=full_small["sg_ln_g"][j].reshape(1, d),
                        ln_b=full_small["sg_ln_b"][j].reshape(1, d), w_s=sg_w_spatial[j], b_st=sg_b_spatial[j].T)
        if i % 3 == 1:
            return dict(w_in=_cols_full(gj["swa_in"]["out"]), w_out=gj["swa_out"]["out"].reshape(d, d),
                        sinks=swa_sinks.reshape(SWA_KV, SWA_REP, 1, 1))
        rw_in_full = _cols_full(gj["rw_in"]["out"])
        return dict(w_main=rw_in_full[:, :4 * d], w_lorain=jnp.pad(rw_in_full[:, 4 * d:], ((0, 0), (0, LORA_PAD - 2 * LORA))),
                    w_out=gj["rw_out"]["out"].reshape(d, d), mu_main=mu[:, :4 * d],
                    mu_lora=jnp.pad(mu[:, 4 * d:], ((0, 0), (0, LORA_PAD - 2 * LORA))),
                    w0=full_small["rwkv_w0"], a0=full_small["rwkv_a0"],
                    w_lora=lora_rows(full_small["rwkv_w_lora"][0], 0), a_lora=lora_rows(full_small["rwkv_a_lora"][0], LORA),
                    hp=[heads(full_small["rwkv_k_k"]), heads(full_small["rwkv_k_a"]), heads(rwkv_r_k),
                        heads(full_small["rwkv_gn_g"]), heads(full_small["rwkv_gn_b"])])

    sj = {}

    def on_grads(i, which, g):
        def stage(nm, parts, host):
            sj[nm] = job(_Chips, _scatter_pairs("scatter_" + nm, parts.astype(BF16)))
            hooks.setdefault(host, []).append(sj[nm])

        rows_of = lambda a: a.reshape(N_DEV, -1, d)
        if (i, which) == (3, "out"):
            stage("sg_out1", rows_of(g["sg_w_out"][1]), "rw_scan_bwd2")
        elif (i, which) == (3, "in"):
            stage("sg_in1", _cols_parts(g["sg_w_in"][1]), "rw_scan_bwd2")
        elif (i, which) == (2, "out"):
            stage("rw_out", rows_of(g["rw_w_out"]), "swa_attn_bwd1")
        elif (i, which) == (2, "in"):
            d_rw_in = jnp.concatenate([g["rw_w_main"], g["rw_w_lorain"][:, :2 * LORA]], axis=1)
            stage("rw_in", _cols_parts(d_rw_in), "swa_attn_bwd1")
        elif (i, which) == (1, "out"):
            stage("swa_out", rows_of(g["swa_w_out"]), "out_dx0")
        elif (i, which) == (1, "in"):
            stage("swa_in", _cols_parts(g["swa_w_in"]), "sg_mix_bwd0")
        elif (i, which) == (0, "out"):
            stage("sg_out0", rows_of(g["sg_w_out"][0]), "sg_in_dw0")
        else:
            stage("sg_in0", _cols_parts(g["sg_w_in"][0]), "sg_in_dx0")

    loss, dx, dmods, g = _local_step(x[0], positions.reshape(t, 1).astype(F32), loss_target[0], mods, norm_g, final_norm_g,
                                     layer_weights, hooks, on_grads)

    dmod_g = _gather("gather_dmod", dmods)
    dmod_all = dmod_g.transpose(1, 0, 2)
    dmod_cols = lax.dynamic_slice_in_dim(dmod_all, me * n_mod, n_mod, axis=2)
    g_mod_w, g_mod_b = _mod_bwd("mod_bwd", c_all, dmod_cols, dmod_all)

    d_b_sp = [g["sg_b_st"][j].T for j in range(2)]
    rep = [loss[0, :1], jnp.concatenate(g["norm_g"], axis=0), g["final_norm_g"], jnp.stack(g["sg_w_s"]), jnp.stack(d_b_sp),
           g["swa_sinks"], g["rw_hp"][2]]
    rep_shapes = [(1,), norm_g.shape, final_norm_g.shape, sg_w_spatial.shape, sg_b_spatial.shape, swa_sinks.shape, rwkv_r_k.shape]
    rep_sum = _sum_parts("sum_rep", _gather("gather_rep", _pack(rep, 128 * 256))).reshape(-1)
    loss_tot, g_norm_g, g_final, g_w_sp, g_b_sp, g_sinks, g_r_k = _unpack(rep_sum, rep_shapes)

    p_sg_in = jnp.concatenate([sj["sg_in0"]["out"], sj["sg_in1"]["out"]], axis=1)
    p_sg_out = jnp.concatenate([sj["sg_out0"]["out"], sj["sg_out1"]["out"]], axis=1)
    p_swa_in, p_swa_out, p_rw_in, p_rw_out = (sj[nm]["out"] for nm in ("swa_in", "swa_out", "rw_in", "rw_out"))
    d_mu = jnp.concatenate([g["rw_mu_main"], g["rw_mu_lora"][:, :2 * LORA]], axis=1)
    hp_flat = lambda a: a.reshape(1, -1)
    small_grads = dict(sg_ln_g=jnp.concatenate(g["sg_ln_g"], axis=0), sg_ln_b=jnp.concatenate(g["sg_ln_b"], axis=0), rwkv_mu=d_mu,
                       rwkv_w0=g["rw_w0"], rwkv_a0=g["rw_a0"], rwkv_k_k=hp_flat(g["rw_hp"][0]), rwkv_k_a=hp_flat(g["rw_hp"][1]),
                       rwkv_gn_g=hp_flat(g["rw_hp"][3]), rwkv_gn_b=hp_flat(g["rw_hp"][4]),
                       rwkv_w_lora=g["rw_w_lora"][None, :LORA], rwkv_a_lora=g["rw_a_lora"][None, LORA:2 * LORA])
    per_dest = []
    for dv in range(N_DEV):
        shards = []
        for n in small_names:
            full, w = small_grads[n], weights[n].shape[-1]
            shards.append(full[..., dv * w:(dv + 1) * w])
        per_dest.append(_pack(shards))
    small_parts = _exchange("scatter_small", jnp.stack(per_dest), True)

    out_g, out_d, out_m, out_v = {}, {}, {}, {}

    def update(name, grad, shape2d, jobs=()):
        w2, m2, v2 = (a[name].reshape(shape2d) for a in (weights, mom_m, mom_v))
        gg, dd, mm, vv = _adamw("adamw_" + name, w2, grad, m2, v2, jobs=jobs)
        shp = weights[name].shape
        out_g[name], out_d[name], out_m[name], out_v[name] = gg.reshape(shp), dd.reshape(shp), mm.reshape(shp), vv.reshape(shp)

    update("mod_w", g_mod_w.reshape(-1, n_mod), (-1, n_mod))
    update("sg_w_in", p_sg_in, (-1, sg_w_in.shape[2]))
    update("sg_w_out", p_sg_out, (-1, d))
    update("swa_w_in", p_swa_in, (-1, swa_w_in.shape[2]))
    update("swa_w_out", p_swa_out, (-1, d))
    update("rwkv_w_in", p_rw_in, (-1, n_rw))
    update("rwkv_w_out", p_rw_out, (-1, d))
    update("sg_w_spatial", g_w_sp.reshape(-1, 128), (-1, 128))
    w_pk, m_pk, v_pk = (_pack([a[n] for n in small_names]) for a in (weights, mom_m, mom_v))
    res = _adamw("adamw_small", w_pk, small_parts, m_pk, v_pk)
    for q, arrs in enumerate(zip(*[_unpack(r_.reshape(-1), small_shapes) for r_ in res])):
        out_g[small_names[q]], out_d[small_names[q]], out_m[small_names[q]], out_v[small_names[q]] = arrs
    rep_names = ["norm_g", "mod_b", "final_norm_g", "sg_b_spatial", "swa_sinks", "rwkv_r_k"]
    rep_grads = [g_norm_g, g_mod_b.reshape(mod_b.shape), g_final, g_b_sp, g_sinks, g_r_k]
    rep_shapes2 = [weights[n].shape for n in rep_names]
    w_pk, m_pk, v_pk = (_pack([a[n] for n in rep_names]) for a in (weights, mom_m, mom_v))
    res = _adamw("adamw_rep", w_pk, _pack(rep_grads), m_pk, v_pk)
    for q, arrs in enumerate(zip(*[_unpack(r_.reshape(-1), rep_shapes2) for r_ in res])):
        out_g[rep_names[q]], out_d[rep_names[q]], out_m[rep_names[q]], out_v[rep_names[q]] = arrs

    return (loss_tot.reshape(()), dx[None], *[out_g[n] for n in names], *[out_d[n] for n in names],
            *[out_m[n] for n in names], *[out_v[n] for n in names])
```

```python
import functools
import math

import jax
import jax.numpy as jnp
from jax import lax
from jax.experimental import pallas as pl
from jax.experimental.pallas import tpu as pltpu

F32, BF16 = jnp.float32, jnp.bfloat16
HI = lax.Precision.HIGHEST
S = jax.ShapeDtypeStruct
MESH = pl.DeviceIdType.MESH

N_DEV = 8
DEPTH = 4
HEAD = 64
SG_GROUPS = 16
SG_CHUNK = 128
SWA_BLOCK = 128
SWA_KV = 4
SWA_REP = 8
ROPE_THETA = 10000.0
LORA = 96
LORA_PAD = 256
RW_CHUNK = 64
RW_HEADS = 16
RW_PREC = lax.Precision.HIGH
DECAY_SCALE = math.exp(-0.5)
GN_EPS = 64e-5
RMS_EPS = 1e-6
LN_EPS = 1e-5
NEG = -1e30
ADAM_LR, ADAM_B1, ADAM_B2, ADAM_EPS, ADAM_WD, ADAM_STEP = 0.001, 0.9, 0.999, 1e-08, 0.01, 10
VMEM_MB = 56


def _params(sem=None):
    kw = dict(vmem_limit_bytes=VMEM_MB << 20)
    if sem is not None:
        kw["dimension_semantics"] = sem
    return pltpu.CompilerParams(**kw)


def _pick(n, opts):
    for o in opts:
        if n % o == 0:
            return o
    raise ValueError(f"no tile for {n}")


def _rows(name, fn, rows, consts, out_rows, out_accs, tm, jobs=()):
    t = rows[0].shape[0]
    nr, nc, no = len(rows), len(consts), len(out_rows)

    def body(*refs):
        outs = fn(*[r[...] for r in refs[:nr + nc]])
        if not isinstance(outs, (tuple, list)):
            outs = (outs,)
        for r, o in zip(refs[nr + nc:nr + nc + no], outs[:no]):
            r[...] = o.astype(r.dtype)
        i = pl.program_id(0)
        for r, o in zip(refs[nr + nc + no:], outs[no:]):
            @pl.when(i == 0)
            def _(r=r, o=o):
                r[...] = o.astype(r.dtype)

            @pl.when(i > 0)
            def _(r=r, o=o):
                r[...] += o.astype(r.dtype)

    in_specs = [pl.BlockSpec((tm, a.shape[1]), lambda i: (i, 0)) for a in rows]
    in_specs += [pl.BlockSpec(c.shape, lambda i, nd=c.ndim: (0,) * nd) for c in consts]
    out_specs = [pl.BlockSpec((tm, n), lambda i: (i, 0)) for n, _ in out_rows]
    out_specs += [pl.BlockSpec(s, lambda i, nd=len(s): (0,) * nd) for s in out_accs]
    out_shape = [S((t, n), dt) for n, dt in out_rows] + [S(s, F32) for s in out_accs]
    return _pcall(body, grid=(t // tm,), in_specs=in_specs, out_specs=out_specs, out_shape=out_shape, name=name,
                  semantics=("arbitrary",), inputs=(*rows, *consts), jobs=jobs)


_DN = {"nn": (((1,), (0,)), ((), ())), "nt": (((1,), (1,)), ((), ())), "tn": (((0,), (0,)), ((), ()))}


def _mm(name, a, b, mode, out_dtype, add=None, resid=None, jobs=()):
    if mode == "nn":
        (m, k), (_, n) = a.shape, b.shape
    elif mode == "nt":
        (m, k), (n, _) = a.shape, b.shape
    else:
        (k, m), (_, n) = a.shape, b.shape
    wide = mode != "tn" and add is None and resid is None
    tm = _pick(m, (1024, 512, 256, 128))
    tn = _pick(n, ((1536,) if wide and n % 1024 else ()) + (1024, 512, 384, 256, 128))
    tk = _pick(k, ((4096,) if mode == "tn" else ()) + (2048, 1536, 1024, 512, 384, 256, 128))
    nk = k // tk
    n_extra = (add is not None) + 2 * (resid is not None)

    def body(*refs):
        a_ref, b_ref = refs[0], refs[1]
        extra, outs, acc = refs[2:2 + n_extra], refs[2 + n_extra:-1], refs[-1]
        kk = pl.program_id(2)
        prod = lax.dot_general(a_ref[...].astype(BF16), b_ref[...].astype(BF16), _DN[mode], preferred_element_type=F32)
        if add is not None:
            prod = jnp.where(kk == 0, prod + extra[0][...].astype(F32), prod) if nk > 1 else prod + extra[0][...].astype(F32)

        def finish(total):
            outs[0][...] = total.astype(outs[0].dtype)
            if resid is not None:
                outs[1][...] = extra[-2][...] + extra[-1][...] * total

        if nk == 1:
            finish(prod)
            return

        @pl.when(kk == 0)
        def _():
            acc[...] = prod

        @pl.when(kk > 0)
        def _():
            acc[...] += prod

        @pl.when(kk == nk - 1)
        def _():
            finish(acc[...])

    a_spec = pl.BlockSpec((tk, tm), lambda i, j, q: (q, i)) if mode == "tn" else pl.BlockSpec((tm, tk), lambda i, j, q: (i, q))
    b_spec = pl.BlockSpec((tn, tk), lambda i, j, q: (j, q)) if mode == "nt" else pl.BlockSpec((tk, tn), lambda i, j, q: (q, j))
    o_spec = pl.BlockSpec((tm, tn), lambda i, j, q: (i, j))
    ins, specs, out_specs, out_shape = [a, b], [a_spec, b_spec], [o_spec], [S((m, n), out_dtype)]
    if add is not None:
        ins.append(add)
        specs.append(o_spec)
    if resid is not None:
        ins += list(resid)
        specs += [o_spec, pl.BlockSpec((1, tn), lambda i, j, q: (0, j))]
        out_specs.append(o_spec)
        out_shape.append(S((m, n), F32))
    res = _pcall(body, grid=(m // tm, n // tn, nk), in_specs=specs, out_specs=out_specs, out_shape=out_shape,
                 scratch_shapes=[pltpu.VMEM((tm, tn), F32)], name=name, semantics=("parallel", "parallel", "arbitrary"),
                 inputs=ins, jobs=jobs)
    return res if resid is not None else res[0]


def _exchange(name, src, scatter):
    blk = src.shape[1:] if scatter else src.shape

    def body(src_ref, dst_ref, send_sems, recv_sems, loc_sem):
        x, y, c = lax.axis_index("x"), lax.axis_index("y"), lax.axis_index("c")
        me = 4 * x + 2 * y + c

        def mine(d):
            return src_ref.at[d] if scatter else src_ref

        local = pltpu.make_async_copy(mine(me), dst_ref.at[me], loc_sem)
        local.start()
        sends, peers = [], []
        for k in range(1, N_DEV):
            px = 1 - x if k & 4 else x
            py = 1 - y if k & 2 else y
            pc = 1 - c if k & 1 else c
            pid = 4 * px + 2 * py + pc
            cp = pltpu.make_async_remote_copy(src_ref=mine(pid), dst_ref=dst_ref.at[me], send_sem=send_sems.at[k - 1],
                                              recv_sem=recv_sems.at[k - 1], device_id=(px, py, pc), device_id_type=MESH)
            cp.start()
            sends.append(cp)
            peers.append((pid, (px, py, pc)))
        for k in range(1, N_DEV):
            pid, dev = peers[k - 1]
            pltpu.make_async_remote_copy(src_ref=mine(pid), dst_ref=dst_ref.at[pid], send_sem=send_sems.at[k - 1],
                                         recv_sem=recv_sems.at[k - 1], device_id=dev, device_id_type=MESH).wait_recv()
        for cp in sends:
            cp.wait_send()
        local.wait()

    return pl.pallas_call(
        body, out_shape=S((N_DEV,) + tuple(blk), src.dtype),
        in_specs=[pl.BlockSpec(memory_space=pl.ANY)], out_specs=pl.BlockSpec(memory_space=pl.ANY),
        scratch_shapes=[pltpu.SemaphoreType.DMA((N_DEV - 1,)), pltpu.SemaphoreType.DMA((N_DEV - 1,)),
                        pltpu.SemaphoreType.DMA],
        name=name)(src)


class _Gather:
    @staticmethod
    def out_shape(src):
        return S((N_DEV,) + tuple(src.shape), src.dtype)

    scratch = (pltpu.SemaphoreType.DMA((N_DEV - 1,)), pltpu.SemaphoreType.DMA((N_DEV - 1,)), pltpu.SemaphoreType.DMA)

    def __init__(self, src_ref, dst_ref, send_sems, recv_sems, loc_sem):
        self.refs = (src_ref, dst_ref, send_sems, recv_sems, loc_sem)
        x, y, c = lax.axis_index("x"), lax.axis_index("y"), lax.axis_index("c")
        self.c, self.me, self.sibling = c, (x, y, c), (x, y, 1 - c)
        self.chips = [(1 - x, y), (x, 1 - y), (1 - x, 1 - y)]

    def rows(self, px, py, pc):
        return self.refs[1].at[4 * px + 2 * py + pc]

    def copy(self, k, block, to, own=False):
        src_ref, _, send_sems, recv_sems, _ = self.refs
        return pltpu.make_async_remote_copy(src_ref=src_ref if own else self.rows(*block), dst_ref=self.rows(*block),
                                            send_sem=send_sems.at[k], recv_sem=recv_sems.at[k], device_id=to,
                                            device_id_type=MESH)

    def local(self):
        return pltpu.make_async_copy(self.refs[0], self.rows(*self.me), self.refs[4])

    def first(self):
        return [self.copy(0, self.me, self.sibling, own=True)] + [self.copy(1 + j, self.me, (*chip, self.c), own=True)
                                                                  for j, chip in enumerate(self.chips)]

    def start(self):
        self.local().start()
        for cp in self.first():
            cp.start()

    def finish(self):
        c = self.c
        passed = [self.copy(4 + j, (*chip, c), self.sibling) for j, chip in enumerate(self.chips)]
        for j, chip in enumerate(self.chips):
            self.copy(1 + j, (*chip, c), self.me).wait_recv()
            passed[j].start()
        self.copy(0, self.sibling, self.me).wait_recv()
        for j, chip in enumerate(self.chips):
            self.copy(4 + j, (*chip, 1 - c), self.me).wait_recv()
        for cp in self.first() + passed:
            cp.wait_send()
        self.local().wait()


class _Chips:
    @staticmethod
    def out_shape(src):
        return S(src.shape, src.dtype)

    scratch = (pltpu.SemaphoreType.DMA((N_DEV // 2 - 1,)), pltpu.SemaphoreType.DMA((N_DEV // 2 - 1,)), pltpu.SemaphoreType.DMA)

    def __init__(self, src_ref, dst_ref, send_sems, recv_sems, loc_sem):
        self.refs = (src_ref, dst_ref, send_sems, recv_sems, loc_sem)
        x, y, c = lax.axis_index("x"), lax.axis_index("y"), lax.axis_index("c")
        self.c, self.mine = c, 2 * x + y
        self.chips = [(1 - x, y), (x, 1 - y), (1 - x, 1 - y)]

    def local(self):
        src_ref, dst_ref, _, _, loc_sem = self.refs
        return pltpu.make_async_copy(src_ref.at[self.mine], dst_ref.at[self.mine], loc_sem)

    def send(self, j):
        src_ref, dst_ref, send_sems, recv_sems, _ = self.refs
        px, py = self.chips[j]
        return pltpu.make_async_remote_copy(src_ref=src_ref.at[2 * px + py], dst_ref=dst_ref.at[self.mine],
                                            send_sem=send_sems.at[j], recv_sem=recv_sems.at[j],
                                            device_id=(px, py, self.c), device_id_type=MESH)

    def arrival(self, j):
        src_ref, dst_ref, send_sems, recv_sems, _ = self.refs
        px, py = self.chips[j]
        return pltpu.make_async_remote_copy(src_ref=src_ref.at[self.mine], dst_ref=dst_ref.at[2 * px + py],
                                            send_sem=send_sems.at[j], recv_sem=recv_sems.at[j],
                                            device_id=(px, py, self.c), device_id_type=MESH)

    def start(self):
        self.local().start()
        for j in range(len(self.chips)):
            self.send(j).start()

    def finish(self):
        for j in range(len(self.chips)):
            self.arrival(j).wait_recv()
        for j in range(len(self.chips)):
            self.send(j).wait_send()
        self.local().wait()


def _exchange_call(name, cls, src):
    def body(*refs):
        ex = cls(*refs)
        ex.start()
        ex.finish()

    return pl.pallas_call(body, out_shape=cls.out_shape(src), in_specs=[pl.BlockSpec(memory_space=pl.ANY)],
                          out_specs=pl.BlockSpec(memory_space=pl.ANY), scratch_shapes=list(cls.scratch), name=name)(src)


def _gather(name, src):
    return _exchange_call(name, _Gather, src)


def _pcall(body, *, grid, in_specs, out_specs, out_shape, scratch_shapes=(), name, semantics, inputs, jobs=()):
    if not jobs:
        return pl.pallas_call(body, grid=grid, in_specs=in_specs, out_specs=out_specs, out_shape=out_shape,
                              scratch_shapes=list(scratch_shapes), name=name, compiler_params=_params(semantics))(*inputs)
    n_in, n_out, n_scr, nj = len(in_specs), len(out_specs), len(scratch_shapes), len(jobs)

    def hosted(*refs):
        ins, srcs = refs[:n_in], refs[n_in:n_in + nj]
        outs, dsts = refs[n_in + nj:n_in + nj + n_out], refs[n_in + nj + n_out:n_in + 2 * nj + n_out]
        scr, sems = refs[n_in + 2 * nj + n_out:n_in + 2 * nj + n_out + n_scr], refs[n_in + 2 * nj + n_out + n_scr:]
        ids = [pl.program_id(q) for q in range(len(grid))]
        first = functools.reduce(jnp.logical_and, [i == 0 for i in ids])
        last = functools.reduce(jnp.logical_and, [i == g - 1 for i, g in zip(ids, grid)])
        make = lambda q: jobs[q]["cls"](srcs[q], dsts[q], *sems[3 * q:3 * q + 3])

        @pl.when(first)
        def _():
            for q in range(nj):
                make(q).start()

        body(*ins, *outs, *scr)

        @pl.when(last)
        def _():
            for q in range(nj):
                make(q).finish()

    anyspec = pl.BlockSpec(memory_space=pl.ANY)
    res = pl.pallas_call(
        hosted, grid=grid, in_specs=list(in_specs) + [anyspec] * nj, out_specs=list(out_specs) + [anyspec] * nj,
        out_shape=list(out_shape) + [j["cls"].out_shape(j["src"]) for j in jobs],
        scratch_shapes=list(scratch_shapes) + [s for j in jobs for s in j["cls"].scratch], name=name,
        compiler_params=_params(("arbitrary",) * len(grid)))(*inputs, *[j["src"] for j in jobs])
    for j, out in zip(jobs, res[n_out:]):
        j["out"] = out
    return res[:n_out]


def _scatter_pairs(name, parts):
    _, r, c_ = parts.shape
    n_chip = N_DEV // 2

    def stage1(src_ref, dst_ref, send_sems, recv_sems):
        x, y, c = lax.axis_index("x"), lax.axis_index("y"), lax.axis_index("c")
        sends = []
        for q in range(n_chip):
            cp = pltpu.make_async_remote_copy(src_ref=src_ref.at[2 * q + 1 - c], dst_ref=dst_ref.at[q],
                                              send_sem=send_sems.at[q], recv_sem=recv_sems.at[q],
                                              device_id=(x, y, 1 - c), device_id_type=MESH)
            cp.start()
            sends.append(cp)
        for q in range(n_chip):
            pltpu.make_async_remote_copy(src_ref=src_ref.at[2 * q + c], dst_ref=dst_ref.at[q], send_sem=send_sems.at[q],
                                         recv_sem=recv_sems.at[q], device_id=(x, y, 1 - c), device_id_type=MESH).wait_recv()
        for cp in sends:
            cp.wait_send()

    from_sibling = pl.pallas_call(
        stage1, out_shape=S((n_chip, r, c_), parts.dtype),
        in_specs=[pl.BlockSpec(memory_space=pl.ANY)], out_specs=pl.BlockSpec(memory_space=pl.ANY),
        scratch_shapes=[pltpu.SemaphoreType.DMA((n_chip,)), pltpu.SemaphoreType.DMA((n_chip,))], name=name + "_pair")(parts)

    tm = _pick(r, (512, 256, 128, 64, 32, 16, 8)) if r % 8 == 0 else r
    core = lax.axis_index("c").astype(jnp.int32).reshape(1)

    def pair_sum(core_ref, mine_ref, sib_ref, o_ref):
        o_ref[...] = (mine_ref[0].astype(F32) + sib_ref[...].astype(F32)).astype(o_ref.dtype)

    pair = pl.pallas_call(
        pair_sum, out_shape=S((n_chip, r, c_), parts.dtype),
        grid_spec=pltpu.PrefetchScalarGridSpec(
            num_scalar_prefetch=1, grid=(n_chip, r // tm),
            in_specs=[pl.BlockSpec((1, 1, tm, c_), lambda q, i, core_ref: (q, core_ref[0], i, 0)),
                      pl.BlockSpec((1, tm, c_), lambda q, i, core_ref: (q, i, 0))],
            out_specs=pl.BlockSpec((1, tm, c_), lambda q, i, core_ref: (q, i, 0))),
        name=name + "_sum", compiler_params=_params(("parallel", "parallel")))(
            core, parts.reshape(n_chip, 2, r, c_), from_sibling)

    return pair


def _sum_parts(name, parts):
    n_parts, r, c = parts.shape
    tm = _pick(r, (512, 256, 128, 64, 32, 16, 8)) if r % 8 == 0 else r

    def body(p_ref, o_ref):
        acc = p_ref[0].astype(F32)
        for d in range(1, n_parts):
            acc = acc + p_ref[d].astype(F32)
        o_ref[...] = acc

    return pl.pallas_call(body, grid=(r // tm,), in_specs=[pl.BlockSpec((n_parts, tm, c), lambda i: (0, i, 0))],
                          out_specs=pl.BlockSpec((tm, c), lambda i: (i, 0)), out_shape=S((r, c), F32), name=name,
                          compiler_params=_params(("parallel",)))(parts)


def _adamw(name, w, g, m, v, jobs=()):
    r, c = w.shape
    parts = g.ndim == 3
    n_parts = g.shape[0] if parts else 1
    tm = _pick(r, (256, 128, 64, 32, 16, 8)) if r % 8 == 0 else r

    def body(w_ref, g_ref, m_ref, v_ref, go_ref, d_ref, mo_ref, vo_ref):
        if parts:
            gg = g_ref[0].astype(F32)
            for d in range(1, n_parts):
                gg = gg + g_ref[d].astype(F32)
        else:
            gg = g_ref[...]
        mm = ADAM_B1 * m_ref[...] + (1.0 - ADAM_B1) * gg
        vv = ADAM_B2 * v_ref[...] + (1.0 - ADAM_B2) * jnp.square(gg)
        m_hat = mm / (1.0 - ADAM_B1 ** ADAM_STEP)
        v_hat = vv / (1.0 - ADAM_B2 ** ADAM_STEP)
        go_ref[...] = gg
        d_ref[...] = -ADAM_LR * (m_hat / (jnp.sqrt(v_hat) + ADAM_EPS) + ADAM_WD * w_ref[...])
        mo_ref[...] = mm
        vo_ref[...] = vv

    spec = pl.BlockSpec((tm, c), lambda i: (i, 0))
    g_spec = pl.BlockSpec((n_parts, tm, c), lambda i: (0, i, 0)) if parts else spec
    return _pcall(body, grid=(r // tm,), in_specs=[spec, g_spec, spec, spec], out_specs=[spec] * 4,
                  out_shape=[S((r, c), F32)] * 4, name=name, semantics=("parallel",), inputs=(w, g, m, v), jobs=jobs)


def _rms(x, g):
    return x * lax.rsqrt(jnp.mean(x * x, axis=-1, keepdims=True) + RMS_EPS) * g


def _adaln(x, g, shift, scale):
    return _rms(x, g) * (1.0 + scale) + shift


def _dot(a, b, dn="nn", hi=False, prec=None):
    if hi or prec is not None:
        return lax.dot_general(a, b, _DN[dn], precision=HI if hi else prec, preferred_element_type=F32)
    return lax.dot_general(a.astype(BF16), b.astype(BF16), _DN[dn], preferred_element_type=F32)


def _sg_mix(p, ln_g, ln_b, w_s, b_st):
    d = p.shape[1] // 3
    gd = d // SG_GROUPS
    u = jax.nn.gelu(p[:, :d])
    vf = jax.nn.gelu(p[:, d:2 * d])
    z = p[:, 2 * d:]
    mean = jnp.mean(vf, axis=-1, keepdims=True)
    var = jnp.mean(jnp.square(vf - mean), axis=-1, keepdims=True)
    vn = (vf - mean) * lax.rsqrt(var + LN_EPS) * ln_g + ln_b
    row = lax.broadcasted_iota(jnp.int32, (SG_CHUNK, SG_CHUNK), 0)
    col = lax.broadcasted_iota(jnp.int32, (SG_CHUNK, SG_CHUNK), 1)
    fs = []
    for g in range(SG_GROUPS):
        w = jnp.where(row >= col, w_s[g], 0.0)
        fs.append(_dot(w, vn[:, g * gd:(g + 1) * gd]))
    sel = (lax.broadcasted_iota(jnp.int32, (SG_GROUPS, d), 1) // gd
           == lax.broadcasted_iota(jnp.int32, (SG_GROUPS, d), 0)).astype(F32)
    f = jnp.concatenate(fs, axis=1) + _dot(b_st, sel, hi=True)
    return u * f * jax.nn.silu(z)


def _rot_half(x):
    n = x.shape[1]
    lane = lax.broadcasted_iota(jnp.int32, x.shape, 1)
    return jnp.where(lane % HEAD < HEAD // 2, -pltpu.roll(x, n - HEAD // 2, 1), pltpu.roll(x, HEAD // 2, 1))


def _rope(x, cos, sin, sign):
    reps = x.shape[1] // cos.shape[1]
    return x * jnp.tile(cos, (1, reps)) + sign * _rot_half(x) * jnp.tile(sin, (1, reps))


def _attn_block(q, kp, kc, vp, vc, sink, prev_bias):
    each = lambda f, *ls: [f(*xs) for xs in zip(*ls)]
    r = sink[0].shape[0]
    scores = lambda a, b: (_dot(a, b, "nt") * (HEAD ** -0.5)).reshape(r, SWA_BLOCK, SWA_BLOCK)
    sp, sc = each(scores, q, kp), each(scores, q, kc)
    qi = lax.broadcasted_iota(jnp.int32, (r, SWA_BLOCK, SWA_BLOCK), 1)
    kj = lax.broadcasted_iota(jnp.int32, (r, SWA_BLOCK, SWA_BLOCK), 2)
    sp = each(lambda s: jnp.where(kj > qi, s, NEG) + prev_bias, sp)
    sc = each(lambda s: jnp.where(kj <= qi, s, NEG), sc)
    m = each(lambda a, b, s: jnp.maximum(jnp.maximum(jnp.max(a, axis=-1, keepdims=True), jnp.max(b, axis=-1, keepdims=True)), s),
             sp, sc, sink)
    ep, ec = each(lambda s, m_: jnp.exp(s - m_), sp, m), each(lambda s, m_: jnp.exp(s - m_), sc, m)
    denom = each(lambda a, b, s, m_: jnp.sum(a, axis=-1, keepdims=True) + jnp.sum(b, axis=-1, keepdims=True) + jnp.exp(s - m_),
                 ep, ec, sink, m)
    flat = lambda e, dn: (e / dn).reshape(r * SWA_BLOCK, SWA_BLOCK)
    pp, pc = each(flat, ep, denom), each(flat, ec, denom)
    return each(lambda a, va, b, vb: _dot(a, va) + _dot(b, vb), pp, vp, pc, vc)


def _attn_block_bwd(q, kp, kc, vp, vc, sink, do, prev_bias):
    each = lambda f, *ls: [f(*xs) for xs in zip(*ls)]
    r = sink[0].shape[0]
    scale = HEAD ** -0.5
    cube = lambda x: x.reshape(r, SWA_BLOCK, SWA_BLOCK)
    flat = lambda x: x.reshape(r * SWA_BLOCK, SWA_BLOCK)
    scores = lambda a, b: cube(_dot(a, b, "nt") * scale)
    sp, sc = each(scores, q, kp), each(scores, q, kc)
    qi = lax.broadcasted_iota(jnp.int32, (r, SWA_BLOCK, SWA_BLOCK), 1)
    kj = lax.broadcasted_iota(jnp.int32, (r, SWA_BLOCK, SWA_BLOCK), 2)
    sp = each(lambda s: jnp.where(kj > qi, s, NEG) + prev_bias, sp)
    sc = each(lambda s: jnp.where(kj <= qi, s, NEG), sc)
    m = each(lambda a, b, s: jnp.maximum(jnp.maximum(jnp.max(a, axis=-1, keepdims=True), jnp.max(b, axis=-1, keepdims=True)), s),
             sp, sc, sink)
    ep, ec = each(lambda s, m_: jnp.exp(s - m_), sp, m), each(lambda s, m_: jnp.exp(s - m_), sc, m)
    es = each(lambda s, m_: jnp.exp(s - m_), sink, m)
    denom = each(lambda a, b, e: jnp.sum(a, axis=-1, keepdims=True) + jnp.sum(b, axis=-1, keepdims=True) + e, ep, ec, es)
    pp, pc = each(lambda e, dn: e / dn, ep, denom), each(lambda e, dn: e / dn, ec, denom)
    dpp, dpc = each(lambda g, v_: cube(_dot(g, v_, "nt")), do, vp), each(lambda g, v_: cube(_dot(g, v_, "nt")), do, vc)
    dvp, dvc = each(lambda p, g: _dot(flat(p), g, "tn"), pp, do), each(lambda p, g: _dot(flat(p), g, "tn"), pc, do)
    delta = each(lambda a, da, b, db: jnp.sum(a * da, axis=-1, keepdims=True) + jnp.sum(b * db, axis=-1, keepdims=True),
                 pp, dpp, pc, dpc)
    dsp, dsc = each(lambda p, dp, dl: flat(p * (dp - dl)), pp, dpp, delta), each(lambda p, dp, dl: flat(p * (dp - dl)), pc, dpc, delta)
    dsink = each(lambda e, dn, dl: -jnp.sum(e / dn * dl, axis=1, keepdims=True), es, denom, delta)
    dq = each(lambda a, ka, b, kb: (_dot(a, ka) + _dot(b, kb)) * scale, dsp, kp, dsc, kc)
    dkp, dkc = each(lambda a, q_: _dot(a, q_, "tn") * scale, dsp, q), each(lambda a, q_: _dot(a, q_, "tn") * scale, dsc, q)
    return dq, dkp, dkc, dvp, dvc, dsink


def _rwkv_chunk(s0, r, k, v, logw, a, k_k, k_a, r_k, gn_g, gn_b):
    c = r[0].shape[0]
    each = lambda f, *ls: [f(*xs) for xs in zip(*ls)]
    gram = functools.partial(_dot, prec=RW_PREC)
    row = lax.broadcasted_iota(jnp.int32, (c, c), 0)
    col = lax.broadcasted_iota(jnp.int32, (c, c), 1)
    incl, strict = row >= col, row > col
    ones_l = incl.astype(F32)

    def unit(x):
        return x / jnp.maximum(jnp.sqrt(jnp.sum(x * x, axis=-1, keepdims=True)), 1e-12)

    kk = each(lambda k_, p: unit(k_ * p), k, k_k)
    km = each(lambda k_, a_, p: k_ * (1.0 + (a_ - 1.0) * p), k, a, k_a)
    b = each(lambda x, a_: x * a_, kk, a)
    first_half = lax.broadcasted_iota(jnp.int32, (c, HEAD), 0) < c // 2
    mid = each(lambda w: jnp.sum(jnp.where(first_half, w, 0.0), axis=0, keepdims=True), logw)
    cum = each(lambda w, m: _dot(ones_l, w, hi=True) - m, logw, mid)
    alpha = each(lambda x, cu, w: x * jnp.exp(cu - w), kk, cum, logw)
    beta = each(lambda x, cu: x * jnp.exp(-cu), b, cum)
    kap = each(lambda x, cu: x * jnp.exp(-cu), km, cum)
    rho = each(lambda x, cu: x * jnp.exp(cu), r, cum)
    s0 = each(lambda s, m: s * jnp.exp(m), s0, mid)
    lab = each(lambda x, y_: jnp.where(strict, gram(x, y_, "nt"), 0.0), alpha, beta)
    lak = each(lambda x, y_: jnp.where(strict, gram(x, y_, "nt"), 0.0), alpha, kap)
    xs = each(lambda al, s, l, v_: _dot(al, s, "nt") + _dot(l, v_), alpha, s0, lak, v)
    xs = each(lambda x, l: x - _dot(l, x), xs, lab)
    lp, power = lab, 2
    while power < c:
        lp = each(lambda l: _dot(l, l), lp)
        xs = each(lambda x, l: x + _dot(l, x), xs, lp)
        power *= 2
    u = each(lambda x: -x, xs)
    mrb = each(lambda x, y_: jnp.where(incl, gram(x, y_, "nt"), 0.0), rho, beta)
    mrk = each(lambda x, y_: jnp.where(incl, gram(x, y_, "nt"), 0.0), rho, kap)
    y = each(lambda rh, s, mb, u_, mk, v_: _dot(rh, s, "nt") + _dot(mb, u_) + _dot(mk, v_), rho, s0, mrb, u, mrk, v)
    s1 = each(lambda s, u_, be, v_, ka, w, m: (s + _dot(u_, be, "tn") + _dot(v_, ka, "tn"))
              * jnp.exp(jnp.sum(w, axis=0, keepdims=True) - m), s0, u, beta, v, kap, logw, mid)

    def finish(y_, g, bias, r_, km_, rk, v_):
        mean = jnp.mean(y_, axis=-1, keepdims=True)
        var = jnp.mean(jnp.square(y_ - mean), axis=-1, keepdims=True)
        y_ = (y_ - mean) * lax.rsqrt(var + GN_EPS) * g + bias
        return y_ + jnp.sum(r_ * km_ * rk, axis=-1, keepdims=True) * v_

    return each(finish, y, gn_g, gn_b, r, km, r_k, v), s1


def _norm_fwd(name, x, g, shift, scale):
    return _rows(name, lambda x_, g_, sh, sc: _adaln(x_, g_, sh, sc), [x], [g, shift, scale], [(x.shape[1], BF16)], [], 256)[0]


def _norm_bwd(name, x, dh, dx_res, g, shift, scale):
    d = x.shape[1]

    def fn(x_, dh_, dr_, g_, sh, sc):
        _, vjp = jax.vjp(_adaln, x_, g_, sh, sc)
        dx, dg, dsh, dsc = vjp(dh_)
        return dx + dr_, dg, dsh, dsc

    return _rows(name, fn, [x, dh, dx_res], [g, shift, scale], [(d, F32)], [(1, d)] * 3, 256)


def _resid_bwd(name, dx, y, gate):
    d = dx.shape[1]
    return _rows(name, lambda dx_, y_, g_: (g_ * dx_, jnp.sum(dx_ * y_, axis=0, keepdims=True)), [dx, y], [gate],
                 [(d, BF16)], [(1, d)], 256)


def _sg_fwd(name, p, ln_g, ln_b, w_s, b_st):
    d = p.shape[1] // 3
    return _rows(name, _sg_mix, [p], [ln_g, ln_b, w_s, b_st], [(d, BF16)], [], SG_CHUNK)[0]


def _sg_bwd(name, p, dmix, ln_g, ln_b, w_s, b_st, jobs=()):
    def fn(p_, dm_, lg, lb, ws, bs):
        _, vjp = jax.vjp(_sg_mix, p_, lg, lb, ws, bs)
        return vjp(dm_)

    return _rows(name, fn, [p, dmix], [ln_g, ln_b, w_s, b_st], [(p.shape[1], BF16)],
                 [ln_g.shape, ln_b.shape, w_s.shape, b_st.shape], SG_CHUNK, jobs=jobs)


def _rope_tables(pos, inv_freq):
    ang = pos * inv_freq
    return jnp.cos(ang), jnp.sin(ang)


def _swa_pre(name, p, pos, inv_freq, d):
    kvw = SWA_KV * HEAD

    def fn(p_, pos_, fr):
        cos, sin = _rope_tables(pos_, fr)
        return (_rope(p_[:, :d], cos, sin, 1.0), _rope(p_[:, d:d + kvw], cos, sin, 1.0), p_[:, d + kvw:d + 2 * kvw])

    return _rows(name, fn, [p, pos], [inv_freq], [(d, BF16), (kvw, BF16), (kvw, BF16)], [], 256)


def _q_groups(ref, kv, rep):
    heads = _head_cols(ref, kv * rep)
    return [jnp.concatenate(heads[g * rep:(g + 1) * rep], axis=0) for g in range(kv)]


def _q_ungroup(groups, rep):
    return jnp.concatenate([g[h * SWA_BLOCK:(h + 1) * SWA_BLOCK] for g in groups for h in range(rep)], axis=1)


def _swa_attn_fwd(name, q, k, v, sinks, jobs=()):
    t, d = q.shape
    kv, rep = sinks.shape[0], sinks.shape[1]
    nb = t // SWA_BLOCK

    def body(q_ref, kp_ref, kc_ref, vp_ref, vc_ref, s_ref, o_ref):
        prev_bias = jnp.where(pl.program_id(0) > 0, 0.0, NEG).astype(F32)
        o = _attn_block(_q_groups(q_ref, kv, rep), *[_head_cols(ref, kv) for ref in (kp_ref, kc_ref, vp_ref, vc_ref)],
                        [s_ref[g] for g in range(kv)], prev_bias)
        o_ref[...] = _q_ungroup(o, rep)

    qs = pl.BlockSpec((SWA_BLOCK, d), lambda n: (n, 0))
    cur = pl.BlockSpec((SWA_BLOCK, kv * HEAD), lambda n: (n, 0))
    prev = pl.BlockSpec((SWA_BLOCK, kv * HEAD), lambda n: (jnp.maximum(n - 1, 0), 0))
    ss = pl.BlockSpec(sinks.shape, lambda n: (0, 0, 0, 0))
    return _pcall(body, grid=(nb,), in_specs=[qs, prev, cur, prev, cur, ss], out_specs=[qs], out_shape=[S((t, d), F32)],
                  name=name, semantics=("parallel",), inputs=(q, k, k, v, v, sinks), jobs=jobs)[0]


def _swa_attn_bwd(name, q, k, v, sinks, do, jobs=()):
    t, d = q.shape
    kv, rep = sinks.shape[0], sinks.shape[1]
    nb = t // SWA_BLOCK

    def body(q_ref, kp_ref, kc_ref, vp_ref, vc_ref, s_ref, do_ref, dq_ref, dkc_ref, dkp_ref, dvc_ref, dvp_ref, ds_ref):
        n = pl.program_id(0)
        prev_bias = jnp.where(n > 0, 0.0, NEG).astype(F32)
        args = [_q_groups(q_ref, kv, rep)] + [_head_cols(ref, kv) for ref in (kp_ref, kc_ref, vp_ref, vc_ref)]
        dq, dkp, dkc, dvp, dvc, ds = _attn_block_bwd(*args, [s_ref[g] for g in range(kv)], _q_groups(do_ref, kv, rep), prev_bias)
        dq_ref[...] = _q_ungroup(dq, rep)
        for ref, val in ((dkc_ref, dkc), (dkp_ref, dkp), (dvc_ref, dvc), (dvp_ref, dvp)):
            ref[...] = jnp.concatenate(val, axis=1)

        @pl.when(n == 0)
        def _():
            ds_ref[...] = jnp.zeros_like(ds_ref)

        for g in range(kv):
            ds_ref[g] += ds[g]

    qs = pl.BlockSpec((SWA_BLOCK, d), lambda n: (n, 0))
    cur = pl.BlockSpec((SWA_BLOCK, kv * HEAD), lambda n: (n, 0))
    prev = pl.BlockSpec((SWA_BLOCK, kv * HEAD), lambda n: (jnp.maximum(n - 1, 0), 0))
    ss = pl.BlockSpec(sinks.shape, lambda n: (0, 0, 0, 0))
    return _pcall(body, grid=(nb,), in_specs=[qs, prev, cur, prev, cur, ss, qs], out_specs=[qs, cur, cur, cur, cur, ss],
                  out_shape=[S((t, d), F32)] + [S((t, kv * HEAD), F32)] * 4 + [S(sinks.shape, F32)], name=name,
                  semantics=("arbitrary",), inputs=(q, k, k, v, v, sinks, do), jobs=jobs)


def _gate_fwd(name, o, z_src, z_off, d):
    return _rows(name, lambda o_, p_: o_ * jax.nn.silu(p_[:, z_off:z_off + d]), [o, z_src], [], [(d, BF16)], [], 256)[0]


def _gate_bwd(name, o, z_src, z_off, d, dmix):
    def fn(o_, p_, dm_):
        _, vjp = jax.vjp(lambda oo, zz: oo * jax.nn.silu(zz), o_, p_[:, z_off:z_off + d])
        return vjp(dm_)

    return _rows(name, fn, [o, z_src, dmix], [], [(d, F32), (d, F32)], [], 256)


def _swa_post_bwd(name, dq, dkc, dkp_up, dvc, dvp_up, dz, pos, inv_freq):
    def fn(dq_, dkc_, dkp_, dvc_, dvp_, dz_, pos_, fr):
        cos, sin = _rope_tables(pos_, fr)
        return jnp.concatenate([_rope(dq_, cos, sin, -1.0), _rope(dkc_ + dkp_, cos, sin, -1.0), dvc_ + dvp_, dz_], axis=1)

    n = dq.shape[1] + dkc.shape[1] + dvc.shape[1] + dz.shape[1]
    return _rows(name, fn, [dq, dkc, dkp_up, dvc, dvp_up, dz, pos], [inv_freq], [(n, BF16)], [], 256)[0]


HALO = 8


def _row_before(x, halo_ref, i):
    first = jnp.where(i > 0, halo_ref[pl.ds(HALO - 1, 1), :], 0.0)
    row = lax.broadcasted_iota(jnp.int32, x.shape, 0)
    return jnp.where(row == 0, first, pltpu.roll(x, 1, 0))


def _row_after(x, halo, i, n_tiles):
    last = jnp.where(i < n_tiles - 1, halo, 0.0)
    row = lax.broadcasted_iota(jnp.int32, x.shape, 0)
    return jnp.where(row == x.shape[0] - 1, last, pltpu.roll(x, x.shape[0] - 1, 0))


def _lerp_fwd(name, p, mu, widths):
    t, n = p.shape
    tm = 128

    def body(p_ref, halo_ref, mu_ref, *o_refs):
        x = p_ref[...]
        pm = x + (_row_before(x, halo_ref, pl.program_id(0)) - x) * mu_ref[...]
        o = 0
        for ref, w in zip(o_refs, widths):
            ref[...] = pm[:, o:o + w]
            o += w

    return pl.pallas_call(
        body, grid=(t // tm,),
        in_specs=[pl.BlockSpec((tm, n), lambda i: (i, 0)),
                  pl.BlockSpec((HALO, n), lambda i: (jnp.maximum(i * (tm // HALO) - 1, 0), 0)),
                  pl.BlockSpec((1, n), lambda i: (0, 0))],
        out_specs=[pl.BlockSpec((tm, w), lambda i: (i, 0)) for w in widths],
        out_shape=[S((t, w), F32) for w in widths], name=name, compiler_params=_params(("parallel",)))(p, p, mu)


def _lerp_bwd(name, dpm_parts, p, mu):
    t, n = p.shape
    k = len(dpm_parts)
    tm = 64
    n_tiles = t // tm

    def body(*refs):
        d_refs, dh_refs = refs[:k], refs[k:2 * k]
        p_ref, ph_ref, mu_ref, dp_ref, dmu_ref = refs[2 * k:]
        i = pl.program_id(0)
        cat = lambda vals: jnp.concatenate(vals, axis=1) if k > 1 else vals[0]
        dpm = cat([r[...] for r in d_refs])
        dnext = cat([r[pl.ds(0, 1), :] for r in dh_refs])
        x, mu_ = p_ref[...], mu_ref[...]
        dp_ref[...] = (dpm * (1.0 - mu_) + _row_after(dpm, dnext, i, n_tiles) * mu_).astype(dp_ref.dtype)
        dmu = jnp.sum(dpm * (_row_before(x, ph_ref, i) - x), axis=0, keepdims=True)

        @pl.when(i == 0)
        def _():
            dmu_ref[...] = dmu

        @pl.when(i > 0)
        def _():
            dmu_ref[...] += dmu

    per = tm // HALO
    d_specs = [pl.BlockSpec((tm, a.shape[1]), lambda i: (i, 0)) for a in dpm_parts]
    dh_specs = [pl.BlockSpec((HALO, a.shape[1]), lambda i: (jnp.minimum((i + 1) * per, t // HALO - 1), 0)) for a in dpm_parts]
    return pl.pallas_call(
        body, grid=(n_tiles,),
        in_specs=d_specs + dh_specs + [pl.BlockSpec((tm, n), lambda i: (i, 0)),
                                       pl.BlockSpec((HALO, n), lambda i: (jnp.maximum(i * per - 1, 0), 0)),
                                       pl.BlockSpec((1, n), lambda i: (0, 0))],
        out_specs=[pl.BlockSpec((tm, n), lambda i: (i, 0)), pl.BlockSpec((1, n), lambda i: (0, 0))],
        out_shape=[S((t, n), BF16), S((1, n), F32)], name=name,
        compiler_params=_params(("arbitrary",)))(*dpm_parts, *dpm_parts, p, p, mu)


def _lora_act(pl_, w0, w_lora, a0, a_lora):
    logw = -DECAY_SCALE * jax.nn.sigmoid(w0 + _dot(jnp.tanh(pl_), w_lora))
    a = jax.nn.sigmoid(a0 + _dot(pl_, a_lora))
    return logw, a


def _lora_fwd(name, pl_, w0, w_lora, a0, a_lora):
    d = w0.shape[1]
    return _rows(name, _lora_act, [pl_], [w0, w_lora, a0, a_lora], [(d, F32), (d, F32)], [], 256)


def _lora_bwd(name, pl_, dlogw, da, w0, w_lora, a0, a_lora):
    def fn(p_, dl_, da_, w0_, wl_, a0_, al_):
        _, vjp = jax.vjp(_lora_act, p_, w0_, wl_, a0_, al_)
        return vjp((dl_, da_))

    return _rows(name, fn, [pl_, dlogw, da], [w0, w_lora, a0, a_lora], [(pl_.shape[1], F32)],
                 [w0.shape, w_lora.shape, a0.shape, a_lora.shape], 256)


def _head_cols(ref, hb):
    x = ref[...].astype(F32)
    xo = pltpu.roll(x, x.shape[1] - HEAD, 1)
    return [(x if j % 2 == 0 else xo)[:, 2 * HEAD * (j // 2):2 * HEAD * (j // 2) + HEAD] for j in range(hb)]


def _rwkv_scan_fwd(name, r, k, v, logw, a, hp, jobs=()):
    t, d = r.shape
    h, nc, hb = d // HEAD, t // RW_CHUNK, RW_HEADS

    def body(r_ref, k_ref, v_ref, w_ref, a_ref, kk_ref, ka_ref, rk_ref, gg_ref, gb_ref, y_ref, st_ref, s_scr):
        @pl.when(pl.program_id(1) == 0)
        def _():
            s_scr[...] = jnp.zeros_like(s_scr)

        s0 = [s_scr[j] for j in range(hb)]
        for j in range(hb):
            st_ref[j, 0] = s0[j]
        y, s1 = _rwkv_chunk(s0, *[_head_cols(ref, hb) for ref in (r_ref, k_ref, v_ref, w_ref, a_ref, kk_ref, ka_ref, rk_ref,
                                                                 gg_ref, gb_ref)])
        y_ref[...] = jnp.concatenate(y, axis=1)
        for j in range(hb):
            s_scr[j] = s1[j]

    seq = pl.BlockSpec((RW_CHUNK, hb * HEAD), lambda i, n: (n, i))
    par = pl.BlockSpec((1, hb * HEAD), lambda i, n: (0, i))
    st = pl.BlockSpec((hb, 1, HEAD, HEAD), lambda i, n: (i, n, 0, 0))
    return _pcall(body, grid=(h // hb, nc), in_specs=[seq] * 5 + [par] * 5, out_specs=[seq, st],
                  out_shape=[S((t, d), F32), S((h, nc, HEAD, HEAD), F32)], scratch_shapes=[pltpu.VMEM((hb, HEAD, HEAD), F32)],
                  name=name, semantics=("parallel", "arbitrary"), inputs=(r, k, v, logw, a, *hp), jobs=jobs)


def _rwkv_scan_bwd(name, r, k, v, logw, a, hp, states, dy, jobs=()):
    t, d = r.shape
    h, nc, hb = d // HEAD, t // RW_CHUNK, RW_HEADS

    def body(r_ref, k_ref, v_ref, w_ref, a_ref, kk_ref, ka_ref, rk_ref, gg_ref, gb_ref, st_ref, dy_ref,
             dr_ref, dk_ref, dv_ref, dw_ref, da_ref, dkk_ref, dka_ref, drk_ref, dgg_ref, dgb_ref, ds_scr):
        n = pl.program_id(1)

        @pl.when(n == 0)
        def _():
            ds_scr[...] = jnp.zeros_like(ds_scr)
            for ref in (dkk_ref, dka_ref, drk_ref, dgg_ref, dgb_ref):
                ref[...] = jnp.zeros_like(ref)

        ins = [[st_ref[j, 0] for j in range(hb)]] + [_head_cols(ref, hb) for ref in (r_ref, k_ref, v_ref, w_ref, a_ref, kk_ref,
                                                                                  ka_ref, rk_ref, gg_ref, gb_ref)]
        _, vjp = jax.vjp(_rwkv_chunk, *ins)
        ds0, *dseq, dkk, dka, drk, dgg, dgb = vjp((_head_cols(dy_ref, hb), [ds_scr[j] for j in range(hb)]))
        for j in range(hb):
            ds_scr[j] = ds0[j]
        for ref, val in zip((dr_ref, dk_ref, dv_ref, dw_ref, da_ref), dseq):
            ref[...] = jnp.concatenate(val, axis=1)
        for ref, val in ((dkk_ref, dkk), (dka_ref, dka), (drk_ref, drk), (dgg_ref, dgg), (dgb_ref, dgb)):
            ref[...] += jnp.concatenate(val, axis=1)

    seq = pl.BlockSpec((RW_CHUNK, hb * HEAD), lambda i, n: (nc - 1 - n, i))
    par = pl.BlockSpec((1, hb * HEAD), lambda i, n: (0, i))
    st = pl.BlockSpec((hb, 1, HEAD, HEAD), lambda i, n: (i, nc - 1 - n, 0, 0))
    return _pcall(body, grid=(h // hb, nc), in_specs=[seq] * 5 + [par] * 5 + [st, seq], out_specs=[seq] * 5 + [par] * 5,
                  out_shape=[S((t, d), F32)] * 5 + [S((1, d), F32)] * 5, scratch_shapes=[pltpu.VMEM((hb, HEAD, HEAD), F32)],
                  name=name, semantics=("parallel", "arbitrary"), inputs=(r, k, v, logw, a, *hp, states, dy), jobs=jobs)


def _loss_head(name, x, target, g):
    d = x.shape[1]

    def fn(x_, t_, g_):
        def f(xx, gg):
            err = _rms(xx, gg) - t_
            return 0.5 * jnp.sum(jnp.mean(err * err, axis=-1, keepdims=True), axis=0, keepdims=True)

        l, vjp = jax.vjp(f, x_, g_)
        dx, dg = vjp(jnp.ones((1, 1), F32))
        return dx, dg, jnp.broadcast_to(l, (1, 128))

    return _rows(name, fn, [x, target], [g], [(d, F32)], [(1, d), (1, 128)], 256)


def _mod_fwd(name, cond_all, mod_w, mod_b_cols):
    l, d, n = mod_w.shape

    def body(c_ref, w_ref, b_ref, o_ref):
        o_ref[0] = _dot(jax.nn.silu(c_ref[...]), w_ref[0], hi=True) + b_ref[0]

    return pl.pallas_call(body, grid=(l,), in_specs=[pl.BlockSpec((N_DEV, d), lambda i: (0, 0)),
                                                      pl.BlockSpec((1, d, n), lambda i: (i, 0, 0)),
                                                      pl.BlockSpec((1, 1, n), lambda i: (i, 0, 0))],
                          out_specs=pl.BlockSpec((1, N_DEV, n), lambda i: (i, 0, 0)), out_shape=S((l, N_DEV, n), F32),
                          name=name, compiler_params=_params(("parallel",)))(cond_all, mod_w, mod_b_cols)


def _mod_bwd(name, cond_all, dmod_cols, dmod_all):
    l, _, n = dmod_cols.shape
    d = cond_all.shape[1]
    nb = dmod_all.shape[2]

    def body(c_ref, dc_ref, da_ref, gw_ref, gb_ref):
        gw_ref[0] = _dot(jax.nn.silu(c_ref[...]), dc_ref[0], "tn", hi=True)
        acc = da_ref[0, 0:1, :]
        for bi in range(1, N_DEV):
            acc = acc + da_ref[0, bi:bi + 1, :]
        gb_ref[0] = acc

    return pl.pallas_call(body, grid=(l,), in_specs=[pl.BlockSpec((N_DEV, d), lambda i: (0, 0)),
                                                      pl.BlockSpec((1, N_DEV, n), lambda i: (i, 0, 0)),
                                                      pl.BlockSpec((1, N_DEV, nb), lambda i: (i, 0, 0))],
                          out_specs=[pl.BlockSpec((1, d, n), lambda i: (i, 0, 0)), pl.BlockSpec((1, 1, nb), lambda i: (i, 0, 0))],
                          out_shape=[S((l, d, n), F32), S((l, 1, nb), F32)], name=name,
                          compiler_params=_params(("parallel",)))(cond_all, dmod_cols, dmod_all)


def _shift_up(a, n=1):
    return jnp.concatenate([a[n:], jnp.zeros_like(a[:n])], axis=0)


def _cols_full(g):
    return g.transpose(1, 0, 2).reshape(g.shape[1], -1)


def _cols_parts(full):
    r, n = full.shape
    return full.reshape(r, N_DEV, n // N_DEV).transpose(1, 0, 2)


def _pack(arrs, mult=1024):
    flat = jnp.concatenate([a.reshape(-1) for a in arrs])
    pad = (-flat.shape[0]) % mult
    return jnp.pad(flat, (0, pad)).reshape(-1, 128)


def _unpack(flat, shapes):
    out, o = [], 0
    for s in shapes:
        n = math.prod(s)
        out.append(flat[o:o + n].reshape(s))
        o += n
    return out


def _local_step(x, pos, target, mods, norm_g, final_norm_g, layer_weights, hooks=None, on_grads=None):
    t, d = x.shape
    hooks = hooks or {}
    jobs = lambda nm: hooks.get(nm, ())
    notify = on_grads or (lambda *a: None)
    kinds = [i % 3 for i in range(DEPTH)]
    inv_freq = (ROPE_THETA ** (-jnp.arange(HEAD // 2, dtype=F32) / (HEAD // 2)))
    inv_freq = jnp.tile(inv_freq, 128 // (HEAD // 2)).reshape(1, 128)
    saved = []
    for i, kind in enumerate(kinds):
        lw = layer_weights(i)
        shift, scale, gate = (mods[i, q * d:(q + 1) * d].reshape(1, d) for q in range(3))
        g = norm_g[i].reshape(1, d)
        h = _norm_fwd(f"norm_fwd{i}", x, g, shift, scale)
        sv = dict(x=x, h=h, g=g, shift=shift, scale=scale, gate=gate, lw=lw)
        if kind == 0:
            p = _mm(f"sg_in{i}", h, lw["w_in"], "nn", F32, jobs=jobs(f"sg_in{i}"))
            mix = _sg_fwd(f"sg_mix{i}", p, lw["ln_g"], lw["ln_b"], lw["w_s"], lw["b_st"])
            sv.update(p=p)
        elif kind == 1:
            p = _mm(f"swa_in{i}", h, lw["w_in"], "nn", F32, jobs=jobs(f"swa_in{i}"))
            q, k, v = _swa_pre(f"swa_pre{i}", p, pos, inv_freq, d)
            o = _swa_attn_fwd(f"swa_attn{i}", q, k, v, lw["sinks"], jobs=jobs(f"swa_attn{i}"))
            mix = _gate_fwd(f"swa_gate{i}", o, p, d + 2 * SWA_KV * HEAD, d)
            sv.update(p=p, qkv=(q, k, v), o=o)
        else:
            pm = _mm(f"rw_in{i}", h, lw["w_main"], "nn", F32)
            plo = _mm(f"rw_inl{i}", h, lw["w_lorain"], "nn", F32)
            r, k, v, z = _lerp_fwd(f"rw_lerp{i}", pm, lw["mu_main"], [d] * 4)
            (pll,) = _lerp_fwd(f"rw_lerpl{i}", plo, lw["mu_lora"], [LORA_PAD])
            logw, a = _lora_fwd(f"rw_lora{i}", pll, lw["w0"], lw["w_lora"], lw["a0"], lw["a_lora"])
            seqs = (r, k, v, logw, a)
            o, states = _rwkv_scan_fwd(f"rw_scan{i}", *seqs, lw["hp"], jobs=jobs(f"rw_scan{i}"))
            mix = _gate_fwd(f"rw_gate{i}", o, z, 0, d)
            sv.update(pm=pm, plo=plo, pll=pll, z=z, seqs=seqs, states=states, o=o)
        y, x = _mm(f"out{i}", mix, lw["w_out"], "nn", F32, resid=(x, gate), jobs=jobs(f"out{i}"))
        sv.update(mix=mix, y=y)
        saved.append(sv)

    dx, d_final_g, loss = _loss_head("loss_head", x, target, final_norm_g.reshape(1, d))

    grads = dict(norm_g=[None] * DEPTH, sg_w_in=[None] * 2, sg_w_out=[None] * 2, sg_ln_g=[None] * 2, sg_ln_b=[None] * 2,
                 sg_w_s=[None] * 2, sg_b_st=[None] * 2, final_norm_g=d_final_g)
    dmods = [None] * DEPTH
    for i in reversed(range(DEPTH)):
        kind, j, sv = kinds[i], i // 3, saved[i]
        lw = sv["lw"]
        dy, dgate = _resid_bwd(f"resid_bwd{i}", dx, sv["y"], sv["gate"])
        d_w_out = _mm(f"out_dw{i}", sv["mix"], dy, "tn", BF16)
        if kind == 0:
            grads["sg_w_out"][j] = d_w_out
        else:
            grads[("swa_w_out", "rw_w_out")[kind - 1]] = d_w_out
        notify(i, "out", grads)
        dmix = _mm(f"out_dx{i}", dy, lw["w_out"], "nt", F32, jobs=jobs(f"out_dx{i}"))
        if kind == 0:
            dp, dlg, dlb, dws, dbs = _sg_bwd(f"sg_mix_bwd{i}", sv["p"], dmix, lw["ln_g"], lw["ln_b"], lw["w_s"], lw["b_st"],
                                            jobs=jobs(f"sg_mix_bwd{i}"))
            grads["sg_ln_g"][j], grads["sg_ln_b"][j], grads["sg_w_s"][j], grads["sg_b_st"][j] = dlg, dlb, dws, dbs
            grads["sg_w_in"][j] = _mm(f"sg_in_dw{i}", sv["h"], dp, "tn", BF16, jobs=jobs(f"sg_in_dw{i}"))
            notify(i, "in", grads)
            dh = _mm(f"sg_in_dx{i}", dp, lw["w_in"], "nt", F32, jobs=jobs(f"sg_in_dx{i}"))
        elif kind == 1:
            z_off = d + 2 * SWA_KV * HEAD
            do, dz = _gate_bwd(f"swa_gate_bwd{i}", sv["o"], sv["p"], z_off, d, dmix)
            dq, dkc, dkp, dvc, dvp, dsinks = _swa_attn_bwd(f"swa_attn_bwd{i}", *sv["qkv"], lw["sinks"], do,
                                                           jobs=jobs(f"swa_attn_bwd{i}"))
            dkp, dvp = _shift_up(dkp, SWA_BLOCK), _shift_up(dvp, SWA_BLOCK)
            dp = _swa_post_bwd(f"swa_post_bwd{i}", dq, dkc, dkp, dvc, dvp, dz, pos, inv_freq)
            grads.update(swa_sinks=dsinks, swa_w_out=d_w_out)
            grads["swa_w_in"] = _mm(f"swa_in_dw{i}", sv["h"], dp, "tn", BF16)
            dh = _mm(f"swa_in_dx{i}", dp, lw["w_in"], "nt", F32)
        else:
            do, dz = _gate_bwd(f"rw_gate_bwd{i}", sv["o"], sv["z"], 0, d, dmix)
            res = _rwkv_scan_bwd(f"rw_scan_bwd{i}", *sv["seqs"], lw["hp"], sv["states"], do, jobs=jobs(f"rw_scan_bwd{i}"))
            dr, dk, dv, dlogw, da = res[:5]
            dpll, dw0, dwl, da0, dal = _lora_bwd(f"rw_lora_bwd{i}", sv["pll"], dlogw, da, lw["w0"], lw["w_lora"],
                                                  lw["a0"], lw["a_lora"])
            dpm, dmu_main = _lerp_bwd(f"rw_lerp_bwd{i}", [dr, dk, dv, dz], sv["pm"], lw["mu_main"])
            dpl, dmu_lora = _lerp_bwd(f"rw_lerpl_bwd{i}", [dpll], sv["plo"], lw["mu_lora"])
            grads.update(rw_w_out=d_w_out, rw_hp=res[5:], rw_w0=dw0, rw_w_lora=dwl, rw_a0=da0, rw_a_lora=dal,
                         rw_mu_main=dmu_main, rw_mu_lora=dmu_lora)
            grads["rw_w_main"] = _mm(f"rw_in_dw{i}", sv["h"], dpm, "tn", BF16)
            grads["rw_w_lorain"] = _mm(f"rw_inl_dw{i}", sv["h"], dpl, "tn", BF16)
            dh = _mm(f"rw_inl_dx{i}", dpl, lw["w_lorain"], "nt", F32)
            dh = _mm(f"rw_in_dx{i}", dpm, lw["w_main"], "nt", F32, add=dh)
        if kind != 0:
            notify(i, "in", grads)
        dx, dg, dshift, dscale = _norm_bwd(f"norm_bwd{i}", sv["x"], dh, dx, sv["g"], sv["shift"], sv["scale"])
        grads["norm_g"][i] = dg
        dmods[i] = jnp.concatenate([dshift, dscale, dgate], axis=1)
    return loss, dx, jnp.concatenate(dmods, axis=0), grads


def kernel(x, c, positions, norm_g, mod_w, mod_b, final_norm_g, sg_w_in, sg_w_out, sg_ln_g, sg_ln_b, sg_w_spatial, sg_b_spatial, swa_w_in, swa_w_out, swa_sinks, rwkv_w_in, rwkv_w_out, rwkv_mu, rwkv_w0, rwkv_w_lora, rwkv_a0, rwkv_a_lora, rwkv_k_k, rwkv_k_a, rwkv_r_k, rwkv_gn_g, rwkv_gn_b, loss_target, m_norm_g, m_mod_w, m_mod_b, m_final_norm_g, m_sg_w_in, m_sg_w_out, m_sg_ln_g, m_sg_ln_b, m_sg_w_spatial, m_sg_b_spatial, m_swa_w_in, m_swa_w_out, m_swa_sinks, m_rwkv_w_in, m_rwkv_w_out, m_rwkv_mu, m_rwkv_w0, m_rwkv_w_lora, m_rwkv_a0, m_rwkv_a_lora, m_rwkv_k_k, m_rwkv_k_a, m_rwkv_r_k, m_rwkv_gn_g, m_rwkv_gn_b, v_norm_g, v_mod_w, v_mod_b, v_final_norm_g, v_sg_w_in, v_sg_w_out, v_sg_ln_g, v_sg_ln_b, v_sg_w_spatial, v_sg_b_spatial, v_swa_w_in, v_swa_w_out, v_swa_sinks, v_rwkv_w_in, v_rwkv_w_out, v_rwkv_mu, v_rwkv_w0, v_rwkv_w_lora, v_rwkv_a0, v_rwkv_a_lora, v_rwkv_k_k, v_rwkv_k_a, v_rwkv_r_k, v_rwkv_gn_g, v_rwkv_gn_b):
    weights = dict(norm_g=norm_g, mod_w=mod_w, mod_b=mod_b, final_norm_g=final_norm_g, sg_w_in=sg_w_in, sg_w_out=sg_w_out,
                   sg_ln_g=sg_ln_g, sg_ln_b=sg_ln_b, sg_w_spatial=sg_w_spatial, sg_b_spatial=sg_b_spatial, swa_w_in=swa_w_in,
                   swa_w_out=swa_w_out, swa_sinks=swa_sinks, rwkv_w_in=rwkv_w_in, rwkv_w_out=rwkv_w_out, rwkv_mu=rwkv_mu,
                   rwkv_w0=rwkv_w0, rwkv_w_lora=rwkv_w_lora, rwkv_a0=rwkv_a0, rwkv_a_lora=rwkv_a_lora, rwkv_k_k=rwkv_k_k,
                   rwkv_k_a=rwkv_k_a, rwkv_r_k=rwkv_r_k, rwkv_gn_g=rwkv_gn_g, rwkv_gn_b=rwkv_gn_b)
    mom_m = dict(norm_g=m_norm_g, mod_w=m_mod_w, mod_b=m_mod_b, final_norm_g=m_final_norm_g, sg_w_in=m_sg_w_in,
                 sg_w_out=m_sg_w_out, sg_ln_g=m_sg_ln_g, sg_ln_b=m_sg_ln_b, sg_w_spatial=m_sg_w_spatial,
                 sg_b_spatial=m_sg_b_spatial, swa_w_in=m_swa_w_in, swa_w_out=m_swa_w_out, swa_sinks=m_swa_sinks,
                 rwkv_w_in=m_rwkv_w_in, rwkv_w_out=m_rwkv_w_out, rwkv_mu=m_rwkv_mu, rwkv_w0=m_rwkv_w0,
                 rwkv_w_lora=m_rwkv_w_lora, rwkv_a0=m_rwkv_a0, rwkv_a_lora=m_rwkv_a_lora, rwkv_k_k=m_rwkv_k_k,
                 rwkv_k_a=m_rwkv_k_a, rwkv_r_k=m_rwkv_r_k, rwkv_gn_g=m_rwkv_gn_g, rwkv_gn_b=m_rwkv_gn_b)
    mom_v = dict(norm_g=v_norm_g, mod_w=v_mod_w, mod_b=v_mod_b, final_norm_g=v_final_norm_g, sg_w_in=v_sg_w_in,
                 sg_w_out=v_sg_w_out, sg_ln_g=v_sg_ln_g, sg_ln_b=v_sg_ln_b, sg_w_spatial=v_sg_w_spatial,
                 sg_b_spatial=v_sg_b_spatial, swa_w_in=v_swa_w_in, swa_w_out=v_swa_w_out, swa_sinks=v_swa_sinks,
                 rwkv_w_in=v_rwkv_w_in, rwkv_w_out=v_rwkv_w_out, rwkv_mu=v_rwkv_mu, rwkv_w0=v_rwkv_w0,
                 rwkv_w_lora=v_rwkv_w_lora, rwkv_a0=v_rwkv_a0, rwkv_a_lora=v_rwkv_a_lora, rwkv_k_k=v_rwkv_k_k,
                 rwkv_k_a=v_rwkv_k_a, rwkv_r_k=v_rwkv_r_k, rwkv_gn_g=v_rwkv_gn_g, rwkv_gn_b=v_rwkv_gn_b)
    names = list(weights)
    t, d = x.shape[1], x.shape[2]
    me = 4 * lax.axis_index("x") + 2 * lax.axis_index("y") + lax.axis_index("c")
    n_mod = mod_w.shape[2]
    n_rw = rwkv_w_in.shape[2]

    small_names = ["sg_ln_g", "sg_ln_b", "rwkv_mu", "rwkv_w0", "rwkv_a0", "rwkv_k_k", "rwkv_k_a", "rwkv_gn_g", "rwkv_gn_b",
                   "rwkv_w_lora", "rwkv_a_lora"]
    small_shapes = [weights[n].shape for n in small_names]
    pk = _pack([c] + [weights[n] for n in small_names])
    gathered = _gather("gather_small", pk).reshape(N_DEV, -1)
    c_all = gathered[:, :d]
    per_dev = [_unpack(gathered[dv, d:], small_shapes) for dv in range(N_DEV)]
    full_small = {}
    for q, n in enumerate(small_names):
        full_small[n] = jnp.concatenate([per_dev[dv][q] for dv in range(N_DEV)], axis=-1)

    mod_b_cols = lax.dynamic_slice_in_dim(mod_b, me * n_mod, n_mod, axis=1).reshape(DEPTH, 1, n_mod)
    mod_part = _mod_fwd("mod_fwd", c_all, mod_w, mod_b_cols)
    mod_g = _gather("gather_mod", mod_part.reshape(DEPTH * N_DEV, n_mod))
    mod_g = mod_g.reshape(N_DEV, DEPTH, N_DEV, n_mod)
    mods = lax.dynamic_index_in_dim(mod_g, me, axis=2, keepdims=False)
    mods = mods.transpose(1, 0, 2).reshape(DEPTH, N_DEV * n_mod)

    job = lambda cls, src: dict(cls=cls, src=src)
    gj = dict(swa_in=job(_Gather, swa_w_in[0].astype(BF16)), swa_out=job(_Gather, swa_w_out[0].astype(BF16)),
              rw_in=job(_Gather, rwkv_w_in[0].astype(BF16)), rw_out=job(_Gather, rwkv_w_out[0].astype(BF16)),
              sg_in1=job(_Gather, sg_w_in[1].astype(BF16)), sg_out1=job(_Gather, sg_w_out[1].astype(BF16)))
    hooks = {"sg_in0": [gj["swa_in"]], "out0": [gj["swa_out"]], "swa_in1": [gj["rw_out"]], "swa_attn1": [gj["rw_in"]],
             "rw_scan2": [gj["sg_in1"], gj["sg_out1"]]}
    g_sg_in0 = _gather("gather_sg_in0", sg_w_in[0].astype(BF16))
    g_sg_out0 = _gather("gather_sg_out0", sg_w_out[0].astype(BF16))
    lora_rows = lambda w, off: jnp.zeros((LORA_PAD, d), F32).at[off:off + LORA].set(w)
    mu = full_small["rwkv_mu"].reshape(1, -1)
    heads = lambda a: a.reshape(1, -1)

    def layer_weights(i):
        if i % 3 == 0:
            j = i // 3
            g_in, g_out = (g_sg_in0, g_sg_out0) if j == 0 else (gj["sg_in1"]["out"], gj["sg_out1"]["out"])
            return dict(w_in=_cols_full(g_in), w_out=g_out.reshape(d, d), ln_g=full_small["sg_ln_g"][j].reshape(1, d),
                        ln_b=full_small["sg_ln_b"][j].reshape(1, d), w_s=sg_w_spatial[j], b_st=sg_b_spatial[j].T)
        if i % 3 == 1:
            return dict(w_in=_cols_full(gj["swa_in"]["out"]), w_out=gj["swa_out"]["out"].reshape(d, d),
                        sinks=swa_sinks.reshape(SWA_KV, SWA_REP, 1, 1))
        rw_in_full = _cols_full(gj["rw_in"]["out"])
        return dict(w_main=rw_in_full[:, :4 * d], w_lorain=jnp.pad(rw_in_full[:, 4 * d:], ((0, 0), (0, LORA_PAD - 2 * LORA))),
                    w_out=gj["rw_out"]["out"].reshape(d, d), mu_main=mu[:, :4 * d],
                    mu_lora=jnp.pad(mu[:, 4 * d:], ((0, 0), (0, LORA_PAD - 2 * LORA))),
                    w0=full_small["rwkv_w0"], a0=full_small["rwkv_a0"],
                    w_lora=lora_rows(full_small["rwkv_w_lora"][0], 0), a_lora=lora_rows(full_small["rwkv_a_lora"][0], LORA),
                    hp=[heads(full_small["rwkv_k_k"]), heads(full_small["rwkv_k_a"]), heads(rwkv_r_k),
                        heads(full_small["rwkv_gn_g"]), heads(full_small["rwkv_gn_b"])])

    sj = {}

    def on_grads(i, which, g):
        def stage(nm, parts, host):
            sj[nm] = job(_Chips, _scatter_pairs("scatter_" + nm, parts.astype(BF16)))
            hooks.setdefault(host, []).append(sj[nm])

        rows_of = lambda a: a.reshape(N_DEV, -1, d)
        if (i, which) == (3, "out"):
            stage("sg_out1", rows_of(g["sg_w_out"][1]), "rw_scan_bwd2")
        elif (i, which) == (3, "in"):
            stage("sg_in1", _cols_parts(g["sg_w_in"][1]), "rw_scan_bwd2")
        elif (i, which) == (2, "out"):
            stage("rw_out", rows_of(g["rw_w_out"]), "swa_attn_bwd1")
        elif (i, which) == (2, "in"):
            d_rw_in = jnp.concatenate([g["rw_w_main"], g["rw_w_lorain"][:, :2 * LORA]], axis=1)
            stage("rw_in", _cols_parts(d_rw_in), "swa_attn_bwd1")
        elif (i, which) == (1, "out"):
            stage("swa_out", rows_of(g["swa_w_out"]), "out_dx0")
        elif (i, which) == (1, "in"):
            stage("swa_in", _cols_parts(g["swa_w_in"]), "sg_mix_bwd0")
        elif (i, which) == (0, "out"):
            stage("sg_out0", rows_of(g["sg_w_out"][0]), "sg_in_dw0")
        else:
            stage("sg_in0", _cols_parts(g["sg_w_in"][0]), "sg_in_dx0")

    loss, dx, dmods, g = _local_step(x[0], positions.reshape(t, 1).astype(F32), loss_target[0], mods, norm_g, final_norm_g,
                                     layer_weights, hooks, on_grads)

    dmod_g = _gather("gather_dmod", dmods)
    dmod_all = dmod_g.transpose(1, 0, 2)
    dmod_cols = lax.dynamic_slice_in_dim(dmod_all, me * n_mod, n_mod, axis=2)
    g_mod_w, g_mod_b = _mod_bwd("mod_bwd", c_all, dmod_cols, dmod_all)

    d_b_sp = [g["sg_b_st"][j].T for j in range(2)]
    rep = [loss[0, :1], jnp.concatenate(g["norm_g"], axis=0), g["final_norm_g"], jnp.stack(g["sg_w_s"]), jnp.stack(d_b_sp),
           g["swa_sinks"], g["rw_hp"][2]]
    rep_shapes = [(1,), norm_g.shape, final_norm_g.shape, sg_w_spatial.shape, sg_b_spatial.shape, swa_sinks.shape, rwkv_r_k.shape]
    rep_sum = _sum_parts("sum_rep", _gather("gather_rep", _pack(rep, 128 * 256))).reshape(-1)
    loss_tot, g_norm_g, g_final, g_w_sp, g_b_sp, g_sinks, g_r_k = _unpack(rep_sum, rep_shapes)

    p_sg_in = jnp.concatenate([sj["sg_in0"]["out"], sj["sg_in1"]["out"]], axis=1)
    p_sg_out = jnp.concatenate([sj["sg_out0"]["out"], sj["sg_out1"]["out"]], axis=1)
    p_swa_in, p_swa_out, p_rw_in, p_rw_out = (sj[nm]["out"] for nm in ("swa_in", "swa_out", "rw_in", "rw_out"))
    d_mu = jnp.concatenate([g["rw_mu_main"], g["rw_mu_lora"][:, :2 * LORA]], axis=1)
    hp_flat = lambda a: a.reshape(1, -1)
    small_grads = dict(sg_ln_g=jnp.concatenate(g["sg_ln_g"], axis=0), sg_ln_b=jnp.concatenate(g["sg_ln_b"], axis=0), rwkv_mu=d_mu,
                       rwkv_w0=g["rw_w0"], rwkv_a0=g["rw_a0"], rwkv_k_k=hp_flat(g["rw_hp"][0]), rwkv_k_a=hp_flat(g["rw_hp"][1]),
                       rwkv_gn_g=hp_flat(g["rw_hp"][3]), rwkv_gn_b=hp_flat(g["rw_hp"][4]),
                       rwkv_w_lora=g["rw_w_lora"][None, :LORA], rwkv_a_lora=g["rw_a_lora"][None, LORA:2 * LORA])
    per_dest = []
    for dv in range(N_DEV):
        shards = []
        for n in small_names:
            full, w = small_grads[n], weights[n].shape[-1]
            shards.append(full[..., dv * w:(dv + 1) * w])
        per_dest.append(_pack(shards))
    small_parts = _exchange("scatter_small", jnp.stack(per_dest), True)

    out_g, out_d, out_m, out_v = {}, {}, {}, {}

    def update(name, grad, shape2d, jobs=()):
        w2, m2, v2 = (a[name].reshape(shape2d) for a in (weights, mom_m, mom_v))
        gg, dd, mm, vv = _adamw("adamw_" + name, w2, grad, m2, v2, jobs=jobs)
        shp = weights[name].shape
        out_g[name], out_d[name], out_m[name], out_v[name] = gg.reshape(shp), dd.reshape(shp), mm.reshape(shp), vv.reshape(shp)

    update("mod_w", g_mod_w.reshape(-1, n_mod), (-1, n_mod))
    update("sg_w_in", p_sg_in, (-1, sg_w_in.shape[2]))
    update("sg_w_out", p_sg_out, (-1, d))
    update("swa_w_in", p_swa_in, (-1, swa_w_in.shape[2]))
    update("swa_w_out", p_swa_out, (-1, d))
    update("rwkv_w_in", p_rw_in, (-1, n_rw))
    update("rwkv_w_out", p_rw_out, (-1, d))
    update("sg_w_spatial", g_w_sp.reshape(-1, 128), (-1, 128))
    w_pk, m_pk, v_pk = (_pack([a[n] for n in small_names]) for a in (weights, mom_m, mom_v))
    res = _adamw("adamw_small", w_pk, small_parts, m_pk, v_pk)
    for q, arrs in enumerate(zip(*[_unpack(r_.reshape(-1), small_shapes) for r_ in res])):
        out_g[small_names[q]], out_d[small_names[q]], out_m[small_names[q]], out_v[small_names[q]] = arrs
    rep_names = ["norm_g", "mod_b", "final_norm_g", "sg_b_spatial", "swa_sinks", "rwkv_r_k"]
    rep_grads = [g_norm_g, g_mod_b.reshape(mod_b.shape), g_final, g_b_sp, g_sinks, g_r_k]
    rep_shapes2 = [weights[n].shape for n in rep_names]
    w_pk, m_pk, v_pk = (_pack([a[n] for n in rep_names]) for a in (weights, mom_m, mom_v))
    res = _adamw("adamw_rep", w_pk, _pack(rep_grads), m_pk, v_pk)
    for q, arrs in enumerate(zip(*[_unpack(r_.reshape(-1), rep_shapes2) for r_ in res])):
        out_g[rep_names[q]], out_d[rep_names[q]], out_m[rep_names[q]], out_v[rep_names[q]] = arrs

    return (loss_tot.reshape(()), dx[None], *[out_g[n] for n in names], *[out_d[n] for n in names],
            *[out_m[n] for n in names], *[out_v[n] for n in names])
```

```python
import functools
import math

import jax
import jax.numpy as jnp
from jax import lax
from jax.experimental import pallas as pl
from jax.experimental.pallas import tpu as pltpu

F32, BF16 = jnp.float32, jnp.bfloat16
HI = lax.Precision.HIGHEST
S = jax.ShapeDtypeStruct
MESH = pl.DeviceIdType.MESH

N_DEV = 8
DEPTH = 4
HEAD = 64
SG_GROUPS = 16
SG_CHUNK = 128
SWA_BLOCK = 128
SWA_KV = 4
SWA_REP = 8
ROPE_THETA = 10000.0
LORA = 96
LORA_PAD = 256
RW_CHUNK = 64
RW_HEADS = 16
RW_HEADS_FWD = 32
RW_PREC = lax.Precision.HIGH
DECAY_SCALE = math.exp(-0.5)
GN_EPS = 64e-5
RMS_EPS = 1e-6
LN_EPS = 1e-5
NEG = -1e30
ADAM_LR, ADAM_B1, ADAM_B2, ADAM_EPS, ADAM_WD, ADAM_STEP = 0.001, 0.9, 0.999, 1e-08, 0.01, 10
VMEM_MB = 56


def _params(sem=None):
    kw = dict(vmem_limit_bytes=VMEM_MB << 20)
    if sem is not None:
        kw["dimension_semantics"] = sem
    return pltpu.CompilerParams(**kw)


def _pick(n, opts):
    for o in opts:
        if n % o == 0:
            return o
    raise ValueError(f"no tile for {n}")


def _rows(name, fn, rows, consts, out_rows, out_accs, tm, jobs=()):
    t = rows[0].shape[0]
    nr, nc, no = len(rows), len(consts), len(out_rows)

    def body(*refs):
        outs = fn(*[r[...] for r in refs[:nr + nc]])
        if not isinstance(outs, (tuple, list)):
            outs = (outs,)
        for r, o in zip(refs[nr + nc:nr + nc + no], outs[:no]):
            r[...] = o.astype(r.dtype)
        i = pl.program_id(0)
        for r, o in zip(refs[nr + nc + no:], outs[no:]):
            @pl.when(i == 0)
            def _(r=r, o=o):
                r[...] = o.astype(r.dtype)

            @pl.when(i > 0)
            def _(r=r, o=o):
                r[...] += o.astype(r.dtype)

    in_specs = [pl.BlockSpec((tm, a.shape[1]), lambda i: (i, 0)) for a in rows]
    in_specs += [pl.BlockSpec(c.shape, lambda i, nd=c.ndim: (0,) * nd) for c in consts]
    out_specs = [pl.BlockSpec((tm, n), lambda i: (i, 0)) for n, _ in out_rows]
    out_specs += [pl.BlockSpec(s, lambda i, nd=len(s): (0,) * nd) for s in out_accs]
    out_shape = [S((t, n), dt) for n, dt in out_rows] + [S(s, F32) for s in out_accs]
    return _pcall(body, grid=(t // tm,), in_specs=in_specs, out_specs=out_specs, out_shape=out_shape, name=name,
                  semantics=("arbitrary",), inputs=(*rows, *consts), jobs=jobs)


_DN = {"nn": (((1,), (0,)), ((), ())), "nt": (((1,), (1,)), ((), ())), "tn": (((0,), (0,)), ((), ()))}


def _mm(name, a, b, mode, out_dtype, add=None, resid=None, jobs=()):
    if mode == "nn":
        (m, k), (_, n) = a.shape, b.shape
    elif mode == "nt":
        (m, k), (n, _) = a.shape, b.shape
    else:
        (k, m), (_, n) = a.shape, b.shape
    wide = mode != "tn" and add is None and resid is None
    tm = _pick(m, (1024, 512, 256, 128))
    tn = _pick(n, ((1536,) if wide and n % 1024 else ()) + (1024, 512, 384, 256, 128))
    long_k = (4096,) if mode == "tn" else (3072, 2304) if mode == "nt" and add is None else ()
    tk = _pick(k, long_k + (2048, 1536, 1024, 512, 384, 256, 128))
    nk = k // tk
    n_extra = (add is not None) + 2 * (resid is not None)

    def body(*refs):
        a_ref, b_ref = refs[0], refs[1]
        extra, outs, acc = refs[2:2 + n_extra], refs[2 + n_extra:-1], refs[-1]
        kk = pl.program_id(2)
        prod = lax.dot_general(a_ref[...].astype(BF16), b_ref[...].astype(BF16), _DN[mode], preferred_element_type=F32)
        if add is not None:
            prod = jnp.where(kk == 0, prod + extra[0][...].astype(F32), prod) if nk > 1 else prod + extra[0][...].astype(F32)

        def finish(total):
            outs[0][...] = total.astype(outs[0].dtype)
            if resid is not None:
                outs[1][...] = extra[-2][...] + extra[-1][...] * total

        if nk == 1:
            finish(prod)
            return

        @pl.when(kk == 0)
        def _():
            acc[...] = prod

        @pl.when(kk > 0)
        def _():
            acc[...] += prod

        @pl.when(kk == nk - 1)
        def _():
            finish(acc[...])

    a_spec = pl.BlockSpec((tk, tm), lambda i, j, q: (q, i)) if mode == "tn" else pl.BlockSpec((tm, tk), lambda i, j, q: (i, q))
    b_spec = pl.BlockSpec((tn, tk), lambda i, j, q: (j, q)) if mode == "nt" else pl.BlockSpec((tk, tn), lambda i, j, q: (q, j))
    o_spec = pl.BlockSpec((tm, tn), lambda i, j, q: (i, j))
    ins, specs, out_specs, out_shape = [a, b], [a_spec, b_spec], [o_spec], [S((m, n), out_dtype)]
    if add is not None:
        ins.append(add)
        specs.append(o_spec)
    if resid is not None:
        ins += list(resid)
        specs += [o_spec, pl.BlockSpec((1, tn), lambda i, j, q: (0, j))]
        out_specs.append(o_spec)
        out_shape.append(S((m, n), F32))
    res = _pcall(body, grid=(m // tm, n // tn, nk), in_specs=specs, out_specs=out_specs, out_shape=out_shape,
                 scratch_shapes=[pltpu.VMEM((tm, tn), F32)], name=name, semantics=("parallel", "parallel", "arbitrary"),
                 inputs=ins, jobs=jobs)
    return res if resid is not None else res[0]


def _exchange(name, src, scatter):
    blk = src.shape[1:] if scatter else src.shape

    def body(src_ref, dst_ref, send_sems, recv_sems, loc_sem):
        x, y, c = lax.axis_index("x"), lax.axis_index("y"), lax.axis_index("c")
        me = 4 * x + 2 * y + c

        def mine(d):
            return src_ref.at[d] if scatter else src_ref

        local = pltpu.make_async_copy(mine(me), dst_ref.at[me], loc_sem)
        local.start()
        sends, peers = [], []
        for k in range(1, N_DEV):
            px = 1 - x if k & 4 else x
            py = 1 - y if k & 2 else y
            pc = 1 - c if k & 1 else c
            pid = 4 * px + 2 * py + pc
            cp = pltpu.make_async_remote_copy(src_ref=mine(pid), dst_ref=dst_ref.at[me], send_sem=send_sems.at[k - 1],
                                              recv_sem=recv_sems.at[k - 1], device_id=(px, py, pc), device_id_type=MESH)
            cp.start()
            sends.append(cp)
            peers.append((pid, (px, py, pc)))
        for k in range(1, N_DEV):
            pid, dev = peers[k - 1]
            pltpu.make_async_remote_copy(src_ref=mine(pid), dst_ref=dst_ref.at[pid], send_sem=send_sems.at[k - 1],
                                         recv_sem=recv_sems.at[k - 1], device_id=dev, device_id_type=MESH).wait_recv()
        for cp in sends:
            cp.wait_send()
        local.wait()

    return pl.pallas_call(
        body, out_shape=S((N_DEV,) + tuple(blk), src.dtype),
        in_specs=[pl.BlockSpec(memory_space=pl.ANY)], out_specs=pl.BlockSpec(memory_space=pl.ANY),
        scratch_shapes=[pltpu.SemaphoreType.DMA((N_DEV - 1,)), pltpu.SemaphoreType.DMA((N_DEV - 1,)),
                        pltpu.SemaphoreType.DMA],
        name=name)(src)


class _Gather:
    @staticmethod
    def out_shape(src):
        return S((N_DEV,) + tuple(src.shape), src.dtype)

    scratch = (pltpu.SemaphoreType.DMA((N_DEV - 1,)), pltpu.SemaphoreType.DMA((N_DEV - 1,)), pltpu.SemaphoreType.DMA)

    def __init__(self, src_ref, dst_ref, send_sems, recv_sems, loc_sem):
        self.refs = (src_ref, dst_ref, send_sems, recv_sems, loc_sem)
        x, y, c = lax.axis_index("x"), lax.axis_index("y"), lax.axis_index("c")
        self.c, self.me, self.sibling = c, (x, y, c), (x, y, 1 - c)
        self.chips = [(1 - x, y), (x, 1 - y), (1 - x, 1 - y)]

    def rows(self, px, py, pc):
        return self.refs[1].at[4 * px + 2 * py + pc]

    def copy(self, k, block, to, own=False):
        src_ref, _, send_sems, recv_sems, _ = self.refs
        return pltpu.make_async_remote_copy(src_ref=src_ref if own else self.rows(*block), dst_ref=self.rows(*block),
                                            send_sem=send_sems.at[k], recv_sem=recv_sems.at[k], device_id=to,
                                            device_id_type=MESH)

    def local(self):
        return pltpu.make_async_copy(self.refs[0], self.rows(*self.me), self.refs[4])

    def first(self):
        return [self.copy(0, self.me, self.sibling, own=True)] + [self.copy(1 + j, self.me, (*chip, self.c), own=True)
                                                                  for j, chip in enumerate(self.chips)]

    def start(self):
        self.local().start()
        for cp in self.first():
            cp.start()

    def finish(self):
        c = self.c
        passed = [self.copy(4 + j, (*chip, c), self.sibling) for j, chip in enumerate(self.chips)]
        for j, chip in enumerate(self.chips):
            self.copy(1 + j, (*chip, c), self.me).wait_recv()
            passed[j].start()
        self.copy(0, self.sibling, self.me).wait_recv()
        for j, chip in enumerate(self.chips):
            self.copy(4 + j, (*chip, 1 - c), self.me).wait_recv()
        for cp in self.first() + passed:
            cp.wait_send()
        self.local().wait()


class _Chips:
    @staticmethod
    def out_shape(src):
        return S(src.shape, src.dtype)

    scratch = (pltpu.SemaphoreType.DMA((N_DEV // 2 - 1,)), pltpu.SemaphoreType.DMA((N_DEV // 2 - 1,)), pltpu.SemaphoreType.DMA)

    def __init__(self, src_ref, dst_ref, send_sems, recv_sems, loc_sem):
        self.refs = (src_ref, dst_ref, send_sems, recv_sems, loc_sem)
        x, y, c = lax.axis_index("x"), lax.axis_index("y"), lax.axis_index("c")
        self.c, self.mine = c, 2 * x + y
        self.chips = [(1 - x, y), (x, 1 - y), (1 - x, 1 - y)]

    def local(self):
        src_ref, dst_ref, _, _, loc_sem = self.refs
        return pltpu.make_async_copy(src_ref.at[self.mine], dst_ref.at[self.mine], loc_sem)

    def send(self, j):
        src_ref, dst_ref, send_sems, recv_sems, _ = self.refs
        px, py = self.chips[j]
        return pltpu.make_async_remote_copy(src_ref=src_ref.at[2 * px + py], dst_ref=dst_ref.at[self.mine],
                                            send_sem=send_sems.at[j], recv_sem=recv_sems.at[j],
                                            device_id=(px, py, self.c), device_id_type=MESH)

    def arrival(self, j):
        src_ref, dst_ref, send_sems, recv_sems, _ = self.refs
        px, py = self.chips[j]
        return pltpu.make_async_remote_copy(src_ref=src_ref.at[self.mine], dst_ref=dst_ref.at[2 * px + py],
                                            send_sem=send_sems.at[j], recv_sem=recv_sems.at[j],
                                            device_id=(px, py, self.c), device_id_type=MESH)

    def start(self):
        self.local().start()
        for j in range(len(self.chips)):
            self.send(j).start()

    def finish(self):
        for j in range(len(self.chips)):
            self.arrival(j).wait_recv()
        for j in range(len(self.chips)):
            self.send(j).wait_send()
        self.local().wait()


def _exchange_call(name, cls, src):
    def body(*refs):
        ex = cls(*refs)
        ex.start()
        ex.finish()

    return pl.pallas_call(body, out_shape=cls.out_shape(src), in_specs=[pl.BlockSpec(memory_space=pl.ANY)],
                          out_specs=pl.BlockSpec(memory_space=pl.ANY), scratch_shapes=list(cls.scratch), name=name)(src)


def _gather(name, src):
    return _exchange_call(name, _Gather, src)


def _pcall(body, *, grid, in_specs, out_specs, out_shape, scratch_shapes=(), name, semantics, inputs, jobs=()):
    if not jobs:
        return pl.pallas_call(body, grid=grid, in_specs=in_specs, out_specs=out_specs, out_shape=out_shape,
                              scratch_shapes=list(scratch_shapes), name=name, compiler_params=_params(semantics))(*inputs)
    n_in, n_out, n_scr, nj = len(in_specs), len(out_specs), len(scratch_shapes), len(jobs)

    def hosted(*refs):
        ins, srcs = refs[:n_in], refs[n_in:n_in + nj]
        outs, dsts = refs[n_in + nj:n_in + nj + n_out], refs[n_in + nj + n_out:n_in + 2 * nj + n_out]
        scr, sems = refs[n_in + 2 * nj + n_out:n_in + 2 * nj + n_out + n_scr], refs[n_in + 2 * nj + n_out + n_scr:]
        ids = [pl.program_id(q) for q in range(len(grid))]
        first = functools.reduce(jnp.logical_and, [i == 0 for i in ids])
        last = functools.reduce(jnp.logical_and, [i == g - 1 for i, g in zip(ids, grid)])
        make = lambda q: jobs[q]["cls"](srcs[q], dsts[q], *sems[3 * q:3 * q + 3])

        @pl.when(first)
        def _():
            for q in range(nj):
                make(q).start()

        body(*ins, *outs, *scr)

        @pl.when(last)
        def _():
            for q in range(nj):
                make(q).finish()

    anyspec = pl.BlockSpec(memory_space=pl.ANY)
    res = pl.pallas_call(
        hosted, grid=grid, in_specs=list(in_specs) + [anyspec] * nj, out_specs=list(out_specs) + [anyspec] * nj,
        out_shape=list(out_shape) + [j["cls"].out_shape(j["src"]) for j in jobs],
        scratch_shapes=list(scratch_shapes) + [s for j in jobs for s in j["cls"].scratch], name=name,
        compiler_params=_params(("arbitrary",) * len(grid)))(*inputs, *[j["src"] for j in jobs])
    for j, out in zip(jobs, res[n_out:]):
        j["out"] = out
    return res[:n_out]


def _scatter_pairs(name, parts):
    _, r, c_ = parts.shape
    n_chip = N_DEV // 2

    def stage1(src_ref, dst_ref, send_sems, recv_sems):
        x, y, c = lax.axis_index("x"), lax.axis_index("y"), lax.axis_index("c")
        sends = []
        for q in range(n_chip):
            cp = pltpu.make_async_remote_copy(src_ref=src_ref.at[2 * q + 1 - c], dst_ref=dst_ref.at[q],
                                              send_sem=send_sems.at[q], recv_sem=recv_sems.at[q],
                                              device_id=(x, y, 1 - c), device_id_type=MESH)
            cp.start()
            sends.append(cp)
        for q in range(n_chip):
            pltpu.make_async_remote_copy(src_ref=src_ref.at[2 * q + c], dst_ref=dst_ref.at[q], send_sem=send_sems.at[q],
                                         recv_sem=recv_sems.at[q], device_id=(x, y, 1 - c), device_id_type=MESH).wait_recv()
        for cp in sends:
            cp.wait_send()

    from_sibling = pl.pallas_call(
        stage1, out_shape=S((n_chip, r, c_), parts.dtype),
        in_specs=[pl.BlockSpec(memory_space=pl.ANY)], out_specs=pl.BlockSpec(memory_space=pl.ANY),
        scratch_shapes=[pltpu.SemaphoreType.DMA((n_chip,)), pltpu.SemaphoreType.DMA((n_chip,))], name=name + "_pair")(parts)

    tm = _pick(r, (512, 256, 128, 64, 32, 16, 8)) if r % 8 == 0 else r
    core = lax.axis_index("c").astype(jnp.int32).reshape(1)

    def pair_sum(core_ref, mine_ref, sib_ref, o_ref):
        o_ref[...] = (mine_ref[0].astype(F32) + sib_ref[...].astype(F32)).astype(o_ref.dtype)

    pair = pl.pallas_call(
        pair_sum, out_shape=S((n_chip, r, c_), parts.dtype),
        grid_spec=pltpu.PrefetchScalarGridSpec(
            num_scalar_prefetch=1, grid=(n_chip, r // tm),
            in_specs=[pl.BlockSpec((1, 1, tm, c_), lambda q, i, core_ref: (q, core_ref[0], i, 0)),
                      pl.BlockSpec((1, tm, c_), lambda q, i, core_ref: (q, i, 0))],
            out_specs=pl.BlockSpec((1, tm, c_), lambda q, i, core_ref: (q, i, 0))),
        name=name + "_sum", compiler_params=_params(("parallel", "parallel")))(
            core, parts.reshape(n_chip, 2, r, c_), from_sibling)

    return pair


def _sum_parts(name, parts):
    n_parts, r, c = parts.shape
    tm = _pick(r, (512, 256, 128, 64, 32, 16, 8)) if r % 8 == 0 else r

    def body(p_ref, o_ref):
        acc = p_ref[0].astype(F32)
        for d in range(1, n_parts):
            acc = acc + p_ref[d].astype(F32)
        o_ref[...] = acc

    return pl.pallas_call(body, grid=(r // tm,), in_specs=[pl.BlockSpec((n_parts, tm, c), lambda i: (0, i, 0))],
                          out_specs=pl.BlockSpec((tm, c), lambda i: (i, 0)), out_shape=S((r, c), F32), name=name,
                          compiler_params=_params(("parallel",)))(parts)


def _adamw(name, w, g, m, v, jobs=()):
    r, c = w.shape
    parts = g.ndim == 3
    n_parts = g.shape[0] if parts else 1
    tile_rows = max(8, (2 << 20) // (4 * c))
    tm = _pick(r, tuple(q for q in (2048, 1024, 512, 256, 128, 64, 32, 16, 8) if q <= tile_rows)) if r % 8 == 0 else r

    def body(w_ref, g_ref, m_ref, v_ref, go_ref, d_ref, mo_ref, vo_ref):
        if parts:
            gg = g_ref[0].astype(F32)
            for d in range(1, n_parts):
                gg = gg + g_ref[d].astype(F32)
        else:
            gg = g_ref[...]
        mm = ADAM_B1 * m_ref[...] + (1.0 - ADAM_B1) * gg
        vv = ADAM_B2 * v_ref[...] + (1.0 - ADAM_B2) * jnp.square(gg)
        m_hat = mm / (1.0 - ADAM_B1 ** ADAM_STEP)
        v_hat = vv / (1.0 - ADAM_B2 ** ADAM_STEP)
        go_ref[...] = gg
        d_ref[...] = -ADAM_LR * (m_hat / (jnp.sqrt(v_hat) + ADAM_EPS) + ADAM_WD * w_ref[...])
        mo_ref[...] = mm
        vo_ref[...] = vv

    spec = pl.BlockSpec((tm, c), lambda i: (i, 0))
    g_spec = pl.BlockSpec((n_parts, tm, c), lambda i: (0, i, 0)) if parts else spec
    return _pcall(body, grid=(r // tm,), in_specs=[spec, g_spec, spec, spec], out_specs=[spec] * 4,
                  out_shape=[S((r, c), F32)] * 4, name=name, semantics=("parallel",), inputs=(w, g, m, v), jobs=jobs)


def _rms(x, g):
    return x * lax.rsqrt(jnp.mean(x * x, axis=-1, keepdims=True) + RMS_EPS) * g


def _adaln(x, g, shift, scale):
    return _rms(x, g) * (1.0 + scale) + shift


def _dot(a, b, dn="nn", hi=False, prec=None):
    if hi or prec is not None:
        return lax.dot_general(a, b, _DN[dn], precision=HI if hi else prec, preferred_element_type=F32)
    return lax.dot_general(a.astype(BF16), b.astype(BF16), _DN[dn], preferred_element_type=F32)


def _sg_mix(p, ln_g, ln_b, w_s, b_st):
    d = p.shape[1] // 3
    gd = d // SG_GROUPS
    u = jax.nn.gelu(p[:, :d])
    vf = jax.nn.gelu(p[:, d:2 * d])
    z = p[:, 2 * d:]
    mean = jnp.mean(vf, axis=-1, keepdims=True)
    var = jnp.mean(jnp.square(vf - mean), axis=-1, keepdims=True)
    vn = (vf - mean) * lax.rsqrt(var + LN_EPS) * ln_g + ln_b
    row = lax.broadcasted_iota(jnp.int32, (SG_CHUNK, SG_CHUNK), 0)
    col = lax.broadcasted_iota(jnp.int32, (SG_CHUNK, SG_CHUNK), 1)
    fs = []
    for g in range(SG_GROUPS):
        w = jnp.where(row >= col, w_s[g], 0.0)
        fs.append(_dot(w, vn[:, g * gd:(g + 1) * gd]))
    sel = (lax.broadcasted_iota(jnp.int32, (SG_GROUPS, d), 1) // gd
           == lax.broadcasted_iota(jnp.int32, (SG_GROUPS, d), 0)).astype(F32)
    f = jnp.concatenate(fs, axis=1) + _dot(b_st, sel, hi=True)
    return u * f * jax.nn.silu(z)


def _rot_half(x):
    n = x.shape[1]
    lane = lax.broadcasted_iota(jnp.int32, x.shape, 1)
    return jnp.where(lane % HEAD < HEAD // 2, -pltpu.roll(x, n - HEAD // 2, 1), pltpu.roll(x, HEAD // 2, 1))


def _rope(x, cos, sin, sign):
    reps = x.shape[1] // cos.shape[1]
    return x * jnp.tile(cos, (1, reps)) + sign * _rot_half(x) * jnp.tile(sin, (1, reps))


def _attn_block(q, kp, kc, vp, vc, sink, prev_bias):
    each = lambda f, *ls: [f(*xs) for xs in zip(*ls)]
    r = sink[0].shape[0]
    scores = lambda a, b: (_dot(a, b, "nt") * (HEAD ** -0.5)).reshape(r, SWA_BLOCK, SWA_BLOCK)
    sp, sc = each(scores, q, kp), each(scores, q, kc)
    qi = lax.broadcasted_iota(jnp.int32, (r, SWA_BLOCK, SWA_BLOCK), 1)
    kj = lax.broadcasted_iota(jnp.int32, (r, SWA_BLOCK, SWA_BLOCK), 2)
    sp = each(lambda s: jnp.where(kj > qi, s, NEG) + prev_bias, sp)
    sc = each(lambda s: jnp.where(kj <= qi, s, NEG), sc)
    m = each(lambda a, b, s: jnp.maximum(jnp.maximum(jnp.max(a, axis=-1, keepdims=True), jnp.max(b, axis=-1, keepdims=True)), s),
             sp, sc, sink)
    ep, ec = each(lambda s, m_: jnp.exp(s - m_), sp, m), each(lambda s, m_: jnp.exp(s - m_), sc, m)
    denom = each(lambda a, b, s, m_: jnp.sum(a, axis=-1, keepdims=True) + jnp.sum(b, axis=-1, keepdims=True) + jnp.exp(s - m_),
                 ep, ec, sink, m)
    flat = lambda e, dn: (e / dn).reshape(r * SWA_BLOCK, SWA_BLOCK)
    pp, pc = each(flat, ep, denom), each(flat, ec, denom)
    return each(lambda a, va, b, vb: _dot(a, va) + _dot(b, vb), pp, vp, pc, vc)


def _attn_block_bwd(q, kp, kc, vp, vc, sink, do, prev_bias):
    each = lambda f, *ls: [f(*xs) for xs in zip(*ls)]
    r = sink[0].shape[0]
    scale = HEAD ** -0.5
    cube = lambda x: x.reshape(r, SWA_BLOCK, SWA_BLOCK)
    flat = lambda x: x.reshape(r * SWA_BLOCK, SWA_BLOCK)
    scores = lambda a, b: cube(_dot(a, b, "nt") * scale)
    sp, sc = each(scores, q, kp), each(scores, q, kc)
    qi = lax.broadcasted_iota(jnp.int32, (r, SWA_BLOCK, SWA_BLOCK), 1)
    kj = lax.broadcasted_iota(jnp.int32, (r, SWA_BLOCK, SWA_BLOCK), 2)
    sp = each(lambda s: jnp.where(kj > qi, s, NEG) + prev_bias, sp)
    sc = each(lambda s: jnp.where(kj <= qi, s, NEG), sc)
    m = each(lambda a, b, s: jnp.maximum(jnp.maximum(jnp.max(a, axis=-1, keepdims=True), jnp.max(b, axis=-1, keepdims=True)), s),
             sp, sc, sink)
    ep, ec = each(lambda s, m_: jnp.exp(s - m_), sp, m), each(lambda s, m_: jnp.exp(s - m_), sc, m)
    es = each(lambda s, m_: jnp.exp(s - m_), sink, m)
    denom = each(lambda a, b, e: jnp.sum(a, axis=-1, keepdims=True) + jnp.sum(b, axis=-1, keepdims=True) + e, ep, ec, es)
    pp, pc = each(lambda e, dn: e / dn, ep, denom), each(lambda e, dn: e / dn, ec, denom)
    dpp, dpc = each(lambda g, v_: cube(_dot(g, v_, "nt")), do, vp), each(lambda g, v_: cube(_dot(g, v_, "nt")), do, vc)
    dvp, dvc = each(lambda p, g: _dot(flat(p), g, "tn"), pp, do), each(lambda p, g: _dot(flat(p), g, "tn"), pc, do)
    delta = each(lambda a, da, b, db: jnp.sum(a * da, axis=-1, keepdims=True) + jnp.sum(b * db, axis=-1, keepdims=True),
                 pp, dpp, pc, dpc)
    dsp, dsc = each(lambda p, dp, dl: flat(p * (dp - dl)), pp, dpp, delta), each(lambda p, dp, dl: flat(p * (dp - dl)), pc, dpc, delta)
    dsink = each(lambda e, dn, dl: -jnp.sum(e / dn * dl, axis=1, keepdims=True), es, denom, delta)
    dq = each(lambda a, ka, b, kb: (_dot(a, ka) + _dot(b, kb)) * scale, dsp, kp, dsc, kc)
    dkp, dkc = each(lambda a, q_: _dot(a, q_, "tn") * scale, dsp, q), each(lambda a, q_: _dot(a, q_, "tn") * scale, dsc, q)
    return dq, dkp, dkc, dvp, dvc, dsink


def _rwkv_chunk(s0, r, k, v, logw, a, k_k, k_a, r_k, gn_g, gn_b):
    c = r[0].shape[0]
    each = lambda f, *ls: [f(*xs) for xs in zip(*ls)]
    gram = functools.partial(_dot, prec=RW_PREC)
    row = lax.broadcasted_iota(jnp.int32, (c, c), 0)
    col = lax.broadcasted_iota(jnp.int32, (c, c), 1)
    incl, strict = row >= col, row > col
    ones_l = incl.astype(F32)

    def unit(x):
        return x / jnp.maximum(jnp.sqrt(jnp.sum(x * x, axis=-1, keepdims=True)), 1e-12)

    kk = each(lambda k_, p: unit(k_ * p), k, k_k)
    km = each(lambda k_, a_, p: k_ * (1.0 + (a_ - 1.0) * p), k, a, k_a)
    b = each(lambda x, a_: x * a_, kk, a)
    first_half = lax.broadcasted_iota(jnp.int32, (c, HEAD), 0) < c // 2
    mid = each(lambda w: jnp.sum(jnp.where(first_half, w, 0.0), axis=0, keepdims=True), logw)
    cum = each(lambda w, m: _dot(ones_l, w, hi=True) - m, logw, mid)
    alpha = each(lambda x, cu, w: x * jnp.exp(cu - w), kk, cum, logw)
    beta = each(lambda x, cu: x * jnp.exp(-cu), b, cum)
    kap = each(lambda x, cu: x * jnp.exp(-cu), km, cum)
    rho = each(lambda x, cu: x * jnp.exp(cu), r, cum)
    s0 = each(lambda s, m: s * jnp.exp(m), s0, mid)
    lab = each(lambda x, y_: jnp.where(strict, gram(x, y_, "nt"), 0.0), alpha, beta)
    lak = each(lambda x, y_: jnp.where(strict, gram(x, y_, "nt"), 0.0), alpha, kap)
    xs = each(lambda al, s, l, v_: _dot(al, s, "nt") + _dot(l, v_), alpha, s0, lak, v)
    xs = each(lambda x, l: x - _dot(l, x), xs, lab)
    lp, power = lab, 2
    while power < c:
        lp = each(lambda l: _dot(l, l), lp)
        xs = each(lambda x, l: x + _dot(l, x), xs, lp)
        power *= 2
    u = each(lambda x: -x, xs)
    mrb = each(lambda x, y_: jnp.where(incl, gram(x, y_, "nt"), 0.0), rho, beta)
    mrk = each(lambda x, y_: jnp.where(incl, gram(x, y_, "nt"), 0.0), rho, kap)
    y = each(lambda rh, s, mb, u_, mk, v_: _dot(rh, s, "nt") + _dot(mb, u_) + _dot(mk, v_), rho, s0, mrb, u, mrk, v)
    s1 = each(lambda s, u_, be, v_, ka, w, m: (s + _dot(u_, be, "tn") + _dot(v_, ka, "tn"))
              * jnp.exp(jnp.sum(w, axis=0, keepdims=True) - m), s0, u, beta, v, kap, logw, mid)

    def finish(y_, g, bias, r_, km_, rk, v_):
        mean = jnp.mean(y_, axis=-1, keepdims=True)
        var = jnp.mean(jnp.square(y_ - mean), axis=-1, keepdims=True)
        y_ = (y_ - mean) * lax.rsqrt(var + GN_EPS) * g + bias
        return y_ + jnp.sum(r_ * km_ * rk, axis=-1, keepdims=True) * v_

    return each(finish, y, gn_g, gn_b, r, km, r_k, v), s1


def _norm_fwd(name, x, g, shift, scale):
    return _rows(name, lambda x_, g_, sh, sc: _adaln(x_, g_, sh, sc), [x], [g, shift, scale], [(x.shape[1], BF16)], [], 256)[0]


def _norm_bwd(name, x, dh, dx_res, g, shift, scale):
    d = x.shape[1]

    def fn(x_, dh_, dr_, g_, sh, sc):
        _, vjp = jax.vjp(_adaln, x_, g_, sh, sc)
        dx, dg, dsh, dsc = vjp(dh_)
        return dx + dr_, dg, dsh, dsc

    return _rows(name, fn, [x, dh, dx_res], [g, shift, scale], [(d, F32)], [(1, d)] * 3, 256)


def _resid_bwd(name, dx, y, gate):
    d = dx.shape[1]
    return _rows(name, lambda dx_, y_, g_: (g_ * dx_, jnp.sum(dx_ * y_, axis=0, keepdims=True)), [dx, y], [gate],
                 [(d, BF16)], [(1, d)], 256)


def _sg_fwd(name, p, ln_g, ln_b, w_s, b_st):
    d = p.shape[1] // 3
    return _rows(name, _sg_mix, [p], [ln_g, ln_b, w_s, b_st], [(d, BF16)], [], SG_CHUNK)[0]


def _sg_bwd(name, p, dmix, ln_g, ln_b, w_s, b_st, jobs=()):
    def fn(p_, dm_, lg, lb, ws, bs):
        _, vjp = jax.vjp(_sg_mix, p_, lg, lb, ws, bs)
        return vjp(dm_)

    return _rows(name, fn, [p, dmix], [ln_g, ln_b, w_s, b_st], [(p.shape[1], BF16)],
                 [ln_g.shape, ln_b.shape, w_s.shape, b_st.shape], SG_CHUNK, jobs=jobs)


def _rope_tables(pos, inv_freq):
    ang = pos * inv_freq
    return jnp.cos(ang), jnp.sin(ang)


def _swa_pre(name, p, pos, inv_freq, d):
    kvw = SWA_KV * HEAD

    def fn(p_, pos_, fr):
        cos, sin = _rope_tables(pos_, fr)
        return (_rope(p_[:, :d], cos, sin, 1.0), _rope(p_[:, d:d + kvw], cos, sin, 1.0), p_[:, d + kvw:d + 2 * kvw])

    return _rows(name, fn, [p, pos], [inv_freq], [(d, BF16), (kvw, BF16), (kvw, BF16)], [], 256)


def _q_groups(ref, kv, rep):
    heads = _head_cols(ref, kv * rep)
    return [jnp.concatenate(heads[g * rep:(g + 1) * rep], axis=0) for g in range(kv)]


def _q_ungroup(groups, rep):
    return jnp.concatenate([g[h * SWA_BLOCK:(h + 1) * SWA_BLOCK] for g in groups for h in range(rep)], axis=1)


def _swa_attn_fwd(name, q, k, v, sinks, jobs=()):
    t, d = q.shape
    kv, rep = sinks.shape[0], sinks.shape[1]
    nb = t // SWA_BLOCK

    def body(q_ref, kp_ref, kc_ref, vp_ref, vc_ref, s_ref, o_ref):
        prev_bias = jnp.where(pl.program_id(0) > 0, 0.0, NEG).astype(F32)
        o = _attn_block(_q_groups(q_ref, kv, rep), *[_head_cols(ref, kv) for ref in (kp_ref, kc_ref, vp_ref, vc_ref)],
                        [s_ref[g] for g in range(kv)], prev_bias)
        o_ref[...] = _q_ungroup(o, rep)

    qs = pl.BlockSpec((SWA_BLOCK, d), lambda n: (n, 0))
    cur = pl.BlockSpec((SWA_BLOCK, kv * HEAD), lambda n: (n, 0))
    prev = pl.BlockSpec((SWA_BLOCK, kv * HEAD), lambda n: (jnp.maximum(n - 1, 0), 0))
    ss = pl.BlockSpec(sinks.shape, lambda n: (0, 0, 0, 0))
    return _pcall(body, grid=(nb,), in_specs=[qs, prev, cur, prev, cur, ss], out_specs=[qs], out_shape=[S((t, d), F32)],
                  name=name, semantics=("parallel",), inputs=(q, k, k, v, v, sinks), jobs=jobs)[0]


def _swa_attn_bwd(name, q, k, v, sinks, do, jobs=()):
    t, d = q.shape
    kv, rep = sinks.shape[0], sinks.shape[1]
    nb = t // SWA_BLOCK

    def body(q_ref, kp_ref, kc_ref, vp_ref, vc_ref, s_ref, do_ref, dq_ref, dkc_ref, dkp_ref, dvc_ref, dvp_ref, ds_ref):
        n = pl.program_id(0)
        prev_bias = jnp.where(n > 0, 0.0, NEG).astype(F32)
        args = [_q_groups(q_ref, kv, rep)] + [_head_cols(ref, kv) for ref in (kp_ref, kc_ref, vp_ref, vc_ref)]
        dq, dkp, dkc, dvp, dvc, ds = _attn_block_bwd(*args, [s_ref[g] for g in range(kv)], _q_groups(do_ref, kv, rep), prev_bias)
        dq_ref[...] = _q_ungroup(dq, rep)
        for ref, val in ((dkc_ref, dkc), (dkp_ref, dkp), (dvc_ref, dvc), (dvp_ref, dvp)):
            ref[...] = jnp.concatenate(val, axis=1)

        @pl.when(n == 0)
        def _():
            ds_ref[...] = jnp.zeros_like(ds_ref)

        for g in range(kv):
            ds_ref[g] += ds[g]

    qs = pl.BlockSpec((SWA_BLOCK, d), lambda n: (n, 0))
    cur = pl.BlockSpec((SWA_BLOCK, kv * HEAD), lambda n: (n, 0))
    prev = pl.BlockSpec((SWA_BLOCK, kv * HEAD), lambda n: (jnp.maximum(n - 1, 0), 0))
    ss = pl.BlockSpec(sinks.shape, lambda n: (0, 0, 0, 0))
    return _pcall(body, grid=(nb,), in_specs=[qs, prev, cur, prev, cur, ss, qs], out_specs=[qs, cur, cur, cur, cur, ss],
                  out_shape=[S((t, d), F32)] + [S((t, kv * HEAD), F32)] * 4 + [S(sinks.shape, F32)], name=name,
                  semantics=("arbitrary",), inputs=(q, k, k, v, v, sinks, do), jobs=jobs)


def _gate_fwd(name, o, z_src, z_off, d):
    return _rows(name, lambda o_, p_: o_ * jax.nn.silu(p_[:, z_off:z_off + d]), [o, z_src], [], [(d, BF16)], [], 256)[0]


def _gate_bwd(name, o, z_src, z_off, d, dmix):
    def fn(o_, p_, dm_):
        _, vjp = jax.vjp(lambda oo, zz: oo * jax.nn.silu(zz), o_, p_[:, z_off:z_off + d])
        return vjp(dm_)

    return _rows(name, fn, [o, z_src, dmix], [], [(d, F32), (d, F32)], [], 256)


def _swa_post_bwd(name, dq, dkc, dkp_up, dvc, dvp_up, dz, pos, inv_freq):
    def fn(dq_, dkc_, dkp_, dvc_, dvp_, dz_, pos_, fr):
        cos, sin = _rope_tables(pos_, fr)
        return jnp.concatenate([_rope(dq_, cos, sin, -1.0), _rope(dkc_ + dkp_, cos, sin, -1.0), dvc_ + dvp_, dz_], axis=1)

    n = dq.shape[1] + dkc.shape[1] + dvc.shape[1] + dz.shape[1]
    return _rows(name, fn, [dq, dkc, dkp_up, dvc, dvp_up, dz, pos], [inv_freq], [(n, BF16)], [], 256)[0]


HALO = 8


def _row_before(x, halo_ref, i):
    first = jnp.where(i > 0, halo_ref[pl.ds(HALO - 1, 1), :], 0.0)
    row = lax.broadcasted_iota(jnp.int32, x.shape, 0)
    return jnp.where(row == 0, first, pltpu.roll(x, 1, 0))


def _row_after(x, halo, i, n_tiles):
    last = jnp.where(i < n_tiles - 1, halo, 0.0)
    row = lax.broadcasted_iota(jnp.int32, x.shape, 0)
    return jnp.where(row == x.shape[0] - 1, last, pltpu.roll(x, x.shape[0] - 1, 0))


def _lerp_fwd(name, p, mu, widths):
    t, n = p.shape
    tm = 128

    def body(p_ref, halo_ref, mu_ref, *o_refs):
        x = p_ref[...]
        pm = x + (_row_before(x, halo_ref, pl.program_id(0)) - x) * mu_ref[...]
        o = 0
        for ref, w in zip(o_refs, widths):
            ref[...] = pm[:, o:o + w]
            o += w

    return pl.pallas_call(
        body, grid=(t // tm,),
        in_specs=[pl.BlockSpec((tm, n), lambda i: (i, 0)),
                  pl.BlockSpec((HALO, n), lambda i: (jnp.maximum(i * (tm // HALO) - 1, 0), 0)),
                  pl.BlockSpec((1, n), lambda i: (0, 0))],
        out_specs=[pl.BlockSpec((tm, w), lambda i: (i, 0)) for w in widths],
        out_shape=[S((t, w), F32) for w in widths], name=name, compiler_params=_params(("parallel",)))(p, p, mu)


def _lerp_bwd(name, dpm_parts, p, mu):
    t, n = p.shape
    k = len(dpm_parts)
    tm = 64
    n_tiles = t // tm

    def body(*refs):
        d_refs, dh_refs = refs[:k], refs[k:2 * k]
        p_ref, ph_ref, mu_ref, dp_ref, dmu_ref = refs[2 * k:]
        i = pl.program_id(0)
        cat = lambda vals: jnp.concatenate(vals, axis=1) if k > 1 else vals[0]
        dpm = cat([r[...] for r in d_refs])
        dnext = cat([r[pl.ds(0, 1), :] for r in dh_refs])
        x, mu_ = p_ref[...], mu_ref[...]
        dp_ref[...] = (dpm * (1.0 - mu_) + _row_after(dpm, dnext, i, n_tiles) * mu_).astype(dp_ref.dtype)
        dmu = jnp.sum(dpm * (_row_before(x, ph_ref, i) - x), axis=0, keepdims=True)

        @pl.when(i == 0)
        def _():
            dmu_ref[...] = dmu

        @pl.when(i > 0)
        def _():
            dmu_ref[...] += dmu

    per = tm // HALO
    d_specs = [pl.BlockSpec((tm, a.shape[1]), lambda i: (i, 0)) for a in dpm_parts]
    dh_specs = [pl.BlockSpec((HALO, a.shape[1]), lambda i: (jnp.minimum((i + 1) * per, t // HALO - 1), 0)) for a in dpm_parts]
    return pl.pallas_call(
        body, grid=(n_tiles,),
        in_specs=d_specs + dh_specs + [pl.BlockSpec((tm, n), lambda i: (i, 0)),
                                       pl.BlockSpec((HALO, n), lambda i: (jnp.maximum(i * per - 1, 0), 0)),
                                       pl.BlockSpec((1, n), lambda i: (0, 0))],
        out_specs=[pl.BlockSpec((tm, n), lambda i: (i, 0)), pl.BlockSpec((1, n), lambda i: (0, 0))],
        out_shape=[S((t, n), BF16), S((1, n), F32)], name=name,
        compiler_params=_params(("arbitrary",)))(*dpm_parts, *dpm_parts, p, p, mu)


def _lora_act(pl_, w0, w_lora, a0, a_lora):
    logw = -DECAY_SCALE * jax.nn.sigmoid(w0 + _dot(jnp.tanh(pl_), w_lora))
    a = jax.nn.sigmoid(a0 + _dot(pl_, a_lora))
    return logw, a


def _lora_fwd(name, pl_, w0, w_lora, a0, a_lora):
    d = w0.shape[1]
    return _rows(name, _lora_act, [pl_], [w0, w_lora, a0, a_lora], [(d, F32), (d, F32)], [], 256)


def _lora_bwd(name, pl_, dlogw, da, w0, w_lora, a0, a_lora):
    def fn(p_, dl_, da_, w0_, wl_, a0_, al_):
        _, vjp = jax.vjp(_lora_act, p_, w0_, wl_, a0_, al_)
        return vjp((dl_, da_))

    return _rows(name, fn, [pl_, dlogw, da], [w0, w_lora, a0, a_lora], [(pl_.shape[1], F32)],
                 [w0.shape, w_lora.shape, a0.shape, a_lora.shape], 256)


def _head_cols(ref, hb):
    x = ref[...].astype(F32)
    xo = pltpu.roll(x, x.shape[1] - HEAD, 1)
    return [(x if j % 2 == 0 else xo)[:, 2 * HEAD * (j // 2):2 * HEAD * (j // 2) + HEAD] for j in range(hb)]


def _rwkv_scan_fwd(name, r, k, v, logw, a, hp, jobs=()):
    t, d = r.shape
    h, nc, hb = d // HEAD, t // RW_CHUNK, RW_HEADS_FWD

    def body(r_ref, k_ref, v_ref, w_ref, a_ref, kk_ref, ka_ref, rk_ref, gg_ref, gb_ref, y_ref, st_ref, s_scr):
        @pl.when(pl.program_id(1) == 0)
        def _():
            s_scr[...] = jnp.zeros_like(s_scr)

        s0 = [s_scr[j] for j in range(hb)]
        for j in range(hb):
            st_ref[j, 0] = s0[j]
        y, s1 = _rwkv_chunk(s0, *[_head_cols(ref, hb) for ref in (r_ref, k_ref, v_ref, w_ref, a_ref, kk_ref, ka_ref, rk_ref,
                                                                 gg_ref, gb_ref)])
        y_ref[...] = jnp.concatenate(y, axis=1)
        for j in range(hb):
            s_scr[j] = s1[j]

    seq = pl.BlockSpec((RW_CHUNK, hb * HEAD), lambda i, n: (n, i))
    par = pl.BlockSpec((1, hb * HEAD), lambda i, n: (0, i))
    st = pl.BlockSpec((hb, 1, HEAD, HEAD), lambda i, n: (i, n, 0, 0))
    return _pcall(body, grid=(h // hb, nc), in_specs=[seq] * 5 + [par] * 5, out_specs=[seq, st],
                  out_shape=[S((t, d), F32), S((h, nc, HEAD, HEAD), F32)], scratch_shapes=[pltpu.VMEM((hb, HEAD, HEAD), F32)],
                  name=name, semantics=("parallel", "arbitrary"), inputs=(r, k, v, logw, a, *hp), jobs=jobs)


def _rwkv_scan_bwd(name, r, k, v, logw, a, hp, states, dy, jobs=()):
    t, d = r.shape
    h, nc, hb = d // HEAD, t // RW_CHUNK, RW_HEADS

    def body(r_ref, k_ref, v_ref, w_ref, a_ref, kk_ref, ka_ref, rk_ref, gg_ref, gb_ref, st_ref, dy_ref,
             dr_ref, dk_ref, dv_ref, dw_ref, da_ref, dkk_ref, dka_ref, drk_ref, dgg_ref, dgb_ref, ds_scr):
        n = pl.program_id(1)

        @pl.when(n == 0)
        def _():
            ds_scr[...] = jnp.zeros_like(ds_scr)
            for ref in (dkk_ref, dka_ref, drk_ref, dgg_ref, dgb_ref):
                ref[...] = jnp.zeros_like(ref)

        ins = [[st_ref[j, 0] for j in range(hb)]] + [_head_cols(ref, hb) for ref in (r_ref, k_ref, v_ref, w_ref, a_ref, kk_ref,
                                                                                  ka_ref, rk_ref, gg_ref, gb_ref)]
        _, vjp = jax.vjp(_rwkv_chunk, *ins)
        ds0, *dseq, dkk, dka, drk, dgg, dgb = vjp((_head_cols(dy_ref, hb), [ds_scr[j] for j in range(hb)]))
        for j in range(hb):
            ds_scr[j] = ds0[j]
        for ref, val in zip((dr_ref, dk_ref, dv_ref, dw_ref, da_ref), dseq):
            ref[...] = jnp.concatenate(val, axis=1)
        for ref, val in ((dkk_ref, dkk), (dka_ref, dka), (drk_ref, drk), (dgg_ref, dgg), (dgb_ref, dgb)):
            ref[...] += jnp.concatenate(val, axis=1)

    seq = pl.BlockSpec((RW_CHUNK, hb * HEAD), lambda i, n: (nc - 1 - n, i))
    par = pl.BlockSpec((1, hb * HEAD), lambda i, n: (0, i))
    st = pl.BlockSpec((hb, 1, HEAD, HEAD), lambda i, n: (i, nc - 1 - n, 0, 0))
    return _pcall(body, grid=(h // hb, nc), in_specs=[seq] * 5 + [par] * 5 + [st, seq], out_specs=[seq] * 5 + [par] * 5,
                  out_shape=[S((t, d), F32)] * 5 + [S((1, d), F32)] * 5, scratch_shapes=[pltpu.VMEM((hb, HEAD, HEAD), F32)],
                  name=name, semantics=("parallel", "arbitrary"), inputs=(r, k, v, logw, a, *hp, states, dy), jobs=jobs)


def _loss_head(name, x, target, g):
    d = x.shape[1]

    def fn(x_, t_, g_):
        def f(xx, gg):
            err = _rms(xx, gg) - t_
            return 0.5 * jnp.sum(jnp.mean(err * err, axis=-1, keepdims=True), axis=0, keepdims=True)

        l, vjp = jax.vjp(f, x_, g_)
        dx, dg = vjp(jnp.ones((1, 1), F32))
        return dx, dg, jnp.broadcast_to(l, (1, 128))

    return _rows(name, fn, [x, target], [g], [(d, F32)], [(1, d), (1, 128)], 256)


def _mod_fwd(name, cond_all, mod_w, mod_b_cols):
    l, d, n = mod_w.shape

    def body(c_ref, w_ref, b_ref, o_ref):
        o_ref[0] = _dot(jax.nn.silu(c_ref[...]), w_ref[0], hi=True) + b_ref[0]

    return pl.pallas_call(body, grid=(l,), in_specs=[pl.BlockSpec((N_DEV, d), lambda i: (0, 0)),
                                                      pl.BlockSpec((1, d, n), lambda i: (i, 0, 0)),
                                                      pl.BlockSpec((1, 1, n), lambda i: (i, 0, 0))],
                          out_specs=pl.BlockSpec((1, N_DEV, n), lambda i: (i, 0, 0)), out_shape=S((l, N_DEV, n), F32),
                          name=name, compiler_params=_params(("parallel",)))(cond_all, mod_w, mod_b_cols)


def _mod_bwd(name, cond_all, dmod_cols, dmod_all):
    l, _, n = dmod_cols.shape
    d = cond_all.shape[1]
    nb = dmod_all.shape[2]

    def body(c_ref, dc_ref, da_ref, gw_ref, gb_ref):
        gw_ref[0] = _dot(jax.nn.silu(c_ref[...]), dc_ref[0], "tn", hi=True)
        acc = da_ref[0, 0:1, :]
        for bi in range(1, N_DEV):
            acc = acc + da_ref[0, bi:bi + 1, :]
        gb_ref[0] = acc

    return pl.pallas_call(body, grid=(l,), in_specs=[pl.BlockSpec((N_DEV, d), lambda i: (0, 0)),
                                                      pl.BlockSpec((1, N_DEV, n), lambda i: (i, 0, 0)),
                                                      pl.BlockSpec((1, N_DEV, nb), lambda i: (i, 0, 0))],
                          out_specs=[pl.BlockSpec((1, d, n), lambda i: (i, 0, 0)), pl.BlockSpec((1, 1, nb), lambda i: (i, 0, 0))],
                          out_shape=[S((l, d, n), F32), S((l, 1, nb), F32)], name=name,
                          compiler_params=_params(("parallel",)))(cond_all, dmod_cols, dmod_all)


def _shift_up(a, n=1):
    return jnp.concatenate([a[n:], jnp.zeros_like(a[:n])], axis=0)


def _cols_full(g):
    return g.transpose(1, 0, 2).reshape(g.shape[1], -1)


def _cols_parts(full):
    r, n = full.shape
    return full.reshape(r, N_DEV, n // N_DEV).transpose(1, 0, 2)


def _pack(arrs, mult=1024):
    flat = jnp.concatenate([a.reshape(-1) for a in arrs])
    pad = (-flat.shape[0]) % mult
    return jnp.pad(flat, (0, pad)).reshape(-1, 128)


def _unpack(flat, shapes):
    out, o = [], 0
    for s in shapes:
        n = math.prod(s)
        out.append(flat[o:o + n].reshape(s))
        o += n
    return out


def _local_step(x, pos, target, mods, norm_g, final_norm_g, layer_weights, hooks=None, on_grads=None):
    t, d = x.shape
    hooks = hooks or {}
    jobs = lambda nm: hooks.get(nm, ())
    notify = on_grads or (lambda *a: None)
    kinds = [i % 3 for i in range(DEPTH)]
    inv_freq = (ROPE_THETA ** (-jnp.arange(HEAD // 2, dtype=F32) / (HEAD // 2)))
    inv_freq = jnp.tile(inv_freq, 128 // (HEAD // 2)).reshape(1, 128)
    saved = []
    for i, kind in enumerate(kinds):
        lw = layer_weights(i)
        shift, scale, gate = (mods[i, q * d:(q + 1) * d].reshape(1, d) for q in range(3))
        g = norm_g[i].reshape(1, d)
        h = _norm_fwd(f"norm_fwd{i}", x, g, shift, scale)
        sv = dict(x=x, h=h, g=g, shift=shift, scale=scale, gate=gate, lw=lw)
        if kind == 0:
            p = _mm(f"sg_in{i}", h, lw["w_in"], "nn", F32, jobs=jobs(f"sg_in{i}"))
            mix = _sg_fwd(f"sg_mix{i}", p, lw["ln_g"], lw["ln_b"], lw["w_s"], lw["b_st"])
            sv.update(p=p)
        elif kind == 1:
            p = _mm(f"swa_in{i}", h, lw["w_in"], "nn", F32, jobs=jobs(f"swa_in{i}"))
            q, k, v = _swa_pre(f"swa_pre{i}", p, pos, inv_freq, d)
            o = _swa_attn_fwd(f"swa_attn{i}", q, k, v, lw["sinks"], jobs=jobs(f"swa_attn{i}"))
            mix = _gate_fwd(f"swa_gate{i}", o, p, d + 2 * SWA_KV * HEAD, d)
            sv.update(p=p, qkv=(q, k, v), o=o)
        else:
            pm = _mm(f"rw_in{i}", h, lw["w_main"], "nn", F32)
            plo = _mm(f"rw_inl{i}", h, lw["w_lorain"], "nn", F32)
            r, k, v, z = _lerp_fwd(f"rw_lerp{i}", pm, lw["mu_main"], [d] * 4)
            (pll,) = _lerp_fwd(f"rw_lerpl{i}", plo, lw["mu_lora"], [LORA_PAD])
            logw, a = _lora_fwd(f"rw_lora{i}", pll, lw["w0"], lw["w_lora"], lw["a0"], lw["a_lora"])
            seqs = (r, k, v, logw, a)
            o, states = _rwkv_scan_fwd(f"rw_scan{i}", *seqs, lw["hp"], jobs=jobs(f"rw_scan{i}"))
            mix = _gate_fwd(f"rw_gate{i}", o, z, 0, d)
            sv.update(pm=pm, plo=plo, pll=pll, z=z, seqs=seqs, states=states, o=o)
        y, x = _mm(f"out{i}", mix, lw["w_out"], "nn", F32, resid=(x, gate), jobs=jobs(f"out{i}"))
        sv.update(mix=mix, y=y)
        saved.append(sv)

    dx, d_final_g, loss = _loss_head("loss_head", x, target, final_norm_g.reshape(1, d))

    grads = dict(norm_g=[None] * DEPTH, sg_w_in=[None] * 2, sg_w_out=[None] * 2, sg_ln_g=[None] * 2, sg_ln_b=[None] * 2,
                 sg_w_s=[None] * 2, sg_b_st=[None] * 2, final_norm_g=d_final_g)
    dmods = [None] * DEPTH
    for i in reversed(range(DEPTH)):
        kind, j, sv = kinds[i], i // 3, saved[i]
        lw = sv["lw"]
        dy, dgate = _resid_bwd(f"resid_bwd{i}", dx, sv["y"], sv["gate"])
        d_w_out = _mm(f"out_dw{i}", sv["mix"], dy, "tn", BF16)
        if kind == 0:
            grads["sg_w_out"][j] = d_w_out
        else:
            grads[("swa_w_out", "rw_w_out")[kind - 1]] = d_w_out
        notify(i, "out", grads)
        dmix = _mm(f"out_dx{i}", dy, lw["w_out"], "nt", F32, jobs=jobs(f"out_dx{i}"))
        if kind == 0:
            dp, dlg, dlb, dws, dbs = _sg_bwd(f"sg_mix_bwd{i}", sv["p"], dmix, lw["ln_g"], lw["ln_b"], lw["w_s"], lw["b_st"],
                                            jobs=jobs(f"sg_mix_bwd{i}"))
            grads["sg_ln_g"][j], grads["sg_ln_b"][j], grads["sg_w_s"][j], grads["sg_b_st"][j] = dlg, dlb, dws, dbs
            grads["sg_w_in"][j] = _mm(f"sg_in_dw{i}", sv["h"], dp, "tn", BF16, jobs=jobs(f"sg_in_dw{i}"))
            notify(i, "in", grads)
            dh = _mm(f"sg_in_dx{i}", dp, lw["w_in"], "nt", F32, jobs=jobs(f"sg_in_dx{i}"))
        elif kind == 1:
            z_off = d + 2 * SWA_KV * HEAD
            do, dz = _gate_bwd(f"swa_gate_bwd{i}", sv["o"], sv["p"], z_off, d, dmix)
            dq, dkc, dkp, dvc, dvp, dsinks = _swa_attn_bwd(f"swa_attn_bwd{i}", *sv["qkv"], lw["sinks"], do,
                                                           jobs=jobs(f"swa_attn_bwd{i}"))
            dkp, dvp = _shift_up(dkp, SWA_BLOCK), _shift_up(dvp, SWA_BLOCK)
            dp = _swa_post_bwd(f"swa_post_bwd{i}", dq, dkc, dkp, dvc, dvp, dz, pos, inv_freq)
            grads.update(swa_sinks=dsinks, swa_w_out=d_w_out)
            grads["swa_w_in"] = _mm(f"swa_in_dw{i}", sv["h"], dp, "tn", BF16)
            dh = _mm(f"swa_in_dx{i}", dp, lw["w_in"], "nt", F32)
        else:
            do, dz = _gate_bwd(f"rw_gate_bwd{i}", sv["o"], sv["z"], 0, d, dmix)
            res = _rwkv_scan_bwd(f"rw_scan_bwd{i}", *sv["seqs"], lw["hp"], sv["states"], do, jobs=jobs(f"rw_scan_bwd{i}"))
            dr, dk, dv, dlogw, da = res[:5]
            dpll, dw0, dwl, da0, dal = _lora_bwd(f"rw_lora_bwd{i}", sv["pll"], dlogw, da, lw["w0"], lw["w_lora"],
                                                  lw["a0"], lw["a_lora"])
            dpm, dmu_main = _lerp_bwd(f"rw_lerp_bwd{i}", [dr, dk, dv, dz], sv["pm"], lw["mu_main"])
            dpl, dmu_lora = _lerp_bwd(f"rw_lerpl_bwd{i}", [dpll], sv["plo"], lw["mu_lora"])
            grads.update(rw_w_out=d_w_out, rw_hp=res[5:], rw_w0=dw0, rw_w_lora=dwl, rw_a0=da0, rw_a_lora=dal,
                         rw_mu_main=dmu_main, rw_mu_lora=dmu_lora)
            grads["rw_w_main"] = _mm(f"rw_in_dw{i}", sv["h"], dpm, "tn", BF16)
            grads["rw_w_lorain"] = _mm(f"rw_inl_dw{i}", sv["h"], dpl, "tn", BF16)
            dh = _mm(f"rw_inl_dx{i}", dpl, lw["w_lorain"], "nt", F32)
            dh = _mm(f"rw_in_dx{i}", dpm, lw["w_main"], "nt", F32, add=dh)
        if kind != 0:
            notify(i, "in", grads)
        dx, dg, dshift, dscale = _norm_bwd(f"norm_bwd{i}", sv["x"], dh, dx, sv["g"], sv["shift"], sv["scale"])
        grads["norm_g"][i] = dg
        dmods[i] = jnp.concatenate([dshift, dscale, dgate], axis=1)
    return loss, dx, jnp.concatenate(dmods, axis=0), grads


def kernel(x, c, positions, norm_g, mod_w, mod_b, final_norm_g, sg_w_in, sg_w_out, sg_ln_g, sg_ln_b, sg_w_spatial, sg_b_spatial, swa_w_in, swa_w_out, swa_sinks, rwkv_w_in, rwkv_w_out, rwkv_mu, rwkv_w0, rwkv_w_lora, rwkv_a0, rwkv_a_lora, rwkv_k_k, rwkv_k_a, rwkv_r_k, rwkv_gn_g, rwkv_gn_b, loss_target, m_norm_g, m_mod_w, m_mod_b, m_final_norm_g, m_sg_w_in, m_sg_w_out, m_sg_ln_g, m_sg_ln_b, m_sg_w_spatial, m_sg_b_spatial, m_swa_w_in, m_swa_w_out, m_swa_sinks, m_rwkv_w_in, m_rwkv_w_out, m_rwkv_mu, m_rwkv_w0, m_rwkv_w_lora, m_rwkv_a0, m_rwkv_a_lora, m_rwkv_k_k, m_rwkv_k_a, m_rwkv_r_k, m_rwkv_gn_g, m_rwkv_gn_b, v_norm_g, v_mod_w, v_mod_b, v_final_norm_g, v_sg_w_in, v_sg_w_out, v_sg_ln_g, v_sg_ln_b, v_sg_w_spatial, v_sg_b_spatial, v_swa_w_in, v_swa_w_out, v_swa_sinks, v_rwkv_w_in, v_rwkv_w_out, v_rwkv_mu, v_rwkv_w0, v_rwkv_w_lora, v_rwkv_a0, v_rwkv_a_lora, v_rwkv_k_k, v_rwkv_k_a, v_rwkv_r_k, v_rwkv_gn_g, v_rwkv_gn_b):
    weights = dict(norm_g=norm_g, mod_w=mod_w, mod_b=mod_b, final_norm_g=final_norm_g, sg_w_in=sg_w_in, sg_w_out=sg_w_out,
                   sg_ln_g=sg_ln_g, sg_ln_b=sg_ln_b, sg_w_spatial=sg_w_spatial, sg_b_spatial=sg_b_spatial, swa_w_in=swa_w_in,
                   swa_w_out=swa_w_out, swa_sinks=swa_sinks, rwkv_w_in=rwkv_w_in, rwkv_w_out=rwkv_w_out, rwkv_mu=rwkv_mu,
                   rwkv_w0=rwkv_w0, rwkv_w_lora=rwkv_w_lora, rwkv_a0=rwkv_a0, rwkv_a_lora=rwkv_a_lora, rwkv_k_k=rwkv_k_k,
                   rwkv_k_a=rwkv_k_a, rwkv_r_k=rwkv_r_k, rwkv_gn_g=rwkv_gn_g, rwkv_gn_b=rwkv_gn_b)
    mom_m = dict(norm_g=m_norm_g, mod_w=m_mod_w, mod_b=m_mod_b, final_norm_g=m_final_norm_g, sg_w_in=m_sg_w_in,
                 sg_w_out=m_sg_w_out, sg_ln_g=m_sg_ln_g, sg_ln_b=m_sg_ln_b, sg_w_spatial=m_sg_w_spatial,
                 sg_b_spatial=m_sg_b_spatial, swa_w_in=m_swa_w_in, swa_w_out=m_swa_w_out, swa_sinks=m_swa_sinks,
                 rwkv_w_in=m_rwkv_w_in, rwkv_w_out=m_rwkv_w_out, rwkv_mu=m_rwkv_mu, rwkv_w0=m_rwkv_w0,
                 rwkv_w_lora=m_rwkv_w_lora, rwkv_a0=m_rwkv_a0, rwkv_a_lora=m_rwkv_a_lora, rwkv_k_k=m_rwkv_k_k,
                 rwkv_k_a=m_rwkv_k_a, rwkv_r_k=m_rwkv_r_k, rwkv_gn_g=m_rwkv_gn_g, rwkv_gn_b=m_rwkv_gn_b)
    mom_v = dict(norm_g=v_norm_g, mod_w=v_mod_w, mod_b=v_mod_b, final_norm_g=v_final_norm_g, sg_w_in=v_sg_w_in,
                 sg_w_out=v_sg_w_out, sg_ln_g=v_sg_ln_g, sg_ln_b=v_sg_ln_b, sg_w_spatial=v_sg_w_spatial,
                 sg_b_spatial=v_sg_b_spatial, swa_w_in=v_swa_w_in, swa_w_out=v_swa_w_out, swa_sinks=v_swa_sinks,
                 rwkv_w_in=v_rwkv_w_in, rwkv_w_out=v_rwkv_w_out, rwkv_mu=v_rwkv_mu, rwkv_w0=v_rwkv_w0,
                 rwkv_w_lora=v_rwkv_w_lora, rwkv_a0=v_rwkv_a0, rwkv_a_lora=v_rwkv_a_lora, rwkv_k_k=v_rwkv_k_k,
                 rwkv_k_a=v_rwkv_k_a, rwkv_r_k=v_rwkv_r_k, rwkv_gn_g=v_rwkv_gn_g, rwkv_gn_b=v_rwkv_gn_b)
    names = list(weights)
    t, d = x.shape[1], x.shape[2]
    me = 4 * lax.axis_index("x") + 2 * lax.axis_index("y") + lax.axis_index("c")
    n_mod = mod_w.shape[2]
    n_rw = rwkv_w_in.shape[2]

    small_names = ["sg_ln_g", "sg_ln_b", "rwkv_mu", "rwkv_w0", "rwkv_a0", "rwkv_k_k", "rwkv_k_a", "rwkv_gn_g", "rwkv_gn_b",
                   "rwkv_w_lora", "rwkv_a_lora"]
    small_shapes = [weights[n].shape for n in small_names]
    pk = _pack([c] + [weights[n] for n in small_names])
    gathered = _gather("gather_small", pk).reshape(N_DEV, -1)
    c_all = gathered[:, :d]
    per_dev = [_unpack(gathered[dv, d:], small_shapes) for dv in range(N_DEV)]
    full_small = {}
    for q, n in enumerate(small_names):
        full_small[n] = jnp.concatenate([per_dev[dv][q] for dv in range(N_DEV)], axis=-1)

    mod_b_cols = lax.dynamic_slice_in_dim(mod_b, me * n_mod, n_mod, axis=1).reshape(DEPTH, 1, n_mod)
    mod_part = _mod_fwd("mod_fwd", c_all, mod_w, mod_b_cols)
    mod_g = _gather("gather_mod", mod_part.reshape(DEPTH * N_DEV, n_mod))
    mod_g = mod_g.reshape(N_DEV, DEPTH, N_DEV, n_mod)
    mods = lax.dynamic_index_in_dim(mod_g, me, axis=2, keepdims=False)
    mods = mods.transpose(1, 0, 2).reshape(DEPTH, N_DEV * n_mod)

    job = lambda cls, src: dict(cls=cls, src=src)
    gj = dict(swa_in=job(_Gather, swa_w_in[0].astype(BF16)), swa_out=job(_Gather, swa_w_out[0].astype(BF16)),
              rw_in=job(_Gather, rwkv_w_in[0].astype(BF16)), rw_out=job(_Gather, rwkv_w_out[0].astype(BF16)),
              sg_in1=job(_Gather, sg_w_in[1].astype(BF16)), sg_out1=job(_Gather, sg_w_out[1].astype(BF16)))
    hooks = {"sg_in0": [gj["swa_in"]], "out0": [gj["swa_out"]], "swa_in1": [gj["rw_out"]], "swa_attn1": [gj["rw_in"]],
             "rw_scan2": [gj["sg_in1"], gj["sg_out1"]]}
    g_sg_in0 = _gather("gather_sg_in0", sg_w_in[0].astype(BF16))
    g_sg_out0 = _gather("gather_sg_out0", sg_w_out[0].astype(BF16))
    lora_rows = lambda w, off: jnp.zeros((LORA_PAD, d), F32).at[off:off + LORA].set(w)
    mu = full_small["rwkv_mu"].reshape(1, -1)
    heads = lambda a: a.reshape(1, -1)

    def layer_weights(i):
        if i % 3 == 0:
            j = i // 3
            g_in, g_out = (g_sg_in0, g_sg_out0) if j == 0 else (gj["sg_in1"]["out"], gj["sg_out1"]["out"])
            return dict(w_in=_cols_full(g_in), w_out=g_out.reshape(d, d), ln_g=full_small["sg_ln_g"][j].reshape(1, d),
                        ln_b=full_small["sg_ln_b"][j].reshape(1, d), w_s=sg_w_spatial[j], b_st=sg_b_spatial[j].T)
        if i % 3 == 1:
            return dict(w_in=_cols_full(gj["swa_in"]["out"]), w_out=gj["swa_out"]["out"].reshape(d, d),
                        sinks=swa_sinks.reshape(SWA_KV, SWA_REP, 1, 1))
        rw_in_full = _cols_full(gj["rw_in"]["out"])
        return dict(w_main=rw_in_full[:, :4 * d], w_lorain=jnp.pad(rw_in_full[:, 4 * d:], ((0, 0), (0, LORA_PAD - 2 * LORA))),
                    w_out=gj["rw_out"]["out"].reshape(d, d), mu_main=mu[:, :4 * d],
                    mu_lora=jnp.pad(mu[:, 4 * d:], ((0, 0), (0, LORA_PAD - 2 * LORA))),
                    w0=full_small["rwkv_w0"], a0=full_small["rwkv_a0"],
                    w_lora=lora_rows(full_small["rwkv_w_lora"][0], 0), a_lora=lora_rows(full_small["rwkv_a_lora"][0], LORA),
                    hp=[heads(full_small["rwkv_k_k"]), heads(full_small["rwkv_k_a"]), heads(rwkv_r_k),
                        heads(full_small["rwkv_gn_g"]), heads(full_small["rwkv_gn_b"])])

    sj = {}

    def on_grads(i, which, g):
        def stage(nm, parts, host):
            sj[nm] = job(_Chips, _scatter_pairs("scatter_" + nm, parts.astype(BF16)))
            hooks.setdefault(host, []).append(sj[nm])

        rows_of = lambda a: a.reshape(N_DEV, -1, d)
        if (i, which) == (3, "out"):
            stage("sg_out1", rows_of(g["sg_w_out"][1]), "rw_scan_bwd2")
        elif (i, which) == (3, "in"):
            stage("sg_in1", _cols_parts(g["sg_w_in"][1]), "rw_scan_bwd2")
        elif (i, which) == (2, "out"):
            stage("rw_out", rows_of(g["rw_w_out"]), "swa_attn_bwd1")
        elif (i, which) == (2, "in"):
            d_rw_in = jnp.concatenate([g["rw_w_main"], g["rw_w_lorain"][:, :2 * LORA]], axis=1)
            stage("rw_in", _cols_parts(d_rw_in), "swa_attn_bwd1")
        elif (i, which) == (1, "out"):
            stage("swa_out", rows_of(g["swa_w_out"]), "out_dx0")
        elif (i, which) == (1, "in"):
            stage("swa_in", _cols_parts(g["swa_w_in"]), "sg_mix_bwd0")
        elif (i, which) == (0, "out"):
            stage("sg_out0", rows_of(g["sg_w_out"][0]), "sg_in_dw0")
        else:
            stage("sg_in0", _cols_parts(g["sg_w_in"][0]), "sg_in_dx0")

    loss, dx, dmods, g = _local_step(x[0], positions.reshape(t, 1).astype(F32), loss_target[0], mods, norm_g, final_norm_g,
                                     layer_weights, hooks, on_grads)

    dmod_g = _gather("gather_dmod", dmods)
    dmod_all = dmod_g.transpose(1, 0, 2)
    dmod_cols = lax.dynamic_slice_in_dim(dmod_all, me * n_mod, n_mod, axis=2)
    g_mod_w, g_mod_b = _mod_bwd("mod_bwd", c_all, dmod_cols, dmod_all)

    d_b_sp = [g["sg_b_st"][j].T for j in range(2)]
    rep = [loss[0, :1], jnp.concatenate(g["norm_g"], axis=0), g["final_norm_g"], jnp.stack(g["sg_w_s"]), jnp.stack(d_b_sp),
           g["swa_sinks"], g["rw_hp"][2]]
    rep_shapes = [(1,), norm_g.shape, final_norm_g.shape, sg_w_spatial.shape, sg_b_spatial.shape, swa_sinks.shape, rwkv_r_k.shape]
    rep_sum = _sum_parts("sum_rep", _gather("gather_rep", _pack(rep, 128 * 256))).reshape(-1)
    loss_tot, g_norm_g, g_final, g_w_sp, g_b_sp, g_sinks, g_r_k = _unpack(rep_sum, rep_shapes)

    p_sg_in = jnp.concatenate([sj["sg_in0"]["out"], sj["sg_in1"]["out"]], axis=1)
    p_sg_out = jnp.concatenate([sj["sg_out0"]["out"], sj["sg_out1"]["out"]], axis=1)
    p_swa_in, p_swa_out, p_rw_in, p_rw_out = (sj[nm]["out"] for nm in ("swa_in", "swa_out", "rw_in", "rw_out"))
    d_mu = jnp.concatenate([g["rw_mu_main"], g["rw_mu_lora"][:, :2 * LORA]], axis=1)
    hp_flat = lambda a: a.reshape(1, -1)
    small_grads = dict(sg_ln_g=jnp.concatenate(g["sg_ln_g"], axis=0), sg_ln_b=jnp.concatenate(g["sg_ln_b"], axis=0), rwkv_mu=d_mu,
                       rwkv_w0=g["rw_w0"], rwkv_a0=g["rw_a0"], rwkv_k_k=hp_flat(g["rw_hp"][0]), rwkv_k_a=hp_flat(g["rw_hp"][1]),
                       rwkv_gn_g=hp_flat(g["rw_hp"][3]), rwkv_gn_b=hp_flat(g["rw_hp"][4]),
                       rwkv_w_lora=g["rw_w_lora"][None, :LORA], rwkv_a_lora=g["rw_a_lora"][None, LORA:2 * LORA])
    per_dest = []
    for dv in range(N_DEV):
        shards = []
        for n in small_names:
            full, w = small_grads[n], weights[n].shape[-1]
            shards.append(full[..., dv * w:(dv + 1) * w])
        per_dest.append(_pack(shards))
    small_parts = _exchange("scatter_small", jnp.stack(per_dest), True)

    out_g, out_d, out_m, out_v = {}, {}, {}, {}

    def update(name, grad, shape2d, jobs=()):
        w2, m2, v2 = (a[name].reshape(shape2d) for a in (weights, mom_m, mom_v))
        gg, dd, mm, vv = _adamw("adamw_" + name, w2, grad, m2, v2, jobs=jobs)
        shp = weights[name].shape
        out_g[name], out_d[name], out_m[name], out_v[name] = gg.reshape(shp), dd.reshape(shp), mm.reshape(shp), vv.reshape(shp)

    update("mod_w", g_mod_w.reshape(-1, n_mod), (-1, n_mod))
    update("sg_w_in", p_sg_in, (-1, sg_w_in.shape[2]))
    update("sg_w_out", p_sg_out, (-1, d))
    update("swa_w_in", p_swa_in, (-1, swa_w_in.shape[2]))
    update("swa_w_out", p_swa_out, (-1, d))
    update("rwkv_w_in", p_rw_in, (-1, n_rw))
    update("rwkv_w_out", p_rw_out, (-1, d))
    update("sg_w_spatial", g_w_sp.reshape(-1, 128), (-1, 128))
    w_pk, m_pk, v_pk = (_pack([a[n] for n in small_names]) for a in (weights, mom_m, mom_v))
    res = _adamw("adamw_small", w_pk, small_parts, m_pk, v_pk)
    for q, arrs in enumerate(zip(*[_unpack(r_.reshape(-1), small_shapes) for r_ in res])):
        out_g[small_names[q]], out_d[small_names[q]], out_m[small_names[q]], out_v[small_names[q]] = arrs
    rep_names = ["norm_g", "mod_b", "final_norm_g", "sg_b_spatial", "swa_sinks", "rwkv_r_k"]
    rep_grads = [g_norm_g, g_mod_b.reshape(mod_b.shape), g_final, g_b_sp, g_sinks, g_r_k]
    rep_shapes2 = [weights[n].shape for n in rep_names]
    w_pk, m_pk, v_pk = (_pack([a[n] for n in rep_names]) for a in (weights, mom_m, mom_v))
    res = _adamw("adamw_rep", w_pk, _pack(rep_grads), m_pk, v_pk)
    for q, arrs in enumerate(zip(*[_unpack(r_.reshape(-1), rep_shapes2) for r_ in res])):
        out_g[rep_names[q]], out_d[rep_names[q]], out_m[rep_names[q]], out_v[rep_names[q]] = arrs

    return (loss_tot.reshape(()), dx[None], *[out_g[n] for n in names], *[out_d[n] for n in names],
            *[out_m[n] for n in names], *[out_v[n] for n in names])
```

```python
import functools
import math

import jax
import jax.numpy as jnp
from jax import lax
from jax.experimental import pallas as pl
from jax.experimental.pallas import tpu as pltpu

F32, BF16 = jnp.float32, jnp.bfloat16
HI = lax.Precision.HIGHEST
S = jax.ShapeDtypeStruct
MESH = pl.DeviceIdType.MESH

N_DEV = 8
DEPTH = 4
HEAD = 64
SG_GROUPS = 16
SG_CHUNK = 128
SWA_BLOCK = 128
SWA_KV = 4
SWA_REP = 8
ROPE_THETA = 10000.0
LORA = 96
LORA_PAD = 256
RW_CHUNK = 64
RW_HEADS = 16
RW_HEADS_FWD = 32
RW_PREC = lax.Precision.HIGH
DECAY_SCALE = math.exp(-0.5)
GN_EPS = 64e-5
RMS_EPS = 1e-6
LN_EPS = 1e-5
NEG = -1e30
ADAM_LR, ADAM_B1, ADAM_B2, ADAM_EPS, ADAM_WD, ADAM_STEP = 0.001, 0.9, 0.999, 1e-08, 0.01, 10
VMEM_MB = 56


def _params(sem=None):
    kw = dict(vmem_limit_bytes=VMEM_MB << 20)
    if sem is not None:
        kw["dimension_semantics"] = sem
    return pltpu.CompilerParams(**kw)


def _pick(n, opts):
    for o in opts:
        if n % o == 0:
            return o
    raise ValueError(f"no tile for {n}")


def _rows(name, fn, rows, consts, out_rows, out_accs, tm, jobs=()):
    t = rows[0].shape[0]
    nr, nc, no = len(rows), len(consts), len(out_rows)

    def body(*refs):
        outs = fn(*[r[...] for r in refs[:nr + nc]])
        if not isinstance(outs, (tuple, list)):
            outs = (outs,)
        for r, o in zip(refs[nr + nc:nr + nc + no], outs[:no]):
            r[...] = o.astype(r.dtype)
        i = pl.program_id(0)
        for r, o in zip(refs[nr + nc + no:], outs[no:]):
            @pl.when(i == 0)
            def _(r=r, o=o):
                r[...] = o.astype(r.dtype)

            @pl.when(i > 0)
            def _(r=r, o=o):
                r[...] += o.astype(r.dtype)

    in_specs = [pl.BlockSpec((tm, a.shape[1]), lambda i: (i, 0)) for a in rows]
    in_specs += [pl.BlockSpec(c.shape, lambda i, nd=c.ndim: (0,) * nd) for c in consts]
    out_specs = [pl.BlockSpec((tm, n), lambda i: (i, 0)) for n, _ in out_rows]
    out_specs += [pl.BlockSpec(s, lambda i, nd=len(s): (0,) * nd) for s in out_accs]
    out_shape = [S((t, n), dt) for n, dt in out_rows] + [S(s, F32) for s in out_accs]
    return _pcall(body, grid=(t // tm,), in_specs=in_specs, out_specs=out_specs, out_shape=out_shape, name=name,
                  semantics=("arbitrary",), inputs=(*rows, *consts), jobs=jobs)


_DN = {"nn": (((1,), (0,)), ((), ())), "nt": (((1,), (1,)), ((), ())), "tn": (((0,), (0,)), ((), ()))}


def _mm(name, a, b, mode, out_dtype, add=None, resid=None, jobs=()):
    if mode == "nn":
        (m, k), (_, n) = a.shape, b.shape
    elif mode == "nt":
        (m, k), (n, _) = a.shape, b.shape
    else:
        (k, m), (_, n) = a.shape, b.shape
    wide = mode != "tn" and add is None and resid is None
    tm = _pick(m, (1024, 512, 256, 128))
    tn = _pick(n, ((1536,) if wide and n % 1024 else ()) + (1024, 512, 384, 256, 128))
    long_k = (4096,) if mode == "tn" else (3072, 2304) if mode == "nt" and add is None else ()
    tk = _pick(k, long_k + (2048, 1536, 1024, 512, 384, 256, 128))
    nk = k // tk
    n_extra = (add is not None) + 2 * (resid is not None)

    def body(*refs):
        a_ref, b_ref = refs[0], refs[1]
        extra, outs, acc = refs[2:2 + n_extra], refs[2 + n_extra:-1], refs[-1]
        kk = pl.program_id(2)
        prod = lax.dot_general(a_ref[...].astype(BF16), b_ref[...].astype(BF16), _DN[mode], preferred_element_type=F32)
        if add is not None:
            prod = jnp.where(kk == 0, prod + extra[0][...].astype(F32), prod) if nk > 1 else prod + extra[0][...].astype(F32)

        def finish(total):
            outs[0][...] = total.astype(outs[0].dtype)
            if resid is not None:
                outs[1][...] = extra[-2][...] + extra[-1][...] * total

        if nk == 1:
            finish(prod)
            return

        @pl.when(kk == 0)
        def _():
            acc[...] = prod

        @pl.when(kk > 0)
        def _():
            acc[...] += prod

        @pl.when(kk == nk - 1)
        def _():
            finish(acc[...])

    a_spec = pl.BlockSpec((tk, tm), lambda i, j, q: (q, i)) if mode == "tn" else pl.BlockSpec((tm, tk), lambda i, j, q: (i, q))
    b_spec = pl.BlockSpec((tn, tk), lambda i, j, q: (j, q)) if mode == "nt" else pl.BlockSpec((tk, tn), lambda i, j, q: (q, j))
    o_spec = pl.BlockSpec((tm, tn), lambda i, j, q: (i, j))
    ins, specs, out_specs, out_shape = [a, b], [a_spec, b_spec], [o_spec], [S((m, n), out_dtype)]
    if add is not None:
        ins.append(add)
        specs.append(o_spec)
    if resid is not None:
        ins += list(resid)
        specs += [o_spec, pl.BlockSpec((1, tn), lambda i, j, q: (0, j))]
        out_specs.append(o_spec)
        out_shape.append(S((m, n), F32))
    res = _pcall(body, grid=(m // tm, n // tn, nk), in_specs=specs, out_specs=out_specs, out_shape=out_shape,
                 scratch_shapes=[pltpu.VMEM((tm, tn), F32)], name=name, semantics=("parallel", "parallel", "arbitrary"),
                 inputs=ins, jobs=jobs)
    return res if resid is not None else res[0]


def _exchange(name, src, scatter):
    blk = src.shape[1:] if scatter else src.shape

    def body(src_ref, dst_ref, send_sems, recv_sems, loc_sem):
        x, y, c = lax.axis_index("x"), lax.axis_index("y"), lax.axis_index("c")
        me = 4 * x + 2 * y + c

        def mine(d):
            return src_ref.at[d] if scatter else src_ref

        local = pltpu.make_async_copy(mine(me), dst_ref.at[me], loc_sem)
        local.start()
        sends, peers = [], []
        for k in range(1, N_DEV):
            px = 1 - x if k & 4 else x
            py = 1 - y if k & 2 else y
            pc = 1 - c if k & 1 else c
            pid = 4 * px + 2 * py + pc
            cp = pltpu.make_async_remote_copy(src_ref=mine(pid), dst_ref=dst_ref.at[me], send_sem=send_sems.at[k - 1],
                                              recv_sem=recv_sems.at[k - 1], device_id=(px, py, pc), device_id_type=MESH)
            cp.start()
            sends.append(cp)
            peers.append((pid, (px, py, pc)))
        for k in range(1, N_DEV):
            pid, dev = peers[k - 1]
            pltpu.make_async_remote_copy(src_ref=mine(pid), dst_ref=dst_ref.at[pid], send_sem=send_sems.at[k - 1],
                                         recv_sem=recv_sems.at[k - 1], device_id=dev, device_id_type=MESH).wait_recv()
        for cp in sends:
            cp.wait_send()
        local.wait()

    return pl.pallas_call(
        body, out_shape=S((N_DEV,) + tuple(blk), src.dtype),
        in_specs=[pl.BlockSpec(memory_space=pl.ANY)], out_specs=pl.BlockSpec(memory_space=pl.ANY),
        scratch_shapes=[pltpu.SemaphoreType.DMA((N_DEV - 1,)), pltpu.SemaphoreType.DMA((N_DEV - 1,)),
                        pltpu.SemaphoreType.DMA],
        name=name)(src)


class _Gather:
    @staticmethod
    def out_shape(src):
        return S((N_DEV,) + tuple(src.shape), src.dtype)

    scratch = (pltpu.SemaphoreType.DMA((N_DEV - 1,)), pltpu.SemaphoreType.DMA((N_DEV - 1,)), pltpu.SemaphoreType.DMA)

    def __init__(self, src_ref, dst_ref, send_sems, recv_sems, loc_sem):
        self.refs = (src_ref, dst_ref, send_sems, recv_sems, loc_sem)
        x, y, c = lax.axis_index("x"), lax.axis_index("y"), lax.axis_index("c")
        self.c, self.me, self.sibling = c, (x, y, c), (x, y, 1 - c)
        self.chips = [(1 - x, y), (x, 1 - y), (1 - x, 1 - y)]

    def rows(self, px, py, pc):
        return self.refs[1].at[4 * px + 2 * py + pc]

    def copy(self, k, block, to, own=False):
        src_ref, _, send_sems, recv_sems, _ = self.refs
        return pltpu.make_async_remote_copy(src_ref=src_ref if own else self.rows(*block), dst_ref=self.rows(*block),
                                            send_sem=send_sems.at[k], recv_sem=recv_sems.at[k], device_id=to,
                                            device_id_type=MESH)

    def local(self):
        return pltpu.make_async_copy(self.refs[0], self.rows(*self.me), self.refs[4])

    def first(self):
        return [self.copy(0, self.me, self.sibling, own=True)] + [self.copy(1 + j, self.me, (*chip, self.c), own=True)
                                                                  for j, chip in enumerate(self.chips)]

    def start(self):
        self.local().start()
        for cp in self.first():
            cp.start()

    def finish(self):
        c = self.c
        passed = [self.copy(4 + j, (*chip, c), self.sibling) for j, chip in enumerate(self.chips)]
        for j, chip in enumerate(self.chips):
            self.copy(1 + j, (*chip, c), self.me).wait_recv()
            passed[j].start()
        self.copy(0, self.sibling, self.me).wait_recv()
        for j, chip in enumerate(self.chips):
            self.copy(4 + j, (*chip, 1 - c), self.me).wait_recv()
        for cp in self.first() + passed:
            cp.wait_send()
        self.local().wait()


class _Chips:
    @staticmethod
    def out_shape(src):
        return S(src.shape, src.dtype)

    scratch = (pltpu.SemaphoreType.DMA((N_DEV // 2 - 1,)), pltpu.SemaphoreType.DMA((N_DEV // 2 - 1,)), pltpu.SemaphoreType.DMA)

    def __init__(self, src_ref, dst_ref, send_sems, recv_sems, loc_sem):
        self.refs = (src_ref, dst_ref, send_sems, recv_sems, loc_sem)
        x, y, c = lax.axis_index("x"), lax.axis_index("y"), lax.axis_index("c")
        self.c, self.mine = c, 2 * x + y
        self.chips = [(1 - x, y), (x, 1 - y), (1 - x, 1 - y)]

    def local(self):
        src_ref, dst_ref, _, _, loc_sem = self.refs
        return pltpu.make_async_copy(src_ref.at[self.mine], dst_ref.at[self.mine], loc_sem)

    def send(self, j):
        src_ref, dst_ref, send_sems, recv_sems, _ = self.refs
        px, py = self.chips[j]
        return pltpu.make_async_remote_copy(src_ref=src_ref.at[2 * px + py], dst_ref=dst_ref.at[self.mine],
                                            send_sem=send_sems.at[j], recv_sem=recv_sems.at[j],
                                            device_id=(px, py, self.c), device_id_type=MESH)

    def arrival(self, j):
        src_ref, dst_ref, send_sems, recv_sems, _ = self.refs
        px, py = self.chips[j]
        return pltpu.make_async_remote_copy(src_ref=src_ref.at[self.mine], dst_ref=dst_ref.at[2 * px + py],
                                            send_sem=send_sems.at[j], recv_sem=recv_sems.at[j],
                                            device_id=(px, py, self.c), device_id_type=MESH)

    def start(self):
        self.local().start()
        for j in range(len(self.chips)):
            self.send(j).start()

    def finish(self):
        for j in range(len(self.chips)):
            self.arrival(j).wait_recv()
        for j in range(len(self.chips)):
            self.send(j).wait_send()
        self.local().wait()


def _exchange_call(name, cls, src):
    def body(*refs):
        ex = cls(*refs)
        ex.start()
        ex.finish()

    return pl.pallas_call(body, out_shape=cls.out_shape(src), in_specs=[pl.BlockSpec(memory_space=pl.ANY)],
                          out_specs=pl.BlockSpec(memory_space=pl.ANY), scratch_shapes=list(cls.scratch), name=name)(src)


def _gather(name, src):
    return _exchange_call(name, _Gather, src)


def _pcall(body, *, grid, in_specs, out_specs, out_shape, scratch_shapes=(), name, semantics, inputs, jobs=()):
    if not jobs:
        return pl.pallas_call(body, grid=grid, in_specs=in_specs, out_specs=out_specs, out_shape=out_shape,
                              scratch_shapes=list(scratch_shapes), name=name, compiler_params=_params(semantics))(*inputs)
    n_in, n_out, n_scr, nj = len(in_specs), len(out_specs), len(scratch_shapes), len(jobs)

    def hosted(*refs):
        ins, srcs = refs[:n_in], refs[n_in:n_in + nj]
        outs, dsts = refs[n_in + nj:n_in + nj + n_out], refs[n_in + nj + n_out:n_in + 2 * nj + n_out]
        scr, sems = refs[n_in + 2 * nj + n_out:n_in + 2 * nj + n_out + n_scr], refs[n_in + 2 * nj + n_out + n_scr:]
        ids = [pl.program_id(q) for q in range(len(grid))]
        first = functools.reduce(jnp.logical_and, [i == 0 for i in ids])
        last = functools.reduce(jnp.logical_and, [i == g - 1 for i, g in zip(ids, grid)])
        make = lambda q: jobs[q]["cls"](srcs[q], dsts[q], *sems[3 * q:3 * q + 3])

        @pl.when(first)
        def _():
            for q in range(nj):
                make(q).start()

        body(*ins, *outs, *scr)

        @pl.when(last)
        def _():
            for q in range(nj):
                make(q).finish()

    anyspec = pl.BlockSpec(memory_space=pl.ANY)
    res = pl.pallas_call(
        hosted, grid=grid, in_specs=list(in_specs) + [anyspec] * nj, out_specs=list(out_specs) + [anyspec] * nj,
        out_shape=list(out_shape) + [j["cls"].out_shape(j["src"]) for j in jobs],
        scratch_shapes=list(scratch_shapes) + [s for j in jobs for s in j["cls"].scratch], name=name,
        compiler_params=_params(("arbitrary",) * len(grid)))(*inputs, *[j["src"] for j in jobs])
    for j, out in zip(jobs, res[n_out:]):
        j["out"] = out
    return res[:n_out]


def _scatter_pairs(name, parts):
    _, r, c_ = parts.shape
    n_chip = N_DEV // 2

    def stage1(src_ref, dst_ref, send_sems, recv_sems):
        x, y, c = lax.axis_index("x"), lax.axis_index("y"), lax.axis_index("c")
        sends = []
        for q in range(n_chip):
            cp = pltpu.make_async_remote_copy(src_ref=src_ref.at[2 * q + 1 - c], dst_ref=dst_ref.at[q],
                                              send_sem=send_sems.at[q], recv_sem=recv_sems.at[q],
                                              device_id=(x, y, 1 - c), device_id_type=MESH)
            cp.start()
            sends.append(cp)
        for q in range(n_chip):
            pltpu.make_async_remote_copy(src_ref=src_ref.at[2 * q + c], dst_ref=dst_ref.at[q], send_sem=send_sems.at[q],
                                         recv_sem=recv_sems.at[q], device_id=(x, y, 1 - c), device_id_type=MESH).wait_recv()
        for cp in sends:
            cp.wait_send()

    from_sibling = pl.pallas_call(
        stage1, out_shape=S((n_chip, r, c_), parts.dtype),
        in_specs=[pl.BlockSpec(memory_space=pl.ANY)], out_specs=pl.BlockSpec(memory_space=pl.ANY),
        scratch_shapes=[pltpu.SemaphoreType.DMA((n_chip,)), pltpu.SemaphoreType.DMA((n_chip,))], name=name + "_pair")(parts)

    tm = _pick(r, (512, 256, 128, 64, 32, 16, 8)) if r % 8 == 0 else r
    core = lax.axis_index("c").astype(jnp.int32).reshape(1)

    def pair_sum(core_ref, mine_ref, sib_ref, o_ref):
        o_ref[...] = (mine_ref[0].astype(F32) + sib_ref[...].astype(F32)).astype(o_ref.dtype)

    pair = pl.pallas_call(
        pair_sum, out_shape=S((n_chip, r, c_), parts.dtype),
        grid_spec=pltpu.PrefetchScalarGridSpec(
            num_scalar_prefetch=1, grid=(n_chip, r // tm),
            in_specs=[pl.BlockSpec((1, 1, tm, c_), lambda q, i, core_ref: (q, core_ref[0], i, 0)),
                      pl.BlockSpec((1, tm, c_), lambda q, i, core_ref: (q, i, 0))],
            out_specs=pl.BlockSpec((1, tm, c_), lambda q, i, core_ref: (q, i, 0))),
        name=name + "_sum", compiler_params=_params(("parallel", "parallel")))(
            core, parts.reshape(n_chip, 2, r, c_), from_sibling)

    return pair


def _sum_parts(name, parts):
    n_parts, r, c = parts.shape
    tm = _pick(r, (512, 256, 128, 64, 32, 16, 8)) if r % 8 == 0 else r

    def body(p_ref, o_ref):
        acc = p_ref[0].astype(F32)
        for d in range(1, n_parts):
            acc = acc + p_ref[d].astype(F32)
        o_ref[...] = acc

    return pl.pallas_call(body, grid=(r // tm,), in_specs=[pl.BlockSpec((n_parts, tm, c), lambda i: (0, i, 0))],
                          out_specs=pl.BlockSpec((tm, c), lambda i: (i, 0)), out_shape=S((r, c), F32), name=name,
                          compiler_params=_params(("parallel",)))(parts)


def _adamw(name, w, g, m, v, jobs=()):
    r, c = w.shape
    parts = g.ndim == 3
    n_parts = g.shape[0] if parts else 1
    tile_rows = max(8, (2 << 20) // (4 * c))
    tm = _pick(r, tuple(q for q in (2048, 1024, 512, 256, 128, 64, 32, 16, 8) if q <= tile_rows)) if r % 8 == 0 else r

    def body(w_ref, g_ref, m_ref, v_ref, go_ref, d_ref, mo_ref, vo_ref):
        if parts:
            gg = g_ref[0].astype(F32)
            for d in range(1, n_parts):
                gg = gg + g_ref[d].astype(F32)
        else:
            gg = g_ref[...]
        mm = ADAM_B1 * m_ref[...] + (1.0 - ADAM_B1) * gg
        vv = ADAM_B2 * v_ref[...] + (1.0 - ADAM_B2) * jnp.square(gg)
        m_hat = mm / (1.0 - ADAM_B1 ** ADAM_STEP)
        v_hat = vv / (1.0 - ADAM_B2 ** ADAM_STEP)
        go_ref[...] = gg
        d_ref[...] = -ADAM_LR * (m_hat / (jnp.sqrt(v_hat) + ADAM_EPS) + ADAM_WD * w_ref[...])
        mo_ref[...] = mm
        vo_ref[...] = vv

    spec = pl.BlockSpec((tm, c), lambda i: (i, 0))
    g_spec = pl.BlockSpec((n_parts, tm, c), lambda i: (0, i, 0)) if parts else spec
    return _pcall(body, grid=(r // tm,), in_specs=[spec, g_spec, spec, spec], out_specs=[spec] * 4,
                  out_shape=[S((r, c), F32)] * 4, name=name, semantics=("parallel",), inputs=(w, g, m, v), jobs=jobs)


def _rms(x, g):
    return x * lax.rsqrt(jnp.mean(x * x, axis=-1, keepdims=True) + RMS_EPS) * g


def _adaln(x, g, shift, scale):
    return _rms(x, g) * (1.0 + scale) + shift


def _dot(a, b, dn="nn", hi=False, prec=None):
    if hi or prec is not None:
        return lax.dot_general(a, b, _DN[dn], precision=HI if hi else prec, preferred_element_type=F32)
    return lax.dot_general(a.astype(BF16), b.astype(BF16), _DN[dn], preferred_element_type=F32)


def _sg_mix(p, ln_g, ln_b, w_s, b_st):
    d = p.shape[1] // 3
    gd = d // SG_GROUPS
    u = jax.nn.gelu(p[:, :d])
    vf = jax.nn.gelu(p[:, d:2 * d])
    z = p[:, 2 * d:]
    mean = jnp.mean(vf, axis=-1, keepdims=True)
    var = jnp.mean(jnp.square(vf - mean), axis=-1, keepdims=True)
    vn = (vf - mean) * lax.rsqrt(var + LN_EPS) * ln_g + ln_b
    row = lax.broadcasted_iota(jnp.int32, (SG_CHUNK, SG_CHUNK), 0)
    col = lax.broadcasted_iota(jnp.int32, (SG_CHUNK, SG_CHUNK), 1)
    fs = []
    for g in range(SG_GROUPS):
        w = jnp.where(row >= col, w_s[g], 0.0)
        fs.append(_dot(w, vn[:, g * gd:(g + 1) * gd]))
    sel = (lax.broadcasted_iota(jnp.int32, (SG_GROUPS, d), 1) // gd
           == lax.broadcasted_iota(jnp.int32, (SG_GROUPS, d), 0)).astype(F32)
    f = jnp.concatenate(fs, axis=1) + _dot(b_st, sel, hi=True)
    return u * f * jax.nn.silu(z)


def _rot_half(x):
    n = x.shape[1]
    lane = lax.broadcasted_iota(jnp.int32, x.shape, 1)
    return jnp.where(lane % HEAD < HEAD // 2, -pltpu.roll(x, n - HEAD // 2, 1), pltpu.roll(x, HEAD // 2, 1))


def _rope(x, cos, sin, sign):
    reps = x.shape[1] // cos.shape[1]
    return x * jnp.tile(cos, (1, reps)) + sign * _rot_half(x) * jnp.tile(sin, (1, reps))


def _attn_block(q, kp, kc, vp, vc, sink, prev_bias):
    each = lambda f, *ls: [f(*xs) for xs in zip(*ls)]
    r = sink[0].shape[0]
    cur, prob, _ = _attn_probs(q, kp, kc, sink, prev_bias)
    flat = lambda x: x.reshape(r * SWA_BLOCK, SWA_BLOCK)
    pc, pp = each(lambda p: flat(jnp.where(cur, p, 0.0)), prob), each(lambda p: flat(jnp.where(cur, 0.0, p)), prob)
    return each(lambda a, va, b, vb: _dot(a, va) + _dot(b, vb), pp, vp, pc, vc)


def _attn_probs(q, kp, kc, sink, prev_bias):
    each = lambda f, *ls: [f(*xs) for xs in zip(*ls)]
    r = sink[0].shape[0]
    scores = lambda a, b: (_dot(a, b, "nt") * (HEAD ** -0.5)).reshape(r, SWA_BLOCK, SWA_BLOCK)
    sp, sc = each(scores, q, kp), each(scores, q, kc)
    cur = (lax.broadcasted_iota(jnp.int32, (r, SWA_BLOCK, SWA_BLOCK), 2)
           <= lax.broadcasted_iota(jnp.int32, (r, SWA_BLOCK, SWA_BLOCK), 1))
    s = each(lambda a, b: jnp.where(cur, b, a + prev_bias), sp, sc)
    m = each(lambda a, sk: jnp.maximum(jnp.max(a, axis=-1, keepdims=True), sk), s, sink)
    e, es = each(lambda a, m_: jnp.exp(a - m_), s, m), each(lambda sk, m_: jnp.exp(sk - m_), sink, m)
    denom = each(lambda a, b: jnp.sum(a, axis=-1, keepdims=True) + b, e, es)
    return cur, each(lambda a, dn: a / dn, e, denom), each(lambda a, dn: a / dn, es, denom)


def _attn_block_bwd(q, kp, kc, vp, vc, sink, do, prev_bias):
    each = lambda f, *ls: [f(*xs) for xs in zip(*ls)]
    r = sink[0].shape[0]
    scale = HEAD ** -0.5
    cube = lambda x: x.reshape(r, SWA_BLOCK, SWA_BLOCK)
    flat = lambda x: x.reshape(r * SWA_BLOCK, SWA_BLOCK)
    cur, prob, p_sink = _attn_probs(q, kp, kc, sink, prev_bias)
    pc, pp = each(lambda p: flat(jnp.where(cur, p, 0.0)), prob), each(lambda p: flat(jnp.where(cur, 0.0, p)), prob)
    dprob = each(lambda g, va, vb: jnp.where(cur, cube(_dot(g, vb, "nt")), cube(_dot(g, va, "nt"))), do, vp, vc)
    dvp, dvc = each(lambda p, g: _dot(p, g, "tn"), pp, do), each(lambda p, g: _dot(p, g, "tn"), pc, do)
    delta = each(lambda p, dp: jnp.sum(p * dp, axis=-1, keepdims=True), prob, dprob)
    ds = each(lambda p, dp, dl: p * (dp - dl), prob, dprob, delta)
    dsc, dsp = each(lambda x: flat(jnp.where(cur, x, 0.0)), ds), each(lambda x: flat(jnp.where(cur, 0.0, x)), ds)
    dsink = each(lambda ps, dl: -jnp.sum(ps * dl, axis=1, keepdims=True), p_sink, delta)
    dq = each(lambda a, ka, b, kb: (_dot(a, ka) + _dot(b, kb)) * scale, dsp, kp, dsc, kc)
    dkp, dkc = each(lambda a, q_: _dot(a, q_, "tn") * scale, dsp, q), each(lambda a, q_: _dot(a, q_, "tn") * scale, dsc, q)
    return dq, dkp, dkc, dvp, dvc, dsink


def _rwkv_chunk(s0, r, k, v, logw, a, k_k, k_a, r_k, gn_g, gn_b):
    c = r[0].shape[0]
    each = lambda f, *ls: [f(*xs) for xs in zip(*ls)]
    gram = functools.partial(_dot, prec=RW_PREC)
    row = lax.broadcasted_iota(jnp.int32, (c, c), 0)
    col = lax.broadcasted_iota(jnp.int32, (c, c), 1)
    incl, strict = row >= col, row > col
    ones_l = incl.astype(F32)

    def unit(x):
        return x / jnp.maximum(jnp.sqrt(jnp.sum(x * x, axis=-1, keepdims=True)), 1e-12)

    kk = each(lambda k_, p: unit(k_ * p), k, k_k)
    km = each(lambda k_, a_, p: k_ * (1.0 + (a_ - 1.0) * p), k, a, k_a)
    b = each(lambda x, a_: x * a_, kk, a)
    first_half = lax.broadcasted_iota(jnp.int32, (c, HEAD), 0) < c // 2
    mid = each(lambda w: jnp.sum(jnp.where(first_half, w, 0.0), axis=0, keepdims=True), logw)
    cum = each(lambda w, m: _dot(ones_l, w, hi=True) - m, logw, mid)
    alpha = each(lambda x, cu, w: x * jnp.exp(cu - w), kk, cum, logw)
    beta = each(lambda x, cu: x * jnp.exp(-cu), b, cum)
    kap = each(lambda x, cu: x * jnp.exp(-cu), km, cum)
    rho = each(lambda x, cu: x * jnp.exp(cu), r, cum)
    s0 = each(lambda s, m: s * jnp.exp(m), s0, mid)
    lab = each(lambda x, y_: jnp.where(strict, gram(x, y_, "nt"), 0.0), alpha, beta)
    lak = each(lambda x, y_: jnp.where(strict, gram(x, y_, "nt"), 0.0), alpha, kap)
    xs = each(lambda al, s, l, v_: _dot(al, s, "nt") + _dot(l, v_), alpha, s0, lak, v)
    xs = each(lambda x, l: x - _dot(l, x), xs, lab)
    lp, power = lab, 2
    while power < c:
        lp = each(lambda l: _dot(l, l), lp)
        xs = each(lambda x, l: x + _dot(l, x), xs, lp)
        power *= 2
    u = each(lambda x: -x, xs)
    mrb = each(lambda x, y_: jnp.where(incl, gram(x, y_, "nt"), 0.0), rho, beta)
    mrk = each(lambda x, y_: jnp.where(incl, gram(x, y_, "nt"), 0.0), rho, kap)
    y = each(lambda rh, s, mb, u_, mk, v_: _dot(rh, s, "nt") + _dot(mb, u_) + _dot(mk, v_), rho, s0, mrb, u, mrk, v)
    s1 = each(lambda s, u_, be, v_, ka, w, m: (s + _dot(u_, be, "tn") + _dot(v_, ka, "tn"))
              * jnp.exp(jnp.sum(w, axis=0, keepdims=True) - m), s0, u, beta, v, kap, logw, mid)

    def finish(y_, g, bias, r_, km_, rk, v_):
        mean = jnp.mean(y_, axis=-1, keepdims=True)
        var = jnp.mean(jnp.square(y_ - mean), axis=-1, keepdims=True)
        y_ = (y_ - mean) * lax.rsqrt(var + GN_EPS) * g + bias
        return y_ + jnp.sum(r_ * km_ * rk, axis=-1, keepdims=True) * v_

    return each(finish, y, gn_g, gn_b, r, km, r_k, v), s1


def _norm_fwd(name, x, g, shift, scale):
    return _rows(name, lambda x_, g_, sh, sc: _adaln(x_, g_, sh, sc), [x], [g, shift, scale], [(x.shape[1], BF16)], [], 256)[0]


def _norm_bwd(name, x, dh, dx_res, g, shift, scale):
    d = x.shape[1]

    def fn(x_, dh_, dr_, g_, sh, sc):
        _, vjp = jax.vjp(_adaln, x_, g_, sh, sc)
        dx, dg, dsh, dsc = vjp(dh_)
        return dx + dr_, dg, dsh, dsc

    return _rows(name, fn, [x, dh, dx_res], [g, shift, scale], [(d, F32)], [(1, d)] * 3, 256)


def _resid_bwd(name, dx, y, gate):
    d = dx.shape[1]
    return _rows(name, lambda dx_, y_, g_: (g_ * dx_, jnp.sum(dx_ * y_, axis=0, keepdims=True)), [dx, y], [gate],
                 [(d, BF16)], [(1, d)], 256)


def _sg_fwd(name, p, ln_g, ln_b, w_s, b_st):
    d = p.shape[1] // 3
    return _rows(name, _sg_mix, [p], [ln_g, ln_b, w_s, b_st], [(d, BF16)], [], SG_CHUNK)[0]


def _sg_bwd(name, p, dmix, ln_g, ln_b, w_s, b_st, jobs=()):
    def fn(p_, dm_, lg, lb, ws, bs):
        _, vjp = jax.vjp(_sg_mix, p_, lg, lb, ws, bs)
        return vjp(dm_)

    return _rows(name, fn, [p, dmix], [ln_g, ln_b, w_s, b_st], [(p.shape[1], BF16)],
                 [ln_g.shape, ln_b.shape, w_s.shape, b_st.shape], SG_CHUNK, jobs=jobs)


def _rope_tables(pos, inv_freq):
    ang = pos * inv_freq
    return jnp.cos(ang), jnp.sin(ang)


def _swa_pre(name, p, pos, inv_freq, d):
    kvw = SWA_KV * HEAD

    def fn(p_, pos_, fr):
        cos, sin = _rope_tables(pos_, fr)
        return (_rope(p_[:, :d], cos, sin, 1.0), _rope(p_[:, d:d + kvw], cos, sin, 1.0), p_[:, d + kvw:d + 2 * kvw])

    return _rows(name, fn, [p, pos], [inv_freq], [(d, BF16), (kvw, BF16), (kvw, BF16)], [], 256)


def _q_groups(ref, kv, rep):
    heads = _head_cols(ref, kv * rep)
    return [jnp.concatenate(heads[g * rep:(g + 1) * rep], axis=0) for g in range(kv)]


def _q_ungroup(groups, rep):
    return jnp.concatenate([g[h * SWA_BLOCK:(h + 1) * SWA_BLOCK] for g in groups for h in range(rep)], axis=1)


def _swa_attn_fwd(name, q, k, v, sinks, jobs=()):
    t, d = q.shape
    kv, rep = sinks.shape[0], sinks.shape[1]
    nb = t // SWA_BLOCK

    def body(q_ref, kp_ref, kc_ref, vp_ref, vc_ref, s_ref, o_ref):
        prev_bias = jnp.where(pl.program_id(0) > 0, 0.0, NEG).astype(F32)
        o = _attn_block(_q_groups(q_ref, kv, rep), *[_head_cols(ref, kv) for ref in (kp_ref, kc_ref, vp_ref, vc_ref)],
                        [s_ref[g] for g in range(kv)], prev_bias)
        o_ref[...] = _q_ungroup(o, rep)

    qs = pl.BlockSpec((SWA_BLOCK, d), lambda n: (n, 0))
    cur = pl.BlockSpec((SWA_BLOCK, kv * HEAD), lambda n: (n, 0))
    prev = pl.BlockSpec((SWA_BLOCK, kv * HEAD), lambda n: (jnp.maximum(n - 1, 0), 0))
    ss = pl.BlockSpec(sinks.shape, lambda n: (0, 0, 0, 0))
    return _pcall(body, grid=(nb,), in_specs=[qs, prev, cur, prev, cur, ss], out_specs=[qs], out_shape=[S((t, d), F32)],
                  name=name, semantics=("parallel",), inputs=(q, k, k, v, v, sinks), jobs=jobs)[0]


def _swa_attn_bwd(name, q, k, v, sinks, do, jobs=()):
    t, d = q.shape
    kv, rep = sinks.shape[0], sinks.shape[1]
    nb = t // SWA_BLOCK

    def body(q_ref, kp_ref, kc_ref, vp_ref, vc_ref, s_ref, do_ref, dq_ref, dkc_ref, dkp_ref, dvc_ref, dvp_ref, ds_ref):
        n = pl.program_id(0)
        prev_bias = jnp.where(n > 0, 0.0, NEG).astype(F32)
        args = [_q_groups(q_ref, kv, rep)] + [_head_cols(ref, kv) for ref in (kp_ref, kc_ref, vp_ref, vc_ref)]
        dq, dkp, dkc, dvp, dvc, ds = _attn_block_bwd(*args, [s_ref[g] for g in range(kv)], _q_groups(do_ref, kv, rep), prev_bias)
        dq_ref[...] = _q_ungroup(dq, rep)
        for ref, val in ((dkc_ref, dkc), (dkp_ref, dkp), (dvc_ref, dvc), (dvp_ref, dvp)):
            ref[...] = jnp.concatenate(val, axis=1)

        @pl.when(n == 0)
        def _():
            ds_ref[...] = jnp.zeros_like(ds_ref)

        for g in range(kv):
            ds_ref[g] += ds[g]

    qs = pl.BlockSpec((SWA_BLOCK, d), lambda n: (n, 0))
    cur = pl.BlockSpec((SWA_BLOCK, kv * HEAD), lambda n: (n, 0))
    prev = pl.BlockSpec((SWA_BLOCK, kv * HEAD), lambda n: (jnp.maximum(n - 1, 0), 0))
    ss = pl.BlockSpec(sinks.shape, lambda n: (0, 0, 0, 0))
    return _pcall(body, grid=(nb,), in_specs=[qs, prev, cur, prev, cur, ss, qs], out_specs=[qs, cur, cur, cur, cur, ss],
                  out_shape=[S((t, d), F32)] + [S((t, kv * HEAD), F32)] * 4 + [S(sinks.shape, F32)], name=name,
                  semantics=("arbitrary",), inputs=(q, k, k, v, v, sinks, do), jobs=jobs)


def _gate_fwd(name, o, z_src, z_off, d):
    return _rows(name, lambda o_, p_: o_ * jax.nn.silu(p_[:, z_off:z_off + d]), [o, z_src], [], [(d, BF16)], [], 256)[0]


def _gate_bwd(name, o, z_src, z_off, d, dmix):
    def fn(o_, p_, dm_):
        _, vjp = jax.vjp(lambda oo, zz: oo * jax.nn.silu(zz), o_, p_[:, z_off:z_off + d])
        return vjp(dm_)

    return _rows(name, fn, [o, z_src, dmix], [], [(d, F32), (d, F32)], [], 256)


def _swa_post_bwd(name, dq, dkc, dkp_up, dvc, dvp_up, dz, pos, inv_freq):
    def fn(dq_, dkc_, dkp_, dvc_, dvp_, dz_, pos_, fr):
        cos, sin = _rope_tables(pos_, fr)
        return jnp.concatenate([_rope(dq_, cos, sin, -1.0), _rope(dkc_ + dkp_, cos, sin, -1.0), dvc_ + dvp_, dz_], axis=1)

    n = dq.shape[1] + dkc.shape[1] + dvc.shape[1] + dz.shape[1]
    return _rows(name, fn, [dq, dkc, dkp_up, dvc, dvp_up, dz, pos], [inv_freq], [(n, BF16)], [], 256)[0]


HALO = 8


def _row_before(x, halo_ref, i):
    first = jnp.where(i > 0, halo_ref[pl.ds(HALO - 1, 1), :], 0.0)
    row = lax.broadcasted_iota(jnp.int32, x.shape, 0)
    return jnp.where(row == 0, first, pltpu.roll(x, 1, 0))


def _row_after(x, halo, i, n_tiles):
    last = jnp.where(i < n_tiles - 1, halo, 0.0)
    row = lax.broadcasted_iota(jnp.int32, x.shape, 0)
    return jnp.where(row == x.shape[0] - 1, last, pltpu.roll(x, x.shape[0] - 1, 0))


def _lerp_fwd(name, p, mu, widths):
    t, n = p.shape
    tm = 128

    def body(p_ref, halo_ref, mu_ref, *o_refs):
        x = p_ref[...]
        pm = x + (_row_before(x, halo_ref, pl.program_id(0)) - x) * mu_ref[...]
        o = 0
        for ref, w in zip(o_refs, widths):
            ref[...] = pm[:, o:o + w]
            o += w

    return pl.pallas_call(
        body, grid=(t // tm,),
        in_specs=[pl.BlockSpec((tm, n), lambda i: (i, 0)),
                  pl.BlockSpec((HALO, n), lambda i: (jnp.maximum(i * (tm // HALO) - 1, 0), 0)),
                  pl.BlockSpec((1, n), lambda i: (0, 0))],
        out_specs=[pl.BlockSpec((tm, w), lambda i: (i, 0)) for w in widths],
        out_shape=[S((t, w), F32) for w in widths], name=name, compiler_params=_params(("parallel",)))(p, p, mu)


def _lerp_bwd(name, dpm_parts, p, mu):
    t, n = p.shape
    k = len(dpm_parts)
    tm = 64
    n_tiles = t // tm

    def body(*refs):
        d_refs, dh_refs = refs[:k], refs[k:2 * k]
        p_ref, ph_ref, mu_ref, dp_ref, dmu_ref = refs[2 * k:]
        i = pl.program_id(0)
        cat = lambda vals: jnp.concatenate(vals, axis=1) if k > 1 else vals[0]
        dpm = cat([r[...] for r in d_refs])
        dnext = cat([r[pl.ds(0, 1), :] for r in dh_refs])
        x, mu_ = p_ref[...], mu_ref[...]
        dp_ref[...] = (dpm * (1.0 - mu_) + _row_after(dpm, dnext, i, n_tiles) * mu_).astype(dp_ref.dtype)
        dmu = jnp.sum(dpm * (_row_before(x, ph_ref, i) - x), axis=0, keepdims=True)

        @pl.when(i == 0)
        def _():
            dmu_ref[...] = dmu

        @pl.when(i > 0)
        def _():
            dmu_ref[...] += dmu

    per = tm // HALO
    d_specs = [pl.BlockSpec((tm, a.shape[1]), lambda i: (i, 0)) for a in dpm_parts]
    dh_specs = [pl.BlockSpec((HALO, a.shape[1]), lambda i: (jnp.minimum((i + 1) * per, t // HALO - 1), 0)) for a in dpm_parts]
    return pl.pallas_call(
        body, grid=(n_tiles,),
        in_specs=d_specs + dh_specs + [pl.BlockSpec((tm, n), lambda i: (i, 0)),
                                       pl.BlockSpec((HALO, n), lambda i: (jnp.maximum(i * per - 1, 0), 0)),
                                       pl.BlockSpec((1, n), lambda i: (0, 0))],
        out_specs=[pl.BlockSpec((tm, n), lambda i: (i, 0)), pl.BlockSpec((1, n), lambda i: (0, 0))],
        out_shape=[S((t, n), BF16), S((1, n), F32)], name=name,
        compiler_params=_params(("arbitrary",)))(*dpm_parts, *dpm_parts, p, p, mu)


def _lora_act(pl_, w0, w_lora, a0, a_lora):
    logw = -DECAY_SCALE * jax.nn.sigmoid(w0 + _dot(jnp.tanh(pl_), w_lora))
    a = jax.nn.sigmoid(a0 + _dot(pl_, a_lora))
    return logw, a


def _lora_fwd(name, pl_, w0, w_lora, a0, a_lora):
    d = w0.shape[1]
    return _rows(name, _lora_act, [pl_], [w0, w_lora, a0, a_lora], [(d, F32), (d, F32)], [], 256)


def _lora_bwd(name, pl_, dlogw, da, w0, w_lora, a0, a_lora):
    def fn(p_, dl_, da_, w0_, wl_, a0_, al_):
        _, vjp = jax.vjp(_lora_act, p_, w0_, wl_, a0_, al_)
        return vjp((dl_, da_))

    return _rows(name, fn, [pl_, dlogw, da], [w0, w_lora, a0, a_lora], [(pl_.shape[1], F32)],
                 [w0.shape, w_lora.shape, a0.shape, a_lora.shape], 256)


def _head_cols(ref, hb):
    x = ref[...].astype(F32)
    xo = pltpu.roll(x, x.shape[1] - HEAD, 1)
    return [(x if j % 2 == 0 else xo)[:, 2 * HEAD * (j // 2):2 * HEAD * (j // 2) + HEAD] for j in range(hb)]


def _rwkv_scan_fwd(name, r, k, v, logw, a, hp, jobs=()):
    t, d = r.shape
    h, nc, hb = d // HEAD, t // RW_CHUNK, RW_HEADS_FWD

    def body(r_ref, k_ref, v_ref, w_ref, a_ref, kk_ref, ka_ref, rk_ref, gg_ref, gb_ref, y_ref, st_ref, s_scr):
        @pl.when(pl.program_id(1) == 0)
        def _():
            s_scr[...] = jnp.zeros_like(s_scr)

        s0 = [s_scr[j] for j in range(hb)]
        for j in range(hb):
            st_ref[j, 0] = s0[j]
        y, s1 = _rwkv_chunk(s0, *[_head_cols(ref, hb) for ref in (r_ref, k_ref, v_ref, w_ref, a_ref, kk_ref, ka_ref, rk_ref,
                                                                 gg_ref, gb_ref)])
        y_ref[...] = jnp.concatenate(y, axis=1)
        for j in range(hb):
            s_scr[j] = s1[j]

    seq = pl.BlockSpec((RW_CHUNK, hb * HEAD), lambda i, n: (n, i))
    par = pl.BlockSpec((1, hb * HEAD), lambda i, n: (0, i))
    st = pl.BlockSpec((hb, 1, HEAD, HEAD), lambda i, n: (i, n, 0, 0))
    return _pcall(body, grid=(h // hb, nc), in_specs=[seq] * 5 + [par] * 5, out_specs=[seq, st],
                  out_shape=[S((t, d), F32), S((h, nc, HEAD, HEAD), F32)], scratch_shapes=[pltpu.VMEM((hb, HEAD, HEAD), F32)],
                  name=name, semantics=("parallel", "arbitrary"), inputs=(r, k, v, logw, a, *hp), jobs=jobs)


def _rwkv_scan_bwd(name, r, k, v, logw, a, hp, states, dy, jobs=()):
    t, d = r.shape
    h, nc, hb = d // HEAD, t // RW_CHUNK, RW_HEADS

    def body(r_ref, k_ref, v_ref, w_ref, a_ref, kk_ref, ka_ref, rk_ref, gg_ref, gb_ref, st_ref, dy_ref,
             dr_ref, dk_ref, dv_ref, dw_ref, da_ref, dkk_ref, dka_ref, drk_ref, dgg_ref, dgb_ref, ds_scr):
        n = pl.program_id(1)

        @pl.when(n == 0)
        def _():
            ds_scr[...] = jnp.zeros_like(ds_scr)
            for ref in (dkk_ref, dka_ref, drk_ref, dgg_ref, dgb_ref):
                ref[...] = jnp.zeros_like(ref)

        ins = [[st_ref[j, 0] for j in range(hb)]] + [_head_cols(ref, hb) for ref in (r_ref, k_ref, v_ref, w_ref, a_ref, kk_ref,
                                                                                  ka_ref, rk_ref, gg_ref, gb_ref)]
        _, vjp = jax.vjp(_rwkv_chunk, *ins)
        ds0, *dseq, dkk, dka, drk, dgg, dgb = vjp((_head_cols(dy_ref, hb), [ds_scr[j] for j in range(hb)]))
        for j in range(hb):
            ds_scr[j] = ds0[j]
        for ref, val in zip((dr_ref, dk_ref, dv_ref, dw_ref, da_ref), dseq):
            ref[...] = jnp.concatenate(val, axis=1)
        for ref, val in ((dkk_ref, dkk), (dka_ref, dka), (drk_ref, drk), (dgg_ref, dgg), (dgb_ref, dgb)):
            ref[...] += jnp.concatenate(val, axis=1)

    seq = pl.BlockSpec((RW_CHUNK, hb * HEAD), lambda i, n: (nc - 1 - n, i))
    par = pl.BlockSpec((1, hb * HEAD), lambda i, n: (0, i))
    st = pl.BlockSpec((hb, 1, HEAD, HEAD), lambda i, n: (i, nc - 1 - n, 0, 0))
    return _pcall(body, grid=(h // hb, nc), in_specs=[seq] * 5 + [par] * 5 + [st, seq], out_specs=[seq] * 5 + [par] * 5,
                  out_shape=[S((t, d), F32)] * 5 + [S((1, d), F32)] * 5, scratch_shapes=[pltpu.VMEM((hb, HEAD, HEAD), F32)],
                  name=name, semantics=("parallel", "arbitrary"), inputs=(r, k, v, logw, a, *hp, states, dy), jobs=jobs)


def _loss_head(name, x, target, g):
    d = x.shape[1]

    def fn(x_, t_, g_):
        def f(xx, gg):
            err = _rms(xx, gg) - t_
            return 0.5 * jnp.sum(jnp.mean(err * err, axis=-1, keepdims=True), axis=0, keepdims=True)

        l, vjp = jax.vjp(f, x_, g_)
        dx, dg = vjp(jnp.ones((1, 1), F32))
        return dx, dg, jnp.broadcast_to(l, (1, 128))

    return _rows(name, fn, [x, target], [g], [(d, F32)], [(1, d), (1, 128)], 256)


def _mod_fwd(name, cond_all, mod_w, mod_b_cols):
    l, d, n = mod_w.shape

    def body(c_ref, w_ref, b_ref, o_ref):
        o_ref[0] = _dot(jax.nn.silu(c_ref[...]), w_ref[0], hi=True) + b_ref[0]

    return pl.pallas_call(body, grid=(l,), in_specs=[pl.BlockSpec((N_DEV, d), lambda i: (0, 0)),
                                                      pl.BlockSpec((1, d, n), lambda i: (i, 0, 0)),
                                                      pl.BlockSpec((1, 1, n), lambda i: (i, 0, 0))],
                          out_specs=pl.BlockSpec((1, N_DEV, n), lambda i: (i, 0, 0)), out_shape=S((l, N_DEV, n), F32),
                          name=name, compiler_params=_params(("parallel",)))(cond_all, mod_w, mod_b_cols)


def _mod_bwd(name, cond_all, dmod_cols, dmod_all):
    l, _, n = dmod_cols.shape
    d = cond_all.shape[1]
    nb = dmod_all.shape[2]

    def body(c_ref, dc_ref, da_ref, gw_ref, gb_ref):
        gw_ref[0] = _dot(jax.nn.silu(c_ref[...]), dc_ref[0], "tn", hi=True)
        acc = da_ref[0, 0:1, :]
        for bi in range(1, N_DEV):
            acc = acc + da_ref[0, bi:bi + 1, :]
        gb_ref[0] = acc

    return pl.pallas_call(body, grid=(l,), in_specs=[pl.BlockSpec((N_DEV, d), lambda i: (0, 0)),
                                                      pl.BlockSpec((1, N_DEV, n), lambda i: (i, 0, 0)),
                                                      pl.BlockSpec((1, N_DEV, nb), lambda i: (i, 0, 0))],
                          out_specs=[pl.BlockSpec((1, d, n), lambda i: (i, 0, 0)), pl.BlockSpec((1, 1, nb), lambda i: (i, 0, 0))],
                          out_shape=[S((l, d, n), F32), S((l, 1, nb), F32)], name=name,
                          compiler_params=_params(("parallel",)))(cond_all, dmod_cols, dmod_all)


def _shift_up(a, n=1):
    return jnp.concatenate([a[n:], jnp.zeros_like(a[:n])], axis=0)


def _cols_full(g):
    return g.transpose(1, 0, 2).reshape(g.shape[1], -1)


def _cols_parts(full):
    r, n = full.shape
    return full.reshape(r, N_DEV, n // N_DEV).transpose(1, 0, 2)


def _pack(arrs, mult=1024):
    flat = jnp.concatenate([a.reshape(-1) for a in arrs])
    pad = (-flat.shape[0]) % mult
    return jnp.pad(flat, (0, pad)).reshape(-1, 128)


def _unpack(flat, shapes):
    out, o = [], 0
    for s in shapes:
        n = math.prod(s)
        out.append(flat[o:o + n].reshape(s))
        o += n
    return out


def _local_step(x, pos, target, mods, norm_g, final_norm_g, layer_weights, hooks=None, on_grads=None):
    t, d = x.shape
    hooks = hooks or {}
    jobs = lambda nm: hooks.get(nm, ())
    notify = on_grads or (lambda *a: None)
    kinds = [i % 3 for i in range(DEPTH)]
    inv_freq = (ROPE_THETA ** (-jnp.arange(HEAD // 2, dtype=F32) / (HEAD // 2)))
    inv_freq = jnp.tile(inv_freq, 128 // (HEAD // 2)).reshape(1, 128)
    saved = []
    for i, kind in enumerate(kinds):
        lw = layer_weights(i)
        shift, scale, gate = (mods[i, q * d:(q + 1) * d].reshape(1, d) for q in range(3))
        g = norm_g[i].reshape(1, d)
        h = _norm_fwd(f"norm_fwd{i}", x, g, shift, scale)
        sv = dict(x=x, h=h, g=g, shift=shift, scale=scale, gate=gate, lw=lw)
        if kind == 0:
            p = _mm(f"sg_in{i}", h, lw["w_in"], "nn", F32, jobs=jobs(f"sg_in{i}"))
            mix = _sg_fwd(f"sg_mix{i}", p, lw["ln_g"], lw["ln_b"], lw["w_s"], lw["b_st"])
            sv.update(p=p)
        elif kind == 1:
            p = _mm(f"swa_in{i}", h, lw["w_in"], "nn", F32, jobs=jobs(f"swa_in{i}"))
            q, k, v = _swa_pre(f"swa_pre{i}", p, pos, inv_freq, d)
            o = _swa_attn_fwd(f"swa_attn{i}", q, k, v, lw["sinks"], jobs=jobs(f"swa_attn{i}"))
            mix = _gate_fwd(f"swa_gate{i}", o, p, d + 2 * SWA_KV * HEAD, d)
            sv.update(p=p, qkv=(q, k, v), o=o)
        else:
            pm = _mm(f"rw_in{i}", h, lw["w_main"], "nn", F32)
            plo = _mm(f"rw_inl{i}", h, lw["w_lorain"], "nn", F32)
            r, k, v, z = _lerp_fwd(f"rw_lerp{i}", pm, lw["mu_main"], [d] * 4)
            (pll,) = _lerp_fwd(f"rw_lerpl{i}", plo, lw["mu_lora"], [LORA_PAD])
            logw, a = _lora_fwd(f"rw_lora{i}", pll, lw["w0"], lw["w_lora"], lw["a0"], lw["a_lora"])
            seqs = (r, k, v, logw, a)
            o, states = _rwkv_scan_fwd(f"rw_scan{i}", *seqs, lw["hp"], jobs=jobs(f"rw_scan{i}"))
            mix = _gate_fwd(f"rw_gate{i}", o, z, 0, d)
            sv.update(pm=pm, plo=plo, pll=pll, z=z, seqs=seqs, states=states, o=o)
        y, x = _mm(f"out{i}", mix, lw["w_out"], "nn", F32, resid=(x, gate), jobs=jobs(f"out{i}"))
        sv.update(mix=mix, y=y)
        saved.append(sv)

    dx, d_final_g, loss = _loss_head("loss_head", x, target, final_norm_g.reshape(1, d))

    grads = dict(norm_g=[None] * DEPTH, sg_w_in=[None] * 2, sg_w_out=[None] * 2, sg_ln_g=[None] * 2, sg_ln_b=[None] * 2,
                 sg_w_s=[None] * 2, sg_b_st=[None] * 2, final_norm_g=d_final_g)
    dmods = [None] * DEPTH
    for i in reversed(range(DEPTH)):
        kind, j, sv = kinds[i], i // 3, saved[i]
        lw = sv["lw"]
        dy, dgate = _resid_bwd(f"resid_bwd{i}", dx, sv["y"], sv["gate"])
        d_w_out = _mm(f"out_dw{i}", sv["mix"], dy, "tn", BF16)
        if kind == 0:
            grads["sg_w_out"][j] = d_w_out
        else:
            grads[("swa_w_out", "rw_w_out")[kind - 1]] = d_w_out
        notify(i, "out", grads)
        dmix = _mm(f"out_dx{i}", dy, lw["w_out"], "nt", F32, jobs=jobs(f"out_dx{i}"))
        if kind == 0:
            dp, dlg, dlb, dws, dbs = _sg_bwd(f"sg_mix_bwd{i}", sv["p"], dmix, lw["ln_g"], lw["ln_b"], lw["w_s"], lw["b_st"],
                                            jobs=jobs(f"sg_mix_bwd{i}"))
            grads["sg_ln_g"][j], grads["sg_ln_b"][j], grads["sg_w_s"][j], grads["sg_b_st"][j] = dlg, dlb, dws, dbs
            grads["sg_w_in"][j] = _mm(f"sg_in_dw{i}", sv["h"], dp, "tn", BF16, jobs=jobs(f"sg_in_dw{i}"))
            notify(i, "in", grads)
            dh = _mm(f"sg_in_dx{i}", dp, lw["w_in"], "nt", F32, jobs=jobs(f"sg_in_dx{i}"))
        elif kind == 1:
            z_off = d + 2 * SWA_KV * HEAD
            do, dz = _gate_bwd(f"swa_gate_bwd{i}", sv["o"], sv["p"], z_off, d, dmix)
            dq, dkc, dkp, dvc, dvp, dsinks = _swa_attn_bwd(f"swa_attn_bwd{i}", *sv["qkv"], lw["sinks"], do,
                                                           jobs=jobs(f"swa_attn_bwd{i}"))
            dkp, dvp = _shift_up(dkp, SWA_BLOCK), _shift_up(dvp, SWA_BLOCK)
            dp = _swa_post_bwd(f"swa_post_bwd{i}", dq, dkc, dkp, dvc, dvp, dz, pos, inv_freq)
            grads.update(swa_sinks=dsinks, swa_w_out=d_w_out)
            grads["swa_w_in"] = _mm(f"swa_in_dw{i}", sv["h"], dp, "tn", BF16)
            dh = _mm(f"swa_in_dx{i}", dp, lw["w_in"], "nt", F32)
        else:
            do, dz = _gate_bwd(f"rw_gate_bwd{i}", sv["o"], sv["z"], 0, d, dmix)
            res = _rwkv_scan_bwd(f"rw_scan_bwd{i}", *sv["seqs"], lw["hp"], sv["states"], do, jobs=jobs(f"rw_scan_bwd{i}"))
            dr, dk, dv, dlogw, da = res[:5]
            dpll, dw0, dwl, da0, dal = _lora_bwd(f"rw_lora_bwd{i}", sv["pll"], dlogw, da, lw["w0"], lw["w_lora"],
                                                  lw["a0"], lw["a_lora"])
            dpm, dmu_main = _lerp_bwd(f"rw_lerp_bwd{i}", [dr, dk, dv, dz], sv["pm"], lw["mu_main"])
            dpl, dmu_lora = _lerp_bwd(f"rw_lerpl_bwd{i}", [dpll], sv["plo"], lw["mu_lora"])
            grads.update(rw_w_out=d_w_out, rw_hp=res[5:], rw_w0=dw0, rw_w_lora=dwl, rw_a0=da0, rw_a_lora=dal,
                         rw_mu_main=dmu_main, rw_mu_lora=dmu_lora)
            grads["rw_w_main"] = _mm(f"rw_in_dw{i}", sv["h"], dpm, "tn", BF16)
            grads["rw_w_lorain"] = _mm(f"rw_inl_dw{i}", sv["h"], dpl, "tn", BF16)
            dh = _mm(f"rw_inl_dx{i}", dpl, lw["w_lorain"], "nt", F32)
            dh = _mm(f"rw_in_dx{i}", dpm, lw["w_main"], "nt", F32, add=dh)
        if kind != 0:
            notify(i, "in", grads)
        dx, dg, dshift, dscale = _norm_bwd(f"norm_bwd{i}", sv["x"], dh, dx, sv["g"], sv["shift"], sv["scale"])
        grads["norm_g"][i] = dg
        dmods[i] = jnp.concatenate([dshift, dscale, dgate], axis=1)
    return loss, dx, jnp.concatenate(dmods, axis=0), grads


def kernel(x, c, positions, norm_g, mod_w, mod_b, final_norm_g, sg_w_in, sg_w_out, sg_ln_g, sg_ln_b, sg_w_spatial, sg_b_spatial, swa_w_in, swa_w_out, swa_sinks, rwkv_w_in, rwkv_w_out, rwkv_mu, rwkv_w0, rwkv_w_lora, rwkv_a0, rwkv_a_lora, rwkv_k_k, rwkv_k_a, rwkv_r_k, rwkv_gn_g, rwkv_gn_b, loss_target, m_norm_g, m_mod_w, m_mod_b, m_final_norm_g, m_sg_w_in, m_sg_w_out, m_sg_ln_g, m_sg_ln_b, m_sg_w_spatial, m_sg_b_spatial, m_swa_w_in, m_swa_w_out, m_swa_sinks, m_rwkv_w_in, m_rwkv_w_out, m_rwkv_mu, m_rwkv_w0, m_rwkv_w_lora, m_rwkv_a0, m_rwkv_a_lora, m_rwkv_k_k, m_rwkv_k_a, m_rwkv_r_k, m_rwkv_gn_g, m_rwkv_gn_b, v_norm_g, v_mod_w, v_mod_b, v_final_norm_g, v_sg_w_in, v_sg_w_out, v_sg_ln_g, v_sg_ln_b, v_sg_w_spatial, v_sg_b_spatial, v_swa_w_in, v_swa_w_out, v_swa_sinks, v_rwkv_w_in, v_rwkv_w_out, v_rwkv_mu, v_rwkv_w0, v_rwkv_w_lora, v_rwkv_a0, v_rwkv_a_lora, v_rwkv_k_k, v_rwkv_k_a, v_rwkv_r_k, v_rwkv_gn_g, v_rwkv_gn_b):
    weights = dict(norm_g=norm_g, mod_w=mod_w, mod_b=mod_b, final_norm_g=final_norm_g, sg_w_in=sg_w_in, sg_w_out=sg_w_out,
                   sg_ln_g=sg_ln_g, sg_ln_b=sg_ln_b, sg_w_spatial=sg_w_spatial, sg_b_spatial=sg_b_spatial, swa_w_in=swa_w_in,
                   swa_w_out=swa_w_out, swa_sinks=swa_sinks, rwkv_w_in=rwkv_w_in, rwkv_w_out=rwkv_w_out, rwkv_mu=rwkv_mu,
                   rwkv_w0=rwkv_w0, rwkv_w_lora=rwkv_w_lora, rwkv_a0=rwkv_a0, rwkv_a_lora=rwkv_a_lora, rwkv_k_k=rwkv_k_k,
                   rwkv_k_a=rwkv_k_a, rwkv_r_k=rwkv_r_k, rwkv_gn_g=rwkv_gn_g, rwkv_gn_b=rwkv_gn_b)
    mom_m = dict(norm_g=m_norm_g, mod_w=m_mod_w, mod_b=m_mod_b, final_norm_g=m_final_norm_g, sg_w_in=m_sg_w_in,
                 sg_w_out=m_sg_w_out, sg_ln_g=m_sg_ln_g, sg_ln_b=m_sg_ln_b, sg_w_spatial=m_sg_w_spatial,
                 sg_b_spatial=m_sg_b_spatial, swa_w_in=m_swa_w_in, swa_w_out=m_swa_w_out, swa_sinks=m_swa_sinks,
                 rwkv_w_in=m_rwkv_w_in, rwkv_w_out=m_rwkv_w_out, rwkv_mu=m_rwkv_mu, rwkv_w0=m_rwkv_w0,
                 rwkv_w_lora=m_rwkv_w_lora, rwkv_a0=m_rwkv_a0, rwkv_a_lora=m_rwkv_a_lora, rwkv_k_k=m_rwkv_k_k,
                 rwkv_k_a=m_rwkv_k_a, rwkv_r_k=m_rwkv_r_k, rwkv_gn_g=m_rwkv_gn_g, rwkv_gn_b=m_rwkv_gn_b)
    mom_v = dict(norm_g=v_norm_g, mod_w=v_mod_w, mod_b=v_mod_b, final_norm_g=v_final_norm_g, sg_w_in=v_sg_w_in,
                 sg_w_out=v_sg_w_out, sg_ln_g=v_sg_ln_g, sg_ln_b=v_sg_ln_b, sg_w_spatial=v_sg_w_spatial,
                 sg_b_spatial=v_sg_b_spatial, swa_w_in=v_swa_w_in, swa_w_out=v_swa_w_out, swa_sinks=v_swa_sinks,
                 rwkv_w_in=v_rwkv_w_in, rwkv_w_out=v_rwkv_w_out, rwkv_mu=v_rwkv_mu, rwkv_w0=v_rwkv_w0,
                 rwkv_w_lora=v_rwkv_w_lora, rwkv_a0=v_rwkv_a0, rwkv_a_lora=v_rwkv_a_lora, rwkv_k_k=v_rwkv_k_k,
                 rwkv_k_a=v_rwkv_k_a, rwkv_r_k=v_rwkv_r_k, rwkv_gn_g=v_rwkv_gn_g, rwkv_gn_b=v_rwkv_gn_b)
    names = list(weights)
    t, d = x.shape[1], x.shape[2]
    me = 4 * lax.axis_index("x") + 2 * lax.axis_index("y") + lax.axis_index("c")
    n_mod = mod_w.shape[2]
    n_rw = rwkv_w_in.shape[2]

    small_names = ["sg_ln_g", "sg_ln_b", "rwkv_mu", "rwkv_w0", "rwkv_a0", "rwkv_k_k", "rwkv_k_a", "rwkv_gn_g", "rwkv_gn_b",
                   "rwkv_w_lora", "rwkv_a_lora"]
    small_shapes = [weights[n].shape for n in small_names]
    pk = _pack([c] + [weights[n] for n in small_names])
    gathered = _gather("gather_small", pk).reshape(N_DEV, -1)
    c_all = gathered[:, :d]
    per_dev = [_unpack(gathered[dv, d:], small_shapes) for dv in range(N_DEV)]
    full_small = {}
    for q, n in enumerate(small_names):
        full_small[n] = jnp.concatenate([per_dev[dv][q] for dv in range(N_DEV)], axis=-1)

    mod_b_cols = lax.dynamic_slice_in_dim(mod_b, me * n_mod, n_mod, axis=1).reshape(DEPTH, 1, n_mod)
    mod_part = _mod_fwd("mod_fwd", c_all, mod_w, mod_b_cols)
    mod_g = _gather("gather_mod", mod_part.reshape(DEPTH * N_DEV, n_mod))
    mod_g = mod_g.reshape(N_DEV, DEPTH, N_DEV, n_mod)
    mods = lax.dynamic_index_in_dim(mod_g, me, axis=2, keepdims=False)
    mods = mods.transpose(1, 0, 2).reshape(DEPTH, N_DEV * n_mod)

    job = lambda cls, src: dict(cls=cls, src=src)
    gj = dict(swa_in=job(_Gather, swa_w_in[0].astype(BF16)), swa_out=job(_Gather, swa_w_out[0].astype(BF16)),
              rw_in=job(_Gather, rwkv_w_in[0].astype(BF16)), rw_out=job(_Gather, rwkv_w_out[0].astype(BF16)),
              sg_in1=job(_Gather, sg_w_in[1].astype(BF16)), sg_out1=job(_Gather, sg_w_out[1].astype(BF16)))
    hooks = {"sg_in0": [gj["swa_in"]], "out0": [gj["swa_out"]], "swa_in1": [gj["rw_out"]], "swa_attn1": [gj["rw_in"]],
             "rw_scan2": [gj["sg_in1"], gj["sg_out1"]]}
    g_sg_in0 = _gather("gather_sg_in0", sg_w_in[0].astype(BF16))
    g_sg_out0 = _gather("gather_sg_out0", sg_w_out[0].astype(BF16))
    lora_rows = lambda w, off: jnp.zeros((LORA_PAD, d), F32).at[off:off + LORA].set(w)
    mu = full_small["rwkv_mu"].reshape(1, -1)
    heads = lambda a: a.reshape(1, -1)

    def layer_weights(i):
        if i % 3 == 0:
            j = i // 3
            g_in, g_out = (g_sg_in0, g_sg_out0) if j == 0 else (gj["sg_in1"]["out"], gj["sg_out1"]["out"])
            return dict(w_in=_cols_full(g_in), w_out=g_out.reshape(d, d), ln_g=full_small["sg_ln_g"][j].reshape(1, d),
                        ln_b=full_small["sg_ln_b"][j].reshape(1, d), w_s=sg_w_spatial[j], b_st=sg_b_spatial[j].T)
        if i % 3 == 1:
            return dict(w_in=_cols_full(gj["swa_in"]["out"]), w_out=gj["swa_out"]["out"].reshape(d, d),
                        sinks=swa_sinks.reshape(SWA_KV, SWA_REP, 1, 1))
        rw_in_full = _cols_full(gj["rw_in"]["out"])
        return dict(w_main=rw_in_full[:, :4 * d], w_lorain=jnp.pad(rw_in_full[:, 4 * d:], ((0, 0), (0, LORA_PAD - 2 * LORA))),
                    w_out=gj["rw_out"]["out"].reshape(d, d), mu_main=mu[:, :4 * d],
                    mu_lora=jnp.pad(mu[:, 4 * d:], ((0, 0), (0, LORA_PAD - 2 * LORA))),
                    w0=full_small["rwkv_w0"], a0=full_small["rwkv_a0"],
                    w_lora=lora_rows(full_small["rwkv_w_lora"][0], 0), a_lora=lora_rows(full_small["rwkv_a_lora"][0], LORA),
                    hp=[heads(full_small["rwkv_k_k"]), heads(full_small["rwkv_k_a"]), heads(rwkv_r_k),
                        heads(full_small["rwkv_gn_g"]), heads(full_small["rwkv_gn_b"])])

    sj = {}

    def on_grads(i, which, g):
        def stage(nm, parts, host):
            sj[nm] = job(_Chips, _scatter_pairs("scatter_" + nm, parts.astype(BF16)))
            hooks.setdefault(host, []).append(sj[nm])

        rows_of = lambda a: a.reshape(N_DEV, -1, d)
        if (i, which) == (3, "out"):
            stage("sg_out1", rows_of(g["sg_w_out"][1]), "rw_scan_bwd2")
        elif (i, which) == (3, "in"):
            stage("sg_in1", _cols_parts(g["sg_w_in"][1]), "rw_scan_bwd2")
        elif (i, which) == (2, "out"):
            stage("rw_out", rows_of(g["rw_w_out"]), "swa_attn_bwd1")
        elif (i, which) == (2, "in"):
            d_rw_in = jnp.concatenate([g["rw_w_main"], g["rw_w_lorain"][:, :2 * LORA]], axis=1)
            stage("rw_in", _cols_parts(d_rw_in), "swa_attn_bwd1")
        elif (i, which) == (1, "out"):
            stage("swa_out", rows_of(g["swa_w_out"]), "out_dx0")
        elif (i, which) == (1, "in"):
            stage("swa_in", _cols_parts(g["swa_w_in"]), "sg_mix_bwd0")
        elif (i, which) == (0, "out"):
            stage("sg_out0", rows_of(g["sg_w_out"][0]), "sg_in_dw0")
        else:
            stage("sg_in0", _cols_parts(g["sg_w_in"][0]), "sg_in_dx0")

    loss, dx, dmods, g = _local_step(x[0], positions.reshape(t, 1).astype(F32), loss_target[0], mods, norm_g, final_norm_g,
                                     layer_weights, hooks, on_grads)

    dmod_g = _gather("gather_dmod", dmods)
    dmod_all = dmod_g.transpose(1, 0, 2)
    dmod_cols = lax.dynamic_slice_in_dim(dmod_all, me * n_mod, n_mod, axis=2)
    g_mod_w, g_mod_b = _mod_bwd("mod_bwd", c_all, dmod_cols, dmod_all)

    d_b_sp = [g["sg_b_st"][j].T for j in range(2)]
    rep = [loss[0, :1], jnp.concatenate(g["norm_g"], axis=0), g["final_norm_g"], jnp.stack(g["sg_w_s"]), jnp.stack(d_b_sp),
           g["swa_sinks"], g["rw_hp"][2]]
    rep_shapes = [(1,), norm_g.shape, final_norm_g.shape, sg_w_spatial.shape, sg_b_spatial.shape, swa_sinks.shape, rwkv_r_k.shape]
    rep_sum = _sum_parts("sum_rep", _gather("gather_rep", _pack(rep, 128 * 256))).reshape(-1)
    loss_tot, g_norm_g, g_final, g_w_sp, g_b_sp, g_sinks, g_r_k = _unpack(rep_sum, rep_shapes)

    p_sg_in = jnp.concatenate([sj["sg_in0"]["out"], sj["sg_in1"]["out"]], axis=1)
    p_sg_out = jnp.concatenate([sj["sg_out0"]["out"], sj["sg_out1"]["out"]], axis=1)
    p_swa_in, p_swa_out, p_rw_in, p_rw_out = (sj[nm]["out"] for nm in ("swa_in", "swa_out", "rw_in", "rw_out"))
    d_mu = jnp.concatenate([g["rw_mu_main"], g["rw_mu_lora"][:, :2 * LORA]], axis=1)
    hp_flat = lambda a: a.reshape(1, -1)
    small_grads = dict(sg_ln_g=jnp.concatenate(g["sg_ln_g"], axis=0), sg_ln_b=jnp.concatenate(g["sg_ln_b"], axis=0), rwkv_mu=d_mu,
                       rwkv_w0=g["rw_w0"], rwkv_a0=g["rw_a0"], rwkv_k_k=hp_flat(g["rw_hp"][0]), rwkv_k_a=hp_flat(g["rw_hp"][1]),
                       rwkv_gn_g=hp_flat(g["rw_hp"][3]), rwkv_gn_b=hp_flat(g["rw_hp"][4]),
                       rwkv_w_lora=g["rw_w_lora"][None, :LORA], rwkv_a_lora=g["rw_a_lora"][None, LORA:2 * LORA])
    per_dest = []
    for dv in range(N_DEV):
        shards = []
        for n in small_names:
            full, w = small_grads[n], weights[n].shape[-1]
            shards.append(full[..., dv * w:(dv + 1) * w])
        per_dest.append(_pack(shards))
    small_parts = _exchange("scatter_small", jnp.stack(per_dest), True)

    out_g, out_d, out_m, out_v = {}, {}, {}, {}

    def update(name, grad, shape2d, jobs=()):
        w2, m2, v2 = (a[name].reshape(shape2d) for a in (weights, mom_m, mom_v))
        gg, dd, mm, vv = _adamw("adamw_" + name, w2, grad, m2, v2, jobs=jobs)
        shp = weights[name].shape
        out_g[name], out_d[name], out_m[name], out_v[name] = gg.reshape(shp), dd.reshape(shp), mm.reshape(shp), vv.reshape(shp)

    update("mod_w", g_mod_w.reshape(-1, n_mod), (-1, n_mod))
    update("sg_w_in", p_sg_in, (-1, sg_w_in.shape[2]))
    update("sg_w_out", p_sg_out, (-1, d))
    update("swa_w_in", p_swa_in, (-1, swa_w_in.shape[2]))
    update("swa_w_out", p_swa_out, (-1, d))
    update("rwkv_w_in", p_rw_in, (-1, n_rw))
    update("rwkv_w_out", p_rw_out, (-1, d))
    update("sg_w_spatial", g_w_sp.reshape(-1, 128), (-1, 128))
    w_pk, m_pk, v_pk = (_pack([a[n] for n in small_names]) for a in (weights, mom_m, mom_v))
    res = _adamw("adamw_small", w_pk, small_parts, m_pk, v_pk)
    for q, arrs in enumerate(zip(*[_unpack(r_.reshape(-1), small_shapes) for r_ in res])):
        out_g[small_names[q]], out_d[small_names[q]], out_m[small_names[q]], out_v[small_names[q]] = arrs
    rep_names = ["norm_g", "mod_b", "final_norm_g", "sg_b_spatial", "swa_sinks", "rwkv_r_k"]
    rep_grads = [g_norm_g, g_mod_b.reshape(mod_b.shape), g_final, g_b_sp, g_sinks, g_r_k]
    rep_shapes2 = [weights[n].shape for n in rep_names]
    w_pk, m_pk, v_pk = (_pack([a[n] for n in rep_names]) for a in (weights, mom_m, mom_v))
    res = _adamw("adamw_rep", w_pk, _pack(rep_grads), m_pk, v_pk)
    for q, arrs in enumerate(zip(*[_unpack(r_.reshape(-1), rep_shapes2) for r_ in res])):
        out_g[rep_names[q]], out_d[rep_names[q]], out_m[rep_names[q]], out_v[rep_names[q]] = arrs

    return (loss_tot.reshape(()), dx[None], *[out_g[n] for n in names], *[out_d[n] for n in names],
            *[out_m[n] for n in names], *[out_v[n] for n in names])
```

```python
import functools
import math

import jax
import jax.numpy as jnp
from jax import lax
from jax.experimental import pallas as pl
from jax.experimental.pallas import tpu as pltpu

F32, BF16 = jnp.float32, jnp.bfloat16
HI = lax.Precision.HIGHEST
S = jax.ShapeDtypeStruct
MESH = pl.DeviceIdType.MESH

N_DEV = 8
DEPTH = 4
HEAD = 64
SG_GROUPS = 16
SG_CHUNK = 128
SWA_BLOCK = 128
SWA_KV = 4
SWA_REP = 8
ROPE_THETA = 10000.0
LORA = 96
LORA_PAD = 256
RW_CHUNK = 64
RW_HEADS = 16
RW_HEADS_FWD = 32
RW_PREC = lax.Precision.HIGH
DECAY_SCALE = math.exp(-0.5)
GN_EPS = 64e-5
RMS_EPS = 1e-6
LN_EPS = 1e-5
NEG = -1e30
ADAM_LR, ADAM_B1, ADAM_B2, ADAM_EPS, ADAM_WD, ADAM_STEP = 0.001, 0.9, 0.999, 1e-08, 0.01, 10
VMEM_MB = 56


def _params(sem=None):
    kw = dict(vmem_limit_bytes=VMEM_MB << 20)
    if sem is not None:
        kw["dimension_semantics"] = sem
    return pltpu.CompilerParams(**kw)


def _pick(n, opts):
    for o in opts:
        if n % o == 0:
            return o
    raise ValueError(f"no tile for {n}")


def _rows(name, fn, rows, consts, out_rows, out_accs, tm, jobs=()):
    t = rows[0].shape[0]
    nr, nc, no = len(rows), len(consts), len(out_rows)

    def body(*refs):
        outs = fn(*[r[...] for r in refs[:nr + nc]])
        if not isinstance(outs, (tuple, list)):
            outs = (outs,)
        for r, o in zip(refs[nr + nc:nr + nc + no], outs[:no]):
            r[...] = o.astype(r.dtype)
        i = pl.program_id(0)
        for r, o in zip(refs[nr + nc + no:], outs[no:]):
            @pl.when(i == 0)
            def _(r=r, o=o):
                r[...] = o.astype(r.dtype)

            @pl.when(i > 0)
            def _(r=r, o=o):
                r[...] += o.astype(r.dtype)

    in_specs = [pl.BlockSpec((tm, a.shape[1]), lambda i: (i, 0)) for a in rows]
    in_specs += [pl.BlockSpec(c.shape, lambda i, nd=c.ndim: (0,) * nd) for c in consts]
    out_specs = [pl.BlockSpec((tm, n), lambda i: (i, 0)) for n, _ in out_rows]
    out_specs += [pl.BlockSpec(s, lambda i, nd=len(s): (0,) * nd) for s in out_accs]
    out_shape = [S((t, n), dt) for n, dt in out_rows] + [S(s, F32) for s in out_accs]
    return _pcall(body, grid=(t // tm,), in_specs=in_specs, out_specs=out_specs, out_shape=out_shape, name=name,
                  semantics=("arbitrary",), inputs=(*rows, *consts), jobs=jobs)


_DN = {"nn": (((1,), (0,)), ((), ())), "nt": (((1,), (1,)), ((), ())), "tn": (((0,), (0,)), ((), ()))}


def _mm(name, a, b, mode, out_dtype, add=None, resid=None, jobs=()):
    if mode == "nn":
        (m, k), (_, n) = a.shape, b.shape
    elif mode == "nt":
        (m, k), (n, _) = a.shape, b.shape
    else:
        (k, m), (_, n) = a.shape, b.shape
    wide = mode != "tn" and add is None and resid is None
    tm = _pick(m, (1024, 512, 256, 128))
    tn = _pick(n, ((1536,) if wide and n % 1024 else ()) + (1024, 512, 384, 256, 128))
    long_k = (4096,) if mode == "tn" else (3072, 2304) if mode == "nt" and add is None else ()
    tk = _pick(k, long_k + (2048, 1536, 1024, 512, 384, 256, 128))
    nk = k // tk
    n_extra = (add is not None) + 2 * (resid is not None)

    def body(*refs):
        a_ref, b_ref = refs[0], refs[1]
        extra, outs, acc = refs[2:2 + n_extra], refs[2 + n_extra:-1], refs[-1]
        kk = pl.program_id(2)
        prod = lax.dot_general(a_ref[...].astype(BF16), b_ref[...].astype(BF16), _DN[mode], preferred_element_type=F32)
        if add is not None:
            prod = jnp.where(kk == 0, prod + extra[0][...].astype(F32), prod) if nk > 1 else prod + extra[0][...].astype(F32)

        def finish(total):
            outs[0][...] = total.astype(outs[0].dtype)
            if resid is not None:
                outs[1][...] = extra[-2][...] + extra[-1][...] * total

        if nk == 1:
            finish(prod)
            return

        @pl.when(kk == 0)
        def _():
            acc[...] = prod

        @pl.when(kk > 0)
        def _():
            acc[...] += prod

        @pl.when(kk == nk - 1)
        def _():
            finish(acc[...])

    a_spec = pl.BlockSpec((tk, tm), lambda i, j, q: (q, i)) if mode == "tn" else pl.BlockSpec((tm, tk), lambda i, j, q: (i, q))
    b_spec = pl.BlockSpec((tn, tk), lambda i, j, q: (j, q)) if mode == "nt" else pl.BlockSpec((tk, tn), lambda i, j, q: (q, j))
    o_spec = pl.BlockSpec((tm, tn), lambda i, j, q: (i, j))
    ins, specs, out_specs, out_shape = [a, b], [a_spec, b_spec], [o_spec], [S((m, n), out_dtype)]
    if add is not None:
        ins.append(add)
        specs.append(o_spec)
    if resid is not None:
        ins += list(resid)
        specs += [o_spec, pl.BlockSpec((1, tn), lambda i, j, q: (0, j))]
        out_specs.append(o_spec)
        out_shape.append(S((m, n), F32))
    res = _pcall(body, grid=(m // tm, n // tn, nk), in_specs=specs, out_specs=out_specs, out_shape=out_shape,
                 scratch_shapes=[pltpu.VMEM((tm, tn), F32)], name=name, semantics=("parallel", "parallel", "arbitrary"),
                 inputs=ins, jobs=jobs)
    return res if resid is not None else res[0]


def _exchange(name, src, scatter):
    blk = src.shape[1:] if scatter else src.shape

    def body(src_ref, dst_ref, send_sems, recv_sems, loc_sem):
        x, y, c = lax.axis_index("x"), lax.axis_index("y"), lax.axis_index("c")
        me = 4 * x + 2 * y + c

        def mine(d):
            return src_ref.at[d] if scatter else src_ref

        local = pltpu.make_async_copy(mine(me), dst_ref.at[me], loc_sem)
        local.start()
        sends, peers = [], []
        for k in range(1, N_DEV):
            px = 1 - x if k & 4 else x
            py = 1 - y if k & 2 else y
            pc = 1 - c if k & 1 else c
            pid = 4 * px + 2 * py + pc
            cp = pltpu.make_async_remote_copy(src_ref=mine(pid), dst_ref=dst_ref.at[me], send_sem=send_sems.at[k - 1],
                                              recv_sem=recv_sems.at[k - 1], device_id=(px, py, pc), device_id_type=MESH)
            cp.start()
            sends.append(cp)
            peers.append((pid, (px, py, pc)))
        for k in range(1, N_DEV):
            pid, dev = peers[k - 1]
            pltpu.make_async_remote_copy(src_ref=mine(pid), dst_ref=dst_ref.at[pid], send_sem=send_sems.at[k - 1],
                                         recv_sem=recv_sems.at[k - 1], device_id=dev, device_id_type=MESH).wait_recv()
        for cp in sends:
            cp.wait_send()
        local.wait()

    return pl.pallas_call(
        body, out_shape=S((N_DEV,) + tuple(blk), src.dtype),
        in_specs=[pl.BlockSpec(memory_space=pl.ANY)], out_specs=pl.BlockSpec(memory_space=pl.ANY),
        scratch_shapes=[pltpu.SemaphoreType.DMA((N_DEV - 1,)), pltpu.SemaphoreType.DMA((N_DEV - 1,)),
                        pltpu.SemaphoreType.DMA],
        name=name)(src)


class _Gather:
    @staticmethod
    def out_shape(src):
        return S((N_DEV,) + tuple(src.shape), src.dtype)

    scratch = (pltpu.SemaphoreType.DMA((N_DEV - 1,)), pltpu.SemaphoreType.DMA((N_DEV - 1,)), pltpu.SemaphoreType.DMA)

    def __init__(self, src_ref, dst_ref, send_sems, recv_sems, loc_sem):
        self.refs = (src_ref, dst_ref, send_sems, recv_sems, loc_sem)
        x, y, c = lax.axis_index("x"), lax.axis_index("y"), lax.axis_index("c")
        self.c, self.me, self.sibling = c, (x, y, c), (x, y, 1 - c)
        self.chips = [(1 - x, y), (x, 1 - y), (1 - x, 1 - y)]

    def rows(self, px, py, pc):
        return self.refs[1].at[4 * px + 2 * py + pc]

    def copy(self, k, block, to, own=False):
        src_ref, _, send_sems, recv_sems, _ = self.refs
        return pltpu.make_async_remote_copy(src_ref=src_ref if own else self.rows(*block), dst_ref=self.rows(*block),
                                            send_sem=send_sems.at[k], recv_sem=recv_sems.at[k], device_id=to,
                                            device_id_type=MESH)

    def local(self):
        return pltpu.make_async_copy(self.refs[0], self.rows(*self.me), self.refs[4])

    def first(self):
        return [self.copy(0, self.me, self.sibling, own=True)] + [self.copy(1 + j, self.me, (*chip, self.c), own=True)
                                                                  for j, chip in enumerate(self.chips)]

    def start(self):
        self.local().start()
        for cp in self.first():
            cp.start()

    def finish(self):
        c = self.c
        passed = [self.copy(4 + j, (*chip, c), self.sibling) for j, chip in enumerate(self.chips)]
        for j, chip in enumerate(self.chips):
            self.copy(1 + j, (*chip, c), self.me).wait_recv()
            passed[j].start()
        self.copy(0, self.sibling, self.me).wait_recv()
        for j, chip in enumerate(self.chips):
            self.copy(4 + j, (*chip, 1 - c), self.me).wait_recv()
        for cp in self.first() + passed:
            cp.wait_send()
        self.local().wait()


class _Chips:
    @staticmethod
    def out_shape(src):
        return S(src.shape, src.dtype)

    scratch = (pltpu.SemaphoreType.DMA((N_DEV // 2 - 1,)), pltpu.SemaphoreType.DMA((N_DEV // 2 - 1,)), pltpu.SemaphoreType.DMA)

    def __init__(self, src_ref, dst_ref, send_sems, recv_sems, loc_sem):
        self.refs = (src_ref, dst_ref, send_sems, recv_sems, loc_sem)
        x, y, c = lax.axis_index("x"), lax.axis_index("y"), lax.axis_index("c")
        self.c, self.mine = c, 2 * x + y
        self.chips = [(1 - x, y), (x, 1 - y), (1 - x, 1 - y)]

    def local(self):
        src_ref, dst_ref, _, _, loc_sem = self.refs
        return pltpu.make_async_copy(src_ref.at[self.mine], dst_ref.at[self.mine], loc_sem)

    def send(self, j):
        src_ref, dst_ref, send_sems, recv_sems, _ = self.refs
        px, py = self.chips[j]
        return pltpu.make_async_remote_copy(src_ref=src_ref.at[2 * px + py], dst_ref=dst_ref.at[self.mine],
                                            send_sem=send_sems.at[j], recv_sem=recv_sems.at[j],
                                            device_id=(px, py, self.c), device_id_type=MESH)

    def arrival(self, j):
        src_ref, dst_ref, send_sems, recv_sems, _ = self.refs
        px, py = self.chips[j]
        return pltpu.make_async_remote_copy(src_ref=src_ref.at[self.mine], dst_ref=dst_ref.at[2 * px + py],
                                            send_sem=send_sems.at[j], recv_sem=recv_sems.at[j],
                                            device_id=(px, py, self.c), device_id_type=MESH)

    def start(self):
        self.local().start()
        for j in range(len(self.chips)):
            self.send(j).start()

    def finish(self):
        for j in range(len(self.chips)):
            self.arrival(j).wait_recv()
        for j in range(len(self.chips)):
            self.send(j).wait_send()
        self.local().wait()


def _exchange_call(name, cls, src):
    def body(*refs):
        ex = cls(*refs)
        ex.start()
        ex.finish()

    return pl.pallas_call(body, out_shape=cls.out_shape(src), in_specs=[pl.BlockSpec(memory_space=pl.ANY)],
                          out_specs=pl.BlockSpec(memory_space=pl.ANY), scratch_shapes=list(cls.scratch), name=name)(src)


def _gather(name, src):
    return _exchange_call(name, _Gather, src)


def _pcall(body, *, grid, in_specs, out_specs, out_shape, scratch_shapes=(), name, semantics, inputs, jobs=()):
    if not jobs:
        return pl.pallas_call(body, grid=grid, in_specs=in_specs, out_specs=out_specs, out_shape=out_shape,
                              scratch_shapes=list(scratch_shapes), name=name, compiler_params=_params(semantics))(*inputs)
    n_in, n_out, n_scr, nj = len(in_specs), len(out_specs), len(scratch_shapes), len(jobs)

    def hosted(*refs):
        ins, srcs = refs[:n_in], refs[n_in:n_in + nj]
        outs, dsts = refs[n_in + nj:n_in + nj + n_out], refs[n_in + nj + n_out:n_in + 2 * nj + n_out]
        scr, sems = refs[n_in + 2 * nj + n_out:n_in + 2 * nj + n_out + n_scr], refs[n_in + 2 * nj + n_out + n_scr:]
        ids = [pl.program_id(q) for q in range(len(grid))]
        first = functools.reduce(jnp.logical_and, [i == 0 for i in ids])
        last = functools.reduce(jnp.logical_and, [i == g - 1 for i, g in zip(ids, grid)])
        make = lambda q: jobs[q]["cls"](srcs[q], dsts[q], *sems[3 * q:3 * q + 3])

        @pl.when(first)
        def _():
            for q in range(nj):
                make(q).start()

        body(*ins, *outs, *scr)

        @pl.when(last)
        def _():
            for q in range(nj):
                make(q).finish()

    anyspec = pl.BlockSpec(memory_space=pl.ANY)
    res = pl.pallas_call(
        hosted, grid=grid, in_specs=list(in_specs) + [anyspec] * nj, out_specs=list(out_specs) + [anyspec] * nj,
        out_shape=list(out_shape) + [j["cls"].out_shape(j["src"]) for j in jobs],
        scratch_shapes=list(scratch_shapes) + [s for j in jobs for s in j["cls"].scratch], name=name,
        compiler_params=_params(("arbitrary",) * len(grid)))(*inputs, *[j["src"] for j in jobs])
    for j, out in zip(jobs, res[n_out:]):
        j["out"] = out
    return res[:n_out]


def _scatter_pairs(name, parts):
    _, r, c_ = parts.shape
    n_chip = N_DEV // 2

    def stage1(src_ref, dst_ref, send_sems, recv_sems):
        x, y, c = lax.axis_index("x"), lax.axis_index("y"), lax.axis_index("c")
        sends = []
        for q in range(n_chip):
            cp = pltpu.make_async_remote_copy(src_ref=src_ref.at[2 * q + 1 - c], dst_ref=dst_ref.at[q],
                                              send_sem=send_sems.at[q], recv_sem=recv_sems.at[q],
                                              device_id=(x, y, 1 - c), device_id_type=MESH)
            cp.start()
            sends.append(cp)
        for q in range(n_chip):
            pltpu.make_async_remote_copy(src_ref=src_ref.at[2 * q + c], dst_ref=dst_ref.at[q], send_sem=send_sems.at[q],
                                         recv_sem=recv_sems.at[q], device_id=(x, y, 1 - c), device_id_type=MESH).wait_recv()
        for cp in sends:
            cp.wait_send()

    from_sibling = pl.pallas_call(
        stage1, out_shape=S((n_chip, r, c_), parts.dtype),
        in_specs=[pl.BlockSpec(memory_space=pl.ANY)], out_specs=pl.BlockSpec(memory_space=pl.ANY),
        scratch_shapes=[pltpu.SemaphoreType.DMA((n_chip,)), pltpu.SemaphoreType.DMA((n_chip,))], name=name + "_pair")(parts)

    tm = _pick(r, (512, 256, 128, 64, 32, 16, 8)) if r % 8 == 0 else r
    core = lax.axis_index("c").astype(jnp.int32).reshape(1)

    def pair_sum(core_ref, mine_ref, sib_ref, o_ref):
        o_ref[...] = (mine_ref[0].astype(F32) + sib_ref[...].astype(F32)).astype(o_ref.dtype)

    pair = pl.pallas_call(
        pair_sum, out_shape=S((n_chip, r, c_), parts.dtype),
        grid_spec=pltpu.PrefetchScalarGridSpec(
            num_scalar_prefetch=1, grid=(n_chip, r // tm),
            in_specs=[pl.BlockSpec((1, 1, tm, c_), lambda q, i, core_ref: (q, core_ref[0], i, 0)),
                      pl.BlockSpec((1, tm, c_), lambda q, i, core_ref: (q, i, 0))],
            out_specs=pl.BlockSpec((1, tm, c_), lambda q, i, core_ref: (q, i, 0))),
        name=name + "_sum", compiler_params=_params(("parallel", "parallel")))(
            core, parts.reshape(n_chip, 2, r, c_), from_sibling)

    return pair


def _sum_parts(name, parts):
    n_parts, r, c = parts.shape
    tm = _pick(r, (512, 256, 128, 64, 32, 16, 8)) if r % 8 == 0 else r

    def body(p_ref, o_ref):
        acc = p_ref[0].astype(F32)
        for d in range(1, n_parts):
            acc = acc + p_ref[d].astype(F32)
        o_ref[...] = acc

    return pl.pallas_call(body, grid=(r // tm,), in_specs=[pl.BlockSpec((n_parts, tm, c), lambda i: (0, i, 0))],
                          out_specs=pl.BlockSpec((tm, c), lambda i: (i, 0)), out_shape=S((r, c), F32), name=name,
                          compiler_params=_params(("parallel",)))(parts)


def _adamw(name, w, g, m, v, jobs=()):
    r, c = w.shape
    parts = g.ndim == 3
    n_parts = g.shape[0] if parts else 1
    tile_rows = max(8, (2 << 20) // (4 * c))
    tm = _pick(r, tuple(q for q in (2048, 1024, 512, 256, 128, 64, 32, 16, 8) if q <= tile_rows)) if r % 8 == 0 else r

    def body(w_ref, g_ref, m_ref, v_ref, go_ref, d_ref, mo_ref, vo_ref):
        if parts:
            gg = g_ref[0].astype(F32)
            for d in range(1, n_parts):
                gg = gg + g_ref[d].astype(F32)
        else:
            gg = g_ref[...]
        mm = ADAM_B1 * m_ref[...] + (1.0 - ADAM_B1) * gg
        vv = ADAM_B2 * v_ref[...] + (1.0 - ADAM_B2) * jnp.square(gg)
        m_hat = mm / (1.0 - ADAM_B1 ** ADAM_STEP)
        v_hat = vv / (1.0 - ADAM_B2 ** ADAM_STEP)
        go_ref[...] = gg
        d_ref[...] = -ADAM_LR * (m_hat / (jnp.sqrt(v_hat) + ADAM_EPS) + ADAM_WD * w_ref[...])
        mo_ref[...] = mm
        vo_ref[...] = vv

    spec = pl.BlockSpec((tm, c), lambda i: (i, 0))
    g_spec = pl.BlockSpec((n_parts, tm, c), lambda i: (0, i, 0)) if parts else spec
    return _pcall(body, grid=(r // tm,), in_specs=[spec, g_spec, spec, spec], out_specs=[spec] * 4,
                  out_shape=[S((r, c), F32)] * 4, name=name, semantics=("parallel",), inputs=(w, g, m, v), jobs=jobs)


def _rms(x, g):
    return x * lax.rsqrt(jnp.mean(x * x, axis=-1, keepdims=True) + RMS_EPS) * g


def _adaln(x, g, shift, scale):
    return _rms(x, g) * (1.0 + scale) + shift


def _dot(a, b, dn="nn", hi=False, prec=None):
    if hi or prec is not None:
        return lax.dot_general(a, b, _DN[dn], precision=HI if hi else prec, preferred_element_type=F32)
    return lax.dot_general(a.astype(BF16), b.astype(BF16), _DN[dn], preferred_element_type=F32)


def _sg_mix(p, ln_g, ln_b, w_s, b_st):
    d = p.shape[1] // 3
    gd = d // SG_GROUPS
    u = jax.nn.gelu(p[:, :d])
    vf = jax.nn.gelu(p[:, d:2 * d])
    z = p[:, 2 * d:]
    mean = jnp.mean(vf, axis=-1, keepdims=True)
    var = jnp.mean(jnp.square(vf - mean), axis=-1, keepdims=True)
    vn = (vf - mean) * lax.rsqrt(var + LN_EPS) * ln_g + ln_b
    row = lax.broadcasted_iota(jnp.int32, (SG_CHUNK, SG_CHUNK), 0)
    col = lax.broadcasted_iota(jnp.int32, (SG_CHUNK, SG_CHUNK), 1)
    fs = []
    for g in range(SG_GROUPS):
        w = jnp.where(row >= col, w_s[g], 0.0)
        fs.append(_dot(w, vn[:, g * gd:(g + 1) * gd]))
    sel = (lax.broadcasted_iota(jnp.int32, (SG_GROUPS, d), 1) // gd
           == lax.broadcasted_iota(jnp.int32, (SG_GROUPS, d), 0)).astype(F32)
    f = jnp.concatenate(fs, axis=1) + _dot(b_st, sel, hi=True)
    return u * f * jax.nn.silu(z)


def _rot_half(x):
    n = x.shape[1]
    lane = lax.broadcasted_iota(jnp.int32, x.shape, 1)
    return jnp.where(lane % HEAD < HEAD // 2, -pltpu.roll(x, n - HEAD // 2, 1), pltpu.roll(x, HEAD // 2, 1))


def _rope(x, cos, sin, sign):
    reps = x.shape[1] // cos.shape[1]
    return x * jnp.tile(cos, (1, reps)) + sign * _rot_half(x) * jnp.tile(sin, (1, reps))


def _attn_block(q, kp, kc, vp, vc, sink, prev_bias):
    each = lambda f, *ls: [f(*xs) for xs in zip(*ls)]
    r = sink[0].shape[0]
    cur, prob, _ = _attn_probs(q, kp, kc, sink, prev_bias)
    flat = lambda x: x.reshape(r * SWA_BLOCK, SWA_BLOCK)
    pc, pp = each(lambda p: flat(jnp.where(cur, p, 0.0)), prob), each(lambda p: flat(jnp.where(cur, 0.0, p)), prob)
    return each(lambda a, va, b, vb: _dot(a, va) + _dot(b, vb), pp, vp, pc, vc)


def _attn_probs(q, kp, kc, sink, prev_bias):
    each = lambda f, *ls: [f(*xs) for xs in zip(*ls)]
    r = sink[0].shape[0]
    scores = lambda a, b: (_dot(a, b, "nt") * (HEAD ** -0.5)).reshape(r, SWA_BLOCK, SWA_BLOCK)
    sp, sc = each(scores, q, kp), each(scores, q, kc)
    cur = (lax.broadcasted_iota(jnp.int32, (r, SWA_BLOCK, SWA_BLOCK), 2)
           <= lax.broadcasted_iota(jnp.int32, (r, SWA_BLOCK, SWA_BLOCK), 1))
    s = each(lambda a, b: jnp.where(cur, b, a + prev_bias), sp, sc)
    m = each(lambda a, sk: jnp.maximum(jnp.max(a, axis=-1, keepdims=True), sk), s, sink)
    e, es = each(lambda a, m_: jnp.exp(a - m_), s, m), each(lambda sk, m_: jnp.exp(sk - m_), sink, m)
    denom = each(lambda a, b: jnp.sum(a, axis=-1, keepdims=True) + b, e, es)
    return cur, each(lambda a, dn: a / dn, e, denom), each(lambda a, dn: a / dn, es, denom)


def _attn_block_bwd(q, kp, kc, vp, vc, sink, do, prev_bias):
    each = lambda f, *ls: [f(*xs) for xs in zip(*ls)]
    r = sink[0].shape[0]
    scale = HEAD ** -0.5
    cube = lambda x: x.reshape(r, SWA_BLOCK, SWA_BLOCK)
    flat = lambda x: x.reshape(r * SWA_BLOCK, SWA_BLOCK)
    cur, prob, p_sink = _attn_probs(q, kp, kc, sink, prev_bias)
    pc, pp = each(lambda p: flat(jnp.where(cur, p, 0.0)), prob), each(lambda p: flat(jnp.where(cur, 0.0, p)), prob)
    dprob = each(lambda g, va, vb: jnp.where(cur, cube(_dot(g, vb, "nt")), cube(_dot(g, va, "nt"))), do, vp, vc)
    dvp, dvc = each(lambda p, g: _dot(p, g, "tn"), pp, do), each(lambda p, g: _dot(p, g, "tn"), pc, do)
    delta = each(lambda p, dp: jnp.sum(p * dp, axis=-1, keepdims=True), prob, dprob)
    ds = each(lambda p, dp, dl: p * (dp - dl), prob, dprob, delta)
    dsc, dsp = each(lambda x: flat(jnp.where(cur, x, 0.0)), ds), each(lambda x: flat(jnp.where(cur, 0.0, x)), ds)
    dsink = each(lambda ps, dl: -jnp.sum(ps * dl, axis=1, keepdims=True), p_sink, delta)
    dq = each(lambda a, ka, b, kb: (_dot(a, ka) + _dot(b, kb)) * scale, dsp, kp, dsc, kc)
    dkp, dkc = each(lambda a, q_: _dot(a, q_, "tn") * scale, dsp, q), each(lambda a, q_: _dot(a, q_, "tn") * scale, dsc, q)
    return dq, dkp, dkc, dvp, dvc, dsink


def _rwkv_chunk(s0, r, k, v, logw, a, k_k, k_a, r_k, gn_g, gn_b):
    c = r[0].shape[0]
    each = lambda f, *ls: [f(*xs) for xs in zip(*ls)]
    gram = functools.partial(_dot, prec=RW_PREC)
    row = lax.broadcasted_iota(jnp.int32, (c, c), 0)
    col = lax.broadcasted_iota(jnp.int32, (c, c), 1)
    incl, strict = row >= col, row > col
    ones_l = incl.astype(F32)

    def unit(x):
        return x / jnp.maximum(jnp.sqrt(jnp.sum(x * x, axis=-1, keepdims=True)), 1e-12)

    kk = each(lambda k_, p: unit(k_ * p), k, k_k)
    km = each(lambda k_, a_, p: k_ * (1.0 + (a_ - 1.0) * p), k, a, k_a)
    b = each(lambda x, a_: x * a_, kk, a)
    first_half = lax.broadcasted_iota(jnp.int32, (c, HEAD), 0) < c // 2
    mid = each(lambda w: jnp.sum(jnp.where(first_half, w, 0.0), axis=0, keepdims=True), logw)
    cum = each(lambda w, m: _dot(ones_l, w, hi=True) - m, logw, mid)
    alpha = each(lambda x, cu, w: x * jnp.exp(cu - w), kk, cum, logw)
    beta = each(lambda x, cu: x * jnp.exp(-cu), b, cum)
    kap = each(lambda x, cu: x * jnp.exp(-cu), km, cum)
    rho = each(lambda x, cu: x * jnp.exp(cu), r, cum)
    s0 = each(lambda s, m: s * jnp.exp(m), s0, mid)
    lab = each(lambda x, y_: jnp.where(strict, gram(x, y_, "nt"), 0.0), alpha, beta)
    lak = each(lambda x, y_: jnp.where(strict, gram(x, y_, "nt"), 0.0), alpha, kap)
    xs = each(lambda al, s, l, v_: _dot(al, s, "nt") + _dot(l, v_), alpha, s0, lak, v)
    xs = each(lambda x, l: x - _dot(l, x), xs, lab)
    lp, power = lab, 2
    while power < c:
        lp = each(lambda l: _dot(l, l), lp)
        xs = each(lambda x, l: x + _dot(l, x), xs, lp)
        power *= 2
    u = each(lambda x: -x, xs)
    mrb = each(lambda x, y_: jnp.where(incl, gram(x, y_, "nt"), 0.0), rho, beta)
    mrk = each(lambda x, y_: jnp.where(incl, gram(x, y_, "nt"), 0.0), rho, kap)
    y = each(lambda rh, s, mb, u_, mk, v_: _dot(rh, s, "nt") + _dot(mb, u_) + _dot(mk, v_), rho, s0, mrb, u, mrk, v)
    s1 = each(lambda s, u_, be, v_, ka, w, m: (s + _dot(u_, be, "tn") + _dot(v_, ka, "tn"))
              * jnp.exp(jnp.sum(w, axis=0, keepdims=True) - m), s0, u, beta, v, kap, logw, mid)

    def finish(y_, g, bias, r_, km_, rk, v_):
        mean = jnp.mean(y_, axis=-1, keepdims=True)
        var = jnp.mean(jnp.square(y_ - mean), axis=-1, keepdims=True)
        y_ = (y_ - mean) * lax.rsqrt(var + GN_EPS) * g + bias
        return y_ + jnp.sum(r_ * km_ * rk, axis=-1, keepdims=True) * v_

    return each(finish, y, gn_g, gn_b, r, km, r_k, v), s1


def _norm_fwd(name, x, g, shift, scale):
    return _rows(name, lambda x_, g_, sh, sc: _adaln(x_, g_, sh, sc), [x], [g, shift, scale], [(x.shape[1], BF16)], [], 256)[0]


def _norm_bwd(name, x, dh, dx_res, g, shift, scale):
    d = x.shape[1]

    def fn(x_, dh_, dr_, g_, sh, sc):
        _, vjp = jax.vjp(_adaln, x_, g_, sh, sc)
        dx, dg, dsh, dsc = vjp(dh_)
        return dx + dr_, dg, dsh, dsc

    return _rows(name, fn, [x, dh, dx_res], [g, shift, scale], [(d, F32)], [(1, d)] * 3, 256)


def _resid_bwd(name, dx, y, gate):
    d = dx.shape[1]
    return _rows(name, lambda dx_, y_, g_: (g_ * dx_, jnp.sum(dx_ * y_, axis=0, keepdims=True)), [dx, y], [gate],
                 [(d, BF16)], [(1, d)], 256)


def _sg_fwd(name, p, ln_g, ln_b, w_s, b_st):
    d = p.shape[1] // 3
    return _rows(name, _sg_mix, [p], [ln_g, ln_b, w_s, b_st], [(d, BF16)], [], SG_CHUNK)[0]


def _sg_bwd(name, p, dmix, ln_g, ln_b, w_s, b_st, jobs=()):
    def fn(p_, dm_, lg, lb, ws, bs):
        _, vjp = jax.vjp(_sg_mix, p_, lg, lb, ws, bs)
        return vjp(dm_)

    return _rows(name, fn, [p, dmix], [ln_g, ln_b, w_s, b_st], [(p.shape[1], BF16)],
                 [ln_g.shape, ln_b.shape, w_s.shape, b_st.shape], SG_CHUNK, jobs=jobs)


def _rope_tables(pos, inv_freq):
    ang = pos * inv_freq
    return jnp.cos(ang), jnp.sin(ang)


def _swa_pre(name, p, pos, inv_freq, d):
    kvw = SWA_KV * HEAD

    def fn(p_, pos_, fr):
        cos, sin = _rope_tables(pos_, fr)
        return (_rope(p_[:, :d], cos, sin, 1.0), _rope(p_[:, d:d + kvw], cos, sin, 1.0), p_[:, d + kvw:d + 2 * kvw])

    return _rows(name, fn, [p, pos], [inv_freq], [(d, BF16), (kvw, BF16), (kvw, BF16)], [], 256)


def _q_groups(ref, kv, rep):
    heads = _head_cols(ref, kv * rep)
    return [jnp.concatenate(heads[g * rep:(g + 1) * rep], axis=0) for g in range(kv)]


def _q_ungroup(groups, rep):
    return jnp.concatenate([g[h * SWA_BLOCK:(h + 1) * SWA_BLOCK] for g in groups for h in range(rep)], axis=1)


def _swa_attn_fwd(name, q, k, v, sinks, jobs=()):
    t, d = q.shape
    kv, rep = sinks.shape[0], sinks.shape[1]
    nb = t // SWA_BLOCK

    def body(q_ref, kp_ref, kc_ref, vp_ref, vc_ref, s_ref, o_ref):
        prev_bias = jnp.where(pl.program_id(0) > 0, 0.0, NEG).astype(F32)
        o = _attn_block(_q_groups(q_ref, kv, rep), *[_head_cols(ref, kv) for ref in (kp_ref, kc_ref, vp_ref, vc_ref)],
                        [s_ref[g] for g in range(kv)], prev_bias)
        o_ref[...] = _q_ungroup(o, rep)

    qs = pl.BlockSpec((SWA_BLOCK, d), lambda n: (n, 0))
    cur = pl.BlockSpec((SWA_BLOCK, kv * HEAD), lambda n: (n, 0))
    prev = pl.BlockSpec((SWA_BLOCK, kv * HEAD), lambda n: (jnp.maximum(n - 1, 0), 0))
    ss = pl.BlockSpec(sinks.shape, lambda n: (0, 0, 0, 0))
    return _pcall(body, grid=(nb,), in_specs=[qs, prev, cur, prev, cur, ss], out_specs=[qs], out_shape=[S((t, d), F32)],
                  name=name, semantics=("parallel",), inputs=(q, k, k, v, v, sinks), jobs=jobs)[0]


def _swa_attn_bwd(name, q, k, v, sinks, do, jobs=()):
    t, d = q.shape
    kv, rep = sinks.shape[0], sinks.shape[1]
    nb = t // SWA_BLOCK

    def body(q_ref, kp_ref, kc_ref, vp_ref, vc_ref, s_ref, do_ref, dq_ref, dkc_ref, dkp_ref, dvc_ref, dvp_ref, ds_ref):
        n = pl.program_id(0)
        prev_bias = jnp.where(n > 0, 0.0, NEG).astype(F32)
        args = [_q_groups(q_ref, kv, rep)] + [_head_cols(ref, kv) for ref in (kp_ref, kc_ref, vp_ref, vc_ref)]
        dq, dkp, dkc, dvp, dvc, ds = _attn_block_bwd(*args, [s_ref[g] for g in range(kv)], _q_groups(do_ref, kv, rep), prev_bias)
        dq_ref[...] = _q_ungroup(dq, rep)
        for ref, val in ((dkc_ref, dkc), (dkp_ref, dkp), (dvc_ref, dvc), (dvp_ref, dvp)):
            ref[...] = jnp.concatenate(val, axis=1)

        @pl.when(n == 0)
        def _():
            ds_ref[...] = jnp.zeros_like(ds_ref)

        for g in range(kv):
            ds_ref[g] += ds[g]

    qs = pl.BlockSpec((SWA_BLOCK, d), lambda n: (n, 0))
    cur = pl.BlockSpec((SWA_BLOCK, kv * HEAD), lambda n: (n, 0))
    prev = pl.BlockSpec((SWA_BLOCK, kv * HEAD), lambda n: (jnp.maximum(n - 1, 0), 0))
    ss = pl.BlockSpec(sinks.shape, lambda n: (0, 0, 0, 0))
    return _pcall(body, grid=(nb,), in_specs=[qs, prev, cur, prev, cur, ss, qs], out_specs=[qs, cur, cur, cur, cur, ss],
                  out_shape=[S((t, d), F32)] + [S((t, kv * HEAD), F32)] * 4 + [S(sinks.shape, F32)], name=name,
                  semantics=("arbitrary",), inputs=(q, k, k, v, v, sinks, do), jobs=jobs)


def _gate_fwd(name, o, z_src, z_off, d):
    return _rows(name, lambda o_, p_: o_ * jax.nn.silu(p_[:, z_off:z_off + d]), [o, z_src], [], [(d, BF16)], [], 256)[0]


def _gate_bwd(name, o, z_src, z_off, d, dmix):
    def fn(o_, p_, dm_):
        _, vjp = jax.vjp(lambda oo, zz: oo * jax.nn.silu(zz), o_, p_[:, z_off:z_off + d])
        return vjp(dm_)

    return _rows(name, fn, [o, z_src, dmix], [], [(d, F32), (d, F32)], [], 256)


def _swa_post_bwd(name, dq, dkc, dkp_up, dvc, dvp_up, dz, pos, inv_freq):
    def fn(dq_, dkc_, dkp_, dvc_, dvp_, dz_, pos_, fr):
        cos, sin = _rope_tables(pos_, fr)
        return jnp.concatenate([_rope(dq_, cos, sin, -1.0), _rope(dkc_ + dkp_, cos, sin, -1.0), dvc_ + dvp_, dz_], axis=1)

    n = dq.shape[1] + dkc.shape[1] + dvc.shape[1] + dz.shape[1]
    return _rows(name, fn, [dq, dkc, dkp_up, dvc, dvp_up, dz, pos], [inv_freq], [(n, BF16)], [], 256)[0]


HALO = 8


def _row_before(x, halo_ref, i):
    first = jnp.where(i > 0, halo_ref[pl.ds(HALO - 1, 1), :], 0.0)
    row = lax.broadcasted_iota(jnp.int32, x.shape, 0)
    return jnp.where(row == 0, first, pltpu.roll(x, 1, 0))


def _row_after(x, halo, i, n_tiles):
    last = jnp.where(i < n_tiles - 1, halo, 0.0)
    row = lax.broadcasted_iota(jnp.int32, x.shape, 0)
    return jnp.where(row == x.shape[0] - 1, last, pltpu.roll(x, x.shape[0] - 1, 0))


def _lerp_fwd(name, p, mu, widths):
    t, n = p.shape
    tm = 128

    def body(p_ref, halo_ref, mu_ref, *o_refs):
        x = p_ref[...]
        pm = x + (_row_before(x, halo_ref, pl.program_id(0)) - x) * mu_ref[...]
        o = 0
        for ref, w in zip(o_refs, widths):
            ref[...] = pm[:, o:o + w]
            o += w

    return pl.pallas_call(
        body, grid=(t // tm,),
        in_specs=[pl.BlockSpec((tm, n), lambda i: (i, 0)),
                  pl.BlockSpec((HALO, n), lambda i: (jnp.maximum(i * (tm // HALO) - 1, 0), 0)),
                  pl.BlockSpec((1, n), lambda i: (0, 0))],
        out_specs=[pl.BlockSpec((tm, w), lambda i: (i, 0)) for w in widths],
        out_shape=[S((t, w), F32) for w in widths], name=name, compiler_params=_params(("parallel",)))(p, p, mu)


def _lerp_bwd(name, dpm_parts, p, mu):
    t, n = p.shape
    k = len(dpm_parts)
    tm = 64
    n_tiles = t // tm

    def body(*refs):
        d_refs, dh_refs = refs[:k], refs[k:2 * k]
        p_ref, ph_ref, mu_ref, dp_ref, dmu_ref = refs[2 * k:]
        i = pl.program_id(0)
        cat = lambda vals: jnp.concatenate(vals, axis=1) if k > 1 else vals[0]
        dpm = cat([r[...] for r in d_refs])
        dnext = cat([r[pl.ds(0, 1), :] for r in dh_refs])
        x, mu_ = p_ref[...], mu_ref[...]
        dp_ref[...] = (dpm * (1.0 - mu_) + _row_after(dpm, dnext, i, n_tiles) * mu_).astype(dp_ref.dtype)
        dmu = jnp.sum(dpm * (_row_before(x, ph_ref, i) - x), axis=0, keepdims=True)

        @pl.when(i == 0)
        def _():
            dmu_ref[...] = dmu

        @pl.when(i > 0)
        def _():
            dmu_ref[...] += dmu

    per = tm // HALO
    d_specs = [pl.BlockSpec((tm, a.shape[1]), lambda i: (i, 0)) for a in dpm_parts]
    dh_specs = [pl.BlockSpec((HALO, a.shape[1]), lambda i: (jnp.minimum((i + 1) * per, t // HALO - 1), 0)) for a in dpm_parts]
    return pl.pallas_call(
        body, grid=(n_tiles,),
        in_specs=d_specs + dh_specs + [pl.BlockSpec((tm, n), lambda i: (i, 0)),
                                       pl.BlockSpec((HALO, n), lambda i: (jnp.maximum(i * per - 1, 0), 0)),
                                       pl.BlockSpec((1, n), lambda i: (0, 0))],
        out_specs=[pl.BlockSpec((tm, n), lambda i: (i, 0)), pl.BlockSpec((1, n), lambda i: (0, 0))],
        out_shape=[S((t, n), BF16), S((1, n), F32)], name=name,
        compiler_params=_params(("arbitrary",)))(*dpm_parts, *dpm_parts, p, p, mu)


def _lora_act(pl_, w0, w_lora, a0, a_lora):
    logw = -DECAY_SCALE * jax.nn.sigmoid(w0 + _dot(jnp.tanh(pl_), w_lora))
    a = jax.nn.sigmoid(a0 + _dot(pl_, a_lora))
    return logw, a


def _lora_fwd(name, pl_, w0, w_lora, a0, a_lora):
    d = w0.shape[1]
    return _rows(name, _lora_act, [pl_], [w0, w_lora, a0, a_lora], [(d, F32), (d, F32)], [], 256)


def _lora_bwd(name, pl_, dlogw, da, w0, w_lora, a0, a_lora):
    def fn(p_, dl_, da_, w0_, wl_, a0_, al_):
        _, vjp = jax.vjp(_lora_act, p_, w0_, wl_, a0_, al_)
        return vjp((dl_, da_))

    return _rows(name, fn, [pl_, dlogw, da], [w0, w_lora, a0, a_lora], [(pl_.shape[1], F32)],
                 [w0.shape, w_lora.shape, a0.shape, a_lora.shape], 256)


def _head_cols(ref, hb):
    x = ref[...].astype(F32)
    xo = pltpu.roll(x, x.shape[1] - HEAD, 1)
    return [(x if j % 2 == 0 else xo)[:, 2 * HEAD * (j // 2):2 * HEAD * (j // 2) + HEAD] for j in range(hb)]


def _rwkv_scan_fwd(name, r, k, v, logw, a, hp, jobs=()):
    t, d = r.shape
    h, nc, hb = d // HEAD, t // RW_CHUNK, RW_HEADS_FWD

    def body(r_ref, k_ref, v_ref, w_ref, a_ref, kk_ref, ka_ref, rk_ref, gg_ref, gb_ref, y_ref, st_ref, s_scr):
        @pl.when(pl.program_id(1) == 0)
        def _():
            s_scr[...] = jnp.zeros_like(s_scr)

        s0 = [s_scr[j] for j in range(hb)]
        for j in range(hb):
            st_ref[j, 0] = s0[j]
        y, s1 = _rwkv_chunk(s0, *[_head_cols(ref, hb) for ref in (r_ref, k_ref, v_ref, w_ref, a_ref, kk_ref, ka_ref, rk_ref,
                                                                 gg_ref, gb_ref)])
        y_ref[...] = jnp.concatenate(y, axis=1)
        for j in range(hb):
            s_scr[j] = s1[j]

    seq = pl.BlockSpec((RW_CHUNK, hb * HEAD), lambda i, n: (n, i))
    par = pl.BlockSpec((1, hb * HEAD), lambda i, n: (0, i))
    st = pl.BlockSpec((hb, 1, HEAD, HEAD), lambda i, n: (i, n, 0, 0))
    return _pcall(body, grid=(h // hb, nc), in_specs=[seq] * 5 + [par] * 5, out_specs=[seq, st],
                  out_shape=[S((t, d), F32), S((h, nc, HEAD, HEAD), F32)], scratch_shapes=[pltpu.VMEM((hb, HEAD, HEAD), F32)],
                  name=name, semantics=("parallel", "arbitrary"), inputs=(r, k, v, logw, a, *hp), jobs=jobs)


def _rwkv_scan_bwd(name, r, k, v, logw, a, hp, states, dy, jobs=()):
    t, d = r.shape
    h, nc, hb = d // HEAD, t // RW_CHUNK, RW_HEADS

    def body(r_ref, k_ref, v_ref, w_ref, a_ref, kk_ref, ka_ref, rk_ref, gg_ref, gb_ref, st_ref, dy_ref,
             dr_ref, dk_ref, dv_ref, dw_ref, da_ref, dkk_ref, dka_ref, drk_ref, dgg_ref, dgb_ref, ds_scr):
        n = pl.program_id(1)

        @pl.when(n == 0)
        def _():
            ds_scr[...] = jnp.zeros_like(ds_scr)
            for ref in (dkk_ref, dka_ref, drk_ref, dgg_ref, dgb_ref):
                ref[...] = jnp.zeros_like(ref)

        ins = [[st_ref[j, 0] for j in range(hb)]] + [_head_cols(ref, hb) for ref in (r_ref, k_ref, v_ref, w_ref, a_ref, kk_ref,
                                                                                  ka_ref, rk_ref, gg_ref, gb_ref)]
        _, vjp = jax.vjp(_rwkv_chunk, *ins)
        ds0, *dseq, dkk, dka, drk, dgg, dgb = vjp((_head_cols(dy_ref, hb), [ds_scr[j] for j in range(hb)]))
        for j in range(hb):
            ds_scr[j] = ds0[j]
        for ref, val in zip((dr_ref, dk_ref, dv_ref, dw_ref, da_ref), dseq):
            ref[...] = jnp.concatenate(val, axis=1)
        for ref, val in ((dkk_ref, dkk), (dka_ref, dka), (drk_ref, drk), (dgg_ref, dgg), (dgb_ref, dgb)):
            ref[...] += jnp.concatenate(val, axis=1)

    seq = pl.BlockSpec((RW_CHUNK, hb * HEAD), lambda i, n: (nc - 1 - n, i))
    par = pl.BlockSpec((1, hb * HEAD), lambda i, n: (0, i))
    st = pl.BlockSpec((hb, 1, HEAD, HEAD), lambda i, n: (i, nc - 1 - n, 0, 0))
    return _pcall(body, grid=(h // hb, nc), in_specs=[seq] * 5 + [par] * 5 + [st, seq], out_specs=[seq] * 5 + [par] * 5,
                  out_shape=[S((t, d), F32)] * 5 + [S((1, d), F32)] * 5, scratch_shapes=[pltpu.VMEM((hb, HEAD, HEAD), F32)],
                  name=name, semantics=("parallel", "arbitrary"), inputs=(r, k, v, logw, a, *hp, states, dy), jobs=jobs)


def _loss_head(name, x, target, g):
    d = x.shape[1]

    def fn(x_, t_, g_):
        def f(xx, gg):
            err = _rms(xx, gg) - t_
            return 0.5 * jnp.sum(jnp.mean(err * err, axis=-1, keepdims=True), axis=0, keepdims=True)

        l, vjp = jax.vjp(f, x_, g_)
        dx, dg = vjp(jnp.ones((1, 1), F32))
        return dx, dg, jnp.broadcast_to(l, (1, 128))

    return _rows(name, fn, [x, target], [g], [(d, F32)], [(1, d), (1, 128)], 256)


def _mod_fwd(name, cond_all, mod_w, mod_b_cols):
    l, d, n = mod_w.shape

    def body(c_ref, w_ref, b_ref, o_ref):
        o_ref[0] = _dot(jax.nn.silu(c_ref[...]), w_ref[0], hi=True) + b_ref[0]

    return pl.pallas_call(body, grid=(l,), in_specs=[pl.BlockSpec((N_DEV, d), lambda i: (0, 0)),
                                                      pl.BlockSpec((1, d, n), lambda i: (i, 0, 0)),
                                                      pl.BlockSpec((1, 1, n), lambda i: (i, 0, 0))],
                          out_specs=pl.BlockSpec((1, N_DEV, n), lambda i: (i, 0, 0)), out_shape=S((l, N_DEV, n), F32),
                          name=name, compiler_params=_params(("parallel",)))(cond_all, mod_w, mod_b_cols)


def _mod_bwd(name, cond_all, dmod_cols, dmod_all):
    l, _, n = dmod_cols.shape
    d = cond_all.shape[1]
    nb = dmod_all.shape[2]

    def body(c_ref, dc_ref, da_ref, gw_ref, gb_ref):
        gw_ref[0] = _dot(jax.nn.silu(c_ref[...]), dc_ref[0], "tn", hi=True)
        acc = da_ref[0, 0:1, :]
        for bi in range(1, N_DEV):
            acc = acc + da_ref[0, bi:bi + 1, :]
        gb_ref[0] = acc

    return pl.pallas_call(body, grid=(l,), in_specs=[pl.BlockSpec((N_DEV, d), lambda i: (0, 0)),
                                                      pl.BlockSpec((1, N_DEV, n), lambda i: (i, 0, 0)),
                                                      pl.BlockSpec((1, N_DEV, nb), lambda i: (i, 0, 0))],
                          out_specs=[pl.BlockSpec((1, d, n), lambda i: (i, 0, 0)), pl.BlockSpec((1, 1, nb), lambda i: (i, 0, 0))],
                          out_shape=[S((l, d, n), F32), S((l, 1, nb), F32)], name=name,
                          compiler_params=_params(("parallel",)))(cond_all, dmod_cols, dmod_all)


def _shift_up(a, n=1):
    return jnp.concatenate([a[n:], jnp.zeros_like(a[:n])], axis=0)


def _cols_full(g):
    return g.transpose(1, 0, 2).reshape(g.shape[1], -1)


def _cols_parts(full):
    r, n = full.shape
    return full.reshape(r, N_DEV, n // N_DEV).transpose(1, 0, 2)


def _pack(arrs, mult=1024):
    flat = jnp.concatenate([a.reshape(-1) for a in arrs])
    pad = (-flat.shape[0]) % mult
    return jnp.pad(flat, (0, pad)).reshape(-1, 128)


def _unpack(flat, shapes):
    out, o = [], 0
    for s in shapes:
        n = math.prod(s)
        out.append(flat[o:o + n].reshape(s))
        o += n
    return out


def _local_step(x, pos, target, mods, norm_g, final_norm_g, layer_weights, hooks=None, on_grads=None):
    t, d = x.shape
    hooks = hooks or {}
    jobs = lambda nm: hooks.get(nm, ())
    notify = on_grads or (lambda *a: None)
    kinds = [i % 3 for i in range(DEPTH)]
    inv_freq = (ROPE_THETA ** (-jnp.arange(HEAD // 2, dtype=F32) / (HEAD // 2)))
    inv_freq = jnp.tile(inv_freq, 128 // (HEAD // 2)).reshape(1, 128)
    saved = []
    for i, kind in enumerate(kinds):
        lw = layer_weights(i)
        shift, scale, gate = (mods[i, q * d:(q + 1) * d].reshape(1, d) for q in range(3))
        g = norm_g[i].reshape(1, d)
        h = _norm_fwd(f"norm_fwd{i}", x, g, shift, scale)
        sv = dict(x=x, h=h, g=g, shift=shift, scale=scale, gate=gate, lw=lw)
        if kind == 0:
            p = _mm(f"sg_in{i}", h, lw["w_in"], "nn", F32, jobs=jobs(f"sg_in{i}"))
            mix = _sg_fwd(f"sg_mix{i}", p, lw["ln_g"], lw["ln_b"], lw["w_s"], lw["b_st"])
            sv.update(p=p)
        elif kind == 1:
            p = _mm(f"swa_in{i}", h, lw["w_in"], "nn", F32, jobs=jobs(f"swa_in{i}"))
            q, k, v = _swa_pre(f"swa_pre{i}", p, pos, inv_freq, d)
            o = _swa_attn_fwd(f"swa_attn{i}", q, k, v, lw["sinks"], jobs=jobs(f"swa_attn{i}"))
            mix = _gate_fwd(f"swa_gate{i}", o, p, d + 2 * SWA_KV * HEAD, d)
            sv.update(p=p, qkv=(q, k, v), o=o)
        else:
            pm = _mm(f"rw_in{i}", h, lw["w_main"], "nn", F32, jobs=jobs(f"rw_in{i}"))
            plo = _mm(f"rw_inl{i}", h, lw["w_lorain"], "nn", F32)
            r, k, v, z = _lerp_fwd(f"rw_lerp{i}", pm, lw["mu_main"], [d] * 4)
            (pll,) = _lerp_fwd(f"rw_lerpl{i}", plo, lw["mu_lora"], [LORA_PAD])
            logw, a = _lora_fwd(f"rw_lora{i}", pll, lw["w0"], lw["w_lora"], lw["a0"], lw["a_lora"])
            seqs = (r, k, v, logw, a)
            o, states = _rwkv_scan_fwd(f"rw_scan{i}", *seqs, lw["hp"], jobs=jobs(f"rw_scan{i}"))
            mix = _gate_fwd(f"rw_gate{i}", o, z, 0, d)
            sv.update(pm=pm, plo=plo, pll=pll, z=z, seqs=seqs, states=states, o=o)
        y, x = _mm(f"out{i}", mix, lw["w_out"], "nn", F32, resid=(x, gate), jobs=jobs(f"out{i}"))
        sv.update(mix=mix, y=y)
        saved.append(sv)

    dx, d_final_g, loss = _loss_head("loss_head", x, target, final_norm_g.reshape(1, d))

    grads = dict(norm_g=[None] * DEPTH, sg_w_in=[None] * 2, sg_w_out=[None] * 2, sg_ln_g=[None] * 2, sg_ln_b=[None] * 2,
                 sg_w_s=[None] * 2, sg_b_st=[None] * 2, final_norm_g=d_final_g)
    dmods = [None] * DEPTH
    for i in reversed(range(DEPTH)):
        kind, j, sv = kinds[i], i // 3, saved[i]
        lw = sv["lw"]
        dy, dgate = _resid_bwd(f"resid_bwd{i}", dx, sv["y"], sv["gate"])
        d_w_out = _mm(f"out_dw{i}", sv["mix"], dy, "tn", BF16)
        if kind == 0:
            grads["sg_w_out"][j] = d_w_out
        else:
            grads[("swa_w_out", "rw_w_out")[kind - 1]] = d_w_out
        notify(i, "out", grads)
        dmix = _mm(f"out_dx{i}", dy, lw["w_out"], "nt", F32, jobs=jobs(f"out_dx{i}"))
        if kind == 0:
            dp, dlg, dlb, dws, dbs = _sg_bwd(f"sg_mix_bwd{i}", sv["p"], dmix, lw["ln_g"], lw["ln_b"], lw["w_s"], lw["b_st"],
                                            jobs=jobs(f"sg_mix_bwd{i}"))
            grads["sg_ln_g"][j], grads["sg_ln_b"][j], grads["sg_w_s"][j], grads["sg_b_st"][j] = dlg, dlb, dws, dbs
            grads["sg_w_in"][j] = _mm(f"sg_in_dw{i}", sv["h"], dp, "tn", BF16, jobs=jobs(f"sg_in_dw{i}"))
            notify(i, "in", grads)
            dh = _mm(f"sg_in_dx{i}", dp, lw["w_in"], "nt", F32, jobs=jobs(f"sg_in_dx{i}"))
        elif kind == 1:
            z_off = d + 2 * SWA_KV * HEAD
            do, dz = _gate_bwd(f"swa_gate_bwd{i}", sv["o"], sv["p"], z_off, d, dmix)
            dq, dkc, dkp, dvc, dvp, dsinks = _swa_attn_bwd(f"swa_attn_bwd{i}", *sv["qkv"], lw["sinks"], do,
                                                           jobs=jobs(f"swa_attn_bwd{i}"))
            dkp, dvp = _shift_up(dkp, SWA_BLOCK), _shift_up(dvp, SWA_BLOCK)
            dp = _swa_post_bwd(f"swa_post_bwd{i}", dq, dkc, dkp, dvc, dvp, dz, pos, inv_freq)
            grads.update(swa_sinks=dsinks, swa_w_out=d_w_out)
            grads["swa_w_in"] = _mm(f"swa_in_dw{i}", sv["h"], dp, "tn", BF16, jobs=jobs(f"swa_in_dw{i}"))
            dh = _mm(f"swa_in_dx{i}", dp, lw["w_in"], "nt", F32, jobs=jobs(f"swa_in_dx{i}"))
        else:
            do, dz = _gate_bwd(f"rw_gate_bwd{i}", sv["o"], sv["z"], 0, d, dmix)
            res = _rwkv_scan_bwd(f"rw_scan_bwd{i}", *sv["seqs"], lw["hp"], sv["states"], do, jobs=jobs(f"rw_scan_bwd{i}"))
            dr, dk, dv, dlogw, da = res[:5]
            dpll, dw0, dwl, da0, dal = _lora_bwd(f"rw_lora_bwd{i}", sv["pll"], dlogw, da, lw["w0"], lw["w_lora"],
                                                  lw["a0"], lw["a_lora"])
            dpm, dmu_main = _lerp_bwd(f"rw_lerp_bwd{i}", [dr, dk, dv, dz], sv["pm"], lw["mu_main"])
            dpl, dmu_lora = _lerp_bwd(f"rw_lerpl_bwd{i}", [dpll], sv["plo"], lw["mu_lora"])
            grads.update(rw_w_out=d_w_out, rw_hp=res[5:], rw_w0=dw0, rw_w_lora=dwl, rw_a0=da0, rw_a_lora=dal,
                         rw_mu_main=dmu_main, rw_mu_lora=dmu_lora)
            grads["rw_w_main"] = _mm(f"rw_in_dw{i}", sv["h"], dpm, "tn", BF16)
            grads["rw_w_lorain"] = _mm(f"rw_inl_dw{i}", sv["h"], dpl, "tn", BF16)
            dh = _mm(f"rw_inl_dx{i}", dpl, lw["w_lorain"], "nt", F32)
            dh = _mm(f"rw_in_dx{i}", dpm, lw["w_main"], "nt", F32, add=dh)
        if kind != 0:
            notify(i, "in", grads)
        dx, dg, dshift, dscale = _norm_bwd(f"norm_bwd{i}", sv["x"], dh, dx, sv["g"], sv["shift"], sv["scale"])
        grads["norm_g"][i] = dg
        dmods[i] = jnp.concatenate([dshift, dscale, dgate], axis=1)
    return loss, dx, jnp.concatenate(dmods, axis=0), grads


def kernel(x, c, positions, norm_g, mod_w, mod_b, final_norm_g, sg_w_in, sg_w_out, sg_ln_g, sg_ln_b, sg_w_spatial, sg_b_spatial, swa_w_in, swa_w_out, swa_sinks, rwkv_w_in, rwkv_w_out, rwkv_mu, rwkv_w0, rwkv_w_lora, rwkv_a0, rwkv_a_lora, rwkv_k_k, rwkv_k_a, rwkv_r_k, rwkv_gn_g, rwkv_gn_b, loss_target, m_norm_g, m_mod_w, m_mod_b, m_final_norm_g, m_sg_w_in, m_sg_w_out, m_sg_ln_g, m_sg_ln_b, m_sg_w_spatial, m_sg_b_spatial, m_swa_w_in, m_swa_w_out, m_swa_sinks, m_rwkv_w_in, m_rwkv_w_out, m_rwkv_mu, m_rwkv_w0, m_rwkv_w_lora, m_rwkv_a0, m_rwkv_a_lora, m_rwkv_k_k, m_rwkv_k_a, m_rwkv_r_k, m_rwkv_gn_g, m_rwkv_gn_b, v_norm_g, v_mod_w, v_mod_b, v_final_norm_g, v_sg_w_in, v_sg_w_out, v_sg_ln_g, v_sg_ln_b, v_sg_w_spatial, v_sg_b_spatial, v_swa_w_in, v_swa_w_out, v_swa_sinks, v_rwkv_w_in, v_rwkv_w_out, v_rwkv_mu, v_rwkv_w0, v_rwkv_w_lora, v_rwkv_a0, v_rwkv_a_lora, v_rwkv_k_k, v_rwkv_k_a, v_rwkv_r_k, v_rwkv_gn_g, v_rwkv_gn_b):
    weights = dict(norm_g=norm_g, mod_w=mod_w, mod_b=mod_b, final_norm_g=final_norm_g, sg_w_in=sg_w_in, sg_w_out=sg_w_out,
                   sg_ln_g=sg_ln_g, sg_ln_b=sg_ln_b, sg_w_spatial=sg_w_spatial, sg_b_spatial=sg_b_spatial, swa_w_in=swa_w_in,
                   swa_w_out=swa_w_out, swa_sinks=swa_sinks, rwkv_w_in=rwkv_w_in, rwkv_w_out=rwkv_w_out, rwkv_mu=rwkv_mu,
                   rwkv_w0=rwkv_w0, rwkv_w_lora=rwkv_w_lora, rwkv_a0=rwkv_a0, rwkv_a_lora=rwkv_a_lora, rwkv_k_k=rwkv_k_k,
                   rwkv_k_a=rwkv_k_a, rwkv_r_k=rwkv_r_k, rwkv_gn_g=rwkv_gn_g, rwkv_gn_b=rwkv_gn_b)
    mom_m = dict(norm_g=m_norm_g, mod_w=m_mod_w, mod_b=m_mod_b, final_norm_g=m_final_norm_g, sg_w_in=m_sg_w_in,
                 sg_w_out=m_sg_w_out, sg_ln_g=m_sg_ln_g, sg_ln_b=m_sg_ln_b, sg_w_spatial=m_sg_w_spatial,
                 sg_b_spatial=m_sg_b_spatial, swa_w_in=m_swa_w_in, swa_w_out=m_swa_w_out, swa_sinks=m_swa_sinks,
                 rwkv_w_in=m_rwkv_w_in, rwkv_w_out=m_rwkv_w_out, rwkv_mu=m_rwkv_mu, rwkv_w0=m_rwkv_w0,
                 rwkv_w_lora=m_rwkv_w_lora, rwkv_a0=m_rwkv_a0, rwkv_a_lora=m_rwkv_a_lora, rwkv_k_k=m_rwkv_k_k,
                 rwkv_k_a=m_rwkv_k_a, rwkv_r_k=m_rwkv_r_k, rwkv_gn_g=m_rwkv_gn_g, rwkv_gn_b=m_rwkv_gn_b)
    mom_v = dict(norm_g=v_norm_g, mod_w=v_mod_w, mod_b=v_mod_b, final_norm_g=v_final_norm_g, sg_w_in=v_sg_w_in,
                 sg_w_out=v_sg_w_out, sg_ln_g=v_sg_ln_g, sg_ln_b=v_sg_ln_b, sg_w_spatial=v_sg_w_spatial,
                 sg_b_spatial=v_sg_b_spatial, swa_w_in=v_swa_w_in, swa_w_out=v_swa_w_out, swa_sinks=v_swa_sinks,
                 rwkv_w_in=v_rwkv_w_in, rwkv_w_out=v_rwkv_w_out, rwkv_mu=v_rwkv_mu, rwkv_w0=v_rwkv_w0,
                 rwkv_w_lora=v_rwkv_w_lora, rwkv_a0=v_rwkv_a0, rwkv_a_lora=v_rwkv_a_lora, rwkv_k_k=v_rwkv_k_k,
                 rwkv_k_a=v_rwkv_k_a, rwkv_r_k=v_rwkv_r_k, rwkv_gn_g=v_rwkv_gn_g, rwkv_gn_b=v_rwkv_gn_b)
    names = list(weights)
    t, d = x.shape[1], x.shape[2]
    me = 4 * lax.axis_index("x") + 2 * lax.axis_index("y") + lax.axis_index("c")
    n_mod = mod_w.shape[2]
    n_rw = rwkv_w_in.shape[2]

    small_names = ["sg_ln_g", "sg_ln_b", "rwkv_mu", "rwkv_w0", "rwkv_a0", "rwkv_k_k", "rwkv_k_a", "rwkv_gn_g", "rwkv_gn_b",
                   "rwkv_w_lora", "rwkv_a_lora"]
    small_shapes = [weights[n].shape for n in small_names]
    pk = _pack([c] + [weights[n] for n in small_names])
    gathered = _gather("gather_small", pk).reshape(N_DEV, -1)
    c_all = gathered[:, :d]
    per_dev = [_unpack(gathered[dv, d:], small_shapes) for dv in range(N_DEV)]
    full_small = {}
    for q, n in enumerate(small_names):
        full_small[n] = jnp.concatenate([per_dev[dv][q] for dv in range(N_DEV)], axis=-1)

    mod_b_cols = lax.dynamic_slice_in_dim(mod_b, me * n_mod, n_mod, axis=1).reshape(DEPTH, 1, n_mod)
    mod_part = _mod_fwd("mod_fwd", c_all, mod_w, mod_b_cols)
    mod_g = _gather("gather_mod", mod_part.reshape(DEPTH * N_DEV, n_mod))
    mod_g = mod_g.reshape(N_DEV, DEPTH, N_DEV, n_mod)
    mods = lax.dynamic_index_in_dim(mod_g, me, axis=2, keepdims=False)
    mods = mods.transpose(1, 0, 2).reshape(DEPTH, N_DEV * n_mod)

    job = lambda cls, src: dict(cls=cls, src=src)
    half = d // 2
    gj = dict(swa_in=job(_Gather, swa_w_in[0].astype(BF16)), swa_out=job(_Gather, swa_w_out[0].astype(BF16)),
              rw_in_a=job(_Gather, rwkv_w_in[0, :half].astype(BF16)), rw_in_b=job(_Gather, rwkv_w_in[0, half:].astype(BF16)),
              rw_out=job(_Gather, rwkv_w_out[0].astype(BF16)),
              sg_in1=job(_Gather, sg_w_in[1].astype(BF16)), sg_out1=job(_Gather, sg_w_out[1].astype(BF16)))
    hooks = {"sg_in0": [gj["swa_in"]], "out0": [gj["swa_out"]], "swa_in1": [gj["rw_in_a"]], "swa_attn1": [gj["rw_in_b"]],
             "out1": [gj["rw_out"]], "rw_scan2": [gj["sg_in1"], gj["sg_out1"]]}
    g_sg_in0 = _gather("gather_sg_in0", sg_w_in[0].astype(BF16))
    g_sg_out0 = _gather("gather_sg_out0", sg_w_out[0].astype(BF16))
    lora_rows = lambda w, off: jnp.zeros((LORA_PAD, d), F32).at[off:off + LORA].set(w)
    mu = full_small["rwkv_mu"].reshape(1, -1)
    heads = lambda a: a.reshape(1, -1)

    def layer_weights(i):
        if i % 3 == 0:
            j = i // 3
            g_in, g_out = (g_sg_in0, g_sg_out0) if j == 0 else (gj["sg_in1"]["out"], gj["sg_out1"]["out"])
            return dict(w_in=_cols_full(g_in), w_out=g_out.reshape(d, d), ln_g=full_small["sg_ln_g"][j].reshape(1, d),
                        ln_b=full_small["sg_ln_b"][j].reshape(1, d), w_s=sg_w_spatial[j], b_st=sg_b_spatial[j].T)
        if i % 3 == 1:
            return dict(w_in=_cols_full(gj["swa_in"]["out"]), w_out=gj["swa_out"]["out"].reshape(d, d),
                        sinks=swa_sinks.reshape(SWA_KV, SWA_REP, 1, 1))
        rw_in_full = jnp.concatenate([_cols_full(gj["rw_in_a"]["out"]), _cols_full(gj["rw_in_b"]["out"])], axis=0)
        return dict(w_main=rw_in_full[:, :4 * d], w_lorain=jnp.pad(rw_in_full[:, 4 * d:], ((0, 0), (0, LORA_PAD - 2 * LORA))),
                    w_out=gj["rw_out"]["out"].reshape(d, d), mu_main=mu[:, :4 * d],
                    mu_lora=jnp.pad(mu[:, 4 * d:], ((0, 0), (0, LORA_PAD - 2 * LORA))),
                    w0=full_small["rwkv_w0"], a0=full_small["rwkv_a0"],
                    w_lora=lora_rows(full_small["rwkv_w_lora"][0], 0), a_lora=lora_rows(full_small["rwkv_a_lora"][0], LORA),
                    hp=[heads(full_small["rwkv_k_k"]), heads(full_small["rwkv_k_a"]), heads(rwkv_r_k),
                        heads(full_small["rwkv_gn_g"]), heads(full_small["rwkv_gn_b"])])

    sj = {}

    def on_grads(i, which, g):
        def stage(nm, parts, host):
            sj[nm] = job(_Chips, _scatter_pairs("scatter_" + nm, parts.astype(BF16)))
            hooks.setdefault(host, []).append(sj[nm])

        rows_of = lambda a: a.reshape(N_DEV, -1, d)
        if (i, which) == (3, "out"):
            stage("sg_out1", rows_of(g["sg_w_out"][1]), "rw_scan_bwd2")
        elif (i, which) == (3, "in"):
            stage("sg_in1", _cols_parts(g["sg_w_in"][1]), "rw_scan_bwd2")
        elif (i, which) == (2, "out"):
            stage("rw_out", rows_of(g["rw_w_out"]), "swa_in_dx1")
        elif (i, which) == (2, "in"):
            d_rw_in = jnp.concatenate([g["rw_w_main"], g["rw_w_lorain"][:, :2 * LORA]], axis=1)
            stage("rw_in_a", _cols_parts(d_rw_in[:half]), "swa_attn_bwd1")
            stage("rw_in_b", _cols_parts(d_rw_in[half:]), "swa_in_dw1")
        elif (i, which) == (1, "out"):
            stage("swa_out", rows_of(g["swa_w_out"]), "out_dx0")
        elif (i, which) == (1, "in"):
            stage("swa_in", _cols_parts(g["swa_w_in"]), "sg_mix_bwd0")
        elif (i, which) == (0, "out"):
            stage("sg_out0", rows_of(g["sg_w_out"][0]), "sg_in_dw0")
        else:
            stage("sg_in0", _cols_parts(g["sg_w_in"][0]), "sg_in_dx0")

    loss, dx, dmods, g = _local_step(x[0], positions.reshape(t, 1).astype(F32), loss_target[0], mods, norm_g, final_norm_g,
                                     layer_weights, hooks, on_grads)

    dmod_g = _gather("gather_dmod", dmods)
    dmod_all = dmod_g.transpose(1, 0, 2)
    dmod_cols = lax.dynamic_slice_in_dim(dmod_all, me * n_mod, n_mod, axis=2)
    g_mod_w, g_mod_b = _mod_bwd("mod_bwd", c_all, dmod_cols, dmod_all)

    d_b_sp = [g["sg_b_st"][j].T for j in range(2)]
    rep = [loss[0, :1], jnp.concatenate(g["norm_g"], axis=0), g["final_norm_g"], jnp.stack(g["sg_w_s"]), jnp.stack(d_b_sp),
           g["swa_sinks"], g["rw_hp"][2]]
    rep_shapes = [(1,), norm_g.shape, final_norm_g.shape, sg_w_spatial.shape, sg_b_spatial.shape, swa_sinks.shape, rwkv_r_k.shape]
    rep_sum = _sum_parts("sum_rep", _gather("gather_rep", _pack(rep, 128 * 256))).reshape(-1)
    loss_tot, g_norm_g, g_final, g_w_sp, g_b_sp, g_sinks, g_r_k = _unpack(rep_sum, rep_shapes)

    p_sg_in = jnp.concatenate([sj["sg_in0"]["out"], sj["sg_in1"]["out"]], axis=1)
    p_sg_out = jnp.concatenate([sj["sg_out0"]["out"], sj["sg_out1"]["out"]], axis=1)
    p_swa_in, p_swa_out, p_rw_out = (sj[nm]["out"] for nm in ("swa_in", "swa_out", "rw_out"))
    p_rw_in = jnp.concatenate([sj["rw_in_a"]["out"], sj["rw_in_b"]["out"]], axis=1)
    d_mu = jnp.concatenate([g["rw_mu_main"], g["rw_mu_lora"][:, :2 * LORA]], axis=1)
    hp_flat = lambda a: a.reshape(1, -1)
    small_grads = dict(sg_ln_g=jnp.concatenate(g["sg_ln_g"], axis=0), sg_ln_b=jnp.concatenate(g["sg_ln_b"], axis=0), rwkv_mu=d_mu,
                       rwkv_w0=g["rw_w0"], rwkv_a0=g["rw_a0"], rwkv_k_k=hp_flat(g["rw_hp"][0]), rwkv_k_a=hp_flat(g["rw_hp"][1]),
                       rwkv_gn_g=hp_flat(g["rw_hp"][3]), rwkv_gn_b=hp_flat(g["rw_hp"][4]),
                       rwkv_w_lora=g["rw_w_lora"][None, :LORA], rwkv_a_lora=g["rw_a_lora"][None, LORA:2 * LORA])
    per_dest = []
    for dv in range(N_DEV):
        shards = []
        for n in small_names:
            full, w = small_grads[n], weights[n].shape[-1]
            shards.append(full[..., dv * w:(dv + 1) * w])
        per_dest.append(_pack(shards))
    small_parts = _exchange("scatter_small", jnp.stack(per_dest), True)

    out_g, out_d, out_m, out_v = {}, {}, {}, {}

    def update(name, grad, shape2d, jobs=()):
        w2, m2, v2 = (a[name].reshape(shape2d) for a in (weights, mom_m, mom_v))
        gg, dd, mm, vv = _adamw("adamw_" + name, w2, grad, m2, v2, jobs=jobs)
        shp = weights[name].shape
        out_g[name], out_d[name], out_m[name], out_v[name] = gg.reshape(shp), dd.reshape(shp), mm.reshape(shp), vv.reshape(shp)

    update("mod_w", g_mod_w.reshape(-1, n_mod), (-1, n_mod))
    update("sg_w_in", p_sg_in, (-1, sg_w_in.shape[2]))
    update("sg_w_out", p_sg_out, (-1, d))
    update("swa_w_in", p_swa_in, (-1, swa_w_in.shape[2]))
    update("swa_w_out", p_swa_out, (-1, d))
    update("rwkv_w_in", p_rw_in, (-1, n_rw))
    update("rwkv_w_out", p_rw_out, (-1, d))
    update("sg_w_spatial", g_w_sp.reshape(-1, 128), (-1, 128))
    w_pk, m_pk, v_pk = (_pack([a[n] for n in small_names]) for a in (weights, mom_m, mom_v))
    res = _adamw("adamw_small", w_pk, small_parts, m_pk, v_pk)
    for q, arrs in enumerate(zip(*[_unpack(r_.reshape(-1), small_shapes) for r_ in res])):
        out_g[small_names[q]], out_d[small_names[q]], out_m[small_names[q]], out_v[small_names[q]] = arrs
    rep_names = ["norm_g", "mod_b", "final_norm_g", "sg_b_spatial", "swa_sinks", "rwkv_r_k"]
    rep_grads = [g_norm_g, g_mod_b.reshape(mod_b.shape), g_final, g_b_sp, g_sinks, g_r_k]
    rep_shapes2 = [weights[n].shape for n in rep_names]
    w_pk, m_pk, v_pk = (_pack([a[n] for n in rep_names]) for a in (weights, mom_m, mom_v))
    res = _adamw("adamw_rep", w_pk, _pack(rep_grads), m_pk, v_pk)
    for q, arrs in enumerate(zip(*[_unpack(r_.reshape(-1), rep_shapes2) for r_ in res])):
        out_g[rep_names[q]], out_d[rep_names[q]], out_m[rep_names[q]], out_v[rep_names[q]] = arrs

    return (loss_tot.reshape(()), dx[None], *[out_g[n] for n in names], *[out_d[n] for n in names],
            *[out_m[n] for n in names], *[out_v[n] for n in names])
```

```python
import functools
import math

import jax
import jax.numpy as jnp
from jax import lax
from jax.experimental import pallas as pl
from jax.experimental.pallas import tpu as pltpu

F32, BF16 = jnp.float32, jnp.bfloat16
HI = lax.Precision.HIGHEST
S = jax.ShapeDtypeStruct
MESH = pl.DeviceIdType.MESH

N_DEV = 8
DEPTH = 4
HEAD = 64
SG_GROUPS = 16
SG_CHUNK = 128
SWA_BLOCK = 128
SWA_KV = 4
SWA_REP = 8
ROPE_THETA = 10000.0
LORA = 96
LORA_PAD = 256
RW_CHUNK = 64
RW_HEADS = 16
RW_HEADS_FWD = 32
RW_PREC = lax.Precision.HIGH
DECAY_SCALE = math.exp(-0.5)
GN_EPS = 64e-5
RMS_EPS = 1e-6
LN_EPS = 1e-5
NEG = -1e30
ADAM_LR, ADAM_B1, ADAM_B2, ADAM_EPS, ADAM_WD, ADAM_STEP = 0.001, 0.9, 0.999, 1e-08, 0.01, 10
VMEM_MB = 56


def _params(sem=None):
    kw = dict(vmem_limit_bytes=VMEM_MB << 20)
    if sem is not None:
        kw["dimension_semantics"] = sem
    return pltpu.CompilerParams(**kw)


def _pick(n, opts):
    for o in opts:
        if n % o == 0:
            return o
    raise ValueError(f"no tile for {n}")


def _rows(name, fn, rows, consts, out_rows, out_accs, tm, jobs=()):
    t = rows[0].shape[0]
    nr, nc, no = len(rows), len(consts), len(out_rows)

    def body(*refs):
        outs = fn(*[r[...] for r in refs[:nr + nc]])
        if not isinstance(outs, (tuple, list)):
            outs = (outs,)
        for r, o in zip(refs[nr + nc:nr + nc + no], outs[:no]):
            r[...] = o.astype(r.dtype)
        i = pl.program_id(0)
        for r, o in zip(refs[nr + nc + no:], outs[no:]):
            @pl.when(i == 0)
            def _(r=r, o=o):
                r[...] = o.astype(r.dtype)

            @pl.when(i > 0)
            def _(r=r, o=o):
                r[...] += o.astype(r.dtype)

    in_specs = [pl.BlockSpec((tm, a.shape[1]), lambda i: (i, 0)) for a in rows]
    in_specs += [pl.BlockSpec(c.shape, lambda i, nd=c.ndim: (0,) * nd) for c in consts]
    out_specs = [pl.BlockSpec((tm, n), lambda i: (i, 0)) for n, _ in out_rows]
    out_specs += [pl.BlockSpec(s, lambda i, nd=len(s): (0,) * nd) for s in out_accs]
    out_shape = [S((t, n), dt) for n, dt in out_rows] + [S(s, F32) for s in out_accs]
    return _pcall(body, grid=(t // tm,), in_specs=in_specs, out_specs=out_specs, out_shape=out_shape, name=name,
                  semantics=("arbitrary",), inputs=(*rows, *consts), jobs=jobs)


_DN = {"nn": (((1,), (0,)), ((), ())), "nt": (((1,), (1,)), ((), ())), "tn": (((0,), (0,)), ((), ()))}


def _mm(name, a, b, mode, out_dtype, add=None, resid=None, jobs=()):
    if mode == "nn":
        (m, k), (_, n) = a.shape, b.shape
    elif mode == "nt":
        (m, k), (n, _) = a.shape, b.shape
    else:
        (k, m), (_, n) = a.shape, b.shape
    wide = mode != "tn" and add is None and resid is None
    tm = _pick(m, (1024, 512, 256, 128))
    tn = _pick(n, ((1536,) if wide and n % 1024 else ()) + (1024, 512, 384, 256, 128))
    long_k = (4096,) if mode == "tn" else (3072, 2304) if mode == "nt" and add is None else ()
    tk = _pick(k, long_k + (2048, 1536, 1024, 512, 384, 256, 128))
    nk = k // tk
    n_extra = (add is not None) + 2 * (resid is not None)

    def body(*refs):
        a_ref, b_ref = refs[0], refs[1]
        extra, outs, acc = refs[2:2 + n_extra], refs[2 + n_extra:-1], refs[-1]
        kk = pl.program_id(2)
        prod = lax.dot_general(a_ref[...].astype(BF16), b_ref[...].astype(BF16), _DN[mode], preferred_element_type=F32)
        if add is not None:
            prod = jnp.where(kk == 0, prod + extra[0][...].astype(F32), prod) if nk > 1 else prod + extra[0][...].astype(F32)

        def finish(total):
            outs[0][...] = total.astype(outs[0].dtype)
            if resid is not None:
                outs[1][...] = extra[-2][...] + extra[-1][...] * total

        if nk == 1:
            finish(prod)
            return

        @pl.when(kk == 0)
        def _():
            acc[...] = prod

        @pl.when(kk > 0)
        def _():
            acc[...] += prod

        @pl.when(kk == nk - 1)
        def _():
            finish(acc[...])

    a_spec = pl.BlockSpec((tk, tm), lambda i, j, q: (q, i)) if mode == "tn" else pl.BlockSpec((tm, tk), lambda i, j, q: (i, q))
    b_spec = pl.BlockSpec((tn, tk), lambda i, j, q: (j, q)) if mode == "nt" else pl.BlockSpec((tk, tn), lambda i, j, q: (q, j))
    o_spec = pl.BlockSpec((tm, tn), lambda i, j, q: (i, j))
    ins, specs, out_specs, out_shape = [a, b], [a_spec, b_spec], [o_spec], [S((m, n), out_dtype)]
    if add is not None:
        ins.append(add)
        specs.append(o_spec)
    if resid is not None:
        ins += list(resid)
        specs += [o_spec, pl.BlockSpec((1, tn), lambda i, j, q: (0, j))]
        out_specs.append(o_spec)
        out_shape.append(S((m, n), F32))
    res = _pcall(body, grid=(m // tm, n // tn, nk), in_specs=specs, out_specs=out_specs, out_shape=out_shape,
                 scratch_shapes=[pltpu.VMEM((tm, tn), F32)], name=name, semantics=("parallel", "parallel", "arbitrary"),
                 inputs=ins, jobs=jobs)
    return res if resid is not None else res[0]


def _exchange(name, src, scatter):
    blk = src.shape[1:] if scatter else src.shape

    def body(src_ref, dst_ref, send_sems, recv_sems, loc_sem):
        x, y, c = lax.axis_index("x"), lax.axis_index("y"), lax.axis_index("c")
        me = 4 * x + 2 * y + c

        def mine(d):
            return src_ref.at[d] if scatter else src_ref

        local = pltpu.make_async_copy(mine(me), dst_ref.at[me], loc_sem)
        local.start()
        sends, peers = [], []
        for k in range(1, N_DEV):
            px = 1 - x if k & 4 else x
            py = 1 - y if k & 2 else y
            pc = 1 - c if k & 1 else c
            pid = 4 * px + 2 * py + pc
            cp = pltpu.make_async_remote_copy(src_ref=mine(pid), dst_ref=dst_ref.at[me], send_sem=send_sems.at[k - 1],
                                              recv_sem=recv_sems.at[k - 1], device_id=(px, py, pc), device_id_type=MESH)
            cp.start()
            sends.append(cp)
            peers.append((pid, (px, py, pc)))
        for k in range(1, N_DEV):
            pid, dev = peers[k - 1]
            pltpu.make_async_remote_copy(src_ref=mine(pid), dst_ref=dst_ref.at[pid], send_sem=send_sems.at[k - 1],
                                         recv_sem=recv_sems.at[k - 1], device_id=dev, device_id_type=MESH).wait_recv()
        for cp in sends:
            cp.wait_send()
        local.wait()

    return pl.pallas_call(
        body, out_shape=S((N_DEV,) + tuple(blk), src.dtype),
        in_specs=[pl.BlockSpec(memory_space=pl.ANY)], out_specs=pl.BlockSpec(memory_space=pl.ANY),
        scratch_shapes=[pltpu.SemaphoreType.DMA((N_DEV - 1,)), pltpu.SemaphoreType.DMA((N_DEV - 1,)),
                        pltpu.SemaphoreType.DMA],
        name=name)(src)


class _Gather:
    @staticmethod
    def out_shape(src):
        return S((N_DEV,) + tuple(src.shape), src.dtype)

    scratch = (pltpu.SemaphoreType.DMA((N_DEV - 1,)), pltpu.SemaphoreType.DMA((N_DEV - 1,)), pltpu.SemaphoreType.DMA)

    def __init__(self, src_ref, dst_ref, send_sems, recv_sems, loc_sem):
        self.refs = (src_ref, dst_ref, send_sems, recv_sems, loc_sem)
        x, y, c = lax.axis_index("x"), lax.axis_index("y"), lax.axis_index("c")
        self.c, self.me, self.sibling = c, (x, y, c), (x, y, 1 - c)
        self.chips = [(1 - x, y), (x, 1 - y), (1 - x, 1 - y)]

    def rows(self, px, py, pc):
        return self.refs[1].at[4 * px + 2 * py + pc]

    def copy(self, k, block, to, own=False):
        src_ref, _, send_sems, recv_sems, _ = self.refs
        return pltpu.make_async_remote_copy(src_ref=src_ref if own else self.rows(*block), dst_ref=self.rows(*block),
                                            send_sem=send_sems.at[k], recv_sem=recv_sems.at[k], device_id=to,
                                            device_id_type=MESH)

    def local(self):
        return pltpu.make_async_copy(self.refs[0], self.rows(*self.me), self.refs[4])

    def first(self):
        return [self.copy(0, self.me, self.sibling, own=True)] + [self.copy(1 + j, self.me, (*chip, self.c), own=True)
                                                                  for j, chip in enumerate(self.chips)]

    def start(self):
        self.local().start()
        for cp in self.first():
            cp.start()

    def finish(self):
        c = self.c
        passed = [self.copy(4 + j, (*chip, c), self.sibling) for j, chip in enumerate(self.chips)]
        for j, chip in enumerate(self.chips):
            self.copy(1 + j, (*chip, c), self.me).wait_recv()
            passed[j].start()
        self.copy(0, self.sibling, self.me).wait_recv()
        for j, chip in enumerate(self.chips):
            self.copy(4 + j, (*chip, 1 - c), self.me).wait_recv()
        for cp in self.first() + passed:
            cp.wait_send()
        self.local().wait()


class _Chips:
    @staticmethod
    def out_shape(src):
        return S(src.shape, src.dtype)

    scratch = (pltpu.SemaphoreType.DMA((N_DEV // 2 - 1,)), pltpu.SemaphoreType.DMA((N_DEV // 2 - 1,)), pltpu.SemaphoreType.DMA)

    def __init__(self, src_ref, dst_ref, send_sems, recv_sems, loc_sem):
        self.refs = (src_ref, dst_ref, send_sems, recv_sems, loc_sem)
        x, y, c = lax.axis_index("x"), lax.axis_index("y"), lax.axis_index("c")
        self.c, self.mine = c, 2 * x + y
        self.chips = [(1 - x, y), (x, 1 - y), (1 - x, 1 - y)]

    def local(self):
        src_ref, dst_ref, _, _, loc_sem = self.refs
        return pltpu.make_async_copy(src_ref.at[self.mine], dst_ref.at[self.mine], loc_sem)

    def send(self, j):
        src_ref, dst_ref, send_sems, recv_sems, _ = self.refs
        px, py = self.chips[j]
        return pltpu.make_async_remote_copy(src_ref=src_ref.at[2 * px + py], dst_ref=dst_ref.at[self.mine],
                                            send_sem=send_sems.at[j], recv_sem=recv_sems.at[j],
                                            device_id=(px, py, self.c), device_id_type=MESH)

    def arrival(self, j):
        src_ref, dst_ref, send_sems, recv_sems, _ = self.refs
        px, py = self.chips[j]
        return pltpu.make_async_remote_copy(src_ref=src_ref.at[self.mine], dst_ref=dst_ref.at[2 * px + py],
                                            send_sem=send_sems.at[j], recv_sem=recv_sems.at[j],
                                            device_id=(px, py, self.c), device_id_type=MESH)

    def start(self):
        self.local().start()
        for j in range(len(self.chips)):
            self.send(j).start()

    def finish(self):
        for j in range(len(self.chips)):
            self.arrival(j).wait_recv()
        for j in range(len(self.chips)):
            self.send(j).wait_send()
        self.local().wait()


def _exchange_call(name, cls, src):
    def body(*refs):
        ex = cls(*refs)
        ex.start()
        ex.finish()

    return pl.pallas_call(body, out_shape=cls.out_shape(src), in_specs=[pl.BlockSpec(memory_space=pl.ANY)],
                          out_specs=pl.BlockSpec(memory_space=pl.ANY), scratch_shapes=list(cls.scratch), name=name)(src)


def _gather(name, src):
    return _exchange_call(name, _Gather, src)


def _pcall(body, *, grid, in_specs, out_specs, out_shape, scratch_shapes=(), name, semantics, inputs, jobs=()):
    if not jobs:
        return pl.pallas_call(body, grid=grid, in_specs=in_specs, out_specs=out_specs, out_shape=out_shape,
                              scratch_shapes=list(scratch_shapes), name=name, compiler_params=_params(semantics))(*inputs)
    n_in, n_out, n_scr, nj = len(in_specs), len(out_specs), len(scratch_shapes), len(jobs)

    def hosted(*refs):
        ins, srcs = refs[:n_in], refs[n_in:n_in + nj]
        outs, dsts = refs[n_in + nj:n_in + nj + n_out], refs[n_in + nj + n_out:n_in + 2 * nj + n_out]
        scr, sems = refs[n_in + 2 * nj + n_out:n_in + 2 * nj + n_out + n_scr], refs[n_in + 2 * nj + n_out + n_scr:]
        ids = [pl.program_id(q) for q in range(len(grid))]
        first = functools.reduce(jnp.logical_and, [i == 0 for i in ids])
        last = functools.reduce(jnp.logical_and, [i == g - 1 for i, g in zip(ids, grid)])
        make = lambda q: jobs[q]["cls"](srcs[q], dsts[q], *sems[3 * q:3 * q + 3])

        @pl.when(first)
        def _():
            for q in range(nj):
                make(q).start()

        body(*ins, *outs, *scr)

        @pl.when(last)
        def _():
            for q in range(nj):
                make(q).finish()

    anyspec = pl.BlockSpec(memory_space=pl.ANY)
    res = pl.pallas_call(
        hosted, grid=grid, in_specs=list(in_specs) + [anyspec] * nj, out_specs=list(out_specs) + [anyspec] * nj,
        out_shape=list(out_shape) + [j["cls"].out_shape(j["src"]) for j in jobs],
        scratch_shapes=list(scratch_shapes) + [s for j in jobs for s in j["cls"].scratch], name=name,
        compiler_params=_params(("arbitrary",) * len(grid)))(*inputs, *[j["src"] for j in jobs])
    for j, out in zip(jobs, res[n_out:]):
        j["out"] = out
    return res[:n_out]


def _scatter_pairs(name, parts):
    _, r, c_ = parts.shape
    n_chip = N_DEV // 2

    def stage1(src_ref, dst_ref, send_sems, recv_sems):
        x, y, c = lax.axis_index("x"), lax.axis_index("y"), lax.axis_index("c")
        sends = []
        for q in range(n_chip):
            cp = pltpu.make_async_remote_copy(src_ref=src_ref.at[2 * q + 1 - c], dst_ref=dst_ref.at[q],
                                              send_sem=send_sems.at[q], recv_sem=recv_sems.at[q],
                                              device_id=(x, y, 1 - c), device_id_type=MESH)
            cp.start()
            sends.append(cp)
        for q in range(n_chip):
            pltpu.make_async_remote_copy(src_ref=src_ref.at[2 * q + c], dst_ref=dst_ref.at[q], send_sem=send_sems.at[q],
                                         recv_sem=recv_sems.at[q], device_id=(x, y, 1 - c), device_id_type=MESH).wait_recv()
        for cp in sends:
            cp.wait_send()

    from_sibling = pl.pallas_call(
        stage1, out_shape=S((n_chip, r, c_), parts.dtype),
        in_specs=[pl.BlockSpec(memory_space=pl.ANY)], out_specs=pl.BlockSpec(memory_space=pl.ANY),
        scratch_shapes=[pltpu.SemaphoreType.DMA((n_chip,)), pltpu.SemaphoreType.DMA((n_chip,))], name=name + "_pair")(parts)

    tm = _pick(r, (512, 256, 128, 64, 32, 16, 8)) if r % 8 == 0 else r
    core = lax.axis_index("c").astype(jnp.int32).reshape(1)

    def pair_sum(core_ref, mine_ref, sib_ref, o_ref):
        o_ref[...] = (mine_ref[0].astype(F32) + sib_ref[...].astype(F32)).astype(o_ref.dtype)

    pair = pl.pallas_call(
        pair_sum, out_shape=S((n_chip, r, c_), parts.dtype),
        grid_spec=pltpu.PrefetchScalarGridSpec(
            num_scalar_prefetch=1, grid=(n_chip, r // tm),
            in_specs=[pl.BlockSpec((1, 1, tm, c_), lambda q, i, core_ref: (q, core_ref[0], i, 0)),
                      pl.BlockSpec((1, tm, c_), lambda q, i, core_ref: (q, i, 0))],
            out_specs=pl.BlockSpec((1, tm, c_), lambda q, i, core_ref: (q, i, 0))),
        name=name + "_sum", compiler_params=_params(("parallel", "parallel")))(
            core, parts.reshape(n_chip, 2, r, c_), from_sibling)

    return pair


def _sum_parts(name, parts):
    n_parts, r, c = parts.shape
    tm = _pick(r, (512, 256, 128, 64, 32, 16, 8)) if r % 8 == 0 else r

    def body(p_ref, o_ref):
        acc = p_ref[0].astype(F32)
        for d in range(1, n_parts):
            acc = acc + p_ref[d].astype(F32)
        o_ref[...] = acc

    return pl.pallas_call(body, grid=(r // tm,), in_specs=[pl.BlockSpec((n_parts, tm, c), lambda i: (0, i, 0))],
                          out_specs=pl.BlockSpec((tm, c), lambda i: (i, 0)), out_shape=S((r, c), F32), name=name,
                          compiler_params=_params(("parallel",)))(parts)


def _adamw(name, w, g, m, v, jobs=()):
    r, c = w.shape
    parts = g.ndim == 3
    n_parts = g.shape[0] if parts else 1
    tile_rows = max(8, (2 << 20) // (4 * c))
    tm = _pick(r, tuple(q for q in (2048, 1024, 512, 256, 128, 64, 32, 16, 8) if q <= tile_rows)) if r % 8 == 0 else r

    def body(w_ref, g_ref, m_ref, v_ref, go_ref, d_ref, mo_ref, vo_ref):
        if parts:
            gg = g_ref[0].astype(F32)
            for d in range(1, n_parts):
                gg = gg + g_ref[d].astype(F32)
        else:
            gg = g_ref[...]
        mm = ADAM_B1 * m_ref[...] + (1.0 - ADAM_B1) * gg
        vv = ADAM_B2 * v_ref[...] + (1.0 - ADAM_B2) * jnp.square(gg)
        m_hat = mm / (1.0 - ADAM_B1 ** ADAM_STEP)
        v_hat = vv / (1.0 - ADAM_B2 ** ADAM_STEP)
        go_ref[...] = gg
        d_ref[...] = -ADAM_LR * (m_hat / (jnp.sqrt(v_hat) + ADAM_EPS) + ADAM_WD * w_ref[...])
        mo_ref[...] = mm
        vo_ref[...] = vv

    spec = pl.BlockSpec((tm, c), lambda i: (i, 0))
    g_spec = pl.BlockSpec((n_parts, tm, c), lambda i: (0, i, 0)) if parts else spec
    return _pcall(body, grid=(r // tm,), in_specs=[spec, g_spec, spec, spec], out_specs=[spec] * 4,
                  out_shape=[S((r, c), F32)] * 4, name=name, semantics=("parallel",), inputs=(w, g, m, v), jobs=jobs)


def _rms(x, g):
    return x * lax.rsqrt(jnp.mean(x * x, axis=-1, keepdims=True) + RMS_EPS) * g


def _adaln(x, g, shift, scale):
    return _rms(x, g) * (1.0 + scale) + shift


def _dot(a, b, dn="nn", hi=False, prec=None):
    if hi or prec is not None:
        return lax.dot_general(a, b, _DN[dn], precision=HI if hi else prec, preferred_element_type=F32)
    return lax.dot_general(a.astype(BF16), b.astype(BF16), _DN[dn], preferred_element_type=F32)


def _sg_mix(p, ln_g, ln_b, w_s, b_st):
    d = p.shape[1] // 3
    gd = d // SG_GROUPS
    u = jax.nn.gelu(p[:, :d])
    vf = jax.nn.gelu(p[:, d:2 * d])
    z = p[:, 2 * d:]
    mean = jnp.mean(vf, axis=-1, keepdims=True)
    var = jnp.mean(jnp.square(vf - mean), axis=-1, keepdims=True)
    vn = (vf - mean) * lax.rsqrt(var + LN_EPS) * ln_g + ln_b
    row = lax.broadcasted_iota(jnp.int32, (SG_CHUNK, SG_CHUNK), 0)
    col = lax.broadcasted_iota(jnp.int32, (SG_CHUNK, SG_CHUNK), 1)
    fs = []
    for g in range(SG_GROUPS):
        w = jnp.where(row >= col, w_s[g], 0.0)
        fs.append(_dot(w, vn[:, g * gd:(g + 1) * gd]))
    sel = (lax.broadcasted_iota(jnp.int32, (SG_GROUPS, d), 1) // gd
           == lax.broadcasted_iota(jnp.int32, (SG_GROUPS, d), 0)).astype(F32)
    f = jnp.concatenate(fs, axis=1) + _dot(b_st, sel, hi=True)
    return u * f * jax.nn.silu(z)


def _rot_half(x):
    n = x.shape[1]
    lane = lax.broadcasted_iota(jnp.int32, x.shape, 1)
    return jnp.where(lane % HEAD < HEAD // 2, -pltpu.roll(x, n - HEAD // 2, 1), pltpu.roll(x, HEAD // 2, 1))


def _rope(x, cos, sin, sign):
    reps = x.shape[1] // cos.shape[1]
    return x * jnp.tile(cos, (1, reps)) + sign * _rot_half(x) * jnp.tile(sin, (1, reps))


def _attn_block(q, kp, kc, vp, vc, sink, prev_bias):
    each = lambda f, *ls: [f(*xs) for xs in zip(*ls)]
    r = sink[0].shape[0]
    cur, prob, _ = _attn_probs(q, kp, kc, sink, prev_bias)
    flat = lambda x: x.reshape(r * SWA_BLOCK, SWA_BLOCK)
    pc, pp = each(lambda p: flat(jnp.where(cur, p, 0.0)), prob), each(lambda p: flat(jnp.where(cur, 0.0, p)), prob)
    return each(lambda a, va, b, vb: _dot(a, va) + _dot(b, vb), pp, vp, pc, vc)


def _attn_probs(q, kp, kc, sink, prev_bias):
    each = lambda f, *ls: [f(*xs) for xs in zip(*ls)]
    r = sink[0].shape[0]
    scores = lambda a, b: (_dot(a, b, "nt") * (HEAD ** -0.5)).reshape(r, SWA_BLOCK, SWA_BLOCK)
    sp, sc = each(scores, q, kp), each(scores, q, kc)
    cur = (lax.broadcasted_iota(jnp.int32, (r, SWA_BLOCK, SWA_BLOCK), 2)
           <= lax.broadcasted_iota(jnp.int32, (r, SWA_BLOCK, SWA_BLOCK), 1))
    s = each(lambda a, b: jnp.where(cur, b, a + prev_bias), sp, sc)
    m = each(lambda a, sk: jnp.maximum(jnp.max(a, axis=-1, keepdims=True), sk), s, sink)
    e, es = each(lambda a, m_: jnp.exp(a - m_), s, m), each(lambda sk, m_: jnp.exp(sk - m_), sink, m)
    denom = each(lambda a, b: jnp.sum(a, axis=-1, keepdims=True) + b, e, es)
    return cur, each(lambda a, dn: a / dn, e, denom), each(lambda a, dn: a / dn, es, denom)


def _attn_block_bwd(q, kp, kc, vp, vc, sink, do, prev_bias):
    each = lambda f, *ls: [f(*xs) for xs in zip(*ls)]
    r = sink[0].shape[0]
    scale = HEAD ** -0.5
    cube = lambda x: x.reshape(r, SWA_BLOCK, SWA_BLOCK)
    flat = lambda x: x.reshape(r * SWA_BLOCK, SWA_BLOCK)
    cur, prob, p_sink = _attn_probs(q, kp, kc, sink, prev_bias)
    pc, pp = each(lambda p: flat(jnp.where(cur, p, 0.0)), prob), each(lambda p: flat(jnp.where(cur, 0.0, p)), prob)
    dprob = each(lambda g, va, vb: jnp.where(cur, cube(_dot(g, vb, "nt")), cube(_dot(g, va, "nt"))), do, vp, vc)
    dvp, dvc = each(lambda p, g: _dot(p, g, "tn"), pp, do), each(lambda p, g: _dot(p, g, "tn"), pc, do)
    delta = each(lambda p, dp: jnp.sum(p * dp, axis=-1, keepdims=True), prob, dprob)
    ds = each(lambda p, dp, dl: p * (dp - dl), prob, dprob, delta)
    dsc, dsp = each(lambda x: flat(jnp.where(cur, x, 0.0)), ds), each(lambda x: flat(jnp.where(cur, 0.0, x)), ds)
    dsink = each(lambda ps, dl: -jnp.sum(ps * dl, axis=1, keepdims=True), p_sink, delta)
    dq = each(lambda a, ka, b, kb: (_dot(a, ka) + _dot(b, kb)) * scale, dsp, kp, dsc, kc)
    dkp, dkc = each(lambda a, q_: _dot(a, q_, "tn") * scale, dsp, q), each(lambda a, q_: _dot(a, q_, "tn") * scale, dsc, q)
    return dq, dkp, dkc, dvp, dvc, dsink


def _rwkv_chunk(s0, r, k, v, logw, a, k_k, k_a, r_k, gn_g, gn_b):
    c = r[0].shape[0]
    each = lambda f, *ls: [f(*xs) for xs in zip(*ls)]
    gram = functools.partial(_dot, prec=RW_PREC)
    row = lax.broadcasted_iota(jnp.int32, (c, c), 0)
    col = lax.broadcasted_iota(jnp.int32, (c, c), 1)
    incl, strict = row >= col, row > col
    ones_l = incl.astype(F32)

    def unit(x):
        return x / jnp.maximum(jnp.sqrt(jnp.sum(x * x, axis=-1, keepdims=True)), 1e-12)

    kk = each(lambda k_, p: unit(k_ * p), k, k_k)
    km = each(lambda k_, a_, p: k_ * (1.0 + (a_ - 1.0) * p), k, a, k_a)
    b = each(lambda x, a_: x * a_, kk, a)
    first_half = lax.broadcasted_iota(jnp.int32, (c, HEAD), 0) < c // 2
    mid = each(lambda w: jnp.sum(jnp.where(first_half, w, 0.0), axis=0, keepdims=True), logw)
    cum = each(lambda w, m: _dot(ones_l, w, hi=True) - m, logw, mid)
    alpha = each(lambda x, cu, w: x * jnp.exp(cu - w), kk, cum, logw)
    beta = each(lambda x, cu: x * jnp.exp(-cu), b, cum)
    kap = each(lambda x, cu: x * jnp.exp(-cu), km, cum)
    rho = each(lambda x, cu: x * jnp.exp(cu), r, cum)
    s0 = each(lambda s, m: s * jnp.exp(m), s0, mid)
    lab = each(lambda x, y_: jnp.where(strict, gram(x, y_, "nt"), 0.0), alpha, beta)
    lak = each(lambda x, y_: jnp.where(strict, gram(x, y_, "nt"), 0.0), alpha, kap)
    xs = each(lambda al, s, l, v_: _dot(al, s, "nt") + _dot(l, v_), alpha, s0, lak, v)
    xs = each(lambda x, l: x - _dot(l, x), xs, lab)
    lp, power = lab, 2
    while power < c:
        lp = each(lambda l: _dot(l, l), lp)
        xs = each(lambda x, l: x + _dot(l, x), xs, lp)
        power *= 2
    u = each(lambda x: -x, xs)
    mrb = each(lambda x, y_: jnp.where(incl, gram(x, y_, "nt"), 0.0), rho, beta)
    mrk = each(lambda x, y_: jnp.where(incl, gram(x, y_, "nt"), 0.0), rho, kap)
    y = each(lambda rh, s, mb, u_, mk, v_: _dot(rh, s, "nt") + _dot(mb, u_) + _dot(mk, v_), rho, s0, mrb, u, mrk, v)
    s1 = each(lambda s, u_, be, v_, ka, w, m: (s + _dot(u_, be, "tn") + _dot(v_, ka, "tn"))
              * jnp.exp(jnp.sum(w, axis=0, keepdims=True) - m), s0, u, beta, v, kap, logw, mid)

    def finish(y_, g, bias, r_, km_, rk, v_):
        mean = jnp.mean(y_, axis=-1, keepdims=True)
        var = jnp.mean(jnp.square(y_ - mean), axis=-1, keepdims=True)
        y_ = (y_ - mean) * lax.rsqrt(var + GN_EPS) * g + bias
        return y_ + jnp.sum(r_ * km_ * rk, axis=-1, keepdims=True) * v_

    return each(finish, y, gn_g, gn_b, r, km, r_k, v), s1


def _norm_fwd(name, x, g, shift, scale):
    return _rows(name, lambda x_, g_, sh, sc: _adaln(x_, g_, sh, sc), [x], [g, shift, scale], [(x.shape[1], BF16)], [], 256)[0]


def _norm_bwd(name, x, dh, dx_res, g, shift, scale, below=None):
    d = x.shape[1]

    def fn(x_, dh_, dr_, *rest):
        g_, sh, sc = rest[-4:-1] if below else rest
        _, vjp = jax.vjp(_adaln, x_, g_, sh, sc)
        dx, dg, dsh, dsc = vjp(dh_)
        dx = dx + dr_
        if not below:
            return dx, dg, dsh, dsc
        y_, gate_ = rest[0], rest[-1]
        return dx, gate_ * dx, dg, dsh, dsc, jnp.sum(dx * y_, axis=0, keepdims=True)

    if not below:
        return _rows(name, fn, [x, dh, dx_res], [g, shift, scale], [(d, F32)], [(1, d)] * 3, 256)
    return _rows(name, fn, [x, dh, dx_res, below[0]], [g, shift, scale, below[1]], [(d, F32), (d, BF16)], [(1, d)] * 4, 256)


def _sg_fwd(name, p, ln_g, ln_b, w_s, b_st):
    d = p.shape[1] // 3
    return _rows(name, _sg_mix, [p], [ln_g, ln_b, w_s, b_st], [(d, BF16)], [], SG_CHUNK)[0]


def _sg_bwd(name, p, dmix, ln_g, ln_b, w_s, b_st, jobs=()):
    def fn(p_, dm_, lg, lb, ws, bs):
        _, vjp = jax.vjp(_sg_mix, p_, lg, lb, ws, bs)
        return vjp(dm_)

    return _rows(name, fn, [p, dmix], [ln_g, ln_b, w_s, b_st], [(p.shape[1], BF16)],
                 [ln_g.shape, ln_b.shape, w_s.shape, b_st.shape], SG_CHUNK, jobs=jobs)


def _rope_tables(pos, inv_freq):
    ang = pos * inv_freq
    return jnp.cos(ang), jnp.sin(ang)


def _swa_pre(name, p, pos, inv_freq, d):
    kvw = SWA_KV * HEAD

    def fn(p_, pos_, fr):
        cos, sin = _rope_tables(pos_, fr)
        return (_rope(p_[:, :d], cos, sin, 1.0), _rope(p_[:, d:d + kvw], cos, sin, 1.0), p_[:, d + kvw:d + 2 * kvw])

    return _rows(name, fn, [p, pos], [inv_freq], [(d, BF16), (kvw, BF16), (kvw, BF16)], [], 256)


def _q_groups(ref, kv, rep):
    heads = _head_cols(ref, kv * rep)
    return [jnp.concatenate(heads[g * rep:(g + 1) * rep], axis=0) for g in range(kv)]


def _q_ungroup(groups, rep):
    return jnp.concatenate([g[h * SWA_BLOCK:(h + 1) * SWA_BLOCK] for g in groups for h in range(rep)], axis=1)


def _swa_attn_fwd(name, q, k, v, sinks, jobs=()):
    t, d = q.shape
    kv, rep = sinks.shape[0], sinks.shape[1]
    nb = t // SWA_BLOCK

    def body(q_ref, kp_ref, kc_ref, vp_ref, vc_ref, s_ref, o_ref):
        prev_bias = jnp.where(pl.program_id(0) > 0, 0.0, NEG).astype(F32)
        o = _attn_block(_q_groups(q_ref, kv, rep), *[_head_cols(ref, kv) for ref in (kp_ref, kc_ref, vp_ref, vc_ref)],
                        [s_ref[g] for g in range(kv)], prev_bias)
        o_ref[...] = _q_ungroup(o, rep)

    qs = pl.BlockSpec((SWA_BLOCK, d), lambda n: (n, 0))
    cur = pl.BlockSpec((SWA_BLOCK, kv * HEAD), lambda n: (n, 0))
    prev = pl.BlockSpec((SWA_BLOCK, kv * HEAD), lambda n: (jnp.maximum(n - 1, 0), 0))
    ss = pl.BlockSpec(sinks.shape, lambda n: (0, 0, 0, 0))
    return _pcall(body, grid=(nb,), in_specs=[qs, prev, cur, prev, cur, ss], out_specs=[qs], out_shape=[S((t, d), F32)],
                  name=name, semantics=("parallel",), inputs=(q, k, k, v, v, sinks), jobs=jobs)[0]


def _swa_attn_bwd(name, q, k, v, sinks, do, jobs=()):
    t, d = q.shape
    kv, rep = sinks.shape[0], sinks.shape[1]
    nb = t // SWA_BLOCK

    def body(q_ref, kp_ref, kc_ref, vp_ref, vc_ref, s_ref, do_ref, dq_ref, dkc_ref, dkp_ref, dvc_ref, dvp_ref, ds_ref):
        n = pl.program_id(0)
        prev_bias = jnp.where(n > 0, 0.0, NEG).astype(F32)
        args = [_q_groups(q_ref, kv, rep)] + [_head_cols(ref, kv) for ref in (kp_ref, kc_ref, vp_ref, vc_ref)]
        dq, dkp, dkc, dvp, dvc, ds = _attn_block_bwd(*args, [s_ref[g] for g in range(kv)], _q_groups(do_ref, kv, rep), prev_bias)
        dq_ref[...] = _q_ungroup(dq, rep)
        for ref, val in ((dkc_ref, dkc), (dkp_ref, dkp), (dvc_ref, dvc), (dvp_ref, dvp)):
            ref[...] = jnp.concatenate(val, axis=1)

        @pl.when(n == 0)
        def _():
            ds_ref[...] = jnp.zeros_like(ds_ref)

        for g in range(kv):
            ds_ref[g] += ds[g]

    qs = pl.BlockSpec((SWA_BLOCK, d), lambda n: (n, 0))
    cur = pl.BlockSpec((SWA_BLOCK, kv * HEAD), lambda n: (n, 0))
    prev = pl.BlockSpec((SWA_BLOCK, kv * HEAD), lambda n: (jnp.maximum(n - 1, 0), 0))
    ss = pl.BlockSpec(sinks.shape, lambda n: (0, 0, 0, 0))
    return _pcall(body, grid=(nb,), in_specs=[qs, prev, cur, prev, cur, ss, qs], out_specs=[qs, cur, cur, cur, cur, ss],
                  out_shape=[S((t, d), F32)] + [S((t, kv * HEAD), F32)] * 4 + [S(sinks.shape, F32)], name=name,
                  semantics=("arbitrary",), inputs=(q, k, k, v, v, sinks, do), jobs=jobs)


def _gate_fwd(name, o, z_src, z_off, d):
    return _rows(name, lambda o_, p_: o_ * jax.nn.silu(p_[:, z_off:z_off + d]), [o, z_src], [], [(d, BF16)], [], 256)[0]


def _gate_bwd(name, o, z_src, z_off, d, dmix):
    def fn(o_, p_, dm_):
        _, vjp = jax.vjp(lambda oo, zz: oo * jax.nn.silu(zz), o_, p_[:, z_off:z_off + d])
        return vjp(dm_)

    return _rows(name, fn, [o, z_src, dmix], [], [(d, F32), (d, F32)], [], 256)


def _swa_post_bwd(name, dq, dkc, dkp_up, dvc, dvp_up, dz, pos, inv_freq):
    def fn(dq_, dkc_, dkp_, dvc_, dvp_, dz_, pos_, fr):
        cos, sin = _rope_tables(pos_, fr)
        return jnp.concatenate([_rope(dq_, cos, sin, -1.0), _rope(dkc_ + dkp_, cos, sin, -1.0), dvc_ + dvp_, dz_], axis=1)

    n = dq.shape[1] + dkc.shape[1] + dvc.shape[1] + dz.shape[1]
    return _rows(name, fn, [dq, dkc, dkp_up, dvc, dvp_up, dz, pos], [inv_freq], [(n, BF16)], [], 256)[0]


HALO = 8


def _row_before(x, halo_ref, i):
    first = jnp.where(i > 0, halo_ref[pl.ds(HALO - 1, 1), :], 0.0)
    row = lax.broadcasted_iota(jnp.int32, x.shape, 0)
    return jnp.where(row == 0, first, pltpu.roll(x, 1, 0))


def _row_after(x, halo, i, n_tiles):
    last = jnp.where(i < n_tiles - 1, halo, 0.0)
    row = lax.broadcasted_iota(jnp.int32, x.shape, 0)
    return jnp.where(row == x.shape[0] - 1, last, pltpu.roll(x, x.shape[0] - 1, 0))


def _lerp_fwd(name, p, mu, widths):
    t, n = p.shape
    tm = 128

    def body(p_ref, halo_ref, mu_ref, *o_refs):
        x = p_ref[...]
        pm = x + (_row_before(x, halo_ref, pl.program_id(0)) - x) * mu_ref[...]
        o = 0
        for ref, w in zip(o_refs, widths):
            ref[...] = pm[:, o:o + w]
            o += w

    return pl.pallas_call(
        body, grid=(t // tm,),
        in_specs=[pl.BlockSpec((tm, n), lambda i: (i, 0)),
                  pl.BlockSpec((HALO, n), lambda i: (jnp.maximum(i * (tm // HALO) - 1, 0), 0)),
                  pl.BlockSpec((1, n), lambda i: (0, 0))],
        out_specs=[pl.BlockSpec((tm, w), lambda i: (i, 0)) for w in widths],
        out_shape=[S((t, w), F32) for w in widths], name=name, compiler_params=_params(("parallel",)))(p, p, mu)


def _lerp_bwd(name, dpm_parts, p, mu):
    t, n = p.shape
    k = len(dpm_parts)
    tm = 64
    n_tiles = t // tm

    def body(*refs):
        d_refs, dh_refs = refs[:k], refs[k:2 * k]
        p_ref, ph_ref, mu_ref, dp_ref, dmu_ref = refs[2 * k:]
        i = pl.program_id(0)
        cat = lambda vals: jnp.concatenate(vals, axis=1) if k > 1 else vals[0]
        dpm = cat([r[...] for r in d_refs])
        dnext = cat([r[pl.ds(0, 1), :] for r in dh_refs])
        x, mu_ = p_ref[...], mu_ref[...]
        dp_ref[...] = (dpm * (1.0 - mu_) + _row_after(dpm, dnext, i, n_tiles) * mu_).astype(dp_ref.dtype)
        dmu = jnp.sum(dpm * (_row_before(x, ph_ref, i) - x), axis=0, keepdims=True)

        @pl.when(i == 0)
        def _():
            dmu_ref[...] = dmu

        @pl.when(i > 0)
        def _():
            dmu_ref[...] += dmu

    per = tm // HALO
    d_specs = [pl.BlockSpec((tm, a.shape[1]), lambda i: (i, 0)) for a in dpm_parts]
    dh_specs = [pl.BlockSpec((HALO, a.shape[1]), lambda i: (jnp.minimum((i + 1) * per, t // HALO - 1), 0)) for a in dpm_parts]
    return pl.pallas_call(
        body, grid=(n_tiles,),
        in_specs=d_specs + dh_specs + [pl.BlockSpec((tm, n), lambda i: (i, 0)),
                                       pl.BlockSpec((HALO, n), lambda i: (jnp.maximum(i * per - 1, 0), 0)),
                                       pl.BlockSpec((1, n), lambda i: (0, 0))],
        out_specs=[pl.BlockSpec((tm, n), lambda i: (i, 0)), pl.BlockSpec((1, n), lambda i: (0, 0))],
        out_shape=[S((t, n), BF16), S((1, n), F32)], name=name,
        compiler_params=_params(("arbitrary",)))(*dpm_parts, *dpm_parts, p, p, mu)


def _lora_act(pl_, w0, w_lora, a0, a_lora):
    logw = -DECAY_SCALE * jax.nn.sigmoid(w0 + _dot(jnp.tanh(pl_), w_lora))
    a = jax.nn.sigmoid(a0 + _dot(pl_, a_lora))
    return logw, a


def _lora_fwd(name, pl_, w0, w_lora, a0, a_lora):
    d = w0.shape[1]
    return _rows(name, _lora_act, [pl_], [w0, w_lora, a0, a_lora], [(d, F32), (d, F32)], [], 256)


def _lora_bwd(name, pl_, dlogw, da, w0, w_lora, a0, a_lora):
    def fn(p_, dl_, da_, w0_, wl_, a0_, al_):
        _, vjp = jax.vjp(_lora_act, p_, w0_, wl_, a0_, al_)
        return vjp((dl_, da_))

    return _rows(name, fn, [pl_, dlogw, da], [w0, w_lora, a0, a_lora], [(pl_.shape[1], F32)],
                 [w0.shape, w_lora.shape, a0.shape, a_lora.shape], 256)


def _head_cols(ref, hb):
    x = ref[...].astype(F32)
    xo = pltpu.roll(x, x.shape[1] - HEAD, 1)
    return [(x if j % 2 == 0 else xo)[:, 2 * HEAD * (j // 2):2 * HEAD * (j // 2) + HEAD] for j in range(hb)]


def _rwkv_scan_fwd(name, r, k, v, logw, a, hp, jobs=()):
    t, d = r.shape
    h, nc, hb = d // HEAD, t // RW_CHUNK, RW_HEADS_FWD

    def body(r_ref, k_ref, v_ref, w_ref, a_ref, kk_ref, ka_ref, rk_ref, gg_ref, gb_ref, y_ref, st_ref, s_scr):
        @pl.when(pl.program_id(1) == 0)
        def _():
            s_scr[...] = jnp.zeros_like(s_scr)

        s0 = [s_scr[j] for j in range(hb)]
        for j in range(hb):
            st_ref[j, 0] = s0[j]
        y, s1 = _rwkv_chunk(s0, *[_head_cols(ref, hb) for ref in (r_ref, k_ref, v_ref, w_ref, a_ref, kk_ref, ka_ref, rk_ref,
                                                                 gg_ref, gb_ref)])
        y_ref[...] = jnp.concatenate(y, axis=1)
        for j in range(hb):
            s_scr[j] = s1[j]

    seq = pl.BlockSpec((RW_CHUNK, hb * HEAD), lambda i, n: (n, i))
    par = pl.BlockSpec((1, hb * HEAD), lambda i, n: (0, i))
    st = pl.BlockSpec((hb, 1, HEAD, HEAD), lambda i, n: (i, n, 0, 0))
    return _pcall(body, grid=(h // hb, nc), in_specs=[seq] * 5 + [par] * 5, out_specs=[seq, st],
                  out_shape=[S((t, d), F32), S((h, nc, HEAD, HEAD), F32)], scratch_shapes=[pltpu.VMEM((hb, HEAD, HEAD), F32)],
                  name=name, semantics=("parallel", "arbitrary"), inputs=(r, k, v, logw, a, *hp), jobs=jobs)


def _rwkv_scan_bwd(name, r, k, v, logw, a, hp, states, dy, jobs=()):
    t, d = r.shape
    h, nc, hb = d // HEAD, t // RW_CHUNK, RW_HEADS

    def body(r_ref, k_ref, v_ref, w_ref, a_ref, kk_ref, ka_ref, rk_ref, gg_ref, gb_ref, st_ref, dy_ref,
             dr_ref, dk_ref, dv_ref, dw_ref, da_ref, dkk_ref, dka_ref, drk_ref, dgg_ref, dgb_ref, ds_scr):
        n = pl.program_id(1)

        @pl.when(n == 0)
        def _():
            ds_scr[...] = jnp.zeros_like(ds_scr)
            for ref in (dkk_ref, dka_ref, drk_ref, dgg_ref, dgb_ref):
                ref[...] = jnp.zeros_like(ref)

        ins = [[st_ref[j, 0] for j in range(hb)]] + [_head_cols(ref, hb) for ref in (r_ref, k_ref, v_ref, w_ref, a_ref, kk_ref,
                                                                                  ka_ref, rk_ref, gg_ref, gb_ref)]
        _, vjp = jax.vjp(_rwkv_chunk, *ins)
        ds0, *dseq, dkk, dka, drk, dgg, dgb = vjp((_head_cols(dy_ref, hb), [ds_scr[j] for j in range(hb)]))
        for j in range(hb):
            ds_scr[j] = ds0[j]
        for ref, val in zip((dr_ref, dk_ref, dv_ref, dw_ref, da_ref), dseq):
            ref[...] = jnp.concatenate(val, axis=1)
        for ref, val in ((dkk_ref, dkk), (dka_ref, dka), (drk_ref, drk), (dgg_ref, dgg), (dgb_ref, dgb)):
            ref[...] += jnp.concatenate(val, axis=1)

    seq = pl.BlockSpec((RW_CHUNK, hb * HEAD), lambda i, n: (nc - 1 - n, i))
    par = pl.BlockSpec((1, hb * HEAD), lambda i, n: (0, i))
    st = pl.BlockSpec((hb, 1, HEAD, HEAD), lambda i, n: (i, nc - 1 - n, 0, 0))
    return _pcall(body, grid=(h // hb, nc), in_specs=[seq] * 5 + [par] * 5 + [st, seq], out_specs=[seq] * 5 + [par] * 5,
                  out_shape=[S((t, d), F32)] * 5 + [S((1, d), F32)] * 5, scratch_shapes=[pltpu.VMEM((hb, HEAD, HEAD), F32)],
                  name=name, semantics=("parallel", "arbitrary"), inputs=(r, k, v, logw, a, *hp, states, dy), jobs=jobs)


def _loss_head(name, x, target, g, y, gate):
    d = x.shape[1]

    def fn(x_, t_, y_, g_, gate_):
        def f(xx, gg):
            err = _rms(xx, gg) - t_
            return 0.5 * jnp.sum(jnp.mean(err * err, axis=-1, keepdims=True), axis=0, keepdims=True)

        l, vjp = jax.vjp(f, x_, g_)
        dx, dg = vjp(jnp.ones((1, 1), F32))
        return dx, gate_ * dx, dg, jnp.broadcast_to(l, (1, 128)), jnp.sum(dx * y_, axis=0, keepdims=True)

    return _rows(name, fn, [x, target, y], [g, gate], [(d, F32), (d, BF16)], [(1, d), (1, 128), (1, d)], 256)


def _mod_fwd(name, cond_all, mod_w, mod_b_cols):
    l, d, n = mod_w.shape

    def body(c_ref, w_ref, b_ref, o_ref):
        o_ref[0] = _dot(jax.nn.silu(c_ref[...]), w_ref[0], hi=True) + b_ref[0]

    return pl.pallas_call(body, grid=(l,), in_specs=[pl.BlockSpec((N_DEV, d), lambda i: (0, 0)),
                                                      pl.BlockSpec((1, d, n), lambda i: (i, 0, 0)),
                                                      pl.BlockSpec((1, 1, n), lambda i: (i, 0, 0))],
                          out_specs=pl.BlockSpec((1, N_DEV, n), lambda i: (i, 0, 0)), out_shape=S((l, N_DEV, n), F32),
                          name=name, compiler_params=_params(("parallel",)))(cond_all, mod_w, mod_b_cols)


def _mod_bwd(name, cond_all, dmod_cols, dmod_all):
    l, _, n = dmod_cols.shape
    d = cond_all.shape[1]
    nb = dmod_all.shape[2]

    def body(c_ref, dc_ref, da_ref, gw_ref, gb_ref):
        gw_ref[0] = _dot(jax.nn.silu(c_ref[...]), dc_ref[0], "tn", hi=True)
        acc = da_ref[0, 0:1, :]
        for bi in range(1, N_DEV):
            acc = acc + da_ref[0, bi:bi + 1, :]
        gb_ref[0] = acc

    return pl.pallas_call(body, grid=(l,), in_specs=[pl.BlockSpec((N_DEV, d), lambda i: (0, 0)),
                                                      pl.BlockSpec((1, N_DEV, n), lambda i: (i, 0, 0)),
                                                      pl.BlockSpec((1, N_DEV, nb), lambda i: (i, 0, 0))],
                          out_specs=[pl.BlockSpec((1, d, n), lambda i: (i, 0, 0)), pl.BlockSpec((1, 1, nb), lambda i: (i, 0, 0))],
                          out_shape=[S((l, d, n), F32), S((l, 1, nb), F32)], name=name,
                          compiler_params=_params(("parallel",)))(cond_all, dmod_cols, dmod_all)


def _shift_up(a, n=1):
    return jnp.concatenate([a[n:], jnp.zeros_like(a[:n])], axis=0)


def _cols_full(g):
    return g.transpose(1, 0, 2).reshape(g.shape[1], -1)


def _cols_parts(full):
    r, n = full.shape
    return full.reshape(r, N_DEV, n // N_DEV).transpose(1, 0, 2)


def _pack(arrs, mult=1024):
    flat = jnp.concatenate([a.reshape(-1) for a in arrs])
    pad = (-flat.shape[0]) % mult
    return jnp.pad(flat, (0, pad)).reshape(-1, 128)


def _unpack(flat, shapes):
    out, o = [], 0
    for s in shapes:
        n = math.prod(s)
        out.append(flat[o:o + n].reshape(s))
        o += n
    return out


def _local_step(x, pos, target, mods, norm_g, final_norm_g, layer_weights, hooks=None, on_grads=None):
    t, d = x.shape
    hooks = hooks or {}
    jobs = lambda nm: hooks.get(nm, ())
    notify = on_grads or (lambda *a: None)
    kinds = [i % 3 for i in range(DEPTH)]
    inv_freq = (ROPE_THETA ** (-jnp.arange(HEAD // 2, dtype=F32) / (HEAD // 2)))
    inv_freq = jnp.tile(inv_freq, 128 // (HEAD // 2)).reshape(1, 128)
    saved = []
    for i, kind in enumerate(kinds):
        lw = layer_weights(i)
        shift, scale, gate = (mods[i, q * d:(q + 1) * d].reshape(1, d) for q in range(3))
        g = norm_g[i].reshape(1, d)
        h = _norm_fwd(f"norm_fwd{i}", x, g, shift, scale)
        sv = dict(x=x, h=h, g=g, shift=shift, scale=scale, gate=gate, lw=lw)
        if kind == 0:
            p = _mm(f"sg_in{i}", h, lw["w_in"], "nn", F32, jobs=jobs(f"sg_in{i}"))
            mix = _sg_fwd(f"sg_mix{i}", p, lw["ln_g"], lw["ln_b"], lw["w_s"], lw["b_st"])
            sv.update(p=p)
        elif kind == 1:
            p = _mm(f"swa_in{i}", h, lw["w_in"], "nn", F32, jobs=jobs(f"swa_in{i}"))
            q, k, v = _swa_pre(f"swa_pre{i}", p, pos, inv_freq, d)
            o = _swa_attn_fwd(f"swa_attn{i}", q, k, v, lw["sinks"], jobs=jobs(f"swa_attn{i}"))
            mix = _gate_fwd(f"swa_gate{i}", o, p, d + 2 * SWA_KV * HEAD, d)
            sv.update(p=p, qkv=(q, k, v), o=o)
        else:
            pm = _mm(f"rw_in{i}", h, lw["w_main"], "nn", F32, jobs=jobs(f"rw_in{i}"))
            plo = _mm(f"rw_inl{i}", h, lw["w_lorain"], "nn", F32)
            r, k, v, z = _lerp_fwd(f"rw_lerp{i}", pm, lw["mu_main"], [d] * 4)
            (pll,) = _lerp_fwd(f"rw_lerpl{i}", plo, lw["mu_lora"], [LORA_PAD])
            logw, a = _lora_fwd(f"rw_lora{i}", pll, lw["w0"], lw["w_lora"], lw["a0"], lw["a_lora"])
            seqs = (r, k, v, logw, a)
            o, states = _rwkv_scan_fwd(f"rw_scan{i}", *seqs, lw["hp"], jobs=jobs(f"rw_scan{i}"))
            mix = _gate_fwd(f"rw_gate{i}", o, z, 0, d)
            sv.update(pm=pm, plo=plo, pll=pll, z=z, seqs=seqs, states=states, o=o)
        y, x = _mm(f"out{i}", mix, lw["w_out"], "nn", F32, resid=(x, gate), jobs=jobs(f"out{i}"))
        sv.update(mix=mix, y=y)
        saved.append(sv)

    top = saved[-1]
    dx, dy, d_final_g, loss, dgate = _loss_head("loss_head", x, target, final_norm_g.reshape(1, d), top["y"], top["gate"])

    grads = dict(norm_g=[None] * DEPTH, sg_w_in=[None] * 2, sg_w_out=[None] * 2, sg_ln_g=[None] * 2, sg_ln_b=[None] * 2,
                 sg_w_s=[None] * 2, sg_b_st=[None] * 2, final_norm_g=d_final_g)
    dmods = [None] * DEPTH
    for i in reversed(range(DEPTH)):
        kind, j, sv = kinds[i], i // 3, saved[i]
        lw = sv["lw"]
        d_w_out = _mm(f"out_dw{i}", sv["mix"], dy, "tn", BF16)
        if kind == 0:
            grads["sg_w_out"][j] = d_w_out
        else:
            grads[("swa_w_out", "rw_w_out")[kind - 1]] = d_w_out
        notify(i, "out", grads)
        dmix = _mm(f"out_dx{i}", dy, lw["w_out"], "nt", F32, jobs=jobs(f"out_dx{i}"))
        if kind == 0:
            dp, dlg, dlb, dws, dbs = _sg_bwd(f"sg_mix_bwd{i}", sv["p"], dmix, lw["ln_g"], lw["ln_b"], lw["w_s"], lw["b_st"],
                                            jobs=jobs(f"sg_mix_bwd{i}"))
            grads["sg_ln_g"][j], grads["sg_ln_b"][j], grads["sg_w_s"][j], grads["sg_b_st"][j] = dlg, dlb, dws, dbs
            grads["sg_w_in"][j] = _mm(f"sg_in_dw{i}", sv["h"], dp, "tn", BF16, jobs=jobs(f"sg_in_dw{i}"))
            notify(i, "in", grads)
            dh = _mm(f"sg_in_dx{i}", dp, lw["w_in"], "nt", F32, jobs=jobs(f"sg_in_dx{i}"))
        elif kind == 1:
            z_off = d + 2 * SWA_KV * HEAD
            do, dz = _gate_bwd(f"swa_gate_bwd{i}", sv["o"], sv["p"], z_off, d, dmix)
            dq, dkc, dkp, dvc, dvp, dsinks = _swa_attn_bwd(f"swa_attn_bwd{i}", *sv["qkv"], lw["sinks"], do,
                                                           jobs=jobs(f"swa_attn_bwd{i}"))
            dkp, dvp = _shift_up(dkp, SWA_BLOCK), _shift_up(dvp, SWA_BLOCK)
            dp = _swa_post_bwd(f"swa_post_bwd{i}", dq, dkc, dkp, dvc, dvp, dz, pos, inv_freq)
            grads.update(swa_sinks=dsinks, swa_w_out=d_w_out)
            grads["swa_w_in"] = _mm(f"swa_in_dw{i}", sv["h"], dp, "tn", BF16, jobs=jobs(f"swa_in_dw{i}"))
            dh = _mm(f"swa_in_dx{i}", dp, lw["w_in"], "nt", F32, jobs=jobs(f"swa_in_dx{i}"))
        else:
            do, dz = _gate_bwd(f"rw_gate_bwd{i}", sv["o"], sv["z"], 0, d, dmix)
            res = _rwkv_scan_bwd(f"rw_scan_bwd{i}", *sv["seqs"], lw["hp"], sv["states"], do, jobs=jobs(f"rw_scan_bwd{i}"))
            dr, dk, dv, dlogw, da = res[:5]
            dpll, dw0, dwl, da0, dal = _lora_bwd(f"rw_lora_bwd{i}", sv["pll"], dlogw, da, lw["w0"], lw["w_lora"],
                                                  lw["a0"], lw["a_lora"])
            dpm, dmu_main = _lerp_bwd(f"rw_lerp_bwd{i}", [dr, dk, dv, dz], sv["pm"], lw["mu_main"])
            dpl, dmu_lora = _lerp_bwd(f"rw_lerpl_bwd{i}", [dpll], sv["plo"], lw["mu_lora"])
            grads.update(rw_w_out=d_w_out, rw_hp=res[5:], rw_w0=dw0, rw_w_lora=dwl, rw_a0=da0, rw_a_lora=dal,
                         rw_mu_main=dmu_main, rw_mu_lora=dmu_lora)
            grads["rw_w_main"] = _mm(f"rw_in_dw{i}", sv["h"], dpm, "tn", BF16)
            grads["rw_w_lorain"] = _mm(f"rw_inl_dw{i}", sv["h"], dpl, "tn", BF16)
            dh = _mm(f"rw_inl_dx{i}", dpl, lw["w_lorain"], "nt", F32)
            dh = _mm(f"rw_in_dx{i}", dpm, lw["w_main"], "nt", F32, add=dh)
        if kind != 0:
            notify(i, "in", grads)
        dgate_here = dgate
        if i > 0:
            dx, dy, dg, dshift, dscale, dgate = _norm_bwd(f"norm_bwd{i}", sv["x"], dh, dx, sv["g"], sv["shift"], sv["scale"],
                                                          below=(saved[i - 1]["y"], saved[i - 1]["gate"]))
        else:
            dx, dg, dshift, dscale = _norm_bwd(f"norm_bwd{i}", sv["x"], dh, dx, sv["g"], sv["shift"], sv["scale"])
        grads["norm_g"][i] = dg
        dmods[i] = jnp.concatenate([dshift, dscale, dgate_here], axis=1)
    return loss, dx, jnp.concatenate(dmods, axis=0), grads


def kernel(x, c, positions, norm_g, mod_w, mod_b, final_norm_g, sg_w_in, sg_w_out, sg_ln_g, sg_ln_b, sg_w_spatial, sg_b_spatial, swa_w_in, swa_w_out, swa_sinks, rwkv_w_in, rwkv_w_out, rwkv_mu, rwkv_w0, rwkv_w_lora, rwkv_a0, rwkv_a_lora, rwkv_k_k, rwkv_k_a, rwkv_r_k, rwkv_gn_g, rwkv_gn_b, loss_target, m_norm_g, m_mod_w, m_mod_b, m_final_norm_g, m_sg_w_in, m_sg_w_out, m_sg_ln_g, m_sg_ln_b, m_sg_w_spatial, m_sg_b_spatial, m_swa_w_in, m_swa_w_out, m_swa_sinks, m_rwkv_w_in, m_rwkv_w_out, m_rwkv_mu, m_rwkv_w0, m_rwkv_w_lora, m_rwkv_a0, m_rwkv_a_lora, m_rwkv_k_k, m_rwkv_k_a, m_rwkv_r_k, m_rwkv_gn_g, m_rwkv_gn_b, v_norm_g, v_mod_w, v_mod_b, v_final_norm_g, v_sg_w_in, v_sg_w_out, v_sg_ln_g, v_sg_ln_b, v_sg_w_spatial, v_sg_b_spatial, v_swa_w_in, v_swa_w_out, v_swa_sinks, v_rwkv_w_in, v_rwkv_w_out, v_rwkv_mu, v_rwkv_w0, v_rwkv_w_lora, v_rwkv_a0, v_rwkv_a_lora, v_rwkv_k_k, v_rwkv_k_a, v_rwkv_r_k, v_rwkv_gn_g, v_rwkv_gn_b):
    weights = dict(norm_g=norm_g, mod_w=mod_w, mod_b=mod_b, final_norm_g=final_norm_g, sg_w_in=sg_w_in, sg_w_out=sg_w_out,
                   sg_ln_g=sg_ln_g, sg_ln_b=sg_ln_b, sg_w_spatial=sg_w_spatial, sg_b_spatial=sg_b_spatial, swa_w_in=swa_w_in,
                   swa_w_out=swa_w_out, swa_sinks=swa_sinks, rwkv_w_in=rwkv_w_in, rwkv_w_out=rwkv_w_out, rwkv_mu=rwkv_mu,
                   rwkv_w0=rwkv_w0, rwkv_w_lora=rwkv_w_lora, rwkv_a0=rwkv_a0, rwkv_a_lora=rwkv_a_lora, rwkv_k_k=rwkv_k_k,
                   rwkv_k_a=rwkv_k_a, rwkv_r_k=rwkv_r_k, rwkv_gn_g=rwkv_gn_g, rwkv_gn_b=rwkv_gn_b)
    mom_m = dict(norm_g=m_norm_g, mod_w=m_mod_w, mod_b=m_mod_b, final_norm_g=m_final_norm_g, sg_w_in=m_sg_w_in,
                 sg_w_out=m_sg_w_out, sg_ln_g=m_sg_ln_g, sg_ln_b=m_sg_ln_b, sg_w_spatial=m_sg_w_spatial,
                 sg_b_spatial=m_sg_b_spatial, swa_w_in=m_swa_w_in, swa_w_out=m_swa_w_out, swa_sinks=m_swa_sinks,
                 rwkv_w_in=m_rwkv_w_in, rwkv_w_out=m_rwkv_w_out, rwkv_mu=m_rwkv_mu, rwkv_w0=m_rwkv_w0,
                 rwkv_w_lora=m_rwkv_w_lora, rwkv_a0=m_rwkv_a0, rwkv_a_lora=m_rwkv_a_lora, rwkv_k_k=m_rwkv_k_k,
                 rwkv_k_a=m_rwkv_k_a, rwkv_r_k=m_rwkv_r_k, rwkv_gn_g=m_rwkv_gn_g, rwkv_gn_b=m_rwkv_gn_b)
    mom_v = dict(norm_g=v_norm_g, mod_w=v_mod_w, mod_b=v_mod_b, final_norm_g=v_final_norm_g, sg_w_in=v_sg_w_in,
                 sg_w_out=v_sg_w_out, sg_ln_g=v_sg_ln_g, sg_ln_b=v_sg_ln_b, sg_w_spatial=v_sg_w_spatial,
                 sg_b_spatial=v_sg_b_spatial, swa_w_in=v_swa_w_in, swa_w_out=v_swa_w_out, swa_sinks=v_swa_sinks,
                 rwkv_w_in=v_rwkv_w_in, rwkv_w_out=v_rwkv_w_out, rwkv_mu=v_rwkv_mu, rwkv_w0=v_rwkv_w0,
                 rwkv_w_lora=v_rwkv_w_lora, rwkv_a0=v_rwkv_a0, rwkv_a_lora=v_rwkv_a_lora, rwkv_k_k=v_rwkv_k_k,
                 rwkv_k_a=v_rwkv_k_a, rwkv_r_k=v_rwkv_r_k, rwkv_gn_g=v_rwkv_gn_g, rwkv_gn_b=v_rwkv_gn_b)
    names = list(weights)
    t, d = x.shape[1], x.shape[2]
    me = 4 * lax.axis_index("x") + 2 * lax.axis_index("y") + lax.axis_index("c")
    n_mod = mod_w.shape[2]
    n_rw = rwkv_w_in.shape[2]

    small_names = ["sg_ln_g", "sg_ln_b", "rwkv_mu", "rwkv_w0", "rwkv_a0", "rwkv_k_k", "rwkv_k_a", "rwkv_gn_g", "rwkv_gn_b",
                   "rwkv_w_lora", "rwkv_a_lora"]
    small_shapes = [weights[n].shape for n in small_names]
    pk = _pack([c] + [weights[n] for n in small_names])
    gathered = _gather("gather_small", pk).reshape(N_DEV, -1)
    c_all = gathered[:, :d]
    per_dev = [_unpack(gathered[dv, d:], small_shapes) for dv in range(N_DEV)]
    full_small = {}
    for q, n in enumerate(small_names):
        full_small[n] = jnp.concatenate([per_dev[dv][q] for dv in range(N_DEV)], axis=-1)

    mod_b_cols = lax.dynamic_slice_in_dim(mod_b, me * n_mod, n_mod, axis=1).reshape(DEPTH, 1, n_mod)
    mod_part = _mod_fwd("mod_fwd", c_all, mod_w, mod_b_cols)
    mod_g = _gather("gather_mod", mod_part.reshape(DEPTH * N_DEV, n_mod))
    mod_g = mod_g.reshape(N_DEV, DEPTH, N_DEV, n_mod)
    mods = lax.dynamic_index_in_dim(mod_g, me, axis=2, keepdims=False)
    mods = mods.transpose(1, 0, 2).reshape(DEPTH, N_DEV * n_mod)

    job = lambda cls, src: dict(cls=cls, src=src)
    half = d // 2
    gj = dict(swa_in=job(_Gather, swa_w_in[0].astype(BF16)), swa_out=job(_Gather, swa_w_out[0].astype(BF16)),
              rw_in_a=job(_Gather, rwkv_w_in[0, :half].astype(BF16)), rw_in_b=job(_Gather, rwkv_w_in[0, half:].astype(BF16)),
              rw_out=job(_Gather, rwkv_w_out[0].astype(BF16)),
              sg_in1=job(_Gather, sg_w_in[1].astype(BF16)), sg_out1=job(_Gather, sg_w_out[1].astype(BF16)))
    hooks = {"sg_in0": [gj["swa_in"]], "out0": [gj["swa_out"]], "swa_in1": [gj["rw_in_a"]], "swa_attn1": [gj["rw_in_b"]],
             "out1": [gj["rw_out"]], "rw_scan2": [gj["sg_in1"], gj["sg_out1"]]}
    g_sg_in0 = _gather("gather_sg_in0", sg_w_in[0].astype(BF16))
    g_sg_out0 = _gather("gather_sg_out0", sg_w_out[0].astype(BF16))
    lora_rows = lambda w, off: jnp.zeros((LORA_PAD, d), F32).at[off:off + LORA].set(w)
    mu = full_small["rwkv_mu"].reshape(1, -1)
    heads = lambda a: a.reshape(1, -1)

    def layer_weights(i):
        if i % 3 == 0:
            j = i // 3
            g_in, g_out = (g_sg_in0, g_sg_out0) if j == 0 else (gj["sg_in1"]["out"], gj["sg_out1"]["out"])
            return dict(w_in=_cols_full(g_in), w_out=g_out.reshape(d, d), ln_g=full_small["sg_ln_g"][j].reshape(1, d),
                        ln_b=full_small["sg_ln_b"][j].reshape(1, d), w_s=sg_w_spatial[j], b_st=sg_b_spatial[j].T)
        if i % 3 == 1:
            return dict(w_in=_cols_full(gj["swa_in"]["out"]), w_out=gj["swa_out"]["out"].reshape(d, d),
                        sinks=swa_sinks.reshape(SWA_KV, SWA_REP, 1, 1))
        rw_in_full = jnp.concatenate([_cols_full(gj["rw_in_a"]["out"]), _cols_full(gj["rw_in_b"]["out"])], axis=0)
        return dict(w_main=rw_in_full[:, :4 * d], w_lorain=jnp.pad(rw_in_full[:, 4 * d:], ((0, 0), (0, LORA_PAD - 2 * LORA))),
                    w_out=gj["rw_out"]["out"].reshape(d, d), mu_main=mu[:, :4 * d],
                    mu_lora=jnp.pad(mu[:, 4 * d:], ((0, 0), (0, LORA_PAD - 2 * LORA))),
                    w0=full_small["rwkv_w0"], a0=full_small["rwkv_a0"],
                    w_lora=lora_rows(full_small["rwkv_w_lora"][0], 0), a_lora=lora_rows(full_small["rwkv_a_lora"][0], LORA),
                    hp=[heads(full_small["rwkv_k_k"]), heads(full_small["rwkv_k_a"]), heads(rwkv_r_k),
                        heads(full_small["rwkv_gn_g"]), heads(full_small["rwkv_gn_b"])])

    sj = {}

    def on_grads(i, which, g):
        def stage(nm, parts, host):
            sj[nm] = job(_Chips, _scatter_pairs("scatter_" + nm, parts.astype(BF16)))
            hooks.setdefault(host, []).append(sj[nm])

        rows_of = lambda a: a.reshape(N_DEV, -1, d)
        if (i, which) == (3, "out"):
            stage("sg_out1", rows_of(g["sg_w_out"][1]), "rw_scan_bwd2")
        elif (i, which) == (3, "in"):
            stage("sg_in1", _cols_parts(g["sg_w_in"][1]), "rw_scan_bwd2")
        elif (i, which) == (2, "out"):
            stage("rw_out", rows_of(g["rw_w_out"]), "swa_in_dx1")
        elif (i, which) == (2, "in"):
            d_rw_in = jnp.concatenate([g["rw_w_main"], g["rw_w_lorain"][:, :2 * LORA]], axis=1)
            stage("rw_in_a", _cols_parts(d_rw_in[:half]), "swa_attn_bwd1")
            stage("rw_in_b", _cols_parts(d_rw_in[half:]), "swa_in_dw1")
        elif (i, which) == (1, "out"):
            stage("swa_out", rows_of(g["swa_w_out"]), "out_dx0")
        elif (i, which) == (1, "in"):
            stage("swa_in", _cols_parts(g["swa_w_in"]), "sg_mix_bwd0")
        elif (i, which) == (0, "out"):
            stage("sg_out0", rows_of(g["sg_w_out"][0]), "sg_in_dw0")
        else:
            stage("sg_in0", _cols_parts(g["sg_w_in"][0]), "sg_in_dx0")

    loss, dx, dmods, g = _local_step(x[0], positions.reshape(t, 1).astype(F32), loss_target[0], mods, norm_g, final_norm_g,
                                     layer_weights, hooks, on_grads)

    dmod_g = _gather("gather_dmod", dmods)
    dmod_all = dmod_g.transpose(1, 0, 2)
    dmod_cols = lax.dynamic_slice_in_dim(dmod_all, me * n_mod, n_mod, axis=2)
    g_mod_w, g_mod_b = _mod_bwd("mod_bwd", c_all, dmod_cols, dmod_all)

    d_b_sp = [g["sg_b_st"][j].T for j in range(2)]
    rep = [loss[0, :1], jnp.concatenate(g["norm_g"], axis=0), g["final_norm_g"], jnp.stack(g["sg_w_s"]), jnp.stack(d_b_sp),
           g["swa_sinks"], g["rw_hp"][2]]
    rep_shapes = [(1,), norm_g.shape, final_norm_g.shape, sg_w_spatial.shape, sg_b_spatial.shape, swa_sinks.shape, rwkv_r_k.shape]
    rep_sum = _sum_parts("sum_rep", _gather("gather_rep", _pack(rep, 128 * 256))).reshape(-1)
    loss_tot, g_norm_g, g_final, g_w_sp, g_b_sp, g_sinks, g_r_k = _unpack(rep_sum, rep_shapes)

    p_sg_in = jnp.concatenate([sj["sg_in0"]["out"], sj["sg_in1"]["out"]], axis=1)
    p_sg_out = jnp.concatenate([sj["sg_out0"]["out"], sj["sg_out1"]["out"]], axis=1)
    p_swa_in, p_swa_out, p_rw_out = (sj[nm]["out"] for nm in ("swa_in", "swa_out", "rw_out"))
    p_rw_in = jnp.concatenate([sj["rw_in_a"]["out"], sj["rw_in_b"]["out"]], axis=1)
    d_mu = jnp.concatenate([g["rw_mu_main"], g["rw_mu_lora"][:, :2 * LORA]], axis=1)
    hp_flat = lambda a: a.reshape(1, -1)
    small_grads = dict(sg_ln_g=jnp.concatenate(g["sg_ln_g"], axis=0), sg_ln_b=jnp.concatenate(g["sg_ln_b"], axis=0), rwkv_mu=d_mu,
                       rwkv_w0=g["rw_w0"], rwkv_a0=g["rw_a0"], rwkv_k_k=hp_flat(g["rw_hp"][0]), rwkv_k_a=hp_flat(g["rw_hp"][1]),
                       rwkv_gn_g=hp_flat(g["rw_hp"][3]), rwkv_gn_b=hp_flat(g["rw_hp"][4]),
                       rwkv_w_lora=g["rw_w_lora"][None, :LORA], rwkv_a_lora=g["rw_a_lora"][None, LORA:2 * LORA])
    per_dest = []
    for dv in range(N_DEV):
        shards = []
        for n in small_names:
            full, w = small_grads[n], weights[n].shape[-1]
            shards.append(full[..., dv * w:(dv + 1) * w])
        per_dest.append(_pack(shards))
    small_parts = _exchange("scatter_small", jnp.stack(per_dest), True)

    out_g, out_d, out_m, out_v = {}, {}, {}, {}

    def update(name, grad, shape2d, jobs=()):
        w2, m2, v2 = (a[name].reshape(shape2d) for a in (weights, mom_m, mom_v))
        gg, dd, mm, vv = _adamw("adamw_" + name, w2, grad, m2, v2, jobs=jobs)
        shp = weights[name].shape
        out_g[name], out_d[name], out_m[name], out_v[name] = gg.reshape(shp), dd.reshape(shp), mm.reshape(shp), vv.reshape(shp)

    update("mod_w", g_mod_w.reshape(-1, n_mod), (-1, n_mod))
    update("sg_w_in", p_sg_in, (-1, sg_w_in.shape[2]))
    update("sg_w_out", p_sg_out, (-1, d))
    update("swa_w_in", p_swa_in, (-1, swa_w_in.shape[2]))
    update("swa_w_out", p_swa_out, (-1, d))
    update("rwkv_w_in", p_rw_in, (-1, n_rw))
    update("rwkv_w_out", p_rw_out, (-1, d))
    update("sg_w_spatial", g_w_sp.reshape(-1, 128), (-1, 128))
    w_pk, m_pk, v_pk = (_pack([a[n] for n in small_names]) for a in (weights, mom_m, mom_v))
    res = _adamw("adamw_small", w_pk, small_parts, m_pk, v_pk)
    for q, arrs in enumerate(zip(*[_unpack(r_.reshape(-1), small_shapes) for r_ in res])):
        out_g[small_names[q]], out_d[small_names[q]], out_m[small_names[q]], out_v[small_names[q]] = arrs
    rep_names = ["norm_g", "mod_b", "final_norm_g", "sg_b_spatial", "swa_sinks", "rwkv_r_k"]
    rep_grads = [g_norm_g, g_mod_b.reshape(mod_b.shape), g_final, g_b_sp, g_sinks, g_r_k]
    rep_shapes2 = [weights[n].shape for n in rep_names]
    w_pk, m_pk, v_pk = (_pack([a[n] for n in rep_names]) for a in (weights, mom_m, mom_v))
    res = _adamw("adamw_rep", w_pk, _pack(rep_grads), m_pk, v_pk)
    for q, arrs in enumerate(zip(*[_unpack(r_.reshape(-1), rep_shapes2) for r_ in res])):
        out_g[rep_names[q]], out_d[rep_names[q]], out_m[rep_names[q]], out_v[rep_names[q]] = arrs

    return (loss_tot.reshape(()), dx[None], *[out_g[n] for n in names], *[out_d[n] for n in names],
            *[out_m[n] for n in names], *[out_v[n] for n in names])
```

```python
import functools
import math

import jax
import jax.numpy as jnp
from jax import lax
from jax.experimental import pallas as pl
from jax.experimental.pallas import tpu as pltpu

F32, BF16 = jnp.float32, jnp.bfloat16
HI = lax.Precision.HIGHEST
S = jax.ShapeDtypeStruct
MESH = pl.DeviceIdType.MESH

N_DEV = 8
DEPTH = 4
HEAD = 64
SG_GROUPS = 16
SG_CHUNK = 128
SWA_BLOCK = 128
SWA_KV = 4
SWA_REP = 8
ROPE_THETA = 10000.0
LORA = 96
LORA_PAD = 256
RW_CHUNK = 64
RW_HEADS = 16
RW_HEADS_FWD = 32
RW_PREC = lax.Precision.HIGH
DECAY_SCALE = math.exp(-0.5)
GN_EPS = 64e-5
RMS_EPS = 1e-6
LN_EPS = 1e-5
NEG = -1e30
ADAM_LR, ADAM_B1, ADAM_B2, ADAM_EPS, ADAM_WD, ADAM_STEP = 0.001, 0.9, 0.999, 1e-08, 0.01, 10
VMEM_MB = 56


def _params(sem=None):
    kw = dict(vmem_limit_bytes=VMEM_MB << 20)
    if sem is not None:
        kw["dimension_semantics"] = sem
    return pltpu.CompilerParams(**kw)


def _pick(n, opts):
    for o in opts:
        if n % o == 0:
            return o
    raise ValueError(f"no tile for {n}")


def _rows(name, fn, rows, consts, out_rows, out_accs, tm, jobs=()):
    t = rows[0].shape[0]
    nr, nc, no = len(rows), len(consts), len(out_rows)

    def body(*refs):
        outs = fn(*[r[...] for r in refs[:nr + nc]])
        if not isinstance(outs, (tuple, list)):
            outs = (outs,)
        for r, o in zip(refs[nr + nc:nr + nc + no], outs[:no]):
            r[...] = o.astype(r.dtype)
        i = pl.program_id(0)
        for r, o in zip(refs[nr + nc + no:], outs[no:]):
            @pl.when(i == 0)
            def _(r=r, o=o):
                r[...] = o.astype(r.dtype)

            @pl.when(i > 0)
            def _(r=r, o=o):
                r[...] += o.astype(r.dtype)

    in_specs = [pl.BlockSpec((tm, a.shape[1]), lambda i: (i, 0)) for a in rows]
    in_specs += [pl.BlockSpec(c.shape, lambda i, nd=c.ndim: (0,) * nd) for c in consts]
    out_specs = [pl.BlockSpec((tm, n), lambda i: (i, 0)) for n, _ in out_rows]
    out_specs += [pl.BlockSpec(s, lambda i, nd=len(s): (0,) * nd) for s in out_accs]
    out_shape = [S((t, n), dt) for n, dt in out_rows] + [S(s, F32) for s in out_accs]
    return _pcall(body, grid=(t // tm,), in_specs=in_specs, out_specs=out_specs, out_shape=out_shape, name=name,
                  semantics=("arbitrary",), inputs=(*rows, *consts), jobs=jobs)


_DN = {"nn": (((1,), (0,)), ((), ())), "nt": (((1,), (1,)), ((), ())), "tn": (((0,), (0,)), ((), ()))}


def _mm(name, a, b, mode, out_dtype, add=None, resid=None, jobs=()):
    if mode == "nn":
        (m, k), (_, n) = a.shape, b.shape
    elif mode == "nt":
        (m, k), (n, _) = a.shape, b.shape
    else:
        (k, m), (_, n) = a.shape, b.shape
    wide = mode != "tn" and add is None and resid is None
    tm = _pick(m, (1024, 512, 256, 128))
    tn = _pick(n, ((1536,) if wide and n % 1024 else ()) + (1024, 512, 384, 256, 128))
    long_k = (4096,) if mode == "tn" else (3072, 2304) if mode == "nt" and add is None else ()
    tk = _pick(k, long_k + (2048, 1536, 1024, 512, 384, 256, 128))
    nk = k // tk
    n_extra = (add is not None) + 2 * (resid is not None)

    def body(*refs):
        a_ref, b_ref = refs[0], refs[1]
        extra, outs, acc = refs[2:2 + n_extra], refs[2 + n_extra:-1], refs[-1]
        kk = pl.program_id(2)
        prod = lax.dot_general(a_ref[...].astype(BF16), b_ref[...].astype(BF16), _DN[mode], preferred_element_type=F32)
        if add is not None:
            prod = jnp.where(kk == 0, prod + extra[0][...].astype(F32), prod) if nk > 1 else prod + extra[0][...].astype(F32)

        def finish(total):
            outs[0][...] = total.astype(outs[0].dtype)
            if resid is not None:
                outs[1][...] = extra[-2][...] + extra[-1][...] * total

        if nk == 1:
            finish(prod)
            return

        @pl.when(kk == 0)
        def _():
            acc[...] = prod

        @pl.when(kk > 0)
        def _():
            acc[...] += prod

        @pl.when(kk == nk - 1)
        def _():
            finish(acc[...])

    a_spec = pl.BlockSpec((tk, tm), lambda i, j, q: (q, i)) if mode == "tn" else pl.BlockSpec((tm, tk), lambda i, j, q: (i, q))
    b_spec = pl.BlockSpec((tn, tk), lambda i, j, q: (j, q)) if mode == "nt" else pl.BlockSpec((tk, tn), lambda i, j, q: (q, j))
    o_spec = pl.BlockSpec((tm, tn), lambda i, j, q: (i, j))
    ins, specs, out_specs, out_shape = [a, b], [a_spec, b_spec], [o_spec], [S((m, n), out_dtype)]
    if add is not None:
        ins.append(add)
        specs.append(o_spec)
    if resid is not None:
        ins += list(resid)
        specs += [o_spec, pl.BlockSpec((1, tn), lambda i, j, q: (0, j))]
        out_specs.append(o_spec)
        out_shape.append(S((m, n), F32))
    res = _pcall(body, grid=(m // tm, n // tn, nk), in_specs=specs, out_specs=out_specs, out_shape=out_shape,
                 scratch_shapes=[pltpu.VMEM((tm, tn), F32)], name=name, semantics=("parallel", "parallel", "arbitrary"),
                 inputs=ins, jobs=jobs)
    return res if resid is not None else res[0]


def _exchange(name, src, scatter):
    blk = src.shape[1:] if scatter else src.shape

    def body(src_ref, dst_ref, send_sems, recv_sems, loc_sem):
        x, y, c = lax.axis_index("x"), lax.axis_index("y"), lax.axis_index("c")
        me = 4 * x + 2 * y + c

        def mine(d):
            return src_ref.at[d] if scatter else src_ref

        local = pltpu.make_async_copy(mine(me), dst_ref.at[me], loc_sem)
        local.start()
        sends, peers = [], []
        for k in range(1, N_DEV):
            px = 1 - x if k & 4 else x
            py = 1 - y if k & 2 else y
            pc = 1 - c if k & 1 else c
            pid = 4 * px + 2 * py + pc
            cp = pltpu.make_async_remote_copy(src_ref=mine(pid), dst_ref=dst_ref.at[me], send_sem=send_sems.at[k - 1],
                                              recv_sem=recv_sems.at[k - 1], device_id=(px, py, pc), device_id_type=MESH)
            cp.start()
            sends.append(cp)
            peers.append((pid, (px, py, pc)))
        for k in range(1, N_DEV):
            pid, dev = peers[k - 1]
            pltpu.make_async_remote_copy(src_ref=mine(pid), dst_ref=dst_ref.at[pid], send_sem=send_sems.at[k - 1],
                                         recv_sem=recv_sems.at[k - 1], device_id=dev, device_id_type=MESH).wait_recv()
        for cp in sends:
            cp.wait_send()
        local.wait()

    return pl.pallas_call(
        body, out_shape=S((N_DEV,) + tuple(blk), src.dtype),
        in_specs=[pl.BlockSpec(memory_space=pl.ANY)], out_specs=pl.BlockSpec(memory_space=pl.ANY),
        scratch_shapes=[pltpu.SemaphoreType.DMA((N_DEV - 1,)), pltpu.SemaphoreType.DMA((N_DEV - 1,)),
                        pltpu.SemaphoreType.DMA],
        name=name)(src)


class _Gather:
    @staticmethod
    def out_shape(src):
        return S((N_DEV,) + tuple(src.shape), src.dtype)

    scratch = (pltpu.SemaphoreType.DMA((N_DEV - 1,)), pltpu.SemaphoreType.DMA((N_DEV - 1,)), pltpu.SemaphoreType.DMA)

    def __init__(self, src_ref, dst_ref, send_sems, recv_sems, loc_sem):
        self.refs = (src_ref, dst_ref, send_sems, recv_sems, loc_sem)
        x, y, c = lax.axis_index("x"), lax.axis_index("y"), lax.axis_index("c")
        self.c, self.me, self.sibling = c, (x, y, c), (x, y, 1 - c)
        self.chips = [(1 - x, y), (x, 1 - y), (1 - x, 1 - y)]

    def rows(self, px, py, pc):
        return self.refs[1].at[4 * px + 2 * py + pc]

    def copy(self, k, block, to, own=False):
        src_ref, _, send_sems, recv_sems, _ = self.refs
        return pltpu.make_async_remote_copy(src_ref=src_ref if own else self.rows(*block), dst_ref=self.rows(*block),
                                            send_sem=send_sems.at[k], recv_sem=recv_sems.at[k], device_id=to,
                                            device_id_type=MESH)

    def local(self):
        return pltpu.make_async_copy(self.refs[0], self.rows(*self.me), self.refs[4])

    def first(self):
        return [self.copy(0, self.me, self.sibling, own=True)] + [self.copy(1 + j, self.me, (*chip, self.c), own=True)
                                                                  for j, chip in enumerate(self.chips)]

    def start(self):
        self.local().start()
        for cp in self.first():
            cp.start()

    def finish(self):
        c = self.c
        passed = [self.copy(4 + j, (*chip, c), self.sibling) for j, chip in enumerate(self.chips)]
        for j, chip in enumerate(self.chips):
            self.copy(1 + j, (*chip, c), self.me).wait_recv()
            passed[j].start()
        self.copy(0, self.sibling, self.me).wait_recv()
        for j, chip in enumerate(self.chips):
            self.copy(4 + j, (*chip, 1 - c), self.me).wait_recv()
        for cp in self.first() + passed:
            cp.wait_send()
        self.local().wait()


class _Chips:
    @staticmethod
    def out_shape(src):
        return S(src.shape, src.dtype)

    scratch = (pltpu.SemaphoreType.DMA((N_DEV // 2 - 1,)), pltpu.SemaphoreType.DMA((N_DEV // 2 - 1,)), pltpu.SemaphoreType.DMA)

    def __init__(self, src_ref, dst_ref, send_sems, recv_sems, loc_sem):
        self.refs = (src_ref, dst_ref, send_sems, recv_sems, loc_sem)
        x, y, c = lax.axis_index("x"), lax.axis_index("y"), lax.axis_index("c")
        self.c, self.mine = c, 2 * x + y
        self.chips = [(1 - x, y), (x, 1 - y), (1 - x, 1 - y)]

    def local(self):
        src_ref, dst_ref, _, _, loc_sem = self.refs
        return pltpu.make_async_copy(src_ref.at[self.mine], dst_ref.at[self.mine], loc_sem)

    def send(self, j):
        src_ref, dst_ref, send_sems, recv_sems, _ = self.refs
        px, py = self.chips[j]
        return pltpu.make_async_remote_copy(src_ref=src_ref.at[2 * px + py], dst_ref=dst_ref.at[self.mine],
                                            send_sem=send_sems.at[j], recv_sem=recv_sems.at[j],
                                            device_id=(px, py, self.c), device_id_type=MESH)

    def arrival(self, j):
        src_ref, dst_ref, send_sems, recv_sems, _ = self.refs
        px, py = self.chips[j]
        return pltpu.make_async_remote_copy(src_ref=src_ref.at[self.mine], dst_ref=dst_ref.at[2 * px + py],
                                            send_sem=send_sems.at[j], recv_sem=recv_sems.at[j],
                                            device_id=(px, py, self.c), device_id_type=MESH)

    def start(self):
        self.local().start()
        for j in range(len(self.chips)):
            self.send(j).start()

    def finish(self):
        for j in range(len(self.chips)):
            self.arrival(j).wait_recv()
        for j in range(len(self.chips)):
            self.send(j).wait_send()
        self.local().wait()


def _exchange_call(name, cls, src):
    def body(*refs):
        ex = cls(*refs)
        ex.start()
        ex.finish()

    return pl.pallas_call(body, out_shape=cls.out_shape(src), in_specs=[pl.BlockSpec(memory_space=pl.ANY)],
                          out_specs=pl.BlockSpec(memory_space=pl.ANY), scratch_shapes=list(cls.scratch), name=name)(src)


def _gather(name, src):
    return _exchange_call(name, _Gather, src)


def _pcall(body, *, grid, in_specs, out_specs, out_shape, scratch_shapes=(), name, semantics, inputs, jobs=()):
    if not jobs:
        return pl.pallas_call(body, grid=grid, in_specs=in_specs, out_specs=out_specs, out_shape=out_shape,
                              scratch_shapes=list(scratch_shapes), name=name, compiler_params=_params(semantics))(*inputs)
    n_in, n_out, n_scr, nj = len(in_specs), len(out_specs), len(scratch_shapes), len(jobs)

    def hosted(*refs):
        ins, srcs = refs[:n_in], refs[n_in:n_in + nj]
        outs, dsts = refs[n_in + nj:n_in + nj + n_out], refs[n_in + nj + n_out:n_in + 2 * nj + n_out]
        scr, sems = refs[n_in + 2 * nj + n_out:n_in + 2 * nj + n_out + n_scr], refs[n_in + 2 * nj + n_out + n_scr:]
        ids = [pl.program_id(q) for q in range(len(grid))]
        first = functools.reduce(jnp.logical_and, [i == 0 for i in ids])
        last = functools.reduce(jnp.logical_and, [i == g - 1 for i, g in zip(ids, grid)])
        make = lambda q: jobs[q]["cls"](srcs[q], dsts[q], *sems[3 * q:3 * q + 3])

        @pl.when(first)
        def _():
            for q in range(nj):
                make(q).start()

        body(*ins, *outs, *scr)

        @pl.when(last)
        def _():
            for q in range(nj):
                make(q).finish()

    anyspec = pl.BlockSpec(memory_space=pl.ANY)
    res = pl.pallas_call(
        hosted, grid=grid, in_specs=list(in_specs) + [anyspec] * nj, out_specs=list(out_specs) + [anyspec] * nj,
        out_shape=list(out_shape) + [j["cls"].out_shape(j["src"]) for j in jobs],
        scratch_shapes=list(scratch_shapes) + [s for j in jobs for s in j["cls"].scratch], name=name,
        compiler_params=_params(("arbitrary",) * len(grid)))(*inputs, *[j["src"] for j in jobs])
    for j, out in zip(jobs, res[n_out:]):
        j["out"] = out
    return res[:n_out]


def _scatter_pairs(name, parts):
    _, r, c_ = parts.shape
    n_chip = N_DEV // 2

    def stage1(src_ref, dst_ref, send_sems, recv_sems):
        x, y, c = lax.axis_index("x"), lax.axis_index("y"), lax.axis_index("c")
        sends = []
        for q in range(n_chip):
            cp = pltpu.make_async_remote_copy(src_ref=src_ref.at[2 * q + 1 - c], dst_ref=dst_ref.at[q],
                                              send_sem=send_sems.at[q], recv_sem=recv_sems.at[q],
                                              device_id=(x, y, 1 - c), device_id_type=MESH)
            cp.start()
            sends.append(cp)
        for q in range(n_chip):
            pltpu.make_async_remote_copy(src_ref=src_ref.at[2 * q + c], dst_ref=dst_ref.at[q], send_sem=send_sems.at[q],
                                         recv_sem=recv_sems.at[q], device_id=(x, y, 1 - c), device_id_type=MESH).wait_recv()
        for cp in sends:
            cp.wait_send()

    from_sibling = pl.pallas_call(
        stage1, out_shape=S((n_chip, r, c_), parts.dtype),
        in_specs=[pl.BlockSpec(memory_space=pl.ANY)], out_specs=pl.BlockSpec(memory_space=pl.ANY),
        scratch_shapes=[pltpu.SemaphoreType.DMA((n_chip,)), pltpu.SemaphoreType.DMA((n_chip,))], name=name + "_pair")(parts)

    tm = _pick(r, (512, 256, 128, 64, 32, 16, 8)) if r % 8 == 0 else r
    core = lax.axis_index("c").astype(jnp.int32).reshape(1)

    def pair_sum(core_ref, mine_ref, sib_ref, o_ref):
        o_ref[...] = (mine_ref[0].astype(F32) + sib_ref[...].astype(F32)).astype(o_ref.dtype)

    pair = pl.pallas_call(
        pair_sum, out_shape=S((n_chip, r, c_), parts.dtype),
        grid_spec=pltpu.PrefetchScalarGridSpec(
            num_scalar_prefetch=1, grid=(n_chip, r // tm),
            in_specs=[pl.BlockSpec((1, 1, tm, c_), lambda q, i, core_ref: (q, core_ref[0], i, 0)),
                      pl.BlockSpec((1, tm, c_), lambda q, i, core_ref: (q, i, 0))],
            out_specs=pl.BlockSpec((1, tm, c_), lambda q, i, core_ref: (q, i, 0))),
        name=name + "_sum", compiler_params=_params(("parallel", "parallel")))(
            core, parts.reshape(n_chip, 2, r, c_), from_sibling)

    return pair


def _sum_parts(name, parts):
    n_parts, r, c = parts.shape
    tm = _pick(r, (512, 256, 128, 64, 32, 16, 8)) if r % 8 == 0 else r

    def body(p_ref, o_ref):
        acc = p_ref[0].astype(F32)
        for d in range(1, n_parts):
            acc = acc + p_ref[d].astype(F32)
        o_ref[...] = acc

    return pl.pallas_call(body, grid=(r // tm,), in_specs=[pl.BlockSpec((n_parts, tm, c), lambda i: (0, i, 0))],
                          out_specs=pl.BlockSpec((tm, c), lambda i: (i, 0)), out_shape=S((r, c), F32), name=name,
                          compiler_params=_params(("parallel",)))(parts)


def _adamw(name, w, g, m, v, jobs=()):
    r, c = w.shape
    parts = g.ndim == 3
    n_parts = g.shape[0] if parts else 1
    tile_rows = max(8, (2 << 20) // (4 * c))
    tm = _pick(r, tuple(q for q in (2048, 1024, 512, 256, 128, 64, 32, 16, 8) if q <= tile_rows)) if r % 8 == 0 else r

    def body(w_ref, g_ref, m_ref, v_ref, go_ref, d_ref, mo_ref, vo_ref):
        if parts:
            gg = g_ref[0].astype(F32)
            for d in range(1, n_parts):
                gg = gg + g_ref[d].astype(F32)
        else:
            gg = g_ref[...]
        mm = ADAM_B1 * m_ref[...] + (1.0 - ADAM_B1) * gg
        vv = ADAM_B2 * v_ref[...] + (1.0 - ADAM_B2) * jnp.square(gg)
        m_hat = mm / (1.0 - ADAM_B1 ** ADAM_STEP)
        v_hat = vv / (1.0 - ADAM_B2 ** ADAM_STEP)
        go_ref[...] = gg
        d_ref[...] = -ADAM_LR * (m_hat / (jnp.sqrt(v_hat) + ADAM_EPS) + ADAM_WD * w_ref[...])
        mo_ref[...] = mm
        vo_ref[...] = vv

    spec = pl.BlockSpec((tm, c), lambda i: (i, 0))
    g_spec = pl.BlockSpec((n_parts, tm, c), lambda i: (0, i, 0)) if parts else spec
    return _pcall(body, grid=(r // tm,), in_specs=[spec, g_spec, spec, spec], out_specs=[spec] * 4,
                  out_shape=[S((r, c), F32)] * 4, name=name, semantics=("parallel",), inputs=(w, g, m, v), jobs=jobs)


def _rms(x, g):
    return x * lax.rsqrt(jnp.mean(x * x, axis=-1, keepdims=True) + RMS_EPS) * g


def _adaln(x, g, shift, scale):
    return _rms(x, g) * (1.0 + scale) + shift


def _dot(a, b, dn="nn", hi=False, prec=None):
    if hi or prec is not None:
        return lax.dot_general(a, b, _DN[dn], precision=HI if hi else prec, preferred_element_type=F32)
    return lax.dot_general(a.astype(BF16), b.astype(BF16), _DN[dn], preferred_element_type=F32)


def _sg_mix(p, ln_g, ln_b, w_s, b_st):
    d = p.shape[1] // 3
    gd = d // SG_GROUPS
    u = jax.nn.gelu(p[:, :d])
    vf = jax.nn.gelu(p[:, d:2 * d])
    z = p[:, 2 * d:]
    mean = jnp.mean(vf, axis=-1, keepdims=True)
    var = jnp.mean(jnp.square(vf - mean), axis=-1, keepdims=True)
    vn = (vf - mean) * lax.rsqrt(var + LN_EPS) * ln_g + ln_b
    row = lax.broadcasted_iota(jnp.int32, (SG_CHUNK, SG_CHUNK), 0)
    col = lax.broadcasted_iota(jnp.int32, (SG_CHUNK, SG_CHUNK), 1)
    fs = []
    for g in range(SG_GROUPS):
        w = jnp.where(row >= col, w_s[g], 0.0)
        fs.append(_dot(w, vn[:, g * gd:(g + 1) * gd]))
    sel = (lax.broadcasted_iota(jnp.int32, (SG_GROUPS, d), 1) // gd
           == lax.broadcasted_iota(jnp.int32, (SG_GROUPS, d), 0)).astype(F32)
    f = jnp.concatenate(fs, axis=1) + _dot(b_st, sel, hi=True)
    return u * f * jax.nn.silu(z)


def _rot_half(x):
    n = x.shape[1]
    lane = lax.broadcasted_iota(jnp.int32, x.shape, 1)
    return jnp.where(lane % HEAD < HEAD // 2, -pltpu.roll(x, n - HEAD // 2, 1), pltpu.roll(x, HEAD // 2, 1))


def _rope(x, cos, sin, sign):
    reps = x.shape[1] // cos.shape[1]
    return x * jnp.tile(cos, (1, reps)) + sign * _rot_half(x) * jnp.tile(sin, (1, reps))


def _attn_block(q, kp, kc, vp, vc, sink, prev_bias):
    each = lambda f, *ls: [f(*xs) for xs in zip(*ls)]
    r = sink[0].shape[0]
    cur, prob, _ = _attn_probs(q, kp, kc, sink, prev_bias)
    flat = lambda x: x.reshape(r * SWA_BLOCK, SWA_BLOCK)
    pc, pp = each(lambda p: flat(jnp.where(cur, p, 0.0)), prob), each(lambda p: flat(jnp.where(cur, 0.0, p)), prob)
    return each(lambda a, va, b, vb: _dot(a, va) + _dot(b, vb), pp, vp, pc, vc)


def _attn_probs(q, kp, kc, sink, prev_bias):
    each = lambda f, *ls: [f(*xs) for xs in zip(*ls)]
    r = sink[0].shape[0]
    scores = lambda a, b: (_dot(a, b, "nt") * (HEAD ** -0.5)).reshape(r, SWA_BLOCK, SWA_BLOCK)
    sp, sc = each(scores, q, kp), each(scores, q, kc)
    cur = (lax.broadcasted_iota(jnp.int32, (r, SWA_BLOCK, SWA_BLOCK), 2)
           <= lax.broadcasted_iota(jnp.int32, (r, SWA_BLOCK, SWA_BLOCK), 1))
    s = each(lambda a, b: jnp.where(cur, b, a + prev_bias), sp, sc)
    m = each(lambda a, sk: jnp.maximum(jnp.max(a, axis=-1, keepdims=True), sk), s, sink)
    e, es = each(lambda a, m_: jnp.exp(a - m_), s, m), each(lambda sk, m_: jnp.exp(sk - m_), sink, m)
    denom = each(lambda a, b: jnp.sum(a, axis=-1, keepdims=True) + b, e, es)
    return cur, each(lambda a, dn: a / dn, e, denom), each(lambda a, dn: a / dn, es, denom)


def _attn_block_bwd(q, kp, kc, vp, vc, sink, do, prev_bias):
    each = lambda f, *ls: [f(*xs) for xs in zip(*ls)]
    r = sink[0].shape[0]
    scale = HEAD ** -0.5
    cube = lambda x: x.reshape(r, SWA_BLOCK, SWA_BLOCK)
    flat = lambda x: x.reshape(r * SWA_BLOCK, SWA_BLOCK)
    cur, prob, p_sink = _attn_probs(q, kp, kc, sink, prev_bias)
    pc, pp = each(lambda p: flat(jnp.where(cur, p, 0.0)), prob), each(lambda p: flat(jnp.where(cur, 0.0, p)), prob)
    dprob = each(lambda g, va, vb: jnp.where(cur, cube(_dot(g, vb, "nt")), cube(_dot(g, va, "nt"))), do, vp, vc)
    dvp, dvc = each(lambda p, g: _dot(p, g, "tn"), pp, do), each(lambda p, g: _dot(p, g, "tn"), pc, do)
    delta = each(lambda p, dp: jnp.sum(p * dp, axis=-1, keepdims=True), prob, dprob)
    ds = each(lambda p, dp, dl: p * (dp - dl), prob, dprob, delta)
    dsc, dsp = each(lambda x: flat(jnp.where(cur, x, 0.0)), ds), each(lambda x: flat(jnp.where(cur, 0.0, x)), ds)
    dsink = each(lambda ps, dl: -jnp.sum(ps * dl, axis=1, keepdims=True), p_sink, delta)
    dq = each(lambda a, ka, b, kb: (_dot(a, ka) + _dot(b, kb)) * scale, dsp, kp, dsc, kc)
    dkp, dkc = each(lambda a, q_: _dot(a, q_, "tn") * scale, dsp, q), each(lambda a, q_: _dot(a, q_, "tn") * scale, dsc, q)
    return dq, dkp, dkc, dvp, dvc, dsink


def _rwkv_chunk(s0, r, k, v, logw, a, k_k, k_a, r_k, gn_g, gn_b):
    c = r[0].shape[0]
    each = lambda f, *ls: [f(*xs) for xs in zip(*ls)]
    gram = functools.partial(_dot, prec=RW_PREC)
    row = lax.broadcasted_iota(jnp.int32, (c, c), 0)
    col = lax.broadcasted_iota(jnp.int32, (c, c), 1)
    incl, strict = row >= col, row > col
    ones_l = incl.astype(F32)

    def unit(x):
        return x / jnp.maximum(jnp.sqrt(jnp.sum(x * x, axis=-1, keepdims=True)), 1e-12)

    kk = each(lambda k_, p: unit(k_ * p), k, k_k)
    km = each(lambda k_, a_, p: k_ * (1.0 + (a_ - 1.0) * p), k, a, k_a)
    b = each(lambda x, a_: x * a_, kk, a)
    first_half = lax.broadcasted_iota(jnp.int32, (c, HEAD), 0) < c // 2
    mid = each(lambda w: jnp.sum(jnp.where(first_half, w, 0.0), axis=0, keepdims=True), logw)
    cum = each(lambda w, m: _dot(ones_l, w, hi=True) - m, logw, mid)
    alpha = each(lambda x, cu, w: x * jnp.exp(cu - w), kk, cum, logw)
    beta = each(lambda x, cu: x * jnp.exp(-cu), b, cum)
    kap = each(lambda x, cu: x * jnp.exp(-cu), km, cum)
    rho = each(lambda x, cu: x * jnp.exp(cu), r, cum)
    s0 = each(lambda s, m: s * jnp.exp(m), s0, mid)
    lab = each(lambda x, y_: jnp.where(strict, gram(x, y_, "nt"), 0.0), alpha, beta)
    lak = each(lambda x, y_: jnp.where(strict, gram(x, y_, "nt"), 0.0), alpha, kap)
    xs = each(lambda al, s, l, v_: _dot(al, s, "nt") + _dot(l, v_), alpha, s0, lak, v)
    xs = each(lambda x, l: x - _dot(l, x), xs, lab)
    lp, power = lab, 2
    while power < c:
        lp = each(lambda l: _dot(l, l), lp)
        xs = each(lambda x, l: x + _dot(l, x), xs, lp)
        power *= 2
    u = each(lambda x: -x, xs)
    mrb = each(lambda x, y_: jnp.where(incl, gram(x, y_, "nt"), 0.0), rho, beta)
    mrk = each(lambda x, y_: jnp.where(incl, gram(x, y_, "nt"), 0.0), rho, kap)
    y = each(lambda rh, s, mb, u_, mk, v_: _dot(rh, s, "nt") + _dot(mb, u_) + _dot(mk, v_), rho, s0, mrb, u, mrk, v)
    s1 = each(lambda s, u_, be, v_, ka, w, m: (s + _dot(u_, be, "tn") + _dot(v_, ka, "tn"))
              * jnp.exp(jnp.sum(w, axis=0, keepdims=True) - m), s0, u, beta, v, kap, logw, mid)

    def finish(y_, g, bias, r_, km_, rk, v_):
        mean = jnp.mean(y_, axis=-1, keepdims=True)
        var = jnp.mean(jnp.square(y_ - mean), axis=-1, keepdims=True)
        y_ = (y_ - mean) * lax.rsqrt(var + GN_EPS) * g + bias
        return y_ + jnp.sum(r_ * km_ * rk, axis=-1, keepdims=True) * v_

    return each(finish, y, gn_g, gn_b, r, km, r_k, v), s1


def _norm_fwd(name, x, g, shift, scale):
    return _rows(name, lambda x_, g_, sh, sc: _adaln(x_, g_, sh, sc), [x], [g, shift, scale], [(x.shape[1], BF16)], [], 256)[0]


def _norm_bwd(name, x, dh, dx_res, g, shift, scale, below=None):
    d = x.shape[1]

    def fn(x_, dh_, dr_, *rest):
        g_, sh, sc = rest[-4:-1] if below else rest
        _, vjp = jax.vjp(_adaln, x_, g_, sh, sc)
        dx, dg, dsh, dsc = vjp(dh_)
        dx = dx + dr_
        if not below:
            return dx, dg, dsh, dsc
        y_, gate_ = rest[0], rest[-1]
        return dx, gate_ * dx, dg, dsh, dsc, jnp.sum(dx * y_, axis=0, keepdims=True)

    if not below:
        return _rows(name, fn, [x, dh, dx_res], [g, shift, scale], [(d, F32)], [(1, d)] * 3, 256)
    return _rows(name, fn, [x, dh, dx_res, below[0]], [g, shift, scale, below[1]], [(d, F32), (d, BF16)], [(1, d)] * 4, 256)


def _sg_fwd(name, p, ln_g, ln_b, w_s, b_st):
    d = p.shape[1] // 3
    return _rows(name, _sg_mix, [p], [ln_g, ln_b, w_s, b_st], [(d, BF16)], [], SG_CHUNK)[0]


def _sg_bwd(name, p, dmix, ln_g, ln_b, w_s, b_st, jobs=()):
    def fn(p_, dm_, lg, lb, ws, bs):
        _, vjp = jax.vjp(_sg_mix, p_, lg, lb, ws, bs)
        return vjp(dm_)

    return _rows(name, fn, [p, dmix], [ln_g, ln_b, w_s, b_st], [(p.shape[1], BF16)],
                 [ln_g.shape, ln_b.shape, w_s.shape, b_st.shape], SG_CHUNK, jobs=jobs)


def _rope_tables(pos, inv_freq):
    ang = pos * inv_freq
    return jnp.cos(ang), jnp.sin(ang)


def _swa_pre(name, p, pos, inv_freq, d):
    kvw = SWA_KV * HEAD

    def fn(p_, pos_, fr):
        cos, sin = _rope_tables(pos_, fr)
        return (_rope(p_[:, :d], cos, sin, 1.0), _rope(p_[:, d:d + kvw], cos, sin, 1.0), p_[:, d + kvw:d + 2 * kvw])

    return _rows(name, fn, [p, pos], [inv_freq], [(d, BF16), (kvw, BF16), (kvw, BF16)], [], 256)


def _q_groups(ref, kv, rep):
    heads = _head_cols(ref, kv * rep)
    return [jnp.concatenate(heads[g * rep:(g + 1) * rep], axis=0) for g in range(kv)]


def _q_ungroup(groups, rep):
    return jnp.concatenate([g[h * SWA_BLOCK:(h + 1) * SWA_BLOCK] for g in groups for h in range(rep)], axis=1)


def _swa_attn_fwd(name, q, k, v, sinks, jobs=()):
    t, d = q.shape
    kv, rep = sinks.shape[0], sinks.shape[1]
    nb = t // SWA_BLOCK

    def body(q_ref, kp_ref, kc_ref, vp_ref, vc_ref, s_ref, o_ref):
        prev_bias = jnp.where(pl.program_id(0) > 0, 0.0, NEG).astype(F32)
        o = _attn_block(_q_groups(q_ref, kv, rep), *[_head_cols(ref, kv) for ref in (kp_ref, kc_ref, vp_ref, vc_ref)],
                        [s_ref[g] for g in range(kv)], prev_bias)
        o_ref[...] = _q_ungroup(o, rep)

    qs = pl.BlockSpec((SWA_BLOCK, d), lambda n: (n, 0))
    cur = pl.BlockSpec((SWA_BLOCK, kv * HEAD), lambda n: (n, 0))
    prev = pl.BlockSpec((SWA_BLOCK, kv * HEAD), lambda n: (jnp.maximum(n - 1, 0), 0))
    ss = pl.BlockSpec(sinks.shape, lambda n: (0, 0, 0, 0))
    return _pcall(body, grid=(nb,), in_specs=[qs, prev, cur, prev, cur, ss], out_specs=[qs], out_shape=[S((t, d), F32)],
                  name=name, semantics=("parallel",), inputs=(q, k, k, v, v, sinks), jobs=jobs)[0]


def _swa_attn_bwd(name, q, k, v, sinks, do, jobs=()):
    t, d = q.shape
    kv, rep = sinks.shape[0], sinks.shape[1]
    nb = t // SWA_BLOCK

    def body(q_ref, kp_ref, kc_ref, vp_ref, vc_ref, s_ref, do_ref, dq_ref, dkc_ref, dkp_ref, dvc_ref, dvp_ref, ds_ref):
        n = pl.program_id(0)
        prev_bias = jnp.where(n > 0, 0.0, NEG).astype(F32)
        args = [_q_groups(q_ref, kv, rep)] + [_head_cols(ref, kv) for ref in (kp_ref, kc_ref, vp_ref, vc_ref)]
        dq, dkp, dkc, dvp, dvc, ds = _attn_block_bwd(*args, [s_ref[g] for g in range(kv)], _q_groups(do_ref, kv, rep), prev_bias)
        dq_ref[...] = _q_ungroup(dq, rep)
        for ref, val in ((dkc_ref, dkc), (dkp_ref, dkp), (dvc_ref, dvc), (dvp_ref, dvp)):
            ref[...] = jnp.concatenate(val, axis=1)

        @pl.when(n == 0)
        def _():
            ds_ref[...] = jnp.zeros_like(ds_ref)

        for g in range(kv):
            ds_ref[g] += ds[g]

    qs = pl.BlockSpec((SWA_BLOCK, d), lambda n: (n, 0))
    cur = pl.BlockSpec((SWA_BLOCK, kv * HEAD), lambda n: (n, 0))
    prev = pl.BlockSpec((SWA_BLOCK, kv * HEAD), lambda n: (jnp.maximum(n - 1, 0), 0))
    ss = pl.BlockSpec(sinks.shape, lambda n: (0, 0, 0, 0))
    return _pcall(body, grid=(nb,), in_specs=[qs, prev, cur, prev, cur, ss, qs], out_specs=[qs, cur, cur, cur, cur, ss],
                  out_shape=[S((t, d), F32)] + [S((t, kv * HEAD), F32)] * 4 + [S(sinks.shape, F32)], name=name,
                  semantics=("arbitrary",), inputs=(q, k, k, v, v, sinks, do), jobs=jobs)


def _gate_fwd(name, o, z_src, z_off, d):
    return _rows(name, lambda o_, p_: o_ * jax.nn.silu(p_[:, z_off:z_off + d]), [o, z_src], [], [(d, BF16)], [], 256)[0]


def _gate_bwd(name, o, z_src, z_off, d, dmix):
    def fn(o_, p_, dm_):
        _, vjp = jax.vjp(lambda oo, zz: oo * jax.nn.silu(zz), o_, p_[:, z_off:z_off + d])
        return vjp(dm_)

    return _rows(name, fn, [o, z_src, dmix], [], [(d, F32), (d, F32)], [], 256)


def _swa_post_bwd(name, dq, dkc, dkp_up, dvc, dvp_up, dz, pos, inv_freq):
    def fn(dq_, dkc_, dkp_, dvc_, dvp_, dz_, pos_, fr):
        cos, sin = _rope_tables(pos_, fr)
        return jnp.concatenate([_rope(dq_, cos, sin, -1.0), _rope(dkc_ + dkp_, cos, sin, -1.0), dvc_ + dvp_, dz_], axis=1)

    n = dq.shape[1] + dkc.shape[1] + dvc.shape[1] + dz.shape[1]
    return _rows(name, fn, [dq, dkc, dkp_up, dvc, dvp_up, dz, pos], [inv_freq], [(n, BF16)], [], 256)[0]


HALO = 8


def _row_before(x, halo_ref, i):
    first = jnp.where(i > 0, halo_ref[pl.ds(HALO - 1, 1), :], 0.0)
    row = lax.broadcasted_iota(jnp.int32, x.shape, 0)
    return jnp.where(row == 0, first, pltpu.roll(x, 1, 0))


def _row_after(x, halo, i, n_tiles):
    last = jnp.where(i < n_tiles - 1, halo, 0.0)
    row = lax.broadcasted_iota(jnp.int32, x.shape, 0)
    return jnp.where(row == x.shape[0] - 1, last, pltpu.roll(x, x.shape[0] - 1, 0))


def _lerp_fwd(name, p, mu, widths):
    t, n = p.shape
    tm = 128

    def body(p_ref, halo_ref, mu_ref, *o_refs):
        x = p_ref[...]
        pm = x + (_row_before(x, halo_ref, pl.program_id(0)) - x) * mu_ref[...]
        o = 0
        for ref, w in zip(o_refs, widths):
            ref[...] = pm[:, o:o + w]
            o += w

    return pl.pallas_call(
        body, grid=(t // tm,),
        in_specs=[pl.BlockSpec((tm, n), lambda i: (i, 0)),
                  pl.BlockSpec((HALO, n), lambda i: (jnp.maximum(i * (tm // HALO) - 1, 0), 0)),
                  pl.BlockSpec((1, n), lambda i: (0, 0))],
        out_specs=[pl.BlockSpec((tm, w), lambda i: (i, 0)) for w in widths],
        out_shape=[S((t, w), F32) for w in widths], name=name, compiler_params=_params(("parallel",)))(p, p, mu)


def _lerp_bwd(name, dpm_parts, p, mu):
    t, n = p.shape
    k = len(dpm_parts)
    tm = 64
    n_tiles = t // tm

    def body(*refs):
        d_refs, dh_refs = refs[:k], refs[k:2 * k]
        p_ref, ph_ref, mu_ref, dp_ref, dmu_ref = refs[2 * k:]
        i = pl.program_id(0)
        cat = lambda vals: jnp.concatenate(vals, axis=1) if k > 1 else vals[0]
        dpm = cat([r[...] for r in d_refs])
        dnext = cat([r[pl.ds(0, 1), :] for r in dh_refs])
        x, mu_ = p_ref[...], mu_ref[...]
        dp_ref[...] = (dpm * (1.0 - mu_) + _row_after(dpm, dnext, i, n_tiles) * mu_).astype(dp_ref.dtype)
        dmu = jnp.sum(dpm * (_row_before(x, ph_ref, i) - x), axis=0, keepdims=True)

        @pl.when(i == 0)
        def _():
            dmu_ref[...] = dmu

        @pl.when(i > 0)
        def _():
            dmu_ref[...] += dmu

    per = tm // HALO
    d_specs = [pl.BlockSpec((tm, a.shape[1]), lambda i: (i, 0)) for a in dpm_parts]
    dh_specs = [pl.BlockSpec((HALO, a.shape[1]), lambda i: (jnp.minimum((i + 1) * per, t // HALO - 1), 0)) for a in dpm_parts]
    return pl.pallas_call(
        body, grid=(n_tiles,),
        in_specs=d_specs + dh_specs + [pl.BlockSpec((tm, n), lambda i: (i, 0)),
                                       pl.BlockSpec((HALO, n), lambda i: (jnp.maximum(i * per - 1, 0), 0)),
                                       pl.BlockSpec((1, n), lambda i: (0, 0))],
        out_specs=[pl.BlockSpec((tm, n), lambda i: (i, 0)), pl.BlockSpec((1, n), lambda i: (0, 0))],
        out_shape=[S((t, n), BF16), S((1, n), F32)], name=name,
        compiler_params=_params(("arbitrary",)))(*dpm_parts, *dpm_parts, p, p, mu)


def _lora_act(pl_, w0, w_lora, a0, a_lora):
    logw = -DECAY_SCALE * jax.nn.sigmoid(w0 + _dot(jnp.tanh(pl_), w_lora))
    a = jax.nn.sigmoid(a0 + _dot(pl_, a_lora))
    return logw, a


def _lora_fwd(name, pl_, w0, w_lora, a0, a_lora):
    d = w0.shape[1]
    return _rows(name, _lora_act, [pl_], [w0, w_lora, a0, a_lora], [(d, F32), (d, F32)], [], 256)


def _lora_bwd(name, pl_, dlogw, da, w0, w_lora, a0, a_lora):
    def fn(p_, dl_, da_, w0_, wl_, a0_, al_):
        _, vjp = jax.vjp(_lora_act, p_, w0_, wl_, a0_, al_)
        return vjp((dl_, da_))

    return _rows(name, fn, [pl_, dlogw, da], [w0, w_lora, a0, a_lora], [(pl_.shape[1], F32)],
                 [w0.shape, w_lora.shape, a0.shape, a_lora.shape], 256)


def _head_cols(ref, hb):
    x = ref[...].astype(F32)
    xo = pltpu.roll(x, x.shape[1] - HEAD, 1)
    return [(x if j % 2 == 0 else xo)[:, 2 * HEAD * (j // 2):2 * HEAD * (j // 2) + HEAD] for j in range(hb)]


def _rwkv_scan_fwd(name, r, k, v, logw, a, hp, jobs=()):
    t, d = r.shape
    h, nc, hb = d // HEAD, t // RW_CHUNK, RW_HEADS_FWD

    def body(r_ref, k_ref, v_ref, w_ref, a_ref, kk_ref, ka_ref, rk_ref, gg_ref, gb_ref, y_ref, st_ref, s_scr):
        @pl.when(pl.program_id(1) == 0)
        def _():
            s_scr[...] = jnp.zeros_like(s_scr)

        s0 = [s_scr[j] for j in range(hb)]
        for j in range(hb):
            st_ref[j, 0] = s0[j]
        y, s1 = _rwkv_chunk(s0, *[_head_cols(ref, hb) for ref in (r_ref, k_ref, v_ref, w_ref, a_ref, kk_ref, ka_ref, rk_ref,
                                                                 gg_ref, gb_ref)])
        y_ref[...] = jnp.concatenate(y, axis=1)
        for j in range(hb):
            s_scr[j] = s1[j]

    seq = pl.BlockSpec((RW_CHUNK, hb * HEAD), lambda i, n: (n, i))
    par = pl.BlockSpec((1, hb * HEAD), lambda i, n: (0, i))
    st = pl.BlockSpec((hb, 1, HEAD, HEAD), lambda i, n: (i, n, 0, 0))
    return _pcall(body, grid=(h // hb, nc), in_specs=[seq] * 5 + [par] * 5, out_specs=[seq, st],
                  out_shape=[S((t, d), F32), S((h, nc, HEAD, HEAD), F32)], scratch_shapes=[pltpu.VMEM((hb, HEAD, HEAD), F32)],
                  name=name, semantics=("parallel", "arbitrary"), inputs=(r, k, v, logw, a, *hp), jobs=jobs)


def _rwkv_scan_bwd(name, r, k, v, logw, a, hp, states, dy, jobs=()):
    t, d = r.shape
    h, nc, hb = d // HEAD, t // RW_CHUNK, RW_HEADS

    def body(r_ref, k_ref, v_ref, w_ref, a_ref, kk_ref, ka_ref, rk_ref, gg_ref, gb_ref, st_ref, dy_ref,
             dr_ref, dk_ref, dv_ref, dw_ref, da_ref, dkk_ref, dka_ref, drk_ref, dgg_ref, dgb_ref, ds_scr):
        n = pl.program_id(1)

        @pl.when(n == 0)
        def _():
            ds_scr[...] = jnp.zeros_like(ds_scr)
            for ref in (dkk_ref, dka_ref, drk_ref, dgg_ref, dgb_ref):
                ref[...] = jnp.zeros_like(ref)

        ins = [[st_ref[j, 0] for j in range(hb)]] + [_head_cols(ref, hb) for ref in (r_ref, k_ref, v_ref, w_ref, a_ref, kk_ref,
                                                                                  ka_ref, rk_ref, gg_ref, gb_ref)]
        _, vjp = jax.vjp(_rwkv_chunk, *ins)
        ds0, *dseq, dkk, dka, drk, dgg, dgb = vjp((_head_cols(dy_ref, hb), [ds_scr[j] for j in range(hb)]))
        for j in range(hb):
            ds_scr[j] = ds0[j]
        for ref, val in zip((dr_ref, dk_ref, dv_ref, dw_ref, da_ref), dseq):
            ref[...] = jnp.concatenate(val, axis=1)
        for ref, val in ((dkk_ref, dkk), (dka_ref, dka), (drk_ref, drk), (dgg_ref, dgg), (dgb_ref, dgb)):
            ref[...] += jnp.concatenate(val, axis=1)

    seq = pl.BlockSpec((RW_CHUNK, hb * HEAD), lambda i, n: (nc - 1 - n, i))
    par = pl.BlockSpec((1, hb * HEAD), lambda i, n: (0, i))
    st = pl.BlockSpec((hb, 1, HEAD, HEAD), lambda i, n: (i, nc - 1 - n, 0, 0))
    return _pcall(body, grid=(h // hb, nc), in_specs=[seq] * 5 + [par] * 5 + [st, seq], out_specs=[seq] * 5 + [par] * 5,
                  out_shape=[S((t, d), F32)] * 5 + [S((1, d), F32)] * 5, scratch_shapes=[pltpu.VMEM((hb, HEAD, HEAD), F32)],
                  name=name, semantics=("parallel", "arbitrary"), inputs=(r, k, v, logw, a, *hp, states, dy), jobs=jobs)


def _loss_head(name, x, target, g, y, gate):
    d = x.shape[1]

    def fn(x_, t_, y_, g_, gate_):
        def f(xx, gg):
            err = _rms(xx, gg) - t_
            return 0.5 * jnp.sum(jnp.mean(err * err, axis=-1, keepdims=True), axis=0, keepdims=True)

        l, vjp = jax.vjp(f, x_, g_)
        dx, dg = vjp(jnp.ones((1, 1), F32))
        return dx, gate_ * dx, dg, jnp.broadcast_to(l, (1, 128)), jnp.sum(dx * y_, axis=0, keepdims=True)

    return _rows(name, fn, [x, target, y], [g, gate], [(d, F32), (d, BF16)], [(1, d), (1, 128), (1, d)], 256)


def _mod_fwd(name, cond_all, mod_w, mod_b_cols):
    l, d, n = mod_w.shape

    def body(c_ref, w_ref, b_ref, o_ref):
        o_ref[0] = _dot(jax.nn.silu(c_ref[...]), w_ref[0], hi=True) + b_ref[0]

    return pl.pallas_call(body, grid=(l,), in_specs=[pl.BlockSpec((N_DEV, d), lambda i: (0, 0)),
                                                      pl.BlockSpec((1, d, n), lambda i: (i, 0, 0)),
                                                      pl.BlockSpec((1, 1, n), lambda i: (i, 0, 0))],
                          out_specs=pl.BlockSpec((1, N_DEV, n), lambda i: (i, 0, 0)), out_shape=S((l, N_DEV, n), F32),
                          name=name, compiler_params=_params(("parallel",)))(cond_all, mod_w, mod_b_cols)


def _mod_bwd(name, cond_all, dmod_cols, dmod_all):
    l, _, n = dmod_cols.shape
    d = cond_all.shape[1]
    nb = dmod_all.shape[2]

    def body(c_ref, dc_ref, da_ref, gw_ref, gb_ref):
        gw_ref[0] = _dot(jax.nn.silu(c_ref[...]), dc_ref[0], "tn", hi=True)
        acc = da_ref[0, 0:1, :]
        for bi in range(1, N_DEV):
            acc = acc + da_ref[0, bi:bi + 1, :]
        gb_ref[0] = acc

    return pl.pallas_call(body, grid=(l,), in_specs=[pl.BlockSpec((N_DEV, d), lambda i: (0, 0)),
                                                      pl.BlockSpec((1, N_DEV, n), lambda i: (i, 0, 0)),
                                                      pl.BlockSpec((1, N_DEV, nb), lambda i: (i, 0, 0))],
                          out_specs=[pl.BlockSpec((1, d, n), lambda i: (i, 0, 0)), pl.BlockSpec((1, 1, nb), lambda i: (i, 0, 0))],
                          out_shape=[S((l, d, n), F32), S((l, 1, nb), F32)], name=name,
                          compiler_params=_params(("parallel",)))(cond_all, dmod_cols, dmod_all)


def _shift_up(a, n=1):
    return jnp.concatenate([a[n:], jnp.zeros_like(a[:n])], axis=0)


def _cols_full(g):
    return g.transpose(1, 0, 2).reshape(g.shape[1], -1)


def _cols_parts(full):
    r, n = full.shape
    return full.reshape(r, N_DEV, n // N_DEV).transpose(1, 0, 2)


def _pack(arrs, mult=1024):
    flat = jnp.concatenate([a.reshape(-1) for a in arrs])
    pad = (-flat.shape[0]) % mult
    return jnp.pad(flat, (0, pad)).reshape(-1, 128)


def _unpack(flat, shapes):
    out, o = [], 0
    for s in shapes:
        n = math.prod(s)
        out.append(flat[o:o + n].reshape(s))
        o += n
    return out


def _local_step(x, pos, target, mods, norm_g, final_norm_g, layer_weights, hooks=None, on_grads=None):
    t, d = x.shape
    hooks = hooks or {}
    jobs = lambda nm: hooks.get(nm, ())
    notify = on_grads or (lambda *a: None)
    kinds = [i % 3 for i in range(DEPTH)]
    inv_freq = (ROPE_THETA ** (-jnp.arange(HEAD // 2, dtype=F32) / (HEAD // 2)))
    inv_freq = jnp.tile(inv_freq, 128 // (HEAD // 2)).reshape(1, 128)
    saved = []
    for i, kind in enumerate(kinds):
        lw = layer_weights(i)
        shift, scale, gate = (mods[i, q * d:(q + 1) * d].reshape(1, d) for q in range(3))
        g = norm_g[i].reshape(1, d)
        h = _norm_fwd(f"norm_fwd{i}", x, g, shift, scale)
        sv = dict(x=x, h=h, g=g, shift=shift, scale=scale, gate=gate, lw=lw)
        if kind == 0:
            p = _mm(f"sg_in{i}", h, lw["w_in"], "nn", F32, jobs=jobs(f"sg_in{i}"))
            mix = _sg_fwd(f"sg_mix{i}", p, lw["ln_g"], lw["ln_b"], lw["w_s"], lw["b_st"])
            sv.update(p=p)
        elif kind == 1:
            p = _mm(f"swa_in{i}", h, lw["w_in"], "nn", F32, jobs=jobs(f"swa_in{i}"))
            q, k, v = _swa_pre(f"swa_pre{i}", p, pos, inv_freq, d)
            o = _swa_attn_fwd(f"swa_attn{i}", q, k, v, lw["sinks"], jobs=jobs(f"swa_attn{i}"))
            mix = _gate_fwd(f"swa_gate{i}", o, p, d + 2 * SWA_KV * HEAD, d)
            sv.update(p=p, qkv=(q, k, v), o=o)
        else:
            pm = _mm(f"rw_in{i}", h, lw["w_main"], "nn", F32, jobs=jobs(f"rw_in{i}"))
            plo = _mm(f"rw_inl{i}", h, lw["w_lorain"], "nn", F32)
            r, k, v, z = _lerp_fwd(f"rw_lerp{i}", pm, lw["mu_main"], [d] * 4)
            (pll,) = _lerp_fwd(f"rw_lerpl{i}", plo, lw["mu_lora"], [LORA_PAD])
            logw, a = _lora_fwd(f"rw_lora{i}", pll, lw["w0"], lw["w_lora"], lw["a0"], lw["a_lora"])
            seqs = (r, k, v, logw, a)
            o, states = _rwkv_scan_fwd(f"rw_scan{i}", *seqs, lw["hp"], jobs=jobs(f"rw_scan{i}"))
            mix = _gate_fwd(f"rw_gate{i}", o, z, 0, d)
            sv.update(pm=pm, plo=plo, pll=pll, z=z, seqs=seqs, states=states, o=o)
        y, x = _mm(f"out{i}", mix, lw["w_out"], "nn", F32, resid=(x, gate), jobs=jobs(f"out{i}"))
        sv.update(mix=mix, y=y)
        saved.append(sv)

    top = saved[-1]
    dx, dy, d_final_g, loss, dgate = _loss_head("loss_head", x, target, final_norm_g.reshape(1, d), top["y"], top["gate"])

    grads = dict(norm_g=[None] * DEPTH, sg_w_in=[None] * 2, sg_w_out=[None] * 2, sg_ln_g=[None] * 2, sg_ln_b=[None] * 2,
                 sg_w_s=[None] * 2, sg_b_st=[None] * 2, final_norm_g=d_final_g)
    dmods = [None] * DEPTH
    for i in reversed(range(DEPTH)):
        kind, j, sv = kinds[i], i // 3, saved[i]
        lw = sv["lw"]
        d_w_out = _mm(f"out_dw{i}", sv["mix"], dy, "tn", BF16)
        if kind == 0:
            grads["sg_w_out"][j] = d_w_out
        else:
            grads[("swa_w_out", "rw_w_out")[kind - 1]] = d_w_out
        notify(i, "out", grads)
        dmix = _mm(f"out_dx{i}", dy, lw["w_out"], "nt", F32, jobs=jobs(f"out_dx{i}"))
        if kind == 0:
            dp, dlg, dlb, dws, dbs = _sg_bwd(f"sg_mix_bwd{i}", sv["p"], dmix, lw["ln_g"], lw["ln_b"], lw["w_s"], lw["b_st"],
                                            jobs=jobs(f"sg_mix_bwd{i}"))
            grads["sg_ln_g"][j], grads["sg_ln_b"][j], grads["sg_w_s"][j], grads["sg_b_st"][j] = dlg, dlb, dws, dbs
            grads["sg_w_in"][j] = _mm(f"sg_in_dw{i}", sv["h"], dp, "tn", BF16, jobs=jobs(f"sg_in_dw{i}"))
            notify(i, "in", grads)
            dh = _mm(f"sg_in_dx{i}", dp, lw["w_in"], "nt", F32, jobs=jobs(f"sg_in_dx{i}"))
        elif kind == 1:
            z_off = d + 2 * SWA_KV * HEAD
            do, dz = _gate_bwd(f"swa_gate_bwd{i}", sv["o"], sv["p"], z_off, d, dmix)
            dq, dkc, dkp, dvc, dvp, dsinks = _swa_attn_bwd(f"swa_attn_bwd{i}", *sv["qkv"], lw["sinks"], do,
                                                           jobs=jobs(f"swa_attn_bwd{i}"))
            dkp, dvp = _shift_up(dkp, SWA_BLOCK), _shift_up(dvp, SWA_BLOCK)
            dp = _swa_post_bwd(f"swa_post_bwd{i}", dq, dkc, dkp, dvc, dvp, dz, pos, inv_freq)
            grads.update(swa_sinks=dsinks, swa_w_out=d_w_out)
            grads["swa_w_in"] = _mm(f"swa_in_dw{i}", sv["h"], dp, "tn", BF16, jobs=jobs(f"swa_in_dw{i}"))
            dh = _mm(f"swa_in_dx{i}", dp, lw["w_in"], "nt", F32, jobs=jobs(f"swa_in_dx{i}"))
        else:
            do, dz = _gate_bwd(f"rw_gate_bwd{i}", sv["o"], sv["z"], 0, d, dmix)
            res = _rwkv_scan_bwd(f"rw_scan_bwd{i}", *sv["seqs"], lw["hp"], sv["states"], do, jobs=jobs(f"rw_scan_bwd{i}"))
            dr, dk, dv, dlogw, da = res[:5]
            dpll, dw0, dwl, da0, dal = _lora_bwd(f"rw_lora_bwd{i}", sv["pll"], dlogw, da, lw["w0"], lw["w_lora"],
                                                  lw["a0"], lw["a_lora"])
            dpm, dmu_main = _lerp_bwd(f"rw_lerp_bwd{i}", [dr, dk, dv, dz], sv["pm"], lw["mu_main"])
            dpl, dmu_lora = _lerp_bwd(f"rw_lerpl_bwd{i}", [dpll], sv["plo"], lw["mu_lora"])
            grads.update(rw_w_out=d_w_out, rw_hp=res[5:], rw_w0=dw0, rw_w_lora=dwl, rw_a0=da0, rw_a_lora=dal,
                         rw_mu_main=dmu_main, rw_mu_lora=dmu_lora)
            grads["rw_w_main"] = _mm(f"rw_in_dw{i}", sv["h"], dpm, "tn", BF16)
            grads["rw_w_lorain"] = _mm(f"rw_inl_dw{i}", sv["h"], dpl, "tn", BF16)
            dh = _mm(f"rw_inl_dx{i}", dpl, lw["w_lorain"], "nt", F32)
            dh = _mm(f"rw_in_dx{i}", dpm, lw["w_main"], "nt", F32, add=dh)
        if kind != 0:
            notify(i, "in", grads)
        dgate_here = dgate
        if i > 0:
            dx, dy, dg, dshift, dscale, dgate = _norm_bwd(f"norm_bwd{i}", sv["x"], dh, dx, sv["g"], sv["shift"], sv["scale"],
                                                          below=(saved[i - 1]["y"], saved[i - 1]["gate"]))
        else:
            dx, dg, dshift, dscale = _norm_bwd(f"norm_bwd{i}", sv["x"], dh, dx, sv["g"], sv["shift"], sv["scale"])
        grads["norm_g"][i] = dg
        dmods[i] = jnp.concatenate([dshift, dscale, dgate_here], axis=1)
    return loss, dx, jnp.concatenate(dmods, axis=0), grads


def kernel(x, c, positions, norm_g, mod_w, mod_b, final_norm_g, sg_w_in, sg_w_out, sg_ln_g, sg_ln_b, sg_w_spatial, sg_b_spatial, swa_w_in, swa_w_out, swa_sinks, rwkv_w_in, rwkv_w_out, rwkv_mu, rwkv_w0, rwkv_w_lora, rwkv_a0, rwkv_a_lora, rwkv_k_k, rwkv_k_a, rwkv_r_k, rwkv_gn_g, rwkv_gn_b, loss_target, m_norm_g, m_mod_w, m_mod_b, m_final_norm_g, m_sg_w_in, m_sg_w_out, m_sg_ln_g, m_sg_ln_b, m_sg_w_spatial, m_sg_b_spatial, m_swa_w_in, m_swa_w_out, m_swa_sinks, m_rwkv_w_in, m_rwkv_w_out, m_rwkv_mu, m_rwkv_w0, m_rwkv_w_lora, m_rwkv_a0, m_rwkv_a_lora, m_rwkv_k_k, m_rwkv_k_a, m_rwkv_r_k, m_rwkv_gn_g, m_rwkv_gn_b, v_norm_g, v_mod_w, v_mod_b, v_final_norm_g, v_sg_w_in, v_sg_w_out, v_sg_ln_g, v_sg_ln_b, v_sg_w_spatial, v_sg_b_spatial, v_swa_w_in, v_swa_w_out, v_swa_sinks, v_rwkv_w_in, v_rwkv_w_out, v_rwkv_mu, v_rwkv_w0, v_rwkv_w_lora, v_rwkv_a0, v_rwkv_a_lora, v_rwkv_k_k, v_rwkv_k_a, v_rwkv_r_k, v_rwkv_gn_g, v_rwkv_gn_b):
    weights = dict(norm_g=norm_g, mod_w=mod_w, mod_b=mod_b, final_norm_g=final_norm_g, sg_w_in=sg_w_in, sg_w_out=sg_w_out,
                   sg_ln_g=sg_ln_g, sg_ln_b=sg_ln_b, sg_w_spatial=sg_w_spatial, sg_b_spatial=sg_b_spatial, swa_w_in=swa_w_in,
                   swa_w_out=swa_w_out, swa_sinks=swa_sinks, rwkv_w_in=rwkv_w_in, rwkv_w_out=rwkv_w_out, rwkv_mu=rwkv_mu,
                   rwkv_w0=rwkv_w0, rwkv_w_lora=rwkv_w_lora, rwkv_a0=rwkv_a0, rwkv_a_lora=rwkv_a_lora, rwkv_k_k=rwkv_k_k,
                   rwkv_k_a=rwkv_k_a, rwkv_r_k=rwkv_r_k, rwkv_gn_g=rwkv_gn_g, rwkv_gn_b=rwkv_gn_b)
    mom_m = dict(norm_g=m_norm_g, mod_w=m_mod_w, mod_b=m_mod_b, final_norm_g=m_final_norm_g, sg_w_in=m_sg_w_in,
                 sg_w_out=m_sg_w_out, sg_ln_g=m_sg_ln_g, sg_ln_b=m_sg_ln_b, sg_w_spatial=m_sg_w_spatial,
                 sg_b_spatial=m_sg_b_spatial, swa_w_in=m_swa_w_in, swa_w_out=m_swa_w_out, swa_sinks=m_swa_sinks,
                 rwkv_w_in=m_rwkv_w_in, rwkv_w_out=m_rwkv_w_out, rwkv_mu=m_rwkv_mu, rwkv_w0=m_rwkv_w0,
                 rwkv_w_lora=m_rwkv_w_lora, rwkv_a0=m_rwkv_a0, rwkv_a_lora=m_rwkv_a_lora, rwkv_k_k=m_rwkv_k_k,
                 rwkv_k_a=m_rwkv_k_a, rwkv_r_k=m_rwkv_r_k, rwkv_gn_g=m_rwkv_gn_g, rwkv_gn_b=m_rwkv_gn_b)
    mom_v = dict(norm_g=v_norm_g, mod_w=v_mod_w, mod_b=v_mod_b, final_norm_g=v_final_norm_g, sg_w_in=v_sg_w_in,
                 sg_w_out=v_sg_w_out, sg_ln_g=v_sg_ln_g, sg_ln_b=v_sg_ln_b, sg_w_spatial=v_sg_w_spatial,
                 sg_b_spatial=v_sg_b_spatial, swa_w_in=v_swa_w_in, swa_w_out=v_swa_w_out, swa_sinks=v_swa_sinks,
                 rwkv_w_in=v_rwkv_w_in, rwkv_w_out=v_rwkv_w_out, rwkv_mu=v_rwkv_mu, rwkv_w0=v_rwkv_w0,
                 rwkv_w_lora=v_rwkv_w_lora, rwkv_a0=v_rwkv_a0, rwkv_a_lora=v_rwkv_a_lora, rwkv_k_k=v_rwkv_k_k,
                 rwkv_k_a=v_rwkv_k_a, rwkv_r_k=v_rwkv_r_k, rwkv_gn_g=v_rwkv_gn_g, rwkv_gn_b=v_rwkv_gn_b)
    names = list(weights)
    t, d = x.shape[1], x.shape[2]
    me = 4 * lax.axis_index("x") + 2 * lax.axis_index("y") + lax.axis_index("c")
    n_mod = mod_w.shape[2]
    n_rw = rwkv_w_in.shape[2]

    small_names = ["sg_ln_g", "sg_ln_b", "rwkv_mu", "rwkv_w0", "rwkv_a0", "rwkv_k_k", "rwkv_k_a", "rwkv_gn_g", "rwkv_gn_b",
                   "rwkv_w_lora", "rwkv_a_lora"]
    small_shapes = [weights[n].shape for n in small_names]
    pk = _pack([c] + [weights[n] for n in small_names])
    gathered = _gather("gather_small", pk).reshape(N_DEV, -1)
    c_all = gathered[:, :d]
    per_dev = [_unpack(gathered[dv, d:], small_shapes) for dv in range(N_DEV)]
    full_small = {}
    for q, n in enumerate(small_names):
        full_small[n] = jnp.concatenate([per_dev[dv][q] for dv in range(N_DEV)], axis=-1)

    mod_b_cols = lax.dynamic_slice_in_dim(mod_b, me * n_mod, n_mod, axis=1).reshape(DEPTH, 1, n_mod)
    mod_part = _mod_fwd("mod_fwd", c_all, mod_w, mod_b_cols)
    mod_g = _gather("gather_mod", mod_part.reshape(DEPTH * N_DEV, n_mod))
    mod_g = mod_g.reshape(N_DEV, DEPTH, N_DEV, n_mod)
    mods = lax.dynamic_index_in_dim(mod_g, me, axis=2, keepdims=False)
    mods = mods.transpose(1, 0, 2).reshape(DEPTH, N_DEV * n_mod)

    job = lambda cls, src: dict(cls=cls, src=src)
    half = d // 2
    gj = dict(swa_in=job(_Gather, swa_w_in[0].astype(BF16)), swa_out=job(_Gather, swa_w_out[0].astype(BF16)),
              rw_in_a=job(_Gather, rwkv_w_in[0, :half].astype(BF16)), rw_in_b=job(_Gather, rwkv_w_in[0, half:].astype(BF16)),
              rw_out=job(_Gather, rwkv_w_out[0].astype(BF16)),
              sg_in1=job(_Gather, sg_w_in[1].astype(BF16)), sg_out1=job(_Gather, sg_w_out[1].astype(BF16)))
    hooks = {"sg_in0": [gj["swa_in"]], "out0": [gj["swa_out"]], "swa_in1": [gj["rw_in_a"]], "swa_attn1": [gj["rw_in_b"]],
             "out1": [gj["rw_out"]], "rw_scan2": [gj["sg_in1"], gj["sg_out1"]]}
    g_sg_in0 = _gather("gather_sg_in0", sg_w_in[0].astype(BF16))
    g_sg_out0 = _gather("gather_sg_out0", sg_w_out[0].astype(BF16))
    lora_rows = lambda w, off: jnp.zeros((LORA_PAD, d), F32).at[off:off + LORA].set(w)
    mu = full_small["rwkv_mu"].reshape(1, -1)
    heads = lambda a: a.reshape(1, -1)

    def layer_weights(i):
        if i % 3 == 0:
            j = i // 3
            g_in, g_out = (g_sg_in0, g_sg_out0) if j == 0 else (gj["sg_in1"]["out"], gj["sg_out1"]["out"])
            return dict(w_in=_cols_full(g_in), w_out=g_out.reshape(d, d), ln_g=full_small["sg_ln_g"][j].reshape(1, d),
                        ln_b=full_small["sg_ln_b"][j].reshape(1, d), w_s=sg_w_spatial[j], b_st=sg_b_spatial[j].T)
        if i % 3 == 1:
            return dict(w_in=_cols_full(gj["swa_in"]["out"]), w_out=gj["swa_out"]["out"].reshape(d, d),
                        sinks=swa_sinks.reshape(SWA_KV, SWA_REP, 1, 1))
        rw_in_full = jnp.concatenate([_cols_full(gj["rw_in_a"]["out"]), _cols_full(gj["rw_in_b"]["out"])], axis=0)
        return dict(w_main=rw_in_full[:, :4 * d], w_lorain=jnp.pad(rw_in_full[:, 4 * d:], ((0, 0), (0, LORA_PAD - 2 * LORA))),
                    w_out=gj["rw_out"]["out"].reshape(d, d), mu_main=mu[:, :4 * d],
                    mu_lora=jnp.pad(mu[:, 4 * d:], ((0, 0), (0, LORA_PAD - 2 * LORA))),
                    w0=full_small["rwkv_w0"], a0=full_small["rwkv_a0"],
                    w_lora=lora_rows(full_small["rwkv_w_lora"][0], 0), a_lora=lora_rows(full_small["rwkv_a_lora"][0], LORA),
                    hp=[heads(full_small["rwkv_k_k"]), heads(full_small["rwkv_k_a"]), heads(rwkv_r_k),
                        heads(full_small["rwkv_gn_g"]), heads(full_small["rwkv_gn_b"])])

    sj = {}

    def on_grads(i, which, g):
        def stage(nm, parts, host):
            sj[nm] = job(_Chips, _scatter_pairs("scatter_" + nm, parts.astype(BF16)))
            hooks.setdefault(host, []).append(sj[nm])

        rows_of = lambda a: a.reshape(N_DEV, -1, d)
        if (i, which) == (3, "out"):
            stage("sg_out1", rows_of(g["sg_w_out"][1]), "rw_scan_bwd2")
        elif (i, which) == (3, "in"):
            stage("sg_in1", _cols_parts(g["sg_w_in"][1]), "rw_scan_bwd2")
            sj["sp1"] = job(_Gather, g["sg_w_s"][1].reshape(-1, 128))
            hooks["rw_scan_bwd2"].append(sj["sp1"])
        elif (i, which) == (2, "out"):
            stage("rw_out", rows_of(g["rw_w_out"]), "swa_in_dx1")
        elif (i, which) == (2, "in"):
            d_rw_in = jnp.concatenate([g["rw_w_main"], g["rw_w_lorain"][:, :2 * LORA]], axis=1)
            stage("rw_in_a", _cols_parts(d_rw_in[:half]), "swa_attn_bwd1")
            stage("rw_in_b", _cols_parts(d_rw_in[half:]), "swa_in_dw1")
        elif (i, which) == (1, "out"):
            stage("swa_out", rows_of(g["swa_w_out"]), "out_dx0")
        elif (i, which) == (1, "in"):
            stage("swa_in", _cols_parts(g["swa_w_in"]), "sg_mix_bwd0")
        elif (i, which) == (0, "out"):
            stage("sg_out0", rows_of(g["sg_w_out"][0]), "sg_in_dw0")
        else:
            stage("sg_in0", _cols_parts(g["sg_w_in"][0]), "sg_in_dx0")

    loss, dx, dmods, g = _local_step(x[0], positions.reshape(t, 1).astype(F32), loss_target[0], mods, norm_g, final_norm_g,
                                     layer_weights, hooks, on_grads)

    dmod_g = _gather("gather_dmod", dmods)
    dmod_all = dmod_g.transpose(1, 0, 2)
    dmod_cols = lax.dynamic_slice_in_dim(dmod_all, me * n_mod, n_mod, axis=2)
    g_mod_w, g_mod_b = _mod_bwd("mod_bwd", c_all, dmod_cols, dmod_all)

    d_b_sp = [g["sg_b_st"][j].T for j in range(2)]
    rep = [loss[0, :1], jnp.concatenate(g["norm_g"], axis=0), g["final_norm_g"], g["sg_w_s"][0], jnp.stack(d_b_sp),
           g["swa_sinks"], g["rw_hp"][2]]
    rep_shapes = [(1,), norm_g.shape, final_norm_g.shape, sg_w_spatial.shape[1:], sg_b_spatial.shape, swa_sinks.shape,
                  rwkv_r_k.shape]
    rep_sum = _sum_parts("sum_rep", _gather("gather_rep", _pack(rep, 128 * 256))).reshape(-1)
    loss_tot, g_norm_g, g_final, g_w_sp0, g_b_sp, g_sinks, g_r_k = _unpack(rep_sum, rep_shapes)
    g_w_sp = jnp.stack([g_w_sp0, _sum_parts("sum_sp1", sj["sp1"]["out"]).reshape(sg_w_spatial.shape[1:])])

    p_sg_in = jnp.concatenate([sj["sg_in0"]["out"], sj["sg_in1"]["out"]], axis=1)
    p_sg_out = jnp.concatenate([sj["sg_out0"]["out"], sj["sg_out1"]["out"]], axis=1)
    p_swa_in, p_swa_out, p_rw_out = (sj[nm]["out"] for nm in ("swa_in", "swa_out", "rw_out"))
    p_rw_in = jnp.concatenate([sj["rw_in_a"]["out"], sj["rw_in_b"]["out"]], axis=1)
    d_mu = jnp.concatenate([g["rw_mu_main"], g["rw_mu_lora"][:, :2 * LORA]], axis=1)
    hp_flat = lambda a: a.reshape(1, -1)
    small_grads = dict(sg_ln_g=jnp.concatenate(g["sg_ln_g"], axis=0), sg_ln_b=jnp.concatenate(g["sg_ln_b"], axis=0), rwkv_mu=d_mu,
                       rwkv_w0=g["rw_w0"], rwkv_a0=g["rw_a0"], rwkv_k_k=hp_flat(g["rw_hp"][0]), rwkv_k_a=hp_flat(g["rw_hp"][1]),
                       rwkv_gn_g=hp_flat(g["rw_hp"][3]), rwkv_gn_b=hp_flat(g["rw_hp"][4]),
                       rwkv_w_lora=g["rw_w_lora"][None, :LORA], rwkv_a_lora=g["rw_a_lora"][None, LORA:2 * LORA])
    per_dest = []
    for dv in range(N_DEV):
        shards = []
        for n in small_names:
            full, w = small_grads[n], weights[n].shape[-1]
            shards.append(full[..., dv * w:(dv + 1) * w])
        per_dest.append(_pack(shards))
    small_parts = _exchange("scatter_small", jnp.stack(per_dest), True)

    out_g, out_d, out_m, out_v = {}, {}, {}, {}

    def update(name, grad, shape2d, jobs=()):
        w2, m2, v2 = (a[name].reshape(shape2d) for a in (weights, mom_m, mom_v))
        gg, dd, mm, vv = _adamw("adamw_" + name, w2, grad, m2, v2, jobs=jobs)
        shp = weights[name].shape
        out_g[name], out_d[name], out_m[name], out_v[name] = gg.reshape(shp), dd.reshape(shp), mm.reshape(shp), vv.reshape(shp)

    update("mod_w", g_mod_w.reshape(-1, n_mod), (-1, n_mod))
    update("sg_w_in", p_sg_in, (-1, sg_w_in.shape[2]))
    update("sg_w_out", p_sg_out, (-1, d))
    update("swa_w_in", p_swa_in, (-1, swa_w_in.shape[2]))
    update("swa_w_out", p_swa_out, (-1, d))
    update("rwkv_w_in", p_rw_in, (-1, n_rw))
    update("rwkv_w_out", p_rw_out, (-1, d))
    update("sg_w_spatial", g_w_sp.reshape(-1, 128), (-1, 128))
    w_pk, m_pk, v_pk = (_pack([a[n] for n in small_names]) for a in (weights, mom_m, mom_v))
    res = _adamw("adamw_small", w_pk, small_parts, m_pk, v_pk)
    for q, arrs in enumerate(zip(*[_unpack(r_.reshape(-1), small_shapes) for r_ in res])):
        out_g[small_names[q]], out_d[small_names[q]], out_m[small_names[q]], out_v[small_names[q]] = arrs
    rep_names = ["norm_g", "mod_b", "final_norm_g", "sg_b_spatial", "swa_sinks", "rwkv_r_k"]
    rep_grads = [g_norm_g, g_mod_b.reshape(mod_b.shape), g_final, g_b_sp, g_sinks, g_r_k]
    rep_shapes2 = [weights[n].shape for n in rep_names]
    w_pk, m_pk, v_pk = (_pack([a[n] for n in rep_names]) for a in (weights, mom_m, mom_v))
    res = _adamw("adamw_rep", w_pk, _pack(rep_grads), m_pk, v_pk)
    for q, arrs in enumerate(zip(*[_unpack(r_.reshape(-1), rep_shapes2) for r_ in res])):
        out_g[rep_names[q]], out_d[rep_names[q]], out_m[rep_names[q]], out_v[rep_names[q]] = arrs

    return (loss_tot.reshape(()), dx[None], *[out_g[n] for n in names], *[out_d[n] for n in names],
            *[out_m[n] for n in names], *[out_v[n] for n in names])
```

```python
import functools
import math

import jax
import jax.numpy as jnp
from jax import lax
from jax.experimental import pallas as pl
from jax.experimental.pallas import tpu as pltpu

F32, BF16 = jnp.float32, jnp.bfloat16
HI = lax.Precision.HIGHEST
S = jax.ShapeDtypeStruct
MESH = pl.DeviceIdType.MESH

N_DEV = 8
DEPTH = 4
HEAD = 64
SG_GROUPS = 16
SG_CHUNK = 128
SWA_BLOCK = 128
SWA_KV = 4
SWA_REP = 8
ROPE_THETA = 10000.0
LORA = 96
LORA_PAD = 256
RW_CHUNK = 64
RW_HEADS = 16
RW_HEADS_FWD = 32
RW_PREC = lax.Precision.HIGH
DECAY_SCALE = math.exp(-0.5)
GN_EPS = 64e-5
RMS_EPS = 1e-6
LN_EPS = 1e-5
NEG = -1e30
ADAM_LR, ADAM_B1, ADAM_B2, ADAM_EPS, ADAM_WD, ADAM_STEP = 0.001, 0.9, 0.999, 1e-08, 0.01, 10
VMEM_MB = 56


def _params(sem=None):
    kw = dict(vmem_limit_bytes=VMEM_MB << 20)
    if sem is not None:
        kw["dimension_semantics"] = sem
    return pltpu.CompilerParams(**kw)


def _pick(n, opts):
    for o in opts:
        if n % o == 0:
            return o
    raise ValueError(f"no tile for {n}")


def _rows(name, fn, rows, consts, out_rows, out_accs, tm, jobs=()):
    t = rows[0].shape[0]
    nr, nc, no = len(rows), len(consts), len(out_rows)

    def body(*refs):
        outs = fn(*[r[...] for r in refs[:nr + nc]])
        if not isinstance(outs, (tuple, list)):
            outs = (outs,)
        for r, o in zip(refs[nr + nc:nr + nc + no], outs[:no]):
            r[...] = o.astype(r.dtype)
        i = pl.program_id(0)
        for r, o in zip(refs[nr + nc + no:], outs[no:]):
            @pl.when(i == 0)
            def _(r=r, o=o):
                r[...] = o.astype(r.dtype)

            @pl.when(i > 0)
            def _(r=r, o=o):
                r[...] += o.astype(r.dtype)

    in_specs = [pl.BlockSpec((tm, a.shape[1]), lambda i: (i, 0)) for a in rows]
    in_specs += [pl.BlockSpec(c.shape, lambda i, nd=c.ndim: (0,) * nd) for c in consts]
    out_specs = [pl.BlockSpec((tm, n), lambda i: (i, 0)) for n, _ in out_rows]
    out_specs += [pl.BlockSpec(s, lambda i, nd=len(s): (0,) * nd) for s in out_accs]
    out_shape = [S((t, n), dt) for n, dt in out_rows] + [S(s, F32) for s in out_accs]
    return _pcall(body, grid=(t // tm,), in_specs=in_specs, out_specs=out_specs, out_shape=out_shape, name=name,
                  semantics=("arbitrary",), inputs=(*rows, *consts), jobs=jobs)


_DN = {"nn": (((1,), (0,)), ((), ())), "nt": (((1,), (1,)), ((), ())), "tn": (((0,), (0,)), ((), ()))}


def _mm(name, a, b, mode, out_dtype, add=None, resid=None, jobs=()):
    if mode == "nn":
        (m, k), (_, n) = a.shape, b.shape
    elif mode == "nt":
        (m, k), (n, _) = a.shape, b.shape
    else:
        (k, m), (_, n) = a.shape, b.shape
    wide = mode != "tn" and add is None and resid is None
    tm = _pick(m, (1024, 512, 256, 128))
    tn = _pick(n, ((1536,) if wide and n % 1024 else ()) + (1024, 512, 384, 256, 128))
    long_k = (4096,) if mode == "tn" else (3072, 2304) if mode == "nt" and add is None else ()
    tk = _pick(k, long_k + (2048, 1536, 1024, 512, 384, 256, 128))
    nk = k // tk
    n_extra = (add is not None) + 2 * (resid is not None)

    def body(*refs):
        a_ref, b_ref = refs[0], refs[1]
        extra, outs, acc = refs[2:2 + n_extra], refs[2 + n_extra:-1], refs[-1]
        kk = pl.program_id(2)
        prod = lax.dot_general(a_ref[...].astype(BF16), b_ref[...].astype(BF16), _DN[mode], preferred_element_type=F32)
        if add is not None:
            prod = jnp.where(kk == 0, prod + extra[0][...].astype(F32), prod) if nk > 1 else prod + extra[0][...].astype(F32)

        def finish(total):
            outs[0][...] = total.astype(outs[0].dtype)
            if resid is not None:
                outs[1][...] = extra[-2][...] + extra[-1][...] * total

        if nk == 1:
            finish(prod)
            return

        @pl.when(kk == 0)
        def _():
            acc[...] = prod

        @pl.when(kk > 0)
        def _():
            acc[...] += prod

        @pl.when(kk == nk - 1)
        def _():
            finish(acc[...])

    a_spec = pl.BlockSpec((tk, tm), lambda i, j, q: (q, i)) if mode == "tn" else pl.BlockSpec((tm, tk), lambda i, j, q: (i, q))
    b_spec = pl.BlockSpec((tn, tk), lambda i, j, q: (j, q)) if mode == "nt" else pl.BlockSpec((tk, tn), lambda i, j, q: (q, j))
    o_spec = pl.BlockSpec((tm, tn), lambda i, j, q: (i, j))
    ins, specs, out_specs, out_shape = [a, b], [a_spec, b_spec], [o_spec], [S((m, n), out_dtype)]
    if add is not None:
        ins.append(add)
        specs.append(o_spec)
    if resid is not None:
        ins += list(resid)
        specs += [o_spec, pl.BlockSpec((1, tn), lambda i, j, q: (0, j))]
        out_specs.append(o_spec)
        out_shape.append(S((m, n), F32))
    res = _pcall(body, grid=(m // tm, n // tn, nk), in_specs=specs, out_specs=out_specs, out_shape=out_shape,
                 scratch_shapes=[pltpu.VMEM((tm, tn), F32)], name=name, semantics=("parallel", "parallel", "arbitrary"),
                 inputs=ins, jobs=jobs)
    return res if resid is not None else res[0]


def _exchange(name, src, scatter):
    blk = src.shape[1:] if scatter else src.shape

    def body(src_ref, dst_ref, send_sems, recv_sems, loc_sem):
        x, y, c = lax.axis_index("x"), lax.axis_index("y"), lax.axis_index("c")
        me = 4 * x + 2 * y + c

        def mine(d):
            return src_ref.at[d] if scatter else src_ref

        local = pltpu.make_async_copy(mine(me), dst_ref.at[me], loc_sem)
        local.start()
        sends, peers = [], []
        for k in range(1, N_DEV):
            px = 1 - x if k & 4 else x
            py = 1 - y if k & 2 else y
            pc = 1 - c if k & 1 else c
            pid = 4 * px + 2 * py + pc
            cp = pltpu.make_async_remote_copy(src_ref=mine(pid), dst_ref=dst_ref.at[me], send_sem=send_sems.at[k - 1],
                                              recv_sem=recv_sems.at[k - 1], device_id=(px, py, pc), device_id_type=MESH)
            cp.start()
            sends.append(cp)
            peers.append((pid, (px, py, pc)))
        for k in range(1, N_DEV):
            pid, dev = peers[k - 1]
            pltpu.make_async_remote_copy(src_ref=mine(pid), dst_ref=dst_ref.at[pid], send_sem=send_sems.at[k - 1],
                                         recv_sem=recv_sems.at[k - 1], device_id=dev, device_id_type=MESH).wait_recv()
        for cp in sends:
            cp.wait_send()
        local.wait()

    return pl.pallas_call(
        body, out_shape=S((N_DEV,) + tuple(blk), src.dtype),
        in_specs=[pl.BlockSpec(memory_space=pl.ANY)], out_specs=pl.BlockSpec(memory_space=pl.ANY),
        scratch_shapes=[pltpu.SemaphoreType.DMA((N_DEV - 1,)), pltpu.SemaphoreType.DMA((N_DEV - 1,)),
                        pltpu.SemaphoreType.DMA],
        name=name)(src)


class _Gather:
    @staticmethod
    def out_shape(src):
        return S((N_DEV,) + tuple(src.shape), src.dtype)

    scratch = (pltpu.SemaphoreType.DMA((N_DEV - 1,)), pltpu.SemaphoreType.DMA((N_DEV - 1,)), pltpu.SemaphoreType.DMA)

    def __init__(self, src_ref, dst_ref, send_sems, recv_sems, loc_sem):
        self.refs = (src_ref, dst_ref, send_sems, recv_sems, loc_sem)
        x, y, c = lax.axis_index("x"), lax.axis_index("y"), lax.axis_index("c")
        self.c, self.me, self.sibling = c, (x, y, c), (x, y, 1 - c)
        self.chips = [(1 - x, y), (x, 1 - y), (1 - x, 1 - y)]

    def rows(self, px, py, pc):
        return self.refs[1].at[4 * px + 2 * py + pc]

    def copy(self, k, block, to, own=False):
        src_ref, _, send_sems, recv_sems, _ = self.refs
        return pltpu.make_async_remote_copy(src_ref=src_ref if own else self.rows(*block), dst_ref=self.rows(*block),
                                            send_sem=send_sems.at[k], recv_sem=recv_sems.at[k], device_id=to,
                                            device_id_type=MESH)

    def local(self):
        return pltpu.make_async_copy(self.refs[0], self.rows(*self.me), self.refs[4])

    def first(self):
        return [self.copy(0, self.me, self.sibling, own=True)] + [self.copy(1 + j, self.me, (*chip, self.c), own=True)
                                                                  for j, chip in enumerate(self.chips)]

    def start(self):
        self.local().start()
        for cp in self.first():
            cp.start()

    def finish(self):
        c = self.c
        passed = [self.copy(4 + j, (*chip, c), self.sibling) for j, chip in enumerate(self.chips)]
        for j, chip in enumerate(self.chips):
            self.copy(1 + j, (*chip, c), self.me).wait_recv()
            passed[j].start()
        self.copy(0, self.sibling, self.me).wait_recv()
        for j, chip in enumerate(self.chips):
            self.copy(4 + j, (*chip, 1 - c), self.me).wait_recv()
        for cp in self.first() + passed:
            cp.wait_send()
        self.local().wait()


class _Chips:
    @staticmethod
    def out_shape(src):
        return S(src.shape, src.dtype)

    scratch = (pltpu.SemaphoreType.DMA((N_DEV // 2 - 1,)), pltpu.SemaphoreType.DMA((N_DEV // 2 - 1,)), pltpu.SemaphoreType.DMA)

    def __init__(self, src_ref, dst_ref, send_sems, recv_sems, loc_sem):
        self.refs = (src_ref, dst_ref, send_sems, recv_sems, loc_sem)
        x, y, c = lax.axis_index("x"), lax.axis_index("y"), lax.axis_index("c")
        self.c, self.mine = c, 2 * x + y
        self.chips = [(1 - x, y), (x, 1 - y), (1 - x, 1 - y)]

    def local(self):
        src_ref, dst_ref, _, _, loc_sem = self.refs
        return pltpu.make_async_copy(src_ref.at[self.mine], dst_ref.at[self.mine], loc_sem)

    def send(self, j):
        src_ref, dst_ref, send_sems, recv_sems, _ = self.refs
        px, py = self.chips[j]
        return pltpu.make_async_remote_copy(src_ref=src_ref.at[2 * px + py], dst_ref=dst_ref.at[self.mine],
                                            send_sem=send_sems.at[j], recv_sem=recv_sems.at[j],
                                            device_id=(px, py, self.c), device_id_type=MESH)

    def arrival(self, j):
        src_ref, dst_ref, send_sems, recv_sems, _ = self.refs
        px, py = self.chips[j]
        return pltpu.make_async_remote_copy(src_ref=src_ref.at[self.mine], dst_ref=dst_ref.at[2 * px + py],
                                            send_sem=send_sems.at[j], recv_sem=recv_sems.at[j],
                                            device_id=(px, py, self.c), device_id_type=MESH)

    def start(self):
        self.local().start()
        for j in range(len(self.chips)):
            self.send(j).start()

    def finish(self):
        for j in range(len(self.chips)):
            self.arrival(j).wait_recv()
        for j in range(len(self.chips)):
            self.send(j).wait_send()
        self.local().wait()


def _exchange_call(name, cls, src):
    def body(*refs):
        ex = cls(*refs)
        ex.start()
        ex.finish()

    return pl.pallas_call(body, out_shape=cls.out_shape(src), in_specs=[pl.BlockSpec(memory_space=pl.ANY)],
                          out_specs=pl.BlockSpec(memory_space=pl.ANY), scratch_shapes=list(cls.scratch), name=name)(src)


def _gather(name, src):
    return _exchange_call(name, _Gather, src)


def _pcall(body, *, grid, in_specs, out_specs, out_shape, scratch_shapes=(), name, semantics, inputs, jobs=()):
    if not jobs:
        return pl.pallas_call(body, grid=grid, in_specs=in_specs, out_specs=out_specs, out_shape=out_shape,
                              scratch_shapes=list(scratch_shapes), name=name, compiler_params=_params(semantics))(*inputs)
    n_in, n_out, n_scr, nj = len(in_specs), len(out_specs), len(scratch_shapes), len(jobs)

    def hosted(*refs):
        ins, srcs = refs[:n_in], refs[n_in:n_in + nj]
        outs, dsts = refs[n_in + nj:n_in + nj + n_out], refs[n_in + nj + n_out:n_in + 2 * nj + n_out]
        scr, sems = refs[n_in + 2 * nj + n_out:n_in + 2 * nj + n_out + n_scr], refs[n_in + 2 * nj + n_out + n_scr:]
        ids = [pl.program_id(q) for q in range(len(grid))]
        first = functools.reduce(jnp.logical_and, [i == 0 for i in ids])
        last = functools.reduce(jnp.logical_and, [i == g - 1 for i, g in zip(ids, grid)])
        make = lambda q: jobs[q]["cls"](srcs[q], dsts[q], *sems[3 * q:3 * q + 3])

        @pl.when(first)
        def _():
            for q in range(nj):
                make(q).start()

        body(*ins, *outs, *scr)

        @pl.when(last)
        def _():
            for q in range(nj):
                make(q).finish()

    anyspec = pl.BlockSpec(memory_space=pl.ANY)
    res = pl.pallas_call(
        hosted, grid=grid, in_specs=list(in_specs) + [anyspec] * nj, out_specs=list(out_specs) + [anyspec] * nj,
        out_shape=list(out_shape) + [j["cls"].out_shape(j["src"]) for j in jobs],
        scratch_shapes=list(scratch_shapes) + [s for j in jobs for s in j["cls"].scratch], name=name,
        compiler_params=_params(("arbitrary",) * len(grid)))(*inputs, *[j["src"] for j in jobs])
    for j, out in zip(jobs, res[n_out:]):
        j["out"] = out
    return res[:n_out]


def _scatter_pairs(name, parts):
    _, r, c_ = parts.shape
    n_chip = N_DEV // 2

    def stage1(src_ref, dst_ref, send_sems, recv_sems):
        x, y, c = lax.axis_index("x"), lax.axis_index("y"), lax.axis_index("c")
        sends = []
        for q in range(n_chip):
            cp = pltpu.make_async_remote_copy(src_ref=src_ref.at[2 * q + 1 - c], dst_ref=dst_ref.at[q],
                                              send_sem=send_sems.at[q], recv_sem=recv_sems.at[q],
                                              device_id=(x, y, 1 - c), device_id_type=MESH)
            cp.start()
            sends.append(cp)
        for q in range(n_chip):
            pltpu.make_async_remote_copy(src_ref=src_ref.at[2 * q + c], dst_ref=dst_ref.at[q], send_sem=send_sems.at[q],
                                         recv_sem=recv_sems.at[q], device_id=(x, y, 1 - c), device_id_type=MESH).wait_recv()
        for cp in sends:
            cp.wait_send()

    from_sibling = pl.pallas_call(
        stage1, out_shape=S((n_chip, r, c_), parts.dtype),
        in_specs=[pl.BlockSpec(memory_space=pl.ANY)], out_specs=pl.BlockSpec(memory_space=pl.ANY),
        scratch_shapes=[pltpu.SemaphoreType.DMA((n_chip,)), pltpu.SemaphoreType.DMA((n_chip,))], name=name + "_pair")(parts)

    tm = _pick(r, (512, 256, 128, 64, 32, 16, 8)) if r % 8 == 0 else r
    core = lax.axis_index("c").astype(jnp.int32).reshape(1)

    def pair_sum(core_ref, mine_ref, sib_ref, o_ref):
        o_ref[...] = (mine_ref[0].astype(F32) + sib_ref[...].astype(F32)).astype(o_ref.dtype)

    pair = pl.pallas_call(
        pair_sum, out_shape=S((n_chip, r, c_), parts.dtype),
        grid_spec=pltpu.PrefetchScalarGridSpec(
            num_scalar_prefetch=1, grid=(n_chip, r // tm),
            in_specs=[pl.BlockSpec((1, 1, tm, c_), lambda q, i, core_ref: (q, core_ref[0], i, 0)),
                      pl.BlockSpec((1, tm, c_), lambda q, i, core_ref: (q, i, 0))],
            out_specs=pl.BlockSpec((1, tm, c_), lambda q, i, core_ref: (q, i, 0))),
        name=name + "_sum", compiler_params=_params(("parallel", "parallel")))(
            core, parts.reshape(n_chip, 2, r, c_), from_sibling)

    return pair


def _sum_parts(name, parts):
    n_parts, r, c = parts.shape
    tm = _pick(r, (512, 256, 128, 64, 32, 16, 8)) if r % 8 == 0 else r

    def body(p_ref, o_ref):
        acc = p_ref[0].astype(F32)
        for d in range(1, n_parts):
            acc = acc + p_ref[d].astype(F32)
        o_ref[...] = acc

    return pl.pallas_call(body, grid=(r // tm,), in_specs=[pl.BlockSpec((n_parts, tm, c), lambda i: (0, i, 0))],
                          out_specs=pl.BlockSpec((tm, c), lambda i: (i, 0)), out_shape=S((r, c), F32), name=name,
                          compiler_params=_params(("parallel",)))(parts)


def _adamw(name, w, g, m, v, jobs=()):
    r, c = w.shape
    parts = g.ndim == 3
    n_parts = g.shape[0] if parts else 1
    tile_rows = max(8, (2 << 20) // (4 * c))
    tm = _pick(r, tuple(q for q in (2048, 1024, 512, 256, 128, 64, 32, 16, 8) if q <= tile_rows)) if r % 8 == 0 else r

    def body(w_ref, g_ref, m_ref, v_ref, go_ref, d_ref, mo_ref, vo_ref):
        if parts:
            gg = g_ref[0].astype(F32)
            for d in range(1, n_parts):
                gg = gg + g_ref[d].astype(F32)
        else:
            gg = g_ref[...]
        mm = ADAM_B1 * m_ref[...] + (1.0 - ADAM_B1) * gg
        vv = ADAM_B2 * v_ref[...] + (1.0 - ADAM_B2) * jnp.square(gg)
        m_hat = mm / (1.0 - ADAM_B1 ** ADAM_STEP)
        v_hat = vv / (1.0 - ADAM_B2 ** ADAM_STEP)
        go_ref[...] = gg
        d_ref[...] = -ADAM_LR * (m_hat / (jnp.sqrt(v_hat) + ADAM_EPS) + ADAM_WD * w_ref[...])
        mo_ref[...] = mm
        vo_ref[...] = vv

    spec = pl.BlockSpec((tm, c), lambda i: (i, 0))
    g_spec = pl.BlockSpec((n_parts, tm, c), lambda i: (0, i, 0)) if parts else spec
    return _pcall(body, grid=(r // tm,), in_specs=[spec, g_spec, spec, spec], out_specs=[spec] * 4,
                  out_shape=[S((r, c), F32)] * 4, name=name, semantics=("parallel",), inputs=(w, g, m, v), jobs=jobs)


def _rms(x, g):
    return x * lax.rsqrt(jnp.mean(x * x, axis=-1, keepdims=True) + RMS_EPS) * g


def _adaln(x, g, shift, scale):
    return _rms(x, g) * (1.0 + scale) + shift


def _dot(a, b, dn="nn", hi=False, prec=None):
    if hi or prec is not None:
        return lax.dot_general(a, b, _DN[dn], precision=HI if hi else prec, preferred_element_type=F32)
    return lax.dot_general(a.astype(BF16), b.astype(BF16), _DN[dn], preferred_element_type=F32)


def _sg_mix(p, ln_g, ln_b, w_s, b_st):
    d = p.shape[1] // 3
    gd = d // SG_GROUPS
    u = jax.nn.gelu(p[:, :d])
    vf = jax.nn.gelu(p[:, d:2 * d])
    z = p[:, 2 * d:]
    mean = jnp.mean(vf, axis=-1, keepdims=True)
    var = jnp.mean(jnp.square(vf - mean), axis=-1, keepdims=True)
    vn = (vf - mean) * lax.rsqrt(var + LN_EPS) * ln_g + ln_b
    row = lax.broadcasted_iota(jnp.int32, (SG_CHUNK, SG_CHUNK), 0)
    col = lax.broadcasted_iota(jnp.int32, (SG_CHUNK, SG_CHUNK), 1)
    fs = []
    for g in range(SG_GROUPS):
        w = jnp.where(row >= col, w_s[g], 0.0)
        fs.append(_dot(w, vn[:, g * gd:(g + 1) * gd]))
    sel = (lax.broadcasted_iota(jnp.int32, (SG_GROUPS, d), 1) // gd
           == lax.broadcasted_iota(jnp.int32, (SG_GROUPS, d), 0)).astype(F32)
    f = jnp.concatenate(fs, axis=1) + _dot(b_st, sel, hi=True)
    return u * f * jax.nn.silu(z)


def _rot_half(x):
    n = x.shape[1]
    lane = lax.broadcasted_iota(jnp.int32, x.shape, 1)
    return jnp.where(lane % HEAD < HEAD // 2, -pltpu.roll(x, n - HEAD // 2, 1), pltpu.roll(x, HEAD // 2, 1))


def _rope(x, cos, sin, sign):
    reps = x.shape[1] // cos.shape[1]
    return x * jnp.tile(cos, (1, reps)) + sign * _rot_half(x) * jnp.tile(sin, (1, reps))


def _attn_block(q, kp, kc, vp, vc, sink, prev_bias):
    each = lambda f, *ls: [f(*xs) for xs in zip(*ls)]
    r = sink[0].shape[0]
    cur, prob, _ = _attn_probs(q, kp, kc, sink, prev_bias)
    flat = lambda x: x.reshape(r * SWA_BLOCK, SWA_BLOCK)
    pc, pp = each(lambda p: flat(jnp.where(cur, p, 0.0)), prob), each(lambda p: flat(jnp.where(cur, 0.0, p)), prob)
    return each(lambda a, va, b, vb: _dot(a, va) + _dot(b, vb), pp, vp, pc, vc)


def _attn_probs(q, kp, kc, sink, prev_bias):
    each = lambda f, *ls: [f(*xs) for xs in zip(*ls)]
    r = sink[0].shape[0]
    scores = lambda a, b: (_dot(a, b, "nt") * (HEAD ** -0.5)).reshape(r, SWA_BLOCK, SWA_BLOCK)
    sp, sc = each(scores, q, kp), each(scores, q, kc)
    cur = (lax.broadcasted_iota(jnp.int32, (r, SWA_BLOCK, SWA_BLOCK), 2)
           <= lax.broadcasted_iota(jnp.int32, (r, SWA_BLOCK, SWA_BLOCK), 1))
    s = each(lambda a, b: jnp.where(cur, b, a + prev_bias), sp, sc)
    m = each(lambda a, sk: jnp.maximum(jnp.max(a, axis=-1, keepdims=True), sk), s, sink)
    e, es = each(lambda a, m_: jnp.exp(a - m_), s, m), each(lambda sk, m_: jnp.exp(sk - m_), sink, m)
    denom = each(lambda a, b: jnp.sum(a, axis=-1, keepdims=True) + b, e, es)
    return cur, each(lambda a, dn: a / dn, e, denom), each(lambda a, dn: a / dn, es, denom)


def _attn_block_bwd(q, kp, kc, vp, vc, sink, do, prev_bias):
    each = lambda f, *ls: [f(*xs) for xs in zip(*ls)]
    r = sink[0].shape[0]
    scale = HEAD ** -0.5
    cube = lambda x: x.reshape(r, SWA_BLOCK, SWA_BLOCK)
    flat = lambda x: x.reshape(r * SWA_BLOCK, SWA_BLOCK)
    cur, prob, p_sink = _attn_probs(q, kp, kc, sink, prev_bias)
    pc, pp = each(lambda p: flat(jnp.where(cur, p, 0.0)), prob), each(lambda p: flat(jnp.where(cur, 0.0, p)), prob)
    dprob = each(lambda g, va, vb: jnp.where(cur, cube(_dot(g, vb, "nt")), cube(_dot(g, va, "nt"))), do, vp, vc)
    dvp, dvc = each(lambda p, g: _dot(p, g, "tn"), pp, do), each(lambda p, g: _dot(p, g, "tn"), pc, do)
    delta = each(lambda p, dp: jnp.sum(p * dp, axis=-1, keepdims=True), prob, dprob)
    ds = each(lambda p, dp, dl: p * (dp - dl), prob, dprob, delta)
    dsc, dsp = each(lambda x: flat(jnp.where(cur, x, 0.0)), ds), each(lambda x: flat(jnp.where(cur, 0.0, x)), ds)
    dsink = each(lambda ps, dl: -jnp.sum(ps * dl, axis=1, keepdims=True), p_sink, delta)
    dq = each(lambda a, ka, b, kb: (_dot(a, ka) + _dot(b, kb)) * scale, dsp, kp, dsc, kc)
    dkp, dkc = each(lambda a, q_: _dot(a, q_, "tn") * scale, dsp, q), each(lambda a, q_: _dot(a, q_, "tn") * scale, dsc, q)
    return dq, dkp, dkc, dvp, dvc, dsink


def _rwkv_chunk(s0, r, k, v, logw, a, k_k, k_a, r_k, gn_g, gn_b):
    c = r[0].shape[0]
    each = lambda f, *ls: [f(*xs) for xs in zip(*ls)]
    gram = functools.partial(_dot, prec=RW_PREC)
    row = lax.broadcasted_iota(jnp.int32, (c, c), 0)
    col = lax.broadcasted_iota(jnp.int32, (c, c), 1)
    incl, strict = row >= col, row > col
    ones_l = incl.astype(F32)

    def unit(x):
        return x / jnp.maximum(jnp.sqrt(jnp.sum(x * x, axis=-1, keepdims=True)), 1e-12)

    kk = each(lambda k_, p: unit(k_ * p), k, k_k)
    km = each(lambda k_, a_, p: k_ * (1.0 + (a_ - 1.0) * p), k, a, k_a)
    b = each(lambda x, a_: x * a_, kk, a)
    first_half = lax.broadcasted_iota(jnp.int32, (c, HEAD), 0) < c // 2
    mid = each(lambda w: jnp.sum(jnp.where(first_half, w, 0.0), axis=0, keepdims=True), logw)
    cum = each(lambda w, m: _dot(ones_l, w, hi=True) - m, logw, mid)
    alpha = each(lambda x, cu, w: x * jnp.exp(cu - w), kk, cum, logw)
    beta = each(lambda x, cu: x * jnp.exp(-cu), b, cum)
    kap = each(lambda x, cu: x * jnp.exp(-cu), km, cum)
    rho = each(lambda x, cu: x * jnp.exp(cu), r, cum)
    s0 = each(lambda s, m: s * jnp.exp(m), s0, mid)
    lab = each(lambda x, y_: jnp.where(strict, gram(x, y_, "nt"), 0.0), alpha, beta)
    lak = each(lambda x, y_: jnp.where(strict, gram(x, y_, "nt"), 0.0), alpha, kap)
    xs = each(lambda al, s, l, v_: _dot(al, s, "nt") + _dot(l, v_), alpha, s0, lak, v)
    xs = each(lambda x, l: x - _dot(l, x), xs, lab)
    lp, power = lab, 2
    while power < c:
        lp = each(lambda l: _dot(l, l), lp)
        xs = each(lambda x, l: x + _dot(l, x), xs, lp)
        power *= 2
    u = each(lambda x: -x, xs)
    mrb = each(lambda x, y_: jnp.where(incl, gram(x, y_, "nt"), 0.0), rho, beta)
    mrk = each(lambda x, y_: jnp.where(incl, gram(x, y_, "nt"), 0.0), rho, kap)
    y = each(lambda rh, s, mb, u_, mk, v_: _dot(rh, s, "nt") + _dot(mb, u_) + _dot(mk, v_), rho, s0, mrb, u, mrk, v)
    s1 = each(lambda s, u_, be, v_, ka, w, m: (s + _dot(u_, be, "tn") + _dot(v_, ka, "tn"))
              * jnp.exp(jnp.sum(w, axis=0, keepdims=True) - m), s0, u, beta, v, kap, logw, mid)

    def finish(y_, g, bias, r_, km_, rk, v_):
        mean = jnp.mean(y_, axis=-1, keepdims=True)
        var = jnp.mean(jnp.square(y_ - mean), axis=-1, keepdims=True)
        y_ = (y_ - mean) * lax.rsqrt(var + GN_EPS) * g + bias
        return y_ + jnp.sum(r_ * km_ * rk, axis=-1, keepdims=True) * v_

    return each(finish, y, gn_g, gn_b, r, km, r_k, v), s1


def _norm_fwd(name, x, g, shift, scale, jobs=()):
    return _rows(name, lambda x_, g_, sh, sc: _adaln(x_, g_, sh, sc), [x], [g, shift, scale], [(x.shape[1], BF16)], [], 256,
                 jobs=jobs)[0]


def _norm_bwd(name, x, dh, dx_res, g, shift, scale, below=None):
    d = x.shape[1]

    def fn(x_, dh_, dr_, *rest):
        g_, sh, sc = rest[-4:-1] if below else rest
        _, vjp = jax.vjp(_adaln, x_, g_, sh, sc)
        dx, dg, dsh, dsc = vjp(dh_)
        dx = dx + dr_
        if not below:
            return dx, dg, dsh, dsc
        y_, gate_ = rest[0], rest[-1]
        return dx, gate_ * dx, dg, dsh, dsc, jnp.sum(dx * y_, axis=0, keepdims=True)

    if not below:
        return _rows(name, fn, [x, dh, dx_res], [g, shift, scale], [(d, F32)], [(1, d)] * 3, 256)
    return _rows(name, fn, [x, dh, dx_res, below[0]], [g, shift, scale, below[1]], [(d, F32), (d, BF16)], [(1, d)] * 4, 256)


def _sg_fwd(name, p, ln_g, ln_b, w_s, b_st):
    d = p.shape[1] // 3
    return _rows(name, _sg_mix, [p], [ln_g, ln_b, w_s, b_st], [(d, BF16)], [], SG_CHUNK)[0]


def _sg_bwd(name, p, dmix, ln_g, ln_b, w_s, b_st, jobs=()):
    def fn(p_, dm_, lg, lb, ws, bs):
        _, vjp = jax.vjp(_sg_mix, p_, lg, lb, ws, bs)
        return vjp(dm_)

    return _rows(name, fn, [p, dmix], [ln_g, ln_b, w_s, b_st], [(p.shape[1], BF16)],
                 [ln_g.shape, ln_b.shape, w_s.shape, b_st.shape], SG_CHUNK, jobs=jobs)


def _rope_tables(pos, inv_freq):
    ang = pos * inv_freq
    return jnp.cos(ang), jnp.sin(ang)


def _swa_pre(name, p, pos, inv_freq, d):
    kvw = SWA_KV * HEAD

    def fn(p_, pos_, fr):
        cos, sin = _rope_tables(pos_, fr)
        return (_rope(p_[:, :d], cos, sin, 1.0), _rope(p_[:, d:d + kvw], cos, sin, 1.0), p_[:, d + kvw:d + 2 * kvw])

    return _rows(name, fn, [p, pos], [inv_freq], [(d, BF16), (kvw, BF16), (kvw, BF16)], [], 256)


def _q_groups(ref, kv, rep):
    heads = _head_cols(ref, kv * rep)
    return [jnp.concatenate(heads[g * rep:(g + 1) * rep], axis=0) for g in range(kv)]


def _q_ungroup(groups, rep):
    return jnp.concatenate([g[h * SWA_BLOCK:(h + 1) * SWA_BLOCK] for g in groups for h in range(rep)], axis=1)


def _swa_attn_fwd(name, q, k, v, sinks, jobs=()):
    t, d = q.shape
    kv, rep = sinks.shape[0], sinks.shape[1]
    nb = t // SWA_BLOCK

    def body(q_ref, kp_ref, kc_ref, vp_ref, vc_ref, s_ref, o_ref):
        prev_bias = jnp.where(pl.program_id(0) > 0, 0.0, NEG).astype(F32)
        o = _attn_block(_q_groups(q_ref, kv, rep), *[_head_cols(ref, kv) for ref in (kp_ref, kc_ref, vp_ref, vc_ref)],
                        [s_ref[g] for g in range(kv)], prev_bias)
        o_ref[...] = _q_ungroup(o, rep)

    qs = pl.BlockSpec((SWA_BLOCK, d), lambda n: (n, 0))
    cur = pl.BlockSpec((SWA_BLOCK, kv * HEAD), lambda n: (n, 0))
    prev = pl.BlockSpec((SWA_BLOCK, kv * HEAD), lambda n: (jnp.maximum(n - 1, 0), 0))
    ss = pl.BlockSpec(sinks.shape, lambda n: (0, 0, 0, 0))
    return _pcall(body, grid=(nb,), in_specs=[qs, prev, cur, prev, cur, ss], out_specs=[qs], out_shape=[S((t, d), F32)],
                  name=name, semantics=("parallel",), inputs=(q, k, k, v, v, sinks), jobs=jobs)[0]


def _swa_attn_bwd(name, q, k, v, sinks, do, jobs=()):
    t, d = q.shape
    kv, rep = sinks.shape[0], sinks.shape[1]
    nb = t // SWA_BLOCK

    def body(q_ref, kp_ref, kc_ref, vp_ref, vc_ref, s_ref, do_ref, dq_ref, dkc_ref, dkp_ref, dvc_ref, dvp_ref, ds_ref):
        n = pl.program_id(0)
        prev_bias = jnp.where(n > 0, 0.0, NEG).astype(F32)
        args = [_q_groups(q_ref, kv, rep)] + [_head_cols(ref, kv) for ref in (kp_ref, kc_ref, vp_ref, vc_ref)]
        dq, dkp, dkc, dvp, dvc, ds = _attn_block_bwd(*args, [s_ref[g] for g in range(kv)], _q_groups(do_ref, kv, rep), prev_bias)
        dq_ref[...] = _q_ungroup(dq, rep)
        for ref, val in ((dkc_ref, dkc), (dkp_ref, dkp), (dvc_ref, dvc), (dvp_ref, dvp)):
            ref[...] = jnp.concatenate(val, axis=1)

        @pl.when(n == 0)
        def _():
            ds_ref[...] = jnp.zeros_like(ds_ref)

        for g in range(kv):
            ds_ref[g] += ds[g]

    qs = pl.BlockSpec((SWA_BLOCK, d), lambda n: (n, 0))
    cur = pl.BlockSpec((SWA_BLOCK, kv * HEAD), lambda n: (n, 0))
    prev = pl.BlockSpec((SWA_BLOCK, kv * HEAD), lambda n: (jnp.maximum(n - 1, 0), 0))
    ss = pl.BlockSpec(sinks.shape, lambda n: (0, 0, 0, 0))
    return _pcall(body, grid=(nb,), in_specs=[qs, prev, cur, prev, cur, ss, qs], out_specs=[qs, cur, cur, cur, cur, ss],
                  out_shape=[S((t, d), F32)] + [S((t, kv * HEAD), F32)] * 4 + [S(sinks.shape, F32)], name=name,
                  semantics=("arbitrary",), inputs=(q, k, k, v, v, sinks, do), jobs=jobs)


def _gate_fwd(name, o, z_src, z_off, d):
    return _rows(name, lambda o_, p_: o_ * jax.nn.silu(p_[:, z_off:z_off + d]), [o, z_src], [], [(d, BF16)], [], 256)[0]


def _gate_bwd(name, o, z_src, z_off, d, dmix):
    def fn(o_, p_, dm_):
        _, vjp = jax.vjp(lambda oo, zz: oo * jax.nn.silu(zz), o_, p_[:, z_off:z_off + d])
        return vjp(dm_)

    return _rows(name, fn, [o, z_src, dmix], [], [(d, F32), (d, F32)], [], 256)


def _swa_post_bwd(name, dq, dkc, dkp_up, dvc, dvp_up, dz, pos, inv_freq):
    def fn(dq_, dkc_, dkp_, dvc_, dvp_, dz_, pos_, fr):
        cos, sin = _rope_tables(pos_, fr)
        return jnp.concatenate([_rope(dq_, cos, sin, -1.0), _rope(dkc_ + dkp_, cos, sin, -1.0), dvc_ + dvp_, dz_], axis=1)

    n = dq.shape[1] + dkc.shape[1] + dvc.shape[1] + dz.shape[1]
    return _rows(name, fn, [dq, dkc, dkp_up, dvc, dvp_up, dz, pos], [inv_freq], [(n, BF16)], [], 256)[0]


HALO = 8


def _row_before(x, halo_ref, i):
    first = jnp.where(i > 0, halo_ref[pl.ds(HALO - 1, 1), :], 0.0)
    row = lax.broadcasted_iota(jnp.int32, x.shape, 0)
    return jnp.where(row == 0, first, pltpu.roll(x, 1, 0))


def _row_after(x, halo, i, n_tiles):
    last = jnp.where(i < n_tiles - 1, halo, 0.0)
    row = lax.broadcasted_iota(jnp.int32, x.shape, 0)
    return jnp.where(row == x.shape[0] - 1, last, pltpu.roll(x, x.shape[0] - 1, 0))


def _lerp_fwd(name, p, mu, widths):
    t, n = p.shape
    tm = 128

    def body(p_ref, halo_ref, mu_ref, *o_refs):
        x = p_ref[...]
        pm = x + (_row_before(x, halo_ref, pl.program_id(0)) - x) * mu_ref[...]
        o = 0
        for ref, w in zip(o_refs, widths):
            ref[...] = pm[:, o:o + w]
            o += w

    return pl.pallas_call(
        body, grid=(t // tm,),
        in_specs=[pl.BlockSpec((tm, n), lambda i: (i, 0)),
                  pl.BlockSpec((HALO, n), lambda i: (jnp.maximum(i * (tm // HALO) - 1, 0), 0)),
                  pl.BlockSpec((1, n), lambda i: (0, 0))],
        out_specs=[pl.BlockSpec((tm, w), lambda i: (i, 0)) for w in widths],
        out_shape=[S((t, w), F32) for w in widths], name=name, compiler_params=_params(("parallel",)))(p, p, mu)


def _lerp_bwd(name, dpm_parts, p, mu):
    t, n = p.shape
    k = len(dpm_parts)
    tm = 64
    n_tiles = t // tm

    def body(*refs):
        d_refs, dh_refs = refs[:k], refs[k:2 * k]
        p_ref, ph_ref, mu_ref, dp_ref, dmu_ref = refs[2 * k:]
        i = pl.program_id(0)
        cat = lambda vals: jnp.concatenate(vals, axis=1) if k > 1 else vals[0]
        dpm = cat([r[...] for r in d_refs])
        dnext = cat([r[pl.ds(0, 1), :] for r in dh_refs])
        x, mu_ = p_ref[...], mu_ref[...]
        dp_ref[...] = (dpm * (1.0 - mu_) + _row_after(dpm, dnext, i, n_tiles) * mu_).astype(dp_ref.dtype)
        dmu = jnp.sum(dpm * (_row_before(x, ph_ref, i) - x), axis=0, keepdims=True)

        @pl.when(i == 0)
        def _():
            dmu_ref[...] = dmu

        @pl.when(i > 0)
        def _():
            dmu_ref[...] += dmu

    per = tm // HALO
    d_specs = [pl.BlockSpec((tm, a.shape[1]), lambda i: (i, 0)) for a in dpm_parts]
    dh_specs = [pl.BlockSpec((HALO, a.shape[1]), lambda i: (jnp.minimum((i + 1) * per, t // HALO - 1), 0)) for a in dpm_parts]
    return pl.pallas_call(
        body, grid=(n_tiles,),
        in_specs=d_specs + dh_specs + [pl.BlockSpec((tm, n), lambda i: (i, 0)),
                                       pl.BlockSpec((HALO, n), lambda i: (jnp.maximum(i * per - 1, 0), 0)),
                                       pl.BlockSpec((1, n), lambda i: (0, 0))],
        out_specs=[pl.BlockSpec((tm, n), lambda i: (i, 0)), pl.BlockSpec((1, n), lambda i: (0, 0))],
        out_shape=[S((t, n), BF16), S((1, n), F32)], name=name,
        compiler_params=_params(("arbitrary",)))(*dpm_parts, *dpm_parts, p, p, mu)


def _lora_act(pl_, w0, w_lora, a0, a_lora):
    logw = -DECAY_SCALE * jax.nn.sigmoid(w0 + _dot(jnp.tanh(pl_), w_lora))
    a = jax.nn.sigmoid(a0 + _dot(pl_, a_lora))
    return logw, a


def _lora_fwd(name, pl_, w0, w_lora, a0, a_lora):
    d = w0.shape[1]
    return _rows(name, _lora_act, [pl_], [w0, w_lora, a0, a_lora], [(d, F32), (d, F32)], [], 256)


def _lora_bwd(name, pl_, dlogw, da, w0, w_lora, a0, a_lora):
    def fn(p_, dl_, da_, w0_, wl_, a0_, al_):
        _, vjp = jax.vjp(_lora_act, p_, w0_, wl_, a0_, al_)
        return vjp((dl_, da_))

    return _rows(name, fn, [pl_, dlogw, da], [w0, w_lora, a0, a_lora], [(pl_.shape[1], F32)],
                 [w0.shape, w_lora.shape, a0.shape, a_lora.shape], 256)


def _head_cols(ref, hb):
    x = ref[...].astype(F32)
    xo = pltpu.roll(x, x.shape[1] - HEAD, 1)
    return [(x if j % 2 == 0 else xo)[:, 2 * HEAD * (j // 2):2 * HEAD * (j // 2) + HEAD] for j in range(hb)]


def _rwkv_scan_fwd(name, r, k, v, logw, a, hp, jobs=()):
    t, d = r.shape
    h, nc, hb = d // HEAD, t // RW_CHUNK, RW_HEADS_FWD

    def body(r_ref, k_ref, v_ref, w_ref, a_ref, kk_ref, ka_ref, rk_ref, gg_ref, gb_ref, y_ref, st_ref, s_scr):
        @pl.when(pl.program_id(1) == 0)
        def _():
            s_scr[...] = jnp.zeros_like(s_scr)

        s0 = [s_scr[j] for j in range(hb)]
        for j in range(hb):
            st_ref[j, 0] = s0[j]
        y, s1 = _rwkv_chunk(s0, *[_head_cols(ref, hb) for ref in (r_ref, k_ref, v_ref, w_ref, a_ref, kk_ref, ka_ref, rk_ref,
                                                                 gg_ref, gb_ref)])
        y_ref[...] = jnp.concatenate(y, axis=1)
        for j in range(hb):
            s_scr[j] = s1[j]

    seq = pl.BlockSpec((RW_CHUNK, hb * HEAD), lambda i, n: (n, i))
    par = pl.BlockSpec((1, hb * HEAD), lambda i, n: (0, i))
    st = pl.BlockSpec((hb, 1, HEAD, HEAD), lambda i, n: (i, n, 0, 0))
    return _pcall(body, grid=(h // hb, nc), in_specs=[seq] * 5 + [par] * 5, out_specs=[seq, st],
                  out_shape=[S((t, d), F32), S((h, nc, HEAD, HEAD), F32)], scratch_shapes=[pltpu.VMEM((hb, HEAD, HEAD), F32)],
                  name=name, semantics=("parallel", "arbitrary"), inputs=(r, k, v, logw, a, *hp), jobs=jobs)


def _rwkv_scan_bwd(name, r, k, v, logw, a, hp, states, dy, jobs=()):
    t, d = r.shape
    h, nc, hb = d // HEAD, t // RW_CHUNK, RW_HEADS

    def body(r_ref, k_ref, v_ref, w_ref, a_ref, kk_ref, ka_ref, rk_ref, gg_ref, gb_ref, st_ref, dy_ref,
             dr_ref, dk_ref, dv_ref, dw_ref, da_ref, dkk_ref, dka_ref, drk_ref, dgg_ref, dgb_ref, ds_scr):
        n = pl.program_id(1)

        @pl.when(n == 0)
        def _():
            ds_scr[...] = jnp.zeros_like(ds_scr)
            for ref in (dkk_ref, dka_ref, drk_ref, dgg_ref, dgb_ref):
                ref[...] = jnp.zeros_like(ref)

        ins = [[st_ref[j, 0] for j in range(hb)]] + [_head_cols(ref, hb) for ref in (r_ref, k_ref, v_ref, w_ref, a_ref, kk_ref,
                                                                                  ka_ref, rk_ref, gg_ref, gb_ref)]
        _, vjp = jax.vjp(_rwkv_chunk, *ins)
        ds0, *dseq, dkk, dka, drk, dgg, dgb = vjp((_head_cols(dy_ref, hb), [ds_scr[j] for j in range(hb)]))
        for j in range(hb):
            ds_scr[j] = ds0[j]
        for ref, val in zip((dr_ref, dk_ref, dv_ref, dw_ref, da_ref), dseq):
            ref[...] = jnp.concatenate(val, axis=1)
        for ref, val in ((dkk_ref, dkk), (dka_ref, dka), (drk_ref, drk), (dgg_ref, dgg), (dgb_ref, dgb)):
            ref[...] += jnp.concatenate(val, axis=1)

    seq = pl.BlockSpec((RW_CHUNK, hb * HEAD), lambda i, n: (nc - 1 - n, i))
    par = pl.BlockSpec((1, hb * HEAD), lambda i, n: (0, i))
    st = pl.BlockSpec((hb, 1, HEAD, HEAD), lambda i, n: (i, nc - 1 - n, 0, 0))
    return _pcall(body, grid=(h // hb, nc), in_specs=[seq] * 5 + [par] * 5 + [st, seq], out_specs=[seq] * 5 + [par] * 5,
                  out_shape=[S((t, d), F32)] * 5 + [S((1, d), F32)] * 5, scratch_shapes=[pltpu.VMEM((hb, HEAD, HEAD), F32)],
                  name=name, semantics=("parallel", "arbitrary"), inputs=(r, k, v, logw, a, *hp, states, dy), jobs=jobs)


def _loss_head(name, x, target, g, y, gate):
    d = x.shape[1]

    def fn(x_, t_, y_, g_, gate_):
        def f(xx, gg):
            err = _rms(xx, gg) - t_
            return 0.5 * jnp.sum(jnp.mean(err * err, axis=-1, keepdims=True), axis=0, keepdims=True)

        l, vjp = jax.vjp(f, x_, g_)
        dx, dg = vjp(jnp.ones((1, 1), F32))
        return dx, gate_ * dx, dg, jnp.broadcast_to(l, (1, 128)), jnp.sum(dx * y_, axis=0, keepdims=True)

    return _rows(name, fn, [x, target, y], [g, gate], [(d, F32), (d, BF16)], [(1, d), (1, 128), (1, d)], 256)


def _mod_fwd(name, cond_all, mod_w, mod_b_cols):
    l, d, n = mod_w.shape

    def body(c_ref, w_ref, b_ref, o_ref):
        o_ref[0] = _dot(jax.nn.silu(c_ref[...]), w_ref[0], hi=True) + b_ref[0]

    return pl.pallas_call(body, grid=(l,), in_specs=[pl.BlockSpec((N_DEV, d), lambda i: (0, 0)),
                                                      pl.BlockSpec((1, d, n), lambda i: (i, 0, 0)),
                                                      pl.BlockSpec((1, 1, n), lambda i: (i, 0, 0))],
                          out_specs=pl.BlockSpec((1, N_DEV, n), lambda i: (i, 0, 0)), out_shape=S((l, N_DEV, n), F32),
                          name=name, compiler_params=_params(("parallel",)))(cond_all, mod_w, mod_b_cols)


def _mod_bwd(name, cond_all, dmod_cols, dmod_all):
    l, _, n = dmod_cols.shape
    d = cond_all.shape[1]
    nb = dmod_all.shape[2]

    def body(c_ref, dc_ref, da_ref, gw_ref, gb_ref):
        gw_ref[0] = _dot(jax.nn.silu(c_ref[...]), dc_ref[0], "tn", hi=True)
        acc = da_ref[0, 0:1, :]
        for bi in range(1, N_DEV):
            acc = acc + da_ref[0, bi:bi + 1, :]
        gb_ref[0] = acc

    return pl.pallas_call(body, grid=(l,), in_specs=[pl.BlockSpec((N_DEV, d), lambda i: (0, 0)),
                                                      pl.BlockSpec((1, N_DEV, n), lambda i: (i, 0, 0)),
                                                      pl.BlockSpec((1, N_DEV, nb), lambda i: (i, 0, 0))],
                          out_specs=[pl.BlockSpec((1, d, n), lambda i: (i, 0, 0)), pl.BlockSpec((1, 1, nb), lambda i: (i, 0, 0))],
                          out_shape=[S((l, d, n), F32), S((l, 1, nb), F32)], name=name,
                          compiler_params=_params(("parallel",)))(cond_all, dmod_cols, dmod_all)


def _shift_up(a, n=1):
    return jnp.concatenate([a[n:], jnp.zeros_like(a[:n])], axis=0)


def _cols_full(g):
    return g.transpose(1, 0, 2).reshape(g.shape[1], -1)


def _cols_parts(full):
    r, n = full.shape
    return full.reshape(r, N_DEV, n // N_DEV).transpose(1, 0, 2)


def _pack(arrs, mult=1024):
    flat = jnp.concatenate([a.reshape(-1) for a in arrs])
    pad = (-flat.shape[0]) % mult
    return jnp.pad(flat, (0, pad)).reshape(-1, 128)


def _unpack(flat, shapes):
    out, o = [], 0
    for s in shapes:
        n = math.prod(s)
        out.append(flat[o:o + n].reshape(s))
        o += n
    return out


def _local_step(x, pos, target, mods, norm_g, final_norm_g, layer_weights, hooks=None, on_grads=None):
    t, d = x.shape
    hooks = hooks or {}
    jobs = lambda nm: hooks.get(nm, ())
    notify = on_grads or (lambda *a: None)
    kinds = [i % 3 for i in range(DEPTH)]
    inv_freq = (ROPE_THETA ** (-jnp.arange(HEAD // 2, dtype=F32) / (HEAD // 2)))
    inv_freq = jnp.tile(inv_freq, 128 // (HEAD // 2)).reshape(1, 128)
    saved = []
    for i, kind in enumerate(kinds):
        shift, scale, gate = (mods[i, q * d:(q + 1) * d].reshape(1, d) for q in range(3))
        g = norm_g[i].reshape(1, d)
        h = _norm_fwd(f"norm_fwd{i}", x, g, shift, scale, jobs=jobs(f"norm_fwd{i}"))
        lw = layer_weights(i)
        sv = dict(x=x, h=h, g=g, shift=shift, scale=scale, gate=gate, lw=lw)
        if kind == 0:
            p = _mm(f"sg_in{i}", h, lw["w_in"], "nn", F32, jobs=jobs(f"sg_in{i}"))
            mix = _sg_fwd(f"sg_mix{i}", p, lw["ln_g"], lw["ln_b"], lw["w_s"], lw["b_st"])
            sv.update(p=p)
        elif kind == 1:
            p = _mm(f"swa_in{i}", h, lw["w_in"], "nn", F32, jobs=jobs(f"swa_in{i}"))
            q, k, v = _swa_pre(f"swa_pre{i}", p, pos, inv_freq, d)
            o = _swa_attn_fwd(f"swa_attn{i}", q, k, v, lw["sinks"], jobs=jobs(f"swa_attn{i}"))
            mix = _gate_fwd(f"swa_gate{i}", o, p, d + 2 * SWA_KV * HEAD, d)
            sv.update(p=p, qkv=(q, k, v), o=o)
        else:
            pm = _mm(f"rw_in{i}", h, lw["w_main"], "nn", F32, jobs=jobs(f"rw_in{i}"))
            plo = _mm(f"rw_inl{i}", h, lw["w_lorain"], "nn", F32)
            r, k, v, z = _lerp_fwd(f"rw_lerp{i}", pm, lw["mu_main"], [d] * 4)
            (pll,) = _lerp_fwd(f"rw_lerpl{i}", plo, lw["mu_lora"], [LORA_PAD])
            logw, a = _lora_fwd(f"rw_lora{i}", pll, lw["w0"], lw["w_lora"], lw["a0"], lw["a_lora"])
            seqs = (r, k, v, logw, a)
            o, states = _rwkv_scan_fwd(f"rw_scan{i}", *seqs, lw["hp"], jobs=jobs(f"rw_scan{i}"))
            mix = _gate_fwd(f"rw_gate{i}", o, z, 0, d)
            sv.update(pm=pm, plo=plo, pll=pll, z=z, seqs=seqs, states=states, o=o)
        y, x = _mm(f"out{i}", mix, lw["w_out"], "nn", F32, resid=(x, gate), jobs=jobs(f"out{i}"))
        sv.update(mix=mix, y=y)
        saved.append(sv)

    top = saved[-1]
    dx, dy, d_final_g, loss, dgate = _loss_head("loss_head", x, target, final_norm_g.reshape(1, d), top["y"], top["gate"])

    grads = dict(norm_g=[None] * DEPTH, sg_w_in=[None] * 2, sg_w_out=[None] * 2, sg_ln_g=[None] * 2, sg_ln_b=[None] * 2,
                 sg_w_s=[None] * 2, sg_b_st=[None] * 2, final_norm_g=d_final_g)
    dmods = [None] * DEPTH
    for i in reversed(range(DEPTH)):
        kind, j, sv = kinds[i], i // 3, saved[i]
        lw = sv["lw"]
        d_w_out = _mm(f"out_dw{i}", sv["mix"], dy, "tn", BF16)
        if kind == 0:
            grads["sg_w_out"][j] = d_w_out
        else:
            grads[("swa_w_out", "rw_w_out")[kind - 1]] = d_w_out
        notify(i, "out", grads)
        dmix = _mm(f"out_dx{i}", dy, lw["w_out"], "nt", F32, jobs=jobs(f"out_dx{i}"))
        if kind == 0:
            dp, dlg, dlb, dws, dbs = _sg_bwd(f"sg_mix_bwd{i}", sv["p"], dmix, lw["ln_g"], lw["ln_b"], lw["w_s"], lw["b_st"],
                                            jobs=jobs(f"sg_mix_bwd{i}"))
            grads["sg_ln_g"][j], grads["sg_ln_b"][j], grads["sg_w_s"][j], grads["sg_b_st"][j] = dlg, dlb, dws, dbs
            grads["sg_w_in"][j] = _mm(f"sg_in_dw{i}", sv["h"], dp, "tn", BF16, jobs=jobs(f"sg_in_dw{i}"))
            notify(i, "in", grads)
            dh = _mm(f"sg_in_dx{i}", dp, lw["w_in"], "nt", F32, jobs=jobs(f"sg_in_dx{i}"))
        elif kind == 1:
            z_off = d + 2 * SWA_KV * HEAD
            do, dz = _gate_bwd(f"swa_gate_bwd{i}", sv["o"], sv["p"], z_off, d, dmix)
            dq, dkc, dkp, dvc, dvp, dsinks = _swa_attn_bwd(f"swa_attn_bwd{i}", *sv["qkv"], lw["sinks"], do,
                                                           jobs=jobs(f"swa_attn_bwd{i}"))
            dkp, dvp = _shift_up(dkp, SWA_BLOCK), _shift_up(dvp, SWA_BLOCK)
            dp = _swa_post_bwd(f"swa_post_bwd{i}", dq, dkc, dkp, dvc, dvp, dz, pos, inv_freq)
            grads.update(swa_sinks=dsinks, swa_w_out=d_w_out)
            grads["swa_w_in"] = _mm(f"swa_in_dw{i}", sv["h"], dp, "tn", BF16, jobs=jobs(f"swa_in_dw{i}"))
            dh = _mm(f"swa_in_dx{i}", dp, lw["w_in"], "nt", F32, jobs=jobs(f"swa_in_dx{i}"))
        else:
            do, dz = _gate_bwd(f"rw_gate_bwd{i}", sv["o"], sv["z"], 0, d, dmix)
            res = _rwkv_scan_bwd(f"rw_scan_bwd{i}", *sv["seqs"], lw["hp"], sv["states"], do, jobs=jobs(f"rw_scan_bwd{i}"))
            dr, dk, dv, dlogw, da = res[:5]
            dpll, dw0, dwl, da0, dal = _lora_bwd(f"rw_lora_bwd{i}", sv["pll"], dlogw, da, lw["w0"], lw["w_lora"],
                                                  lw["a0"], lw["a_lora"])
            dpm, dmu_main = _lerp_bwd(f"rw_lerp_bwd{i}", [dr, dk, dv, dz], sv["pm"], lw["mu_main"])
            dpl, dmu_lora = _lerp_bwd(f"rw_lerpl_bwd{i}", [dpll], sv["plo"], lw["mu_lora"])
            grads.update(rw_w_out=d_w_out, rw_hp=res[5:], rw_w0=dw0, rw_w_lora=dwl, rw_a0=da0, rw_a_lora=dal,
                         rw_mu_main=dmu_main, rw_mu_lora=dmu_lora)
            grads["rw_w_main"] = _mm(f"rw_in_dw{i}", sv["h"], dpm, "tn", BF16)
            grads["rw_w_lorain"] = _mm(f"rw_inl_dw{i}", sv["h"], dpl, "tn", BF16)
            dh = _mm(f"rw_inl_dx{i}", dpl, lw["w_lorain"], "nt", F32)
            dh = _mm(f"rw_in_dx{i}", dpm, lw["w_main"], "nt", F32, add=dh)
        if kind != 0:
            notify(i, "in", grads)
        dgate_here = dgate
        if i > 0:
            dx, dy, dg, dshift, dscale, dgate = _norm_bwd(f"norm_bwd{i}", sv["x"], dh, dx, sv["g"], sv["shift"], sv["scale"],
                                                          below=(saved[i - 1]["y"], saved[i - 1]["gate"]))
        else:
            dx, dg, dshift, dscale = _norm_bwd(f"norm_bwd{i}", sv["x"], dh, dx, sv["g"], sv["shift"], sv["scale"])
        grads["norm_g"][i] = dg
        dmods[i] = jnp.concatenate([dshift, dscale, dgate_here], axis=1)
    return loss, dx, jnp.concatenate(dmods, axis=0), grads


def kernel(x, c, positions, norm_g, mod_w, mod_b, final_norm_g, sg_w_in, sg_w_out, sg_ln_g, sg_ln_b, sg_w_spatial, sg_b_spatial, swa_w_in, swa_w_out, swa_sinks, rwkv_w_in, rwkv_w_out, rwkv_mu, rwkv_w0, rwkv_w_lora, rwkv_a0, rwkv_a_lora, rwkv_k_k, rwkv_k_a, rwkv_r_k, rwkv_gn_g, rwkv_gn_b, loss_target, m_norm_g, m_mod_w, m_mod_b, m_final_norm_g, m_sg_w_in, m_sg_w_out, m_sg_ln_g, m_sg_ln_b, m_sg_w_spatial, m_sg_b_spatial, m_swa_w_in, m_swa_w_out, m_swa_sinks, m_rwkv_w_in, m_rwkv_w_out, m_rwkv_mu, m_rwkv_w0, m_rwkv_w_lora, m_rwkv_a0, m_rwkv_a_lora, m_rwkv_k_k, m_rwkv_k_a, m_rwkv_r_k, m_rwkv_gn_g, m_rwkv_gn_b, v_norm_g, v_mod_w, v_mod_b, v_final_norm_g, v_sg_w_in, v_sg_w_out, v_sg_ln_g, v_sg_ln_b, v_sg_w_spatial, v_sg_b_spatial, v_swa_w_in, v_swa_w_out, v_swa_sinks, v_rwkv_w_in, v_rwkv_w_out, v_rwkv_mu, v_rwkv_w0, v_rwkv_w_lora, v_rwkv_a0, v_rwkv_a_lora, v_rwkv_k_k, v_rwkv_k_a, v_rwkv_r_k, v_rwkv_gn_g, v_rwkv_gn_b):
    weights = dict(norm_g=norm_g, mod_w=mod_w, mod_b=mod_b, final_norm_g=final_norm_g, sg_w_in=sg_w_in, sg_w_out=sg_w_out,
                   sg_ln_g=sg_ln_g, sg_ln_b=sg_ln_b, sg_w_spatial=sg_w_spatial, sg_b_spatial=sg_b_spatial, swa_w_in=swa_w_in,
                   swa_w_out=swa_w_out, swa_sinks=swa_sinks, rwkv_w_in=rwkv_w_in, rwkv_w_out=rwkv_w_out, rwkv_mu=rwkv_mu,
                   rwkv_w0=rwkv_w0, rwkv_w_lora=rwkv_w_lora, rwkv_a0=rwkv_a0, rwkv_a_lora=rwkv_a_lora, rwkv_k_k=rwkv_k_k,
                   rwkv_k_a=rwkv_k_a, rwkv_r_k=rwkv_r_k, rwkv_gn_g=rwkv_gn_g, rwkv_gn_b=rwkv_gn_b)
    mom_m = dict(norm_g=m_norm_g, mod_w=m_mod_w, mod_b=m_mod_b, final_norm_g=m_final_norm_g, sg_w_in=m_sg_w_in,
                 sg_w_out=m_sg_w_out, sg_ln_g=m_sg_ln_g, sg_ln_b=m_sg_ln_b, sg_w_spatial=m_sg_w_spatial,
                 sg_b_spatial=m_sg_b_spatial, swa_w_in=m_swa_w_in, swa_w_out=m_swa_w_out, swa_sinks=m_swa_sinks,
                 rwkv_w_in=m_rwkv_w_in, rwkv_w_out=m_rwkv_w_out, rwkv_mu=m_rwkv_mu, rwkv_w0=m_rwkv_w0,
                 rwkv_w_lora=m_rwkv_w_lora, rwkv_a0=m_rwkv_a0, rwkv_a_lora=m_rwkv_a_lora, rwkv_k_k=m_rwkv_k_k,
                 rwkv_k_a=m_rwkv_k_a, rwkv_r_k=m_rwkv_r_k, rwkv_gn_g=m_rwkv_gn_g, rwkv_gn_b=m_rwkv_gn_b)
    mom_v = dict(norm_g=v_norm_g, mod_w=v_mod_w, mod_b=v_mod_b, final_norm_g=v_final_norm_g, sg_w_in=v_sg_w_in,
                 sg_w_out=v_sg_w_out, sg_ln_g=v_sg_ln_g, sg_ln_b=v_sg_ln_b, sg_w_spatial=v_sg_w_spatial,
                 sg_b_spatial=v_sg_b_spatial, swa_w_in=v_swa_w_in, swa_w_out=v_swa_w_out, swa_sinks=v_swa_sinks,
                 rwkv_w_in=v_rwkv_w_in, rwkv_w_out=v_rwkv_w_out, rwkv_mu=v_rwkv_mu, rwkv_w0=v_rwkv_w0,
                 rwkv_w_lora=v_rwkv_w_lora, rwkv_a0=v_rwkv_a0, rwkv_a_lora=v_rwkv_a_lora, rwkv_k_k=v_rwkv_k_k,
                 rwkv_k_a=v_rwkv_k_a, rwkv_r_k=v_rwkv_r_k, rwkv_gn_g=v_rwkv_gn_g, rwkv_gn_b=v_rwkv_gn_b)
    names = list(weights)
    t, d = x.shape[1], x.shape[2]
    me = 4 * lax.axis_index("x") + 2 * lax.axis_index("y") + lax.axis_index("c")
    n_mod = mod_w.shape[2]
    n_rw = rwkv_w_in.shape[2]

    small_names = ["sg_ln_g", "sg_ln_b", "rwkv_mu", "rwkv_w0", "rwkv_a0", "rwkv_k_k", "rwkv_k_a", "rwkv_gn_g", "rwkv_gn_b",
                   "rwkv_w_lora", "rwkv_a_lora"]
    small_shapes = [weights[n].shape for n in small_names]
    pk = _pack([c] + [weights[n] for n in small_names])
    gathered = _gather("gather_small", pk).reshape(N_DEV, -1)
    c_all = gathered[:, :d]
    per_dev = [_unpack(gathered[dv, d:], small_shapes) for dv in range(N_DEV)]
    full_small = {}
    for q, n in enumerate(small_names):
        full_small[n] = jnp.concatenate([per_dev[dv][q] for dv in range(N_DEV)], axis=-1)

    mod_b_cols = lax.dynamic_slice_in_dim(mod_b, me * n_mod, n_mod, axis=1).reshape(DEPTH, 1, n_mod)
    mod_part = _mod_fwd("mod_fwd", c_all, mod_w, mod_b_cols)
    mod_g = _gather("gather_mod", mod_part.reshape(DEPTH * N_DEV, n_mod))
    mod_g = mod_g.reshape(N_DEV, DEPTH, N_DEV, n_mod)
    mods = lax.dynamic_index_in_dim(mod_g, me, axis=2, keepdims=False)
    mods = mods.transpose(1, 0, 2).reshape(DEPTH, N_DEV * n_mod)

    job = lambda cls, src: dict(cls=cls, src=src)
    half = d // 2
    gj = dict(swa_in=job(_Gather, swa_w_in[0].astype(BF16)), swa_out=job(_Gather, swa_w_out[0].astype(BF16)),
              rw_in_a=job(_Gather, rwkv_w_in[0, :half].astype(BF16)), rw_in_b=job(_Gather, rwkv_w_in[0, half:].astype(BF16)),
              rw_out=job(_Gather, rwkv_w_out[0].astype(BF16)),
              sg_in1=job(_Gather, sg_w_in[1].astype(BF16)), sg_out1=job(_Gather, sg_w_out[1].astype(BF16)))
    hooks = {"sg_in0": [gj["swa_in"]], "out0": [gj["swa_out"]], "swa_in1": [gj["rw_in_a"]], "swa_attn1": [gj["rw_in_b"]],
             "out1": [gj["rw_out"]], "rw_scan2": [gj["sg_in1"], gj["sg_out1"]]}
    g_sg_in0 = _gather("gather_sg_in0", sg_w_in[0].astype(BF16))
    gj["sg_out0"] = job(_Gather, sg_w_out[0].astype(BF16))
    hooks["norm_fwd0"] = [gj["sg_out0"]]
    lora_rows = lambda w, off: jnp.zeros((LORA_PAD, d), F32).at[off:off + LORA].set(w)
    mu = full_small["rwkv_mu"].reshape(1, -1)
    heads = lambda a: a.reshape(1, -1)

    def layer_weights(i):
        if i % 3 == 0:
            j = i // 3
            g_in, g_out = (g_sg_in0, gj["sg_out0"]["out"]) if j == 0 else (gj["sg_in1"]["out"], gj["sg_out1"]["out"])
            return dict(w_in=_cols_full(g_in), w_out=g_out.reshape(d, d), ln_g=full_small["sg_ln_g"][j].reshape(1, d),
                        ln_b=full_small["sg_ln_b"][j].reshape(1, d), w_s=sg_w_spatial[j], b_st=sg_b_spatial[j].T)
        if i % 3 == 1:
            return dict(w_in=_cols_full(gj["swa_in"]["out"]), w_out=gj["swa_out"]["out"].reshape(d, d),
                        sinks=swa_sinks.reshape(SWA_KV, SWA_REP, 1, 1))
        rw_in_full = jnp.concatenate([_cols_full(gj["rw_in_a"]["out"]), _cols_full(gj["rw_in_b"]["out"])], axis=0)
        return dict(w_main=rw_in_full[:, :4 * d], w_lorain=jnp.pad(rw_in_full[:, 4 * d:], ((0, 0), (0, LORA_PAD - 2 * LORA))),
                    w_out=gj["rw_out"]["out"].reshape(d, d), mu_main=mu[:, :4 * d],
                    mu_lora=jnp.pad(mu[:, 4 * d:], ((0, 0), (0, LORA_PAD - 2 * LORA))),
                    w0=full_small["rwkv_w0"], a0=full_small["rwkv_a0"],
                    w_lora=lora_rows(full_small["rwkv_w_lora"][0], 0), a_lora=lora_rows(full_small["rwkv_a_lora"][0], LORA),
                    hp=[heads(full_small["rwkv_k_k"]), heads(full_small["rwkv_k_a"]), heads(rwkv_r_k),
                        heads(full_small["rwkv_gn_g"]), heads(full_small["rwkv_gn_b"])])

    sj = {}

    def on_grads(i, which, g):
        def stage(nm, parts, host):
            sj[nm] = job(_Chips, _scatter_pairs("scatter_" + nm, parts.astype(BF16)))
            hooks.setdefault(host, []).append(sj[nm])

        rows_of = lambda a: a.reshape(N_DEV, -1, d)
        if (i, which) == (3, "out"):
            stage("sg_out1", rows_of(g["sg_w_out"][1]), "rw_scan_bwd2")
        elif (i, which) == (3, "in"):
            stage("sg_in1", _cols_parts(g["sg_w_in"][1]), "rw_scan_bwd2")
            sj["sp1"] = job(_Gather, g["sg_w_s"][1].reshape(-1, 128))
            hooks["rw_scan_bwd2"].append(sj["sp1"])
        elif (i, which) == (2, "out"):
            stage("rw_out", rows_of(g["rw_w_out"]), "swa_in_dx1")
        elif (i, which) == (2, "in"):
            d_rw_in = jnp.concatenate([g["rw_w_main"], g["rw_w_lorain"][:, :2 * LORA]], axis=1)
            stage("rw_in_a", _cols_parts(d_rw_in[:half]), "swa_attn_bwd1")
            stage("rw_in_b", _cols_parts(d_rw_in[half:]), "swa_in_dw1")
        elif (i, which) == (1, "out"):
            stage("swa_out", rows_of(g["swa_w_out"]), "out_dx0")
        elif (i, which) == (1, "in"):
            stage("swa_in", _cols_parts(g["swa_w_in"]), "sg_mix_bwd0")
        elif (i, which) == (0, "out"):
            stage("sg_out0", rows_of(g["sg_w_out"][0]), "sg_in_dw0")
        else:
            stage("sg_in0", _cols_parts(g["sg_w_in"][0]), "sg_in_dx0")

    loss, dx, dmods, g = _local_step(x[0], positions.reshape(t, 1).astype(F32), loss_target[0], mods, norm_g, final_norm_g,
                                     layer_weights, hooks, on_grads)

    dmod_g = _gather("gather_dmod", dmods)
    dmod_all = dmod_g.transpose(1, 0, 2)
    dmod_cols = lax.dynamic_slice_in_dim(dmod_all, me * n_mod, n_mod, axis=2)
    g_mod_w, g_mod_b = _mod_bwd("mod_bwd", c_all, dmod_cols, dmod_all)

    d_b_sp = [g["sg_b_st"][j].T for j in range(2)]
    rep = [loss[0, :1], jnp.concatenate(g["norm_g"], axis=0), g["final_norm_g"], g["sg_w_s"][0], jnp.stack(d_b_sp),
           g["swa_sinks"], g["rw_hp"][2]]
    rep_shapes = [(1,), norm_g.shape, final_norm_g.shape, sg_w_spatial.shape[1:], sg_b_spatial.shape, swa_sinks.shape,
                  rwkv_r_k.shape]
    rep_sum = _sum_parts("sum_rep", _gather("gather_rep", _pack(rep, 128 * 256))).reshape(-1)
    loss_tot, g_norm_g, g_final, g_w_sp0, g_b_sp, g_sinks, g_r_k = _unpack(rep_sum, rep_shapes)
    g_w_sp = jnp.stack([g_w_sp0, _sum_parts("sum_sp1", sj["sp1"]["out"]).reshape(sg_w_spatial.shape[1:])])

    p_sg_in = jnp.concatenate([sj["sg_in0"]["out"], sj["sg_in1"]["out"]], axis=1)
    p_sg_out = jnp.concatenate([sj["sg_out0"]["out"], sj["sg_out1"]["out"]], axis=1)
    p_swa_in, p_swa_out, p_rw_out = (sj[nm]["out"] for nm in ("swa_in", "swa_out", "rw_out"))
    p_rw_in = jnp.concatenate([sj["rw_in_a"]["out"], sj["rw_in_b"]["out"]], axis=1)
    d_mu = jnp.concatenate([g["rw_mu_main"], g["rw_mu_lora"][:, :2 * LORA]], axis=1)
    hp_flat = lambda a: a.reshape(1, -1)
    small_grads = dict(sg_ln_g=jnp.concatenate(g["sg_ln_g"], axis=0), sg_ln_b=jnp.concatenate(g["sg_ln_b"], axis=0), rwkv_mu=d_mu,
                       rwkv_w0=g["rw_w0"], rwkv_a0=g["rw_a0"], rwkv_k_k=hp_flat(g["rw_hp"][0]), rwkv_k_a=hp_flat(g["rw_hp"][1]),
                       rwkv_gn_g=hp_flat(g["rw_hp"][3]), rwkv_gn_b=hp_flat(g["rw_hp"][4]),
                       rwkv_w_lora=g["rw_w_lora"][None, :LORA], rwkv_a_lora=g["rw_a_lora"][None, LORA:2 * LORA])
    per_dest = []
    for dv in range(N_DEV):
        shards = []
        for n in small_names:
            full, w = small_grads[n], weights[n].shape[-1]
            shards.append(full[..., dv * w:(dv + 1) * w])
        per_dest.append(_pack(shards))
    small_parts = _exchange("scatter_small", jnp.stack(per_dest), True)

    out_g, out_d, out_m, out_v = {}, {}, {}, {}

    def update(name, grad, shape2d, jobs=()):
        w2, m2, v2 = (a[name].reshape(shape2d) for a in (weights, mom_m, mom_v))
        gg, dd, mm, vv = _adamw("adamw_" + name, w2, grad, m2, v2, jobs=jobs)
        shp = weights[name].shape
        out_g[name], out_d[name], out_m[name], out_v[name] = gg.reshape(shp), dd.reshape(shp), mm.reshape(shp), vv.reshape(shp)

    update("mod_w", g_mod_w.reshape(-1, n_mod), (-1, n_mod))
    update("sg_w_in", p_sg_in, (-1, sg_w_in.shape[2]))
    update("sg_w_out", p_sg_out, (-1, d))
    update("swa_w_in", p_swa_in, (-1, swa_w_in.shape[2]))
    update("swa_w_out", p_swa_out, (-1, d))
    update("rwkv_w_in", p_rw_in, (-1, n_rw))
    update("rwkv_w_out", p_rw_out, (-1, d))
    update("sg_w_spatial", g_w_sp.reshape(-1, 128), (-1, 128))
    w_pk, m_pk, v_pk = (_pack([a[n] for n in small_names]) for a in (weights, mom_m, mom_v))
    res = _adamw("adamw_small", w_pk, small_parts, m_pk, v_pk)
    for q, arrs in enumerate(zip(*[_unpack(r_.reshape(-1), small_shapes) for r_ in res])):
        out_g[small_names[q]], out_d[small_names[q]], out_m[small_names[q]], out_v[small_names[q]] = arrs
    rep_names = ["norm_g", "mod_b", "final_norm_g", "sg_b_spatial", "swa_sinks", "rwkv_r_k"]
    rep_grads = [g_norm_g, g_mod_b.reshape(mod_b.shape), g_final, g_b_sp, g_sinks, g_r_k]
    rep_shapes2 = [weights[n].shape for n in rep_names]
    w_pk, m_pk, v_pk = (_pack([a[n] for n in rep_names]) for a in (weights, mom_m, mom_v))
    res = _adamw("adamw_rep", w_pk, _pack(rep_grads), m_pk, v_pk)
    for q, arrs in enumerate(zip(*[_unpack(r_.reshape(-1), rep_shapes2) for r_ in res])):
        out_g[rep_names[q]], out_d[rep_names[q]], out_m[rep_names[q]], out_v[rep_names[q]] = arrs

    return (loss_tot.reshape(()), dx[None], *[out_g[n] for n in names], *[out_d[n] for n in names],
            *[out_m[n] for n in names], *[out_v[n] for n in names])
```
